```python
import jax, jax.numpy as jnp
from jax import lax
import numpy as np

D_MODEL = 1024
BATCH = 16
SEQ = 4096
DEPTH = 2

CHUNK = 64
N_MIXERS = 2
RET_HEADS = 4
RET_DK = D_MODEL // RET_HEADS
RET_DV = 2 * RET_DK
RET_QK = RET_HEADS * RET_DK
RET_VW = RET_HEADS * RET_DV
ROPE_BASE = 10000.0
ATT_HEADS = 16
ATT_DH = D_MODEL // ATT_HEADS
PAST_CHUNKS = 8
BAND_PAST = PAST_CHUNKS * CHUNK
BAND = (PAST_CHUNKS + 1) * CHUNK
MAX_REL = 256
REL_TABLE = MAX_REL + CHUNK
D_FF = 4 * D_MODEL
EPS = 1e-6

kernel_name = "hybrid_retention_chunkattn_encoder"


def _rmsnorm(x, g):
    xf = x.astype(jnp.float32)
    y = xf * lax.rsqrt(jnp.mean(xf * xf, axis=-1, keepdims=True) + EPS)
    return (y * g.astype(jnp.float32)).astype(x.dtype)


def _rope(t, pos):
    d = t.shape[-1]
    half = d // 2
    inv = jnp.exp(-jnp.log(ROPE_BASE) * jnp.arange(half, dtype=jnp.float32) / half)
    ang = pos[:, None] * inv[None, :]
    cos = jnp.cos(ang)[None, :, None, :]
    sin = jnp.sin(ang)[None, :, None, :]
    tf = t.astype(jnp.float32)
    t1, t2 = tf[..., :half], tf[..., half:]
    return jnp.concatenate([t1 * cos - t2 * sin, t1 * sin + t2 * cos], axis=-1).astype(t.dtype)


def _retention(h, w_in, gn_g, w_out):
    B, S, _ = h.shape
    nc = S // CHUNK
    proj = h @ w_in
    q, k, v, g = jnp.split(proj, [RET_QK, 2 * RET_QK, 2 * RET_QK + RET_VW], axis=-1)
    q = q.reshape(B, S, RET_HEADS, RET_DK)
    k = k.reshape(B, S, RET_HEADS, RET_DK)
    v = v.reshape(B, S, RET_HEADS, RET_DV)
    pos = jnp.arange(S, dtype=jnp.float32)
    q = _rope(q, pos)
    k = _rope(k, pos) * (RET_DK ** -0.5)

    def to_chunks(t):
        return t.reshape(B, nc, CHUNK, RET_HEADS, t.shape[-1]).transpose(1, 0, 3, 2, 4)

    qc, kc, vc = to_chunks(q), to_chunks(k), to_chunks(v)
    dt = q.dtype
    log_gamma = jnp.log1p(-jnp.exp2(-5.0 - jnp.arange(RET_HEADS, dtype=jnp.float32)))
    idx = jnp.arange(CHUNK, dtype=jnp.float32)
    intra_decay = jnp.exp(log_gamma[:, None, None] * jnp.abs(idx[:, None] - idx[None, :])).astype(dt)
    key_decay = jnp.exp(log_gamma[:, None] * (CHUNK - 1 - idx)[None, :]).astype(dt)
    query_decay = jnp.exp(log_gamma[:, None] * (idx + 1.0)[None, :]).astype(dt)
    chunk_decay = jnp.exp(log_gamma * CHUNK).astype(dt)

    def step(state, inp):
        qb, kb, vb = inp
        scores = jnp.einsum('bhid,bhjd->bhij', qb, kb) * intra_decay[None]
        o_intra = jnp.einsum('bhij,bhje->bhie', scores, vb)
        o_cross = jnp.einsum('bhid,bhde->bhie', qb, state) * query_decay[None, :, :, None]
        new_state = (state * chunk_decay[None, :, None, None]
                     + jnp.einsum('bhjd,bhje->bhde', kb * key_decay[None, :, :, None], vb))
        return new_state, o_intra + o_cross

    state0 = jnp.zeros((B, RET_HEADS, RET_DK, RET_DV), dt)
    _, o = lax.scan(step, state0, (qc, kc, vc))
    o = o.transpose(1, 0, 3, 2, 4).reshape(B, S, RET_HEADS, RET_DV).astype(jnp.float32)
    mu = jnp.mean(o, axis=-1, keepdims=True)
    var = jnp.mean(jnp.square(o - mu), axis=-1, keepdims=True)
    o = ((o - mu) * lax.rsqrt(var + EPS)).reshape(B, S, RET_VW) * gn_g.astype(jnp.float32)
    y = (jax.nn.silu(g.astype(jnp.float32)) * o).astype(h.dtype)
    return y @ w_out


def _chunk_attention(h, w_in, rel_bias, w_out):
    B, S, _ = h.shape
    nc = S // CHUNK
    q, k, v = jnp.split(h @ w_in, 3, axis=-1)
    q = q.reshape(B, S, ATT_HEADS, ATT_DH)
    k = k.reshape(B, S, ATT_HEADS, ATT_DH)
    v = v.reshape(B, S, ATT_HEADS, ATT_DH)
    pad = ((0, 0), (BAND_PAST, 0), (0, 0), (0, 0))
    kp = jnp.pad(k, pad)
    vp = jnp.pad(v, pad)
    qi = jnp.arange(CHUNK)[:, None]
    kj = jnp.arange(BAND)[None, :]
    rel = kj - BAND_PAST - qi
    bidx = jnp.maximum(rel, -MAX_REL) + MAX_REL
    bias = rel_bias[:, bidx].astype(jnp.float32)
    scale = ATT_DH ** -0.5

    def one_chunk(c):
        start = c * CHUNK
        qb = lax.dynamic_slice_in_dim(q, start, CHUNK, axis=1)
        kb = lax.dynamic_slice_in_dim(kp, start, BAND, axis=1)
        vb = lax.dynamic_slice_in_dim(vp, start, BAND, axis=1)
        s = jnp.einsum('bqhd,bkhd->bhqk', qb, kb).astype(jnp.float32) * scale + bias[None]
        valid = (start - BAND_PAST + jnp.arange(BAND)) >= 0
        s = jnp.where(valid[None, None, None, :], s, -jnp.inf)
        p = jax.nn.softmax(s, axis=-1).astype(vb.dtype)
        return jnp.einsum('bhqk,bkhd->bqhd', p, vb)

    o = lax.map(one_chunk, jnp.arange(nc))
    o = o.transpose(1, 0, 2, 3, 4).reshape(B, S, ATT_HEADS * ATT_DH)
    return o @ w_out


def _sqrelu_mlp(h, w1, w2):
    u = jax.nn.relu(h @ w1)
    return (u * u) @ w2


def _fwd_setup_inputs(seed: int = 0) -> dict:
    key = jax.random.key(seed)
    ks = jax.random.split(key, 12)
    n_ret = (DEPTH + 1) // 2
    n_att = DEPTH // 2
    f = jnp.float32
    x = jax.random.normal(ks[0], (BATCH, SEQ, D_MODEL), f)
    mix_norm_g = 1.0 + 0.02 * jax.random.normal(ks[1], (DEPTH, D_MODEL), f)
    ret_w_in = jax.random.normal(ks[2], (n_ret, D_MODEL, 2 * RET_QK + 2 * RET_VW), f) * D_MODEL ** -0.5
    ret_gn_g = 1.0 + 0.02 * jax.random.normal(ks[3], (n_ret, RET_VW), f)
    ret_w_out = jax.random.normal(ks[4], (n_ret, RET_VW, D_MODEL), f) * RET_VW ** -0.5
    att_w_in = jax.random.normal(ks[5], (n_att, D_MODEL, 3 * D_MODEL), f) * D_MODEL ** -0.5
    att_rel_bias = 0.1 * jax.random.normal(ks[6], (n_att, ATT_HEADS, REL_TABLE), f)
    att_w_out = jax.random.normal(ks[7], (n_att, D_MODEL, D_MODEL), f) * D_MODEL ** -0.5
    mlp_norm_g = 1.0 + 0.02 * jax.random.normal(ks[8], (DEPTH, D_MODEL), f)
    mlp_w1 = jax.random.normal(ks[9], (DEPTH, D_MODEL, D_FF), f) * D_MODEL ** -0.5
    mlp_w2 = jax.random.normal(ks[10], (DEPTH, D_FF, D_MODEL), f) * D_FF ** -0.5
    final_norm_g = 1.0 + 0.02 * jax.random.normal(ks[11], (D_MODEL,), f)
    return {"x": x, "mix_norm_g": mix_norm_g, "ret_w_in": ret_w_in, "ret_gn_g": ret_gn_g,
            "ret_w_out": ret_w_out, "att_w_in": att_w_in, "att_rel_bias": att_rel_bias,
            "att_w_out": att_w_out, "mlp_norm_g": mlp_norm_g, "mlp_w1": mlp_w1,
            "mlp_w2": mlp_w2, "final_norm_g": final_norm_g}


def _fwd_reference(x, mix_norm_g, ret_w_in, ret_gn_g, ret_w_out, att_w_in, att_rel_bias,
              att_w_out, mlp_norm_g, mlp_w1, mlp_w2, final_norm_g):
    h = x
    for i in range(DEPTH):
        hn = _rmsnorm(h, mix_norm_g[i])
        j = i // N_MIXERS
        if i % N_MIXERS == 0:
            h = h + _retention(hn, ret_w_in[j], ret_gn_g[j], ret_w_out[j])
        else:
            h = h + _chunk_attention(hn, att_w_in[j], att_rel_bias[j], att_w_out[j])
        h = h + _sqrelu_mlp(_rmsnorm(h, mlp_norm_g[i]), mlp_w1[i], mlp_w2[i])
    return _rmsnorm(h, final_norm_g)


import jax as _jax
import jax.numpy as _jnp

TWIN_FORMAT = 'train_step'
FWD_PARAMS = ['x', 'mix_norm_g', 'ret_w_in', 'ret_gn_g', 'ret_w_out', 'att_w_in', 'att_rel_bias', 'att_w_out', 'mlp_norm_g', 'mlp_w1', 'mlp_w2', 'final_norm_g']
TWIN_WEIGHTS = ['mix_norm_g', 'ret_w_in', 'ret_gn_g', 'ret_w_out', 'att_w_in', 'att_rel_bias', 'att_w_out', 'mlp_norm_g', 'mlp_w1', 'mlp_w2', 'final_norm_g']
TWIN_DIFF_INPUT = 'x'
TWIN_INPUTS = ['x', 'mix_norm_g', 'ret_w_in', 'ret_gn_g', 'ret_w_out', 'att_w_in', 'att_rel_bias', 'att_w_out', 'mlp_norm_g', 'mlp_w1', 'mlp_w2', 'final_norm_g', 'loss_target', 'm_mix_norm_g', 'm_ret_w_in', 'm_ret_gn_g', 'm_ret_w_out', 'm_att_w_in', 'm_att_rel_bias', 'm_att_w_out', 'm_mlp_norm_g', 'm_mlp_w1', 'm_mlp_w2', 'm_final_norm_g', 'v_mix_norm_g', 'v_ret_w_in', 'v_ret_gn_g', 'v_ret_w_out', 'v_att_w_in', 'v_att_rel_bias', 'v_att_w_out', 'v_mlp_norm_g', 'v_mlp_w1', 'v_mlp_w2', 'v_final_norm_g']
TWIN_OUTPUTS = ['loss', 'grad_x', 'grad_mix_norm_g', 'grad_ret_w_in', 'grad_ret_gn_g', 'grad_ret_w_out', 'grad_att_w_in', 'grad_att_rel_bias', 'grad_att_w_out', 'grad_mlp_norm_g', 'grad_mlp_w1', 'grad_mlp_w2', 'grad_final_norm_g', 'delta_mix_norm_g', 'delta_ret_w_in', 'delta_ret_gn_g', 'delta_ret_w_out', 'delta_att_w_in', 'delta_att_rel_bias', 'delta_att_w_out', 'delta_mlp_norm_g', 'delta_mlp_w1', 'delta_mlp_w2', 'delta_final_norm_g', 'new_m_mix_norm_g', 'new_m_ret_w_in', 'new_m_ret_gn_g', 'new_m_ret_w_out', 'new_m_att_w_in', 'new_m_att_rel_bias', 'new_m_att_w_out', 'new_m_mlp_norm_g', 'new_m_mlp_w1', 'new_m_mlp_w2', 'new_m_final_norm_g', 'new_v_mix_norm_g', 'new_v_ret_w_in', 'new_v_ret_gn_g', 'new_v_ret_w_out', 'new_v_att_w_in', 'new_v_att_rel_bias', 'new_v_att_w_out', 'new_v_mlp_norm_g', 'new_v_mlp_w1', 'new_v_mlp_w2', 'new_v_final_norm_g']
TWIN_LEAF_KINDS = {'loss': 'loss', 'grad_x': 'grad_x', 'grad_mix_norm_g': 'grad_w', 'grad_ret_w_in': 'grad_w', 'grad_ret_gn_g': 'grad_w', 'grad_ret_w_out': 'grad_w', 'grad_att_w_in': 'grad_w', 'grad_att_rel_bias': 'grad_w', 'grad_att_w_out': 'grad_w', 'grad_mlp_norm_g': 'grad_w', 'grad_mlp_w1': 'grad_w', 'grad_mlp_w2': 'grad_w', 'grad_final_norm_g': 'grad_w', 'delta_mix_norm_g': 'delta_w', 'delta_ret_w_in': 'delta_w', 'delta_ret_gn_g': 'delta_w', 'delta_ret_w_out': 'delta_w', 'delta_att_w_in': 'delta_w', 'delta_att_rel_bias': 'delta_w', 'delta_att_w_out': 'delta_w', 'delta_mlp_norm_g': 'delta_w', 'delta_mlp_w1': 'delta_w', 'delta_mlp_w2': 'delta_w', 'delta_final_norm_g': 'delta_w', 'new_m_mix_norm_g': 'new_m', 'new_m_ret_w_in': 'new_m', 'new_m_ret_gn_g': 'new_m', 'new_m_ret_w_out': 'new_m', 'new_m_att_w_in': 'new_m', 'new_m_att_rel_bias': 'new_m', 'new_m_att_w_out': 'new_m', 'new_m_mlp_norm_g': 'new_m', 'new_m_mlp_w1': 'new_m', 'new_m_mlp_w2': 'new_m', 'new_m_final_norm_g': 'new_m', 'new_v_mix_norm_g': 'new_v', 'new_v_ret_w_in': 'new_v', 'new_v_ret_gn_g': 'new_v', 'new_v_ret_w_out': 'new_v', 'new_v_att_w_in': 'new_v', 'new_v_att_rel_bias': 'new_v', 'new_v_att_w_out': 'new_v', 'new_v_mlp_norm_g': 'new_v', 'new_v_mlp_w1': 'new_v', 'new_v_mlp_w2': 'new_v', 'new_v_final_norm_g': 'new_v'}


def _forward(args):
    return _fwd_reference(*[args[k] for k in FWD_PARAMS])


def _output_shape():
    out = _jax.eval_shape(lambda: _forward(_fwd_setup_inputs(0)))
    return out.shape, out.dtype

N_MICROBATCH = 1
ADAM_LR = 0.001
ADAM_B1 = 0.9
ADAM_B2 = 0.999
ADAM_EPS = 1e-08
ADAM_WD = 0.01
ADAM_STEP = 10
PER_EXAMPLE_BATCH_AXIS = {'x': 0, 'loss_target': 0}
SHARED_INPUTS = []
_WEIGHT_DTYPES = {'mix_norm_g': _jnp.float32, 'ret_w_in': _jnp.float32, 'ret_gn_g': _jnp.float32, 'ret_w_out': _jnp.float32, 'att_w_in': _jnp.float32, 'att_rel_bias': _jnp.float32, 'att_w_out': _jnp.float32, 'mlp_norm_g': _jnp.float32, 'mlp_w1': _jnp.float32, 'mlp_w2': _jnp.float32, 'final_norm_g': _jnp.float32}
MOMENT_SCALE = {'mix_norm_g': 2.181795e-01, 'ret_w_in': 1.201234e-01, 'ret_gn_g': 1.037744e-01, 'ret_w_out': 1.455454e-01, 'att_w_in': 4.221621e-02, 'att_rel_bias': 8.692947e-03, 'att_w_out': 7.750998e-02, 'mlp_norm_g': 2.084968e-01, 'mlp_w1': 9.993493e-02, 'mlp_w2': 2.074020e-01, 'final_norm_g': 6.540420e+01}


def _to_microbatches(a, axis):
    t = _jnp.moveaxis(a, axis, 0)
    t = t.reshape((N_MICROBATCH, t.shape[0] // N_MICROBATCH) + t.shape[1:])
    return _jnp.moveaxis(t, 1, axis + 1)


def setup_inputs(seed: int = 0) -> dict:
    inp = _fwd_setup_inputs(seed)
    key = _jax.random.fold_in(_jax.random.key(seed), 7919)
    shape, _ = _output_shape()
    out = dict(inp)
    out["loss_target"] = _jax.random.normal(_jax.random.fold_in(key, 0), shape, _jnp.float32)
    for i, name in enumerate(TWIN_WEIGHTS):
        w = inp[name].astype(_jnp.float32)
        if MOMENT_SCALE is None:
            s = _jnp.sqrt(_jnp.mean(_jnp.square(w)) + 1e-30)
        else:
            s = MOMENT_SCALE[name]
        km, kv = _jax.random.split(_jax.random.fold_in(key, i + 1))
        out[name] = w
        out["m_" + name] = s * _jax.random.normal(km, w.shape, _jnp.float32)
        out["v_" + name] = (s * s) * _jax.random.uniform(kv, w.shape, _jnp.float32, 0.5, 1.5)
    if N_MICROBATCH > 1:
        for name, axis in PER_EXAMPLE_BATCH_AXIS.items():
            out[name] = _to_microbatches(out[name], axis)
    return {'x': out['x'], 'mix_norm_g': out['mix_norm_g'], 'ret_w_in': out['ret_w_in'], 'ret_gn_g': out['ret_gn_g'], 'ret_w_out': out['ret_w_out'], 'att_w_in': out['att_w_in'], 'att_rel_bias': out['att_rel_bias'], 'att_w_out': out['att_w_out'], 'mlp_norm_g': out['mlp_norm_g'], 'mlp_w1': out['mlp_w1'], 'mlp_w2': out['mlp_w2'], 'final_norm_g': out['final_norm_g'], 'loss_target': out['loss_target'], 'm_mix_norm_g': out['m_mix_norm_g'], 'm_ret_w_in': out['m_ret_w_in'], 'm_ret_gn_g': out['m_ret_gn_g'], 'm_ret_w_out': out['m_ret_w_out'], 'm_att_w_in': out['m_att_w_in'], 'm_att_rel_bias': out['m_att_rel_bias'], 'm_att_w_out': out['m_att_w_out'], 'm_mlp_norm_g': out['m_mlp_norm_g'], 'm_mlp_w1': out['m_mlp_w1'], 'm_mlp_w2': out['m_mlp_w2'], 'm_final_norm_g': out['m_final_norm_g'], 'v_mix_norm_g': out['v_mix_norm_g'], 'v_ret_w_in': out['v_ret_w_in'], 'v_ret_gn_g': out['v_ret_gn_g'], 'v_ret_w_out': out['v_ret_w_out'], 'v_att_w_in': out['v_att_w_in'], 'v_att_rel_bias': out['v_att_rel_bias'], 'v_att_w_out': out['v_att_w_out'], 'v_mlp_norm_g': out['v_mlp_norm_g'], 'v_mlp_w1': out['v_mlp_w1'], 'v_mlp_w2': out['v_mlp_w2'], 'v_final_norm_g': out['v_final_norm_g']}


def _loss(weights, diff, rest, loss_target):
    with _jax.named_scope("forward"):
        args = {**rest, TWIN_DIFF_INPUT: diff, **{k: w.astype(_WEIGHT_DTYPES[k]) for k, w in weights.items()}}
        y = _forward(args)
    with _jax.named_scope("loss_head"):
        err = _jnp.square(y.astype(_jnp.float32) - loss_target)
        return 0.5 * _jnp.sum(_jnp.mean(err, axis=-1)) if err.ndim else 0.5 * err


def _adamw(w, g, m, v):
    m = ADAM_B1 * m + (1.0 - ADAM_B1) * g
    v = ADAM_B2 * v + (1.0 - ADAM_B2) * _jnp.square(g)
    m_hat = m / (1.0 - ADAM_B1 ** ADAM_STEP)
    v_hat = v / (1.0 - ADAM_B2 ** ADAM_STEP)
    delta = -ADAM_LR * (m_hat / (_jnp.sqrt(v_hat) + ADAM_EPS) + ADAM_WD * w)
    return delta, m, v


def reference(x, mix_norm_g, ret_w_in, ret_gn_g, ret_w_out, att_w_in, att_rel_bias, att_w_out, mlp_norm_g, mlp_w1, mlp_w2, final_norm_g, loss_target, m_mix_norm_g, m_ret_w_in, m_ret_gn_g, m_ret_w_out, m_att_w_in, m_att_rel_bias, m_att_w_out, m_mlp_norm_g, m_mlp_w1, m_mlp_w2, m_final_norm_g, v_mix_norm_g, v_ret_w_in, v_ret_gn_g, v_ret_w_out, v_att_w_in, v_att_rel_bias, v_att_w_out, v_mlp_norm_g, v_mlp_w1, v_mlp_w2, v_final_norm_g):
    given = dict(x=x, mix_norm_g=mix_norm_g, ret_w_in=ret_w_in, ret_gn_g=ret_gn_g, ret_w_out=ret_w_out, att_w_in=att_w_in, att_rel_bias=att_rel_bias, att_w_out=att_w_out, mlp_norm_g=mlp_norm_g, mlp_w1=mlp_w1, mlp_w2=mlp_w2, final_norm_g=final_norm_g, loss_target=loss_target, m_mix_norm_g=m_mix_norm_g, m_ret_w_in=m_ret_w_in, m_ret_gn_g=m_ret_gn_g, m_ret_w_out=m_ret_w_out, m_att_w_in=m_att_w_in, m_att_rel_bias=m_att_rel_bias, m_att_w_out=m_att_w_out, m_mlp_norm_g=m_mlp_norm_g, m_mlp_w1=m_mlp_w1, m_mlp_w2=m_mlp_w2, m_final_norm_g=m_final_norm_g, v_mix_norm_g=v_mix_norm_g, v_ret_w_in=v_ret_w_in, v_ret_gn_g=v_ret_gn_g, v_ret_w_out=v_ret_w_out, v_att_w_in=v_att_w_in, v_att_rel_bias=v_att_rel_bias, v_att_w_out=v_att_w_out, v_mlp_norm_g=v_mlp_norm_g, v_mlp_w1=v_mlp_w1, v_mlp_w2=v_mlp_w2, v_final_norm_g=v_final_norm_g)
    weights = {n: given[n] for n in TWIN_WEIGHTS}
    shared = {n: given[n] for n in SHARED_INPUTS}
    per_example = {n: given[n] for n in ['x']}
    grad_fn = _jax.value_and_grad(_loss, argnums=(0, 1))

    def one_microbatch(ex, loss_target):
        ex = dict(ex)
        diff = ex.pop(TWIN_DIFF_INPUT)
        return grad_fn(weights, diff, {**shared, **ex}, loss_target)

    if N_MICROBATCH == 1:
        loss, (grad_w, grad_x) = one_microbatch(per_example, given["loss_target"])
    else:
        def body(carry, xs):
            loss_sum, grad_sum = carry
            l_k, (gw_k, gx_k) = one_microbatch(xs[0], xs[1])
            with _jax.named_scope("update"):
                return (loss_sum + l_k, _jax.tree.map(_jnp.add, grad_sum, gw_k)), gx_k

        init = (_jnp.zeros((), _jnp.float32), _jax.tree.map(_jnp.zeros_like, weights))
        (loss, grad_w), grad_x = _jax.lax.scan(body, init, (per_example, given["loss_target"]))
    with _jax.named_scope("update"):
        delta_w, new_m, new_v = {}, {}, {}
        for n in TWIN_WEIGHTS:
            delta_w[n], new_m[n], new_v[n] = _adamw(weights[n], grad_w[n], given["m_" + n], given["v_" + n])
    return (loss, grad_x, *[grad_w[n] for n in TWIN_WEIGHTS], *[delta_w[n] for n in TWIN_WEIGHTS],
            *[new_m[n] for n in TWIN_WEIGHTS], *[new_v[n] for n in TWIN_WEIGHTS])
```

```python
import jax
import jax.numpy as jnp
from jax import lax
from jax.experimental import pallas as pl
from jax.experimental.pallas import tpu as pltpu

F32 = jnp.float32
BF16 = jnp.bfloat16
MESH = pl.DeviceIdType.MESH

D_MODEL = 1024
CHUNK = 64
RET_HEADS = 4
RET_DK = 256
RET_DV = 512
ROPE_BASE = 10000.0
ATT_HEADS = 16
ATT_DH = 64
PAST = 512
MAX_REL = 256
REL_TABLE = MAX_REL + CHUNK
EPS = 1e-6
NEG = -1e30
N_CHIPS = 4

ADAM_LR = 0.001
ADAM_B1 = 0.9
ADAM_B2 = 0.999
ADAM_EPS = 1e-08
ADAM_WD = 0.01
ADAM_STEP = 10

RET_BLOCK = 256
ATT_BLOCK = 256
VMEM_LIMIT = 56 * 1024 * 1024


def _params(n_axes, **kw):
    return pltpu.CompilerParams(dimension_semantics=("arbitrary",) * n_axes, vmem_limit_bytes=VMEM_LIMIT, **kw)


def _dot(a, b):
    return jnp.dot(a, b, preferred_element_type=F32)


def _dot_nt(a, b):
    return lax.dot_general(a, b, (((1,), (1,)), ((), ())), preferred_element_type=F32)


def _dot_tn(a, b):
    return lax.dot_general(a, b, (((0,), (0,)), ((), ())), preferred_element_type=F32)


def _sigmoid(x):
    return 1.0 / (1.0 + jnp.exp(-x))


def _w_spec_nn(w, wkind, tk, tn):
    if wkind == "col":
        per = w.shape[2] // tn
        assert w.shape[2] % tn == 0 and w.shape[1] % tk == 0
        return 4 * w.shape[2], pl.BlockSpec((None, tk, tn), lambda i, j, k: (j // per, k, j % per))
    per = w.shape[1] // tk
    assert w.shape[1] % tk == 0 and w.shape[2] % tn == 0
    return w.shape[2], pl.BlockSpec((None, tk, tn), lambda i, j, k: (k // per, k % per, j))


def _mm_nn(a, w, wkind, *, tm, tn, tk, out_dtype, name, norm_g=None, act=None, res=None):
    M, K = a.shape
    N, w_spec = _w_spec_nn(w, wkind, tk, tn)
    nk = K // tk
    has_norm = norm_g is not None
    assert M % tm == 0 and N % tn == 0 and K % tk == 0
    assert not has_norm or nk == 1

    def body(*refs):
        it = iter(refs)
        a_ref, w_ref = next(it), next(it)
        g_ref = next(it) if has_norm else None
        r_ref = next(it) if res is not None else None
        o_ref = next(it)
        hn_ref = next(it) if has_norm else None
        acc_ref = next(it) if nk > 1 else None
        hn_s = next(it) if has_norm else None
        j, k = pl.program_id(1), pl.program_id(2)
        if has_norm:
            @pl.when(j == 0)
            def _():
                x = a_ref[...].astype(F32)
                r = lax.rsqrt(jnp.mean(x * x, axis=-1, keepdims=True) + EPS)
                y = (x * r * g_ref[...]).astype(BF16)
                hn_s[...] = y
                hn_ref[...] = y
            lhs = hn_s[...]
        else:
            lhs = a_ref[...].astype(BF16)
        part = _dot(lhs, w_ref[...])

        def finish(acc):
            if act == "relu2":
                u = jnp.maximum(acc, 0.0)
                acc = u * u
            if r_ref is not None:
                acc = acc + r_ref[...]
            o_ref[...] = acc.astype(out_dtype)

        if nk == 1:
            finish(part)
        else:
            @pl.when(k == 0)
            def _():
                acc_ref[...] = part

            @pl.when(k > 0)
            def _():
                acc_ref[...] += part

            @pl.when(k == nk - 1)
            def _():
                finish(acc_ref[...])

    in_specs = [pl.BlockSpec((tm, tk), lambda i, j, k: (i, k)), w_spec]
    args = [a, w]
    if has_norm:
        in_specs.append(pl.BlockSpec((1, K), lambda i, j, k: (0, 0)))
        args.append(norm_g.reshape(1, K))
    if res is not None:
        in_specs.append(pl.BlockSpec((tm, tn), lambda i, j, k: (i, j)))
        args.append(res)
    out_shape = [jax.ShapeDtypeStruct((M, N), out_dtype)]
    out_specs = [pl.BlockSpec((tm, tn), lambda i, j, k: (i, j))]
    scratch = []
    if has_norm:
        out_shape.append(jax.ShapeDtypeStruct((M, K), BF16))
        out_specs.append(pl.BlockSpec((tm, K), lambda i, j, k: (i, 0)))
    if nk > 1:
        scratch.append(pltpu.VMEM((tm, tn), F32))
    if has_norm:
        scratch.append(pltpu.VMEM((tm, K), BF16))
    outs = pl.pallas_call(
        body, name=name, grid=(M // tm, N // tn, nk), in_specs=in_specs, out_specs=out_specs, out_shape=out_shape,
        scratch_shapes=scratch, compiler_params=_params(3),
    )(*args)
    return outs if has_norm else outs[0]


def _mm_nt(a, w, wkind, *, tm, tn, tk, name, epi, a2=None, h=None, g=None, dres=None):
    M, Nw = a.shape
    if wkind == "col":
        Kw, per = w.shape[1], w.shape[2] // tk
        assert w.shape[2] % tk == 0 and Nw == 4 * w.shape[2]
        w_spec = pl.BlockSpec((None, tn, tk), lambda i, j, k: (k // per, j, k % per))
    else:
        Kw, per = 4 * w.shape[1], w.shape[1] // tn
        assert w.shape[1] % tn == 0 and Nw == w.shape[2]
        w_spec = pl.BlockSpec((None, tn, tk), lambda i, j, k: (j // per, j % per, k))
    nk = Nw // tk
    assert M % tm == 0 and Kw % tn == 0 and Nw % tk == 0
    if epi == "normbwd":
        assert tn == Kw

    def body(*refs):
        it = iter(refs)
        a_ref, w_ref = next(it), next(it)
        a2_ref = next(it) if epi == "sqrt2" else None
        if epi == "normbwd":
            h_ref, g_ref, dres_ref = next(it), next(it), next(it)
        o_ref = next(it)
        dg_ref = next(it) if epi == "normbwd" else None
        acc_ref = next(it) if nk > 1 else None
        i, k = pl.program_id(0), pl.program_id(2)
        part = _dot_nt(a_ref[...].astype(BF16), w_ref[...])

        def finish(acc):
            if epi == "bf16":
                o_ref[...] = acc.astype(BF16)
            elif epi == "sqrt2":
                o_ref[...] = (acc * (2.0 * jnp.sqrt(a2_ref[...].astype(F32)))).astype(BF16)
            else:
                x = h_ref[...]
                r = lax.rsqrt(jnp.mean(x * x, axis=-1, keepdims=True) + EPS)
                xh = x * r
                dg_part = jnp.sum(acc * xh, axis=0, keepdims=True)

                @pl.when(i == 0)
                def _():
                    dg_ref[...] = dg_part

                @pl.when(i > 0)
                def _():
                    dg_ref[...] += dg_part
                t = acc * g_ref[...]
                dx = r * (t - xh * jnp.mean(t * xh, axis=-1, keepdims=True))
                o_ref[...] = dres_ref[...] + dx

        if nk == 1:
            finish(part)
        else:
            @pl.when(k == 0)
            def _():
                acc_ref[...] = part

            @pl.when(k > 0)
            def _():
                acc_ref[...] += part

            @pl.when(k == nk - 1)
            def _():
                finish(acc_ref[...])

    in_specs = [pl.BlockSpec((tm, tk), lambda i, j, k: (i, k)), w_spec]
    args = [a, w]
    out_dtype = BF16
    if epi == "sqrt2":
        in_specs.append(pl.BlockSpec((tm, tn), lambda i, j, k: (i, j)))
        args.append(a2)
    if epi == "normbwd":
        in_specs += [pl.BlockSpec((tm, Kw), lambda i, j, k: (i, 0)), pl.BlockSpec((1, Kw), lambda i, j, k: (0, 0)),
                     pl.BlockSpec((tm, Kw), lambda i, j, k: (i, 0))]
        args += [h, g.reshape(1, Kw), dres]
        out_dtype = F32
    out_shape = [jax.ShapeDtypeStruct((M, Kw), out_dtype)]
    out_specs = [pl.BlockSpec((tm, tn), lambda i, j, k: (i, j))]
    if epi == "normbwd":
        out_shape.append(jax.ShapeDtypeStruct((1, Kw), F32))
        out_specs.append(pl.BlockSpec((1, Kw), lambda i, j, k: (0, 0)))
    scratch = [pltpu.VMEM((tm, tn), F32)] if nk > 1 else []
    outs = pl.pallas_call(
        body, name=name, grid=(M // tm, Kw // tn, nk), in_specs=in_specs, out_specs=out_specs, out_shape=out_shape,
        scratch_shapes=scratch, compiler_params=_params(3),
    )(*args)
    return outs if epi == "normbwd" else outs[0]


def _mm_tn(a, b, okind, *, tt, tk, tn, name):
    T, K = a.shape
    N = b.shape[1]
    assert T % tt == 0 and K % tk == 0 and N % tn == 0
    nt = T // tt
    if okind == "col":
        per = (N // N_CHIPS) // tn
        assert (N // N_CHIPS) % tn == 0
        out_shape = jax.ShapeDtypeStruct((N_CHIPS, K, N // N_CHIPS), F32)
        out_spec = pl.BlockSpec((None, tk, tn), lambda ki, nj, t: (nj // per, ki, nj % per))
    else:
        per = (K // N_CHIPS) // tk
        assert (K // N_CHIPS) % tk == 0
        out_shape = jax.ShapeDtypeStruct((N_CHIPS, K // N_CHIPS, N), F32)
        out_spec = pl.BlockSpec((None, tk, tn), lambda ki, nj, t: (ki // per, ki % per, nj))

    def body(a_ref, b_ref, o_ref):
        t = pl.program_id(2)
        part = _dot_tn(a_ref[...].astype(BF16), b_ref[...].astype(BF16))

        @pl.when(t == 0)
        def _():
            o_ref[...] = part

        @pl.when(t > 0)
        def _():
            o_ref[...] += part

    return pl.pallas_call(
        body, name=name, grid=(K // tk, N // tn, nt),
        in_specs=[pl.BlockSpec((tt, tk), lambda ki, nj, t: (t, ki)), pl.BlockSpec((tt, tn), lambda ki, nj, t: (t, nj))],
        out_specs=out_spec, out_shape=out_shape, compiler_params=_params(3),
    )(a, b)


def _ret_consts(S, LB):
    log_gamma = jnp.log1p(-jnp.exp2(-5.0 - jnp.arange(RET_HEADS, dtype=F32)))
    idx = jnp.arange(LB, dtype=F32)
    n, m = idx[:, None], idx[None, :]
    cn, cm = jnp.floor(n / CHUNK), jnp.floor(m / CHUNK)
    dist = jnp.where(cm == cn, jnp.abs(n - m), n - m)
    dmat = jnp.where((cm <= cn)[None], jnp.exp(log_gamma[:, None, None] * dist[None]), 0.0)
    qd = jnp.exp(log_gamma[:, None] * (idx + 1.0)[None, :])[..., None]
    kd = jnp.exp(log_gamma[:, None] * (LB - 1 - idx)[None, :])[..., None]
    bd = jnp.exp(log_gamma * LB).reshape(RET_HEADS, 1, 1) * jnp.ones((RET_HEADS, 1, 128), F32)
    half = RET_DK // 2
    inv = jnp.exp(-jnp.log(ROPE_BASE) * jnp.arange(half, dtype=F32) / half)
    ang = jnp.arange(S, dtype=F32)[:, None] * inv[None, :]
    return dmat.astype(F32), qd.astype(F32), kd.astype(F32), bd, jnp.cos(ang), jnp.sin(ang)


def _rope(t, c, s):
    t1, t2 = t[:, :128], t[:, 128:]
    return jnp.concatenate([t1 * c - t2 * s, t1 * s + t2 * c], axis=-1)


def _rope_inv(d, c, s):
    d1, d2 = d[:, :128], d[:, 128:]
    return jnp.concatenate([d1 * c + d2 * s, d2 * c - d1 * s], axis=-1)


def _ret_block_fwd(p_ref, h, c, s, d_ref, qd_ref, kd_ref, stb):
    q = _rope(p_ref[:, h * RET_DK:(h + 1) * RET_DK].astype(F32), c, s)
    k = _rope(p_ref[:, 1024 + h * RET_DK:1024 + (h + 1) * RET_DK].astype(F32), c, s) * (RET_DK ** -0.5)
    v = p_ref[:, 2048 + h * RET_DV:2048 + (h + 1) * RET_DV]
    qb, kb = q.astype(BF16), k.astype(BF16)
    scb = (_dot_nt(qb, kb) * d_ref[h]).astype(BF16)
    o = _dot(scb, v) + qd_ref[h] * _dot(qb, stb)
    return q, k, qb, kb, v, scb, o


def _ret_fwd(proj, gn_g, consts, *, B, S):
    LB = RET_BLOCK
    nb = S // LB
    T = B * S
    dmat, qd, kd, bd, cos, sin = consts

    def body(p_ref, cos_ref, sin_ref, d_ref, qd_ref, kd_ref, bd_ref, gng_ref, y_ref, st_ref, state_s):
        i = pl.program_id(1)

        @pl.when(i == 0)
        def _():
            state_s[...] = jnp.zeros_like(state_s)
        c, s = cos_ref[...], sin_ref[...]
        for h in range(RET_HEADS):
            st = state_s[h]
            stb = st.astype(BF16)
            st_ref[h] = stb
            q, k, qb, kb, v, scb, o = _ret_block_fwd(p_ref, h, c, s, d_ref, qd_ref, kd_ref, stb)
            kdk = (k * kd_ref[h]).astype(BF16)
            state_s[h] = st * bd_ref[h][:, :1] + _dot_tn(kdk, v)
            gate = p_ref[:, 4096 + h * RET_DV:4096 + (h + 1) * RET_DV].astype(F32)
            mu = jnp.mean(o, axis=-1, keepdims=True)
            oc = o - mu
            xh = oc * lax.rsqrt(jnp.mean(oc * oc, axis=-1, keepdims=True) + EPS)
            y = (gate * _sigmoid(gate)) * (xh * gng_ref[:, h * RET_DV:(h + 1) * RET_DV])
            y_ref[:, h * RET_DV:(h + 1) * RET_DV] = y.astype(BF16)

    const = lambda b, i: (0, 0, 0)
    return pl.pallas_call(
        body, name="ret_fwd", grid=(B, nb),
        in_specs=[pl.BlockSpec((LB, 6144), lambda b, i: (b * nb + i, 0)),
                  pl.BlockSpec((LB, 128), lambda b, i: (i, 0)), pl.BlockSpec((LB, 128), lambda b, i: (i, 0)),
                  pl.BlockSpec((RET_HEADS, LB, LB), const), pl.BlockSpec((RET_HEADS, LB, 1), const),
                  pl.BlockSpec((RET_HEADS, LB, 1), const), pl.BlockSpec((RET_HEADS, 1, 128), const),
                  pl.BlockSpec((1, 2048), lambda b, i: (0, 0))],
        out_specs=[pl.BlockSpec((LB, 2048), lambda b, i: (b * nb + i, 0)),
                   pl.BlockSpec((None, None, RET_HEADS, RET_DK, RET_DV), lambda b, i: (b, i, 0, 0, 0))],
        out_shape=[jax.ShapeDtypeStruct((T, 2048), BF16), jax.ShapeDtypeStruct((B, nb, RET_HEADS, RET_DK, RET_DV), BF16)],
        scratch_shapes=[pltpu.VMEM((RET_HEADS, RET_DK, RET_DV), F32)], compiler_params=_params(2),
    )(proj, cos, sin, dmat, qd, kd, bd, gn_g.reshape(1, 2048))


def _ret_bwd(proj, dy, states, gn_g, consts, *, B, S):
    LB = RET_BLOCK
    nb = S // LB
    T = B * S
    dmat, qd, kd, bd, cos, sin = consts

    def body(p_ref, dy_ref, st_ref, cos_ref, sin_ref, d_ref, qd_ref, kd_ref, bd_ref, gng_ref, dp_ref, dgn_ref, dstate_s):
        b, i = pl.program_id(0), pl.program_id(1)

        @pl.when(i == 0)
        def _():
            dstate_s[...] = jnp.zeros_like(dstate_s)

        @pl.when((b == 0) & (i == 0))
        def _():
            dgn_ref[...] = jnp.zeros_like(dgn_ref)
        c, s = cos_ref[...], sin_ref[...]
        for h in range(RET_HEADS):
            vs = slice(h * RET_DV, (h + 1) * RET_DV)
            stb = st_ref[h]
            q, k, qb, kb, v, scb, o = _ret_block_fwd(p_ref, h, c, s, d_ref, qd_ref, kd_ref, stb)
            gate = p_ref[:, 4096 + h * RET_DV:4096 + (h + 1) * RET_DV].astype(F32)
            mu = jnp.mean(o, axis=-1, keepdims=True)
            oc = o - mu
            rstd = lax.rsqrt(jnp.mean(oc * oc, axis=-1, keepdims=True) + EPS)
            xh = oc * rstd
            gng = gng_ref[:, vs]
            dyh = dy_ref[:, vs].astype(F32)
            sg = _sigmoid(gate)
            silu = gate * sg
            dgn_ref[:, vs] += jnp.sum(dyh * silu * xh, axis=0, keepdims=True)
            dxh = dyh * silu * gng
            do = rstd * (dxh - jnp.mean(dxh, axis=-1, keepdims=True) - xh * jnp.mean(dxh * xh, axis=-1, keepdims=True))
            dgate = dyh * xh * gng * (sg * (1.0 + gate * (1.0 - sg)))
            dob = do.astype(BF16)
            dsb = (_dot_nt(dob, v) * d_ref[h]).astype(BF16)
            dst = dstate_s[h]
            dstb = dst.astype(BF16)
            kdk = (k * kd_ref[h]).astype(BF16)
            dqr = _dot(dsb, kb) + qd_ref[h] * _dot_nt(dob, stb)
            dkr = _dot_tn(dsb, qb) + kd_ref[h] * _dot_nt(v, dstb)
            dv = _dot_tn(scb, dob) + _dot(kdk, dstb)
            dstate_s[h] = dst * bd_ref[h][:, :1] + _dot_tn((q * qd_ref[h]).astype(BF16), dob)
            dp_ref[:, h * RET_DK:(h + 1) * RET_DK] = _rope_inv(dqr, c, s).astype(BF16)
            dp_ref[:, 1024 + h * RET_DK:1024 + (h + 1) * RET_DK] = (_rope_inv(dkr, c, s) * (RET_DK ** -0.5)).astype(BF16)
            dp_ref[:, 2048 + h * RET_DV:2048 + (h + 1) * RET_DV] = dv.astype(BF16)
            dp_ref[:, 4096 + h * RET_DV:4096 + (h + 1) * RET_DV] = dgate.astype(BF16)

    const = lambda b, i: (0, 0, 0)
    rev = lambda b, i: (b * nb + nb - 1 - i, 0)
    return pl.pallas_call(
        body, name="ret_bwd", grid=(B, nb),
        in_specs=[pl.BlockSpec((LB, 6144), rev), pl.BlockSpec((LB, 2048), rev),
                  pl.BlockSpec((None, None, RET_HEADS, RET_DK, RET_DV), lambda b, i: (b, nb - 1 - i, 0, 0, 0)),
                  pl.BlockSpec((LB, 128), lambda b, i: (nb - 1 - i, 0)), pl.BlockSpec((LB, 128), lambda b, i: (nb - 1 - i, 0)),
                  pl.BlockSpec((RET_HEADS, LB, LB), const), pl.BlockSpec((RET_HEADS, LB, 1), const),
                  pl.BlockSpec((RET_HEADS, LB, 1), const), pl.BlockSpec((RET_HEADS, 1, 128), const),
                  pl.BlockSpec((1, 2048), lambda b, i: (0, 0))],
        out_specs=[pl.BlockSpec((LB, 6144), rev), pl.BlockSpec((1, 2048), lambda b, i: (0, 0))],
        out_shape=[jax.ShapeDtypeStruct((T, 6144), BF16), jax.ShapeDtypeStruct((1, 2048), F32)],
        scratch_shapes=[pltpu.VMEM((RET_HEADS, RET_DK, RET_DV), F32)], compiler_params=_params(2),
    )(proj, dy, states, cos, sin, dmat, qd, kd, bd, gn_g.reshape(1, 2048))


def _bias_onehot(qi, QB, KW):
    r = lax.broadcasted_iota(jnp.int32, (REL_TABLE, KW), 0)
    kj = lax.broadcasted_iota(jnp.int32, (REL_TABLE, KW), 1)
    idx = jnp.maximum(kj - PAST - qi, -MAX_REL) + MAX_REL
    lo = (qi // CHUNK) * CHUNK
    valid = (kj >= lo) & (kj < lo + PAST + CHUNK)
    return jnp.where((idx == r) & valid, 1.0, 0.0).astype(F32), valid[:1]


def _att_bias(table, *, QB):
    KW = 3 * QB

    def body(t_ref, o_ref):
        onehot, valid = _bias_onehot(pl.program_id(0), QB, KW)
        val = jnp.dot(t_ref[...], onehot, preferred_element_type=F32, precision=lax.Precision.HIGHEST)
        o_ref[0] = jnp.where(valid, val, NEG)

    return pl.pallas_call(
        body, name="att_bias", grid=(QB,),
        in_specs=[pl.BlockSpec((ATT_HEADS, REL_TABLE), lambda i: (0, 0))],
        out_specs=pl.BlockSpec((1, ATT_HEADS, KW), lambda i: (i, 0, 0)),
        out_shape=jax.ShapeDtypeStruct((QB, ATT_HEADS, KW), F32), compiler_params=_params(1),
    )(table)


def _att_bias_grad(dbias, *, QB):
    KW = 3 * QB

    def body(d_ref, o_ref):
        onehot, _ = _bias_onehot(pl.program_id(0), QB, KW)
        part = lax.dot_general(d_ref[0], onehot, (((1,), (1,)), ((), ())), preferred_element_type=F32,
                               precision=lax.Precision.HIGHEST)

        @pl.when(pl.program_id(0) == 0)
        def _():
            o_ref[...] = part

        @pl.when(pl.program_id(0) > 0)
        def _():
            o_ref[...] += part

    return pl.pallas_call(
        body, name="att_bias_grad", grid=(QB,),
        in_specs=[pl.BlockSpec((1, ATT_HEADS, KW), lambda i: (i, 0, 0))],
        out_specs=pl.BlockSpec((ATT_HEADS, REL_TABLE), lambda i: (0, 0)),
        out_shape=jax.ShapeDtypeStruct((ATT_HEADS, REL_TABLE), F32), compiler_params=_params(1),
    )(dbias)


def _att_fwd(qkv, bias, *, B, S):
    QB = ATT_BLOCK
    nb = S // QB
    KW = 3 * QB
    T = B * S
    scale = ATT_DH ** -0.5

    def body(q_ref, k0, k1, k2, v0, v1, v2, b_ref, o_ref):
        i = pl.program_id(2)
        k3 = jnp.concatenate([k0[...], k1[...], k2[...]], axis=0)
        v3 = jnp.concatenate([v0[...], v1[...], v2[...]], axis=0)
        col = lax.broadcasted_iota(jnp.int32, (QB, KW), 1)
        in_seq = col >= (2 - i) * QB
        q = q_ref[...]
        outs = []
        for hh in range(2):
            hs = slice(hh * ATT_DH, (hh + 1) * ATT_DH)
            s = _dot_nt(q[:, hs], k3[:, hs]) * scale + b_ref[hh]
            s = jnp.where(in_seq, s, NEG)
            p = jnp.exp(s - jnp.max(s, axis=-1, keepdims=True))
            p = p / jnp.sum(p, axis=-1, keepdims=True)
            outs.append(_dot(p.astype(BF16), v3[:, hs]))
        o_ref[...] = jnp.concatenate(outs, axis=1).astype(BF16)

    def kv(d, col0):
        return pl.BlockSpec((QB, 128), lambda hp, b, i: (b * nb + jnp.maximum(i - d, 0), col0 + hp))

    return pl.pallas_call(
        body, name="att_fwd", grid=(8, B, nb),
        in_specs=[pl.BlockSpec((QB, 128), lambda hp, b, i: (b * nb + i, hp)),
                  kv(2, 8), kv(1, 8), kv(0, 8), kv(2, 16), kv(1, 16), kv(0, 16),
                  pl.BlockSpec((2, QB, KW), lambda hp, b, i: (hp, 0, 0))],
        out_specs=pl.BlockSpec((QB, 128), lambda hp, b, i: (b * nb + i, hp)),
        out_shape=jax.ShapeDtypeStruct((T, 1024), BF16), compiler_params=_params(3),
    )(qkv, qkv, qkv, qkv, qkv, qkv, qkv, bias)


def _att_bwd(qkv, do, bias_t, *, B, S):
    QB = ATT_BLOCK
    nb = S // QB
    KW = 3 * QB
    T = B * S
    scale = ATT_DH ** -0.5

    def body(q_ref, k0, k1, k2, v0, v1, v2, do_ref, b_ref, dq_ref, dk_ref, dv_ref, db_ref, dk_acc, dv_acc):
        b, i = pl.program_id(1), pl.program_id(2)

        @pl.when(i == 0)
        def _():
            dk_acc[...] = jnp.zeros_like(dk_acc)
            dv_acc[...] = jnp.zeros_like(dv_acc)

        @pl.when((b == 0) & (i == 0))
        def _():
            db_ref[...] = jnp.zeros_like(db_ref)

        @pl.when(i < nb)
        def _():
            k3 = jnp.concatenate([k0[...], k1[...], k2[...]], axis=0)
            v3 = jnp.concatenate([v0[...], v1[...], v2[...]], axis=0)
            k3t = k3.astype(F32).T.astype(BF16)
            q, dout = q_ref[...], do_ref[...]
            row = lax.broadcasted_iota(jnp.int32, (KW, QB), 0)
            in_seq = row >= (2 - i) * QB
            dqt, dk3, dv3 = [], [], []
            for hh in range(2):
                hs = slice(hh * ATT_DH, (hh + 1) * ATT_DH)
                st = _dot_nt(k3[:, hs], q[:, hs]) * scale + b_ref[hh]
                st = jnp.where(in_seq, st, NEG)
                p = jnp.exp(st - jnp.max(st, axis=0, keepdims=True))
                p = p / jnp.sum(p, axis=0, keepdims=True)
                dp = _dot_nt(v3[:, hs], dout[:, hs])
                ds = p * (dp - jnp.sum(p * dp, axis=0, keepdims=True))
                db_ref[hh] += ds
                dsb = (ds * scale).astype(BF16)
                dk3.append(_dot(dsb, q[:, hs]))
                dv3.append(_dot(p.astype(BF16), dout[:, hs]))
                dqt.append(_dot(k3t[hs, :], dsb))
            dq_ref[...] = jnp.concatenate(dqt, axis=0).T.astype(BF16)
            dk3 = jnp.concatenate(dk3, axis=1)
            dv3 = jnp.concatenate(dv3, axis=1)
            for d in range(3):
                slot = (i + 1 + d) % 3
                dk_acc[slot] += dk3[d * QB:(d + 1) * QB]
                dv_acc[slot] += dv3[d * QB:(d + 1) * QB]

        @pl.when(i >= 2)
        def _():
            slot = (i + 1) % 3
            dk_ref[...] = dk_acc[slot].astype(BF16)
            dv_ref[...] = dv_acc[slot].astype(BF16)
            dk_acc[slot] = jnp.zeros((QB, 128), F32)
            dv_acc[slot] = jnp.zeros((QB, 128), F32)

    def qrow(b, i):
        return b * nb + jnp.minimum(i, nb - 1)

    def kv(d, col0):
        return pl.BlockSpec((QB, 128), lambda hp, b, i: (b * nb + jnp.maximum(jnp.minimum(i, nb - 1) - d, 0), col0 + hp))

    late = pl.BlockSpec((QB, 128), lambda hp, b, i: (b * nb + jnp.maximum(i - 2, 0), hp))
    return pl.pallas_call(
        body, name="att_bwd", grid=(8, B, nb + 2),
        in_specs=[pl.BlockSpec((QB, 128), lambda hp, b, i: (qrow(b, i), hp)),
                  kv(2, 8), kv(1, 8), kv(0, 8), kv(2, 16), kv(1, 16), kv(0, 16),
                  pl.BlockSpec((QB, 128), lambda hp, b, i: (qrow(b, i), hp)),
                  pl.BlockSpec((2, KW, QB), lambda hp, b, i: (hp, 0, 0))],
        out_specs=[pl.BlockSpec((QB, 128), lambda hp, b, i: (qrow(b, i), hp)), late, late,
                   pl.BlockSpec((2, KW, QB), lambda hp, b, i: (hp, 0, 0))],
        out_shape=[jax.ShapeDtypeStruct((T, 1024), BF16)] * 3 + [jax.ShapeDtypeStruct((ATT_HEADS, KW, QB), F32)],
        scratch_shapes=[pltpu.VMEM((3, QB, 128), F32), pltpu.VMEM((3, QB, 128), F32)], compiler_params=_params(3),
    )(qkv, qkv, qkv, qkv, qkv, qkv, qkv, do, bias_t)


def _loss_head(h, tgt, g, *, tm):
    T, D = h.shape
    n = T // tm

    def body(h_ref, t_ref, g_ref, dh_ref, dg_ref, loss_ref, acc_ref):
        i = pl.program_id(0)
        x = h_ref[...]
        r = lax.rsqrt(jnp.mean(x * x, axis=-1, keepdims=True) + EPS)
        xh = x * r
        gg = g_ref[...]
        diff = xh * gg - t_ref[...]
        sq = jnp.sum(diff * diff, axis=0, keepdims=True)
        dy = diff * (1.0 / D)
        dg_part = jnp.sum(dy * xh, axis=0, keepdims=True)

        @pl.when(i == 0)
        def _():
            acc_ref[...] = sq
            dg_ref[...] = dg_part

        @pl.when(i > 0)
        def _():
            acc_ref[...] += sq
            dg_ref[...] += dg_part
        t = dy * gg
        dh_ref[...] = r * (t - xh * jnp.mean(t * xh, axis=-1, keepdims=True))

        @pl.when(i == n - 1)
        def _():
            loss_ref[...] = (0.5 / D) * jnp.sum(acc_ref[...], axis=1, keepdims=True)

    return pl.pallas_call(
        body, name="loss_head", grid=(n,),
        in_specs=[pl.BlockSpec((tm, D), lambda i: (i, 0)), pl.BlockSpec((tm, D), lambda i: (i, 0)),
                  pl.BlockSpec((1, D), lambda i: (0, 0))],
        out_specs=[pl.BlockSpec((tm, D), lambda i: (i, 0)), pl.BlockSpec((1, D), lambda i: (0, 0)),
                   pl.BlockSpec((1, 1), lambda i: (0, 0))],
        out_shape=[jax.ShapeDtypeStruct((T, D), F32), jax.ShapeDtypeStruct((1, D), F32), jax.ShapeDtypeStruct((1, 1), F32)],
        scratch_shapes=[pltpu.VMEM((1, D), F32)], compiler_params=_params(1),
    )(h, tgt, g.reshape(1, D))


def _tok_tile(T, want):
    t = min(T, want)
    assert T % t == 0
    return t


def _local_step(x, tgt, w, mix_g, gn_g, rel_bias, mlp_g, fin_g):
    B, S, D = x.shape
    T = B * S
    h0 = x.reshape(T, D)
    tgt = tgt.reshape(T, D)
    tm = _tok_tile(T, 1024)
    tb = _tok_tile(T, 512)
    consts = _ret_consts(S, RET_BLOCK)

    proj, hn0 = _mm_nn(h0, w["ret_w_in"], "col", tm=tm, tn=768, tk=D, out_dtype=BF16, name="ret_in", norm_g=mix_g[0])
    y_ret, states = _ret_fwd(proj, gn_g, consts, B=B, S=S)
    h1 = _mm_nn(y_ret, w["ret_w_out"], "row", tm=tm, tn=D, tk=512, out_dtype=F32, name="ret_out", res=h0)
    a0, hm0 = _mm_nn(h1, w["mlp_w1_0"], "col", tm=tm, tn=1024, tk=D, out_dtype=BF16, name="mlp0_up", norm_g=mlp_g[0], act="relu2")
    h2 = _mm_nn(a0, w["mlp_w2_0"], "row", tm=tm, tn=D, tk=1024, out_dtype=F32, name="mlp0_down", res=h1)
    bias_q = _att_bias(rel_bias, QB=ATT_BLOCK)
    qkv, hn1 = _mm_nn(h2, w["att_w_in"], "col", tm=tm, tn=768, tk=D, out_dtype=BF16, name="att_in", norm_g=mix_g[1])
    o_att = _att_fwd(qkv, jnp.transpose(bias_q, (1, 0, 2)), B=B, S=S)
    h3 = _mm_nn(o_att, w["att_w_out"], "row", tm=tm, tn=D, tk=256, out_dtype=F32, name="att_out", res=h2)
    a1, hm1 = _mm_nn(h3, w["mlp_w1_1"], "col", tm=tm, tn=1024, tk=D, out_dtype=BF16, name="mlp1_up", norm_g=mlp_g[1], act="relu2")
    h4 = _mm_nn(a1, w["mlp_w2_1"], "row", tm=tm, tn=D, tk=1024, out_dtype=F32, name="mlp1_down", res=h3)
    dh4, d_fin_g, loss = _loss_head(h4, tgt, fin_g, tm=tb)

    gw = {}
    gw["mlp_w2_1"] = _mm_tn(a1, dh4, "row", tt=tb, tk=1024, tn=D, name="d_mlp1_w2")
    dz1 = _mm_nt(dh4, w["mlp_w2_1"], "row", tm=tb, tn=1024, tk=D, name="d_mlp1_act", epi="sqrt2", a2=a1)
    gw["mlp_w1_1"] = _mm_tn(hm1, dz1, "col", tt=tb, tk=D, tn=1024, name="d_mlp1_w1")
    dh3, d_mlp_g1 = _mm_nt(dz1, w["mlp_w1_1"], "col", tm=tb, tn=D, tk=1024, name="d_mlp1_in", epi="normbwd",
                           h=h3, g=mlp_g[1], dres=dh4)
    gw["att_w_out"] = _mm_tn(o_att, dh3, "row", tt=tb, tk=256, tn=D, name="d_att_wout")
    do_att = _mm_nt(dh3, w["att_w_out"], "row", tm=tb, tn=256, tk=D, name="d_att_o", epi="bf16")
    dq, dk, dv, dbias_t = _att_bwd(qkv, do_att, jnp.transpose(bias_q, (1, 2, 0)), B=B, S=S)
    d_rel = _att_bias_grad(jnp.transpose(dbias_t, (2, 0, 1)), QB=ATT_BLOCK)
    dqkv = jnp.concatenate([dq, dk, dv], axis=1)
    gw["att_w_in"] = _mm_tn(hn1, dqkv, "col", tt=tb, tk=D, tn=768, name="d_att_win")
    dh2, d_mix_g1 = _mm_nt(dqkv, w["att_w_in"], "col", tm=tb, tn=D, tk=768, name="d_att_in", epi="normbwd",
                           h=h2, g=mix_g[1], dres=dh3)
    gw["mlp_w2_0"] = _mm_tn(a0, dh2, "row", tt=tb, tk=1024, tn=D, name="d_mlp0_w2")
    dz0 = _mm_nt(dh2, w["mlp_w2_0"], "row", tm=tb, tn=1024, tk=D, name="d_mlp0_act", epi="sqrt2", a2=a0)
    gw["mlp_w1_0"] = _mm_tn(hm0, dz0, "col", tt=tb, tk=D, tn=1024, name="d_mlp0_w1")
    dh1, d_mlp_g0 = _mm_nt(dz0, w["mlp_w1_0"], "col", tm=tb, tn=D, tk=1024, name="d_mlp0_in", epi="normbwd",
                           h=h1, g=mlp_g[0], dres=dh2)
    gw["ret_w_out"] = _mm_tn(y_ret, dh1, "row", tt=tb, tk=512, tn=D, name="d_ret_wout")
    dy_ret = _mm_nt(dh1, w["ret_w_out"], "row", tm=tb, tn=512, tk=D, name="d_ret_y", epi="bf16")
    dproj, d_gn = _ret_bwd(proj, dy_ret, states, gn_g, consts, B=B, S=S)
    gw["ret_w_in"] = _mm_tn(hn0, dproj, "col", tt=tb, tk=D, tn=768, name="d_ret_win")
    dx, d_mix_g0 = _mm_nt(dproj, w["ret_w_in"], "col", tm=tb, tn=D, tk=768, name="d_ret_in", epi="normbwd",
                          h=h0, g=mix_g[0], dres=dh1)
    small = {"mix_norm_g": jnp.concatenate([d_mix_g0, d_mix_g1], axis=0), "ret_gn_g": d_gn, "att_rel_bias": d_rel,
             "mlp_norm_g": jnp.concatenate([d_mlp_g0, d_mlp_g1], axis=0), "final_norm_g": d_fin_g}
    return loss.reshape(()), dx.reshape(B, S, D), gw, small


def _row_tile(r, want=256):
    t = min(r, want)
    assert r % t == 0
    return t


def _cast_bf16(a, name):
    r, c = a.shape
    tr = _row_tile(r)

    def body(a_ref, o_ref):
        o_ref[...] = a_ref[...].astype(BF16)

    return pl.pallas_call(
        body, name=name, grid=(r // tr,), in_specs=[pl.BlockSpec((tr, c), lambda i: (i, 0))],
        out_specs=pl.BlockSpec((tr, c), lambda i: (i, 0)), out_shape=jax.ShapeDtypeStruct((r, c), BF16),
        compiler_params=_params(1),
    )(a)


def _add_sibling(place, g, recv, name):
    _, r, c = g.shape
    hr = r // 2
    tr = _row_tile(hr)
    nrt = hr // tr

    def body(place_ref, g_ref, r_ref, sb_ref, own_ref):
        v = g_ref[...] + r_ref[...]
        sb_ref[...] = v.astype(BF16)

        @pl.when(pl.program_id(1) == place_ref[0])
        def _():
            own_ref[...] = v

    grid_spec = pltpu.PrefetchScalarGridSpec(
        num_scalar_prefetch=1, grid=(nrt, N_CHIPS),
        in_specs=[pl.BlockSpec((None, tr, c), lambda i, s, pr: (s, pr[1] * nrt + i, 0)),
                  pl.BlockSpec((None, tr, c), lambda i, s, pr: (s, i, 0))],
        out_specs=[pl.BlockSpec((None, tr, c), lambda i, s, pr: (s, i, 0)), pl.BlockSpec((tr, c), lambda i, s, pr: (i, 0))],
    )
    return pl.pallas_call(
        body, name=name, grid_spec=grid_spec,
        out_shape=[jax.ShapeDtypeStruct((N_CHIPS, hr, c), BF16), jax.ShapeDtypeStruct((hr, c), F32)],
        compiler_params=_params(2),
    )(place, g, recv)


def _add_chips(own, recv, name):
    hr, c = own.shape
    tr = _row_tile(hr)

    def body(o_ref, r_ref, t_ref):
        t_ref[...] = ((o_ref[...] + r_ref[0].astype(F32)) + r_ref[1].astype(F32)) + r_ref[2].astype(F32)

    return pl.pallas_call(
        body, name=name, grid=(hr // tr,),
        in_specs=[pl.BlockSpec((tr, c), lambda i: (i, 0)), pl.BlockSpec((3, tr, c), lambda i: (0, i, 0))],
        out_specs=pl.BlockSpec((tr, c), lambda i: (i, 0)), out_shape=jax.ShapeDtypeStruct((hr, c), F32),
        compiler_params=_params(1),
    )(own, recv)


def _adamw(w, g, m, v, name):
    r, c = w.shape
    tr = _row_tile(r)

    def body(w_ref, g_ref, m_ref, v_ref, d_ref, nm_ref, nv_ref):
        gg = g_ref[...]
        nm = ADAM_B1 * m_ref[...] + (1.0 - ADAM_B1) * gg
        nv = ADAM_B2 * v_ref[...] + (1.0 - ADAM_B2) * (gg * gg)
        m_hat = nm / (1.0 - ADAM_B1 ** ADAM_STEP)
        v_hat = nv / (1.0 - ADAM_B2 ** ADAM_STEP)
        d_ref[...] = -ADAM_LR * (m_hat / (jnp.sqrt(v_hat) + ADAM_EPS) + ADAM_WD * w_ref[...])
        nm_ref[...] = nm
        nv_ref[...] = nv

    spec = pl.BlockSpec((tr, c), lambda i: (i, 0))
    return pl.pallas_call(
        body, name=name, grid=(r // tr,), in_specs=[spec] * 4, out_specs=[spec] * 3,
        out_shape=[jax.ShapeDtypeStruct((r, c), F32)] * 3, compiler_params=_params(1),
    )(w, g, m, v)


HBM_SPEC = pl.BlockSpec(memory_space=pltpu.HBM)


def _place():
    return lax.axis_index("x"), lax.axis_index("y"), lax.axis_index("c")


def _other_chips(x, y):
    return [(1 - x, y), (x, 1 - y), (1 - x, 1 - y)]


def _remote(src, dst, ssem, rsem, dev):
    return pltpu.make_async_remote_copy(src_ref=src, dst_ref=dst, send_sem=ssem, recv_sem=rsem, device_id=dev,
                                        device_id_type=MESH)


def _all_gather_shards(shards):
    n = len(shards)

    def body(*refs):
        ins, outs = refs[:n], refs[n:2 * n]
        lsem, s1, r1, s2, r2 = refs[2 * n:]
        x, y, c = _place()
        p = 2 * x + y
        chips = _other_chips(x, y)
        cidx = [2 * qx + qy for qx, qy in chips]
        local, pending = [], []
        for t in range(n):
            hr = ins[t].shape[0] // 2
            mine = pl.ds(c * hr, hr)
            lc = pltpu.make_async_copy(ins[t], outs[t].at[p], lsem.at[t])
            lc.start()
            local.append(lc)
            for j, (qx, qy) in enumerate(chips):
                cp = _remote(ins[t].at[mine], outs[t].at[p, mine], s1.at[t, j], r1.at[t, j], (qx, qy, c))
                cp.start()
                pending.append(cp)
        for t in range(n):
            hr = ins[t].shape[0] // 2
            mine = pl.ds(c * hr, hr)
            for j, (qx, qy) in enumerate(chips):
                _remote(ins[t].at[mine], outs[t].at[cidx[j], mine], s1.at[t, j], r1.at[t, j], (qx, qy, c)).wait_recv()
                fw = _remote(outs[t].at[cidx[j], mine], outs[t].at[cidx[j], mine], s2.at[t, j], r2.at[t, j], (x, y, 1 - c))
                fw.start()
                pending.append(fw)
        for t in range(n):
            hr = ins[t].shape[0] // 2
            theirs = pl.ds((1 - c) * hr, hr)
            for j in range(3):
                _remote(outs[t].at[cidx[j], theirs], outs[t].at[cidx[j], theirs], s2.at[t, j], r2.at[t, j],
                        (x, y, 1 - c)).wait_recv()
        for cp in pending:
            cp.wait_send()
        for lc in local:
            lc.wait()

    sem = pltpu.SemaphoreType.DMA
    return pl.pallas_call(
        body, name="gather_weights", in_specs=[HBM_SPEC] * n, out_specs=[HBM_SPEC] * n,
        out_shape=[jax.ShapeDtypeStruct((N_CHIPS,) + s.shape, s.dtype) for s in shards],
        scratch_shapes=[sem((n,)), sem((n, 3)), sem((n, 3)), sem((n, 3)), sem((n, 3))],
        compiler_params=pltpu.CompilerParams(has_side_effects=True),
    )(*shards)


def _exchange_with_sibling(grads):
    n = len(grads)

    def body(*refs):
        ins, outs = refs[:n], refs[n:2 * n]
        ssem, rsem = refs[2 * n:]
        x, y, c = _place()
        copies = []
        for t in range(n):
            hr = ins[t].shape[1] // 2
            cp = _remote(ins[t].at[:, pl.ds((1 - c) * hr, hr), :], outs[t], ssem.at[t], rsem.at[t], (x, y, 1 - c))
            cp.start()
            copies.append(cp)
        for cp in copies:
            cp.wait()

    sem = pltpu.SemaphoreType.DMA
    return pl.pallas_call(
        body, name="grads_to_sibling", in_specs=[HBM_SPEC] * n, out_specs=[HBM_SPEC] * n,
        out_shape=[jax.ShapeDtypeStruct((N_CHIPS, g.shape[1] // 2, g.shape[2]), g.dtype) for g in grads],
        scratch_shapes=[sem((n,)), sem((n,))], compiler_params=pltpu.CompilerParams(has_side_effects=True),
    )(*grads)


def _exchange_with_chips(sums):
    n = len(sums)

    def body(*refs):
        ins, outs = refs[:n], refs[n:2 * n]
        ssem, rsem = refs[2 * n:]
        x, y, c = _place()
        chips = _other_chips(x, y)
        copies = []
        for t in range(n):
            for j, (qx, qy) in enumerate(chips):
                cp = _remote(ins[t].at[2 * qx + qy], outs[t].at[j], ssem.at[t, j], rsem.at[t, j], (qx, qy, c))
                cp.start()
                copies.append(cp)
        for cp in copies:
            cp.wait()

    sem = pltpu.SemaphoreType.DMA
    return pl.pallas_call(
        body, name="grads_to_chips", in_specs=[HBM_SPEC] * n, out_specs=[HBM_SPEC] * n,
        out_shape=[jax.ShapeDtypeStruct((3,) + s.shape[1:], s.dtype) for s in sums],
        scratch_shapes=[sem((n, 3)), sem((n, 3))], compiler_params=pltpu.CompilerParams(has_side_effects=True),
    )(*sums)


def _share_with_sibling(totals):
    n = len(totals)

    def body(*refs):
        ins, outs = refs[:n], refs[n:2 * n]
        lsem, ssem, rsem = refs[2 * n:]
        x, y, c = _place()
        local, copies = [], []
        for t in range(n):
            hr = ins[t].shape[0]
            mine = pl.ds(c * hr, hr)
            lc = pltpu.make_async_copy(ins[t], outs[t].at[mine], lsem.at[t])
            lc.start()
            local.append(lc)
            cp = _remote(ins[t], outs[t].at[mine], ssem.at[t], rsem.at[t], (x, y, 1 - c))
            cp.start()
            copies.append(cp)
        for t in range(n):
            hr = ins[t].shape[0]
            _remote(ins[t], outs[t].at[pl.ds((1 - c) * hr, hr)], ssem.at[t], rsem.at[t], (x, y, 1 - c)).wait_recv()
        for cp in copies:
            cp.wait_send()
        for lc in local:
            lc.wait()

    sem = pltpu.SemaphoreType.DMA
    return pl.pallas_call(
        body, name="grads_share", in_specs=[HBM_SPEC] * n, out_specs=[HBM_SPEC] * n,
        out_shape=[jax.ShapeDtypeStruct((2 * s.shape[0], s.shape[1]), s.dtype) for s in totals],
        scratch_shapes=[sem((n,)), sem((n,)), sem((n,))], compiler_params=pltpu.CompilerParams(has_side_effects=True),
    )(*totals)


def _all_reduce_small(buf):
    R, C = buf.shape

    def body(in_ref, out_ref, gather, ssem, rsem):
        x, y, c = _place()
        me = 4 * x + 2 * y + c
        gather[me] = in_ref[...]
        flips = [(fx, fy, fc) for fx in (0, 1) for fy in (0, 1) for fc in (0, 1) if fx or fy or fc]
        peers = [(x + fx - 2 * x * fx, y + fy - 2 * y * fy, c + fc - 2 * c * fc) for fx, fy, fc in flips]
        copies = [_remote(in_ref, gather.at[me], ssem.at[k], rsem.at[k], peer) for k, peer in enumerate(peers)]
        for cp in copies:
            cp.start()
        for k, (px, py, pc) in enumerate(peers):
            _remote(in_ref, gather.at[4 * px + 2 * py + pc], ssem.at[k], rsem.at[k], (px, py, pc)).wait_recv()
        for cp in copies:
            cp.wait_send()
        acc = gather[0]
        for d in range(1, 8):
            acc = acc + gather[d]
        out_ref[...] = acc

    sem = pltpu.SemaphoreType.DMA
    vmem = pl.BlockSpec(memory_space=pltpu.VMEM)
    return pl.pallas_call(
        body, name="small_grads_sum", in_specs=[vmem], out_specs=vmem, out_shape=jax.ShapeDtypeStruct((R, C), F32),
        scratch_shapes=[pltpu.VMEM((8, R, C), F32), sem((7,)), sem((7,))],
        compiler_params=pltpu.CompilerParams(has_side_effects=True),
    )(buf)


BIG = ["ret_w_in", "ret_w_out", "att_w_in", "att_w_out", "mlp_w1_0", "mlp_w1_1", "mlp_w2_0", "mlp_w2_1"]


def _split_big(ret_w_in, ret_w_out, att_w_in, att_w_out, mlp_w1, mlp_w2):
    return {"ret_w_in": ret_w_in[0], "ret_w_out": ret_w_out[0], "att_w_in": att_w_in[0], "att_w_out": att_w_out[0],
            "mlp_w1_0": mlp_w1[0], "mlp_w1_1": mlp_w1[1], "mlp_w2_0": mlp_w2[0], "mlp_w2_1": mlp_w2[1]}


def kernel(x, mix_norm_g, ret_w_in, ret_gn_g, ret_w_out, att_w_in, att_rel_bias, att_w_out, mlp_norm_g, mlp_w1, mlp_w2, final_norm_g, loss_target, m_mix_norm_g, m_ret_w_in, m_ret_gn_g, m_ret_w_out, m_att_w_in, m_att_rel_bias, m_att_w_out, m_mlp_norm_g, m_mlp_w1, m_mlp_w2, m_final_norm_g, v_mix_norm_g, v_ret_w_in, v_ret_gn_g, v_ret_w_out, v_att_w_in, v_att_rel_bias, v_att_w_out, v_mlp_norm_g, v_mlp_w1, v_mlp_w2, v_final_norm_g):
    xi, yi, ci = _place()
    chip = 2 * xi + yi
    w32 = _split_big(ret_w_in, ret_w_out, att_w_in, att_w_out, mlp_w1, mlp_w2)
    m32 = _split_big(m_ret_w_in, m_ret_w_out, m_att_w_in, m_att_w_out, m_mlp_w1, m_mlp_w2)
    v32 = _split_big(v_ret_w_in, v_ret_w_out, v_att_w_in, v_att_w_out, v_mlp_w1, v_mlp_w2)

    gathered = _all_gather_shards([_cast_bf16(w32[n], name="cast_" + n) for n in BIG] + [att_rel_bias[0]])
    w = dict(zip(BIG, gathered[:-1]))
    rel_full = jnp.transpose(gathered[-1], (1, 0, 2)).reshape(ATT_HEADS, REL_TABLE)

    loss_local, grad_x, gw, small = _local_step(x, loss_target, w, mix_norm_g, ret_gn_g[0], rel_full, mlp_norm_g, final_norm_g)
    loss = lax.psum(loss_local, ("x", "y", "c"))

    place = jnp.stack([chip, ci]).astype(jnp.int32)
    from_sibling = _exchange_with_sibling([gw[n] for n in BIG])
    sums = [_add_sibling(place, gw[n], r, name="chip_sum_" + n) for n, r in zip(BIG, from_sibling)]
    from_chips = _exchange_with_chips([s[0] for s in sums])
    totals = [_add_chips(s[1], r, name="total_" + n) for n, s, r in zip(BIG, sums, from_chips)]
    g_big = dict(zip(BIG, _share_with_sibling(totals)))

    rows = jnp.concatenate([small["mix_norm_g"], small["mlp_norm_g"], small["final_norm_g"], small["ret_gn_g"].reshape(2, D_MODEL),
                            small["att_rel_bias"].reshape(5, D_MODEL), jnp.zeros((4, D_MODEL), F32)], axis=0)
    rows = _all_reduce_small(rows)
    g_small = {"mix_norm_g": rows[0:2], "mlp_norm_g": rows[2:4], "final_norm_g": rows[4:5], "ret_gn_g": rows[5:7].reshape(1, 2048),
               "att_rel_bias": lax.dynamic_slice_in_dim(rows[7:12].reshape(ATT_HEADS, REL_TABLE), chip * (REL_TABLE // N_CHIPS),
                                                        REL_TABLE // N_CHIPS, axis=1)}

    upd = {n: _adamw(w32[n], g_big[n], m32[n], v32[n], name="adamw_" + n) for n in BIG}
    small_in = {"mix_norm_g": (mix_norm_g, m_mix_norm_g, v_mix_norm_g), "ret_gn_g": (ret_gn_g, m_ret_gn_g, v_ret_gn_g),
                "att_rel_bias": (att_rel_bias[0], m_att_rel_bias[0], v_att_rel_bias[0]),
                "mlp_norm_g": (mlp_norm_g, m_mlp_norm_g, v_mlp_norm_g),
                "final_norm_g": (final_norm_g.reshape(1, -1), m_final_norm_g.reshape(1, -1), v_final_norm_g.reshape(1, -1))}
    upd_small = {n: _adamw(wv, g_small[n], mv, vv, name="adamw_" + n) for n, (wv, mv, vv) in small_in.items()}

    def big_out(k):
        def get(n):
            return g_big[n] if k is None else upd[n][k]
        return {"ret_w_in": get("ret_w_in")[None], "ret_w_out": get("ret_w_out")[None], "att_w_in": get("att_w_in")[None],
                "att_w_out": get("att_w_out")[None], "mlp_w1": jnp.stack([get("mlp_w1_0"), get("mlp_w1_1")]),
                "mlp_w2": jnp.stack([get("mlp_w2_0"), get("mlp_w2_1")])}

    def small_out(k):
        def get(n):
            return g_small[n] if k is None else upd_small[n][k]
        return {"mix_norm_g": get("mix_norm_g"), "ret_gn_g": get("ret_gn_g"), "att_rel_bias": get("att_rel_bias")[None],
                "mlp_norm_g": get("mlp_norm_g"), "final_norm_g": get("final_norm_g").reshape(-1)}

    order = ["mix_norm_g", "ret_w_in", "ret_gn_g", "ret_w_out", "att_w_in", "att_rel_bias", "att_w_out", "mlp_norm_g",
             "mlp_w1", "mlp_w2", "final_norm_g"]
    outs = [loss, grad_x]
    for k in (None, 0, 1, 2):
        both = {**big_out(k), **small_out(k)}
        outs += [both[n] for n in order]
    return tuple(outs)
```

```python
import jax
import jax.numpy as jnp
from jax import lax
from jax.experimental import pallas as pl
from jax.experimental.pallas import tpu as pltpu

F32 = jnp.float32
BF16 = jnp.bfloat16
MESH = pl.DeviceIdType.MESH

D_MODEL = 1024
CHUNK = 64
RET_HEADS = 4
RET_DK = 256
RET_DV = 512
ROPE_BASE = 10000.0
ATT_HEADS = 16
ATT_DH = 64
PAST = 512
MAX_REL = 256
REL_TABLE = MAX_REL + CHUNK
EPS = 1e-6
NEG = -1e30
N_CHIPS = 4

ADAM_LR = 0.001
ADAM_B1 = 0.9
ADAM_B2 = 0.999
ADAM_EPS = 1e-08
ADAM_WD = 0.01
ADAM_STEP = 10

RET_BLOCK = 256
ATT_BLOCK = 256
VMEM_LIMIT = 56 * 1024 * 1024


def _params(n_axes, **kw):
    return pltpu.CompilerParams(dimension_semantics=("arbitrary",) * n_axes, vmem_limit_bytes=VMEM_LIMIT, **kw)


def _dot(a, b):
    return jnp.dot(a, b, preferred_element_type=F32)


def _dot_nt(a, b):
    return lax.dot_general(a, b, (((1,), (1,)), ((), ())), preferred_element_type=F32)


def _dot_tn(a, b):
    return lax.dot_general(a, b, (((0,), (0,)), ((), ())), preferred_element_type=F32)


def _sigmoid(x):
    return 1.0 / (1.0 + jnp.exp(-x))


def _w_spec_nn(w, wkind, tk, tn):
    if wkind == "col":
        per = w.shape[2] // tn
        assert w.shape[2] % tn == 0 and w.shape[1] % tk == 0
        return 4 * w.shape[2], pl.BlockSpec((None, tk, tn), lambda i, j, k: (j // per, k, j % per))
    per = w.shape[1] // tk
    assert w.shape[1] % tk == 0 and w.shape[2] % tn == 0
    return w.shape[2], pl.BlockSpec((None, tk, tn), lambda i, j, k: (k // per, k % per, j))


def _mm_nn(a, w, wkind, *, tm, tn, tk, out_dtype, name, norm_g=None, act=None, res=None):
    M, K = a.shape
    N, w_spec = _w_spec_nn(w, wkind, tk, tn)
    nk = K // tk
    has_norm = norm_g is not None
    assert M % tm == 0 and N % tn == 0 and K % tk == 0
    assert not has_norm or nk == 1

    def body(*refs):
        it = iter(refs)
        a_ref, w_ref = next(it), next(it)
        g_ref = next(it) if has_norm else None
        r_ref = next(it) if res is not None else None
        o_ref = next(it)
        hn_ref = next(it) if has_norm else None
        acc_ref = next(it) if nk > 1 else None
        hn_s = next(it) if has_norm else None
        j, k = pl.program_id(1), pl.program_id(2)
        if has_norm:
            @pl.when(j == 0)
            def _():
                x = a_ref[...].astype(F32)
                r = lax.rsqrt(jnp.mean(x * x, axis=-1, keepdims=True) + EPS)
                y = (x * r * g_ref[...]).astype(BF16)
                hn_s[...] = y
                hn_ref[...] = y
            lhs = hn_s[...]
        else:
            lhs = a_ref[...].astype(BF16)
        part = _dot(lhs, w_ref[...])

        def finish(acc):
            if act == "relu2":
                u = jnp.maximum(acc, 0.0)
                acc = u * u
            if r_ref is not None:
                acc = acc + r_ref[...]
            o_ref[...] = acc.astype(out_dtype)

        if nk == 1:
            finish(part)
        else:
            @pl.when(k == 0)
            def _():
                acc_ref[...] = part

            @pl.when(k > 0)
            def _():
                acc_ref[...] += part

            @pl.when(k == nk - 1)
            def _():
                finish(acc_ref[...])

    in_specs = [pl.BlockSpec((tm, tk), lambda i, j, k: (i, k)), w_spec]
    args = [a, w]
    if has_norm:
        in_specs.append(pl.BlockSpec((1, K), lambda i, j, k: (0, 0)))
        args.append(norm_g.reshape(1, K))
    if res is not None:
        in_specs.append(pl.BlockSpec((tm, tn), lambda i, j, k: (i, j)))
        args.append(res)
    out_shape = [jax.ShapeDtypeStruct((M, N), out_dtype)]
    out_specs = [pl.BlockSpec((tm, tn), lambda i, j, k: (i, j))]
    scratch = []
    if has_norm:
        out_shape.append(jax.ShapeDtypeStruct((M, K), BF16))
        out_specs.append(pl.BlockSpec((tm, K), lambda i, j, k: (i, 0)))
    if nk > 1:
        scratch.append(pltpu.VMEM((tm, tn), F32))
    if has_norm:
        scratch.append(pltpu.VMEM((tm, K), BF16))
    outs = pl.pallas_call(
        body, name=name, grid=(M // tm, N // tn, nk), in_specs=in_specs, out_specs=out_specs, out_shape=out_shape,
        scratch_shapes=scratch, compiler_params=_params(3),
    )(*args)
    return outs if has_norm else outs[0]


def _mm_nt(a, w, wkind, *, tm, tn, tk, name, epi, a2=None, h=None, g=None, dres=None):
    M, Nw = a.shape
    if wkind == "col":
        Kw, per = w.shape[1], w.shape[2] // tk
        assert w.shape[2] % tk == 0 and Nw == 4 * w.shape[2]
        w_spec = pl.BlockSpec((None, tn, tk), lambda i, j, k: (k // per, j, k % per))
    else:
        Kw, per = 4 * w.shape[1], w.shape[1] // tn
        assert w.shape[1] % tn == 0 and Nw == w.shape[2]
        w_spec = pl.BlockSpec((None, tn, tk), lambda i, j, k: (j // per, j % per, k))
    nk = Nw // tk
    assert M % tm == 0 and Kw % tn == 0 and Nw % tk == 0
    if epi == "normbwd":
        assert tn == Kw

    def body(*refs):
        it = iter(refs)
        a_ref, w_ref = next(it), next(it)
        a2_ref = next(it) if epi == "sqrt2" else None
        if epi == "normbwd":
            h_ref, g_ref, dres_ref = next(it), next(it), next(it)
        o_ref = next(it)
        dg_ref = next(it) if epi == "normbwd" else None
        acc_ref = next(it) if nk > 1 else None
        i, k = pl.program_id(0), pl.program_id(2)
        part = _dot_nt(a_ref[...].astype(BF16), w_ref[...])

        def finish(acc):
            if epi == "bf16":
                o_ref[...] = acc.astype(BF16)
            elif epi == "sqrt2":
                o_ref[...] = (acc * (2.0 * jnp.sqrt(a2_ref[...].astype(F32)))).astype(BF16)
            else:
                x = h_ref[...]
                r = lax.rsqrt(jnp.mean(x * x, axis=-1, keepdims=True) + EPS)
                xh = x * r
                dg_part = jnp.sum(acc * xh, axis=0, keepdims=True)

                @pl.when(i == 0)
                def _():
                    dg_ref[...] = dg_part

                @pl.when(i > 0)
                def _():
                    dg_ref[...] += dg_part
                t = acc * g_ref[...]
                dx = r * (t - xh * jnp.mean(t * xh, axis=-1, keepdims=True))
                o_ref[...] = dres_ref[...] + dx

        if nk == 1:
            finish(part)
        else:
            @pl.when(k == 0)
            def _():
                acc_ref[...] = part

            @pl.when(k > 0)
            def _():
                acc_ref[...] += part

            @pl.when(k == nk - 1)
            def _():
                finish(acc_ref[...])

    in_specs = [pl.BlockSpec((tm, tk), lambda i, j, k: (i, k)), w_spec]
    args = [a, w]
    out_dtype = BF16
    if epi == "sqrt2":
        in_specs.append(pl.BlockSpec((tm, tn), lambda i, j, k: (i, j)))
        args.append(a2)
    if epi == "normbwd":
        in_specs += [pl.BlockSpec((tm, Kw), lambda i, j, k: (i, 0)), pl.BlockSpec((1, Kw), lambda i, j, k: (0, 0)),
                     pl.BlockSpec((tm, Kw), lambda i, j, k: (i, 0))]
        args += [h, g.reshape(1, Kw), dres]
        out_dtype = F32
    out_shape = [jax.ShapeDtypeStruct((M, Kw), out_dtype)]
    out_specs = [pl.BlockSpec((tm, tn), lambda i, j, k: (i, j))]
    if epi == "normbwd":
        out_shape.append(jax.ShapeDtypeStruct((1, Kw), F32))
        out_specs.append(pl.BlockSpec((1, Kw), lambda i, j, k: (0, 0)))
    scratch = [pltpu.VMEM((tm, tn), F32)] if nk > 1 else []
    outs = pl.pallas_call(
        body, name=name, grid=(M // tm, Kw // tn, nk), in_specs=in_specs, out_specs=out_specs, out_shape=out_shape,
        scratch_shapes=scratch, compiler_params=_params(3),
    )(*args)
    return outs if epi == "normbwd" else outs[0]


def _mm_tn(a, b, okind, *, tt, tk, tn, name):
    T, K = a.shape
    N = b.shape[1]
    assert T % tt == 0 and K % tk == 0 and N % tn == 0
    nt = T // tt
    if okind == "col":
        per = (N // N_CHIPS) // tn
        assert (N // N_CHIPS) % tn == 0
        out_shape = jax.ShapeDtypeStruct((N_CHIPS, K, N // N_CHIPS), F32)
        out_spec = pl.BlockSpec((None, tk, tn), lambda ki, nj, t: (nj // per, ki, nj % per))
    else:
        per = (K // N_CHIPS) // tk
        assert (K // N_CHIPS) % tk == 0
        out_shape = jax.ShapeDtypeStruct((N_CHIPS, K // N_CHIPS, N), F32)
        out_spec = pl.BlockSpec((None, tk, tn), lambda ki, nj, t: (ki // per, ki % per, nj))

    def body(a_ref, b_ref, o_ref):
        t = pl.program_id(2)
        part = _dot_tn(a_ref[...].astype(BF16), b_ref[...].astype(BF16))

        @pl.when(t == 0)
        def _():
            o_ref[...] = part

        @pl.when(t > 0)
        def _():
            o_ref[...] += part

    return pl.pallas_call(
        body, name=name, grid=(K // tk, N // tn, nt),
        in_specs=[pl.BlockSpec((tt, tk), lambda ki, nj, t: (t, ki)), pl.BlockSpec((tt, tn), lambda ki, nj, t: (t, nj))],
        out_specs=out_spec, out_shape=out_shape, compiler_params=_params(3),
    )(a, b)


def _ret_consts(S, LB):
    log_gamma = jnp.log1p(-jnp.exp2(-5.0 - jnp.arange(RET_HEADS, dtype=F32)))
    idx = jnp.arange(LB, dtype=F32)
    n, m = idx[:, None], idx[None, :]
    cn, cm = jnp.floor(n / CHUNK), jnp.floor(m / CHUNK)
    dist = jnp.where(cm == cn, jnp.abs(n - m), n - m)
    dmat = jnp.where((cm <= cn)[None], jnp.exp(log_gamma[:, None, None] * dist[None]), 0.0)
    qd = jnp.exp(log_gamma[:, None] * (idx + 1.0)[None, :])[..., None]
    kd = jnp.exp(log_gamma[:, None] * (LB - 1 - idx)[None, :])[..., None]
    bd = jnp.exp(log_gamma * LB).reshape(RET_HEADS, 1, 1) * jnp.ones((RET_HEADS, 1, 128), F32)
    half = RET_DK // 2
    inv = jnp.exp(-jnp.log(ROPE_BASE) * jnp.arange(half, dtype=F32) / half)
    ang = jnp.arange(S, dtype=F32)[:, None] * inv[None, :]
    return dmat.astype(F32), qd.astype(F32), kd.astype(F32), bd, jnp.cos(ang), jnp.sin(ang)


def _rope(t, c, s):
    t1, t2 = t[:, :128], t[:, 128:]
    return jnp.concatenate([t1 * c - t2 * s, t1 * s + t2 * c], axis=-1)


def _rope_inv(d, c, s):
    d1, d2 = d[:, :128], d[:, 128:]
    return jnp.concatenate([d1 * c + d2 * s, d2 * c - d1 * s], axis=-1)


def _ret_block_fwd(p_ref, h, c, s, d_ref, qd_ref, kd_ref, stb):
    q = _rope(p_ref[:, h * RET_DK:(h + 1) * RET_DK].astype(F32), c, s)
    k = _rope(p_ref[:, 1024 + h * RET_DK:1024 + (h + 1) * RET_DK].astype(F32), c, s) * (RET_DK ** -0.5)
    v = p_ref[:, 2048 + h * RET_DV:2048 + (h + 1) * RET_DV]
    qb, kb = q.astype(BF16), k.astype(BF16)
    scb = (_dot_nt(qb, kb) * d_ref[h]).astype(BF16)
    o = _dot(scb, v) + qd_ref[h] * _dot(qb, stb)
    return q, k, qb, kb, v, scb, o


def _ret_fwd(proj, gn_g, consts, *, B, S):
    LB = RET_BLOCK
    nb = S // LB
    T = B * S
    dmat, qd, kd, bd, cos, sin = consts

    def body(p_ref, cos_ref, sin_ref, d_ref, qd_ref, kd_ref, bd_ref, gng_ref, y_ref, st_ref, state_s):
        i = pl.program_id(1)

        @pl.when(i == 0)
        def _():
            state_s[...] = jnp.zeros_like(state_s)
        c, s = cos_ref[...], sin_ref[...]
        for h in range(RET_HEADS):
            st = state_s[h]
            stb = st.astype(BF16)
            st_ref[h] = stb
            q, k, qb, kb, v, scb, o = _ret_block_fwd(p_ref, h, c, s, d_ref, qd_ref, kd_ref, stb)
            kdk = (k * kd_ref[h]).astype(BF16)
            state_s[h] = st * bd_ref[h][:, :1] + _dot_tn(kdk, v)
            gate = p_ref[:, 4096 + h * RET_DV:4096 + (h + 1) * RET_DV].astype(F32)
            mu = jnp.mean(o, axis=-1, keepdims=True)
            oc = o - mu
            xh = oc * lax.rsqrt(jnp.mean(oc * oc, axis=-1, keepdims=True) + EPS)
            y = (gate * _sigmoid(gate)) * (xh * gng_ref[:, h * RET_DV:(h + 1) * RET_DV])
            y_ref[:, h * RET_DV:(h + 1) * RET_DV] = y.astype(BF16)

    const = lambda b, i: (0, 0, 0)
    return pl.pallas_call(
        body, name="ret_fwd", grid=(B, nb),
        in_specs=[pl.BlockSpec((LB, 6144), lambda b, i: (b * nb + i, 0)),
                  pl.BlockSpec((LB, 128), lambda b, i: (i, 0)), pl.BlockSpec((LB, 128), lambda b, i: (i, 0)),
                  pl.BlockSpec((RET_HEADS, LB, LB), const), pl.BlockSpec((RET_HEADS, LB, 1), const),
                  pl.BlockSpec((RET_HEADS, LB, 1), const), pl.BlockSpec((RET_HEADS, 1, 128), const),
                  pl.BlockSpec((1, 2048), lambda b, i: (0, 0))],
        out_specs=[pl.BlockSpec((LB, 2048), lambda b, i: (b * nb + i, 0)),
                   pl.BlockSpec((None, None, RET_HEADS, RET_DK, RET_DV), lambda b, i: (b, i, 0, 0, 0))],
        out_shape=[jax.ShapeDtypeStruct((T, 2048), BF16), jax.ShapeDtypeStruct((B, nb, RET_HEADS, RET_DK, RET_DV), BF16)],
        scratch_shapes=[pltpu.VMEM((RET_HEADS, RET_DK, RET_DV), F32)], compiler_params=_params(2),
    )(proj, cos, sin, dmat, qd, kd, bd, gn_g.reshape(1, 2048))


def _ret_bwd(proj, dy, states, gn_g, consts, *, B, S):
    LB = RET_BLOCK
    nb = S // LB
    T = B * S
    dmat, qd, kd, bd, cos, sin = consts

    def body(p_ref, dy_ref, st_ref, cos_ref, sin_ref, d_ref, qd_ref, kd_ref, bd_ref, gng_ref, dp_ref, dgn_ref, dstate_s):
        b, i = pl.program_id(0), pl.program_id(1)

        @pl.when(i == 0)
        def _():
            dstate_s[...] = jnp.zeros_like(dstate_s)

        @pl.when((b == 0) & (i == 0))
        def _():
            dgn_ref[...] = jnp.zeros_like(dgn_ref)
        c, s = cos_ref[...], sin_ref[...]
        for h in range(RET_HEADS):
            vs = slice(h * RET_DV, (h + 1) * RET_DV)
            stb = st_ref[h]
            q, k, qb, kb, v, scb, o = _ret_block_fwd(p_ref, h, c, s, d_ref, qd_ref, kd_ref, stb)
            gate = p_ref[:, 4096 + h * RET_DV:4096 + (h + 1) * RET_DV].astype(F32)
            mu = jnp.mean(o, axis=-1, keepdims=True)
            oc = o - mu
            rstd = lax.rsqrt(jnp.mean(oc * oc, axis=-1, keepdims=True) + EPS)
            xh = oc * rstd
            gng = gng_ref[:, vs]
            dyh = dy_ref[:, vs].astype(F32)
            sg = _sigmoid(gate)
            silu = gate * sg
            dgn_ref[:, vs] += jnp.sum(dyh * silu * xh, axis=0, keepdims=True)
            dxh = dyh * silu * gng
            do = rstd * (dxh - jnp.mean(dxh, axis=-1, keepdims=True) - xh * jnp.mean(dxh * xh, axis=-1, keepdims=True))
            dgate = dyh * xh * gng * (sg * (1.0 + gate * (1.0 - sg)))
            dob = do.astype(BF16)
            dsb = (_dot_nt(dob, v) * d_ref[h]).astype(BF16)
            dst = dstate_s[h]
            dstb = dst.astype(BF16)
            kdk = (k * kd_ref[h]).astype(BF16)
            dqr = _dot(dsb, kb) + qd_ref[h] * _dot_nt(dob, stb)
            dkr = _dot_tn(dsb, qb) + kd_ref[h] * _dot_nt(v, dstb)
            dv = _dot_tn(scb, dob) + _dot(kdk, dstb)
            dstate_s[h] = dst * bd_ref[h][:, :1] + _dot_tn((q * qd_ref[h]).astype(BF16), dob)
            dp_ref[:, h * RET_DK:(h + 1) * RET_DK] = _rope_inv(dqr, c, s).astype(BF16)
            dp_ref[:, 1024 + h * RET_DK:1024 + (h + 1) * RET_DK] = (_rope_inv(dkr, c, s) * (RET_DK ** -0.5)).astype(BF16)
            dp_ref[:, 2048 + h * RET_DV:2048 + (h + 1) * RET_DV] = dv.astype(BF16)
            dp_ref[:, 4096 + h * RET_DV:4096 + (h + 1) * RET_DV] = dgate.astype(BF16)

    const = lambda b, i: (0, 0, 0)
    rev = lambda b, i: (b * nb + nb - 1 - i, 0)
    return pl.pallas_call(
        body, name="ret_bwd", grid=(B, nb),
        in_specs=[pl.BlockSpec((LB, 6144), rev), pl.BlockSpec((LB, 2048), rev),
                  pl.BlockSpec((None, None, RET_HEADS, RET_DK, RET_DV), lambda b, i: (b, nb - 1 - i, 0, 0, 0)),
                  pl.BlockSpec((LB, 128), lambda b, i: (nb - 1 - i, 0)), pl.BlockSpec((LB, 128), lambda b, i: (nb - 1 - i, 0)),
                  pl.BlockSpec((RET_HEADS, LB, LB), const), pl.BlockSpec((RET_HEADS, LB, 1), const),
                  pl.BlockSpec((RET_HEADS, LB, 1), const), pl.BlockSpec((RET_HEADS, 1, 128), const),
                  pl.BlockSpec((1, 2048), lambda b, i: (0, 0))],
        out_specs=[pl.BlockSpec((LB, 6144), rev), pl.BlockSpec((1, 2048), lambda b, i: (0, 0))],
        out_shape=[jax.ShapeDtypeStruct((T, 6144), BF16), jax.ShapeDtypeStruct((1, 2048), F32)],
        scratch_shapes=[pltpu.VMEM((RET_HEADS, RET_DK, RET_DV), F32)], compiler_params=_params(2),
    )(proj, dy, states, cos, sin, dmat, qd, kd, bd, gn_g.reshape(1, 2048))


BIAS_LANES = 4 * ATT_BLOCK


def _diag_onehot():
    r = lax.broadcasted_iota(jnp.int32, (REL_TABLE, BIAS_LANES), 0)
    j = lax.broadcasted_iota(jnp.int32, (REL_TABLE, BIAS_LANES), 1)
    idx = jnp.maximum(j - ATT_BLOCK - PAST, -MAX_REL) + MAX_REL
    return jnp.where(idx == r, 1.0, 0.0).astype(F32)


def _row_is(j):
    return lax.broadcasted_iota(jnp.int32, (8, BIAS_LANES), 0) == j


def _att_bias(table):
    QB, KW = ATT_BLOCK, 3 * ATT_BLOCK

    def body(t_ref, b_ref, bt_ref):
        row = jnp.broadcast_to(t_ref[...], (8, REL_TABLE))
        diag = jnp.dot(row, _diag_onehot(), preferred_element_type=F32, precision=lax.Precision.HIGHEST)
        rows = jnp.zeros((8, BIAS_LANES), F32)
        for j in range(8):
            rows = jnp.where(_row_is(j), diag if j == 0 else pltpu.roll(diag, j, axis=1), rows)
        n = 8
        while n < QB:
            rows = jnp.concatenate([rows, pltpu.roll(rows, n, axis=1)], axis=0)
            n *= 2
        bias = rows[:, QB:]
        qi = lax.broadcasted_iota(jnp.int32, (QB, KW), 0)
        kj = lax.broadcasted_iota(jnp.int32, (QB, KW), 1)
        lo = (qi // CHUNK) * CHUNK
        bias = jnp.where((kj >= lo) & (kj < lo + PAST + CHUNK), bias, NEG)
        b_ref[...] = bias
        bt_ref[...] = bias.T

    return pl.pallas_call(
        body, name="att_bias", grid=(ATT_HEADS,),
        in_specs=[pl.BlockSpec((None, 1, REL_TABLE), lambda h: (h, 0, 0))],
        out_specs=[pl.BlockSpec((None, QB, KW), lambda h: (h, 0, 0)), pl.BlockSpec((None, KW, QB), lambda h: (h, 0, 0))],
        out_shape=[jax.ShapeDtypeStruct((ATT_HEADS, QB, KW), F32), jax.ShapeDtypeStruct((ATT_HEADS, KW, QB), F32)],
        compiler_params=_params(1),
    )(table.reshape(ATT_HEADS, 1, REL_TABLE))


def _att_bias_grad(dbias_t):
    QB, KW = ATT_BLOCK, 3 * ATT_BLOCK

    def body(d_ref, o_ref):
        rows = jnp.concatenate([jnp.zeros((QB, QB), F32), d_ref[...].T], axis=1)
        n = QB // 2
        while n >= 8:
            rows = rows[:n] + pltpu.roll(rows[n:], BIAS_LANES - n, axis=1)
            n //= 2
        acc = jnp.zeros((8, BIAS_LANES), F32)
        for j in range(8):
            acc = acc + jnp.where(_row_is(j), rows if j == 0 else pltpu.roll(rows, BIAS_LANES - j, axis=1), 0.0)
        diag = jnp.broadcast_to(jnp.sum(acc, axis=0, keepdims=True), (8, BIAS_LANES))
        grad = lax.dot_general(diag, _diag_onehot(), (((1,), (1,)), ((), ())), preferred_element_type=F32,
                               precision=lax.Precision.HIGHEST)
        o_ref[...] = grad[:1]

    return pl.pallas_call(
        body, name="att_bias_grad", grid=(ATT_HEADS,),
        in_specs=[pl.BlockSpec((None, KW, QB), lambda h: (h, 0, 0))],
        out_specs=pl.BlockSpec((None, 1, REL_TABLE), lambda h: (h, 0, 0)),
        out_shape=jax.ShapeDtypeStruct((ATT_HEADS, 1, REL_TABLE), F32), compiler_params=_params(1),
    )(dbias_t).reshape(ATT_HEADS, REL_TABLE)


def _att_fwd(qkv, bias, *, B, S):
    QB = ATT_BLOCK
    nb = S // QB
    KW = 3 * QB
    T = B * S
    scale = ATT_DH ** -0.5

    def body(q_ref, k0, k1, k2, v0, v1, v2, b_ref, o_ref):
        i = pl.program_id(2)
        k3 = jnp.concatenate([k0[...], k1[...], k2[...]], axis=0)
        v3 = jnp.concatenate([v0[...], v1[...], v2[...]], axis=0)
        col = lax.broadcasted_iota(jnp.int32, (QB, KW), 1)
        in_seq = col >= (2 - i) * QB
        q = q_ref[...]
        outs = []
        for hh in range(2):
            hs = slice(hh * ATT_DH, (hh + 1) * ATT_DH)
            s = _dot_nt(q[:, hs], k3[:, hs]) * scale + b_ref[hh]
            s = jnp.where(in_seq, s, NEG)
            p = jnp.exp(s - jnp.max(s, axis=-1, keepdims=True))
            p = p / jnp.sum(p, axis=-1, keepdims=True)
            outs.append(_dot(p.astype(BF16), v3[:, hs]))
        o_ref[...] = jnp.concatenate(outs, axis=1).astype(BF16)

    def kv(d, col0):
        return pl.BlockSpec((QB, 128), lambda hp, b, i: (b * nb + jnp.maximum(i - d, 0), col0 + hp))

    return pl.pallas_call(
        body, name="att_fwd", grid=(8, B, nb),
        in_specs=[pl.BlockSpec((QB, 128), lambda hp, b, i: (b * nb + i, hp)),
                  kv(2, 8), kv(1, 8), kv(0, 8), kv(2, 16), kv(1, 16), kv(0, 16),
                  pl.BlockSpec((2, QB, KW), lambda hp, b, i: (hp, 0, 0))],
        out_specs=pl.BlockSpec((QB, 128), lambda hp, b, i: (b * nb + i, hp)),
        out_shape=jax.ShapeDtypeStruct((T, 1024), BF16), compiler_params=_params(3),
    )(qkv, qkv, qkv, qkv, qkv, qkv, qkv, bias)


def _att_bwd(qkv, do, bias_t, *, B, S):
    QB = ATT_BLOCK
    nb = S // QB
    KW = 3 * QB
    T = B * S
    scale = ATT_DH ** -0.5

    def body(q_ref, k0, k1, k2, v0, v1, v2, do_ref, b_ref, dq_ref, dk_ref, dv_ref, db_ref, dk_acc, dv_acc):
        b, i = pl.program_id(1), pl.program_id(2)

        @pl.when(i == 0)
        def _():
            dk_acc[...] = jnp.zeros_like(dk_acc)
            dv_acc[...] = jnp.zeros_like(dv_acc)

        @pl.when((b == 0) & (i == 0))
        def _():
            db_ref[...] = jnp.zeros_like(db_ref)

        @pl.when(i < nb)
        def _():
            k3 = jnp.concatenate([k0[...], k1[...], k2[...]], axis=0)
            v3 = jnp.concatenate([v0[...], v1[...], v2[...]], axis=0)
            k3t = k3.astype(F32).T.astype(BF16)
            q, dout = q_ref[...], do_ref[...]
            row = lax.broadcasted_iota(jnp.int32, (KW, QB), 0)
            in_seq = row >= (2 - i) * QB
            dqt, dk3, dv3 = [], [], []
            for hh in range(2):
                hs = slice(hh * ATT_DH, (hh + 1) * ATT_DH)
                st = _dot_nt(k3[:, hs], q[:, hs]) * scale + b_ref[hh]
                st = jnp.where(in_seq, st, NEG)
                p = jnp.exp(st - jnp.max(st, axis=0, keepdims=True))
                p = p / jnp.sum(p, axis=0, keepdims=True)
                dp = _dot_nt(v3[:, hs], dout[:, hs])
                ds = p * (dp - jnp.sum(p * dp, axis=0, keepdims=True))
                db_ref[hh] += ds
                dsb = (ds * scale).astype(BF16)
                dk3.append(_dot(dsb, q[:, hs]))
                dv3.append(_dot(p.astype(BF16), dout[:, hs]))
                dqt.append(_dot(k3t[hs, :], dsb))
            dq_ref[...] = jnp.concatenate(dqt, axis=0).T.astype(BF16)
            dk3 = jnp.concatenate(dk3, axis=1)
            dv3 = jnp.concatenate(dv3, axis=1)
            for d in range(3):
                slot = (i + 1 + d) % 3
                dk_acc[slot] += dk3[d * QB:(d + 1) * QB]
                dv_acc[slot] += dv3[d * QB:(d + 1) * QB]

        @pl.when(i >= 2)
        def _():
            slot = (i + 1) % 3
            dk_ref[...] = dk_acc[slot].astype(BF16)
            dv_ref[...] = dv_acc[slot].astype(BF16)
            dk_acc[slot] = jnp.zeros((QB, 128), F32)
            dv_acc[slot] = jnp.zeros((QB, 128), F32)

    def qrow(b, i):
        return b * nb + jnp.minimum(i, nb - 1)

    def kv(d, col0):
        return pl.BlockSpec((QB, 128), lambda hp, b, i: (b * nb + jnp.maximum(jnp.minimum(i, nb - 1) - d, 0), col0 + hp))

    late = pl.BlockSpec((QB, 128), lambda hp, b, i: (b * nb + jnp.maximum(i - 2, 0), hp))
    return pl.pallas_call(
        body, name="att_bwd", grid=(8, B, nb + 2),
        in_specs=[pl.BlockSpec((QB, 128), lambda hp, b, i: (qrow(b, i), hp)),
                  kv(2, 8), kv(1, 8), kv(0, 8), kv(2, 16), kv(1, 16), kv(0, 16),
                  pl.BlockSpec((QB, 128), lambda hp, b, i: (qrow(b, i), hp)),
                  pl.BlockSpec((2, KW, QB), lambda hp, b, i: (hp, 0, 0))],
        out_specs=[pl.BlockSpec((QB, 128), lambda hp, b, i: (qrow(b, i), hp)), late, late,
                   pl.BlockSpec((2, KW, QB), lambda hp, b, i: (hp, 0, 0))],
        out_shape=[jax.ShapeDtypeStruct((T, 1024), BF16)] * 3 + [jax.ShapeDtypeStruct((ATT_HEADS, KW, QB), F32)],
        scratch_shapes=[pltpu.VMEM((3, QB, 128), F32), pltpu.VMEM((3, QB, 128), F32)], compiler_params=_params(3),
    )(qkv, qkv, qkv, qkv, qkv, qkv, qkv, do, bias_t)


def _loss_head(h, tgt, g, *, tm):
    T, D = h.shape
    n = T // tm

    def body(h_ref, t_ref, g_ref, dh_ref, dg_ref, loss_ref, acc_ref):
        i = pl.program_id(0)
        x = h_ref[...]
        r = lax.rsqrt(jnp.mean(x * x, axis=-1, keepdims=True) + EPS)
        xh = x * r
        gg = g_ref[...]
        diff = xh * gg - t_ref[...]
        sq = jnp.sum(diff * diff, axis=0, keepdims=True)
        dy = diff * (1.0 / D)
        dg_part = jnp.sum(dy * xh, axis=0, keepdims=True)

        @pl.when(i == 0)
        def _():
            acc_ref[...] = sq
            dg_ref[...] = dg_part

        @pl.when(i > 0)
        def _():
            acc_ref[...] += sq
            dg_ref[...] += dg_part
        t = dy * gg
        dh_ref[...] = r * (t - xh * jnp.mean(t * xh, axis=-1, keepdims=True))

        @pl.when(i == n - 1)
        def _():
            loss_ref[...] = (0.5 / D) * jnp.sum(acc_ref[...], axis=1, keepdims=True)

    return pl.pallas_call(
        body, name="loss_head", grid=(n,),
        in_specs=[pl.BlockSpec((tm, D), lambda i: (i, 0)), pl.BlockSpec((tm, D), lambda i: (i, 0)),
                  pl.BlockSpec((1, D), lambda i: (0, 0))],
        out_specs=[pl.BlockSpec((tm, D), lambda i: (i, 0)), pl.BlockSpec((1, D), lambda i: (0, 0)),
                   pl.BlockSpec((1, 1), lambda i: (0, 0))],
        out_shape=[jax.ShapeDtypeStruct((T, D), F32), jax.ShapeDtypeStruct((1, D), F32), jax.ShapeDtypeStruct((1, 1), F32)],
        scratch_shapes=[pltpu.VMEM((1, D), F32)], compiler_params=_params(1),
    )(h, tgt, g.reshape(1, D))


def _tok_tile(T, want):
    t = min(T, want)
    assert T % t == 0
    return t


def _local_step(x, tgt, w, mix_g, gn_g, rel_bias, mlp_g, fin_g):
    B, S, D = x.shape
    T = B * S
    h0 = x.reshape(T, D)
    tgt = tgt.reshape(T, D)
    tm = _tok_tile(T, 1024)
    tb = _tok_tile(T, 512)
    consts = _ret_consts(S, RET_BLOCK)

    proj, hn0 = _mm_nn(h0, w["ret_w_in"], "col", tm=tm, tn=768, tk=D, out_dtype=BF16, name="ret_in", norm_g=mix_g[0])
    y_ret, states = _ret_fwd(proj, gn_g, consts, B=B, S=S)
    h1 = _mm_nn(y_ret, w["ret_w_out"], "row", tm=tm, tn=D, tk=512, out_dtype=F32, name="ret_out", res=h0)
    a0, hm0 = _mm_nn(h1, w["mlp_w1_0"], "col", tm=tm, tn=1024, tk=D, out_dtype=BF16, name="mlp0_up", norm_g=mlp_g[0], act="relu2")
    h2 = _mm_nn(a0, w["mlp_w2_0"], "row", tm=tm, tn=D, tk=1024, out_dtype=F32, name="mlp0_down", res=h1)
    bias, bias_t = _att_bias(rel_bias)
    qkv, hn1 = _mm_nn(h2, w["att_w_in"], "col", tm=tm, tn=768, tk=D, out_dtype=BF16, name="att_in", norm_g=mix_g[1])
    o_att = _att_fwd(qkv, bias, B=B, S=S)
    h3 = _mm_nn(o_att, w["att_w_out"], "row", tm=tm, tn=D, tk=256, out_dtype=F32, name="att_out", res=h2)
    a1, hm1 = _mm_nn(h3, w["mlp_w1_1"], "col", tm=tm, tn=1024, tk=D, out_dtype=BF16, name="mlp1_up", norm_g=mlp_g[1], act="relu2")
    h4 = _mm_nn(a1, w["mlp_w2_1"], "row", tm=tm, tn=D, tk=1024, out_dtype=F32, name="mlp1_down", res=h3)
    dh4, d_fin_g, loss = _loss_head(h4, tgt, fin_g, tm=tb)

    gw = {}
    gw["mlp_w2_1"] = _mm_tn(a1, dh4, "row", tt=tb, tk=1024, tn=D, name="d_mlp1_w2")
    dz1 = _mm_nt(dh4, w["mlp_w2_1"], "row", tm=tb, tn=1024, tk=D, name="d_mlp1_act", epi="sqrt2", a2=a1)
    gw["mlp_w1_1"] = _mm_tn(hm1, dz1, "col", tt=tb, tk=D, tn=1024, name="d_mlp1_w1")
    dh3, d_mlp_g1 = _mm_nt(dz1, w["mlp_w1_1"], "col", tm=tb, tn=D, tk=1024, name="d_mlp1_in", epi="normbwd",
                           h=h3, g=mlp_g[1], dres=dh4)
    gw["att_w_out"] = _mm_tn(o_att, dh3, "row", tt=tb, tk=256, tn=D, name="d_att_wout")
    do_att = _mm_nt(dh3, w["att_w_out"], "row", tm=tb, tn=256, tk=D, name="d_att_o", epi="bf16")
    dq, dk, dv, dbias_t = _att_bwd(qkv, do_att, bias_t, B=B, S=S)
    d_rel = _att_bias_grad(dbias_t)
    dqkv = jnp.concatenate([dq, dk, dv], axis=1)
    gw["att_w_in"] = _mm_tn(hn1, dqkv, "col", tt=tb, tk=D, tn=768, name="d_att_win")
    dh2, d_mix_g1 = _mm_nt(dqkv, w["att_w_in"], "col", tm=tb, tn=D, tk=768, name="d_att_in", epi="normbwd",
                           h=h2, g=mix_g[1], dres=dh3)
    gw["mlp_w2_0"] = _mm_tn(a0, dh2, "row", tt=tb, tk=1024, tn=D, name="d_mlp0_w2")
    dz0 = _mm_nt(dh2, w["mlp_w2_0"], "row", tm=tb, tn=1024, tk=D, name="d_mlp0_act", epi="sqrt2", a2=a0)
    gw["mlp_w1_0"] = _mm_tn(hm0, dz0, "col", tt=tb, tk=D, tn=1024, name="d_mlp0_w1")
    dh1, d_mlp_g0 = _mm_nt(dz0, w["mlp_w1_0"], "col", tm=tb, tn=D, tk=1024, name="d_mlp0_in", epi="normbwd",
                           h=h1, g=mlp_g[0], dres=dh2)
    gw["ret_w_out"] = _mm_tn(y_ret, dh1, "row", tt=tb, tk=512, tn=D, name="d_ret_wout")
    dy_ret = _mm_nt(dh1, w["ret_w_out"], "row", tm=tb, tn=512, tk=D, name="d_ret_y", epi="bf16")
    dproj, d_gn = _ret_bwd(proj, dy_ret, states, gn_g, consts, B=B, S=S)
    gw["ret_w_in"] = _mm_tn(hn0, dproj, "col", tt=tb, tk=D, tn=768, name="d_ret_win")
    dx, d_mix_g0 = _mm_nt(dproj, w["ret_w_in"], "col", tm=tb, tn=D, tk=768, name="d_ret_in", epi="normbwd",
                          h=h0, g=mix_g[0], dres=dh1)
    small = [d_mix_g0, d_mix_g1, d_mlp_g0, d_mlp_g1, d_fin_g, d_gn.reshape(2, D), d_rel.reshape(5, D)]
    return loss.reshape(()), dx.reshape(B, S, D), gw, small


def _row_tile(r, want=256):
    t = min(r, want)
    assert r % t == 0
    return t


def _into_slab(place, a, dtype, name):
    r, c = a.shape
    tr = _row_tile(r)

    def body(place_ref, a_ref, o_ref):
        o_ref[...] = a_ref[...].astype(dtype)

    grid_spec = pltpu.PrefetchScalarGridSpec(
        num_scalar_prefetch=1, grid=(r // tr,), in_specs=[pl.BlockSpec((tr, c), lambda i, pr: (i, 0))],
        out_specs=pl.BlockSpec((None, tr, c), lambda i, pr: (pr[0], i, 0)),
    )
    return pl.pallas_call(
        body, name=name, grid_spec=grid_spec, out_shape=jax.ShapeDtypeStruct((N_CHIPS, r, c), dtype), compiler_params=_params(1),
    )(place, a)


def _add_sibling(place, g, recv, name):
    _, r, c = g.shape
    hr = r // 2
    tr = _row_tile(hr)
    nrt = hr // tr

    def body(place_ref, g_ref, r_ref, sb_ref, own_ref):
        v = g_ref[...] + r_ref[...]
        sb_ref[...] = v.astype(BF16)

        @pl.when(pl.program_id(1) == place_ref[0])
        def _():
            own_ref[...] = v

    grid_spec = pltpu.PrefetchScalarGridSpec(
        num_scalar_prefetch=1, grid=(nrt, N_CHIPS),
        in_specs=[pl.BlockSpec((None, tr, c), lambda i, s, pr: (s, pr[1] * nrt + i, 0)),
                  pl.BlockSpec((None, tr, c), lambda i, s, pr: (s, i, 0))],
        out_specs=[pl.BlockSpec((None, tr, c), lambda i, s, pr: (s, i, 0)), pl.BlockSpec((tr, c), lambda i, s, pr: (i, 0))],
    )
    return pl.pallas_call(
        body, name=name, grid_spec=grid_spec,
        out_shape=[jax.ShapeDtypeStruct((N_CHIPS, hr, c), BF16), jax.ShapeDtypeStruct((hr, c), F32)],
        compiler_params=_params(2),
    )(place, g, recv)


def _add_chips(place, own, recv, name):
    hr, c = own.shape
    tr = _row_tile(hr)
    nrt = hr // tr

    def body(place_ref, o_ref, r_ref, t_ref):
        t_ref[...] = ((o_ref[...] + r_ref[0].astype(F32)) + r_ref[1].astype(F32)) + r_ref[2].astype(F32)

    grid_spec = pltpu.PrefetchScalarGridSpec(
        num_scalar_prefetch=1, grid=(nrt,),
        in_specs=[pl.BlockSpec((tr, c), lambda i, pr: (i, 0)), pl.BlockSpec((3, tr, c), lambda i, pr: (0, i, 0))],
        out_specs=pl.BlockSpec((tr, c), lambda i, pr: (pr[1] * nrt + i, 0)),
    )
    return pl.pallas_call(
        body, name=name, grid_spec=grid_spec, out_shape=jax.ShapeDtypeStruct((2 * hr, c), F32), compiler_params=_params(1),
    )(place, own, recv)


def _adamw(w, g, m, v, name):
    r, c = w.shape
    tr = _row_tile(r)

    def body(w_ref, g_ref, m_ref, v_ref, d_ref, nm_ref, nv_ref):
        gg = g_ref[...]
        nm = ADAM_B1 * m_ref[...] + (1.0 - ADAM_B1) * gg
        nv = ADAM_B2 * v_ref[...] + (1.0 - ADAM_B2) * (gg * gg)
        m_hat = nm / (1.0 - ADAM_B1 ** ADAM_STEP)
        v_hat = nv / (1.0 - ADAM_B2 ** ADAM_STEP)
        d_ref[...] = -ADAM_LR * (m_hat / (jnp.sqrt(v_hat) + ADAM_EPS) + ADAM_WD * w_ref[...])
        nm_ref[...] = nm
        nv_ref[...] = nv

    spec = pl.BlockSpec((tr, c), lambda i: (i, 0))
    return pl.pallas_call(
        body, name=name, grid=(r // tr,), in_specs=[spec] * 4, out_specs=[spec] * 3,
        out_shape=[jax.ShapeDtypeStruct((r, c), F32)] * 3, compiler_params=_params(1),
    )(w, g, m, v)


HBM_SPEC = pl.BlockSpec(memory_space=pltpu.HBM)


def _place():
    return lax.axis_index("x"), lax.axis_index("y"), lax.axis_index("c")


def _other_chips(x, y):
    return [(1 - x, y), (x, 1 - y), (1 - x, 1 - y)]


def _remote(src, dst, ssem, rsem, dev):
    return pltpu.make_async_remote_copy(src_ref=src, dst_ref=dst, send_sem=ssem, recv_sem=rsem, device_id=dev,
                                        device_id_type=MESH)


def _all_gather_slabs(slabs):
    n = len(slabs)

    def body(*refs):
        outs = refs[n:2 * n]
        s1, r1, s2, r2 = refs[2 * n:]
        x, y, c = _place()
        p = 2 * x + y
        chips = _other_chips(x, y)
        cidx = [2 * qx + qy for qx, qy in chips]
        pending = []
        for t in range(n):
            hr = outs[t].shape[1] // 2
            mine = pl.ds(c * hr, hr)
            for j, (qx, qy) in enumerate(chips):
                cp = _remote(outs[t].at[p, mine], outs[t].at[p, mine], s1.at[t, j], r1.at[t, j], (qx, qy, c))
                cp.start()
                pending.append(cp)
        for t in range(n):
            hr = outs[t].shape[1] // 2
            mine = pl.ds(c * hr, hr)
            for j, (qx, qy) in enumerate(chips):
                got = outs[t].at[cidx[j], mine]
                _remote(got, got, s1.at[t, j], r1.at[t, j], (qx, qy, c)).wait_recv()
                fw = _remote(got, got, s2.at[t, j], r2.at[t, j], (x, y, 1 - c))
                fw.start()
                pending.append(fw)
        for t in range(n):
            hr = outs[t].shape[1] // 2
            theirs = pl.ds((1 - c) * hr, hr)
            for j in range(3):
                got = outs[t].at[cidx[j], theirs]
                _remote(got, got, s2.at[t, j], r2.at[t, j], (x, y, 1 - c)).wait_recv()
        for cp in pending:
            cp.wait_send()

    sem = pltpu.SemaphoreType.DMA
    return pl.pallas_call(
        body, name="gather_weights", in_specs=[HBM_SPEC] * n, out_specs=[HBM_SPEC] * n,
        out_shape=[jax.ShapeDtypeStruct(s.shape, s.dtype) for s in slabs], input_output_aliases={t: t for t in range(n)},
        scratch_shapes=[sem((n, 3)), sem((n, 3)), sem((n, 3)), sem((n, 3))],
        compiler_params=pltpu.CompilerParams(has_side_effects=True),
    )(*slabs)


def _exchange_with_sibling(grads):
    n = len(grads)

    def body(*refs):
        ins, outs = refs[:n], refs[n:2 * n]
        ssem, rsem = refs[2 * n:]
        x, y, c = _place()
        copies = []
        for t in range(n):
            hr = ins[t].shape[1] // 2
            cp = _remote(ins[t].at[:, pl.ds((1 - c) * hr, hr), :], outs[t], ssem.at[t], rsem.at[t], (x, y, 1 - c))
            cp.start()
            copies.append(cp)
        for cp in copies:
            cp.wait()

    sem = pltpu.SemaphoreType.DMA
    return pl.pallas_call(
        body, name="grads_to_sibling", in_specs=[HBM_SPEC] * n, out_specs=[HBM_SPEC] * n,
        out_shape=[jax.ShapeDtypeStruct((N_CHIPS, g.shape[1] // 2, g.shape[2]), g.dtype) for g in grads],
        scratch_shapes=[sem((n,)), sem((n,))], compiler_params=pltpu.CompilerParams(has_side_effects=True),
    )(*grads)


def _exchange_with_chips(sums):
    n = len(sums)

    def body(*refs):
        ins, outs = refs[:n], refs[n:2 * n]
        ssem, rsem = refs[2 * n:]
        x, y, c = _place()
        chips = _other_chips(x, y)
        copies = []
        for t in range(n):
            for j, (qx, qy) in enumerate(chips):
                cp = _remote(ins[t].at[2 * qx + qy], outs[t].at[j], ssem.at[t, j], rsem.at[t, j], (qx, qy, c))
                cp.start()
                copies.append(cp)
        for cp in copies:
            cp.wait()

    sem = pltpu.SemaphoreType.DMA
    return pl.pallas_call(
        body, name="grads_to_chips", in_specs=[HBM_SPEC] * n, out_specs=[HBM_SPEC] * n,
        out_shape=[jax.ShapeDtypeStruct((3,) + s.shape[1:], s.dtype) for s in sums],
        scratch_shapes=[sem((n, 3)), sem((n, 3))], compiler_params=pltpu.CompilerParams(has_side_effects=True),
    )(*sums)


def _share_with_sibling(shards):
    n = len(shards)

    def body(*refs):
        outs = refs[n:2 * n]
        ssem, rsem = refs[2 * n:]
        x, y, c = _place()
        copies = []
        for t in range(n):
            hr = outs[t].shape[0] // 2
            mine = outs[t].at[pl.ds(c * hr, hr)]
            cp = _remote(mine, mine, ssem.at[t], rsem.at[t], (x, y, 1 - c))
            cp.start()
            copies.append(cp)
        for t in range(n):
            hr = outs[t].shape[0] // 2
            theirs = outs[t].at[pl.ds((1 - c) * hr, hr)]
            _remote(theirs, theirs, ssem.at[t], rsem.at[t], (x, y, 1 - c)).wait_recv()
        for cp in copies:
            cp.wait_send()

    sem = pltpu.SemaphoreType.DMA
    return pl.pallas_call(
        body, name="grads_share", in_specs=[HBM_SPEC] * n, out_specs=[HBM_SPEC] * n,
        out_shape=[jax.ShapeDtypeStruct(s.shape, s.dtype) for s in shards], input_output_aliases={t: t for t in range(n)},
        scratch_shapes=[sem((n,)), sem((n,))], compiler_params=pltpu.CompilerParams(has_side_effects=True),
    )(*shards)


def _all_reduce_small(buf):
    R, C = buf.shape

    def body(in_ref, out_ref, gather, ssem, rsem):
        x, y, c = _place()
        me = 4 * x + 2 * y + c
        gather[me] = in_ref[...]
        flips = [(fx, fy, fc) for fx in (0, 1) for fy in (0, 1) for fc in (0, 1) if fx or fy or fc]
        peers = [(x + fx - 2 * x * fx, y + fy - 2 * y * fy, c + fc - 2 * c * fc) for fx, fy, fc in flips]
        copies = [_remote(in_ref, gather.at[me], ssem.at[k], rsem.at[k], peer) for k, peer in enumerate(peers)]
        for cp in copies:
            cp.start()
        for k, (px, py, pc) in enumerate(peers):
            _remote(in_ref, gather.at[4 * px + 2 * py + pc], ssem.at[k], rsem.at[k], (px, py, pc)).wait_recv()
        for cp in copies:
            cp.wait_send()
        acc = gather[0]
        for d in range(1, 8):
            acc = acc + gather[d]
        out_ref[...] = acc

    sem = pltpu.SemaphoreType.DMA
    vmem = pl.BlockSpec(memory_space=pltpu.VMEM)
    return pl.pallas_call(
        body, name="small_grads_sum", in_specs=[vmem], out_specs=vmem, out_shape=jax.ShapeDtypeStruct((R, C), F32),
        scratch_shapes=[pltpu.VMEM((8, R, C), F32), sem((7,)), sem((7,))],
        compiler_params=pltpu.CompilerParams(has_side_effects=True),
    )(buf)


BIG = ["ret_w_in", "ret_w_out", "att_w_in", "att_w_out", "mlp_w1_0", "mlp_w1_1", "mlp_w2_0", "mlp_w2_1"]


def _split_big(ret_w_in, ret_w_out, att_w_in, att_w_out, mlp_w1, mlp_w2):
    return {"ret_w_in": ret_w_in[0], "ret_w_out": ret_w_out[0], "att_w_in": att_w_in[0], "att_w_out": att_w_out[0],
            "mlp_w1_0": mlp_w1[0], "mlp_w1_1": mlp_w1[1], "mlp_w2_0": mlp_w2[0], "mlp_w2_1": mlp_w2[1]}


def kernel(x, mix_norm_g, ret_w_in, ret_gn_g, ret_w_out, att_w_in, att_rel_bias, att_w_out, mlp_norm_g, mlp_w1, mlp_w2, final_norm_g, loss_target, m_mix_norm_g, m_ret_w_in, m_ret_gn_g, m_ret_w_out, m_att_w_in, m_att_rel_bias, m_att_w_out, m_mlp_norm_g, m_mlp_w1, m_mlp_w2, m_final_norm_g, v_mix_norm_g, v_ret_w_in, v_ret_gn_g, v_ret_w_out, v_att_w_in, v_att_rel_bias, v_att_w_out, v_mlp_norm_g, v_mlp_w1, v_mlp_w2, v_final_norm_g):
    xi, yi, ci = _place()
    chip = 2 * xi + yi
    w32 = _split_big(ret_w_in, ret_w_out, att_w_in, att_w_out, mlp_w1, mlp_w2)
    m32 = _split_big(m_ret_w_in, m_ret_w_out, m_att_w_in, m_att_w_out, m_mlp_w1, m_mlp_w2)
    v32 = _split_big(v_ret_w_in, v_ret_w_out, v_att_w_in, v_att_w_out, v_mlp_w1, v_mlp_w2)

    place = jnp.stack([chip, ci]).astype(jnp.int32)
    gathered = _all_gather_slabs([_into_slab(place, w32[n], BF16, name="cast_" + n) for n in BIG]
                                 + [_into_slab(place, att_rel_bias[0], F32, name="slab_rel_bias")])
    w = dict(zip(BIG, gathered[:-1]))
    rel_full = jnp.transpose(gathered[-1], (1, 0, 2)).reshape(ATT_HEADS, REL_TABLE)

    loss_local, grad_x, gw, small = _local_step(x, loss_target, w, mix_norm_g, ret_gn_g[0], rel_full, mlp_norm_g, final_norm_g)
    loss = lax.psum(loss_local, ("x", "y", "c"))

    from_sibling = _exchange_with_sibling([gw[n] for n in BIG])
    sums = [_add_sibling(place, gw[n], r, name="chip_sum_" + n) for n, r in zip(BIG, from_sibling)]
    from_chips = _exchange_with_chips([s[0] for s in sums])
    totals = [_add_chips(place, s[1], r, name="total_" + n) for n, s, r in zip(BIG, sums, from_chips)]
    g_big = dict(zip(BIG, _share_with_sibling(totals)))

    rows, at = jnp.zeros((16, D_MODEL), F32), 0
    for part in small:
        rows = rows + jnp.pad(part, ((at, 16 - at - part.shape[0]), (0, 0)))
        at += part.shape[0]
    rows = _all_reduce_small(rows)
    g_small = {"mix_norm_g": rows[0:2], "mlp_norm_g": rows[2:4], "final_norm_g": rows[4:5], "ret_gn_g": rows[5:7].reshape(1, 2048),
               "att_rel_bias": lax.dynamic_slice_in_dim(rows[7:12].reshape(ATT_HEADS, REL_TABLE), chip * (REL_TABLE // N_CHIPS),
                                                        REL_TABLE // N_CHIPS, axis=1)}

    upd = {n: _adamw(w32[n], g_big[n], m32[n], v32[n], name="adamw_" + n) for n in BIG}
    small_in = {"mix_norm_g": (mix_norm_g, m_mix_norm_g, v_mix_norm_g), "ret_gn_g": (ret_gn_g, m_ret_gn_g, v_ret_gn_g),
                "att_rel_bias": (att_rel_bias[0], m_att_rel_bias[0], v_att_rel_bias[0]),
                "mlp_norm_g": (mlp_norm_g, m_mlp_norm_g, v_mlp_norm_g),
                "final_norm_g": (final_norm_g.reshape(1, -1), m_final_norm_g.reshape(1, -1), v_final_norm_g.reshape(1, -1))}
    upd_small = {n: _adamw(wv, g_small[n], mv, vv, name="adamw_" + n) for n, (wv, mv, vv) in small_in.items()}

    def big_out(k):
        def get(n):
            return g_big[n] if k is None else upd[n][k]
        return {"ret_w_in": get("ret_w_in")[None], "ret_w_out": get("ret_w_out")[None], "att_w_in": get("att_w_in")[None],
                "att_w_out": get("att_w_out")[None], "mlp_w1": jnp.stack([get("mlp_w1_0"), get("mlp_w1_1")]),
                "mlp_w2": jnp.stack([get("mlp_w2_0"), get("mlp_w2_1")])}

    def small_out(k):
        def get(n):
            return g_small[n] if k is None else upd_small[n][k]
        return {"mix_norm_g": get("mix_norm_g"), "ret_gn_g": get("ret_gn_g"), "att_rel_bias": get("att_rel_bias")[None],
                "mlp_norm_g": get("mlp_norm_g"), "final_norm_g": get("final_norm_g").reshape(-1)}

    order = ["mix_norm_g", "ret_w_in", "ret_gn_g", "ret_w_out", "att_w_in", "att_rel_bias", "att_w_out", "mlp_norm_g",
             "mlp_w1", "mlp_w2", "final_norm_g"]
    outs = [loss, grad_x]
    for k in (None, 0, 1, 2):
        both = {**big_out(k), **small_out(k)}
        outs += [both[n] for n in order]
    return tuple(outs)
```

```python
import jax
import jax.numpy as jnp
from jax import lax
from jax.experimental import pallas as pl
from jax.experimental.pallas import tpu as pltpu

F32 = jnp.float32
BF16 = jnp.bfloat16
MESH = pl.DeviceIdType.MESH

D_MODEL = 1024
CHUNK = 64
RET_HEADS = 4
RET_DK = 256
RET_DV = 512
ROPE_BASE = 10000.0
ATT_HEADS = 16
ATT_DH = 64
PAST = 512
MAX_REL = 256
REL_TABLE = MAX_REL + CHUNK
EPS = 1e-6
NEG = -1e30
N_CHIPS = 4

ADAM_LR = 0.001
ADAM_B1 = 0.9
ADAM_B2 = 0.999
ADAM_EPS = 1e-08
ADAM_WD = 0.01
ADAM_STEP = 10

RET_BLOCK = 256
ATT_BLOCK = 256
VMEM_LIMIT = 56 * 1024 * 1024


def _params(n_axes, **kw):
    return pltpu.CompilerParams(dimension_semantics=("arbitrary",) * n_axes, vmem_limit_bytes=VMEM_LIMIT, **kw)


def _dot(a, b):
    return jnp.dot(a, b, preferred_element_type=F32)


def _dot_nt(a, b):
    return lax.dot_general(a, b, (((1,), (1,)), ((), ())), preferred_element_type=F32)


def _dot_tn(a, b):
    return lax.dot_general(a, b, (((0,), (0,)), ((), ())), preferred_element_type=F32)


def _sigmoid(x):
    return 1.0 / (1.0 + jnp.exp(-x))


def _mm_nn(a, w, wkind, *, tm, out_dtype, name, norm_g=None, act=None, res=None):
    M, K = a.shape
    cols = w.shape[2]
    N = N_CHIPS * cols if wkind == "col" else cols
    has_norm = norm_g is not None
    assert M % tm == 0 and K == (w.shape[1] if wkind == "col" else N_CHIPS * w.shape[1])

    def body(*refs):
        it = iter(refs)
        a_ref, w_ref = next(it), next(it)
        g_ref = next(it) if has_norm else None
        r_ref = next(it) if res is not None else None
        o_ref = next(it)
        hn_ref = next(it) if has_norm else None
        if has_norm:
            x = a_ref[...].astype(F32)
            r = lax.rsqrt(jnp.mean(x * x, axis=-1, keepdims=True) + EPS)
            lhs = (x * r * g_ref[...]).astype(BF16)
            hn_ref[...] = lhs
        else:
            lhs = a_ref[...].astype(BF16)

        def finish(acc, sl):
            if act == "relu2":
                u = jnp.maximum(acc, 0.0)
                acc = u * u
            if r_ref is not None:
                acc = acc + r_ref[:, sl]
            o_ref[:, sl] = acc.astype(out_dtype)

        if wkind == "col":
            for s in range(N_CHIPS):
                finish(_dot(lhs, w_ref[s]), slice(s * cols, (s + 1) * cols))
        else:
            finish(_dot(lhs, w_ref[...].reshape(K, N)), slice(None))

    in_specs = [pl.BlockSpec((tm, K), lambda i: (i, 0)), pl.BlockSpec(w.shape, lambda i: (0, 0, 0))]
    args = [a, w]
    if has_norm:
        in_specs.append(pl.BlockSpec((1, K), lambda i: (0, 0)))
        args.append(norm_g.reshape(1, K))
    if res is not None:
        in_specs.append(pl.BlockSpec((tm, N), lambda i: (i, 0)))
        args.append(res)
    out_shape = [jax.ShapeDtypeStruct((M, N), out_dtype)]
    out_specs = [pl.BlockSpec((tm, N), lambda i: (i, 0))]
    if has_norm:
        out_shape.append(jax.ShapeDtypeStruct((M, K), BF16))
        out_specs.append(pl.BlockSpec((tm, K), lambda i: (i, 0)))
    outs = pl.pallas_call(
        body, name=name, grid=(M // tm,), in_specs=in_specs, out_specs=out_specs, out_shape=out_shape, compiler_params=_params(1),
    )(*args)
    return outs if has_norm else outs[0]


def _mm_nt(a, w, wkind, *, tm, name, epi, a2=None, h=None, g=None, dres=None):
    M, Nw = a.shape
    rows, cols = w.shape[1], w.shape[2]
    Kw = rows if wkind == "col" else N_CHIPS * rows
    assert M % tm == 0 and Nw == (N_CHIPS * cols if wkind == "col" else cols)
    assert epi != "normbwd" or wkind == "col"

    def body(*refs):
        it = iter(refs)
        a_ref, w_ref = next(it), next(it)
        a2_ref = next(it) if epi == "sqrt2" else None
        if epi == "normbwd":
            h_ref, g_ref, dres_ref = next(it), next(it), next(it)
        o_ref = next(it)
        dg_ref = next(it) if epi == "normbwd" else None
        i = pl.program_id(0)

        def finish(acc, sl):
            if epi == "bf16":
                o_ref[:, sl] = acc.astype(BF16)
            elif epi == "sqrt2":
                o_ref[:, sl] = (acc * (2.0 * jnp.sqrt(a2_ref[:, sl].astype(F32)))).astype(BF16)
            else:
                x = h_ref[...]
                r = lax.rsqrt(jnp.mean(x * x, axis=-1, keepdims=True) + EPS)
                xh = x * r
                dg_part = jnp.sum(acc * xh, axis=0, keepdims=True)

                @pl.when(i == 0)
                def _():
                    dg_ref[...] = dg_part

                @pl.when(i > 0)
                def _():
                    dg_ref[...] += dg_part
                t = acc * g_ref[...]
                dx = r * (t - xh * jnp.mean(t * xh, axis=-1, keepdims=True))
                o_ref[...] = dres_ref[...] + dx

        if wkind == "col":
            acc = _dot_nt(a_ref[:, :cols].astype(BF16), w_ref[0])
            for s in range(1, N_CHIPS):
                acc += _dot_nt(a_ref[:, s * cols:(s + 1) * cols].astype(BF16), w_ref[s])
            finish(acc, slice(None))
        else:
            lhs = a_ref[...].astype(BF16)
            for s in range(N_CHIPS):
                finish(_dot_nt(lhs, w_ref[s]), slice(s * rows, (s + 1) * rows))

    in_specs = [pl.BlockSpec((tm, Nw), lambda i: (i, 0)), pl.BlockSpec(w.shape, lambda i: (0, 0, 0))]
    args = [a, w]
    out_dtype = BF16
    if epi == "sqrt2":
        in_specs.append(pl.BlockSpec((tm, Kw), lambda i: (i, 0)))
        args.append(a2)
    if epi == "normbwd":
        in_specs += [pl.BlockSpec((tm, Kw), lambda i: (i, 0)), pl.BlockSpec((1, Kw), lambda i: (0, 0)),
                     pl.BlockSpec((tm, Kw), lambda i: (i, 0))]
        args += [h, g.reshape(1, Kw), dres]
        out_dtype = F32
    out_shape = [jax.ShapeDtypeStruct((M, Kw), out_dtype)]
    out_specs = [pl.BlockSpec((tm, Kw), lambda i: (i, 0))]
    if epi == "normbwd":
        out_shape.append(jax.ShapeDtypeStruct((1, Kw), F32))
        out_specs.append(pl.BlockSpec((1, Kw), lambda i: (0, 0)))
    outs = pl.pallas_call(
        body, name=name, grid=(M // tm,), in_specs=in_specs, out_specs=out_specs, out_shape=out_shape, compiler_params=_params(1),
    )(*args)
    return outs if epi == "normbwd" else outs[0]


def _mm_tn(a, b, okind, *, tt, tk, tn, name):
    T, K = a.shape
    N = b.shape[1]
    assert T % tt == 0 and K % tk == 0 and N % tn == 0
    nt = T // tt
    if okind == "col":
        per = (N // N_CHIPS) // tn
        assert (N // N_CHIPS) % tn == 0
        out_shape = jax.ShapeDtypeStruct((N_CHIPS, K, N // N_CHIPS), F32)
        out_spec = pl.BlockSpec((None, tk, tn), lambda ki, nj, t: (nj // per, ki, nj % per))
    else:
        per = (K // N_CHIPS) // tk
        assert (K // N_CHIPS) % tk == 0
        out_shape = jax.ShapeDtypeStruct((N_CHIPS, K // N_CHIPS, N), F32)
        out_spec = pl.BlockSpec((None, tk, tn), lambda ki, nj, t: (ki // per, ki % per, nj))

    def body(a_ref, b_ref, o_ref):
        t = pl.program_id(2)
        part = _dot_tn(a_ref[...].astype(BF16), b_ref[...].astype(BF16))

        @pl.when(t == 0)
        def _():
            o_ref[...] = part

        @pl.when(t > 0)
        def _():
            o_ref[...] += part

    return pl.pallas_call(
        body, name=name, grid=(K // tk, N // tn, nt),
        in_specs=[pl.BlockSpec((tt, tk), lambda ki, nj, t: (t, ki)), pl.BlockSpec((tt, tn), lambda ki, nj, t: (t, nj))],
        out_specs=out_spec, out_shape=out_shape, compiler_params=_params(3),
    )(a, b)


def _ret_consts(S, LB):
    log_gamma = jnp.log1p(-jnp.exp2(-5.0 - jnp.arange(RET_HEADS, dtype=F32)))
    idx = jnp.arange(LB, dtype=F32)
    n, m = idx[:, None], idx[None, :]
    cn, cm = jnp.floor(n / CHUNK), jnp.floor(m / CHUNK)
    dist = jnp.where(cm == cn, jnp.abs(n - m), n - m)
    dmat = jnp.where((cm <= cn)[None], jnp.exp(log_gamma[:, None, None] * dist[None]), 0.0)
    qd = jnp.exp(log_gamma[:, None] * (idx + 1.0)[None, :])[..., None]
    kd = jnp.exp(log_gamma[:, None] * (LB - 1 - idx)[None, :])[..., None]
    bd = jnp.exp(log_gamma * LB).reshape(RET_HEADS, 1, 1) * jnp.ones((RET_HEADS, 1, 128), F32)
    half = RET_DK // 2
    inv = jnp.exp(-jnp.log(ROPE_BASE) * jnp.arange(half, dtype=F32) / half)
    ang = jnp.arange(S, dtype=F32)[:, None] * inv[None, :]
    return dmat.astype(F32), qd.astype(F32), kd.astype(F32), bd, jnp.cos(ang), jnp.sin(ang)


def _rope(t, c, s):
    t1, t2 = t[:, :128], t[:, 128:]
    return jnp.concatenate([t1 * c - t2 * s, t1 * s + t2 * c], axis=-1)


def _rope_inv(d, c, s):
    d1, d2 = d[:, :128], d[:, 128:]
    return jnp.concatenate([d1 * c + d2 * s, d2 * c - d1 * s], axis=-1)


def _ret_block_fwd(p_ref, h, c, s, d_ref, qd_ref, kd_ref, stb):
    q = _rope(p_ref[:, h * RET_DK:(h + 1) * RET_DK].astype(F32), c, s)
    k = _rope(p_ref[:, 1024 + h * RET_DK:1024 + (h + 1) * RET_DK].astype(F32), c, s) * (RET_DK ** -0.5)
    v = p_ref[:, 2048 + h * RET_DV:2048 + (h + 1) * RET_DV]
    qb, kb = q.astype(BF16), k.astype(BF16)
    scb = (_dot_nt(qb, kb) * d_ref[h]).astype(BF16)
    o = _dot(scb, v) + qd_ref[h] * _dot(qb, stb)
    return q, k, qb, kb, v, scb, o


def _ret_fwd(proj, gn_g, consts, *, B, S):
    LB = RET_BLOCK
    nb = S // LB
    T = B * S
    dmat, qd, kd, bd, cos, sin = consts

    def body(p_ref, cos_ref, sin_ref, d_ref, qd_ref, kd_ref, bd_ref, gng_ref, y_ref, st_ref, state_s):
        i = pl.program_id(1)

        @pl.when(i == 0)
        def _():
            state_s[...] = jnp.zeros_like(state_s)
        c, s = cos_ref[...], sin_ref[...]
        for h in range(RET_HEADS):
            st = state_s[h]
            stb = st.astype(BF16)
            st_ref[h] = stb
            q, k, qb, kb, v, scb, o = _ret_block_fwd(p_ref, h, c, s, d_ref, qd_ref, kd_ref, stb)
            kdk = (k * kd_ref[h]).astype(BF16)
            state_s[h] = st * bd_ref[h][:, :1] + _dot_tn(kdk, v)
            gate = p_ref[:, 4096 + h * RET_DV:4096 + (h + 1) * RET_DV].astype(F32)
            mu = jnp.mean(o, axis=-1, keepdims=True)
            oc = o - mu
            xh = oc * lax.rsqrt(jnp.mean(oc * oc, axis=-1, keepdims=True) + EPS)
            y = (gate * _sigmoid(gate)) * (xh * gng_ref[:, h * RET_DV:(h + 1) * RET_DV])
            y_ref[:, h * RET_DV:(h + 1) * RET_DV] = y.astype(BF16)

    const = lambda b, i: (0, 0, 0)
    return pl.pallas_call(
        body, name="ret_fwd", grid=(B, nb),
        in_specs=[pl.BlockSpec((LB, 6144), lambda b, i: (b * nb + i, 0)),
                  pl.BlockSpec((LB, 128), lambda b, i: (i, 0)), pl.BlockSpec((LB, 128), lambda b, i: (i, 0)),
                  pl.BlockSpec((RET_HEADS, LB, LB), const), pl.BlockSpec((RET_HEADS, LB, 1), const),
                  pl.BlockSpec((RET_HEADS, LB, 1), const), pl.BlockSpec((RET_HEADS, 1, 128), const),
                  pl.BlockSpec((1, 2048), lambda b, i: (0, 0))],
        out_specs=[pl.BlockSpec((LB, 2048), lambda b, i: (b * nb + i, 0)),
                   pl.BlockSpec((None, None, RET_HEADS, RET_DK, RET_DV), lambda b, i: (b, i, 0, 0, 0))],
        out_shape=[jax.ShapeDtypeStruct((T, 2048), BF16), jax.ShapeDtypeStruct((B, nb, RET_HEADS, RET_DK, RET_DV), BF16)],
        scratch_shapes=[pltpu.VMEM((RET_HEADS, RET_DK, RET_DV), F32)], compiler_params=_params(2),
    )(proj, cos, sin, dmat, qd, kd, bd, gn_g.reshape(1, 2048))


def _ret_bwd(proj, dy, states, gn_g, consts, *, B, S):
    LB = RET_BLOCK
    nb = S // LB
    T = B * S
    dmat, qd, kd, bd, cos, sin = consts

    def body(p_ref, dy_ref, st_ref, cos_ref, sin_ref, d_ref, qd_ref, kd_ref, bd_ref, gng_ref, dp_ref, dgn_ref, dstate_s):
        b, i = pl.program_id(0), pl.program_id(1)

        @pl.when(i == 0)
        def _():
            dstate_s[...] = jnp.zeros_like(dstate_s)

        @pl.when((b == 0) & (i == 0))
        def _():
            dgn_ref[...] = jnp.zeros_like(dgn_ref)
        c, s = cos_ref[...], sin_ref[...]
        for h in range(RET_HEADS):
            vs = slice(h * RET_DV, (h + 1) * RET_DV)
            stb = st_ref[h]
            q, k, qb, kb, v, scb, o = _ret_block_fwd(p_ref, h, c, s, d_ref, qd_ref, kd_ref, stb)
            gate = p_ref[:, 4096 + h * RET_DV:4096 + (h + 1) * RET_DV].astype(F32)
            mu = jnp.mean(o, axis=-1, keepdims=True)
            oc = o - mu
            rstd = lax.rsqrt(jnp.mean(oc * oc, axis=-1, keepdims=True) + EPS)
            xh = oc * rstd
            gng = gng_ref[:, vs]
            dyh = dy_ref[:, vs].astype(F32)
            sg = _sigmoid(gate)
            silu = gate * sg
            dgn_ref[:, vs] += jnp.sum(dyh * silu * xh, axis=0, keepdims=True)
            dxh = dyh * silu * gng
            do = rstd * (dxh - jnp.mean(dxh, axis=-1, keepdims=True) - xh * jnp.mean(dxh * xh, axis=-1, keepdims=True))
            dgate = dyh * xh * gng * (sg * (1.0 + gate * (1.0 - sg)))
            dob = do.astype(BF16)
            dsb = (_dot_nt(dob, v) * d_ref[h]).astype(BF16)
            dst = dstate_s[h]
            dstb = dst.astype(BF16)
            kdk = (k * kd_ref[h]).astype(BF16)
            dqr = _dot(dsb, kb) + qd_ref[h] * _dot_nt(dob, stb)
            dkr = _dot_tn(dsb, qb) + kd_ref[h] * _dot_nt(v, dstb)
            dv = _dot_tn(scb, dob) + _dot(kdk, dstb)
            dstate_s[h] = dst * bd_ref[h][:, :1] + _dot_tn((q * qd_ref[h]).astype(BF16), dob)
            dp_ref[:, h * RET_DK:(h + 1) * RET_DK] = _rope_inv(dqr, c, s).astype(BF16)
            dp_ref[:, 1024 + h * RET_DK:1024 + (h + 1) * RET_DK] = (_rope_inv(dkr, c, s) * (RET_DK ** -0.5)).astype(BF16)
            dp_ref[:, 2048 + h * RET_DV:2048 + (h + 1) * RET_DV] = dv.astype(BF16)
            dp_ref[:, 4096 + h * RET_DV:4096 + (h + 1) * RET_DV] = dgate.astype(BF16)

    const = lambda b, i: (0, 0, 0)
    rev = lambda b, i: (b * nb + nb - 1 - i, 0)
    return pl.pallas_call(
        body, name="ret_bwd", grid=(B, nb),
        in_specs=[pl.BlockSpec((LB, 6144), rev), pl.BlockSpec((LB, 2048), rev),
                  pl.BlockSpec((None, None, RET_HEADS, RET_DK, RET_DV), lambda b, i: (b, nb - 1 - i, 0, 0, 0)),
                  pl.BlockSpec((LB, 128), lambda b, i: (nb - 1 - i, 0)), pl.BlockSpec((LB, 128), lambda b, i: (nb - 1 - i, 0)),
                  pl.BlockSpec((RET_HEADS, LB, LB), const), pl.BlockSpec((RET_HEADS, LB, 1), const),
                  pl.BlockSpec((RET_HEADS, LB, 1), const), pl.BlockSpec((RET_HEADS, 1, 128), const),
                  pl.BlockSpec((1, 2048), lambda b, i: (0, 0))],
        out_specs=[pl.BlockSpec((LB, 6144), rev), pl.BlockSpec((1, 2048), lambda b, i: (0, 0))],
        out_shape=[jax.ShapeDtypeStruct((T, 6144), BF16), jax.ShapeDtypeStruct((1, 2048), F32)],
        scratch_shapes=[pltpu.VMEM((RET_HEADS, RET_DK, RET_DV), F32)], compiler_params=_params(2),
    )(proj, dy, states, cos, sin, dmat, qd, kd, bd, gn_g.reshape(1, 2048))


BIAS_LANES = 4 * ATT_BLOCK


def _diag_onehot():
    r = lax.broadcasted_iota(jnp.int32, (REL_TABLE, BIAS_LANES), 0)
    j = lax.broadcasted_iota(jnp.int32, (REL_TABLE, BIAS_LANES), 1)
    idx = jnp.maximum(j - ATT_BLOCK - PAST, -MAX_REL) + MAX_REL
    return jnp.where(idx == r, 1.0, 0.0).astype(F32)


def _row_is(j):
    return lax.broadcasted_iota(jnp.int32, (8, BIAS_LANES), 0) == j


def _att_bias(table):
    QB, KW = ATT_BLOCK, 3 * ATT_BLOCK

    def body(t_ref, b_ref, bt_ref):
        row = jnp.broadcast_to(t_ref[...], (8, REL_TABLE))
        diag = jnp.dot(row, _diag_onehot(), preferred_element_type=F32, precision=lax.Precision.HIGHEST)
        rows = jnp.zeros((8, BIAS_LANES), F32)
        for j in range(8):
            rows = jnp.where(_row_is(j), diag if j == 0 else pltpu.roll(diag, j, axis=1), rows)
        n = 8
        while n < QB:
            rows = jnp.concatenate([rows, pltpu.roll(rows, n, axis=1)], axis=0)
            n *= 2
        bias = rows[:, QB:]
        qi = lax.broadcasted_iota(jnp.int32, (QB, KW), 0)
        kj = lax.broadcasted_iota(jnp.int32, (QB, KW), 1)
        lo = (qi // CHUNK) * CHUNK
        bias = jnp.where((kj >= lo) & (kj < lo + PAST + CHUNK), bias, NEG)
        b_ref[...] = bias
        bt_ref[...] = bias.T

    return pl.pallas_call(
        body, name="att_bias", grid=(ATT_HEADS,),
        in_specs=[pl.BlockSpec((None, 1, REL_TABLE), lambda h: (h, 0, 0))],
        out_specs=[pl.BlockSpec((None, QB, KW), lambda h: (h, 0, 0)), pl.BlockSpec((None, KW, QB), lambda h: (h, 0, 0))],
        out_shape=[jax.ShapeDtypeStruct((ATT_HEADS, QB, KW), F32), jax.ShapeDtypeStruct((ATT_HEADS, KW, QB), F32)],
        compiler_params=_params(1),
    )(table.reshape(ATT_HEADS, 1, REL_TABLE))


def _att_bias_grad(dbias_t):
    QB, KW = ATT_BLOCK, 3 * ATT_BLOCK

    def body(d_ref, o_ref):
        rows = jnp.concatenate([jnp.zeros((QB, QB), F32), d_ref[...].T], axis=1)
        n = QB // 2
        while n >= 8:
            rows = rows[:n] + pltpu.roll(rows[n:], BIAS_LANES - n, axis=1)
            n //= 2
        acc = jnp.zeros((8, BIAS_LANES), F32)
        for j in range(8):
            acc = acc + jnp.where(_row_is(j), rows if j == 0 else pltpu.roll(rows, BIAS_LANES - j, axis=1), 0.0)
        diag = jnp.broadcast_to(jnp.sum(acc, axis=0, keepdims=True), (8, BIAS_LANES))
        grad = lax.dot_general(diag, _diag_onehot(), (((1,), (1,)), ((), ())), preferred_element_type=F32,
                               precision=lax.Precision.HIGHEST)
        o_ref[...] = grad[:1]

    return pl.pallas_call(
        body, name="att_bias_grad", grid=(ATT_HEADS,),
        in_specs=[pl.BlockSpec((None, KW, QB), lambda h: (h, 0, 0))],
        out_specs=pl.BlockSpec((None, 1, REL_TABLE), lambda h: (h, 0, 0)),
        out_shape=jax.ShapeDtypeStruct((ATT_HEADS, 1, REL_TABLE), F32), compiler_params=_params(1),
    )(dbias_t).reshape(ATT_HEADS, REL_TABLE)


def _att_fwd(qkv, bias, *, B, S):
    QB = ATT_BLOCK
    nb = S // QB
    KW = 3 * QB
    T = B * S
    scale = ATT_DH ** -0.5

    def body(q_ref, k0, k1, k2, v0, v1, v2, b_ref, o_ref):
        i = pl.program_id(2)
        k3 = jnp.concatenate([k0[...], k1[...], k2[...]], axis=0)
        v3 = jnp.concatenate([v0[...], v1[...], v2[...]], axis=0)
        col = lax.broadcasted_iota(jnp.int32, (QB, KW), 1)
        in_seq = col >= (2 - i) * QB
        q = q_ref[...]
        outs = []
        for hh in range(2):
            hs = slice(hh * ATT_DH, (hh + 1) * ATT_DH)
            s = _dot_nt(q[:, hs], k3[:, hs]) * scale + b_ref[hh]
            s = jnp.where(in_seq, s, NEG)
            p = jnp.exp(s - jnp.max(s, axis=-1, keepdims=True))
            p = p / jnp.sum(p, axis=-1, keepdims=True)
            outs.append(_dot(p.astype(BF16), v3[:, hs]))
        o_ref[...] = jnp.concatenate(outs, axis=1).astype(BF16)

    def kv(d, col0):
        return pl.BlockSpec((QB, 128), lambda hp, b, i: (b * nb + jnp.maximum(i - d, 0), col0 + hp))

    return pl.pallas_call(
        body, name="att_fwd", grid=(8, B, nb),
        in_specs=[pl.BlockSpec((QB, 128), lambda hp, b, i: (b * nb + i, hp)),
                  kv(2, 8), kv(1, 8), kv(0, 8), kv(2, 16), kv(1, 16), kv(0, 16),
                  pl.BlockSpec((2, QB, KW), lambda hp, b, i: (hp, 0, 0))],
        out_specs=pl.BlockSpec((QB, 128), lambda hp, b, i: (b * nb + i, hp)),
        out_shape=jax.ShapeDtypeStruct((T, 1024), BF16), compiler_params=_params(3),
    )(qkv, qkv, qkv, qkv, qkv, qkv, qkv, bias)


def _att_bwd(qkv, do, bias_t, *, B, S):
    QB = ATT_BLOCK
    nb = S // QB
    KW = 3 * QB
    T = B * S
    scale = ATT_DH ** -0.5

    def body(q_ref, k0, k1, k2, v0, v1, v2, do_ref, b_ref, dq_ref, dk_ref, dv_ref, db_ref, dk_acc, dv_acc):
        b, i = pl.program_id(1), pl.program_id(2)

        @pl.when(i == 0)
        def _():
            dk_acc[...] = jnp.zeros_like(dk_acc)
            dv_acc[...] = jnp.zeros_like(dv_acc)

        @pl.when((b == 0) & (i == 0))
        def _():
            db_ref[...] = jnp.zeros_like(db_ref)

        @pl.when(i < nb)
        def _():
            k3 = jnp.concatenate([k0[...], k1[...], k2[...]], axis=0)
            v3 = jnp.concatenate([v0[...], v1[...], v2[...]], axis=0)
            k3t = k3.astype(F32).T.astype(BF16)
            q, dout = q_ref[...], do_ref[...]
            row = lax.broadcasted_iota(jnp.int32, (KW, QB), 0)
            in_seq = row >= (2 - i) * QB
            dqt, dk3, dv3 = [], [], []
            for hh in range(2):
                hs = slice(hh * ATT_DH, (hh + 1) * ATT_DH)
                st = _dot_nt(k3[:, hs], q[:, hs]) * scale + b_ref[hh]
                st = jnp.where(in_seq, st, NEG)
                p = jnp.exp(st - jnp.max(st, axis=0, keepdims=True))
                p = p / jnp.sum(p, axis=0, keepdims=True)
                dp = _dot_nt(v3[:, hs], dout[:, hs])
                ds = p * (dp - jnp.sum(p * dp, axis=0, keepdims=True))
                db_ref[hh] += ds
                dsb = (ds * scale).astype(BF16)
                dk3.append(_dot(dsb, q[:, hs]))
                dv3.append(_dot(p.astype(BF16), dout[:, hs]))
                dqt.append(_dot(k3t[hs, :], dsb))
            dq_ref[...] = jnp.concatenate(dqt, axis=0).T.astype(BF16)
            dk3 = jnp.concatenate(dk3, axis=1)
            dv3 = jnp.concatenate(dv3, axis=1)
            for d in range(3):
                slot = (i + 1 + d) % 3
                dk_acc[slot] += dk3[d * QB:(d + 1) * QB]
                dv_acc[slot] += dv3[d * QB:(d + 1) * QB]

        @pl.when(i >= 2)
        def _():
            slot = (i + 1) % 3
            dk_ref[...] = dk_acc[slot].astype(BF16)
            dv_ref[...] = dv_acc[slot].astype(BF16)
            dk_acc[slot] = jnp.zeros((QB, 128), F32)
            dv_acc[slot] = jnp.zeros((QB, 128), F32)

    def qrow(b, i):
        return b * nb + jnp.minimum(i, nb - 1)

    def kv(d, col0):
        return pl.BlockSpec((QB, 128), lambda hp, b, i: (b * nb + jnp.maximum(jnp.minimum(i, nb - 1) - d, 0), col0 + hp))

    late = pl.BlockSpec((QB, 128), lambda hp, b, i: (b * nb + jnp.maximum(i - 2, 0), hp))
    return pl.pallas_call(
        body, name="att_bwd", grid=(8, B, nb + 2),
        in_specs=[pl.BlockSpec((QB, 128), lambda hp, b, i: (qrow(b, i), hp)),
                  kv(2, 8), kv(1, 8), kv(0, 8), kv(2, 16), kv(1, 16), kv(0, 16),
                  pl.BlockSpec((QB, 128), lambda hp, b, i: (qrow(b, i), hp)),
                  pl.BlockSpec((2, KW, QB), lambda hp, b, i: (hp, 0, 0))],
        out_specs=[pl.BlockSpec((QB, 128), lambda hp, b, i: (qrow(b, i), hp)), late, late,
                   pl.BlockSpec((2, KW, QB), lambda hp, b, i: (hp, 0, 0))],
        out_shape=[jax.ShapeDtypeStruct((T, 1024), BF16)] * 3 + [jax.ShapeDtypeStruct((ATT_HEADS, KW, QB), F32)],
        scratch_shapes=[pltpu.VMEM((3, QB, 128), F32), pltpu.VMEM((3, QB, 128), F32)], compiler_params=_params(3),
    )(qkv, qkv, qkv, qkv, qkv, qkv, qkv, do, bias_t)


def _loss_head(h, tgt, g, *, tm):
    T, D = h.shape
    n = T // tm

    def body(h_ref, t_ref, g_ref, dh_ref, dg_ref, loss_ref, acc_ref):
        i = pl.program_id(0)
        x = h_ref[...]
        r = lax.rsqrt(jnp.mean(x * x, axis=-1, keepdims=True) + EPS)
        xh = x * r
        gg = g_ref[...]
        diff = xh * gg - t_ref[...]
        sq = jnp.sum(diff * diff, axis=0, keepdims=True)
        dy = diff * (1.0 / D)
        dg_part = jnp.sum(dy * xh, axis=0, keepdims=True)

        @pl.when(i == 0)
        def _():
            acc_ref[...] = sq
            dg_ref[...] = dg_part

        @pl.when(i > 0)
        def _():
            acc_ref[...] += sq
            dg_ref[...] += dg_part
        t = dy * gg
        dh_ref[...] = r * (t - xh * jnp.mean(t * xh, axis=-1, keepdims=True))

        @pl.when(i == n - 1)
        def _():
            loss_ref[...] = (0.5 / D) * jnp.sum(acc_ref[...], axis=1, keepdims=True)

    return pl.pallas_call(
        body, name="loss_head", grid=(n,),
        in_specs=[pl.BlockSpec((tm, D), lambda i: (i, 0)), pl.BlockSpec((tm, D), lambda i: (i, 0)),
                  pl.BlockSpec((1, D), lambda i: (0, 0))],
        out_specs=[pl.BlockSpec((tm, D), lambda i: (i, 0)), pl.BlockSpec((1, D), lambda i: (0, 0)),
                   pl.BlockSpec((1, 1), lambda i: (0, 0))],
        out_shape=[jax.ShapeDtypeStruct((T, D), F32), jax.ShapeDtypeStruct((1, D), F32), jax.ShapeDtypeStruct((1, 1), F32)],
        scratch_shapes=[pltpu.VMEM((1, D), F32)], compiler_params=_params(1),
    )(h, tgt, g.reshape(1, D))


def _tok_tile(T, want):
    t = min(T, want)
    assert T % t == 0
    return t


def _local_step(x, tgt, w, mix_g, gn_g, rel_bias, mlp_g, fin_g):
    B, S, D = x.shape
    T = B * S
    h0 = x.reshape(T, D)
    tgt = tgt.reshape(T, D)
    tm = _tok_tile(T, 1024)
    tb = _tok_tile(T, 512)
    tq = _tok_tile(T, 256)
    tt = _tok_tile(T, 2048)
    consts = _ret_consts(S, RET_BLOCK)

    proj, hn0 = _mm_nn(h0, w["ret_w_in"], "col", tm=tb, out_dtype=BF16, name="ret_in", norm_g=mix_g[0])
    y_ret, states = _ret_fwd(proj, gn_g, consts, B=B, S=S)
    h1 = _mm_nn(y_ret, w["ret_w_out"], "row", tm=tm, out_dtype=F32, name="ret_out", res=h0)
    a0, hm0 = _mm_nn(h1, w["mlp_w1_0"], "col", tm=tb, out_dtype=BF16, name="mlp0_up", norm_g=mlp_g[0], act="relu2")
    h2 = _mm_nn(a0, w["mlp_w2_0"], "row", tm=tb, out_dtype=F32, name="mlp0_down", res=h1)
    bias, bias_t = _att_bias(rel_bias)
    qkv, hn1 = _mm_nn(h2, w["att_w_in"], "col", tm=tb, out_dtype=BF16, name="att_in", norm_g=mix_g[1])
    o_att = _att_fwd(qkv, bias, B=B, S=S)
    h3 = _mm_nn(o_att, w["att_w_out"], "row", tm=tm, out_dtype=F32, name="att_out", res=h2)
    a1, hm1 = _mm_nn(h3, w["mlp_w1_1"], "col", tm=tb, out_dtype=BF16, name="mlp1_up", norm_g=mlp_g[1], act="relu2")
    h4 = _mm_nn(a1, w["mlp_w2_1"], "row", tm=tb, out_dtype=F32, name="mlp1_down", res=h3)
    dh4, d_fin_g, loss = _loss_head(h4, tgt, fin_g, tm=tb)

    gw = {}
    gw["mlp_w2_1"] = _mm_tn(a1, dh4, "row", tt=tt, tk=1024, tn=D, name="d_mlp1_w2")
    dz1 = _mm_nt(dh4, w["mlp_w2_1"], "row", tm=tb, name="d_mlp1_act", epi="sqrt2", a2=a1)
    gw["mlp_w1_1"] = _mm_tn(hm1, dz1, "col", tt=tt, tk=D, tn=1024, name="d_mlp1_w1")
    dh3, d_mlp_g1 = _mm_nt(dz1, w["mlp_w1_1"], "col", tm=tb, name="d_mlp1_in", epi="normbwd",
                           h=h3, g=mlp_g[1], dres=dh4)
    gw["att_w_out"] = _mm_tn(o_att, dh3, "row", tt=tt, tk=256, tn=D, name="d_att_wout")
    do_att = _mm_nt(dh3, w["att_w_out"], "row", tm=tb, name="d_att_o", epi="bf16")
    dq, dk, dv, dbias_t = _att_bwd(qkv, do_att, bias_t, B=B, S=S)
    d_rel = _att_bias_grad(dbias_t)
    dqkv = jnp.concatenate([dq, dk, dv], axis=1)
    gw["att_w_in"] = _mm_tn(hn1, dqkv, "col", tt=tt, tk=D, tn=768, name="d_att_win")
    dh2, d_mix_g1 = _mm_nt(dqkv, w["att_w_in"], "col", tm=tb, name="d_att_in", epi="normbwd",
                           h=h2, g=mix_g[1], dres=dh3)
    gw["mlp_w2_0"] = _mm_tn(a0, dh2, "row", tt=tt, tk=1024, tn=D, name="d_mlp0_w2")
    dz0 = _mm_nt(dh2, w["mlp_w2_0"], "row", tm=tb, name="d_mlp0_act", epi="sqrt2", a2=a0)
    gw["mlp_w1_0"] = _mm_tn(hm0, dz0, "col", tt=tt, tk=D, tn=1024, name="d_mlp0_w1")
    dh1, d_mlp_g0 = _mm_nt(dz0, w["mlp_w1_0"], "col", tm=tb, name="d_mlp0_in", epi="normbwd",
                           h=h1, g=mlp_g[0], dres=dh2)
    gw["ret_w_out"] = _mm_tn(y_ret, dh1, "row", tt=tt, tk=512, tn=D, name="d_ret_wout")
    dy_ret = _mm_nt(dh1, w["ret_w_out"], "row", tm=tb, name="d_ret_y", epi="bf16")
    dproj, d_gn = _ret_bwd(proj, dy_ret, states, gn_g, consts, B=B, S=S)
    gw["ret_w_in"] = _mm_tn(hn0, dproj, "col", tt=tt, tk=D, tn=768, name="d_ret_win")
    dx, d_mix_g0 = _mm_nt(dproj, w["ret_w_in"], "col", tm=tq, name="d_ret_in", epi="normbwd",
                          h=h0, g=mix_g[0], dres=dh1)
    small = [d_mix_g0, d_mix_g1, d_mlp_g0, d_mlp_g1, d_fin_g, d_gn.reshape(2, D), d_rel.reshape(5, D)]
    return loss.reshape(()), dx.reshape(B, S, D), gw, small


def _row_tile(r, want=256):
    t = min(r, want)
    assert r % t == 0
    return t


def _into_slab(place, a, dtype, name):
    r, c = a.shape
    tr = _row_tile(r)

    def body(place_ref, a_ref, o_ref):
        o_ref[...] = a_ref[...].astype(dtype)

    grid_spec = pltpu.PrefetchScalarGridSpec(
        num_scalar_prefetch=1, grid=(r // tr,), in_specs=[pl.BlockSpec((tr, c), lambda i, pr: (i, 0))],
        out_specs=pl.BlockSpec((None, tr, c), lambda i, pr: (pr[0], i, 0)),
    )
    return pl.pallas_call(
        body, name=name, grid_spec=grid_spec, out_shape=jax.ShapeDtypeStruct((N_CHIPS, r, c), dtype), compiler_params=_params(1),
    )(place, a)


def _add_sibling(place, g, recv, name):
    _, r, c = g.shape
    hr = r // 2
    tr = _row_tile(hr)
    nrt = hr // tr

    def body(place_ref, g_ref, r_ref, sb_ref, own_ref):
        v = g_ref[...] + r_ref[...]
        sb_ref[...] = v.astype(BF16)

        @pl.when(pl.program_id(1) == place_ref[0])
        def _():
            own_ref[...] = v

    grid_spec = pltpu.PrefetchScalarGridSpec(
        num_scalar_prefetch=1, grid=(nrt, N_CHIPS),
        in_specs=[pl.BlockSpec((None, tr, c), lambda i, s, pr: (s, pr[1] * nrt + i, 0)),
                  pl.BlockSpec((None, tr, c), lambda i, s, pr: (s, i, 0))],
        out_specs=[pl.BlockSpec((None, tr, c), lambda i, s, pr: (s, i, 0)), pl.BlockSpec((tr, c), lambda i, s, pr: (i, 0))],
    )
    return pl.pallas_call(
        body, name=name, grid_spec=grid_spec,
        out_shape=[jax.ShapeDtypeStruct((N_CHIPS, hr, c), BF16), jax.ShapeDtypeStruct((hr, c), F32)],
        compiler_params=_params(2),
    )(place, g, recv)


def _add_chips(place, own, recv, name):
    hr, c = own.shape
    tr = _row_tile(hr)
    nrt = hr // tr

    def body(place_ref, o_ref, r_ref, t_ref):
        t_ref[...] = ((o_ref[...] + r_ref[0].astype(F32)) + r_ref[1].astype(F32)) + r_ref[2].astype(F32)

    grid_spec = pltpu.PrefetchScalarGridSpec(
        num_scalar_prefetch=1, grid=(nrt,),
        in_specs=[pl.BlockSpec((tr, c), lambda i, pr: (i, 0)), pl.BlockSpec((3, tr, c), lambda i, pr: (0, i, 0))],
        out_specs=pl.BlockSpec((tr, c), lambda i, pr: (pr[1] * nrt + i, 0)),
    )
    return pl.pallas_call(
        body, name=name, grid_spec=grid_spec, out_shape=jax.ShapeDtypeStruct((2 * hr, c), F32), compiler_params=_params(1),
    )(place, own, recv)


def _adamw(w, g, m, v, name):
    r, c = w.shape
    tr = _row_tile(r)

    def body(w_ref, g_ref, m_ref, v_ref, d_ref, nm_ref, nv_ref):
        gg = g_ref[...]
        nm = ADAM_B1 * m_ref[...] + (1.0 - ADAM_B1) * gg
        nv = ADAM_B2 * v_ref[...] + (1.0 - ADAM_B2) * (gg * gg)
        m_hat = nm / (1.0 - ADAM_B1 ** ADAM_STEP)
        v_hat = nv / (1.0 - ADAM_B2 ** ADAM_STEP)
        d_ref[...] = -ADAM_LR * (m_hat / (jnp.sqrt(v_hat) + ADAM_EPS) + ADAM_WD * w_ref[...])
        nm_ref[...] = nm
        nv_ref[...] = nv

    spec = pl.BlockSpec((tr, c), lambda i: (i, 0))
    return pl.pallas_call(
        body, name=name, grid=(r // tr,), in_specs=[spec] * 4, out_specs=[spec] * 3,
        out_shape=[jax.ShapeDtypeStruct((r, c), F32)] * 3, compiler_params=_params(1),
    )(w, g, m, v)


HBM_SPEC = pl.BlockSpec(memory_space=pltpu.HBM)


def _place():
    return lax.axis_index("x"), lax.axis_index("y"), lax.axis_index("c")


def _other_chips(x, y):
    return [(1 - x, y), (x, 1 - y), (1 - x, 1 - y)]


def _remote(src, dst, ssem, rsem, dev):
    return pltpu.make_async_remote_copy(src_ref=src, dst_ref=dst, send_sem=ssem, recv_sem=rsem, device_id=dev,
                                        device_id_type=MESH)


def _all_gather_slabs(slabs):
    n = len(slabs)

    def body(*refs):
        outs = refs[n:2 * n]
        s1, r1, s2, r2 = refs[2 * n:]
        x, y, c = _place()
        p = 2 * x + y
        chips = _other_chips(x, y)
        cidx = [2 * qx + qy for qx, qy in chips]
        pending = []
        for t in range(n):
            hr = outs[t].shape[1] // 2
            mine = pl.ds(c * hr, hr)
            for j, (qx, qy) in enumerate(chips):
                cp = _remote(outs[t].at[p, mine], outs[t].at[p, mine], s1.at[t, j], r1.at[t, j], (qx, qy, c))
                cp.start()
                pending.append(cp)
        for t in range(n):
            hr = outs[t].shape[1] // 2
            mine = pl.ds(c * hr, hr)
            for j, (qx, qy) in enumerate(chips):
                got = outs[t].at[cidx[j], mine]
                _remote(got, got, s1.at[t, j], r1.at[t, j], (qx, qy, c)).wait_recv()
                fw = _remote(got, got, s2.at[t, j], r2.at[t, j], (x, y, 1 - c))
                fw.start()
                pending.append(fw)
        for t in range(n):
            hr = outs[t].shape[1] // 2
            theirs = pl.ds((1 - c) * hr, hr)
            for j in range(3):
                got = outs[t].at[cidx[j], theirs]
                _remote(got, got, s2.at[t, j], r2.at[t, j], (x, y, 1 - c)).wait_recv()
        for cp in pending:
            cp.wait_send()

    sem = pltpu.SemaphoreType.DMA
    return pl.pallas_call(
        body, name="gather_weights", in_specs=[HBM_SPEC] * n, out_specs=[HBM_SPEC] * n,
        out_shape=[jax.ShapeDtypeStruct(s.shape, s.dtype) for s in slabs], input_output_aliases={t: t for t in range(n)},
        scratch_shapes=[sem((n, 3)), sem((n, 3)), sem((n, 3)), sem((n, 3))],
        compiler_params=pltpu.CompilerParams(has_side_effects=True),
    )(*slabs)


def _exchange_with_sibling(grads):
    n = len(grads)

    def body(*refs):
        ins, outs = refs[:n], refs[n:2 * n]
        ssem, rsem = refs[2 * n:]
        x, y, c = _place()
        copies = []
        for t in range(n):
            hr = ins[t].shape[1] // 2
            cp = _remote(ins[t].at[:, pl.ds((1 - c) * hr, hr), :], outs[t], ssem.at[t], rsem.at[t], (x, y, 1 - c))
            cp.start()
            copies.append(cp)
        for cp in copies:
            cp.wait()

    sem = pltpu.SemaphoreType.DMA
    return pl.pallas_call(
        body, name="grads_to_sibling", in_specs=[HBM_SPEC] * n, out_specs=[HBM_SPEC] * n,
        out_shape=[jax.ShapeDtypeStruct((N_CHIPS, g.shape[1] // 2, g.shape[2]), g.dtype) for g in grads],
        scratch_shapes=[sem((n,)), sem((n,))], compiler_params=pltpu.CompilerParams(has_side_effects=True),
    )(*grads)


def _exchange_with_chips(sums):
    n = len(sums)

    def body(*refs):
        ins, outs = refs[:n], refs[n:2 * n]
        ssem, rsem = refs[2 * n:]
        x, y, c = _place()
        chips = _other_chips(x, y)
        copies = []
        for t in range(n):
            for j, (qx, qy) in enumerate(chips):
                cp = _remote(ins[t].at[2 * qx + qy], outs[t].at[j], ssem.at[t, j], rsem.at[t, j], (qx, qy, c))
                cp.start()
                copies.append(cp)
        for cp in copies:
            cp.wait()

    sem = pltpu.SemaphoreType.DMA
    return pl.pallas_call(
        body, name="grads_to_chips", in_specs=[HBM_SPEC] * n, out_specs=[HBM_SPEC] * n,
        out_shape=[jax.ShapeDtypeStruct((3,) + s.shape[1:], s.dtype) for s in sums],
        scratch_shapes=[sem((n, 3)), sem((n, 3))], compiler_params=pltpu.CompilerParams(has_side_effects=True),
    )(*sums)


def _share_with_sibling(shards):
    n = len(shards)

    def body(*refs):
        outs = refs[n:2 * n]
        ssem, rsem = refs[2 * n:]
        x, y, c = _place()
        copies = []
        for t in range(n):
            hr = outs[t].shape[0] // 2
            mine = outs[t].at[pl.ds(c * hr, hr)]
            cp = _remote(mine, mine, ssem.at[t], rsem.at[t], (x, y, 1 - c))
            cp.start()
            copies.append(cp)
        for t in range(n):
            hr = outs[t].shape[0] // 2
            theirs = outs[t].at[pl.ds((1 - c) * hr, hr)]
            _remote(theirs, theirs, ssem.at[t], rsem.at[t], (x, y, 1 - c)).wait_recv()
        for cp in copies:
            cp.wait_send()

    sem = pltpu.SemaphoreType.DMA
    return pl.pallas_call(
        body, name="grads_share", in_specs=[HBM_SPEC] * n, out_specs=[HBM_SPEC] * n,
        out_shape=[jax.ShapeDtypeStruct(s.shape, s.dtype) for s in shards], input_output_aliases={t: t for t in range(n)},
        scratch_shapes=[sem((n,)), sem((n,))], compiler_params=pltpu.CompilerParams(has_side_effects=True),
    )(*shards)


def _all_reduce_small(buf):
    R, C = buf.shape

    def body(in_ref, out_ref, gather, ssem, rsem):
        x, y, c = _place()
        me = 4 * x + 2 * y + c
        gather[me] = in_ref[...]
        flips = [(fx, fy, fc) for fx in (0, 1) for fy in (0, 1) for fc in (0, 1) if fx or fy or fc]
        peers = [(x + fx - 2 * x * fx, y + fy - 2 * y * fy, c + fc - 2 * c * fc) for fx, fy, fc in flips]
        copies = [_remote(in_ref, gather.at[me], ssem.at[k], rsem.at[k], peer) for k, peer in enumerate(peers)]
        for cp in copies:
            cp.start()
        for k, (px, py, pc) in enumerate(peers):
            _remote(in_ref, gather.at[4 * px + 2 * py + pc], ssem.at[k], rsem.at[k], (px, py, pc)).wait_recv()
        for cp in copies:
            cp.wait_send()
        acc = gather[0]
        for d in range(1, 8):
            acc = acc + gather[d]
        out_ref[...] = acc

    sem = pltpu.SemaphoreType.DMA
    vmem = pl.BlockSpec(memory_space=pltpu.VMEM)
    return pl.pallas_call(
        body, name="small_grads_sum", in_specs=[vmem], out_specs=vmem, out_shape=jax.ShapeDtypeStruct((R, C), F32),
        scratch_shapes=[pltpu.VMEM((8, R, C), F32), sem((7,)), sem((7,))],
        compiler_params=pltpu.CompilerParams(has_side_effects=True),
    )(buf)


BIG = ["ret_w_in", "ret_w_out", "att_w_in", "att_w_out", "mlp_w1_0", "mlp_w1_1", "mlp_w2_0", "mlp_w2_1"]


def _split_big(ret_w_in, ret_w_out, att_w_in, att_w_out, mlp_w1, mlp_w2):
    return {"ret_w_in": ret_w_in[0], "ret_w_out": ret_w_out[0], "att_w_in": att_w_in[0], "att_w_out": att_w_out[0],
            "mlp_w1_0": mlp_w1[0], "mlp_w1_1": mlp_w1[1], "mlp_w2_0": mlp_w2[0], "mlp_w2_1": mlp_w2[1]}


def kernel(x, mix_norm_g, ret_w_in, ret_gn_g, ret_w_out, att_w_in, att_rel_bias, att_w_out, mlp_norm_g, mlp_w1, mlp_w2, final_norm_g, loss_target, m_mix_norm_g, m_ret_w_in, m_ret_gn_g, m_ret_w_out, m_att_w_in, m_att_rel_bias, m_att_w_out, m_mlp_norm_g, m_mlp_w1, m_mlp_w2, m_final_norm_g, v_mix_norm_g, v_ret_w_in, v_ret_gn_g, v_ret_w_out, v_att_w_in, v_att_rel_bias, v_att_w_out, v_mlp_norm_g, v_mlp_w1, v_mlp_w2, v_final_norm_g):
    xi, yi, ci = _place()
    chip = 2 * xi + yi
    w32 = _split_big(ret_w_in, ret_w_out, att_w_in, att_w_out, mlp_w1, mlp_w2)
    m32 = _split_big(m_ret_w_in, m_ret_w_out, m_att_w_in, m_att_w_out, m_mlp_w1, m_mlp_w2)
    v32 = _split_big(v_ret_w_in, v_ret_w_out, v_att_w_in, v_att_w_out, v_mlp_w1, v_mlp_w2)

    place = jnp.stack([chip, ci]).astype(jnp.int32)
    gathered = _all_gather_slabs([_into_slab(place, w32[n], BF16, name="cast_" + n) for n in BIG]
                                 + [_into_slab(place, att_rel_bias[0], F32, name="slab_rel_bias")])
    w = dict(zip(BIG, gathered[:-1]))
    rel_full = jnp.transpose(gathered[-1], (1, 0, 2)).reshape(ATT_HEADS, REL_TABLE)

    loss_local, grad_x, gw, small = _local_step(x, loss_target, w, mix_norm_g, ret_gn_g[0], rel_full, mlp_norm_g, final_norm_g)
    loss = lax.psum(loss_local, ("x", "y", "c"))

    from_sibling = _exchange_with_sibling([gw[n] for n in BIG])
    sums = [_add_sibling(place, gw[n], r, name="chip_sum_" + n) for n, r in zip(BIG, from_sibling)]
    from_chips = _exchange_with_chips([s[0] for s in sums])
    totals = [_add_chips(place, s[1], r, name="total_" + n) for n, s, r in zip(BIG, sums, from_chips)]
    g_big = dict(zip(BIG, _share_with_sibling(totals)))

    rows, at = jnp.zeros((16, D_MODEL), F32), 0
    for part in small:
        rows = rows + jnp.pad(part, ((at, 16 - at - part.shape[0]), (0, 0)))
        at += part.shape[0]
    rows = _all_reduce_small(rows)
    g_small = {"mix_norm_g": rows[0:2], "mlp_norm_g": rows[2:4], "final_norm_g": rows[4:5], "ret_gn_g": rows[5:7].reshape(1, 2048),
               "att_rel_bias": lax.dynamic_slice_in_dim(rows[7:12].reshape(ATT_HEADS, REL_TABLE), chip * (REL_TABLE // N_CHIPS),
                                                        REL_TABLE // N_CHIPS, axis=1)}

    upd = {n: _adamw(w32[n], g_big[n], m32[n], v32[n], name="adamw_" + n) for n in BIG}
    small_in = {"mix_norm_g": (mix_norm_g, m_mix_norm_g, v_mix_norm_g), "ret_gn_g": (ret_gn_g, m_ret_gn_g, v_ret_gn_g),
                "att_rel_bias": (att_rel_bias[0], m_att_rel_bias[0], v_att_rel_bias[0]),
                "mlp_norm_g": (mlp_norm_g, m_mlp_norm_g, v_mlp_norm_g),
                "final_norm_g": (final_norm_g.reshape(1, -1), m_final_norm_g.reshape(1, -1), v_final_norm_g.reshape(1, -1))}
    upd_small = {n: _adamw(wv, g_small[n], mv, vv, name="adamw_" + n) for n, (wv, mv, vv) in small_in.items()}

    def big_out(k):
        def get(n):
            return g_big[n] if k is None else upd[n][k]
        return {"ret_w_in": get("ret_w_in")[None], "ret_w_out": get("ret_w_out")[None], "att_w_in": get("att_w_in")[None],
                "att_w_out": get("att_w_out")[None], "mlp_w1": jnp.stack([get("mlp_w1_0"), get("mlp_w1_1")]),
                "mlp_w2": jnp.stack([get("mlp_w2_0"), get("mlp_w2_1")])}

    def small_out(k):
        def get(n):
            return g_small[n] if k is None else upd_small[n][k]
        return {"mix_norm_g": get("mix_norm_g"), "ret_gn_g": get("ret_gn_g"), "att_rel_bias": get("att_rel_bias")[None],
                "mlp_norm_g": get("mlp_norm_g"), "final_norm_g": get("final_norm_g").reshape(-1)}

    order = ["mix_norm_g", "ret_w_in", "ret_gn_g", "ret_w_out", "att_w_in", "att_rel_bias", "att_w_out", "mlp_norm_g",
             "mlp_w1", "mlp_w2", "final_norm_g"]
    outs = [loss, grad_x]
    for k in (None, 0, 1, 2):
        both = {**big_out(k), **small_out(k)}
        outs += [both[n] for n in order]
    return tuple(outs)
```

```python
import jax
import jax.numpy as jnp
from jax import lax
from jax.experimental import pallas as pl
from jax.experimental.pallas import tpu as pltpu

F32 = jnp.float32
BF16 = jnp.bfloat16
MESH = pl.DeviceIdType.MESH

D_MODEL = 1024
CHUNK = 64
RET_HEADS = 4
RET_DK = 256
RET_DV = 512
ROPE_BASE = 10000.0
ATT_HEADS = 16
ATT_DH = 64
PAST = 512
MAX_REL = 256
REL_TABLE = MAX_REL + CHUNK
EPS = 1e-6
NEG = -1e30
N_CHIPS = 4

ADAM_LR = 0.001
ADAM_B1 = 0.9
ADAM_B2 = 0.999
ADAM_EPS = 1e-08
ADAM_WD = 0.01
ADAM_STEP = 10

RET_BLOCK = 256
ATT_BLOCK = 256
VMEM_LIMIT = 56 * 1024 * 1024


def _params(n_axes, **kw):
    return pltpu.CompilerParams(dimension_semantics=("arbitrary",) * n_axes, vmem_limit_bytes=VMEM_LIMIT, **kw)


def _dot(a, b):
    return jnp.dot(a, b, preferred_element_type=F32)


def _dot_nt(a, b):
    return lax.dot_general(a, b, (((1,), (1,)), ((), ())), preferred_element_type=F32)


def _dot_tn(a, b):
    return lax.dot_general(a, b, (((0,), (0,)), ((), ())), preferred_element_type=F32)


def _sigmoid(x):
    return 1.0 / (1.0 + jnp.exp(-x))


def _mm_nn(a, w, wkind, *, tm, out_dtype, name, norm_g=None, act=None, res=None):
    M, K = a.shape
    cols = w.shape[2]
    N = N_CHIPS * cols if wkind == "col" else cols
    has_norm = norm_g is not None
    assert M % tm == 0 and K == (w.shape[1] if wkind == "col" else N_CHIPS * w.shape[1])

    def body(*refs):
        it = iter(refs)
        a_ref, w_ref = next(it), next(it)
        g_ref = next(it) if has_norm else None
        r_ref = next(it) if res is not None else None
        o_ref = next(it)
        hn_ref = next(it) if has_norm else None
        if has_norm:
            x = a_ref[...].astype(F32)
            r = lax.rsqrt(jnp.mean(x * x, axis=-1, keepdims=True) + EPS)
            lhs = (x * r * g_ref[...]).astype(BF16)
            hn_ref[...] = lhs
        else:
            lhs = a_ref[...].astype(BF16)

        def finish(acc, sl):
            if act == "relu2":
                u = jnp.maximum(acc, 0.0)
                acc = u * u
            if r_ref is not None:
                acc = acc + r_ref[:, sl]
            o_ref[:, sl] = acc.astype(out_dtype)

        if wkind == "col":
            for s in range(N_CHIPS):
                finish(_dot(lhs, w_ref[s]), slice(s * cols, (s + 1) * cols))
        else:
            finish(_dot(lhs, w_ref[...].reshape(K, N)), slice(None))

    in_specs = [pl.BlockSpec((tm, K), lambda i: (i, 0)), pl.BlockSpec(w.shape, lambda i: (0, 0, 0))]
    args = [a, w]
    if has_norm:
        in_specs.append(pl.BlockSpec((1, K), lambda i: (0, 0)))
        args.append(norm_g.reshape(1, K))
    if res is not None:
        in_specs.append(pl.BlockSpec((tm, N), lambda i: (i, 0)))
        args.append(res)
    out_shape = [jax.ShapeDtypeStruct((M, N), out_dtype)]
    out_specs = [pl.BlockSpec((tm, N), lambda i: (i, 0))]
    if has_norm:
        out_shape.append(jax.ShapeDtypeStruct((M, K), BF16))
        out_specs.append(pl.BlockSpec((tm, K), lambda i: (i, 0)))
    outs = pl.pallas_call(
        body, name=name, grid=(M // tm,), in_specs=in_specs, out_specs=out_specs, out_shape=out_shape, compiler_params=_params(1),
    )(*args)
    return outs if has_norm else outs[0]


def _mm_nt(a, w, wkind, *, tm, name, epi, a2=None, h=None, g=None, dres=None):
    M, Nw = a.shape
    rows, cols = w.shape[1], w.shape[2]
    Kw = rows if wkind == "col" else N_CHIPS * rows
    assert M % tm == 0 and Nw == (N_CHIPS * cols if wkind == "col" else cols)
    assert epi != "normbwd" or wkind == "col"

    def body(*refs):
        it = iter(refs)
        a_ref, w_ref = next(it), next(it)
        a2_ref = next(it) if epi == "sqrt2" else None
        if epi == "normbwd":
            h_ref, g_ref, dres_ref = next(it), next(it), next(it)
        o_ref = next(it)
        dg_ref = next(it) if epi == "normbwd" else None
        i = pl.program_id(0)

        def finish(acc, sl):
            if epi == "bf16":
                o_ref[:, sl] = acc.astype(BF16)
            elif epi == "sqrt2":
                o_ref[:, sl] = (acc * (2.0 * jnp.sqrt(a2_ref[:, sl].astype(F32)))).astype(BF16)
            else:
                x = h_ref[...]
                r = lax.rsqrt(jnp.mean(x * x, axis=-1, keepdims=True) + EPS)
                xh = x * r
                dg_part = jnp.sum(acc * xh, axis=0, keepdims=True)

                @pl.when(i == 0)
                def _():
                    dg_ref[...] = dg_part

                @pl.when(i > 0)
                def _():
                    dg_ref[...] += dg_part
                t = acc * g_ref[...]
                dx = r * (t - xh * jnp.mean(t * xh, axis=-1, keepdims=True))
                o_ref[...] = dres_ref[...] + dx

        if wkind == "col":
            acc = _dot_nt(a_ref[:, :cols].astype(BF16), w_ref[0])
            for s in range(1, N_CHIPS):
                acc += _dot_nt(a_ref[:, s * cols:(s + 1) * cols].astype(BF16), w_ref[s])
            finish(acc, slice(None))
        else:
            lhs = a_ref[...].astype(BF16)
            for s in range(N_CHIPS):
                finish(_dot_nt(lhs, w_ref[s]), slice(s * rows, (s + 1) * rows))

    in_specs = [pl.BlockSpec((tm, Nw), lambda i: (i, 0)), pl.BlockSpec(w.shape, lambda i: (0, 0, 0))]
    args = [a, w]
    out_dtype = BF16
    if epi == "sqrt2":
        in_specs.append(pl.BlockSpec((tm, Kw), lambda i: (i, 0)))
        args.append(a2)
    if epi == "normbwd":
        in_specs += [pl.BlockSpec((tm, Kw), lambda i: (i, 0)), pl.BlockSpec((1, Kw), lambda i: (0, 0)),
                     pl.BlockSpec((tm, Kw), lambda i: (i, 0))]
        args += [h, g.reshape(1, Kw), dres]
        out_dtype = F32
    out_shape = [jax.ShapeDtypeStruct((M, Kw), out_dtype)]
    out_specs = [pl.BlockSpec((tm, Kw), lambda i: (i, 0))]
    if epi == "normbwd":
        out_shape.append(jax.ShapeDtypeStruct((1, Kw), F32))
        out_specs.append(pl.BlockSpec((1, Kw), lambda i: (0, 0)))
    outs = pl.pallas_call(
        body, name=name, grid=(M // tm,), in_specs=in_specs, out_specs=out_specs, out_shape=out_shape, compiler_params=_params(1),
    )(*args)
    return outs if epi == "normbwd" else outs[0]


def _mm_tn(a, b, okind, *, tt, tk, tn, name):
    T, K = a.shape
    N = b.shape[1]
    assert T % tt == 0 and K % tk == 0 and N % tn == 0
    nt = T // tt
    if okind == "col":
        per = (N // N_CHIPS) // tn
        assert (N // N_CHIPS) % tn == 0
        out_shape = jax.ShapeDtypeStruct((N_CHIPS, K, N // N_CHIPS), F32)
        out_spec = pl.BlockSpec((None, tk, tn), lambda ki, nj, t: (nj // per, ki, nj % per))
    else:
        per = (K // N_CHIPS) // tk
        assert (K // N_CHIPS) % tk == 0
        out_shape = jax.ShapeDtypeStruct((N_CHIPS, K // N_CHIPS, N), F32)
        out_spec = pl.BlockSpec((None, tk, tn), lambda ki, nj, t: (ki // per, ki % per, nj))

    def body(a_ref, b_ref, o_ref):
        t = pl.program_id(2)
        part = _dot_tn(a_ref[...].astype(BF16), b_ref[...].astype(BF16))

        @pl.when(t == 0)
        def _():
            o_ref[...] = part

        @pl.when(t > 0)
        def _():
            o_ref[...] += part

    return pl.pallas_call(
        body, name=name, grid=(K // tk, N // tn, nt),
        in_specs=[pl.BlockSpec((tt, tk), lambda ki, nj, t: (t, ki)), pl.BlockSpec((tt, tn), lambda ki, nj, t: (t, nj))],
        out_specs=out_spec, out_shape=out_shape, compiler_params=_params(3),
    )(a, b)


def _ret_consts(S, LB):
    log_gamma = jnp.log1p(-jnp.exp2(-5.0 - jnp.arange(RET_HEADS, dtype=F32)))
    idx = jnp.arange(LB, dtype=F32)
    n, m = idx[:, None], idx[None, :]
    cn, cm = jnp.floor(n / CHUNK), jnp.floor(m / CHUNK)
    dist = jnp.where(cm == cn, jnp.abs(n - m), n - m)
    dmat = jnp.where((cm <= cn)[None], jnp.exp(log_gamma[:, None, None] * dist[None]), 0.0)
    qd = jnp.exp(log_gamma[:, None] * (idx + 1.0)[None, :])[..., None]
    kd = jnp.exp(log_gamma[:, None] * (LB - 1 - idx)[None, :])[..., None]
    bd = jnp.exp(log_gamma * LB).reshape(RET_HEADS, 1, 1) * jnp.ones((RET_HEADS, 1, 128), F32)
    half = RET_DK // 2
    inv = jnp.exp(-jnp.log(ROPE_BASE) * jnp.arange(half, dtype=F32) / half)
    ang = jnp.arange(S, dtype=F32)[:, None] * inv[None, :]
    return dmat.astype(F32), qd.astype(F32), kd.astype(F32), bd, jnp.cos(ang), jnp.sin(ang)


def _rope(t, c, s):
    t1, t2 = t[:, :128], t[:, 128:]
    return jnp.concatenate([t1 * c - t2 * s, t1 * s + t2 * c], axis=-1)


def _rope_inv(d, c, s):
    d1, d2 = d[:, :128], d[:, 128:]
    return jnp.concatenate([d1 * c + d2 * s, d2 * c - d1 * s], axis=-1)


def _ret_block_fwd(p_ref, h, c, s, d_ref, qd_ref, kd_ref, stb):
    q = _rope(p_ref[:, h * RET_DK:(h + 1) * RET_DK].astype(F32), c, s)
    k = _rope(p_ref[:, 1024 + h * RET_DK:1024 + (h + 1) * RET_DK].astype(F32), c, s) * (RET_DK ** -0.5)
    v = p_ref[:, 2048 + h * RET_DV:2048 + (h + 1) * RET_DV]
    qb, kb = q.astype(BF16), k.astype(BF16)
    scb = (_dot_nt(qb, kb) * d_ref[h]).astype(BF16)
    o = _dot(scb, v) + qd_ref[h] * _dot(qb, stb)
    return q, k, qb, kb, v, scb, o


def _ret_fwd(proj, gn_g, consts, *, B, S):
    LB = RET_BLOCK
    nb = S // LB
    T = B * S
    dmat, qd, kd, bd, cos, sin = consts

    def body(p_ref, cos_ref, sin_ref, d_ref, qd_ref, kd_ref, bd_ref, gng_ref, y_ref, st_ref, state_s):
        i = pl.program_id(1)

        @pl.when(i == 0)
        def _():
            state_s[...] = jnp.zeros_like(state_s)
        c, s = cos_ref[...], sin_ref[...]
        for h in range(RET_HEADS):
            st = state_s[h]
            stb = st.astype(BF16)
            st_ref[h] = stb
            q, k, qb, kb, v, scb, o = _ret_block_fwd(p_ref, h, c, s, d_ref, qd_ref, kd_ref, stb)
            kdk = (k * kd_ref[h]).astype(BF16)
            state_s[h] = st * bd_ref[h][:, :1] + _dot_tn(kdk, v)
            gate = p_ref[:, 4096 + h * RET_DV:4096 + (h + 1) * RET_DV].astype(F32)
            mu = jnp.mean(o, axis=-1, keepdims=True)
            oc = o - mu
            xh = oc * lax.rsqrt(jnp.mean(oc * oc, axis=-1, keepdims=True) + EPS)
            y = (gate * _sigmoid(gate)) * (xh * gng_ref[:, h * RET_DV:(h + 1) * RET_DV])
            y_ref[:, h * RET_DV:(h + 1) * RET_DV] = y.astype(BF16)

    const = lambda b, i: (0, 0, 0)
    return pl.pallas_call(
        body, name="ret_fwd", grid=(B, nb),
        in_specs=[pl.BlockSpec((LB, 6144), lambda b, i: (b * nb + i, 0)),
                  pl.BlockSpec((LB, 128), lambda b, i: (i, 0)), pl.BlockSpec((LB, 128), lambda b, i: (i, 0)),
                  pl.BlockSpec((RET_HEADS, LB, LB), const), pl.BlockSpec((RET_HEADS, LB, 1), const),
                  pl.BlockSpec((RET_HEADS, LB, 1), const), pl.BlockSpec((RET_HEADS, 1, 128), const),
                  pl.BlockSpec((1, 2048), lambda b, i: (0, 0))],
        out_specs=[pl.BlockSpec((LB, 2048), lambda b, i: (b * nb + i, 0)),
                   pl.BlockSpec((None, None, RET_HEADS, RET_DK, RET_DV), lambda b, i: (b, i, 0, 0, 0))],
        out_shape=[jax.ShapeDtypeStruct((T, 2048), BF16), jax.ShapeDtypeStruct((B, nb, RET_HEADS, RET_DK, RET_DV), BF16)],
        scratch_shapes=[pltpu.VMEM((RET_HEADS, RET_DK, RET_DV), F32)], compiler_params=_params(2),
    )(proj, cos, sin, dmat, qd, kd, bd, gn_g.reshape(1, 2048))


def _ret_bwd(proj, dy, states, gn_g, consts, *, B, S):
    LB = RET_BLOCK
    nb = S // LB
    T = B * S
    dmat, qd, kd, bd, cos, sin = consts

    def body(p_ref, dy_ref, st_ref, cos_ref, sin_ref, d_ref, qd_ref, kd_ref, bd_ref, gng_ref, dp_ref, dgn_ref, dstate_s):
        b, i = pl.program_id(0), pl.program_id(1)

        @pl.when(i == 0)
        def _():
            dstate_s[...] = jnp.zeros_like(dstate_s)

        @pl.when((b == 0) & (i == 0))
        def _():
            dgn_ref[...] = jnp.zeros_like(dgn_ref)
        c, s = cos_ref[...], sin_ref[...]
        for h in range(RET_HEADS):
            vs = slice(h * RET_DV, (h + 1) * RET_DV)
            stb = st_ref[h]
            q, k, qb, kb, v, scb, o = _ret_block_fwd(p_ref, h, c, s, d_ref, qd_ref, kd_ref, stb)
            gate = p_ref[:, 4096 + h * RET_DV:4096 + (h + 1) * RET_DV].astype(F32)
            mu = jnp.mean(o, axis=-1, keepdims=True)
            oc = o - mu
            rstd = lax.rsqrt(jnp.mean(oc * oc, axis=-1, keepdims=True) + EPS)
            xh = oc * rstd
            gng = gng_ref[:, vs]
            dyh = dy_ref[:, vs].astype(F32)
            sg = _sigmoid(gate)
            silu = gate * sg
            dgn_ref[:, vs] += jnp.sum(dyh * silu * xh, axis=0, keepdims=True)
            dxh = dyh * silu * gng
            do = rstd * (dxh - jnp.mean(dxh, axis=-1, keepdims=True) - xh * jnp.mean(dxh * xh, axis=-1, keepdims=True))
            dgate = dyh * xh * gng * (sg * (1.0 + gate * (1.0 - sg)))
            dob = do.astype(BF16)
            dsb = (_dot_nt(dob, v) * d_ref[h]).astype(BF16)
            dst = dstate_s[h]
            dstb = dst.astype(BF16)
            kdk = (k * kd_ref[h]).astype(BF16)
            dqr = _dot(dsb, kb) + qd_ref[h] * _dot_nt(dob, stb)
            dkr = _dot_tn(dsb, qb) + kd_ref[h] * _dot_nt(v, dstb)
            dv = _dot_tn(scb, dob) + _dot(kdk, dstb)
            dstate_s[h] = dst * bd_ref[h][:, :1] + _dot_tn((q * qd_ref[h]).astype(BF16), dob)
            dp_ref[:, h * RET_DK:(h + 1) * RET_DK] = _rope_inv(dqr, c, s).astype(BF16)
            dp_ref[:, 1024 + h * RET_DK:1024 + (h + 1) * RET_DK] = (_rope_inv(dkr, c, s) * (RET_DK ** -0.5)).astype(BF16)
            dp_ref[:, 2048 + h * RET_DV:2048 + (h + 1) * RET_DV] = dv.astype(BF16)
            dp_ref[:, 4096 + h * RET_DV:4096 + (h + 1) * RET_DV] = dgate.astype(BF16)

    const = lambda b, i: (0, 0, 0)
    rev = lambda b, i: (b * nb + nb - 1 - i, 0)
    return pl.pallas_call(
        body, name="ret_bwd", grid=(B, nb),
        in_specs=[pl.BlockSpec((LB, 6144), rev), pl.BlockSpec((LB, 2048), rev),
                  pl.BlockSpec((None, None, RET_HEADS, RET_DK, RET_DV), lambda b, i: (b, nb - 1 - i, 0, 0, 0)),
                  pl.BlockSpec((LB, 128), lambda b, i: (nb - 1 - i, 0)), pl.BlockSpec((LB, 128), lambda b, i: (nb - 1 - i, 0)),
                  pl.BlockSpec((RET_HEADS, LB, LB), const), pl.BlockSpec((RET_HEADS, LB, 1), const),
                  pl.BlockSpec((RET_HEADS, LB, 1), const), pl.BlockSpec((RET_HEADS, 1, 128), const),
                  pl.BlockSpec((1, 2048), lambda b, i: (0, 0))],
        out_specs=[pl.BlockSpec((LB, 6144), rev), pl.BlockSpec((1, 2048), lambda b, i: (0, 0))],
        out_shape=[jax.ShapeDtypeStruct((T, 6144), BF16), jax.ShapeDtypeStruct((1, 2048), F32)],
        scratch_shapes=[pltpu.VMEM((RET_HEADS, RET_DK, RET_DV), F32)], compiler_params=_params(2),
    )(proj, dy, states, cos, sin, dmat, qd, kd, bd, gn_g.reshape(1, 2048))


BIAS_LANES = 4 * ATT_BLOCK


def _diag_onehot():
    r = lax.broadcasted_iota(jnp.int32, (REL_TABLE, BIAS_LANES), 0)
    j = lax.broadcasted_iota(jnp.int32, (REL_TABLE, BIAS_LANES), 1)
    idx = jnp.maximum(j - ATT_BLOCK - PAST, -MAX_REL) + MAX_REL
    return jnp.where(idx == r, 1.0, 0.0).astype(F32)


def _row_is(j):
    return lax.broadcasted_iota(jnp.int32, (8, BIAS_LANES), 0) == j


def _att_bias(table):
    QB, KW = ATT_BLOCK, 3 * ATT_BLOCK

    def body(t_ref, b_ref, bt_ref):
        row = jnp.broadcast_to(t_ref[...], (8, REL_TABLE))
        diag = jnp.dot(row, _diag_onehot(), preferred_element_type=F32, precision=lax.Precision.HIGHEST)
        rows = jnp.zeros((8, BIAS_LANES), F32)
        for j in range(8):
            rows = jnp.where(_row_is(j), diag if j == 0 else pltpu.roll(diag, j, axis=1), rows)
        n = 8
        while n < QB:
            rows = jnp.concatenate([rows, pltpu.roll(rows, n, axis=1)], axis=0)
            n *= 2
        bias = rows[:, QB:]
        qi = lax.broadcasted_iota(jnp.int32, (QB, KW), 0)
        kj = lax.broadcasted_iota(jnp.int32, (QB, KW), 1)
        lo = (qi // CHUNK) * CHUNK
        bias = jnp.where((kj >= lo) & (kj < lo + PAST + CHUNK), bias, NEG)
        b_ref[...] = bias
        bt_ref[...] = bias.T

    return pl.pallas_call(
        body, name="att_bias", grid=(ATT_HEADS,),
        in_specs=[pl.BlockSpec((None, 1, REL_TABLE), lambda h: (h, 0, 0))],
        out_specs=[pl.BlockSpec((None, QB, KW), lambda h: (h, 0, 0)), pl.BlockSpec((None, KW, QB), lambda h: (h // 2, 0, h % 2))],
        out_shape=[jax.ShapeDtypeStruct((ATT_HEADS, QB, KW), F32), jax.ShapeDtypeStruct((ATT_HEADS // 2, KW, 2 * QB), F32)],
        compiler_params=_params(1),
    )(table.reshape(ATT_HEADS, 1, REL_TABLE))


def _att_bias_grad(dbias_t):
    QB, KW = ATT_BLOCK, 3 * ATT_BLOCK

    def body(d_ref, o_ref):
        rows = jnp.concatenate([jnp.zeros((QB, QB), F32), d_ref[...].T], axis=1)
        n = QB // 2
        while n >= 8:
            rows = rows[:n] + pltpu.roll(rows[n:], BIAS_LANES - n, axis=1)
            n //= 2
        acc = jnp.zeros((8, BIAS_LANES), F32)
        for j in range(8):
            acc = acc + jnp.where(_row_is(j), rows if j == 0 else pltpu.roll(rows, BIAS_LANES - j, axis=1), 0.0)
        diag = jnp.broadcast_to(jnp.sum(acc, axis=0, keepdims=True), (8, BIAS_LANES))
        grad = lax.dot_general(diag, _diag_onehot(), (((1,), (1,)), ((), ())), preferred_element_type=F32,
                               precision=lax.Precision.HIGHEST)
        o_ref[...] = grad[:1]

    return pl.pallas_call(
        body, name="att_bias_grad", grid=(ATT_HEADS,),
        in_specs=[pl.BlockSpec((None, KW, QB), lambda h: (h // 2, 0, h % 2))],
        out_specs=pl.BlockSpec((None, 1, REL_TABLE), lambda h: (h, 0, 0)),
        out_shape=jax.ShapeDtypeStruct((ATT_HEADS, 1, REL_TABLE), F32), compiler_params=_params(1),
    )(dbias_t).reshape(ATT_HEADS, REL_TABLE)


def _att_fwd(qkv, bias, *, B, S):
    QB = ATT_BLOCK
    nb = S // QB
    KW = 3 * QB
    T = B * S
    scale = ATT_DH ** -0.5

    def body(q_ref, k0, k1, k2, v0, v1, v2, b_ref, o_ref, lse_ref):
        i = pl.program_id(2)
        k3 = jnp.concatenate([k0[...], k1[...], k2[...]], axis=0)
        v3 = jnp.concatenate([v0[...], v1[...], v2[...]], axis=0)
        col = lax.broadcasted_iota(jnp.int32, (QB, KW), 1)
        in_seq = col >= (2 - i) * QB
        lane = lax.broadcasted_iota(jnp.int32, (QB, 128), 1)
        lane_kv = lax.broadcasted_iota(jnp.int32, (KW, 128), 1)
        q = (q_ref[...].astype(F32) * scale).astype(BF16)
        out = jnp.zeros((QB, 128), F32)
        lse = jnp.zeros((QB, 128), F32)
        for hh in range(2):
            qh = jnp.where((lane < ATT_DH) == (hh == 0), q, jnp.zeros_like(q))
            vh = jnp.where((lane_kv < ATT_DH) == (hh == 0), v3, jnp.zeros_like(v3))
            s = jnp.where(in_seq, _dot_nt(qh, k3) + b_ref[hh], NEG)
            m = jnp.max(s, axis=-1, keepdims=True)
            e = jnp.exp(s - m)
            l = jnp.sum(e, axis=-1, keepdims=True)
            out = out + _dot(e.astype(BF16), vh) / l
            lse = jnp.where(lane == hh, m + jnp.log(l), lse)
        o_ref[...] = out.astype(BF16)
        lse_ref[...] = lse.T[:8]

    def kv(d, col0):
        return pl.BlockSpec((QB, 128), lambda hp, b, i: (b * nb + jnp.maximum(i - d, 0), col0 + hp))

    return pl.pallas_call(
        body, name="att_fwd", grid=(8, B, nb),
        in_specs=[pl.BlockSpec((QB, 128), lambda hp, b, i: (b * nb + i, hp)),
                  kv(2, 8), kv(1, 8), kv(0, 8), kv(2, 16), kv(1, 16), kv(0, 16),
                  pl.BlockSpec((2, QB, KW), lambda hp, b, i: (hp, 0, 0))],
        out_specs=[pl.BlockSpec((QB, 128), lambda hp, b, i: (b * nb + i, hp)),
                   pl.BlockSpec((None, 8, QB), lambda hp, b, i: (hp, 0, b * nb + i))],
        out_shape=[jax.ShapeDtypeStruct((T, 1024), BF16), jax.ShapeDtypeStruct((8, 8, T), F32)], compiler_params=_params(3),
    )(qkv, qkv, qkv, qkv, qkv, qkv, qkv, bias)


def _att_bwd(qkv, do, o, lse, bias_t, *, B, S):
    QB = ATT_BLOCK
    nb = S // QB
    KW = 3 * QB
    T = B * S
    scale = ATT_DH ** -0.5
    TK = 256

    def body(q_ref, k0, k1, k2, v0, v1, v2, do_ref, o_ref, lse_ref, b_ref, dq_ref, dk_ref, dv_ref, db_ref, dk_acc, dv_acc):
        b, i = pl.program_id(1), pl.program_id(2)

        @pl.when(i == 0)
        def _():
            dk_acc[...] = jnp.zeros_like(dk_acc)
            dv_acc[...] = jnp.zeros_like(dv_acc)

        @pl.when((b == 0) & (i == 0))
        def _():
            db_ref[...] = jnp.zeros_like(db_ref)

        @pl.when(i < nb)
        def _():
            lane = lax.broadcasted_iota(jnp.int32, (QB, 128), 1)
            first = lane < ATT_DH

            def by_head(x):
                zero = jnp.zeros_like(x)
                return jnp.concatenate([jnp.where(first, x, zero), jnp.where(first, zero, x)], axis=0)

            dout = do_ref[...]
            q2 = by_head((q_ref[...].astype(F32) * scale).astype(BF16))
            do2 = by_head(dout)
            delta_t = (o_ref[...].astype(F32) * dout.astype(F32)).T
            delta2 = jnp.concatenate([jnp.sum(delta_t[:ATT_DH], axis=0, keepdims=True),
                                      jnp.sum(delta_t[ATT_DH:], axis=0, keepdims=True)], axis=1)
            lse2 = jnp.concatenate([lse_ref[0:1, :], lse_ref[1:2, :]], axis=1)
            dq_t = jnp.zeros((128, 2 * QB), F32)
            for d, (k_ref, v_ref) in enumerate(((k0, v0), (k1, v1), (k2, v2))):
                kblk, vblk = k_ref[...], v_ref[...]
                kt = kblk.astype(F32).T.astype(BF16)
                lse_d = jnp.where(i + d >= 2, lse2, -NEG)
                slot = (i + 1 + d) % 3
                for t in range(QB // TK):
                    rows = slice(t * TK, (t + 1) * TK)
                    wrows = slice(d * QB + t * TK, d * QB + (t + 1) * TK)
                    p = jnp.exp(_dot_nt(kblk[rows], q2) + b_ref[wrows, :] - lse_d)
                    ds = p * (_dot_nt(vblk[rows], do2) - delta2)
                    db_ref[wrows, :] += ds
                    dsb = ds.astype(BF16)
                    dk_acc[slot, rows, :] += _dot(dsb, q2)
                    dv_acc[slot, rows, :] += _dot(p.astype(BF16), do2)
                    dq_t += _dot(kt[:, rows], dsb)
            row = lax.broadcasted_iota(jnp.int32, (128, QB), 0)
            dq_ref[...] = (jnp.where(row < ATT_DH, dq_t[:, :QB], dq_t[:, QB:]) * scale).T.astype(BF16)

        @pl.when(i >= 2)
        def _():
            slot = (i + 1) % 3
            dk_ref[...] = dk_acc[slot].astype(BF16)
            dv_ref[...] = dv_acc[slot].astype(BF16)
            dk_acc[slot] = jnp.zeros((QB, 128), F32)
            dv_acc[slot] = jnp.zeros((QB, 128), F32)

    def qrow(b, i):
        return b * nb + jnp.minimum(i, nb - 1)

    def kv(d, col0):
        return pl.BlockSpec((QB, 128), lambda hp, b, i: (b * nb + jnp.maximum(jnp.minimum(i, nb - 1) - d, 0), col0 + hp))

    late = pl.BlockSpec((QB, 128), lambda hp, b, i: (b * nb + jnp.maximum(i - 2, 0), hp))
    return pl.pallas_call(
        body, name="att_bwd", grid=(8, B, nb + 2),
        in_specs=[pl.BlockSpec((QB, 128), lambda hp, b, i: (qrow(b, i), hp)),
                  kv(2, 8), kv(1, 8), kv(0, 8), kv(2, 16), kv(1, 16), kv(0, 16),
                  pl.BlockSpec((QB, 128), lambda hp, b, i: (qrow(b, i), hp)),
                  pl.BlockSpec((QB, 128), lambda hp, b, i: (qrow(b, i), hp)),
                  pl.BlockSpec((None, 8, QB), lambda hp, b, i: (hp, 0, qrow(b, i))),
                  pl.BlockSpec((None, KW, 2 * QB), lambda hp, b, i: (hp, 0, 0))],
        out_specs=[pl.BlockSpec((QB, 128), lambda hp, b, i: (qrow(b, i), hp)), late, late,
                   pl.BlockSpec((None, KW, 2 * QB), lambda hp, b, i: (hp, 0, 0))],
        out_shape=[jax.ShapeDtypeStruct((T, 1024), BF16)] * 3 + [jax.ShapeDtypeStruct((ATT_HEADS // 2, KW, 2 * QB), F32)],
        scratch_shapes=[pltpu.VMEM((3, QB, 128), F32), pltpu.VMEM((3, QB, 128), F32)], compiler_params=_params(3),
    )(qkv, qkv, qkv, qkv, qkv, qkv, qkv, do, o, lse, bias_t)


def _loss_head(h, tgt, g, *, tm):
    T, D = h.shape
    n = T // tm

    def body(h_ref, t_ref, g_ref, dh_ref, dg_ref, loss_ref, acc_ref):
        i = pl.program_id(0)
        x = h_ref[...]
        r = lax.rsqrt(jnp.mean(x * x, axis=-1, keepdims=True) + EPS)
        xh = x * r
        gg = g_ref[...]
        diff = xh * gg - t_ref[...]
        sq = jnp.sum(diff * diff, axis=0, keepdims=True)
        dy = diff * (1.0 / D)
        dg_part = jnp.sum(dy * xh, axis=0, keepdims=True)

        @pl.when(i == 0)
        def _():
            acc_ref[...] = sq
            dg_ref[...] = dg_part

        @pl.when(i > 0)
        def _():
            acc_ref[...] += sq
            dg_ref[...] += dg_part
        t = dy * gg
        dh_ref[...] = r * (t - xh * jnp.mean(t * xh, axis=-1, keepdims=True))

        @pl.when(i == n - 1)
        def _():
            loss_ref[...] = (0.5 / D) * jnp.sum(acc_ref[...], axis=1, keepdims=True)

    return pl.pallas_call(
        body, name="loss_head", grid=(n,),
        in_specs=[pl.BlockSpec((tm, D), lambda i: (i, 0)), pl.BlockSpec((tm, D), lambda i: (i, 0)),
                  pl.BlockSpec((1, D), lambda i: (0, 0))],
        out_specs=[pl.BlockSpec((tm, D), lambda i: (i, 0)), pl.BlockSpec((1, D), lambda i: (0, 0)),
                   pl.BlockSpec((1, 1), lambda i: (0, 0))],
        out_shape=[jax.ShapeDtypeStruct((T, D), F32), jax.ShapeDtypeStruct((1, D), F32), jax.ShapeDtypeStruct((1, 1), F32)],
        scratch_shapes=[pltpu.VMEM((1, D), F32)], compiler_params=_params(1),
    )(h, tgt, g.reshape(1, D))


def _tok_tile(T, want):
    t = min(T, want)
    assert T % t == 0
    return t


def _local_step(x, tgt, w, mix_g, gn_g, rel_bias, mlp_g, fin_g):
    B, S, D = x.shape
    T = B * S
    h0 = x.reshape(T, D)
    tgt = tgt.reshape(T, D)
    tm = _tok_tile(T, 1024)
    tb = _tok_tile(T, 512)
    tq = _tok_tile(T, 256)
    tt = _tok_tile(T, 2048)
    consts = _ret_consts(S, RET_BLOCK)

    proj, hn0 = _mm_nn(h0, w["ret_w_in"], "col", tm=tb, out_dtype=BF16, name="ret_in", norm_g=mix_g[0])
    y_ret, states = _ret_fwd(proj, gn_g, consts, B=B, S=S)
    h1 = _mm_nn(y_ret, w["ret_w_out"], "row", tm=tm, out_dtype=F32, name="ret_out", res=h0)
    a0, hm0 = _mm_nn(h1, w["mlp_w1_0"], "col", tm=tb, out_dtype=BF16, name="mlp0_up", norm_g=mlp_g[0], act="relu2")
    h2 = _mm_nn(a0, w["mlp_w2_0"], "row", tm=tb, out_dtype=F32, name="mlp0_down", res=h1)
    bias, bias_t = _att_bias(rel_bias)
    qkv, hn1 = _mm_nn(h2, w["att_w_in"], "col", tm=tb, out_dtype=BF16, name="att_in", norm_g=mix_g[1])
    o_att, lse = _att_fwd(qkv, bias, B=B, S=S)
    h3 = _mm_nn(o_att, w["att_w_out"], "row", tm=tm, out_dtype=F32, name="att_out", res=h2)
    a1, hm1 = _mm_nn(h3, w["mlp_w1_1"], "col", tm=tb, out_dtype=BF16, name="mlp1_up", norm_g=mlp_g[1], act="relu2")
    h4 = _mm_nn(a1, w["mlp_w2_1"], "row", tm=tb, out_dtype=F32, name="mlp1_down", res=h3)
    dh4, d_fin_g, loss = _loss_head(h4, tgt, fin_g, tm=tb)

    gw = {}
    gw["mlp_w2_1"] = _mm_tn(a1, dh4, "row", tt=tt, tk=1024, tn=D, name="d_mlp1_w2")
    dz1 = _mm_nt(dh4, w["mlp_w2_1"], "row", tm=tb, name="d_mlp1_act", epi="sqrt2", a2=a1)
    gw["mlp_w1_1"] = _mm_tn(hm1, dz1, "col", tt=tt, tk=D, tn=1024, name="d_mlp1_w1")
    dh3, d_mlp_g1 = _mm_nt(dz1, w["mlp_w1_1"], "col", tm=tb, name="d_mlp1_in", epi="normbwd",
                           h=h3, g=mlp_g[1], dres=dh4)
    gw["att_w_out"] = _mm_tn(o_att, dh3, "row", tt=tt, tk=256, tn=D, name="d_att_wout")
    do_att = _mm_nt(dh3, w["att_w_out"], "row", tm=tb, name="d_att_o", epi="bf16")
    dq, dk, dv, dbias_t = _att_bwd(qkv, do_att, o_att, lse, bias_t, B=B, S=S)
    d_rel = _att_bias_grad(dbias_t)
    dqkv = jnp.concatenate([dq, dk, dv], axis=1)
    gw["att_w_in"] = _mm_tn(hn1, dqkv, "col", tt=tt, tk=D, tn=768, name="d_att_win")
    dh2, d_mix_g1 = _mm_nt(dqkv, w["att_w_in"], "col", tm=tb, name="d_att_in", epi="normbwd",
                           h=h2, g=mix_g[1], dres=dh3)
    gw["mlp_w2_0"] = _mm_tn(a0, dh2, "row", tt=tt, tk=1024, tn=D, name="d_mlp0_w2")
    dz0 = _mm_nt(dh2, w["mlp_w2_0"], "row", tm=tb, name="d_mlp0_act", epi="sqrt2", a2=a0)
    gw["mlp_w1_0"] = _mm_tn(hm0, dz0, "col", tt=tt, tk=D, tn=1024, name="d_mlp0_w1")
    dh1, d_mlp_g0 = _mm_nt(dz0, w["mlp_w1_0"], "col", tm=tb, name="d_mlp0_in", epi="normbwd",
                           h=h1, g=mlp_g[0], dres=dh2)
    gw["ret_w_out"] = _mm_tn(y_ret, dh1, "row", tt=tt, tk=512, tn=D, name="d_ret_wout")
    dy_ret = _mm_nt(dh1, w["ret_w_out"], "row", tm=tb, name="d_ret_y", epi="bf16")
    dproj, d_gn = _ret_bwd(proj, dy_ret, states, gn_g, consts, B=B, S=S)
    gw["ret_w_in"] = _mm_tn(hn0, dproj, "col", tt=tt, tk=D, tn=768, name="d_ret_win")
    dx, d_mix_g0 = _mm_nt(dproj, w["ret_w_in"], "col", tm=tq, name="d_ret_in", epi="normbwd",
                          h=h0, g=mix_g[0], dres=dh1)
    small = [d_mix_g0, d_mix_g1, d_mlp_g0, d_mlp_g1, d_fin_g, d_gn.reshape(2, D), d_rel.reshape(5, D)]
    return loss.reshape(()), dx.reshape(B, S, D), gw, small


def _row_tile(r, want=256):
    t = min(r, want)
    assert r % t == 0
    return t


def _into_slab(place, a, dtype, name):
    r, c = a.shape
    tr = _row_tile(r)

    def body(place_ref, a_ref, o_ref):
        o_ref[...] = a_ref[...].astype(dtype)

    grid_spec = pltpu.PrefetchScalarGridSpec(
        num_scalar_prefetch=1, grid=(r // tr,), in_specs=[pl.BlockSpec((tr, c), lambda i, pr: (i, 0))],
        out_specs=pl.BlockSpec((None, tr, c), lambda i, pr: (pr[0], i, 0)),
    )
    return pl.pallas_call(
        body, name=name, grid_spec=grid_spec, out_shape=jax.ShapeDtypeStruct((N_CHIPS, r, c), dtype), compiler_params=_params(1),
    )(place, a)


def _add_sibling(place, g, recv, name):
    _, r, c = g.shape
    hr = r // 2
    tr = _row_tile(hr)
    nrt = hr // tr

    def body(place_ref, g_ref, r_ref, sb_ref, own_ref):
        v = g_ref[...] + r_ref[...]
        sb_ref[...] = v.astype(BF16)

        @pl.when(pl.program_id(1) == place_ref[0])
        def _():
            own_ref[...] = v

    grid_spec = pltpu.PrefetchScalarGridSpec(
        num_scalar_prefetch=1, grid=(nrt, N_CHIPS),
        in_specs=[pl.BlockSpec((None, tr, c), lambda i, s, pr: (s, pr[1] * nrt + i, 0)),
                  pl.BlockSpec((None, tr, c), lambda i, s, pr: (s, i, 0))],
        out_specs=[pl.BlockSpec((None, tr, c), lambda i, s, pr: (s, i, 0)), pl.BlockSpec((tr, c), lambda i, s, pr: (i, 0))],
    )
    return pl.pallas_call(
        body, name=name, grid_spec=grid_spec,
        out_shape=[jax.ShapeDtypeStruct((N_CHIPS, hr, c), BF16), jax.ShapeDtypeStruct((hr, c), F32)],
        compiler_params=_params(2),
    )(place, g, recv)


def _add_chips(place, own, recv, name):
    hr, c = own.shape
    tr = _row_tile(hr)
    nrt = hr // tr

    def body(place_ref, o_ref, r_ref, t_ref):
        t_ref[...] = ((o_ref[...] + r_ref[0].astype(F32)) + r_ref[1].astype(F32)) + r_ref[2].astype(F32)

    grid_spec = pltpu.PrefetchScalarGridSpec(
        num_scalar_prefetch=1, grid=(nrt,),
        in_specs=[pl.BlockSpec((tr, c), lambda i, pr: (i, 0)), pl.BlockSpec((3, tr, c), lambda i, pr: (0, i, 0))],
        out_specs=pl.BlockSpec((tr, c), lambda i, pr: (pr[1] * nrt + i, 0)),
    )
    return pl.pallas_call(
        body, name=name, grid_spec=grid_spec, out_shape=jax.ShapeDtypeStruct((2 * hr, c), F32), compiler_params=_params(1),
    )(place, own, recv)


def _adamw(w, g, m, v, name):
    r, c = w.shape
    tr = _row_tile(r)

    def body(w_ref, g_ref, m_ref, v_ref, d_ref, nm_ref, nv_ref):
        gg = g_ref[...]
        nm = ADAM_B1 * m_ref[...] + (1.0 - ADAM_B1) * gg
        nv = ADAM_B2 * v_ref[...] + (1.0 - ADAM_B2) * (gg * gg)
        m_hat = nm / (1.0 - ADAM_B1 ** ADAM_STEP)
        v_hat = nv / (1.0 - ADAM_B2 ** ADAM_STEP)
        d_ref[...] = -ADAM_LR * (m_hat / (jnp.sqrt(v_hat) + ADAM_EPS) + ADAM_WD * w_ref[...])
        nm_ref[...] = nm
        nv_ref[...] = nv

    spec = pl.BlockSpec((tr, c), lambda i: (i, 0))
    return pl.pallas_call(
        body, name=name, grid=(r // tr,), in_specs=[spec] * 4, out_specs=[spec] * 3,
        out_shape=[jax.ShapeDtypeStruct((r, c), F32)] * 3, compiler_params=_params(1),
    )(w, g, m, v)


HBM_SPEC = pl.BlockSpec(memory_space=pltpu.HBM)


def _place():
    return lax.axis_index("x"), lax.axis_index("y"), lax.axis_index("c")


def _other_chips(x, y):
    return [(1 - x, y), (x, 1 - y), (1 - x, 1 - y)]


def _remote(src, dst, ssem, rsem, dev):
    return pltpu.make_async_remote_copy(src_ref=src, dst_ref=dst, send_sem=ssem, recv_sem=rsem, device_id=dev,
                                        device_id_type=MESH)


def _all_gather_slabs(slabs):
    n = len(slabs)

    def body(*refs):
        outs = refs[n:2 * n]
        s1, r1, s2, r2 = refs[2 * n:]
        x, y, c = _place()
        p = 2 * x + y
        chips = _other_chips(x, y)
        cidx = [2 * qx + qy for qx, qy in chips]
        pending = []
        for t in range(n):
            hr = outs[t].shape[1] // 2
            mine = pl.ds(c * hr, hr)
            for j, (qx, qy) in enumerate(chips):
                cp = _remote(outs[t].at[p, mine], outs[t].at[p, mine], s1.at[t, j], r1.at[t, j], (qx, qy, c))
                cp.start()
                pending.append(cp)
        for t in range(n):
            hr = outs[t].shape[1] // 2
            mine = pl.ds(c * hr, hr)
            for j, (qx, qy) in enumerate(chips):
                got = outs[t].at[cidx[j], mine]
                _remote(got, got, s1.at[t, j], r1.at[t, j], (qx, qy, c)).wait_recv()
                fw = _remote(got, got, s2.at[t, j], r2.at[t, j], (x, y, 1 - c))
                fw.start()
                pending.append(fw)
        for t in range(n):
            hr = outs[t].shape[1] // 2
            theirs = pl.ds((1 - c) * hr, hr)
            for j in range(3):
                got = outs[t].at[cidx[j], theirs]
                _remote(got, got, s2.at[t, j], r2.at[t, j], (x, y, 1 - c)).wait_recv()
        for cp in pending:
            cp.wait_send()

    sem = pltpu.SemaphoreType.DMA
    return pl.pallas_call(
        body, name="gather_weights", in_specs=[HBM_SPEC] * n, out_specs=[HBM_SPEC] * n,
        out_shape=[jax.ShapeDtypeStruct(s.shape, s.dtype) for s in slabs], input_output_aliases={t: t for t in range(n)},
        scratch_shapes=[sem((n, 3)), sem((n, 3)), sem((n, 3)), sem((n, 3))],
        compiler_params=pltpu.CompilerParams(has_side_effects=True),
    )(*slabs)


def _exchange_with_sibling(grads):
    n = len(grads)

    def body(*refs):
        ins, outs = refs[:n], refs[n:2 * n]
        ssem, rsem = refs[2 * n:]
        x, y, c = _place()
        copies = []
        for t in range(n):
            hr = ins[t].shape[1] // 2
            cp = _remote(ins[t].at[:, pl.ds((1 - c) * hr, hr), :], outs[t], ssem.at[t], rsem.at[t], (x, y, 1 - c))
            cp.start()
            copies.append(cp)
        for cp in copies:
            cp.wait()

    sem = pltpu.SemaphoreType.DMA
    return pl.pallas_call(
        body, name="grads_to_sibling", in_specs=[HBM_SPEC] * n, out_specs=[HBM_SPEC] * n,
        out_shape=[jax.ShapeDtypeStruct((N_CHIPS, g.shape[1] // 2, g.shape[2]), g.dtype) for g in grads],
        scratch_shapes=[sem((n,)), sem((n,))], compiler_params=pltpu.CompilerParams(has_side_effects=True),
    )(*grads)


def _exchange_with_chips(sums):
    n = len(sums)

    def body(*refs):
        ins, outs = refs[:n], refs[n:2 * n]
        ssem, rsem = refs[2 * n:]
        x, y, c = _place()
        chips = _other_chips(x, y)
        copies = []
        for t in range(n):
            for j, (qx, qy) in enumerate(chips):
                cp = _remote(ins[t].at[2 * qx + qy], outs[t].at[j], ssem.at[t, j], rsem.at[t, j], (qx, qy, c))
                cp.start()
                copies.append(cp)
        for cp in copies:
            cp.wait()

    sem = pltpu.SemaphoreType.DMA
    return pl.pallas_call(
        body, name="grads_to_chips", in_specs=[HBM_SPEC] * n, out_specs=[HBM_SPEC] * n,
        out_shape=[jax.ShapeDtypeStruct((3,) + s.shape[1:], s.dtype) for s in sums],
        scratch_shapes=[sem((n, 3)), sem((n, 3))], compiler_params=pltpu.CompilerParams(has_side_effects=True),
    )(*sums)


def _share_with_sibling(shards):
    n = len(shards)

    def body(*refs):
        outs = refs[n:2 * n]
        ssem, rsem = refs[2 * n:]
        x, y, c = _place()
        copies = []
        for t in range(n):
            hr = outs[t].shape[0] // 2
            mine = outs[t].at[pl.ds(c * hr, hr)]
            cp = _remote(mine, mine, ssem.at[t], rsem.at[t], (x, y, 1 - c))
            cp.start()
            copies.append(cp)
        for t in range(n):
            hr = outs[t].shape[0] // 2
            theirs = outs[t].at[pl.ds((1 - c) * hr, hr)]
            _remote(theirs, theirs, ssem.at[t], rsem.at[t], (x, y, 1 - c)).wait_recv()
        for cp in copies:
            cp.wait_send()

    sem = pltpu.SemaphoreType.DMA
    return pl.pallas_call(
        body, name="grads_share", in_specs=[HBM_SPEC] * n, out_specs=[HBM_SPEC] * n,
        out_shape=[jax.ShapeDtypeStruct(s.shape, s.dtype) for s in shards], input_output_aliases={t: t for t in range(n)},
        scratch_shapes=[sem((n,)), sem((n,))], compiler_params=pltpu.CompilerParams(has_side_effects=True),
    )(*shards)


def _all_reduce_small(buf):
    R, C = buf.shape

    def body(in_ref, out_ref, gather, ssem, rsem):
        x, y, c = _place()
        me = 4 * x + 2 * y + c
        gather[me] = in_ref[...]
        flips = [(fx, fy, fc) for fx in (0, 1) for fy in (0, 1) for fc in (0, 1) if fx or fy or fc]
        peers = [(x + fx - 2 * x * fx, y + fy - 2 * y * fy, c + fc - 2 * c * fc) for fx, fy, fc in flips]
        copies = [_remote(in_ref, gather.at[me], ssem.at[k], rsem.at[k], peer) for k, peer in enumerate(peers)]
        for cp in copies:
            cp.start()
        for k, (px, py, pc) in enumerate(peers):
            _remote(in_ref, gather.at[4 * px + 2 * py + pc], ssem.at[k], rsem.at[k], (px, py, pc)).wait_recv()
        for cp in copies:
            cp.wait_send()
        acc = gather[0]
        for d in range(1, 8):
            acc = acc + gather[d]
        out_ref[...] = acc

    sem = pltpu.SemaphoreType.DMA
    vmem = pl.BlockSpec(memory_space=pltpu.VMEM)
    return pl.pallas_call(
        body, name="small_grads_sum", in_specs=[vmem], out_specs=vmem, out_shape=jax.ShapeDtypeStruct((R, C), F32),
        scratch_shapes=[pltpu.VMEM((8, R, C), F32), sem((7,)), sem((7,))],
        compiler_params=pltpu.CompilerParams(has_side_effects=True),
    )(buf)


BIG = ["ret_w_in", "ret_w_out", "att_w_in", "att_w_out", "mlp_w1_0", "mlp_w1_1", "mlp_w2_0", "mlp_w2_1"]


def _split_big(ret_w_in, ret_w_out, att_w_in, att_w_out, mlp_w1, mlp_w2):
    return {"ret_w_in": ret_w_in[0], "ret_w_out": ret_w_out[0], "att_w_in": att_w_in[0], "att_w_out": att_w_out[0],
            "mlp_w1_0": mlp_w1[0], "mlp_w1_1": mlp_w1[1], "mlp_w2_0": mlp_w2[0], "mlp_w2_1": mlp_w2[1]}


def kernel(x, mix_norm_g, ret_w_in, ret_gn_g, ret_w_out, att_w_in, att_rel_bias, att_w_out, mlp_norm_g, mlp_w1, mlp_w2, final_norm_g, loss_target, m_mix_norm_g, m_ret_w_in, m_ret_gn_g, m_ret_w_out, m_att_w_in, m_att_rel_bias, m_att_w_out, m_mlp_norm_g, m_mlp_w1, m_mlp_w2, m_final_norm_g, v_mix_norm_g, v_ret_w_in, v_ret_gn_g, v_ret_w_out, v_att_w_in, v_att_rel_bias, v_att_w_out, v_mlp_norm_g, v_mlp_w1, v_mlp_w2, v_final_norm_g):
    xi, yi, ci = _place()
    chip = 2 * xi + yi
    w32 = _split_big(ret_w_in, ret_w_out, att_w_in, att_w_out, mlp_w1, mlp_w2)
    m32 = _split_big(m_ret_w_in, m_ret_w_out, m_att_w_in, m_att_w_out, m_mlp_w1, m_mlp_w2)
    v32 = _split_big(v_ret_w_in, v_ret_w_out, v_att_w_in, v_att_w_out, v_mlp_w1, v_mlp_w2)

    place = jnp.stack([chip, ci]).astype(jnp.int32)
    gathered = _all_gather_slabs([_into_slab(place, w32[n], BF16, name="cast_" + n) for n in BIG]
                                 + [_into_slab(place, att_rel_bias[0], F32, name="slab_rel_bias")])
    w = dict(zip(BIG, gathered[:-1]))
    rel_full = jnp.transpose(gathered[-1], (1, 0, 2)).reshape(ATT_HEADS, REL_TABLE)

    loss_local, grad_x, gw, small = _local_step(x, loss_target, w, mix_norm_g, ret_gn_g[0], rel_full, mlp_norm_g, final_norm_g)
    loss = lax.psum(loss_local, ("x", "y", "c"))

    from_sibling = _exchange_with_sibling([gw[n] for n in BIG])
    sums = [_add_sibling(place, gw[n], r, name="chip_sum_" + n) for n, r in zip(BIG, from_sibling)]
    from_chips = _exchange_with_chips([s[0] for s in sums])
    totals = [_add_chips(place, s[1], r, name="total_" + n) for n, s, r in zip(BIG, sums, from_chips)]
    g_big = dict(zip(BIG, _share_with_sibling(totals)))

    rows, at = jnp.zeros((16, D_MODEL), F32), 0
    for part in small:
        rows = rows + jnp.pad(part, ((at, 16 - at - part.shape[0]), (0, 0)))
        at += part.shape[0]
    rows = _all_reduce_small(rows)
    g_small = {"mix_norm_g": rows[0:2], "mlp_norm_g": rows[2:4], "final_norm_g": rows[4:5], "ret_gn_g": rows[5:7].reshape(1, 2048),
               "att_rel_bias": lax.dynamic_slice_in_dim(rows[7:12].reshape(ATT_HEADS, REL_TABLE), chip * (REL_TABLE // N_CHIPS),
                                                        REL_TABLE // N_CHIPS, axis=1)}

    upd = {n: _adamw(w32[n], g_big[n], m32[n], v32[n], name="adamw_" + n) for n in BIG}
    small_in = {"mix_norm_g": (mix_norm_g, m_mix_norm_g, v_mix_norm_g), "ret_gn_g": (ret_gn_g, m_ret_gn_g, v_ret_gn_g),
                "att_rel_bias": (att_rel_bias[0], m_att_rel_bias[0], v_att_rel_bias[0]),
                "mlp_norm_g": (mlp_norm_g, m_mlp_norm_g, v_mlp_norm_g),
                "final_norm_g": (final_norm_g.reshape(1, -1), m_final_norm_g.reshape(1, -1), v_final_norm_g.reshape(1, -1))}
    upd_small = {n: _adamw(wv, g_small[n], mv, vv, name="adamw_" + n) for n, (wv, mv, vv) in small_in.items()}

    def big_out(k):
        def get(n):
            return g_big[n] if k is None else upd[n][k]
        return {"ret_w_in": get("ret_w_in")[None], "ret_w_out": get("ret_w_out")[None], "att_w_in": get("att_w_in")[None],
                "att_w_out": get("att_w_out")[None], "mlp_w1": jnp.stack([get("mlp_w1_0"), get("mlp_w1_1")]),
                "mlp_w2": jnp.stack([get("mlp_w2_0"), get("mlp_w2_1")])}

    def small_out(k):
        def get(n):
            return g_small[n] if k is None else upd_small[n][k]
        return {"mix_norm_g": get("mix_norm_g"), "ret_gn_g": get("ret_gn_g"), "att_rel_bias": get("att_rel_bias")[None],
                "mlp_norm_g": get("mlp_norm_g"), "final_norm_g": get("final_norm_g").reshape(-1)}

    order = ["mix_norm_g", "ret_w_in", "ret_gn_g", "ret_w_out", "att_w_in", "att_rel_bias", "att_w_out", "mlp_norm_g",
             "mlp_w1", "mlp_w2", "final_norm_g"]
    outs = [loss, grad_x]
    for k in (None, 0, 1, 2):
        both = {**big_out(k), **small_out(k)}
        outs += [both[n] for n in order]
    return tuple(outs)
```

```python
import jax
import jax.numpy as jnp
from jax import lax
from jax.experimental import pallas as pl
from jax.experimental.pallas import tpu as pltpu

F32 = jnp.float32
BF16 = jnp.bfloat16
MESH = pl.DeviceIdType.MESH

D_MODEL = 1024
CHUNK = 64
RET_HEADS = 4
RET_DK = 256
RET_DV = 512
ROPE_BASE = 10000.0
ATT_HEADS = 16
ATT_DH = 64
PAST = 512
MAX_REL = 256
REL_TABLE = MAX_REL + CHUNK
EPS = 1e-6
NEG = -1e30
N_CHIPS = 4

ADAM_LR = 0.001
ADAM_B1 = 0.9
ADAM_B2 = 0.999
ADAM_EPS = 1e-08
ADAM_WD = 0.01
ADAM_STEP = 10

RET_BLOCK = 256
ATT_BLOCK = 256
VMEM_LIMIT = 56 * 1024 * 1024


def _params(n_axes, **kw):
    return pltpu.CompilerParams(dimension_semantics=("arbitrary",) * n_axes, vmem_limit_bytes=VMEM_LIMIT, **kw)


def _dot(a, b):
    return jnp.dot(a, b, preferred_element_type=F32)


def _dot_nt(a, b):
    return lax.dot_general(a, b, (((1,), (1,)), ((), ())), preferred_element_type=F32)


def _dot_tn(a, b):
    return lax.dot_general(a, b, (((0,), (0,)), ((), ())), preferred_element_type=F32)


def _sigmoid(x):
    return 1.0 / (1.0 + jnp.exp(-x))


HBM_SPEC = pl.BlockSpec(memory_space=pltpu.HBM)


class _Carry:
    def __init__(self, arrays, sems, stages):
        self.arrays, self.sems, self.stages = list(arrays), list(sems), list(stages)


def _carry_call(body, carry, *, name, steps, in_specs, out_specs, out_shape, args):
    if carry is None:
        outs = pl.pallas_call(body, name=name, grid=(steps,), in_specs=in_specs, out_specs=out_specs, out_shape=out_shape,
                              compiler_params=_params(1))(*args)
        return list(outs), []
    n_in, n_out, n_c = len(in_specs), len(out_specs), len(carry.arrays)
    assert all(-steps <= at < steps for at, _ in carry.stages)

    def carrying(*refs):
        ins, outs = refs[:n_in], refs[n_in + n_c:n_in + n_c + n_out]
        carried = refs[n_in + n_c + n_out:n_in + 2 * n_c + n_out]
        sems = refs[n_in + 2 * n_c + n_out:]
        step = pl.program_id(0)
        for at, fn in carry.stages:
            if at == 0:
                pl.when(step == 0)(lambda fn=fn: fn(carried, sems))
        body(*ins, *outs)
        for at, fn in carry.stages:
            if at != 0:
                pl.when(step == at % steps)(lambda fn=fn: fn(carried, sems))

    outs = pl.pallas_call(
        carrying, name=name, grid=(steps,), in_specs=list(in_specs) + [HBM_SPEC] * n_c, out_specs=list(out_specs) + [HBM_SPEC] * n_c,
        out_shape=list(out_shape) + [jax.ShapeDtypeStruct(a.shape, a.dtype) for a in carry.arrays],
        input_output_aliases={n_in + t: n_out + t for t in range(n_c)}, scratch_shapes=carry.sems,
        compiler_params=_params(1, has_side_effects=True),
    )(*args, *carry.arrays)
    return list(outs[:n_out]), list(outs[n_out:])


def _mm_nn(a, w, wkind, *, tm, out_dtype, name, norm_g=None, act=None, res=None, carry=None):
    M, K = a.shape
    cols = w.shape[2]
    N = N_CHIPS * cols if wkind == "col" else cols
    has_norm = norm_g is not None
    assert M % tm == 0 and K == (w.shape[1] if wkind == "col" else N_CHIPS * w.shape[1])

    def body(*refs):
        it = iter(refs)
        a_ref, w_ref = next(it), next(it)
        g_ref = next(it) if has_norm else None
        r_ref = next(it) if res is not None else None
        o_ref = next(it)
        hn_ref = next(it) if has_norm else None
        if has_norm:
            x = a_ref[...].astype(F32)
            r = lax.rsqrt(jnp.mean(x * x, axis=-1, keepdims=True) + EPS)
            lhs = (x * r * g_ref[...]).astype(BF16)
            hn_ref[...] = lhs
        else:
            lhs = a_ref[...].astype(BF16)

        def finish(acc, sl):
            if act == "relu2":
                u = jnp.maximum(acc, 0.0)
                acc = u * u
            if r_ref is not None:
                acc = acc + r_ref[:, sl]
            o_ref[:, sl] = acc.astype(out_dtype)

        if wkind == "col":
            for s in range(N_CHIPS):
                finish(_dot(lhs, w_ref[s]), slice(s * cols, (s + 1) * cols))
        else:
            finish(_dot(lhs, w_ref[...].reshape(K, N)), slice(None))

    in_specs = [pl.BlockSpec((tm, K), lambda i: (i, 0)), pl.BlockSpec(w.shape, lambda i: (0, 0, 0))]
    args = [a, w]
    if has_norm:
        in_specs.append(pl.BlockSpec((1, K), lambda i: (0, 0)))
        args.append(norm_g.reshape(1, K))
    if res is not None:
        in_specs.append(pl.BlockSpec((tm, N), lambda i: (i, 0)))
        args.append(res)
    out_shape = [jax.ShapeDtypeStruct((M, N), out_dtype)]
    out_specs = [pl.BlockSpec((tm, N), lambda i: (i, 0))]
    if has_norm:
        out_shape.append(jax.ShapeDtypeStruct((M, K), BF16))
        out_specs.append(pl.BlockSpec((tm, K), lambda i: (i, 0)))
    outs, carried = _carry_call(body, carry, name=name, steps=M // tm, in_specs=in_specs, out_specs=out_specs,
                                out_shape=out_shape, args=args)
    result = outs if has_norm else outs[0]
    return result if carry is None else (result, carried)


def _mm_nt(a, w, wkind, *, tm, name, epi, a2=None, h=None, g=None, dres=None, carry=None):
    M, Nw = a.shape
    rows, cols = w.shape[1], w.shape[2]
    Kw = rows if wkind == "col" else N_CHIPS * rows
    assert M % tm == 0 and Nw == (N_CHIPS * cols if wkind == "col" else cols)
    assert epi != "normbwd" or wkind == "col"

    def body(*refs):
        it = iter(refs)
        a_ref, w_ref = next(it), next(it)
        a2_ref = next(it) if epi == "sqrt2" else None
        if epi == "normbwd":
            h_ref, g_ref, dres_ref = next(it), next(it), next(it)
        o_ref = next(it)
        dg_ref = next(it) if epi == "normbwd" else None
        i = pl.program_id(0)

        def finish(acc, sl):
            if epi == "bf16":
                o_ref[:, sl] = acc.astype(BF16)
            elif epi == "sqrt2":
                o_ref[:, sl] = (acc * (2.0 * jnp.sqrt(a2_ref[:, sl].astype(F32)))).astype(BF16)
            else:
                x = h_ref[...]
                r = lax.rsqrt(jnp.mean(x * x, axis=-1, keepdims=True) + EPS)
                xh = x * r
                dg_part = jnp.sum(acc * xh, axis=0, keepdims=True)

                @pl.when(i == 0)
                def _():
                    dg_ref[...] = dg_part

                @pl.when(i > 0)
                def _():
                    dg_ref[...] += dg_part
                t = acc * g_ref[...]
                dx = r * (t - xh * jnp.mean(t * xh, axis=-1, keepdims=True))
                o_ref[...] = dres_ref[...] + dx

        if wkind == "col":
            acc = _dot_nt(a_ref[:, :cols].astype(BF16), w_ref[0])
            for s in range(1, N_CHIPS):
                acc += _dot_nt(a_ref[:, s * cols:(s + 1) * cols].astype(BF16), w_ref[s])
            finish(acc, slice(None))
        else:
            lhs = a_ref[...].astype(BF16)
            for s in range(N_CHIPS):
                finish(_dot_nt(lhs, w_ref[s]), slice(s * rows, (s + 1) * rows))

    in_specs = [pl.BlockSpec((tm, Nw), lambda i: (i, 0)), pl.BlockSpec(w.shape, lambda i: (0, 0, 0))]
    args = [a, w]
    out_dtype = BF16
    if epi == "sqrt2":
        in_specs.append(pl.BlockSpec((tm, Kw), lambda i: (i, 0)))
        args.append(a2)
    if epi == "normbwd":
        in_specs += [pl.BlockSpec((tm, Kw), lambda i: (i, 0)), pl.BlockSpec((1, Kw), lambda i: (0, 0)),
                     pl.BlockSpec((tm, Kw), lambda i: (i, 0))]
        args += [h, g.reshape(1, Kw), dres]
        out_dtype = F32
    out_shape = [jax.ShapeDtypeStruct((M, Kw), out_dtype)]
    out_specs = [pl.BlockSpec((tm, Kw), lambda i: (i, 0))]
    if epi == "normbwd":
        out_shape.append(jax.ShapeDtypeStruct((1, Kw), F32))
        out_specs.append(pl.BlockSpec((1, Kw), lambda i: (0, 0)))
    outs, carried = _carry_call(body, carry, name=name, steps=M // tm, in_specs=in_specs, out_specs=out_specs,
                                out_shape=out_shape, args=args)
    result = outs if epi == "normbwd" else outs[0]
    return result if carry is None else (result, carried)


def _mm_tn(a, b, okind, *, tt, tk, tn, name):
    T, K = a.shape
    N = b.shape[1]
    assert T % tt == 0 and K % tk == 0 and N % tn == 0
    nt = T // tt
    if okind == "col":
        per = (N // N_CHIPS) // tn
        assert (N // N_CHIPS) % tn == 0
        out_shape = jax.ShapeDtypeStruct((N_CHIPS, K, N // N_CHIPS), F32)
        out_spec = pl.BlockSpec((None, tk, tn), lambda ki, nj, t: (nj // per, ki, nj % per))
    else:
        per = (K // N_CHIPS) // tk
        assert (K // N_CHIPS) % tk == 0
        out_shape = jax.ShapeDtypeStruct((N_CHIPS, K // N_CHIPS, N), F32)
        out_spec = pl.BlockSpec((None, tk, tn), lambda ki, nj, t: (ki // per, ki % per, nj))

    def body(a_ref, b_ref, o_ref):
        t = pl.program_id(2)
        part = _dot_tn(a_ref[...].astype(BF16), b_ref[...].astype(BF16))

        @pl.when(t == 0)
        def _():
            o_ref[...] = part

        @pl.when(t > 0)
        def _():
            o_ref[...] += part

    return pl.pallas_call(
        body, name=name, grid=(K // tk, N // tn, nt),
        in_specs=[pl.BlockSpec((tt, tk), lambda ki, nj, t: (t, ki)), pl.BlockSpec((tt, tn), lambda ki, nj, t: (t, nj))],
        out_specs=out_spec, out_shape=out_shape, compiler_params=_params(3),
    )(a, b)


def _ret_consts(S, LB):
    log_gamma = jnp.log1p(-jnp.exp2(-5.0 - jnp.arange(RET_HEADS, dtype=F32)))
    idx = jnp.arange(LB, dtype=F32)
    n, m = idx[:, None], idx[None, :]
    cn, cm = jnp.floor(n / CHUNK), jnp.floor(m / CHUNK)
    dist = jnp.where(cm == cn, jnp.abs(n - m), n - m)
    dmat = jnp.where((cm <= cn)[None], jnp.exp(log_gamma[:, None, None] * dist[None]), 0.0)
    qd = jnp.exp(log_gamma[:, None] * (idx + 1.0)[None, :])[..., None]
    kd = jnp.exp(log_gamma[:, None] * (LB - 1 - idx)[None, :])[..., None]
    bd = jnp.exp(log_gamma * LB).reshape(RET_HEADS, 1, 1) * jnp.ones((RET_HEADS, 1, 128), F32)
    half = RET_DK // 2
    inv = jnp.exp(-jnp.log(ROPE_BASE) * jnp.arange(half, dtype=F32) / half)
    ang = jnp.arange(S, dtype=F32)[:, None] * inv[None, :]
    return dmat.astype(F32), qd.astype(F32), kd.astype(F32), bd, jnp.cos(ang), jnp.sin(ang)


def _rope(t, c, s):
    t1, t2 = t[:, :128], t[:, 128:]
    return jnp.concatenate([t1 * c - t2 * s, t1 * s + t2 * c], axis=-1)


def _rope_inv(d, c, s):
    d1, d2 = d[:, :128], d[:, 128:]
    return jnp.concatenate([d1 * c + d2 * s, d2 * c - d1 * s], axis=-1)


def _ret_block_fwd(p_ref, h, c, s, d_ref, qd_ref, kd_ref, stb):
    q = _rope(p_ref[:, h * RET_DK:(h + 1) * RET_DK].astype(F32), c, s)
    k = _rope(p_ref[:, 1024 + h * RET_DK:1024 + (h + 1) * RET_DK].astype(F32), c, s) * (RET_DK ** -0.5)
    v = p_ref[:, 2048 + h * RET_DV:2048 + (h + 1) * RET_DV]
    qb, kb = q.astype(BF16), k.astype(BF16)
    scb = (_dot_nt(qb, kb) * d_ref[h]).astype(BF16)
    o = _dot(scb, v) + qd_ref[h] * _dot(qb, stb)
    return q, k, qb, kb, v, scb, o


def _ret_fwd(proj, gn_g, consts, *, B, S):
    LB = RET_BLOCK
    nb = S // LB
    T = B * S
    dmat, qd, kd, bd, cos, sin = consts

    def body(p_ref, cos_ref, sin_ref, d_ref, qd_ref, kd_ref, bd_ref, gng_ref, y_ref, st_ref, state_s):
        i = pl.program_id(1)

        @pl.when(i == 0)
        def _():
            state_s[...] = jnp.zeros_like(state_s)
        c, s = cos_ref[...], sin_ref[...]
        for h in range(RET_HEADS):
            st = state_s[h]
            stb = st.astype(BF16)
            st_ref[h] = stb
            q, k, qb, kb, v, scb, o = _ret_block_fwd(p_ref, h, c, s, d_ref, qd_ref, kd_ref, stb)
            kdk = (k * kd_ref[h]).astype(BF16)
            state_s[h] = st * bd_ref[h][:, :1] + _dot_tn(kdk, v)
            gate = p_ref[:, 4096 + h * RET_DV:4096 + (h + 1) * RET_DV].astype(F32)
            mu = jnp.mean(o, axis=-1, keepdims=True)
            oc = o - mu
            xh = oc * lax.rsqrt(jnp.mean(oc * oc, axis=-1, keepdims=True) + EPS)
            y = (gate * _sigmoid(gate)) * (xh * gng_ref[:, h * RET_DV:(h + 1) * RET_DV])
            y_ref[:, h * RET_DV:(h + 1) * RET_DV] = y.astype(BF16)

    const = lambda b, i: (0, 0, 0)
    return pl.pallas_call(
        body, name="ret_fwd", grid=(B, nb),
        in_specs=[pl.BlockSpec((LB, 6144), lambda b, i: (b * nb + i, 0)),
                  pl.BlockSpec((LB, 128), lambda b, i: (i, 0)), pl.BlockSpec((LB, 128), lambda b, i: (i, 0)),
                  pl.BlockSpec((RET_HEADS, LB, LB), const), pl.BlockSpec((RET_HEADS, LB, 1), const),
                  pl.BlockSpec((RET_HEADS, LB, 1), const), pl.BlockSpec((RET_HEADS, 1, 128), const),
                  pl.BlockSpec((1, 2048), lambda b, i: (0, 0))],
        out_specs=[pl.BlockSpec((LB, 2048), lambda b, i: (b * nb + i, 0)),
                   pl.BlockSpec((None, None, RET_HEADS, RET_DK, RET_DV), lambda b, i: (b, i, 0, 0, 0))],
        out_shape=[jax.ShapeDtypeStruct((T, 2048), BF16), jax.ShapeDtypeStruct((B, nb, RET_HEADS, RET_DK, RET_DV), BF16)],
        scratch_shapes=[pltpu.VMEM((RET_HEADS, RET_DK, RET_DV), F32)], compiler_params=_params(2),
    )(proj, cos, sin, dmat, qd, kd, bd, gn_g.reshape(1, 2048))


def _ret_bwd(proj, dy, states, gn_g, consts, *, B, S):
    LB = RET_BLOCK
    nb = S // LB
    T = B * S
    dmat, qd, kd, bd, cos, sin = consts

    def body(p_ref, dy_ref, st_ref, cos_ref, sin_ref, d_ref, qd_ref, kd_ref, bd_ref, gng_ref, dp_ref, dgn_ref, dstate_s):
        b, i = pl.program_id(0), pl.program_id(1)

        @pl.when(i == 0)
        def _():
            dstate_s[...] = jnp.zeros_like(dstate_s)

        @pl.when((b == 0) & (i == 0))
        def _():
            dgn_ref[...] = jnp.zeros_like(dgn_ref)
        c, s = cos_ref[...], sin_ref[...]
        for h in range(RET_HEADS):
            vs = slice(h * RET_DV, (h + 1) * RET_DV)
            stb = st_ref[h]
            q, k, qb, kb, v, scb, o = _ret_block_fwd(p_ref, h, c, s, d_ref, qd_ref, kd_ref, stb)
            gate = p_ref[:, 4096 + h * RET_DV:4096 + (h + 1) * RET_DV].astype(F32)
            mu = jnp.mean(o, axis=-1, keepdims=True)
            oc = o - mu
            rstd = lax.rsqrt(jnp.mean(oc * oc, axis=-1, keepdims=True) + EPS)
            xh = oc * rstd
            gng = gng_ref[:, vs]
            dyh = dy_ref[:, vs].astype(F32)
            sg = _sigmoid(gate)
            silu = gate * sg
            dgn_ref[:, vs] += jnp.sum(dyh * silu * xh, axis=0, keepdims=True)
            dxh = dyh * silu * gng
            do = rstd * (dxh - jnp.mean(dxh, axis=-1, keepdims=True) - xh * jnp.mean(dxh * xh, axis=-1, keepdims=True))
            dgate = dyh * xh * gng * (sg * (1.0 + gate * (1.0 - sg)))
            dob = do.astype(BF16)
            dsb = (_dot_nt(dob, v) * d_ref[h]).astype(BF16)
            dst = dstate_s[h]
            dstb = dst.astype(BF16)
            kdk = (k * kd_ref[h]).astype(BF16)
            dqr = _dot(dsb, kb) + qd_ref[h] * _dot_nt(dob, stb)
            dkr = _dot_tn(dsb, qb) + kd_ref[h] * _dot_nt(v, dstb)
            dv = _dot_tn(scb, dob) + _dot(kdk, dstb)
            dstate_s[h] = dst * bd_ref[h][:, :1] + _dot_tn((q * qd_ref[h]).astype(BF16), dob)
            dp_ref[:, h * RET_DK:(h + 1) * RET_DK] = _rope_inv(dqr, c, s).astype(BF16)
            dp_ref[:, 1024 + h * RET_DK:1024 + (h + 1) * RET_DK] = (_rope_inv(dkr, c, s) * (RET_DK ** -0.5)).astype(BF16)
            dp_ref[:, 2048 + h * RET_DV:2048 + (h + 1) * RET_DV] = dv.astype(BF16)
            dp_ref[:, 4096 + h * RET_DV:4096 + (h + 1) * RET_DV] = dgate.astype(BF16)

    const = lambda b, i: (0, 0, 0)
    rev = lambda b, i: (b * nb + nb - 1 - i, 0)
    return pl.pallas_call(
        body, name="ret_bwd", grid=(B, nb),
        in_specs=[pl.BlockSpec((LB, 6144), rev), pl.BlockSpec((LB, 2048), rev),
                  pl.BlockSpec((None, None, RET_HEADS, RET_DK, RET_DV), lambda b, i: (b, nb - 1 - i, 0, 0, 0)),
                  pl.BlockSpec((LB, 128), lambda b, i: (nb - 1 - i, 0)), pl.BlockSpec((LB, 128), lambda b, i: (nb - 1 - i, 0)),
                  pl.BlockSpec((RET_HEADS, LB, LB), const), pl.BlockSpec((RET_HEADS, LB, 1), const),
                  pl.BlockSpec((RET_HEADS, LB, 1), const), pl.BlockSpec((RET_HEADS, 1, 128), const),
                  pl.BlockSpec((1, 2048), lambda b, i: (0, 0))],
        out_specs=[pl.BlockSpec((LB, 6144), rev), pl.BlockSpec((1, 2048), lambda b, i: (0, 0))],
        out_shape=[jax.ShapeDtypeStruct((T, 6144), BF16), jax.ShapeDtypeStruct((1, 2048), F32)],
        scratch_shapes=[pltpu.VMEM((RET_HEADS, RET_DK, RET_DV), F32)], compiler_params=_params(2),
    )(proj, dy, states, cos, sin, dmat, qd, kd, bd, gn_g.reshape(1, 2048))


BIAS_LANES = 4 * ATT_BLOCK


def _diag_onehot():
    r = lax.broadcasted_iota(jnp.int32, (REL_TABLE, BIAS_LANES), 0)
    j = lax.broadcasted_iota(jnp.int32, (REL_TABLE, BIAS_LANES), 1)
    idx = jnp.maximum(j - ATT_BLOCK - PAST, -MAX_REL) + MAX_REL
    return jnp.where(idx == r, 1.0, 0.0).astype(F32)


def _row_is(j):
    return lax.broadcasted_iota(jnp.int32, (8, BIAS_LANES), 0) == j


def _att_bias(table):
    QB, KW = ATT_BLOCK, 3 * ATT_BLOCK

    def body(t_ref, b_ref, bt_ref):
        row = jnp.broadcast_to(t_ref[...], (8, REL_TABLE))
        diag = jnp.dot(row, _diag_onehot(), preferred_element_type=F32, precision=lax.Precision.HIGHEST)
        rows = jnp.zeros((8, BIAS_LANES), F32)
        for j in range(8):
            rows = jnp.where(_row_is(j), diag if j == 0 else pltpu.roll(diag, j, axis=1), rows)
        n = 8
        while n < QB:
            rows = jnp.concatenate([rows, pltpu.roll(rows, n, axis=1)], axis=0)
            n *= 2
        bias = rows[:, QB:]
        qi = lax.broadcasted_iota(jnp.int32, (QB, KW), 0)
        kj = lax.broadcasted_iota(jnp.int32, (QB, KW), 1)
        lo = (qi // CHUNK) * CHUNK
        bias = jnp.where((kj >= lo) & (kj < lo + PAST + CHUNK), bias, NEG)
        b_ref[...] = bias
        bt_ref[...] = bias.T

    return pl.pallas_call(
        body, name="att_bias", grid=(ATT_HEADS,),
        in_specs=[pl.BlockSpec((None, 1, REL_TABLE), lambda h: (h, 0, 0))],
        out_specs=[pl.BlockSpec((None, QB, KW), lambda h: (h, 0, 0)), pl.BlockSpec((None, KW, QB), lambda h: (h // 2, 0, h % 2))],
        out_shape=[jax.ShapeDtypeStruct((ATT_HEADS, QB, KW), F32), jax.ShapeDtypeStruct((ATT_HEADS // 2, KW, 2 * QB), F32)],
        compiler_params=_params(1),
    )(table.reshape(ATT_HEADS, 1, REL_TABLE))


def _att_bias_grad(dbias_t):
    QB, KW = ATT_BLOCK, 3 * ATT_BLOCK

    def body(d_ref, o_ref):
        rows = jnp.concatenate([jnp.zeros((QB, QB), F32), d_ref[...].T], axis=1)
        n = QB // 2
        while n >= 8:
            rows = rows[:n] + pltpu.roll(rows[n:], BIAS_LANES - n, axis=1)
            n //= 2
        acc = jnp.zeros((8, BIAS_LANES), F32)
        for j in range(8):
            acc = acc + jnp.where(_row_is(j), rows if j == 0 else pltpu.roll(rows, BIAS_LANES - j, axis=1), 0.0)
        diag = jnp.broadcast_to(jnp.sum(acc, axis=0, keepdims=True), (8, BIAS_LANES))
        grad = lax.dot_general(diag, _diag_onehot(), (((1,), (1,)), ((), ())), preferred_element_type=F32,
                               precision=lax.Precision.HIGHEST)
        o_ref[...] = grad[:1]

    return pl.pallas_call(
        body, name="att_bias_grad", grid=(ATT_HEADS,),
        in_specs=[pl.BlockSpec((None, KW, QB), lambda h: (h // 2, 0, h % 2))],
        out_specs=pl.BlockSpec((None, 1, REL_TABLE), lambda h: (h, 0, 0)),
        out_shape=jax.ShapeDtypeStruct((ATT_HEADS, 1, REL_TABLE), F32), compiler_params=_params(1),
    )(dbias_t).reshape(ATT_HEADS, REL_TABLE)


def _att_fwd(qkv, bias, *, B, S):
    QB = ATT_BLOCK
    nb = S // QB
    KW = 3 * QB
    T = B * S
    scale = ATT_DH ** -0.5

    def body(q_ref, k0, k1, k2, v0, v1, v2, b_ref, o_ref, lse_ref):
        i = pl.program_id(2)
        k3 = jnp.concatenate([k0[...], k1[...], k2[...]], axis=0)
        v3 = jnp.concatenate([v0[...], v1[...], v2[...]], axis=0)
        col = lax.broadcasted_iota(jnp.int32, (QB, KW), 1)
        in_seq = col >= (2 - i) * QB
        lane = lax.broadcasted_iota(jnp.int32, (QB, 128), 1)
        lane_kv = lax.broadcasted_iota(jnp.int32, (KW, 128), 1)
        q = (q_ref[...].astype(F32) * scale).astype(BF16)
        out = jnp.zeros((QB, 128), F32)
        lse = jnp.zeros((QB, 128), F32)
        for hh in range(2):
            qh = jnp.where((lane < ATT_DH) == (hh == 0), q, jnp.zeros_like(q))
            vh = jnp.where((lane_kv < ATT_DH) == (hh == 0), v3, jnp.zeros_like(v3))
            s = jnp.where(in_seq, _dot_nt(qh, k3) + b_ref[hh], NEG)
            m = jnp.max(s, axis=-1, keepdims=True)
            e = jnp.exp(s - m)
            l = jnp.sum(e, axis=-1, keepdims=True)
            out = out + _dot(e.astype(BF16), vh) / l
            lse = jnp.where(lane == hh, m + jnp.log(l), lse)
        o_ref[...] = out.astype(BF16)
        lse_ref[...] = lse.T[:8]

    def kv(d, col0):
        return pl.BlockSpec((QB, 128), lambda hp, b, i: (b * nb + jnp.maximum(i - d, 0), col0 + hp))

    return pl.pallas_call(
        body, name="att_fwd", grid=(8, B, nb),
        in_specs=[pl.BlockSpec((QB, 128), lambda hp, b, i: (b * nb + i, hp)),
                  kv(2, 8), kv(1, 8), kv(0, 8), kv(2, 16), kv(1, 16), kv(0, 16),
                  pl.BlockSpec((2, QB, KW), lambda hp, b, i: (hp, 0, 0))],
        out_specs=[pl.BlockSpec((QB, 128), lambda hp, b, i: (b * nb + i, hp)),
                   pl.BlockSpec((None, 8, QB), lambda hp, b, i: (hp, 0, b * nb + i))],
        out_shape=[jax.ShapeDtypeStruct((T, 1024), BF16), jax.ShapeDtypeStruct((8, 8, T), F32)], compiler_params=_params(3),
    )(qkv, qkv, qkv, qkv, qkv, qkv, qkv, bias)


def _att_bwd(qkv, do, o, lse, bias_t, *, B, S):
    QB = ATT_BLOCK
    nb = S // QB
    KW = 3 * QB
    T = B * S
    scale = ATT_DH ** -0.5
    TK = 256

    def body(q_ref, k0, k1, k2, v0, v1, v2, do_ref, o_ref, lse_ref, b_ref, dq_ref, dk_ref, dv_ref, db_ref, dk_acc, dv_acc):
        b, i = pl.program_id(1), pl.program_id(2)

        @pl.when(i == 0)
        def _():
            dk_acc[...] = jnp.zeros_like(dk_acc)
            dv_acc[...] = jnp.zeros_like(dv_acc)

        @pl.when((b == 0) & (i == 0))
        def _():
            db_ref[...] = jnp.zeros_like(db_ref)

        @pl.when(i < nb)
        def _():
            lane = lax.broadcasted_iota(jnp.int32, (QB, 128), 1)
            first = lane < ATT_DH

            def by_head(x):
                zero = jnp.zeros_like(x)
                return jnp.concatenate([jnp.where(first, x, zero), jnp.where(first, zero, x)], axis=0)

            dout = do_ref[...]
            q2 = by_head((q_ref[...].astype(F32) * scale).astype(BF16))
            do2 = by_head(dout)
            delta_t = (o_ref[...].astype(F32) * dout.astype(F32)).T
            delta2 = jnp.concatenate([jnp.sum(delta_t[:ATT_DH], axis=0, keepdims=True),
                                      jnp.sum(delta_t[ATT_DH:], axis=0, keepdims=True)], axis=1)
            lse2 = jnp.concatenate([lse_ref[0:1, :], lse_ref[1:2, :]], axis=1)
            dq_t = jnp.zeros((128, 2 * QB), F32)
            for d, (k_ref, v_ref) in enumerate(((k0, v0), (k1, v1), (k2, v2))):
                kblk, vblk = k_ref[...], v_ref[...]
                kt = kblk.astype(F32).T.astype(BF16)
                lse_d = jnp.where(i + d >= 2, lse2, -NEG)
                slot = (i + 1 + d) % 3
                for t in range(QB // TK):
                    rows = slice(t * TK, (t + 1) * TK)
                    wrows = slice(d * QB + t * TK, d * QB + (t + 1) * TK)
                    p = jnp.exp(_dot_nt(kblk[rows], q2) + b_ref[wrows, :] - lse_d)
                    ds = p * (_dot_nt(vblk[rows], do2) - delta2)
                    db_ref[wrows, :] += ds
                    dsb = ds.astype(BF16)
                    dk_acc[slot, rows, :] += _dot(dsb, q2)
                    dv_acc[slot, rows, :] += _dot(p.astype(BF16), do2)
                    dq_t += _dot(kt[:, rows], dsb)
            row = lax.broadcasted_iota(jnp.int32, (128, QB), 0)
            dq_ref[...] = (jnp.where(row < ATT_DH, dq_t[:, :QB], dq_t[:, QB:]) * scale).T.astype(BF16)

        @pl.when(i >= 2)
        def _():
            slot = (i + 1) % 3
            dk_ref[...] = dk_acc[slot].astype(BF16)
            dv_ref[...] = dv_acc[slot].astype(BF16)
            dk_acc[slot] = jnp.zeros((QB, 128), F32)
            dv_acc[slot] = jnp.zeros((QB, 128), F32)

    def qrow(b, i):
        return b * nb + jnp.minimum(i, nb - 1)

    def kv(d, col0):
        return pl.BlockSpec((QB, 128), lambda hp, b, i: (b * nb + jnp.maximum(jnp.minimum(i, nb - 1) - d, 0), col0 + hp))

    late = pl.BlockSpec((QB, 128), lambda hp, b, i: (b * nb + jnp.maximum(i - 2, 0), hp))
    return pl.pallas_call(
        body, name="att_bwd", grid=(8, B, nb + 2),
        in_specs=[pl.BlockSpec((QB, 128), lambda hp, b, i: (qrow(b, i), hp)),
                  kv(2, 8), kv(1, 8), kv(0, 8), kv(2, 16), kv(1, 16), kv(0, 16),
                  pl.BlockSpec((QB, 128), lambda hp, b, i: (qrow(b, i), hp)),
                  pl.BlockSpec((QB, 128), lambda hp, b, i: (qrow(b, i), hp)),
                  pl.BlockSpec((None, 8, QB), lambda hp, b, i: (hp, 0, qrow(b, i))),
                  pl.BlockSpec((None, KW, 2 * QB), lambda hp, b, i: (hp, 0, 0))],
        out_specs=[pl.BlockSpec((QB, 128), lambda hp, b, i: (qrow(b, i), hp)), late, late,
                   pl.BlockSpec((None, KW, 2 * QB), lambda hp, b, i: (hp, 0, 0))],
        out_shape=[jax.ShapeDtypeStruct((T, 1024), BF16)] * 3 + [jax.ShapeDtypeStruct((ATT_HEADS // 2, KW, 2 * QB), F32)],
        scratch_shapes=[pltpu.VMEM((3, QB, 128), F32), pltpu.VMEM((3, QB, 128), F32)], compiler_params=_params(3),
    )(qkv, qkv, qkv, qkv, qkv, qkv, qkv, do, o, lse, bias_t)


def _loss_head(h, tgt, g, *, tm):
    T, D = h.shape
    n = T // tm

    def body(h_ref, t_ref, g_ref, dh_ref, dg_ref, loss_ref, acc_ref):
        i = pl.program_id(0)
        x = h_ref[...]
        r = lax.rsqrt(jnp.mean(x * x, axis=-1, keepdims=True) + EPS)
        xh = x * r
        gg = g_ref[...]
        diff = xh * gg - t_ref[...]
        sq = jnp.sum(diff * diff, axis=0, keepdims=True)
        dy = diff * (1.0 / D)
        dg_part = jnp.sum(dy * xh, axis=0, keepdims=True)

        @pl.when(i == 0)
        def _():
            acc_ref[...] = sq
            dg_ref[...] = dg_part

        @pl.when(i > 0)
        def _():
            acc_ref[...] += sq
            dg_ref[...] += dg_part
        t = dy * gg
        dh_ref[...] = r * (t - xh * jnp.mean(t * xh, axis=-1, keepdims=True))

        @pl.when(i == n - 1)
        def _():
            loss_ref[...] = (0.5 / D) * jnp.sum(acc_ref[...], axis=1, keepdims=True)

    return pl.pallas_call(
        body, name="loss_head", grid=(n,),
        in_specs=[pl.BlockSpec((tm, D), lambda i: (i, 0)), pl.BlockSpec((tm, D), lambda i: (i, 0)),
                  pl.BlockSpec((1, D), lambda i: (0, 0))],
        out_specs=[pl.BlockSpec((tm, D), lambda i: (i, 0)), pl.BlockSpec((1, D), lambda i: (0, 0)),
                   pl.BlockSpec((1, 1), lambda i: (0, 0))],
        out_shape=[jax.ShapeDtypeStruct((T, D), F32), jax.ShapeDtypeStruct((1, D), F32), jax.ShapeDtypeStruct((1, 1), F32)],
        scratch_shapes=[pltpu.VMEM((1, D), F32)], compiler_params=_params(1),
    )(h, tgt, g.reshape(1, D))


def _tok_tile(T, want):
    t = min(T, want)
    assert T % t == 0
    return t


def _chip_sums(place, names, gw, tag):
    from_sibling = _exchange_with_sibling([gw[n] for n in names], tag)
    return {n: _add_sibling(place, gw[n], r, name="chip_sum_" + n) for n, r in zip(names, from_sibling)}


def _step(x, tgt, slab, slab_rel, place, mix_g, gn_g, mlp_g, fin_g):
    B, S, D = x.shape
    T = B * S
    h0 = x.reshape(T, D)
    tgt = tgt.reshape(T, D)
    tm = _tok_tile(T, 1024)
    tb = _tok_tile(T, 512)
    tq = _tok_tile(T, 256)
    tt = _tok_tile(T, 2048)
    consts = _ret_consts(S, RET_BLOCK)
    w = {}

    (w["ret_w_in"],) = _all_gather_slabs([slab["ret_w_in"]])
    (proj, hn0), (w["ret_w_out"], w["mlp_w1_0"], w["mlp_w2_0"]) = _mm_nn(
        h0, w["ret_w_in"], "col", tm=tb, out_dtype=BF16, name="ret_in", norm_g=mix_g[0],
        carry=_gather_carry([slab["ret_w_out"], slab["mlp_w1_0"], slab["mlp_w2_0"]]))
    y_ret, states = _ret_fwd(proj, gn_g, consts, B=B, S=S)
    h1, (w["att_w_in"], w["att_w_out"], rel_slabs) = _mm_nn(
        y_ret, w["ret_w_out"], "row", tm=tm, out_dtype=F32, name="ret_out", res=h0,
        carry=_gather_carry([slab["att_w_in"], slab["att_w_out"], slab_rel]))
    (a0, hm0), (w["mlp_w1_1"],) = _mm_nn(h1, w["mlp_w1_0"], "col", tm=tb, out_dtype=BF16, name="mlp0_up", norm_g=mlp_g[0],
                                         act="relu2", carry=_gather_carry([slab["mlp_w1_1"]]))
    h2, (w["mlp_w2_1"],) = _mm_nn(a0, w["mlp_w2_0"], "row", tm=tb, out_dtype=F32, name="mlp0_down", res=h1,
                                  carry=_gather_carry([slab["mlp_w2_1"]]))
    rel_bias = jnp.transpose(rel_slabs, (1, 0, 2)).reshape(ATT_HEADS, REL_TABLE)
    bias, bias_t = _att_bias(rel_bias)
    qkv, hn1 = _mm_nn(h2, w["att_w_in"], "col", tm=tb, out_dtype=BF16, name="att_in", norm_g=mix_g[1])
    o_att, lse = _att_fwd(qkv, bias, B=B, S=S)
    h3 = _mm_nn(o_att, w["att_w_out"], "row", tm=tm, out_dtype=F32, name="att_out", res=h2)
    a1, hm1 = _mm_nn(h3, w["mlp_w1_1"], "col", tm=tb, out_dtype=BF16, name="mlp1_up", norm_g=mlp_g[1], act="relu2")
    h4 = _mm_nn(a1, w["mlp_w2_1"], "row", tm=tb, out_dtype=F32, name="mlp1_down", res=h3)
    dh4, d_fin_g, loss = _loss_head(h4, tgt, fin_g, tm=tb)

    gw = {}
    gw["mlp_w2_1"] = _mm_tn(a1, dh4, "row", tt=tt, tk=1024, tn=D, name="d_mlp1_w2")
    dz1 = _mm_nt(dh4, w["mlp_w2_1"], "row", tm=tb, name="d_mlp1_act", epi="sqrt2", a2=a1)
    gw["mlp_w1_1"] = _mm_tn(hm1, dz1, "col", tt=tt, tk=D, tn=1024, name="d_mlp1_w1")
    dh3, d_mlp_g1 = _mm_nt(dz1, w["mlp_w1_1"], "col", tm=tb, name="d_mlp1_in", epi="normbwd",
                           h=h3, g=mlp_g[1], dres=dh4)
    gw["att_w_out"] = _mm_tn(o_att, dh3, "row", tt=tt, tk=256, tn=D, name="d_att_wout")
    do_att = _mm_nt(dh3, w["att_w_out"], "row", tm=tb, name="d_att_o", epi="bf16")
    dq, dk, dv, dbias_t = _att_bwd(qkv, do_att, o_att, lse, bias_t, B=B, S=S)
    d_rel = _att_bias_grad(dbias_t)
    dqkv = jnp.concatenate([dq, dk, dv], axis=1)
    gw["att_w_in"] = _mm_tn(hn1, dqkv, "col", tt=tt, tk=D, tn=768, name="d_att_win")
    sums = _chip_sums(place, ["mlp_w1_1", "mlp_w2_1", "att_w_in", "att_w_out"], gw, "layer1")
    landed = {}

    def carried_exchange(names):
        return _chips_carry([sums[n][0] for n in names])

    def keep(names, carried):
        landed.update(zip(names, carried[len(names):]))

    dh2, d_mix_g1 = _mm_nt(dqkv, w["att_w_in"], "col", tm=tb, name="d_att_in", epi="normbwd",
                           h=h2, g=mix_g[1], dres=dh3)
    gw["mlp_w2_0"] = _mm_tn(a0, dh2, "row", tt=tt, tk=1024, tn=D, name="d_mlp0_w2")
    dz0, carried = _mm_nt(dh2, w["mlp_w2_0"], "row", tm=tb, name="d_mlp0_act", epi="sqrt2", a2=a0,
                          carry=carried_exchange(["mlp_w1_1", "mlp_w2_1"]))
    keep(["mlp_w1_1", "mlp_w2_1"], carried)
    gw["mlp_w1_0"] = _mm_tn(hm0, dz0, "col", tt=tt, tk=D, tn=1024, name="d_mlp0_w1")
    (dh1, d_mlp_g0), carried = _mm_nt(dz0, w["mlp_w1_0"], "col", tm=tb, name="d_mlp0_in", epi="normbwd",
                                      h=h1, g=mlp_g[0], dres=dh2, carry=carried_exchange(["att_w_in", "att_w_out"]))
    keep(["att_w_in", "att_w_out"], carried)
    sums.update(_chip_sums(place, ["mlp_w1_0", "mlp_w2_0"], gw, "mlp0"))
    gw["ret_w_out"] = _mm_tn(y_ret, dh1, "row", tt=tt, tk=512, tn=D, name="d_ret_wout")
    dy_ret = _mm_nt(dh1, w["ret_w_out"], "row", tm=tb, name="d_ret_y", epi="bf16")
    dproj, d_gn = _ret_bwd(proj, dy_ret, states, gn_g, consts, B=B, S=S)
    gw["ret_w_in"] = _mm_tn(hn0, dproj, "col", tt=tt, tk=D, tn=768, name="d_ret_win")
    (dx, d_mix_g0), carried = _mm_nt(dproj, w["ret_w_in"], "col", tm=tq, name="d_ret_in", epi="normbwd",
                                     h=h0, g=mix_g[0], dres=dh1, carry=carried_exchange(["mlp_w1_0", "mlp_w2_0"]))
    keep(["mlp_w1_0", "mlp_w2_0"], carried)
    sums.update(_chip_sums(place, ["ret_w_in", "ret_w_out"], gw, "ret"))
    landed.update(zip(["ret_w_in", "ret_w_out"], _exchange_with_chips([sums[n][0] for n in ("ret_w_in", "ret_w_out")])))
    totals = {n: _add_chips(place, sums[n][1], landed[n], name="total_" + n) for n in BIG}
    small = [d_mix_g0, d_mix_g1, d_mlp_g0, d_mlp_g1, d_fin_g, d_gn.reshape(2, D), d_rel.reshape(5, D)]
    return loss.reshape(()), dx.reshape(B, S, D), totals, small


def _row_tile(r, want=256):
    t = min(r, want)
    assert r % t == 0
    return t


def _into_slab(place, a, dtype, name):
    r, c = a.shape
    tr = _row_tile(r)

    def body(place_ref, a_ref, o_ref):
        o_ref[...] = a_ref[...].astype(dtype)

    grid_spec = pltpu.PrefetchScalarGridSpec(
        num_scalar_prefetch=1, grid=(r // tr,), in_specs=[pl.BlockSpec((tr, c), lambda i, pr: (i, 0))],
        out_specs=pl.BlockSpec((None, tr, c), lambda i, pr: (pr[0], i, 0)),
    )
    return pl.pallas_call(
        body, name=name, grid_spec=grid_spec, out_shape=jax.ShapeDtypeStruct((N_CHIPS, r, c), dtype), compiler_params=_params(1),
    )(place, a)


def _add_sibling(place, g, recv, name):
    _, r, c = g.shape
    hr = r // 2
    tr = _row_tile(hr)
    nrt = hr // tr

    def body(place_ref, g_ref, r_ref, sb_ref, own_ref):
        v = g_ref[...] + r_ref[...]
        sb_ref[...] = v.astype(BF16)

        @pl.when(pl.program_id(1) == place_ref[0])
        def _():
            own_ref[...] = v

    grid_spec = pltpu.PrefetchScalarGridSpec(
        num_scalar_prefetch=1, grid=(nrt, N_CHIPS),
        in_specs=[pl.BlockSpec((None, tr, c), lambda i, s, pr: (s, pr[1] * nrt + i, 0)),
                  pl.BlockSpec((None, tr, c), lambda i, s, pr: (s, i, 0))],
        out_specs=[pl.BlockSpec((None, tr, c), lambda i, s, pr: (s, i, 0)), pl.BlockSpec((tr, c), lambda i, s, pr: (i, 0))],
    )
    return pl.pallas_call(
        body, name=name, grid_spec=grid_spec,
        out_shape=[jax.ShapeDtypeStruct((N_CHIPS, hr, c), BF16), jax.ShapeDtypeStruct((hr, c), F32)],
        compiler_params=_params(2),
    )(place, g, recv)


def _add_chips(place, own, recv, name):
    hr, c = own.shape
    tr = _row_tile(hr)
    nrt = hr // tr

    def body(place_ref, o_ref, r_ref, t_ref):
        t_ref[...] = ((o_ref[...] + r_ref[0].astype(F32)) + r_ref[1].astype(F32)) + r_ref[2].astype(F32)

    grid_spec = pltpu.PrefetchScalarGridSpec(
        num_scalar_prefetch=1, grid=(nrt,),
        in_specs=[pl.BlockSpec((tr, c), lambda i, pr: (i, 0)), pl.BlockSpec((3, tr, c), lambda i, pr: (0, i, 0))],
        out_specs=pl.BlockSpec((tr, c), lambda i, pr: (pr[1] * nrt + i, 0)),
    )
    return pl.pallas_call(
        body, name=name, grid_spec=grid_spec, out_shape=jax.ShapeDtypeStruct((2 * hr, c), F32), compiler_params=_params(1),
    )(place, own, recv)


def _adamw(w, g, m, v, name):
    r, c = w.shape
    tr = _row_tile(r)

    def body(w_ref, g_ref, m_ref, v_ref, d_ref, nm_ref, nv_ref):
        gg = g_ref[...]
        nm = ADAM_B1 * m_ref[...] + (1.0 - ADAM_B1) * gg
        nv = ADAM_B2 * v_ref[...] + (1.0 - ADAM_B2) * (gg * gg)
        m_hat = nm / (1.0 - ADAM_B1 ** ADAM_STEP)
        v_hat = nv / (1.0 - ADAM_B2 ** ADAM_STEP)
        d_ref[...] = -ADAM_LR * (m_hat / (jnp.sqrt(v_hat) + ADAM_EPS) + ADAM_WD * w_ref[...])
        nm_ref[...] = nm
        nv_ref[...] = nv

    spec = pl.BlockSpec((tr, c), lambda i: (i, 0))
    return pl.pallas_call(
        body, name=name, grid=(r // tr,), in_specs=[spec] * 4, out_specs=[spec] * 3,
        out_shape=[jax.ShapeDtypeStruct((r, c), F32)] * 3, compiler_params=_params(1),
    )(w, g, m, v)


def _place():
    return lax.axis_index("x"), lax.axis_index("y"), lax.axis_index("c")


def _other_chips(x, y):
    return [(1 - x, y), (x, 1 - y), (1 - x, 1 - y)]


def _remote(src, dst, ssem, rsem, dev):
    return pltpu.make_async_remote_copy(src_ref=src, dst_ref=dst, send_sem=ssem, recv_sem=rsem, device_id=dev,
                                        device_id_type=MESH)


def _gather_phases(n):
    def geometry(refs, t):
        x, y, c = _place()
        hr = refs[t].shape[1] // 2
        chips = _other_chips(x, y)
        return x, y, c, 2 * x + y, chips, [2 * qx + qy for qx, qy in chips], pl.ds(c * hr, hr), pl.ds((1 - c) * hr, hr)

    def send(refs, sems):
        s1, r1, _, _ = sems
        for t in range(n):
            x, y, c, p, chips, cidx, mine, theirs = geometry(refs, t)
            for j, (qx, qy) in enumerate(chips):
                _remote(refs[t].at[p, mine], refs[t].at[p, mine], s1.at[t, j], r1.at[t, j], (qx, qy, c)).start()

    def pass_on(refs, sems):
        s1, r1, s2, r2 = sems
        for t in range(n):
            x, y, c, p, chips, cidx, mine, theirs = geometry(refs, t)
            for j, (qx, qy) in enumerate(chips):
                got = refs[t].at[cidx[j], mine]
                _remote(got, got, s1.at[t, j], r1.at[t, j], (qx, qy, c)).wait_recv()
                _remote(got, got, s2.at[t, j], r2.at[t, j], (x, y, 1 - c)).start()

    def finish(refs, sems):
        s1, r1, s2, r2 = sems
        for t in range(n):
            x, y, c, p, chips, cidx, mine, theirs = geometry(refs, t)
            for j, (qx, qy) in enumerate(chips):
                got = refs[t].at[cidx[j], theirs]
                _remote(got, got, s2.at[t, j], r2.at[t, j], (x, y, 1 - c)).wait_recv()
        for t in range(n):
            x, y, c, p, chips, cidx, mine, theirs = geometry(refs, t)
            for j, (qx, qy) in enumerate(chips):
                _remote(refs[t].at[p, mine], refs[t].at[p, mine], s1.at[t, j], r1.at[t, j], (qx, qy, c)).wait_send()
                sent = refs[t].at[cidx[j], mine]
                _remote(sent, sent, s2.at[t, j], r2.at[t, j], (x, y, 1 - c)).wait_send()

    sem = pltpu.SemaphoreType.DMA
    return send, pass_on, finish, [sem((n, 3)), sem((n, 3)), sem((n, 3)), sem((n, 3))]


def _gather_carry(slabs):
    send, pass_on, finish, sems = _gather_phases(len(slabs))
    return _Carry(slabs, sems, [(0, send), (-2, pass_on), (-1, finish)])


def _all_gather_slabs(slabs):
    n = len(slabs)
    send, pass_on, finish, sems = _gather_phases(n)

    def body(*refs):
        outs, scratch = refs[n:2 * n], refs[2 * n:]
        send(outs, scratch)
        pass_on(outs, scratch)
        finish(outs, scratch)

    return pl.pallas_call(
        body, name="gather_weights", in_specs=[HBM_SPEC] * n, out_specs=[HBM_SPEC] * n,
        out_shape=[jax.ShapeDtypeStruct(s.shape, s.dtype) for s in slabs], input_output_aliases={t: t for t in range(n)},
        scratch_shapes=sems, compiler_params=pltpu.CompilerParams(has_side_effects=True),
    )(*slabs)


def _exchange_with_sibling(grads, tag):
    n = len(grads)

    def body(*refs):
        ins, outs = refs[:n], refs[n:2 * n]
        ssem, rsem = refs[2 * n:]
        x, y, c = _place()
        copies = []
        for t in range(n):
            hr = ins[t].shape[1] // 2
            cp = _remote(ins[t].at[:, pl.ds((1 - c) * hr, hr), :], outs[t], ssem.at[t], rsem.at[t], (x, y, 1 - c))
            cp.start()
            copies.append(cp)
        for cp in copies:
            cp.wait()

    sem = pltpu.SemaphoreType.DMA
    return pl.pallas_call(
        body, name="grads_to_sibling_" + tag, in_specs=[HBM_SPEC] * n, out_specs=[HBM_SPEC] * n,
        out_shape=[jax.ShapeDtypeStruct((N_CHIPS, g.shape[1] // 2, g.shape[2]), g.dtype) for g in grads],
        scratch_shapes=[sem((n,)), sem((n,))], compiler_params=pltpu.CompilerParams(has_side_effects=True),
    )(*grads)


def _chips_phases(n):
    def copies(refs, sems):
        ssem, rsem = sems
        x, y, c = _place()
        return [_remote(refs[t].at[2 * qx + qy], refs[n + t].at[j], ssem.at[t, j], rsem.at[t, j], (qx, qy, c))
                for t in range(n) for j, (qx, qy) in enumerate(_other_chips(x, y))]

    def send(refs, sems):
        for cp in copies(refs, sems):
            cp.start()

    def finish(refs, sems):
        for cp in copies(refs, sems):
            cp.wait()

    sem = pltpu.SemaphoreType.DMA
    return send, finish, [sem((n, 3)), sem((n, 3))]


def _landing(sums):
    return [lax.empty((3,) + s.shape[1:], s.dtype) for s in sums]


def _chips_carry(sums):
    send, finish, sems = _chips_phases(len(sums))
    return _Carry(list(sums) + _landing(sums), sems, [(0, send), (-1, finish)])


def _exchange_with_chips(sums):
    n = len(sums)
    send, finish, sems = _chips_phases(n)

    def body(*refs):
        outs, scratch = refs[2 * n:4 * n], refs[4 * n:]
        send(outs, scratch)
        finish(outs, scratch)

    return pl.pallas_call(
        body, name="grads_to_chips", in_specs=[HBM_SPEC] * (2 * n), out_specs=[HBM_SPEC] * (2 * n),
        out_shape=[jax.ShapeDtypeStruct(s.shape, s.dtype) for s in list(sums) + _landing(sums)],
        input_output_aliases={t: t for t in range(2 * n)}, scratch_shapes=sems,
        compiler_params=pltpu.CompilerParams(has_side_effects=True),
    )(*sums, *_landing(sums))[n:]


def _share_with_sibling(shards):
    n = len(shards)

    def body(*refs):
        outs = refs[n:2 * n]
        ssem, rsem = refs[2 * n:]
        x, y, c = _place()
        copies = []
        for t in range(n):
            hr = outs[t].shape[0] // 2
            mine = outs[t].at[pl.ds(c * hr, hr)]
            cp = _remote(mine, mine, ssem.at[t], rsem.at[t], (x, y, 1 - c))
            cp.start()
            copies.append(cp)
        for t in range(n):
            hr = outs[t].shape[0] // 2
            theirs = outs[t].at[pl.ds((1 - c) * hr, hr)]
            _remote(theirs, theirs, ssem.at[t], rsem.at[t], (x, y, 1 - c)).wait_recv()
        for cp in copies:
            cp.wait_send()

    sem = pltpu.SemaphoreType.DMA
    return pl.pallas_call(
        body, name="grads_share", in_specs=[HBM_SPEC] * n, out_specs=[HBM_SPEC] * n,
        out_shape=[jax.ShapeDtypeStruct(s.shape, s.dtype) for s in shards], input_output_aliases={t: t for t in range(n)},
        scratch_shapes=[sem((n,)), sem((n,))], compiler_params=pltpu.CompilerParams(has_side_effects=True),
    )(*shards)


def _all_reduce_small(buf):
    R, C = buf.shape

    def body(in_ref, out_ref, gather, ssem, rsem):
        x, y, c = _place()
        me = 4 * x + 2 * y + c
        gather[me] = in_ref[...]
        flips = [(fx, fy, fc) for fx in (0, 1) for fy in (0, 1) for fc in (0, 1) if fx or fy or fc]
        peers = [(x + fx - 2 * x * fx, y + fy - 2 * y * fy, c + fc - 2 * c * fc) for fx, fy, fc in flips]
        copies = [_remote(in_ref, gather.at[me], ssem.at[k], rsem.at[k], peer) for k, peer in enumerate(peers)]
        for cp in copies:
            cp.start()
        for k, (px, py, pc) in enumerate(peers):
            _remote(in_ref, gather.at[4 * px + 2 * py + pc], ssem.at[k], rsem.at[k], (px, py, pc)).wait_recv()
        for cp in copies:
            cp.wait_send()
        acc = gather[0]
        for d in range(1, 8):
            acc = acc + gather[d]
        out_ref[...] = acc

    sem = pltpu.SemaphoreType.DMA
    vmem = pl.BlockSpec(memory_space=pltpu.VMEM)
    return pl.pallas_call(
        body, name="small_grads_sum", in_specs=[vmem], out_specs=vmem, out_shape=jax.ShapeDtypeStruct((R, C), F32),
        scratch_shapes=[pltpu.VMEM((8, R, C), F32), sem((7,)), sem((7,))],
        compiler_params=pltpu.CompilerParams(has_side_effects=True),
    )(buf)


BIG = ["ret_w_in", "ret_w_out", "att_w_in", "att_w_out", "mlp_w1_0", "mlp_w1_1", "mlp_w2_0", "mlp_w2_1"]


def _split_big(ret_w_in, ret_w_out, att_w_in, att_w_out, mlp_w1, mlp_w2):
    return {"ret_w_in": ret_w_in[0], "ret_w_out": ret_w_out[0], "att_w_in": att_w_in[0], "att_w_out": att_w_out[0],
            "mlp_w1_0": mlp_w1[0], "mlp_w1_1": mlp_w1[1], "mlp_w2_0": mlp_w2[0], "mlp_w2_1": mlp_w2[1]}


def kernel(x, mix_norm_g, ret_w_in, ret_gn_g, ret_w_out, att_w_in, att_rel_bias, att_w_out, mlp_norm_g, mlp_w1, mlp_w2, final_norm_g, loss_target, m_mix_norm_g, m_ret_w_in, m_ret_gn_g, m_ret_w_out, m_att_w_in, m_att_rel_bias, m_att_w_out, m_mlp_norm_g, m_mlp_w1, m_mlp_w2, m_final_norm_g, v_mix_norm_g, v_ret_w_in, v_ret_gn_g, v_ret_w_out, v_att_w_in, v_att_rel_bias, v_att_w_out, v_mlp_norm_g, v_mlp_w1, v_mlp_w2, v_final_norm_g):
    xi, yi, ci = _place()
    chip = 2 * xi + yi
    w32 = _split_big(ret_w_in, ret_w_out, att_w_in, att_w_out, mlp_w1, mlp_w2)
    m32 = _split_big(m_ret_w_in, m_ret_w_out, m_att_w_in, m_att_w_out, m_mlp_w1, m_mlp_w2)
    v32 = _split_big(v_ret_w_in, v_ret_w_out, v_att_w_in, v_att_w_out, v_mlp_w1, v_mlp_w2)

    place = jnp.stack([chip, ci]).astype(jnp.int32)
    slab = {n: _into_slab(place, w32[n], BF16, name="cast_" + n) for n in BIG}
    slab_rel = _into_slab(place, att_rel_bias[0], F32, name="slab_rel_bias")

    loss_local, grad_x, totals, small = _step(x, loss_target, slab, slab_rel, place, mix_norm_g, ret_gn_g[0], mlp_norm_g,
                                              final_norm_g)
    loss = lax.psum(loss_local, ("x", "y", "c"))

    g_big = dict(zip(BIG, _share_with_sibling([totals[n] for n in BIG])))

    rows, at = jnp.zeros((16, D_MODEL), F32), 0
    for part in small:
        rows = rows + jnp.pad(part, ((at, 16 - at - part.shape[0]), (0, 0)))
        at += part.shape[0]
    rows = _all_reduce_small(rows)
    g_small = {"mix_norm_g": rows[0:2], "mlp_norm_g": rows[2:4], "final_norm_g": rows[4:5], "ret_gn_g": rows[5:7].reshape(1, 2048),
               "att_rel_bias": lax.dynamic_slice_in_dim(rows[7:12].reshape(ATT_HEADS, REL_TABLE), chip * (REL_TABLE // N_CHIPS),
                                                        REL_TABLE // N_CHIPS, axis=1)}

    upd = {n: _adamw(w32[n], g_big[n], m32[n], v32[n], name="adamw_" + n) for n in BIG}
    small_in = {"mix_norm_g": (mix_norm_g, m_mix_norm_g, v_mix_norm_g), "ret_gn_g": (ret_gn_g, m_ret_gn_g, v_ret_gn_g),
                "att_rel_bias": (att_rel_bias[0], m_att_rel_bias[0], v_att_rel_bias[0]),
                "mlp_norm_g": (mlp_norm_g, m_mlp_norm_g, v_mlp_norm_g),
                "final_norm_g": (final_norm_g.reshape(1, -1), m_final_norm_g.reshape(1, -1), v_final_norm_g.reshape(1, -1))}
    upd_small = {n: _adamw(wv, g_small[n], mv, vv, name="adamw_" + n) for n, (wv, mv, vv) in small_in.items()}

    def big_out(k):
        def get(n):
            return g_big[n] if k is None else upd[n][k]
        return {"ret_w_in": get("ret_w_in")[None], "ret_w_out": get("ret_w_out")[None], "att_w_in": get("att_w_in")[None],
                "att_w_out": get("att_w_out")[None], "mlp_w1": jnp.stack([get("mlp_w1_0"), get("mlp_w1_1")]),
                "mlp_w2": jnp.stack([get("mlp_w2_0"), get("mlp_w2_1")])}

    def small_out(k):
        def get(n):
            return g_small[n] if k is None else upd_small[n][k]
        return {"mix_norm_g": get("mix_norm_g"), "ret_gn_g": get("ret_gn_g"), "att_rel_bias": get("att_rel_bias")[None],
                "mlp_norm_g": get("mlp_norm_g"), "final_norm_g": get("final_norm_g").reshape(-1)}

    order = ["mix_norm_g", "ret_w_in", "ret_gn_g", "ret_w_out", "att_w_in", "att_rel_bias", "att_w_out", "mlp_norm_g",
             "mlp_w1", "mlp_w2", "final_norm_g"]
    outs = [loss, grad_x]
    for k in (None, 0, 1, 2):
        both = {**big_out(k), **small_out(k)}
        outs += [both[n] for n in order]
    return tuple(outs)
```

```python
import jax
import jax.numpy as jnp
from jax import lax
from jax.experimental import pallas as pl
from jax.experimental.pallas import tpu as pltpu

F32 = jnp.float32
BF16 = jnp.bfloat16
MESH = pl.DeviceIdType.MESH

D_MODEL = 1024
CHUNK = 64
RET_HEADS = 4
RET_DK = 256
RET_DV = 512
ROPE_BASE = 10000.0
ATT_HEADS = 16
ATT_DH = 64
PAST = 512
MAX_REL = 256
REL_TABLE = MAX_REL + CHUNK
EPS = 1e-6
NEG = -1e30
N_CHIPS = 4

ADAM_LR = 0.001
ADAM_B1 = 0.9
ADAM_B2 = 0.999
ADAM_EPS = 1e-08
ADAM_WD = 0.01
ADAM_STEP = 10

RET_BLOCK = 256
ATT_BLOCK = 256
VMEM_LIMIT = 56 * 1024 * 1024


def _params(n_axes, **kw):
    return pltpu.CompilerParams(dimension_semantics=("arbitrary",) * n_axes, vmem_limit_bytes=VMEM_LIMIT, **kw)


def _dot(a, b):
    return jnp.dot(a, b, preferred_element_type=F32)


def _dot_nt(a, b):
    return lax.dot_general(a, b, (((1,), (1,)), ((), ())), preferred_element_type=F32)


def _dot_tn(a, b):
    return lax.dot_general(a, b, (((0,), (0,)), ((), ())), preferred_element_type=F32)


def _sigmoid(x):
    return 1.0 / (1.0 + jnp.exp(-x))


HBM_SPEC = pl.BlockSpec(memory_space=pltpu.HBM)


class _Carry:
    def __init__(self, arrays, sems, stages):
        self.arrays, self.sems, self.stages = list(arrays), list(sems), list(stages)


def _carry_call(body, carry, *, name, grid, in_specs, out_specs, out_shape, args):
    if carry is None:
        outs = pl.pallas_call(body, name=name, grid=grid, in_specs=in_specs, out_specs=out_specs, out_shape=out_shape,
                              compiler_params=_params(len(grid)))(*args)
        return list(outs), []
    n_in, n_out, n_c = len(in_specs), len(out_specs), len(carry.arrays)
    steps = 1
    for g in grid:
        steps *= g
    assert all(-steps <= at < steps for at, _ in carry.stages)

    def carrying(*refs):
        ins, outs = refs[:n_in], refs[n_in + n_c:n_in + n_c + n_out]
        carried = refs[n_in + n_c + n_out:n_in + 2 * n_c + n_out]
        sems = refs[n_in + 2 * n_c + n_out:]
        step = pl.program_id(0)
        for axis in range(1, len(grid)):
            step = step * grid[axis] + pl.program_id(axis)
        for at, fn in carry.stages:
            if at == 0:
                pl.when(step == 0)(lambda fn=fn: fn(carried, sems))
        body(*ins, *outs)
        for at, fn in carry.stages:
            if at != 0:
                pl.when(step == at % steps)(lambda fn=fn: fn(carried, sems))

    outs = pl.pallas_call(
        carrying, name=name, grid=grid, in_specs=list(in_specs) + [HBM_SPEC] * n_c, out_specs=list(out_specs) + [HBM_SPEC] * n_c,
        out_shape=list(out_shape) + [jax.ShapeDtypeStruct(a.shape, a.dtype) for a in carry.arrays],
        input_output_aliases={n_in + t: n_out + t for t in range(n_c)}, scratch_shapes=carry.sems,
        compiler_params=_params(len(grid), has_side_effects=True),
    )(*args, *carry.arrays)
    return list(outs[:n_out]), list(outs[n_out:])


def _mm_nn(a, w, wkind, *, tm, out_dtype, name, norm_g=None, act=None, res=None, carry=None):
    M, K = a.shape
    cols = w.shape[2]
    N = N_CHIPS * cols if wkind == "col" else cols
    has_norm = norm_g is not None
    assert M % tm == 0 and K == (w.shape[1] if wkind == "col" else N_CHIPS * w.shape[1])

    def body(*refs):
        it = iter(refs)
        a_ref, w_ref = next(it), next(it)
        g_ref = next(it) if has_norm else None
        r_ref = next(it) if res is not None else None
        o_ref = next(it)
        hn_ref = next(it) if has_norm else None
        if has_norm:
            x = a_ref[...].astype(F32)
            r = lax.rsqrt(jnp.mean(x * x, axis=-1, keepdims=True) + EPS)
            lhs = (x * r * g_ref[...]).astype(BF16)
            hn_ref[...] = lhs
        else:
            lhs = a_ref[...].astype(BF16)

        def finish(acc, sl):
            if act == "relu2":
                u = jnp.maximum(acc, 0.0)
                acc = u * u
            if r_ref is not None:
                acc = acc + r_ref[:, sl]
            o_ref[:, sl] = acc.astype(out_dtype)

        if wkind == "col":
            for s in range(N_CHIPS):
                finish(_dot(lhs, w_ref[s]), slice(s * cols, (s + 1) * cols))
        else:
            finish(_dot(lhs, w_ref[...].reshape(K, N)), slice(None))

    in_specs = [pl.BlockSpec((tm, K), lambda i: (i, 0)), pl.BlockSpec(w.shape, lambda i: (0, 0, 0))]
    args = [a, w]
    if has_norm:
        in_specs.append(pl.BlockSpec((1, K), lambda i: (0, 0)))
        args.append(norm_g.reshape(1, K))
    if res is not None:
        in_specs.append(pl.BlockSpec((tm, N), lambda i: (i, 0)))
        args.append(res)
    out_shape = [jax.ShapeDtypeStruct((M, N), out_dtype)]
    out_specs = [pl.BlockSpec((tm, N), lambda i: (i, 0))]
    if has_norm:
        out_shape.append(jax.ShapeDtypeStruct((M, K), BF16))
        out_specs.append(pl.BlockSpec((tm, K), lambda i: (i, 0)))
    outs, carried = _carry_call(body, carry, name=name, grid=(M // tm,), in_specs=in_specs, out_specs=out_specs,
                                out_shape=out_shape, args=args)
    result = outs if has_norm else outs[0]
    return result if carry is None else (result, carried)


def _mm_nt(a, w, wkind, *, tm, name, epi, a2=None, h=None, g=None, dres=None, carry=None):
    M, Nw = a.shape
    rows, cols = w.shape[1], w.shape[2]
    Kw = rows if wkind == "col" else N_CHIPS * rows
    assert M % tm == 0 and Nw == (N_CHIPS * cols if wkind == "col" else cols)
    assert epi != "normbwd" or wkind == "col"

    def body(*refs):
        it = iter(refs)
        a_ref, w_ref = next(it), next(it)
        a2_ref = next(it) if epi == "sqrt2" else None
        if epi == "normbwd":
            h_ref, g_ref, dres_ref = next(it), next(it), next(it)
        o_ref = next(it)
        dg_ref = next(it) if epi == "normbwd" else None
        i = pl.program_id(0)

        def finish(acc, sl):
            if epi == "bf16":
                o_ref[:, sl] = acc.astype(BF16)
            elif epi == "sqrt2":
                o_ref[:, sl] = (acc * (2.0 * jnp.sqrt(a2_ref[:, sl].astype(F32)))).astype(BF16)
            else:
                x = h_ref[...]
                r = lax.rsqrt(jnp.mean(x * x, axis=-1, keepdims=True) + EPS)
                xh = x * r
                dg_part = jnp.sum(acc * xh, axis=0, keepdims=True)

                @pl.when(i == 0)
                def _():
                    dg_ref[...] = dg_part

                @pl.when(i > 0)
                def _():
                    dg_ref[...] += dg_part
                t = acc * g_ref[...]
                dx = r * (t - xh * jnp.mean(t * xh, axis=-1, keepdims=True))
                o_ref[...] = dres_ref[...] + dx

        if wkind == "col":
            acc = _dot_nt(a_ref[:, :cols].astype(BF16), w_ref[0])
            for s in range(1, N_CHIPS):
                acc += _dot_nt(a_ref[:, s * cols:(s + 1) * cols].astype(BF16), w_ref[s])
            finish(acc, slice(None))
        else:
            lhs = a_ref[...].astype(BF16)
            for s in range(N_CHIPS):
                finish(_dot_nt(lhs, w_ref[s]), slice(s * rows, (s + 1) * rows))

    in_specs = [pl.BlockSpec((tm, Nw), lambda i: (i, 0)), pl.BlockSpec(w.shape, lambda i: (0, 0, 0))]
    args = [a, w]
    out_dtype = BF16
    if epi == "sqrt2":
        in_specs.append(pl.BlockSpec((tm, Kw), lambda i: (i, 0)))
        args.append(a2)
    if epi == "normbwd":
        in_specs += [pl.BlockSpec((tm, Kw), lambda i: (i, 0)), pl.BlockSpec((1, Kw), lambda i: (0, 0)),
                     pl.BlockSpec((tm, Kw), lambda i: (i, 0))]
        args += [h, g.reshape(1, Kw), dres]
        out_dtype = F32
    out_shape = [jax.ShapeDtypeStruct((M, Kw), out_dtype)]
    out_specs = [pl.BlockSpec((tm, Kw), lambda i: (i, 0))]
    if epi == "normbwd":
        out_shape.append(jax.ShapeDtypeStruct((1, Kw), F32))
        out_specs.append(pl.BlockSpec((1, Kw), lambda i: (0, 0)))
    outs, carried = _carry_call(body, carry, name=name, grid=(M // tm,), in_specs=in_specs, out_specs=out_specs,
                                out_shape=out_shape, args=args)
    result = outs if epi == "normbwd" else outs[0]
    return result if carry is None else (result, carried)


def _mm_tn(a, b, okind, *, tt, tk, tn, name, carry=None):
    T, K = a.shape
    N = b.shape[1]
    assert T % tt == 0 and K % tk == 0 and N % tn == 0
    nt = T // tt
    if okind == "col":
        per = (N // N_CHIPS) // tn
        assert (N // N_CHIPS) % tn == 0
        out_shape = jax.ShapeDtypeStruct((N_CHIPS, K, N // N_CHIPS), F32)
        out_spec = pl.BlockSpec((None, tk, tn), lambda ki, nj, t: (nj // per, ki, nj % per))
    else:
        per = (K // N_CHIPS) // tk
        assert (K // N_CHIPS) % tk == 0
        out_shape = jax.ShapeDtypeStruct((N_CHIPS, K // N_CHIPS, N), F32)
        out_spec = pl.BlockSpec((None, tk, tn), lambda ki, nj, t: (ki // per, ki % per, nj))

    def body(a_ref, b_ref, o_ref):
        t = pl.program_id(2)
        part = _dot_tn(a_ref[...].astype(BF16), b_ref[...].astype(BF16))

        @pl.when(t == 0)
        def _():
            o_ref[...] = part

        @pl.when(t > 0)
        def _():
            o_ref[...] += part

    outs, carried = _carry_call(
        body, carry, name=name, grid=(K // tk, N // tn, nt),
        in_specs=[pl.BlockSpec((tt, tk), lambda ki, nj, t: (t, ki)), pl.BlockSpec((tt, tn), lambda ki, nj, t: (t, nj))],
        out_specs=[out_spec], out_shape=[out_shape], args=[a, b])
    return outs[0] if carry is None else (outs[0], carried)


def _ret_consts(S, LB):
    log_gamma = jnp.log1p(-jnp.exp2(-5.0 - jnp.arange(RET_HEADS, dtype=F32)))
    idx = jnp.arange(LB, dtype=F32)
    n, m = idx[:, None], idx[None, :]
    cn, cm = jnp.floor(n / CHUNK), jnp.floor(m / CHUNK)
    dist = jnp.where(cm == cn, jnp.abs(n - m), n - m)
    dmat = jnp.where((cm <= cn)[None], jnp.exp(log_gamma[:, None, None] * dist[None]), 0.0)
    qd = jnp.exp(log_gamma[:, None] * (idx + 1.0)[None, :])[..., None]
    kd = jnp.exp(log_gamma[:, None] * (LB - 1 - idx)[None, :])[..., None]
    bd = jnp.exp(log_gamma * LB).reshape(RET_HEADS, 1, 1) * jnp.ones((RET_HEADS, 1, 128), F32)
    half = RET_DK // 2
    inv = jnp.exp(-jnp.log(ROPE_BASE) * jnp.arange(half, dtype=F32) / half)
    ang = jnp.arange(S, dtype=F32)[:, None] * inv[None, :]
    return dmat.astype(F32), qd.astype(F32), kd.astype(F32), bd, jnp.cos(ang), jnp.sin(ang)


def _rope(t, c, s):
    t1, t2 = t[:, :128], t[:, 128:]
    return jnp.concatenate([t1 * c - t2 * s, t1 * s + t2 * c], axis=-1)


def _rope_inv(d, c, s):
    d1, d2 = d[:, :128], d[:, 128:]
    return jnp.concatenate([d1 * c + d2 * s, d2 * c - d1 * s], axis=-1)


def _ret_block_fwd(p_ref, h, c, s, d_ref, qd_ref, kd_ref, stb):
    q = _rope(p_ref[:, h * RET_DK:(h + 1) * RET_DK].astype(F32), c, s)
    k = _rope(p_ref[:, 1024 + h * RET_DK:1024 + (h + 1) * RET_DK].astype(F32), c, s) * (RET_DK ** -0.5)
    v = p_ref[:, 2048 + h * RET_DV:2048 + (h + 1) * RET_DV]
    qb, kb = q.astype(BF16), k.astype(BF16)
    scb = (_dot_nt(qb, kb) * d_ref[h]).astype(BF16)
    o = _dot(scb, v) + qd_ref[h] * _dot(qb, stb)
    return q, k, qb, kb, v, scb, o


def _ret_fwd(proj, gn_g, consts, *, B, S):
    LB = RET_BLOCK
    nb = S // LB
    T = B * S
    dmat, qd, kd, bd, cos, sin = consts

    def body(p_ref, cos_ref, sin_ref, d_ref, qd_ref, kd_ref, bd_ref, gng_ref, y_ref, st_ref, state_s):
        i = pl.program_id(1)

        @pl.when(i == 0)
        def _():
            state_s[...] = jnp.zeros_like(state_s)
        c, s = cos_ref[...], sin_ref[...]
        for h in range(RET_HEADS):
            st = state_s[h]
            stb = st.astype(BF16)
            st_ref[h] = stb
            q, k, qb, kb, v, scb, o = _ret_block_fwd(p_ref, h, c, s, d_ref, qd_ref, kd_ref, stb)
            kdk = (k * kd_ref[h]).astype(BF16)
            state_s[h] = st * bd_ref[h][:, :1] + _dot_tn(kdk, v)
            gate = p_ref[:, 4096 + h * RET_DV:4096 + (h + 1) * RET_DV].astype(F32)
            mu = jnp.mean(o, axis=-1, keepdims=True)
            oc = o - mu
            xh = oc * lax.rsqrt(jnp.mean(oc * oc, axis=-1, keepdims=True) + EPS)
            y = (gate * _sigmoid(gate)) * (xh * gng_ref[:, h * RET_DV:(h + 1) * RET_DV])
            y_ref[:, h * RET_DV:(h + 1) * RET_DV] = y.astype(BF16)

    const = lambda b, i: (0, 0, 0)
    return pl.pallas_call(
        body, name="ret_fwd", grid=(B, nb),
        in_specs=[pl.BlockSpec((LB, 6144), lambda b, i: (b * nb + i, 0)),
                  pl.BlockSpec((LB, 128), lambda b, i: (i, 0)), pl.BlockSpec((LB, 128), lambda b, i: (i, 0)),
                  pl.BlockSpec((RET_HEADS, LB, LB), const), pl.BlockSpec((RET_HEADS, LB, 1), const),
                  pl.BlockSpec((RET_HEADS, LB, 1), const), pl.BlockSpec((RET_HEADS, 1, 128), const),
                  pl.BlockSpec((1, 2048), lambda b, i: (0, 0))],
        out_specs=[pl.BlockSpec((LB, 2048), lambda b, i: (b * nb + i, 0)),
                   pl.BlockSpec((None, None, RET_HEADS, RET_DK, RET_DV), lambda b, i: (b, i, 0, 0, 0))],
        out_shape=[jax.ShapeDtypeStruct((T, 2048), BF16), jax.ShapeDtypeStruct((B, nb, RET_HEADS, RET_DK, RET_DV), BF16)],
        scratch_shapes=[pltpu.VMEM((RET_HEADS, RET_DK, RET_DV), F32)], compiler_params=_params(2),
    )(proj, cos, sin, dmat, qd, kd, bd, gn_g.reshape(1, 2048))


def _ret_bwd(proj, dy, states, gn_g, consts, *, B, S):
    LB = RET_BLOCK
    nb = S // LB
    T = B * S
    dmat, qd, kd, bd, cos, sin = consts

    def body(p_ref, dy_ref, st_ref, cos_ref, sin_ref, d_ref, qd_ref, kd_ref, bd_ref, gng_ref, dp_ref, dgn_ref, dstate_s):
        b, i = pl.program_id(0), pl.program_id(1)

        @pl.when(i == 0)
        def _():
            dstate_s[...] = jnp.zeros_like(dstate_s)

        @pl.when((b == 0) & (i == 0))
        def _():
            dgn_ref[...] = jnp.zeros_like(dgn_ref)
        c, s = cos_ref[...], sin_ref[...]
        for h in range(RET_HEADS):
            vs = slice(h * RET_DV, (h + 1) * RET_DV)
            stb = st_ref[h]
            q, k, qb, kb, v, scb, o = _ret_block_fwd(p_ref, h, c, s, d_ref, qd_ref, kd_ref, stb)
            gate = p_ref[:, 4096 + h * RET_DV:4096 + (h + 1) * RET_DV].astype(F32)
            mu = jnp.mean(o, axis=-1, keepdims=True)
            oc = o - mu
            rstd = lax.rsqrt(jnp.mean(oc * oc, axis=-1, keepdims=True) + EPS)
            xh = oc * rstd
            gng = gng_ref[:, vs]
            dyh = dy_ref[:, vs].astype(F32)
            sg = _sigmoid(gate)
            silu = gate * sg
            dgn_ref[:, vs] += jnp.sum(dyh * silu * xh, axis=0, keepdims=True)
            dxh = dyh * silu * gng
            do = rstd * (dxh - jnp.mean(dxh, axis=-1, keepdims=True) - xh * jnp.mean(dxh * xh, axis=-1, keepdims=True))
            dgate = dyh * xh * gng * (sg * (1.0 + gate * (1.0 - sg)))
            dob = do.astype(BF16)
            dsb = (_dot_nt(dob, v) * d_ref[h]).astype(BF16)
            dst = dstate_s[h]
            dstb = dst.astype(BF16)
            kdk = (k * kd_ref[h]).astype(BF16)
            dqr = _dot(dsb, kb) + qd_ref[h] * _dot_nt(dob, stb)
            dkr = _dot_tn(dsb, qb) + kd_ref[h] * _dot_nt(v, dstb)
            dv = _dot_tn(scb, dob) + _dot(kdk, dstb)
            dstate_s[h] = dst * bd_ref[h][:, :1] + _dot_tn((q * qd_ref[h]).astype(BF16), dob)
            dp_ref[:, h * RET_DK:(h + 1) * RET_DK] = _rope_inv(dqr, c, s).astype(BF16)
            dp_ref[:, 1024 + h * RET_DK:1024 + (h + 1) * RET_DK] = (_rope_inv(dkr, c, s) * (RET_DK ** -0.5)).astype(BF16)
            dp_ref[:, 2048 + h * RET_DV:2048 + (h + 1) * RET_DV] = dv.astype(BF16)
            dp_ref[:, 4096 + h * RET_DV:4096 + (h + 1) * RET_DV] = dgate.astype(BF16)

    const = lambda b, i: (0, 0, 0)
    rev = lambda b, i: (b * nb + nb - 1 - i, 0)
    return pl.pallas_call(
        body, name="ret_bwd", grid=(B, nb),
        in_specs=[pl.BlockSpec((LB, 6144), rev), pl.BlockSpec((LB, 2048), rev),
                  pl.BlockSpec((None, None, RET_HEADS, RET_DK, RET_DV), lambda b, i: (b, nb - 1 - i, 0, 0, 0)),
                  pl.BlockSpec((LB, 128), lambda b, i: (nb - 1 - i, 0)), pl.BlockSpec((LB, 128), lambda b, i: (nb - 1 - i, 0)),
                  pl.BlockSpec((RET_HEADS, LB, LB), const), pl.BlockSpec((RET_HEADS, LB, 1), const),
                  pl.BlockSpec((RET_HEADS, LB, 1), const), pl.BlockSpec((RET_HEADS, 1, 128), const),
                  pl.BlockSpec((1, 2048), lambda b, i: (0, 0))],
        out_specs=[pl.BlockSpec((LB, 6144), rev), pl.BlockSpec((1, 2048), lambda b, i: (0, 0))],
        out_shape=[jax.ShapeDtypeStruct((T, 6144), BF16), jax.ShapeDtypeStruct((1, 2048), F32)],
        scratch_shapes=[pltpu.VMEM((RET_HEADS, RET_DK, RET_DV), F32)], compiler_params=_params(2),
    )(proj, dy, states, cos, sin, dmat, qd, kd, bd, gn_g.reshape(1, 2048))


BIAS_LANES = 4 * ATT_BLOCK


def _diag_onehot():
    r = lax.broadcasted_iota(jnp.int32, (REL_TABLE, BIAS_LANES), 0)
    j = lax.broadcasted_iota(jnp.int32, (REL_TABLE, BIAS_LANES), 1)
    idx = jnp.maximum(j - ATT_BLOCK - PAST, -MAX_REL) + MAX_REL
    return jnp.where(idx == r, 1.0, 0.0).astype(F32)


def _row_is(j):
    return lax.broadcasted_iota(jnp.int32, (8, BIAS_LANES), 0) == j


def _att_bias(table):
    QB, KW = ATT_BLOCK, 3 * ATT_BLOCK

    def body(t_ref, b_ref, bt_ref):
        row = jnp.broadcast_to(t_ref[...], (8, REL_TABLE))
        diag = jnp.dot(row, _diag_onehot(), preferred_element_type=F32, precision=lax.Precision.HIGHEST)
        rows = jnp.zeros((8, BIAS_LANES), F32)
        for j in range(8):
            rows = jnp.where(_row_is(j), diag if j == 0 else pltpu.roll(diag, j, axis=1), rows)
        n = 8
        while n < QB:
            rows = jnp.concatenate([rows, pltpu.roll(rows, n, axis=1)], axis=0)
            n *= 2
        bias = rows[:, QB:]
        qi = lax.broadcasted_iota(jnp.int32, (QB, KW), 0)
        kj = lax.broadcasted_iota(jnp.int32, (QB, KW), 1)
        lo = (qi // CHUNK) * CHUNK
        bias = jnp.where((kj >= lo) & (kj < lo + PAST + CHUNK), bias, NEG)
        b_ref[...] = bias
        bt_ref[...] = bias.T

    return pl.pallas_call(
        body, name="att_bias", grid=(ATT_HEADS,),
        in_specs=[pl.BlockSpec((None, 1, REL_TABLE), lambda h: (h, 0, 0))],
        out_specs=[pl.BlockSpec((None, QB, KW), lambda h: (h, 0, 0)), pl.BlockSpec((None, KW, QB), lambda h: (h // 2, 0, h % 2))],
        out_shape=[jax.ShapeDtypeStruct((ATT_HEADS, QB, KW), F32), jax.ShapeDtypeStruct((ATT_HEADS // 2, KW, 2 * QB), F32)],
        compiler_params=_params(1),
    )(table.reshape(ATT_HEADS, 1, REL_TABLE))


def _att_bias_grad(dbias_t):
    QB, KW = ATT_BLOCK, 3 * ATT_BLOCK

    def body(d_ref, o_ref):
        rows = jnp.concatenate([jnp.zeros((QB, QB), F32), d_ref[...].T], axis=1)
        n = QB // 2
        while n >= 8:
            rows = rows[:n] + pltpu.roll(rows[n:], BIAS_LANES - n, axis=1)
            n //= 2
        acc = jnp.zeros((8, BIAS_LANES), F32)
        for j in range(8):
            acc = acc + jnp.where(_row_is(j), rows if j == 0 else pltpu.roll(rows, BIAS_LANES - j, axis=1), 0.0)
        diag = jnp.broadcast_to(jnp.sum(acc, axis=0, keepdims=True), (8, BIAS_LANES))
        grad = lax.dot_general(diag, _diag_onehot(), (((1,), (1,)), ((), ())), preferred_element_type=F32,
                               precision=lax.Precision.HIGHEST)
        o_ref[...] = grad[:1]

    return pl.pallas_call(
        body, name="att_bias_grad", grid=(ATT_HEADS,),
        in_specs=[pl.BlockSpec((None, KW, QB), lambda h: (h // 2, 0, h % 2))],
        out_specs=pl.BlockSpec((None, 1, REL_TABLE), lambda h: (h, 0, 0)),
        out_shape=jax.ShapeDtypeStruct((ATT_HEADS, 1, REL_TABLE), F32), compiler_params=_params(1),
    )(dbias_t).reshape(ATT_HEADS, REL_TABLE)


def _att_fwd(qkv, bias, *, B, S):
    QB = ATT_BLOCK
    nb = S // QB
    KW = 3 * QB
    T = B * S
    scale = ATT_DH ** -0.5

    def body(q_ref, k0, k1, k2, v0, v1, v2, b_ref, o_ref, lse_ref):
        i = pl.program_id(2)
        k3 = jnp.concatenate([k0[...], k1[...], k2[...]], axis=0)
        v3 = jnp.concatenate([v0[...], v1[...], v2[...]], axis=0)
        col = lax.broadcasted_iota(jnp.int32, (QB, KW), 1)
        in_seq = col >= (2 - i) * QB
        lane = lax.broadcasted_iota(jnp.int32, (QB, 128), 1)
        lane_kv = lax.broadcasted_iota(jnp.int32, (KW, 128), 1)
        q = (q_ref[...].astype(F32) * scale).astype(BF16)
        out = jnp.zeros((QB, 128), F32)
        lse = jnp.zeros((QB, 128), F32)
        for hh in range(2):
            qh = jnp.where((lane < ATT_DH) == (hh == 0), q, jnp.zeros_like(q))
            vh = jnp.where((lane_kv < ATT_DH) == (hh == 0), v3, jnp.zeros_like(v3))
            s = jnp.where(in_seq, _dot_nt(qh, k3) + b_ref[hh], NEG)
            m = jnp.max(s, axis=-1, keepdims=True)
            e = jnp.exp(s - m)
            l = jnp.sum(e, axis=-1, keepdims=True)
            out = out + _dot(e.astype(BF16), vh) / l
            lse = jnp.where(lane == hh, m + jnp.log(l), lse)
        o_ref[...] = out.astype(BF16)
        lse_ref[...] = lse.T[:8]

    def kv(d, col0):
        return pl.BlockSpec((QB, 128), lambda hp, b, i: (b * nb + jnp.maximum(i - d, 0), col0 + hp))

    return pl.pallas_call(
        body, name="att_fwd", grid=(8, B, nb),
        in_specs=[pl.BlockSpec((QB, 128), lambda hp, b, i: (b * nb + i, hp)),
                  kv(2, 8), kv(1, 8), kv(0, 8), kv(2, 16), kv(1, 16), kv(0, 16),
                  pl.BlockSpec((2, QB, KW), lambda hp, b, i: (hp, 0, 0))],
        out_specs=[pl.BlockSpec((QB, 128), lambda hp, b, i: (b * nb + i, hp)),
                   pl.BlockSpec((None, 8, QB), lambda hp, b, i: (hp, 0, b * nb + i))],
        out_shape=[jax.ShapeDtypeStruct((T, 1024), BF16), jax.ShapeDtypeStruct((8, 8, T), F32)], compiler_params=_params(3),
    )(qkv, qkv, qkv, qkv, qkv, qkv, qkv, bias)


def _att_bwd(qkv, do, o, lse, bias_t, *, B, S):
    QB = ATT_BLOCK
    nb = S // QB
    KW = 3 * QB
    T = B * S
    scale = ATT_DH ** -0.5
    TK = 256

    def body(q_ref, k0, k1, k2, v0, v1, v2, do_ref, o_ref, lse_ref, b_ref, dq_ref, dk_ref, dv_ref, db_ref, dk_acc, dv_acc):
        b, i = pl.program_id(1), pl.program_id(2)

        @pl.when(i == 0)
        def _():
            dk_acc[...] = jnp.zeros_like(dk_acc)
            dv_acc[...] = jnp.zeros_like(dv_acc)

        @pl.when((b == 0) & (i == 0))
        def _():
            db_ref[...] = jnp.zeros_like(db_ref)

        @pl.when(i < nb)
        def _():
            lane = lax.broadcasted_iota(jnp.int32, (QB, 128), 1)
            first = lane < ATT_DH

            def by_head(x):
                zero = jnp.zeros_like(x)
                return jnp.concatenate([jnp.where(first, x, zero), jnp.where(first, zero, x)], axis=0)

            dout = do_ref[...]
            q2 = by_head((q_ref[...].astype(F32) * scale).astype(BF16))
            do2 = by_head(dout)
            delta_t = (o_ref[...].astype(F32) * dout.astype(F32)).T
            delta2 = jnp.concatenate([jnp.sum(delta_t[:ATT_DH], axis=0, keepdims=True),
                                      jnp.sum(delta_t[ATT_DH:], axis=0, keepdims=True)], axis=1)
            lse2 = jnp.concatenate([lse_ref[0:1, :], lse_ref[1:2, :]], axis=1)
            dq_t = jnp.zeros((128, 2 * QB), F32)
            for d, (k_ref, v_ref) in enumerate(((k0, v0), (k1, v1), (k2, v2))):
                kblk, vblk = k_ref[...], v_ref[...]
                kt = kblk.astype(F32).T.astype(BF16)
                lse_d = jnp.where(i + d >= 2, lse2, -NEG)
                slot = (i + 1 + d) % 3
                for t in range(QB // TK):
                    rows = slice(t * TK, (t + 1) * TK)
                    wrows = slice(d * QB + t * TK, d * QB + (t + 1) * TK)
                    p = jnp.exp(_dot_nt(kblk[rows], q2) + b_ref[wrows, :] - lse_d)
                    ds = p * (_dot_nt(vblk[rows], do2) - delta2)
                    db_ref[wrows, :] += ds
                    dsb = ds.astype(BF16)
                    dk_acc[slot, rows, :] += _dot(dsb, q2)
                    dv_acc[slot, rows, :] += _dot(p.astype(BF16), do2)
                    dq_t += _dot(kt[:, rows], dsb)
            row = lax.broadcasted_iota(jnp.int32, (128, QB), 0)
            dq_ref[...] = (jnp.where(row < ATT_DH, dq_t[:, :QB], dq_t[:, QB:]) * scale).T.astype(BF16)

        @pl.when(i >= 2)
        def _():
            slot = (i + 1) % 3
            dk_ref[...] = dk_acc[slot].astype(BF16)
            dv_ref[...] = dv_acc[slot].astype(BF16)
            dk_acc[slot] = jnp.zeros((QB, 128), F32)
            dv_acc[slot] = jnp.zeros((QB, 128), F32)

    def qrow(b, i):
        return b * nb + jnp.minimum(i, nb - 1)

    def kv(d, col0):
        return pl.BlockSpec((QB, 128), lambda hp, b, i: (b * nb + jnp.maximum(jnp.minimum(i, nb - 1) - d, 0), col0 + hp))

    late = pl.BlockSpec((QB, 128), lambda hp, b, i: (b * nb + jnp.maximum(i - 2, 0), hp))
    return pl.pallas_call(
        body, name="att_bwd", grid=(8, B, nb + 2),
        in_specs=[pl.BlockSpec((QB, 128), lambda hp, b, i: (qrow(b, i), hp)),
                  kv(2, 8), kv(1, 8), kv(0, 8), kv(2, 16), kv(1, 16), kv(0, 16),
                  pl.BlockSpec((QB, 128), lambda hp, b, i: (qrow(b, i), hp)),
                  pl.BlockSpec((QB, 128), lambda hp, b, i: (qrow(b, i), hp)),
                  pl.BlockSpec((None, 8, QB), lambda hp, b, i: (hp, 0, qrow(b, i))),
                  pl.BlockSpec((None, KW, 2 * QB), lambda hp, b, i: (hp, 0, 0))],
        out_specs=[pl.BlockSpec((QB, 128), lambda hp, b, i: (qrow(b, i), hp)), late, late,
                   pl.BlockSpec((None, KW, 2 * QB), lambda hp, b, i: (hp, 0, 0))],
        out_shape=[jax.ShapeDtypeStruct((T, 1024), BF16)] * 3 + [jax.ShapeDtypeStruct((ATT_HEADS // 2, KW, 2 * QB), F32)],
        scratch_shapes=[pltpu.VMEM((3, QB, 128), F32), pltpu.VMEM((3, QB, 128), F32)], compiler_params=_params(3),
    )(qkv, qkv, qkv, qkv, qkv, qkv, qkv, do, o, lse, bias_t)


def _loss_head(h, tgt, g, *, tm):
    T, D = h.shape
    n = T // tm

    def body(h_ref, t_ref, g_ref, dh_ref, dg_ref, loss_ref, acc_ref):
        i = pl.program_id(0)
        x = h_ref[...]
        r = lax.rsqrt(jnp.mean(x * x, axis=-1, keepdims=True) + EPS)
        xh = x * r
        gg = g_ref[...]
        diff = xh * gg - t_ref[...]
        sq = jnp.sum(diff * diff, axis=0, keepdims=True)
        dy = diff * (1.0 / D)
        dg_part = jnp.sum(dy * xh, axis=0, keepdims=True)

        @pl.when(i == 0)
        def _():
            acc_ref[...] = sq
            dg_ref[...] = dg_part

        @pl.when(i > 0)
        def _():
            acc_ref[...] += sq
            dg_ref[...] += dg_part
        t = dy * gg
        dh_ref[...] = r * (t - xh * jnp.mean(t * xh, axis=-1, keepdims=True))

        @pl.when(i == n - 1)
        def _():
            loss_ref[...] = (0.5 / D) * jnp.sum(acc_ref[...], axis=1, keepdims=True)

    return pl.pallas_call(
        body, name="loss_head", grid=(n,),
        in_specs=[pl.BlockSpec((tm, D), lambda i: (i, 0)), pl.BlockSpec((tm, D), lambda i: (i, 0)),
                  pl.BlockSpec((1, D), lambda i: (0, 0))],
        out_specs=[pl.BlockSpec((tm, D), lambda i: (i, 0)), pl.BlockSpec((1, D), lambda i: (0, 0)),
                   pl.BlockSpec((1, 1), lambda i: (0, 0))],
        out_shape=[jax.ShapeDtypeStruct((T, D), F32), jax.ShapeDtypeStruct((1, D), F32), jax.ShapeDtypeStruct((1, 1), F32)],
        scratch_shapes=[pltpu.VMEM((1, D), F32)], compiler_params=_params(1),
    )(h, tgt, g.reshape(1, D))


def _tok_tile(T, want):
    t = min(T, want)
    assert T % t == 0
    return t


def _chip_sums(place, names, gw, tag):
    from_sibling = _exchange_with_sibling([gw[n] for n in names], tag)
    return {n: _add_sibling(place, gw[n], r, name="chip_sum_" + n) for n, r in zip(names, from_sibling)}


def _step(x, tgt, slab, slab_rel, place, mix_g, gn_g, mlp_g, fin_g):
    B, S, D = x.shape
    T = B * S
    h0 = x.reshape(T, D)
    tgt = tgt.reshape(T, D)
    tm = _tok_tile(T, 1024)
    tb = _tok_tile(T, 512)
    tq = _tok_tile(T, 256)
    tt = _tok_tile(T, 2048)
    consts = _ret_consts(S, RET_BLOCK)
    w = {}

    (w["ret_w_in"],) = _all_gather_slabs([slab["ret_w_in"]])
    (proj, hn0), (w["ret_w_out"], w["mlp_w1_0"], w["mlp_w2_0"]) = _mm_nn(
        h0, w["ret_w_in"], "col", tm=tb, out_dtype=BF16, name="ret_in", norm_g=mix_g[0],
        carry=_gather_carry([slab["ret_w_out"], slab["mlp_w1_0"], slab["mlp_w2_0"]]))
    y_ret, states = _ret_fwd(proj, gn_g, consts, B=B, S=S)
    h1, (w["att_w_in"], w["att_w_out"], rel_slabs) = _mm_nn(
        y_ret, w["ret_w_out"], "row", tm=tm, out_dtype=F32, name="ret_out", res=h0,
        carry=_gather_carry([slab["att_w_in"], slab["att_w_out"], slab_rel]))
    (a0, hm0), (w["mlp_w1_1"],) = _mm_nn(h1, w["mlp_w1_0"], "col", tm=tb, out_dtype=BF16, name="mlp0_up", norm_g=mlp_g[0],
                                         act="relu2", carry=_gather_carry([slab["mlp_w1_1"]]))
    h2, (w["mlp_w2_1"],) = _mm_nn(a0, w["mlp_w2_0"], "row", tm=tb, out_dtype=F32, name="mlp0_down", res=h1,
                                  carry=_gather_carry([slab["mlp_w2_1"]]))
    rel_bias = jnp.transpose(rel_slabs, (1, 0, 2)).reshape(ATT_HEADS, REL_TABLE)
    bias, bias_t = _att_bias(rel_bias)
    qkv, hn1 = _mm_nn(h2, w["att_w_in"], "col", tm=tb, out_dtype=BF16, name="att_in", norm_g=mix_g[1])
    o_att, lse = _att_fwd(qkv, bias, B=B, S=S)
    h3 = _mm_nn(o_att, w["att_w_out"], "row", tm=tm, out_dtype=F32, name="att_out", res=h2)
    a1, hm1 = _mm_nn(h3, w["mlp_w1_1"], "col", tm=tb, out_dtype=BF16, name="mlp1_up", norm_g=mlp_g[1], act="relu2")
    h4 = _mm_nn(a1, w["mlp_w2_1"], "row", tm=tb, out_dtype=F32, name="mlp1_down", res=h3)
    dh4, d_fin_g, loss = _loss_head(h4, tgt, fin_g, tm=tb)

    gw = {}
    gw["mlp_w2_1"] = _mm_tn(a1, dh4, "row", tt=tt, tk=1024, tn=D, name="d_mlp1_w2")
    dz1 = _mm_nt(dh4, w["mlp_w2_1"], "row", tm=tb, name="d_mlp1_act", epi="sqrt2", a2=a1)
    gw["mlp_w1_1"] = _mm_tn(hm1, dz1, "col", tt=tt, tk=D, tn=1024, name="d_mlp1_w1")
    dh3, d_mlp_g1 = _mm_nt(dz1, w["mlp_w1_1"], "col", tm=tb, name="d_mlp1_in", epi="normbwd",
                           h=h3, g=mlp_g[1], dres=dh4)
    gw["att_w_out"] = _mm_tn(o_att, dh3, "row", tt=tt, tk=256, tn=D, name="d_att_wout")
    do_att = _mm_nt(dh3, w["att_w_out"], "row", tm=tb, name="d_att_o", epi="bf16")
    dq, dk, dv, dbias_t = _att_bwd(qkv, do_att, o_att, lse, bias_t, B=B, S=S)
    d_rel = _att_bias_grad(dbias_t)
    dqkv = jnp.concatenate([dq, dk, dv], axis=1)
    gw["att_w_in"] = _mm_tn(hn1, dqkv, "col", tt=tt, tk=D, tn=768, name="d_att_win")
    sums = _chip_sums(place, ["mlp_w1_1", "mlp_w2_1", "att_w_in", "att_w_out"], gw, "layer1")
    landed = {}

    def carried_exchange(names):
        return _chips_carry([sums[n][0] for n in names])

    def keep(names, carried):
        landed.update(zip(names, carried[len(names):]))

    dh2, d_mix_g1 = _mm_nt(dqkv, w["att_w_in"], "col", tm=tb, name="d_att_in", epi="normbwd",
                           h=h2, g=mix_g[1], dres=dh3)
    gw["mlp_w2_0"] = _mm_tn(a0, dh2, "row", tt=tt, tk=1024, tn=D, name="d_mlp0_w2")
    dz0, carried = _mm_nt(dh2, w["mlp_w2_0"], "row", tm=tb, name="d_mlp0_act", epi="sqrt2", a2=a0,
                          carry=carried_exchange(["mlp_w1_1", "mlp_w2_1"]))
    keep(["mlp_w1_1", "mlp_w2_1"], carried)
    gw["mlp_w1_0"] = _mm_tn(hm0, dz0, "col", tt=tt, tk=D, tn=1024, name="d_mlp0_w1")
    (dh1, d_mlp_g0), carried = _mm_nt(dz0, w["mlp_w1_0"], "col", tm=tb, name="d_mlp0_in", epi="normbwd",
                                      h=h1, g=mlp_g[0], dres=dh2, carry=carried_exchange(["att_w_in", "att_w_out"]))
    keep(["att_w_in", "att_w_out"], carried)
    sums.update(_chip_sums(place, ["mlp_w1_0", "mlp_w2_0"], gw, "mlp0"))
    gw["ret_w_out"] = _mm_tn(y_ret, dh1, "row", tt=tt, tk=512, tn=D, name="d_ret_wout")
    sums.update(_chip_sums(place, ["ret_w_out"], gw, "ret_out"))
    dy_ret = _mm_nt(dh1, w["ret_w_out"], "row", tm=tb, name="d_ret_y", epi="bf16")
    dproj, d_gn = _ret_bwd(proj, dy_ret, states, gn_g, consts, B=B, S=S)
    gw["ret_w_in"], carried = _mm_tn(hn0, dproj, "col", tt=tt, tk=D, tn=768, name="d_ret_win",
                                     carry=carried_exchange(["mlp_w1_0", "mlp_w2_0", "ret_w_out"]))
    keep(["mlp_w1_0", "mlp_w2_0", "ret_w_out"], carried)
    sums.update(_chip_sums(place, ["ret_w_in"], gw, "ret_in"))
    (dx, d_mix_g0), carried = _mm_nt(dproj, w["ret_w_in"], "col", tm=tq, name="d_ret_in", epi="normbwd",
                                     h=h0, g=mix_g[0], dres=dh1, carry=carried_exchange(["ret_w_in"]))
    keep(["ret_w_in"], carried)
    totals = {n: _add_chips(place, sums[n][1], landed[n], name="total_" + n) for n in BIG}
    small = [d_mix_g0, d_mix_g1, d_mlp_g0, d_mlp_g1, d_fin_g, d_gn.reshape(2, D), d_rel.reshape(5, D)]
    return loss.reshape(()), dx.reshape(B, S, D), totals, small


def _row_tile(r, want=256):
    t = min(r, want)
    assert r % t == 0
    return t


def _into_slab(place, a, layer, dtype, name):
    _, r, c = a.shape
    tr = _row_tile(r)

    def body(place_ref, a_ref, o_ref):
        o_ref[...] = a_ref[...].astype(dtype)

    grid_spec = pltpu.PrefetchScalarGridSpec(
        num_scalar_prefetch=1, grid=(r // tr,), in_specs=[pl.BlockSpec((None, tr, c), lambda i, pr: (layer, i, 0))],
        out_specs=pl.BlockSpec((None, tr, c), lambda i, pr: (pr[0], i, 0)),
    )
    return pl.pallas_call(
        body, name=name, grid_spec=grid_spec, out_shape=jax.ShapeDtypeStruct((N_CHIPS, r, c), dtype), compiler_params=_params(1),
    )(place, a)


def _add_sibling(place, g, recv, name):
    _, r, c = g.shape
    hr = r // 2
    tr = _row_tile(hr)
    nrt = hr // tr

    def body(place_ref, g_ref, r_ref, sb_ref, own_ref):
        v = g_ref[...] + r_ref[...]
        sb_ref[...] = v.astype(BF16)

        @pl.when(pl.program_id(1) == place_ref[0])
        def _():
            own_ref[...] = v

    grid_spec = pltpu.PrefetchScalarGridSpec(
        num_scalar_prefetch=1, grid=(nrt, N_CHIPS),
        in_specs=[pl.BlockSpec((None, tr, c), lambda i, s, pr: (s, pr[1] * nrt + i, 0)),
                  pl.BlockSpec((None, tr, c), lambda i, s, pr: (s, i, 0))],
        out_specs=[pl.BlockSpec((None, tr, c), lambda i, s, pr: (s, i, 0)), pl.BlockSpec((tr, c), lambda i, s, pr: (i, 0))],
    )
    return pl.pallas_call(
        body, name=name, grid_spec=grid_spec,
        out_shape=[jax.ShapeDtypeStruct((N_CHIPS, hr, c), BF16), jax.ShapeDtypeStruct((hr, c), F32)],
        compiler_params=_params(2),
    )(place, g, recv)


def _add_chips(place, own, recv, name):
    hr, c = own.shape
    tr = _row_tile(hr)
    nrt = hr // tr

    def body(place_ref, o_ref, r_ref, t_ref):
        t_ref[...] = ((o_ref[...] + r_ref[0].astype(F32)) + r_ref[1].astype(F32)) + r_ref[2].astype(F32)

    grid_spec = pltpu.PrefetchScalarGridSpec(
        num_scalar_prefetch=1, grid=(nrt,),
        in_specs=[pl.BlockSpec((tr, c), lambda i, pr: (i, 0)), pl.BlockSpec((3, tr, c), lambda i, pr: (0, i, 0))],
        out_specs=pl.BlockSpec((tr, c), lambda i, pr: (pr[1] * nrt + i, 0)),
    )
    return pl.pallas_call(
        body, name=name, grid_spec=grid_spec, out_shape=jax.ShapeDtypeStruct((2 * hr, c), F32), compiler_params=_params(1),
    )(place, own, recv)


def _adamw(w, gs, m, v, name):
    L, r, c = w.shape
    tr = _row_tile(r)
    assert len(gs) == L

    def body(*refs):
        w_ref, m_ref, v_ref = refs[:3]
        g_refs = refs[3:3 + L]
        go_ref, d_ref, nm_ref, nv_ref = refs[3 + L:]
        gg = g_refs[0][...]
        for k in range(1, L):
            gg = jnp.where(pl.program_id(0) == k, g_refs[k][...], gg)
        go_ref[...] = gg
        nm = ADAM_B1 * m_ref[...] + (1.0 - ADAM_B1) * gg
        nv = ADAM_B2 * v_ref[...] + (1.0 - ADAM_B2) * (gg * gg)
        m_hat = nm / (1.0 - ADAM_B1 ** ADAM_STEP)
        v_hat = nv / (1.0 - ADAM_B2 ** ADAM_STEP)
        d_ref[...] = -ADAM_LR * (m_hat / (jnp.sqrt(v_hat) + ADAM_EPS) + ADAM_WD * w_ref[...])
        nm_ref[...] = nm
        nv_ref[...] = nv

    spec = pl.BlockSpec((None, tr, c), lambda l, i: (l, i, 0))
    return pl.pallas_call(
        body, name=name, grid=(L, r // tr), in_specs=[spec] * 3 + [pl.BlockSpec((tr, c), lambda l, i: (i, 0))] * L,
        out_specs=[spec] * 4, out_shape=[jax.ShapeDtypeStruct((L, r, c), F32)] * 4, compiler_params=_params(2),
    )(w, m, v, *gs)


def _place():
    return lax.axis_index("x"), lax.axis_index("y"), lax.axis_index("c")


def _other_chips(x, y):
    return [(1 - x, y), (x, 1 - y), (1 - x, 1 - y)]


def _remote(src, dst, ssem, rsem, dev):
    return pltpu.make_async_remote_copy(src_ref=src, dst_ref=dst, send_sem=ssem, recv_sem=rsem, device_id=dev,
                                        device_id_type=MESH)


def _gather_phases(n):
    def geometry(refs, t):
        x, y, c = _place()
        hr = refs[t].shape[1] // 2
        chips = _other_chips(x, y)
        return x, y, c, 2 * x + y, chips, [2 * qx + qy for qx, qy in chips], pl.ds(c * hr, hr), pl.ds((1 - c) * hr, hr)

    def send(refs, sems):
        s1, r1, _, _ = sems
        for t in range(n):
            x, y, c, p, chips, cidx, mine, theirs = geometry(refs, t)
            for j, (qx, qy) in enumerate(chips):
                _remote(refs[t].at[p, mine], refs[t].at[p, mine], s1.at[t, j], r1.at[t, j], (qx, qy, c)).start()

    def pass_on(refs, sems):
        s1, r1, s2, r2 = sems
        for t in range(n):
            x, y, c, p, chips, cidx, mine, theirs = geometry(refs, t)
            for j, (qx, qy) in enumerate(chips):
                got = refs[t].at[cidx[j], mine]
                _remote(got, got, s1.at[t, j], r1.at[t, j], (qx, qy, c)).wait_recv()
                _remote(got, got, s2.at[t, j], r2.at[t, j], (x, y, 1 - c)).start()

    def finish(refs, sems):
        s1, r1, s2, r2 = sems
        for t in range(n):
            x, y, c, p, chips, cidx, mine, theirs = geometry(refs, t)
            for j, (qx, qy) in enumerate(chips):
                got = refs[t].at[cidx[j], theirs]
                _remote(got, got, s2.at[t, j], r2.at[t, j], (x, y, 1 - c)).wait_recv()
        for t in range(n):
            x, y, c, p, chips, cidx, mine, theirs = geometry(refs, t)
            for j, (qx, qy) in enumerate(chips):
                _remote(refs[t].at[p, mine], refs[t].at[p, mine], s1.at[t, j], r1.at[t, j], (qx, qy, c)).wait_send()
                sent = refs[t].at[cidx[j], mine]
                _remote(sent, sent, s2.at[t, j], r2.at[t, j], (x, y, 1 - c)).wait_send()

    sem = pltpu.SemaphoreType.DMA
    return send, pass_on, finish, [sem((n, 3)), sem((n, 3)), sem((n, 3)), sem((n, 3))]


def _gather_carry(slabs):
    send, pass_on, finish, sems = _gather_phases(len(slabs))
    return _Carry(slabs, sems, [(0, send), (-2, pass_on), (-1, finish)])


def _all_gather_slabs(slabs):
    n = len(slabs)
    send, pass_on, finish, sems = _gather_phases(n)

    def body(*refs):
        outs, scratch = refs[n:2 * n], refs[2 * n:]
        send(outs, scratch)
        pass_on(outs, scratch)
        finish(outs, scratch)

    return pl.pallas_call(
        body, name="gather_weights", in_specs=[HBM_SPEC] * n, out_specs=[HBM_SPEC] * n,
        out_shape=[jax.ShapeDtypeStruct(s.shape, s.dtype) for s in slabs], input_output_aliases={t: t for t in range(n)},
        scratch_shapes=sems, compiler_params=pltpu.CompilerParams(has_side_effects=True),
    )(*slabs)


def _exchange_with_sibling(grads, tag):
    n = len(grads)

    def body(*refs):
        ins, outs = refs[:n], refs[n:2 * n]
        ssem, rsem = refs[2 * n:]
        x, y, c = _place()
        copies = []
        for t in range(n):
            hr = ins[t].shape[1] // 2
            cp = _remote(ins[t].at[:, pl.ds((1 - c) * hr, hr), :], outs[t], ssem.at[t], rsem.at[t], (x, y, 1 - c))
            cp.start()
            copies.append(cp)
        for cp in copies:
            cp.wait()

    sem = pltpu.SemaphoreType.DMA
    return pl.pallas_call(
        body, name="grads_to_sibling_" + tag, in_specs=[HBM_SPEC] * n, out_specs=[HBM_SPEC] * n,
        out_shape=[jax.ShapeDtypeStruct((N_CHIPS, g.shape[1] // 2, g.shape[2]), g.dtype) for g in grads],
        scratch_shapes=[sem((n,)), sem((n,))], compiler_params=pltpu.CompilerParams(has_side_effects=True),
    )(*grads)


def _chips_phases(n):
    def copies(refs, sems):
        ssem, rsem = sems
        x, y, c = _place()
        return [_remote(refs[t].at[2 * qx + qy], refs[n + t].at[j], ssem.at[t, j], rsem.at[t, j], (qx, qy, c))
                for t in range(n) for j, (qx, qy) in enumerate(_other_chips(x, y))]

    def send(refs, sems):
        for cp in copies(refs, sems):
            cp.start()

    def finish(refs, sems):
        for cp in copies(refs, sems):
            cp.wait()

    sem = pltpu.SemaphoreType.DMA
    return send, finish, [sem((n, 3)), sem((n, 3))]


def _landing(sums):
    return [lax.empty((3,) + s.shape[1:], s.dtype) for s in sums]


def _chips_carry(sums):
    send, finish, sems = _chips_phases(len(sums))
    return _Carry(list(sums) + _landing(sums), sems, [(0, send), (-1, finish)])


def _share_with_sibling(shards):
    n = len(shards)

    def body(*refs):
        outs = refs[n:2 * n]
        ssem, rsem = refs[2 * n:]
        x, y, c = _place()
        copies = []
        for t in range(n):
            hr = outs[t].shape[0] // 2
            mine = outs[t].at[pl.ds(c * hr, hr)]
            cp = _remote(mine, mine, ssem.at[t], rsem.at[t], (x, y, 1 - c))
            cp.start()
            copies.append(cp)
        for t in range(n):
            hr = outs[t].shape[0] // 2
            theirs = outs[t].at[pl.ds((1 - c) * hr, hr)]
            _remote(theirs, theirs, ssem.at[t], rsem.at[t], (x, y, 1 - c)).wait_recv()
        for cp in copies:
            cp.wait_send()

    sem = pltpu.SemaphoreType.DMA
    return pl.pallas_call(
        body, name="grads_share", in_specs=[HBM_SPEC] * n, out_specs=[HBM_SPEC] * n,
        out_shape=[jax.ShapeDtypeStruct(s.shape, s.dtype) for s in shards], input_output_aliases={t: t for t in range(n)},
        scratch_shapes=[sem((n,)), sem((n,))], compiler_params=pltpu.CompilerParams(has_side_effects=True),
    )(*shards)


def _all_reduce_small(buf):
    R, C = buf.shape

    def body(in_ref, out_ref, gather, ssem, rsem):
        x, y, c = _place()
        me = 4 * x + 2 * y + c
        gather[me] = in_ref[...]
        flips = [(fx, fy, fc) for fx in (0, 1) for fy in (0, 1) for fc in (0, 1) if fx or fy or fc]
        peers = [(x + fx - 2 * x * fx, y + fy - 2 * y * fy, c + fc - 2 * c * fc) for fx, fy, fc in flips]
        copies = [_remote(in_ref, gather.at[me], ssem.at[k], rsem.at[k], peer) for k, peer in enumerate(peers)]
        for cp in copies:
            cp.start()
        for k, (px, py, pc) in enumerate(peers):
            _remote(in_ref, gather.at[4 * px + 2 * py + pc], ssem.at[k], rsem.at[k], (px, py, pc)).wait_recv()
        for cp in copies:
            cp.wait_send()
        acc = gather[0]
        for d in range(1, 8):
            acc = acc + gather[d]
        out_ref[...] = acc

    sem = pltpu.SemaphoreType.DMA
    vmem = pl.BlockSpec(memory_space=pltpu.VMEM)
    return pl.pallas_call(
        body, name="small_grads_sum", in_specs=[vmem], out_specs=vmem, out_shape=jax.ShapeDtypeStruct((R, C), F32),
        scratch_shapes=[pltpu.VMEM((8, R, C), F32), sem((7,)), sem((7,))],
        compiler_params=pltpu.CompilerParams(has_side_effects=True),
    )(buf)


BIG = ["ret_w_in", "ret_w_out", "att_w_in", "att_w_out", "mlp_w1_0", "mlp_w1_1", "mlp_w2_0", "mlp_w2_1"]
LAYER_OF = {"ret_w_in": ("ret_w_in", 0), "ret_w_out": ("ret_w_out", 0), "att_w_in": ("att_w_in", 0), "att_w_out": ("att_w_out", 0),
            "mlp_w1_0": ("mlp_w1", 0), "mlp_w1_1": ("mlp_w1", 1), "mlp_w2_0": ("mlp_w2", 0), "mlp_w2_1": ("mlp_w2", 1)}
ORDER = ["mix_norm_g", "ret_w_in", "ret_gn_g", "ret_w_out", "att_w_in", "att_rel_bias", "att_w_out", "mlp_norm_g", "mlp_w1", "mlp_w2",
         "final_norm_g"]


def kernel(x, mix_norm_g, ret_w_in, ret_gn_g, ret_w_out, att_w_in, att_rel_bias, att_w_out, mlp_norm_g, mlp_w1, mlp_w2, final_norm_g, loss_target, m_mix_norm_g, m_ret_w_in, m_ret_gn_g, m_ret_w_out, m_att_w_in, m_att_rel_bias, m_att_w_out, m_mlp_norm_g, m_mlp_w1, m_mlp_w2, m_final_norm_g, v_mix_norm_g, v_ret_w_in, v_ret_gn_g, v_ret_w_out, v_att_w_in, v_att_rel_bias, v_att_w_out, v_mlp_norm_g, v_mlp_w1, v_mlp_w2, v_final_norm_g):
    xi, yi, ci = _place()
    chip = 2 * xi + yi
    weights = dict(zip(ORDER, (mix_norm_g, ret_w_in, ret_gn_g, ret_w_out, att_w_in, att_rel_bias, att_w_out, mlp_norm_g, mlp_w1,
                               mlp_w2, final_norm_g)))
    first = dict(zip(ORDER, (m_mix_norm_g, m_ret_w_in, m_ret_gn_g, m_ret_w_out, m_att_w_in, m_att_rel_bias, m_att_w_out,
                             m_mlp_norm_g, m_mlp_w1, m_mlp_w2, m_final_norm_g)))
    second = dict(zip(ORDER, (v_mix_norm_g, v_ret_w_in, v_ret_gn_g, v_ret_w_out, v_att_w_in, v_att_rel_bias, v_att_w_out,
                              v_mlp_norm_g, v_mlp_w1, v_mlp_w2, v_final_norm_g)))

    place = jnp.stack([chip, ci]).astype(jnp.int32)
    slab = {n: _into_slab(place, weights[LAYER_OF[n][0]], LAYER_OF[n][1], BF16, name="cast_" + n) for n in BIG}
    slab_rel = _into_slab(place, att_rel_bias, 0, F32, name="slab_rel_bias")

    loss_local, grad_x, totals, small = _step(x, loss_target, slab, slab_rel, place, mix_norm_g, ret_gn_g[0], mlp_norm_g,
                                              final_norm_g)
    loss = lax.psum(loss_local, ("x", "y", "c"))

    g_big = dict(zip(BIG, _share_with_sibling([totals[n] for n in BIG])))

    rows, at = jnp.zeros((16, D_MODEL), F32), 0
    for part in small:
        rows = rows + jnp.pad(part, ((at, 16 - at - part.shape[0]), (0, 0)))
        at += part.shape[0]
    rows = _all_reduce_small(rows)
    grads = {"mix_norm_g": [rows[0:2]], "mlp_norm_g": [rows[2:4]], "final_norm_g": [rows[4:5]], "ret_gn_g": [rows[5:7].reshape(1, 2048)],
             "att_rel_bias": [lax.dynamic_slice_in_dim(rows[7:12].reshape(ATT_HEADS, REL_TABLE), chip * (REL_TABLE // N_CHIPS),
                                                       REL_TABLE // N_CHIPS, axis=1)]}
    for n in BIG:
        grads.setdefault(LAYER_OF[n][0], []).append(g_big[n])

    def as3(a):
        return a.reshape((1,) * (3 - a.ndim) + a.shape)

    results = {}
    for n in ORDER:
        outs = _adamw(as3(weights[n]), grads[n], as3(first[n]), as3(second[n]), name="adamw_" + n)
        results[n] = [o.reshape(weights[n].shape) for o in outs]
    return (loss, grad_x) + tuple(results[n][k] for k in range(4) for n in ORDER)
```

```python
import jax
import jax.numpy as jnp
from jax import lax
from jax.experimental import pallas as pl
from jax.experimental.pallas import tpu as pltpu

F32 = jnp.float32
BF16 = jnp.bfloat16
MESH = pl.DeviceIdType.MESH

D_MODEL = 1024
CHUNK = 64
RET_HEADS = 4
RET_DK = 256
RET_DV = 512
ROPE_BASE = 10000.0
ATT_HEADS = 16
ATT_DH = 64
PAST = 512
MAX_REL = 256
REL_TABLE = MAX_REL + CHUNK
EPS = 1e-6
NEG = -1e30
N_CHIPS = 4

ADAM_LR = 0.001
ADAM_B1 = 0.9
ADAM_B2 = 0.999
ADAM_EPS = 1e-08
ADAM_WD = 0.01
ADAM_STEP = 10

RET_BLOCK = 256
ATT_BLOCK = 256
VMEM_LIMIT = 56 * 1024 * 1024


def _params(n_axes, **kw):
    return pltpu.CompilerParams(dimension_semantics=("arbitrary",) * n_axes, vmem_limit_bytes=VMEM_LIMIT, **kw)


def _dot(a, b):
    return jnp.dot(a, b, preferred_element_type=F32)


def _dot_nt(a, b):
    return lax.dot_general(a, b, (((1,), (1,)), ((), ())), preferred_element_type=F32)


def _dot_tn(a, b):
    return lax.dot_general(a, b, (((0,), (0,)), ((), ())), preferred_element_type=F32)


def _sigmoid(x):
    return 1.0 / (1.0 + jnp.exp(-x))


HBM_SPEC = pl.BlockSpec(memory_space=pltpu.HBM)


class _Carry:
    def __init__(self, arrays, sems, stages):
        self.arrays, self.sems, self.stages = list(arrays), list(sems), list(stages)


def _merge(a, b):
    na, sa = len(a.arrays), len(a.sems)

    def of_a(fn):
        return lambda refs, sems: fn(refs[:na], sems[:sa])

    def of_b(fn):
        return lambda refs, sems: fn(refs[na:], sems[sa:])

    return _Carry(a.arrays + b.arrays, a.sems + b.sems,
                  [(at, of_a(fn)) for at, fn in a.stages] + [(at, of_b(fn)) for at, fn in b.stages])


def _carry_call(body, carry, *, name, grid, in_specs, out_specs, out_shape, args):
    if carry is None:
        outs = pl.pallas_call(body, name=name, grid=grid, in_specs=in_specs, out_specs=out_specs, out_shape=out_shape,
                              compiler_params=_params(len(grid)))(*args)
        return list(outs), []
    n_in, n_out, n_c = len(in_specs), len(out_specs), len(carry.arrays)
    steps = 1
    for g in grid:
        steps *= g
    assert all(-steps <= at < steps for at, _ in carry.stages)

    def carrying(*refs):
        ins, outs = refs[:n_in], refs[n_in + n_c:n_in + n_c + n_out]
        carried = refs[n_in + n_c + n_out:n_in + 2 * n_c + n_out]
        sems = refs[n_in + 2 * n_c + n_out:]
        step = pl.program_id(0)
        for axis in range(1, len(grid)):
            step = step * grid[axis] + pl.program_id(axis)
        for at, fn in carry.stages:
            if at == 0:
                pl.when(step == 0)(lambda fn=fn: fn(carried, sems))
        body(*ins, *outs)
        for at, fn in carry.stages:
            if at != 0:
                pl.when(step == at % steps)(lambda fn=fn: fn(carried, sems))

    outs = pl.pallas_call(
        carrying, name=name, grid=grid, in_specs=list(in_specs) + [HBM_SPEC] * n_c, out_specs=list(out_specs) + [HBM_SPEC] * n_c,
        out_shape=list(out_shape) + [jax.ShapeDtypeStruct(a.shape, a.dtype) for a in carry.arrays],
        input_output_aliases={n_in + t: n_out + t for t in range(n_c)}, scratch_shapes=carry.sems,
        compiler_params=_params(len(grid), has_side_effects=True),
    )(*args, *carry.arrays)
    return list(outs[:n_out]), list(outs[n_out:])


def _mm_nn(a, w, wkind, *, tm, out_dtype, name, norm_g=None, act=None, res=None, carry=None):
    M, K = a.shape
    cols = w.shape[2]
    N = N_CHIPS * cols if wkind == "col" else cols
    has_norm = norm_g is not None
    assert M % tm == 0 and K == (w.shape[1] if wkind == "col" else N_CHIPS * w.shape[1])

    def body(*refs):
        it = iter(refs)
        a_ref, w_ref = next(it), next(it)
        g_ref = next(it) if has_norm else None
        r_ref = next(it) if res is not None else None
        o_ref = next(it)
        hn_ref = next(it) if has_norm else None
        if has_norm:
            x = a_ref[...].astype(F32)
            r = lax.rsqrt(jnp.mean(x * x, axis=-1, keepdims=True) + EPS)
            lhs = (x * r * g_ref[...]).astype(BF16)
            hn_ref[...] = lhs
        else:
            lhs = a_ref[...].astype(BF16)

        def finish(acc, sl):
            if act == "relu2":
                u = jnp.maximum(acc, 0.0)
                acc = u * u
            if r_ref is not None:
                acc = acc + r_ref[:, sl]
            o_ref[:, sl] = acc.astype(out_dtype)

        if wkind == "col":
            for s in range(N_CHIPS):
                finish(_dot(lhs, w_ref[s]), slice(s * cols, (s + 1) * cols))
        else:
            finish(_dot(lhs, w_ref[...].reshape(K, N)), slice(None))

    in_specs = [pl.BlockSpec((tm, K), lambda i: (i, 0)), pl.BlockSpec(w.shape, lambda i: (0, 0, 0))]
    args = [a, w]
    if has_norm:
        in_specs.append(pl.BlockSpec((1, K), lambda i: (0, 0)))
        args.append(norm_g.reshape(1, K))
    if res is not None:
        in_specs.append(pl.BlockSpec((tm, N), lambda i: (i, 0)))
        args.append(res)
    out_shape = [jax.ShapeDtypeStruct((M, N), out_dtype)]
    out_specs = [pl.BlockSpec((tm, N), lambda i: (i, 0))]
    if has_norm:
        out_shape.append(jax.ShapeDtypeStruct((M, K), BF16))
        out_specs.append(pl.BlockSpec((tm, K), lambda i: (i, 0)))
    outs, carried = _carry_call(body, carry, name=name, grid=(M // tm,), in_specs=in_specs, out_specs=out_specs,
                                out_shape=out_shape, args=args)
    result = outs if has_norm else outs[0]
    return result if carry is None else (result, carried)


def _mm_nt(a, w, wkind, *, tm, name, epi, a2=None, h=None, g=None, dres=None, carry=None):
    M, Nw = a.shape
    rows, cols = w.shape[1], w.shape[2]
    Kw = rows if wkind == "col" else N_CHIPS * rows
    assert M % tm == 0 and Nw == (N_CHIPS * cols if wkind == "col" else cols)
    assert epi != "normbwd" or wkind == "col"

    def body(*refs):
        it = iter(refs)
        a_ref, w_ref = next(it), next(it)
        a2_ref = next(it) if epi == "sqrt2" else None
        if epi == "normbwd":
            h_ref, g_ref, dres_ref = next(it), next(it), next(it)
        o_ref = next(it)
        dg_ref = next(it) if epi == "normbwd" else None
        i = pl.program_id(0)

        def finish(acc, sl):
            if epi == "bf16":
                o_ref[:, sl] = acc.astype(BF16)
            elif epi == "sqrt2":
                o_ref[:, sl] = (acc * (2.0 * jnp.sqrt(a2_ref[:, sl].astype(F32)))).astype(BF16)
            else:
                x = h_ref[...]
                r = lax.rsqrt(jnp.mean(x * x, axis=-1, keepdims=True) + EPS)
                xh = x * r
                dg_part = jnp.sum(acc * xh, axis=0, keepdims=True)

                @pl.when(i == 0)
                def _():
                    dg_ref[...] = dg_part

                @pl.when(i > 0)
                def _():
                    dg_ref[...] += dg_part
                t = acc * g_ref[...]
                dx = r * (t - xh * jnp.mean(t * xh, axis=-1, keepdims=True))
                o_ref[...] = dres_ref[...] + dx

        if wkind == "col":
            acc = _dot_nt(a_ref[:, :cols].astype(BF16), w_ref[0])
            for s in range(1, N_CHIPS):
                acc += _dot_nt(a_ref[:, s * cols:(s + 1) * cols].astype(BF16), w_ref[s])
            finish(acc, slice(None))
        else:
            lhs = a_ref[...].astype(BF16)
            for s in range(N_CHIPS):
                finish(_dot_nt(lhs, w_ref[s]), slice(s * rows, (s + 1) * rows))

    in_specs = [pl.BlockSpec((tm, Nw), lambda i: (i, 0)), pl.BlockSpec(w.shape, lambda i: (0, 0, 0))]
    args = [a, w]
    out_dtype = BF16
    if epi == "sqrt2":
        in_specs.append(pl.BlockSpec((tm, Kw), lambda i: (i, 0)))
        args.append(a2)
    if epi == "normbwd":
        in_specs += [pl.BlockSpec((tm, Kw), lambda i: (i, 0)), pl.BlockSpec((1, Kw), lambda i: (0, 0)),
                     pl.BlockSpec((tm, Kw), lambda i: (i, 0))]
        args += [h, g.reshape(1, Kw), dres]
        out_dtype = F32
    out_shape = [jax.ShapeDtypeStruct((M, Kw), out_dtype)]
    out_specs = [pl.BlockSpec((tm, Kw), lambda i: (i, 0))]
    if epi == "normbwd":
        out_shape.append(jax.ShapeDtypeStruct((1, Kw), F32))
        out_specs.append(pl.BlockSpec((1, Kw), lambda i: (0, 0)))
    outs, carried = _carry_call(body, carry, name=name, grid=(M // tm,), in_specs=in_specs, out_specs=out_specs,
                                out_shape=out_shape, args=args)
    result = outs if epi == "normbwd" else outs[0]
    return result if carry is None else (result, carried)


def _mm_tn(a, b, okind, *, tt, tk, tn, name, carry=None):
    T, K = a.shape
    N = b.shape[1]
    assert T % tt == 0 and K % tk == 0 and N % tn == 0
    nt = T // tt
    if okind == "col":
        per = (N // N_CHIPS) // tn
        assert (N // N_CHIPS) % tn == 0
        out_shape = jax.ShapeDtypeStruct((N_CHIPS, K, N // N_CHIPS), F32)
        out_spec = pl.BlockSpec((None, tk, tn), lambda ki, nj, t: (nj // per, ki, nj % per))
    else:
        per = (K // N_CHIPS) // tk
        assert (K // N_CHIPS) % tk == 0
        out_shape = jax.ShapeDtypeStruct((N_CHIPS, K // N_CHIPS, N), F32)
        out_spec = pl.BlockSpec((None, tk, tn), lambda ki, nj, t: (ki // per, ki % per, nj))

    def body(a_ref, b_ref, o_ref):
        t = pl.program_id(2)
        part = _dot_tn(a_ref[...].astype(BF16), b_ref[...].astype(BF16))

        @pl.when(t == 0)
        def _():
            o_ref[...] = part

        @pl.when(t > 0)
        def _():
            o_ref[...] += part

    outs, carried = _carry_call(
        body, carry, name=name, grid=(K // tk, N // tn, nt),
        in_specs=[pl.BlockSpec((tt, tk), lambda ki, nj, t: (t, ki)), pl.BlockSpec((tt, tn), lambda ki, nj, t: (t, nj))],
        out_specs=[out_spec], out_shape=[out_shape], args=[a, b])
    return outs[0] if carry is None else (outs[0], carried)


def _ret_consts(S, LB):
    log_gamma = jnp.log1p(-jnp.exp2(-5.0 - jnp.arange(RET_HEADS, dtype=F32)))
    idx = jnp.arange(LB, dtype=F32)
    n, m = idx[:, None], idx[None, :]
    cn, cm = jnp.floor(n / CHUNK), jnp.floor(m / CHUNK)
    dist = jnp.where(cm == cn, jnp.abs(n - m), n - m)
    dmat = jnp.where((cm <= cn)[None], jnp.exp(log_gamma[:, None, None] * dist[None]), 0.0)
    qd = jnp.exp(log_gamma[:, None] * (idx + 1.0)[None, :])[..., None]
    kd = jnp.exp(log_gamma[:, None] * (LB - 1 - idx)[None, :])[..., None]
    bd = jnp.exp(log_gamma * LB).reshape(RET_HEADS, 1, 1) * jnp.ones((RET_HEADS, 1, 128), F32)
    half = RET_DK // 2
    inv = jnp.exp(-jnp.log(ROPE_BASE) * jnp.arange(half, dtype=F32) / half)
    ang = jnp.arange(S, dtype=F32)[:, None] * inv[None, :]
    return dmat.astype(F32), qd.astype(F32), kd.astype(F32), bd, jnp.cos(ang), jnp.sin(ang)


def _rope(t, c, s):
    t1, t2 = t[:, :128], t[:, 128:]
    return jnp.concatenate([t1 * c - t2 * s, t1 * s + t2 * c], axis=-1)


def _rope_inv(d, c, s):
    d1, d2 = d[:, :128], d[:, 128:]
    return jnp.concatenate([d1 * c + d2 * s, d2 * c - d1 * s], axis=-1)


def _ret_block_fwd(p_ref, h, c, s, d_ref, qd_ref, kd_ref, stb):
    q = _rope(p_ref[:, h * RET_DK:(h + 1) * RET_DK].astype(F32), c, s)
    k = _rope(p_ref[:, 1024 + h * RET_DK:1024 + (h + 1) * RET_DK].astype(F32), c, s) * (RET_DK ** -0.5)
    v = p_ref[:, 2048 + h * RET_DV:2048 + (h + 1) * RET_DV]
    qb, kb = q.astype(BF16), k.astype(BF16)
    scb = (_dot_nt(qb, kb) * d_ref[h]).astype(BF16)
    o = _dot(scb, v) + qd_ref[h] * _dot(qb, stb)
    return q, k, qb, kb, v, scb, o


def _ret_fwd(proj, gn_g, consts, *, B, S):
    LB = RET_BLOCK
    nb = S // LB
    T = B * S
    dmat, qd, kd, bd, cos, sin = consts

    def body(p_ref, cos_ref, sin_ref, d_ref, qd_ref, kd_ref, bd_ref, gng_ref, y_ref, st_ref, state_s):
        i = pl.program_id(1)

        @pl.when(i == 0)
        def _():
            state_s[...] = jnp.zeros_like(state_s)
        c, s = cos_ref[...], sin_ref[...]
        for h in range(RET_HEADS):
            st = state_s[h]
            stb = st.astype(BF16)
            st_ref[h] = stb
            q, k, qb, kb, v, scb, o = _ret_block_fwd(p_ref, h, c, s, d_ref, qd_ref, kd_ref, stb)
            kdk = (k * kd_ref[h]).astype(BF16)
            state_s[h] = st * bd_ref[h][:, :1] + _dot_tn(kdk, v)
            gate = p_ref[:, 4096 + h * RET_DV:4096 + (h + 1) * RET_DV].astype(F32)
            mu = jnp.mean(o, axis=-1, keepdims=True)
            oc = o - mu
            xh = oc * lax.rsqrt(jnp.mean(oc * oc, axis=-1, keepdims=True) + EPS)
            y = (gate * _sigmoid(gate)) * (xh * gng_ref[:, h * RET_DV:(h + 1) * RET_DV])
            y_ref[:, h * RET_DV:(h + 1) * RET_DV] = y.astype(BF16)

    const = lambda b, i: (0, 0, 0)
    return pl.pallas_call(
        body, name="ret_fwd", grid=(B, nb),
        in_specs=[pl.BlockSpec((LB, 6144), lambda b, i: (b * nb + i, 0)),
                  pl.BlockSpec((LB, 128), lambda b, i: (i, 0)), pl.BlockSpec((LB, 128), lambda b, i: (i, 0)),
                  pl.BlockSpec((RET_HEADS, LB, LB), const), pl.BlockSpec((RET_HEADS, LB, 1), const),
                  pl.BlockSpec((RET_HEADS, LB, 1), const), pl.BlockSpec((RET_HEADS, 1, 128), const),
                  pl.BlockSpec((1, 2048), lambda b, i: (0, 0))],
        out_specs=[pl.BlockSpec((LB, 2048), lambda b, i: (b * nb + i, 0)),
                   pl.BlockSpec((None, None, RET_HEADS, RET_DK, RET_DV), lambda b, i: (b, i, 0, 0, 0))],
        out_shape=[jax.ShapeDtypeStruct((T, 2048), BF16), jax.ShapeDtypeStruct((B, nb, RET_HEADS, RET_DK, RET_DV), BF16)],
        scratch_shapes=[pltpu.VMEM((RET_HEADS, RET_DK, RET_DV), F32)], compiler_params=_params(2),
    )(proj, cos, sin, dmat, qd, kd, bd, gn_g.reshape(1, 2048))


def _ret_bwd(proj, dy, states, gn_g, consts, *, B, S):
    LB = RET_BLOCK
    nb = S // LB
    T = B * S
    dmat, qd, kd, bd, cos, sin = consts

    def body(p_ref, dy_ref, st_ref, cos_ref, sin_ref, d_ref, qd_ref, kd_ref, bd_ref, gng_ref, dp_ref, dgn_ref, dstate_s):
        b, i = pl.program_id(0), pl.program_id(1)

        @pl.when(i == 0)
        def _():
            dstate_s[...] = jnp.zeros_like(dstate_s)

        @pl.when((b == 0) & (i == 0))
        def _():
            dgn_ref[...] = jnp.zeros_like(dgn_ref)
        c, s = cos_ref[...], sin_ref[...]
        for h in range(RET_HEADS):
            vs = slice(h * RET_DV, (h + 1) * RET_DV)
            stb = st_ref[h]
            q, k, qb, kb, v, scb, o = _ret_block_fwd(p_ref, h, c, s, d_ref, qd_ref, kd_ref, stb)
            gate = p_ref[:, 4096 + h * RET_DV:4096 + (h + 1) * RET_DV].astype(F32)
            mu = jnp.mean(o, axis=-1, keepdims=True)
            oc = o - mu
            rstd = lax.rsqrt(jnp.mean(oc * oc, axis=-1, keepdims=True) + EPS)
            xh = oc * rstd
            gng = gng_ref[:, vs]
            dyh = dy_ref[:, vs].astype(F32)
            sg = _sigmoid(gate)
            silu = gate * sg
            dgn_ref[:, vs] += jnp.sum(dyh * silu * xh, axis=0, keepdims=True)
            dxh = dyh * silu * gng
            do = rstd * (dxh - jnp.mean(dxh, axis=-1, keepdims=True) - xh * jnp.mean(dxh * xh, axis=-1, keepdims=True))
            dgate = dyh * xh * gng * (sg * (1.0 + gate * (1.0 - sg)))
            dob = do.astype(BF16)
            dsb = (_dot_nt(dob, v) * d_ref[h]).astype(BF16)
            dst = dstate_s[h]
            dstb = dst.astype(BF16)
            kdk = (k * kd_ref[h]).astype(BF16)
            dqr = _dot(dsb, kb) + qd_ref[h] * _dot_nt(dob, stb)
            dkr = _dot_tn(dsb, qb) + kd_ref[h] * _dot_nt(v, dstb)
            dv = _dot_tn(scb, dob) + _dot(kdk, dstb)
            dstate_s[h] = dst * bd_ref[h][:, :1] + _dot_tn((q * qd_ref[h]).astype(BF16), dob)
            dp_ref[:, h * RET_DK:(h + 1) * RET_DK] = _rope_inv(dqr, c, s).astype(BF16)
            dp_ref[:, 1024 + h * RET_DK:1024 + (h + 1) * RET_DK] = (_rope_inv(dkr, c, s) * (RET_DK ** -0.5)).astype(BF16)
            dp_ref[:, 2048 + h * RET_DV:2048 + (h + 1) * RET_DV] = dv.astype(BF16)
            dp_ref[:, 4096 + h * RET_DV:4096 + (h + 1) * RET_DV] = dgate.astype(BF16)

    const = lambda b, i: (0, 0, 0)
    rev = lambda b, i: (b * nb + nb - 1 - i, 0)
    return pl.pallas_call(
        body, name="ret_bwd", grid=(B, nb),
        in_specs=[pl.BlockSpec((LB, 6144), rev), pl.BlockSpec((LB, 2048), rev),
                  pl.BlockSpec((None, None, RET_HEADS, RET_DK, RET_DV), lambda b, i: (b, nb - 1 - i, 0, 0, 0)),
                  pl.BlockSpec((LB, 128), lambda b, i: (nb - 1 - i, 0)), pl.BlockSpec((LB, 128), lambda b, i: (nb - 1 - i, 0)),
                  pl.BlockSpec((RET_HEADS, LB, LB), const), pl.BlockSpec((RET_HEADS, LB, 1), const),
                  pl.BlockSpec((RET_HEADS, LB, 1), const), pl.BlockSpec((RET_HEADS, 1, 128), const),
                  pl.BlockSpec((1, 2048), lambda b, i: (0, 0))],
        out_specs=[pl.BlockSpec((LB, 6144), rev), pl.BlockSpec((1, 2048), lambda b, i: (0, 0))],
        out_shape=[jax.ShapeDtypeStruct((T, 6144), BF16), jax.ShapeDtypeStruct((1, 2048), F32)],
        scratch_shapes=[pltpu.VMEM((RET_HEADS, RET_DK, RET_DV), F32)], compiler_params=_params(2),
    )(proj, dy, states, cos, sin, dmat, qd, kd, bd, gn_g.reshape(1, 2048))


BIAS_LANES = 4 * ATT_BLOCK


def _diag_onehot():
    r = lax.broadcasted_iota(jnp.int32, (REL_TABLE, BIAS_LANES), 0)
    j = lax.broadcasted_iota(jnp.int32, (REL_TABLE, BIAS_LANES), 1)
    idx = jnp.maximum(j - ATT_BLOCK - PAST, -MAX_REL) + MAX_REL
    return jnp.where(idx == r, 1.0, 0.0).astype(F32)


def _row_is(j):
    return lax.broadcasted_iota(jnp.int32, (8, BIAS_LANES), 0) == j


def _att_bias(table):
    QB, KW = ATT_BLOCK, 3 * ATT_BLOCK

    def body(t_ref, b_ref, bt_ref):
        row = jnp.broadcast_to(t_ref[...], (8, REL_TABLE))
        diag = jnp.dot(row, _diag_onehot(), preferred_element_type=F32, precision=lax.Precision.HIGHEST)
        rows = jnp.zeros((8, BIAS_LANES), F32)
        for j in range(8):
            rows = jnp.where(_row_is(j), diag if j == 0 else pltpu.roll(diag, j, axis=1), rows)
        n = 8
        while n < QB:
            rows = jnp.concatenate([rows, pltpu.roll(rows, n, axis=1)], axis=0)
            n *= 2
        bias = rows[:, QB:]
        qi = lax.broadcasted_iota(jnp.int32, (QB, KW), 0)
        kj = lax.broadcasted_iota(jnp.int32, (QB, KW), 1)
        lo = (qi // CHUNK) * CHUNK
        bias = jnp.where((kj >= lo) & (kj < lo + PAST + CHUNK), bias, NEG)
        b_ref[...] = bias
        bt_ref[...] = bias.T

    return pl.pallas_call(
        body, name="att_bias", grid=(ATT_HEADS,),
        in_specs=[pl.BlockSpec((None, 1, REL_TABLE), lambda h: (h, 0, 0))],
        out_specs=[pl.BlockSpec((None, QB, KW), lambda h: (h, 0, 0)), pl.BlockSpec((None, KW, QB), lambda h: (h // 2, 0, h % 2))],
        out_shape=[jax.ShapeDtypeStruct((ATT_HEADS, QB, KW), F32), jax.ShapeDtypeStruct((ATT_HEADS // 2, KW, 2 * QB), F32)],
        compiler_params=_params(1),
    )(table.reshape(ATT_HEADS, 1, REL_TABLE))


def _att_bias_grad(dbias_t):
    QB, KW = ATT_BLOCK, 3 * ATT_BLOCK

    def body(d_ref, o_ref):
        rows = jnp.concatenate([jnp.zeros((QB, QB), F32), d_ref[...].T], axis=1)
        n = QB // 2
        while n >= 8:
            rows = rows[:n] + pltpu.roll(rows[n:], BIAS_LANES - n, axis=1)
            n //= 2
        acc = jnp.zeros((8, BIAS_LANES), F32)
        for j in range(8):
            acc = acc + jnp.where(_row_is(j), rows if j == 0 else pltpu.roll(rows, BIAS_LANES - j, axis=1), 0.0)
        diag = jnp.broadcast_to(jnp.sum(acc, axis=0, keepdims=True), (8, BIAS_LANES))
        grad = lax.dot_general(diag, _diag_onehot(), (((1,), (1,)), ((), ())), preferred_element_type=F32,
                               precision=lax.Precision.HIGHEST)
        o_ref[...] = grad[:1]

    return pl.pallas_call(
        body, name="att_bias_grad", grid=(ATT_HEADS,),
        in_specs=[pl.BlockSpec((None, KW, QB), lambda h: (h // 2, 0, h % 2))],
        out_specs=pl.BlockSpec((None, 1, REL_TABLE), lambda h: (h, 0, 0)),
        out_shape=jax.ShapeDtypeStruct((ATT_HEADS, 1, REL_TABLE), F32), compiler_params=_params(1),
    )(dbias_t).reshape(ATT_HEADS, REL_TABLE)


def _att_fwd(qkv, bias, *, B, S):
    QB = ATT_BLOCK
    nb = S // QB
    KW = 3 * QB
    T = B * S
    scale = ATT_DH ** -0.5

    def body(q_ref, k0, k1, k2, v0, v1, v2, b_ref, o_ref, lse_ref):
        i = pl.program_id(2)
        k3 = jnp.concatenate([k0[...], k1[...], k2[...]], axis=0)
        v3 = jnp.concatenate([v0[...], v1[...], v2[...]], axis=0)
        TQ = QB
        col = lax.broadcasted_iota(jnp.int32, (TQ, KW), 1)
        in_seq = col >= (2 - i) * QB
        lane = lax.broadcasted_iota(jnp.int32, (TQ, 128), 1)
        lane_kv = lax.broadcasted_iota(jnp.int32, (KW, 128), 1)
        vh = [jnp.where((lane_kv < ATT_DH) == (hh == 0), v3, jnp.zeros_like(v3)) for hh in range(2)]
        outs, lses = [], []
        for r in range(QB // TQ):
            rows = slice(r * TQ, (r + 1) * TQ)
            q = (q_ref[rows, :].astype(F32) * scale).astype(BF16)
            out = jnp.zeros((TQ, 128), F32)
            lse = jnp.zeros((TQ, 128), F32)
            for hh in range(2):
                qh = jnp.where((lane < ATT_DH) == (hh == 0), q, jnp.zeros_like(q))
                s = jnp.where(in_seq, _dot_nt(qh, k3) + b_ref[hh, rows, :], NEG)
                m = jnp.max(s, axis=-1, keepdims=True)
                e = jnp.exp(s - m)
                l = jnp.sum(e, axis=-1, keepdims=True)
                out = out + _dot(e.astype(BF16), vh[hh]) / l
                lse = jnp.where(lane == hh, m + jnp.log(l), lse)
            outs.append(out)
            lses.append(lse)
        o_ref[...] = jnp.concatenate(outs, axis=0).astype(BF16)
        lse_ref[...] = jnp.concatenate(lses, axis=0).T[:8]

    def kv(d, col0):
        return pl.BlockSpec((QB, 128), lambda hp, b, i: (b * nb + jnp.maximum(i - d, 0), col0 + hp))

    return pl.pallas_call(
        body, name="att_fwd", grid=(8, B, nb),
        in_specs=[pl.BlockSpec((QB, 128), lambda hp, b, i: (b * nb + i, hp)),
                  kv(2, 8), kv(1, 8), kv(0, 8), kv(2, 16), kv(1, 16), kv(0, 16),
                  pl.BlockSpec((2, QB, KW), lambda hp, b, i: (hp, 0, 0))],
        out_specs=[pl.BlockSpec((QB, 128), lambda hp, b, i: (b * nb + i, hp)),
                   pl.BlockSpec((None, 8, QB), lambda hp, b, i: (hp, 0, b * nb + i))],
        out_shape=[jax.ShapeDtypeStruct((T, 1024), BF16), jax.ShapeDtypeStruct((8, 8, T), F32)], compiler_params=_params(3),
    )(qkv, qkv, qkv, qkv, qkv, qkv, qkv, bias)


def _att_bwd(qkv, do, o, lse, bias_t, *, B, S):
    QB = ATT_BLOCK
    nb = S // QB
    KW = 3 * QB
    T = B * S
    scale = ATT_DH ** -0.5
    TK = 256

    def body(q_ref, k0, k1, k2, v0, v1, v2, do_ref, o_ref, lse_ref, b_ref, dq_ref, dk_ref, dv_ref, db_ref, dk_acc, dv_acc):
        b, i = pl.program_id(1), pl.program_id(2)

        @pl.when(i == 0)
        def _():
            dk_acc[...] = jnp.zeros_like(dk_acc)
            dv_acc[...] = jnp.zeros_like(dv_acc)

        @pl.when((b == 0) & (i == 0))
        def _():
            db_ref[...] = jnp.zeros_like(db_ref)

        @pl.when(i < nb)
        def _():
            lane = lax.broadcasted_iota(jnp.int32, (QB, 128), 1)
            first = lane < ATT_DH

            def by_head(x):
                zero = jnp.zeros_like(x)
                return jnp.concatenate([jnp.where(first, x, zero), jnp.where(first, zero, x)], axis=0)

            dout = do_ref[...]
            q2 = by_head((q_ref[...].astype(F32) * scale).astype(BF16))
            do2 = by_head(dout)
            delta_t = (o_ref[...].astype(F32) * dout.astype(F32)).T
            delta2 = jnp.concatenate([jnp.sum(delta_t[:ATT_DH], axis=0, keepdims=True),
                                      jnp.sum(delta_t[ATT_DH:], axis=0, keepdims=True)], axis=1)
            lse2 = jnp.concatenate([lse_ref[0:1, :], lse_ref[1:2, :]], axis=1)
            dq_t = jnp.zeros((128, 2 * QB), F32)
            for d, (k_ref, v_ref) in enumerate(((k0, v0), (k1, v1), (k2, v2))):
                kblk, vblk = k_ref[...], v_ref[...]
                kt = kblk.astype(F32).T.astype(BF16)
                lse_d = jnp.where(i + d >= 2, lse2, -NEG)
                slot = (i + 1 + d) % 3
                for t in range(QB // TK):
                    rows = slice(t * TK, (t + 1) * TK)
                    wrows = slice(d * QB + t * TK, d * QB + (t + 1) * TK)
                    p = jnp.exp(_dot_nt(kblk[rows], q2) + b_ref[wrows, :] - lse_d)
                    ds = p * (_dot_nt(vblk[rows], do2) - delta2)
                    db_ref[wrows, :] += ds
                    dsb = ds.astype(BF16)
                    dk_acc[slot, rows, :] += _dot(dsb, q2)
                    dv_acc[slot, rows, :] += _dot(p.astype(BF16), do2)
                    dq_t += _dot(kt[:, rows], dsb)
            row = lax.broadcasted_iota(jnp.int32, (128, QB), 0)
            dq_ref[...] = (jnp.where(row < ATT_DH, dq_t[:, :QB], dq_t[:, QB:]) * scale).T.astype(BF16)

        @pl.when(i >= 2)
        def _():
            slot = (i + 1) % 3
            dk_ref[...] = dk_acc[slot].astype(BF16)
            dv_ref[...] = dv_acc[slot].astype(BF16)
            dk_acc[slot] = jnp.zeros((QB, 128), F32)
            dv_acc[slot] = jnp.zeros((QB, 128), F32)

    def qrow(b, i):
        return b * nb + jnp.minimum(i, nb - 1)

    def kv(d, col0):
        return pl.BlockSpec((QB, 128), lambda hp, b, i: (b * nb + jnp.maximum(jnp.minimum(i, nb - 1) - d, 0), col0 + hp))

    late = pl.BlockSpec((QB, 128), lambda hp, b, i: (b * nb + jnp.maximum(i - 2, 0), hp))
    return pl.pallas_call(
        body, name="att_bwd", grid=(8, B, nb + 2),
        in_specs=[pl.BlockSpec((QB, 128), lambda hp, b, i: (qrow(b, i), hp)),
                  kv(2, 8), kv(1, 8), kv(0, 8), kv(2, 16), kv(1, 16), kv(0, 16),
                  pl.BlockSpec((QB, 128), lambda hp, b, i: (qrow(b, i), hp)),
                  pl.BlockSpec((QB, 128), lambda hp, b, i: (qrow(b, i), hp)),
                  pl.BlockSpec((None, 8, QB), lambda hp, b, i: (hp, 0, qrow(b, i))),
                  pl.BlockSpec((None, KW, 2 * QB), lambda hp, b, i: (hp, 0, 0))],
        out_specs=[pl.BlockSpec((QB, 128), lambda hp, b, i: (qrow(b, i), hp)), late, late,
                   pl.BlockSpec((None, KW, 2 * QB), lambda hp, b, i: (hp, 0, 0))],
        out_shape=[jax.ShapeDtypeStruct((T, 1024), BF16)] * 3 + [jax.ShapeDtypeStruct((ATT_HEADS // 2, KW, 2 * QB), F32)],
        scratch_shapes=[pltpu.VMEM((3, QB, 128), F32), pltpu.VMEM((3, QB, 128), F32)], compiler_params=_params(3),
    )(qkv, qkv, qkv, qkv, qkv, qkv, qkv, do, o, lse, bias_t)


def _loss_head(h, tgt, g, *, tm):
    T, D = h.shape
    n = T // tm

    def body(h_ref, t_ref, g_ref, dh_ref, dg_ref, loss_ref, acc_ref):
        i = pl.program_id(0)
        x = h_ref[...]
        r = lax.rsqrt(jnp.mean(x * x, axis=-1, keepdims=True) + EPS)
        xh = x * r
        gg = g_ref[...]
        diff = xh * gg - t_ref[...]
        sq = jnp.sum(diff * diff, axis=0, keepdims=True)
        dy = diff * (1.0 / D)
        dg_part = jnp.sum(dy * xh, axis=0, keepdims=True)

        @pl.when(i == 0)
        def _():
            acc_ref[...] = sq
            dg_ref[...] = dg_part

        @pl.when(i > 0)
        def _():
            acc_ref[...] += sq
            dg_ref[...] += dg_part
        t = dy * gg
        dh_ref[...] = r * (t - xh * jnp.mean(t * xh, axis=-1, keepdims=True))

        @pl.when(i == n - 1)
        def _():
            loss_ref[...] = (0.5 / D) * jnp.sum(acc_ref[...], axis=1, keepdims=True)

    return pl.pallas_call(
        body, name="loss_head", grid=(n,),
        in_specs=[pl.BlockSpec((tm, D), lambda i: (i, 0)), pl.BlockSpec((tm, D), lambda i: (i, 0)),
                  pl.BlockSpec((1, D), lambda i: (0, 0))],
        out_specs=[pl.BlockSpec((tm, D), lambda i: (i, 0)), pl.BlockSpec((1, D), lambda i: (0, 0)),
                   pl.BlockSpec((1, 1), lambda i: (0, 0))],
        out_shape=[jax.ShapeDtypeStruct((T, D), F32), jax.ShapeDtypeStruct((1, D), F32), jax.ShapeDtypeStruct((1, 1), F32)],
        scratch_shapes=[pltpu.VMEM((1, D), F32)], compiler_params=_params(1),
    )(h, tgt, g.reshape(1, D))


def _tok_tile(T, want):
    t = min(T, want)
    assert T % t == 0
    return t


def _step(x, tgt, slab, slab_rel, place, mix_g, gn_g, mlp_g, fin_g):
    B, S, D = x.shape
    T = B * S
    h0 = x.reshape(T, D)
    tgt = tgt.reshape(T, D)
    tm = _tok_tile(T, 1024)
    tb = _tok_tile(T, 512)
    tq = _tok_tile(T, 256)
    tt = _tok_tile(T, 2048)
    consts = _ret_consts(S, RET_BLOCK)
    w = {}

    (w["ret_w_in"],) = _all_gather_slabs([slab["ret_w_in"]])
    (proj, hn0), (w["ret_w_out"], w["mlp_w1_0"], w["mlp_w2_0"]) = _mm_nn(
        h0, w["ret_w_in"], "col", tm=tb, out_dtype=BF16, name="ret_in", norm_g=mix_g[0],
        carry=_gather_carry([slab["ret_w_out"], slab["mlp_w1_0"], slab["mlp_w2_0"]]))
    y_ret, states = _ret_fwd(proj, gn_g, consts, B=B, S=S)
    h1, (w["att_w_in"], w["att_w_out"], rel_slabs) = _mm_nn(
        y_ret, w["ret_w_out"], "row", tm=tm, out_dtype=F32, name="ret_out", res=h0,
        carry=_gather_carry([slab["att_w_in"], slab["att_w_out"], slab_rel]))
    (a0, hm0), (w["mlp_w1_1"],) = _mm_nn(h1, w["mlp_w1_0"], "col", tm=tb, out_dtype=BF16, name="mlp0_up", norm_g=mlp_g[0],
                                         act="relu2", carry=_gather_carry([slab["mlp_w1_1"]]))
    h2, (w["mlp_w2_1"],) = _mm_nn(a0, w["mlp_w2_0"], "row", tm=tb, out_dtype=F32, name="mlp0_down", res=h1,
                                  carry=_gather_carry([slab["mlp_w2_1"]]))
    rel_bias = jnp.transpose(rel_slabs, (1, 0, 2)).reshape(ATT_HEADS, REL_TABLE)
    bias, bias_t = _att_bias(rel_bias)
    qkv, hn1 = _mm_nn(h2, w["att_w_in"], "col", tm=tb, out_dtype=BF16, name="att_in", norm_g=mix_g[1])
    o_att, lse = _att_fwd(qkv, bias, B=B, S=S)
    h3 = _mm_nn(o_att, w["att_w_out"], "row", tm=tm, out_dtype=F32, name="att_out", res=h2)
    a1, hm1 = _mm_nn(h3, w["mlp_w1_1"], "col", tm=tb, out_dtype=BF16, name="mlp1_up", norm_g=mlp_g[1], act="relu2")
    h4 = _mm_nn(a1, w["mlp_w2_1"], "row", tm=tb, out_dtype=F32, name="mlp1_down", res=h3)
    dh4, d_fin_g, loss = _loss_head(h4, tgt, fin_g, tm=tb)

    gw = {}
    gw["mlp_w2_1"] = _mm_tn(a1, dh4, "row", tt=tt, tk=1024, tn=D, name="d_mlp1_w2")
    dz1 = _mm_nt(dh4, w["mlp_w2_1"], "row", tm=tb, name="d_mlp1_act", epi="sqrt2", a2=a1)
    gw["mlp_w1_1"] = _mm_tn(hm1, dz1, "col", tt=tt, tk=D, tn=1024, name="d_mlp1_w1")
    dh3, d_mlp_g1 = _mm_nt(dz1, w["mlp_w1_1"], "col", tm=tb, name="d_mlp1_in", epi="normbwd",
                           h=h3, g=mlp_g[1], dres=dh4)
    gw["att_w_out"] = _mm_tn(o_att, dh3, "row", tt=tt, tk=256, tn=D, name="d_att_wout")
    do_att = _mm_nt(dh3, w["att_w_out"], "row", tm=tb, name="d_att_o", epi="bf16")
    dq, dk, dv, dbias_t = _att_bwd(qkv, do_att, o_att, lse, bias_t, B=B, S=S)
    d_rel = _att_bias_grad(dbias_t)
    dqkv = jnp.concatenate([dq, dk, dv], axis=1)
    gw["att_w_in"] = _mm_tn(hn1, dqkv, "col", tt=tt, tk=D, tn=768, name="d_att_win")
    sums, landed = {}, {}

    def swap_halves(names):
        return _sibling_carry([gw[n] for n in names])

    def add_halves(names, carried):
        for t, n in enumerate(names):
            sums[n] = _add_sibling(place, carried[t], carried[len(names) + t], name="chip_sum_" + n)

    def carried_exchange(names):
        return _chips_carry([sums[n][0] for n in names])

    def keep(names, carried):
        landed.update(zip(names, carried[len(names):]))

    layer1 = ["mlp_w1_1", "mlp_w2_1", "att_w_in", "att_w_out"]
    (dh2, d_mix_g1), carried = _mm_nt(dqkv, w["att_w_in"], "col", tm=tb, name="d_att_in", epi="normbwd",
                                      h=h2, g=mix_g[1], dres=dh3, carry=swap_halves(layer1))
    add_halves(layer1, carried)
    gw["mlp_w2_0"] = _mm_tn(a0, dh2, "row", tt=tt, tk=1024, tn=D, name="d_mlp0_w2")
    dz0, carried = _mm_nt(dh2, w["mlp_w2_0"], "row", tm=tb, name="d_mlp0_act", epi="sqrt2", a2=a0,
                          carry=carried_exchange(["mlp_w1_1", "mlp_w2_1"]))
    keep(["mlp_w1_1", "mlp_w2_1"], carried)
    gw["mlp_w1_0"] = _mm_tn(hm0, dz0, "col", tt=tt, tk=D, tn=1024, name="d_mlp0_w1")
    (dh1, d_mlp_g0), carried = _mm_nt(dz0, w["mlp_w1_0"], "col", tm=tb, name="d_mlp0_in", epi="normbwd",
                                      h=h1, g=mlp_g[0], dres=dh2,
                                      carry=_merge(carried_exchange(["att_w_in", "att_w_out"]), swap_halves(["mlp_w1_0", "mlp_w2_0"])))
    keep(["att_w_in", "att_w_out"], carried[:4])
    add_halves(["mlp_w1_0", "mlp_w2_0"], carried[4:])
    gw["ret_w_out"] = _mm_tn(y_ret, dh1, "row", tt=tt, tk=512, tn=D, name="d_ret_wout")
    dy_ret, carried = _mm_nt(dh1, w["ret_w_out"], "row", tm=tb, name="d_ret_y", epi="bf16", carry=swap_halves(["ret_w_out"]))
    add_halves(["ret_w_out"], carried)
    dproj, d_gn = _ret_bwd(proj, dy_ret, states, gn_g, consts, B=B, S=S)
    gw["ret_w_in"], carried = _mm_tn(hn0, dproj, "col", tt=tt, tk=D, tn=768, name="d_ret_win",
                                     carry=carried_exchange(["mlp_w1_0", "mlp_w2_0", "ret_w_out"]))
    keep(["mlp_w1_0", "mlp_w2_0", "ret_w_out"], carried)
    add_halves(["ret_w_in"], [gw["ret_w_in"]] + list(_exchange_with_sibling([gw["ret_w_in"]], "ret_in")))
    (dx, d_mix_g0), carried = _mm_nt(dproj, w["ret_w_in"], "col", tm=tq, name="d_ret_in", epi="normbwd",
                                     h=h0, g=mix_g[0], dres=dh1, carry=carried_exchange(["ret_w_in"]))
    keep(["ret_w_in"], carried)
    totals = {n: _add_chips(place, sums[n][1], landed[n], name="total_" + n) for n in BIG}
    small = [d_mix_g0, d_mix_g1, d_mlp_g0, d_mlp_g1, d_fin_g, d_gn.reshape(2, D), d_rel.reshape(5, D)]
    return loss.reshape(()), dx.reshape(B, S, D), totals, small


def _row_tile(r, want=256):
    t = min(r, want)
    assert r % t == 0
    return t


def _into_slab(place, a, layer, dtype, name):
    _, r, c = a.shape
    tr = _row_tile(r)

    def body(place_ref, a_ref, o_ref):
        o_ref[...] = a_ref[...].astype(dtype)

    grid_spec = pltpu.PrefetchScalarGridSpec(
        num_scalar_prefetch=1, grid=(r // tr,), in_specs=[pl.BlockSpec((None, tr, c), lambda i, pr: (layer, i, 0))],
        out_specs=pl.BlockSpec((None, tr, c), lambda i, pr: (pr[0], i, 0)),
    )
    return pl.pallas_call(
        body, name=name, grid_spec=grid_spec, out_shape=jax.ShapeDtypeStruct((N_CHIPS, r, c), dtype), compiler_params=_params(1),
    )(place, a)


def _add_sibling(place, g, recv, name):
    _, r, c = g.shape
    hr = r // 2
    tr = _row_tile(hr)
    nrt = hr // tr

    def body(place_ref, g_ref, r_ref, sb_ref, own_ref):
        v = g_ref[...] + r_ref[...]
        sb_ref[...] = v.astype(BF16)

        @pl.when(pl.program_id(1) == place_ref[0])
        def _():
            own_ref[...] = v

    grid_spec = pltpu.PrefetchScalarGridSpec(
        num_scalar_prefetch=1, grid=(nrt, N_CHIPS),
        in_specs=[pl.BlockSpec((None, tr, c), lambda i, s, pr: (s, pr[1] * nrt + i, 0)),
                  pl.BlockSpec((None, tr, c), lambda i, s, pr: (s, i, 0))],
        out_specs=[pl.BlockSpec((None, tr, c), lambda i, s, pr: (s, i, 0)), pl.BlockSpec((tr, c), lambda i, s, pr: (i, 0))],
    )
    return pl.pallas_call(
        body, name=name, grid_spec=grid_spec,
        out_shape=[jax.ShapeDtypeStruct((N_CHIPS, hr, c), BF16), jax.ShapeDtypeStruct((hr, c), F32)],
        compiler_params=_params(2),
    )(place, g, recv)


def _add_chips(place, own, recv, name):
    hr, c = own.shape
    tr = _row_tile(hr)
    nrt = hr // tr

    def body(place_ref, o_ref, r_ref, t_ref):
        t_ref[...] = ((o_ref[...] + r_ref[0].astype(F32)) + r_ref[1].astype(F32)) + r_ref[2].astype(F32)

    grid_spec = pltpu.PrefetchScalarGridSpec(
        num_scalar_prefetch=1, grid=(nrt,),
        in_specs=[pl.BlockSpec((tr, c), lambda i, pr: (i, 0)), pl.BlockSpec((3, tr, c), lambda i, pr: (0, i, 0))],
        out_specs=pl.BlockSpec((tr, c), lambda i, pr: (pr[1] * nrt + i, 0)),
    )
    return pl.pallas_call(
        body, name=name, grid_spec=grid_spec, out_shape=jax.ShapeDtypeStruct((2 * hr, c), F32), compiler_params=_params(1),
    )(place, own, recv)


def _adamw(w, gs, m, v, name):
    L, r, c = w.shape
    tr = _row_tile(r)
    assert len(gs) == L

    def body(*refs):
        w_ref, m_ref, v_ref = refs[:3]
        g_refs = refs[3:3 + L]
        go_ref, d_ref, nm_ref, nv_ref = refs[3 + L:]
        gg = g_refs[0][...]
        for k in range(1, L):
            gg = jnp.where(pl.program_id(0) == k, g_refs[k][...], gg)
        go_ref[...] = gg
        nm = ADAM_B1 * m_ref[...] + (1.0 - ADAM_B1) * gg
        nv = ADAM_B2 * v_ref[...] + (1.0 - ADAM_B2) * (gg * gg)
        m_hat = nm / (1.0 - ADAM_B1 ** ADAM_STEP)
        v_hat = nv / (1.0 - ADAM_B2 ** ADAM_STEP)
        d_ref[...] = -ADAM_LR * (m_hat / (jnp.sqrt(v_hat) + ADAM_EPS) + ADAM_WD * w_ref[...])
        nm_ref[...] = nm
        nv_ref[...] = nv

    spec = pl.BlockSpec((None, tr, c), lambda l, i: (l, i, 0))
    return pl.pallas_call(
        body, name=name, grid=(L, r // tr), in_specs=[spec] * 3 + [pl.BlockSpec((tr, c), lambda l, i: (i, 0))] * L,
        out_specs=[spec] * 4, out_shape=[jax.ShapeDtypeStruct((L, r, c), F32)] * 4, compiler_params=_params(2),
    )(w, m, v, *gs)


def _place():
    return lax.axis_index("x"), lax.axis_index("y"), lax.axis_index("c")


def _other_chips(x, y):
    return [(1 - x, y), (x, 1 - y), (1 - x, 1 - y)]


def _remote(src, dst, ssem, rsem, dev):
    return pltpu.make_async_remote_copy(src_ref=src, dst_ref=dst, send_sem=ssem, recv_sem=rsem, device_id=dev,
                                        device_id_type=MESH)


def _gather_phases(n):
    def geometry(refs, t):
        x, y, c = _place()
        hr = refs[t].shape[1] // 2
        chips = _other_chips(x, y)
        return x, y, c, 2 * x + y, chips, [2 * qx + qy for qx, qy in chips], pl.ds(c * hr, hr), pl.ds((1 - c) * hr, hr)

    def send(refs, sems):
        s1, r1, _, _ = sems
        for t in range(n):
            x, y, c, p, chips, cidx, mine, theirs = geometry(refs, t)
            for j, (qx, qy) in enumerate(chips):
                _remote(refs[t].at[p, mine], refs[t].at[p, mine], s1.at[t, j], r1.at[t, j], (qx, qy, c)).start()

    def pass_on(refs, sems):
        s1, r1, s2, r2 = sems
        for t in range(n):
            x, y, c, p, chips, cidx, mine, theirs = geometry(refs, t)
            for j, (qx, qy) in enumerate(chips):
                got = refs[t].at[cidx[j], mine]
                _remote(got, got, s1.at[t, j], r1.at[t, j], (qx, qy, c)).wait_recv()
                _remote(got, got, s2.at[t, j], r2.at[t, j], (x, y, 1 - c)).start()

    def finish(refs, sems):
        s1, r1, s2, r2 = sems
        for t in range(n):
            x, y, c, p, chips, cidx, mine, theirs = geometry(refs, t)
            for j, (qx, qy) in enumerate(chips):
                got = refs[t].at[cidx[j], theirs]
                _remote(got, got, s2.at[t, j], r2.at[t, j], (x, y, 1 - c)).wait_recv()
        for t in range(n):
            x, y, c, p, chips, cidx, mine, theirs = geometry(refs, t)
            for j, (qx, qy) in enumerate(chips):
                _remote(refs[t].at[p, mine], refs[t].at[p, mine], s1.at[t, j], r1.at[t, j], (qx, qy, c)).wait_send()
                sent = refs[t].at[cidx[j], mine]
                _remote(sent, sent, s2.at[t, j], r2.at[t, j], (x, y, 1 - c)).wait_send()

    sem = pltpu.SemaphoreType.DMA
    return send, pass_on, finish, [sem((n, 3)), sem((n, 3)), sem((n, 3)), sem((n, 3))]


def _gather_carry(slabs):
    send, pass_on, finish, sems = _gather_phases(len(slabs))
    return _Carry(slabs, sems, [(0, send), (-2, pass_on), (-1, finish)])


def _all_gather_slabs(slabs):
    n = len(slabs)
    send, pass_on, finish, sems = _gather_phases(n)

    def body(*refs):
        outs, scratch = refs[n:2 * n], refs[2 * n:]
        send(outs, scratch)
        pass_on(outs, scratch)
        finish(outs, scratch)

    return pl.pallas_call(
        body, name="gather_weights", in_specs=[HBM_SPEC] * n, out_specs=[HBM_SPEC] * n,
        out_shape=[jax.ShapeDtypeStruct(s.shape, s.dtype) for s in slabs], input_output_aliases={t: t for t in range(n)},
        scratch_shapes=sems, compiler_params=pltpu.CompilerParams(has_side_effects=True),
    )(*slabs)


def _sibling_phases(n):
    def copies(refs, sems):
        ssem, rsem = sems
        x, y, c = _place()
        out = []
        for t in range(n):
            hr = refs[t].shape[1] // 2
            out.append(_remote(refs[t].at[:, pl.ds((1 - c) * hr, hr), :], refs[n + t], ssem.at[t], rsem.at[t], (x, y, 1 - c)))
        return out

    def send(refs, sems):
        for cp in copies(refs, sems):
            cp.start()

    def finish(refs, sems):
        for cp in copies(refs, sems):
            cp.wait()

    sem = pltpu.SemaphoreType.DMA
    return send, finish, [sem((n,)), sem((n,))]


def _halves_landing(grads):
    return [lax.empty((N_CHIPS, g.shape[1] // 2, g.shape[2]), g.dtype) for g in grads]


def _sibling_carry(grads):
    send, finish, sems = _sibling_phases(len(grads))
    return _Carry(list(grads) + _halves_landing(grads), sems, [(0, send), (-1, finish)])


def _exchange_with_sibling(grads, tag):
    n = len(grads)
    send, finish, sems = _sibling_phases(n)

    def body(*refs):
        both, scratch = refs[:2 * n], refs[2 * n:]
        send(both, scratch)
        finish(both, scratch)

    return pl.pallas_call(
        body, name="grads_to_sibling_" + tag, in_specs=[HBM_SPEC] * n, out_specs=[HBM_SPEC] * n,
        out_shape=[jax.ShapeDtypeStruct((N_CHIPS, g.shape[1] // 2, g.shape[2]), g.dtype) for g in grads],
        scratch_shapes=sems, compiler_params=pltpu.CompilerParams(has_side_effects=True),
    )(*grads)


def _chips_phases(n):
    def copies(refs, sems):
        ssem, rsem = sems
        x, y, c = _place()
        return [_remote(refs[t].at[2 * qx + qy], refs[n + t].at[j], ssem.at[t, j], rsem.at[t, j], (qx, qy, c))
                for t in range(n) for j, (qx, qy) in enumerate(_other_chips(x, y))]

    def send(refs, sems):
        for cp in copies(refs, sems):
            cp.start()

    def finish(refs, sems):
        for cp in copies(refs, sems):
            cp.wait()

    sem = pltpu.SemaphoreType.DMA
    return send, finish, [sem((n, 3)), sem((n, 3))]


def _landing(sums):
    return [lax.empty((3,) + s.shape[1:], s.dtype) for s in sums]


def _chips_carry(sums):
    send, finish, sems = _chips_phases(len(sums))
    return _Carry(list(sums) + _landing(sums), sems, [(0, send), (-1, finish)])


def _share_with_sibling(shards):
    n = len(shards)

    def body(*refs):
        outs = refs[n:2 * n]
        ssem, rsem = refs[2 * n:]
        x, y, c = _place()
        copies = []
        for t in range(n):
            hr = outs[t].shape[0] // 2
            mine = outs[t].at[pl.ds(c * hr, hr)]
            cp = _remote(mine, mine, ssem.at[t], rsem.at[t], (x, y, 1 - c))
            cp.start()
            copies.append(cp)
        for t in range(n):
            hr = outs[t].shape[0] // 2
            theirs = outs[t].at[pl.ds((1 - c) * hr, hr)]
            _remote(theirs, theirs, ssem.at[t], rsem.at[t], (x, y, 1 - c)).wait_recv()
        for cp in copies:
            cp.wait_send()

    sem = pltpu.SemaphoreType.DMA
    return pl.pallas_call(
        body, name="grads_share", in_specs=[HBM_SPEC] * n, out_specs=[HBM_SPEC] * n,
        out_shape=[jax.ShapeDtypeStruct(s.shape, s.dtype) for s in shards], input_output_aliases={t: t for t in range(n)},
        scratch_shapes=[sem((n,)), sem((n,))], compiler_params=pltpu.CompilerParams(has_side_effects=True),
    )(*shards)


def _all_reduce_small(buf):
    R, C = buf.shape

    def body(in_ref, out_ref, gather, ssem, rsem):
        x, y, c = _place()
        me = 4 * x + 2 * y + c
        gather[me] = in_ref[...]
        flips = [(fx, fy, fc) for fx in (0, 1) for fy in (0, 1) for fc in (0, 1) if fx or fy or fc]
        peers = [(x + fx - 2 * x * fx, y + fy - 2 * y * fy, c + fc - 2 * c * fc) for fx, fy, fc in flips]
        copies = [_remote(in_ref, gather.at[me], ssem.at[k], rsem.at[k], peer) for k, peer in enumerate(peers)]
        for cp in copies:
            cp.start()
        for k, (px, py, pc) in enumerate(peers):
            _remote(in_ref, gather.at[4 * px + 2 * py + pc], ssem.at[k], rsem.at[k], (px, py, pc)).wait_recv()
        for cp in copies:
            cp.wait_send()
        acc = gather[0]
        for d in range(1, 8):
            acc = acc + gather[d]
        out_ref[...] = acc

    sem = pltpu.SemaphoreType.DMA
    vmem = pl.BlockSpec(memory_space=pltpu.VMEM)
    return pl.pallas_call(
        body, name="small_grads_sum", in_specs=[vmem], out_specs=vmem, out_shape=jax.ShapeDtypeStruct((R, C), F32),
        scratch_shapes=[pltpu.VMEM((8, R, C), F32), sem((7,)), sem((7,))],
        compiler_params=pltpu.CompilerParams(has_side_effects=True),
    )(buf)


BIG = ["ret_w_in", "ret_w_out", "att_w_in", "att_w_out", "mlp_w1_0", "mlp_w1_1", "mlp_w2_0", "mlp_w2_1"]
LAYER_OF = {"ret_w_in": ("ret_w_in", 0), "ret_w_out": ("ret_w_out", 0), "att_w_in": ("att_w_in", 0), "att_w_out": ("att_w_out", 0),
            "mlp_w1_0": ("mlp_w1", 0), "mlp_w1_1": ("mlp_w1", 1), "mlp_w2_0": ("mlp_w2", 0), "mlp_w2_1": ("mlp_w2", 1)}
ORDER = ["mix_norm_g", "ret_w_in", "ret_gn_g", "ret_w_out", "att_w_in", "att_rel_bias", "att_w_out", "mlp_norm_g", "mlp_w1", "mlp_w2",
         "final_norm_g"]


def kernel(x, mix_norm_g, ret_w_in, ret_gn_g, ret_w_out, att_w_in, att_rel_bias, att_w_out, mlp_norm_g, mlp_w1, mlp_w2, final_norm_g, loss_target, m_mix_norm_g, m_ret_w_in, m_ret_gn_g, m_ret_w_out, m_att_w_in, m_att_rel_bias, m_att_w_out, m_mlp_norm_g, m_mlp_w1, m_mlp_w2, m_final_norm_g, v_mix_norm_g, v_ret_w_in, v_ret_gn_g, v_ret_w_out, v_att_w_in, v_att_rel_bias, v_att_w_out, v_mlp_norm_g, v_mlp_w1, v_mlp_w2, v_final_norm_g):
    xi, yi, ci = _place()
    chip = 2 * xi + yi
    weights = dict(zip(ORDER, (mix_norm_g, ret_w_in, ret_gn_g, ret_w_out, att_w_in, att_rel_bias, att_w_out, mlp_norm_g, mlp_w1,
                               mlp_w2, final_norm_g)))
    first = dict(zip(ORDER, (m_mix_norm_g, m_ret_w_in, m_ret_gn_g, m_ret_w_out, m_att_w_in, m_att_rel_bias, m_att_w_out,
                             m_mlp_norm_g, m_mlp_w1, m_mlp_w2, m_final_norm_g)))
    second = dict(zip(ORDER, (v_mix_norm_g, v_ret_w_in, v_ret_gn_g, v_ret_w_out, v_att_w_in, v_att_rel_bias, v_att_w_out,
                              v_mlp_norm_g, v_mlp_w1, v_mlp_w2, v_final_norm_g)))

    place = jnp.stack([chip, ci]).astype(jnp.int32)
    slab = {n: _into_slab(place, weights[LAYER_OF[n][0]], LAYER_OF[n][1], BF16, name="cast_" + n) for n in BIG}
    slab_rel = _into_slab(place, att_rel_bias, 0, F32, name="slab_rel_bias")

    loss_local, grad_x, totals, small = _step(x, loss_target, slab, slab_rel, place, mix_norm_g, ret_gn_g[0], mlp_norm_g,
                                              final_norm_g)
    loss = lax.psum(loss_local, ("x", "y", "c"))

    g_big = dict(zip(BIG, _share_with_sibling([totals[n] for n in BIG])))

    rows, at = jnp.zeros((16, D_MODEL), F32), 0
    for part in small:
        rows = rows + jnp.pad(part, ((at, 16 - at - part.shape[0]), (0, 0)))
        at += part.shape[0]
    rows = _all_reduce_small(rows)
    grads = {"mix_norm_g": [rows[0:2]], "mlp_norm_g": [rows[2:4]], "final_norm_g": [rows[4:5]], "ret_gn_g": [rows[5:7].reshape(1, 2048)],
             "att_rel_bias": [lax.dynamic_slice_in_dim(rows[7:12].reshape(ATT_HEADS, REL_TABLE), chip * (REL_TABLE // N_CHIPS),
                                                       REL_TABLE // N_CHIPS, axis=1)]}
    for n in BIG:
        grads.setdefault(LAYER_OF[n][0], []).append(g_big[n])

    def as3(a):
        return a.reshape((1,) * (3 - a.ndim) + a.shape)

    results = {}
    for n in ORDER:
        outs = _adamw(as3(weights[n]), grads[n], as3(first[n]), as3(second[n]), name="adamw_" + n)
        results[n] = [o.reshape(weights[n].shape) for o in outs]
    return (loss, grad_x) + tuple(results[n][k] for k in range(4) for n in ORDER)
```

```python
import jax
import jax.numpy as jnp
from jax import lax
from jax.experimental import pallas as pl
from jax.experimental.pallas import tpu as pltpu

F32 = jnp.float32
BF16 = jnp.bfloat16
MESH = pl.DeviceIdType.MESH

D_MODEL = 1024
CHUNK = 64
RET_HEADS = 4
RET_DK = 256
RET_DV = 512
ROPE_BASE = 10000.0
ATT_HEADS = 16
ATT_DH = 64
PAST = 512
MAX_REL = 256
REL_TABLE = MAX_REL + CHUNK
EPS = 1e-6
NEG = -1e30
N_CHIPS = 4

ADAM_LR = 0.001
ADAM_B1 = 0.9
ADAM_B2 = 0.999
ADAM_EPS = 1e-08
ADAM_WD = 0.01
ADAM_STEP = 10

RET_BLOCK = 256
ATT_BLOCK = 256
VMEM_LIMIT = 56 * 1024 * 1024


def _params(n_axes, **kw):
    return pltpu.CompilerParams(dimension_semantics=("arbitrary",) * n_axes, vmem_limit_bytes=VMEM_LIMIT, **kw)


def _dot(a, b):
    return jnp.dot(a, b, preferred_element_type=F32)


def _dot_nt(a, b):
    return lax.dot_general(a, b, (((1,), (1,)), ((), ())), preferred_element_type=F32)


def _dot_tn(a, b):
    return lax.dot_general(a, b, (((0,), (0,)), ((), ())), preferred_element_type=F32)


def _sigmoid(x):
    return 1.0 / (1.0 + jnp.exp(-x))


HBM_SPEC = pl.BlockSpec(memory_space=pltpu.HBM)


class _Carry:
    def __init__(self, arrays, sems, stages):
        self.arrays, self.sems, self.stages = list(arrays), list(sems), list(stages)


def _merge(a, b):
    na, sa = len(a.arrays), len(a.sems)

    def of_a(fn):
        return lambda refs, sems: fn(refs[:na], sems[:sa])

    def of_b(fn):
        return lambda refs, sems: fn(refs[na:], sems[sa:])

    return _Carry(a.arrays + b.arrays, a.sems + b.sems,
                  [(at, of_a(fn)) for at, fn in a.stages] + [(at, of_b(fn)) for at, fn in b.stages])


def _carry_call(body, carry, *, name, grid, in_specs, out_specs, out_shape, args):
    if carry is None:
        outs = pl.pallas_call(body, name=name, grid=grid, in_specs=in_specs, out_specs=out_specs, out_shape=out_shape,
                              compiler_params=_params(len(grid)))(*args)
        return list(outs), []
    n_in, n_out, n_c = len(in_specs), len(out_specs), len(carry.arrays)
    steps = 1
    for g in grid:
        steps *= g
    assert all(-steps <= at < steps for at, _ in carry.stages)

    def carrying(*refs):
        ins, outs = refs[:n_in], refs[n_in + n_c:n_in + n_c + n_out]
        carried = refs[n_in + n_c + n_out:n_in + 2 * n_c + n_out]
        sems = refs[n_in + 2 * n_c + n_out:]
        step = pl.program_id(0)
        for axis in range(1, len(grid)):
            step = step * grid[axis] + pl.program_id(axis)
        for at, fn in carry.stages:
            if at == 0:
                pl.when(step == 0)(lambda fn=fn: fn(carried, sems))
        body(*ins, *outs)
        for at, fn in carry.stages:
            if at != 0:
                pl.when(step == at % steps)(lambda fn=fn: fn(carried, sems))

    outs = pl.pallas_call(
        carrying, name=name, grid=grid, in_specs=list(in_specs) + [HBM_SPEC] * n_c, out_specs=list(out_specs) + [HBM_SPEC] * n_c,
        out_shape=list(out_shape) + [jax.ShapeDtypeStruct(a.shape, a.dtype) for a in carry.arrays],
        input_output_aliases={n_in + t: n_out + t for t in range(n_c)}, scratch_shapes=carry.sems,
        compiler_params=_params(len(grid), has_side_effects=True),
    )(*args, *carry.arrays)
    return list(outs[:n_out]), list(outs[n_out:])


def _mm_nn(a, w, wkind, *, tm, out_dtype, name, norm_g=None, act=None, res=None, carry=None):
    M, K = a.shape
    cols = w.shape[2]
    N = N_CHIPS * cols if wkind == "col" else cols
    has_norm = norm_g is not None
    assert M % tm == 0 and K == (w.shape[1] if wkind == "col" else N_CHIPS * w.shape[1])

    def body(*refs):
        it = iter(refs)
        a_ref, w_ref = next(it), next(it)
        g_ref = next(it) if has_norm else None
        r_ref = next(it) if res is not None else None
        o_ref = next(it)
        hn_ref = next(it) if has_norm else None
        if has_norm:
            x = a_ref[...].astype(F32)
            r = lax.rsqrt(jnp.mean(x * x, axis=-1, keepdims=True) + EPS)
            lhs = (x * r * g_ref[...]).astype(BF16)
            hn_ref[...] = lhs
        else:
            lhs = a_ref[...].astype(BF16)

        def finish(acc, sl):
            if act == "relu2":
                u = jnp.maximum(acc, 0.0)
                acc = u * u
            if r_ref is not None:
                acc = acc + r_ref[:, sl]
            o_ref[:, sl] = acc.astype(out_dtype)

        if wkind == "col":
            for s in range(N_CHIPS):
                finish(_dot(lhs, w_ref[s]), slice(s * cols, (s + 1) * cols))
        else:
            finish(_dot(lhs, w_ref[...].reshape(K, N)), slice(None))

    in_specs = [pl.BlockSpec((tm, K), lambda i: (i, 0)), pl.BlockSpec(w.shape, lambda i: (0, 0, 0))]
    args = [a, w]
    if has_norm:
        in_specs.append(pl.BlockSpec((1, K), lambda i: (0, 0)))
        args.append(norm_g.reshape(1, K))
    if res is not None:
        in_specs.append(pl.BlockSpec((tm, N), lambda i: (i, 0)))
        args.append(res)
    out_shape = [jax.ShapeDtypeStruct((M, N), out_dtype)]
    out_specs = [pl.BlockSpec((tm, N), lambda i: (i, 0))]
    if has_norm:
        out_shape.append(jax.ShapeDtypeStruct((M, K), BF16))
        out_specs.append(pl.BlockSpec((tm, K), lambda i: (i, 0)))
    outs, carried = _carry_call(body, carry, name=name, grid=(M // tm,), in_specs=in_specs, out_specs=out_specs,
                                out_shape=out_shape, args=args)
    result = outs if has_norm else outs[0]
    return result if carry is None else (result, carried)


def _mm_nt(a, w, wkind, *, tm, name, epi, a2=None, h=None, g=None, dres=None, carry=None):
    M, Nw = a.shape
    rows, cols = w.shape[1], w.shape[2]
    Kw = rows if wkind == "col" else N_CHIPS * rows
    assert M % tm == 0 and Nw == (N_CHIPS * cols if wkind == "col" else cols)
    assert epi != "normbwd" or wkind == "col"

    def body(*refs):
        it = iter(refs)
        a_ref, w_ref = next(it), next(it)
        a2_ref = next(it) if epi == "sqrt2" else None
        if epi == "normbwd":
            h_ref, g_ref, dres_ref = next(it), next(it), next(it)
        o_ref = next(it)
        dg_ref = next(it) if epi == "normbwd" else None
        i = pl.program_id(0)

        def finish(acc, sl):
            if epi == "bf16":
                o_ref[:, sl] = acc.astype(BF16)
            elif epi == "sqrt2":
                o_ref[:, sl] = (acc * (2.0 * jnp.sqrt(a2_ref[:, sl].astype(F32)))).astype(BF16)
            else:
                x = h_ref[...]
                r = lax.rsqrt(jnp.mean(x * x, axis=-1, keepdims=True) + EPS)
                xh = x * r
                dg_part = jnp.sum(acc * xh, axis=0, keepdims=True)

                @pl.when(i == 0)
                def _():
                    dg_ref[...] = dg_part

                @pl.when(i > 0)
                def _():
                    dg_ref[...] += dg_part
                t = acc * g_ref[...]
                dx = r * (t - xh * jnp.mean(t * xh, axis=-1, keepdims=True))
                o_ref[...] = dres_ref[...] + dx

        if wkind == "col":
            acc = _dot_nt(a_ref[:, :cols].astype(BF16), w_ref[0])
            for s in range(1, N_CHIPS):
                acc += _dot_nt(a_ref[:, s * cols:(s + 1) * cols].astype(BF16), w_ref[s])
            finish(acc, slice(None))
        else:
            lhs = a_ref[...].astype(BF16)
            for s in range(N_CHIPS):
                finish(_dot_nt(lhs, w_ref[s]), slice(s * rows, (s + 1) * rows))

    in_specs = [pl.BlockSpec((tm, Nw), lambda i: (i, 0)), pl.BlockSpec(w.shape, lambda i: (0, 0, 0))]
    args = [a, w]
    out_dtype = BF16
    if epi == "sqrt2":
        in_specs.append(pl.BlockSpec((tm, Kw), lambda i: (i, 0)))
        args.append(a2)
    if epi == "normbwd":
        in_specs += [pl.BlockSpec((tm, Kw), lambda i: (i, 0)), pl.BlockSpec((1, Kw), lambda i: (0, 0)),
                     pl.BlockSpec((tm, Kw), lambda i: (i, 0))]
        args += [h, g.reshape(1, Kw), dres]
        out_dtype = F32
    out_shape = [jax.ShapeDtypeStruct((M, Kw), out_dtype)]
    out_specs = [pl.BlockSpec((tm, Kw), lambda i: (i, 0))]
    if epi == "normbwd":
        out_shape.append(jax.ShapeDtypeStruct((1, Kw), F32))
        out_specs.append(pl.BlockSpec((1, Kw), lambda i: (0, 0)))
    outs, carried = _carry_call(body, carry, name=name, grid=(M // tm,), in_specs=in_specs, out_specs=out_specs,
                                out_shape=out_shape, args=args)
    result = outs if epi == "normbwd" else outs[0]
    return result if carry is None else (result, carried)


def _mm_tn(a, b, okind, *, tt, tk, tn, name, carry=None):
    T, K = a.shape
    N = b.shape[1]
    assert T % tt == 0 and K % tk == 0 and N % tn == 0
    nt = T // tt
    if okind == "col":
        per = (N // N_CHIPS) // tn
        assert (N // N_CHIPS) % tn == 0
        out_shape = jax.ShapeDtypeStruct((N_CHIPS, K, N // N_CHIPS), F32)
        out_spec = pl.BlockSpec((None, tk, tn), lambda ki, nj, t: (nj // per, ki, nj % per))
    else:
        per = (K // N_CHIPS) // tk
        assert (K // N_CHIPS) % tk == 0
        out_shape = jax.ShapeDtypeStruct((N_CHIPS, K // N_CHIPS, N), F32)
        out_spec = pl.BlockSpec((None, tk, tn), lambda ki, nj, t: (ki // per, ki % per, nj))

    def body(a_ref, b_ref, o_ref):
        t = pl.program_id(2)
        part = _dot_tn(a_ref[...].astype(BF16), b_ref[...].astype(BF16))

        @pl.when(t == 0)
        def _():
            o_ref[...] = part

        @pl.when(t > 0)
        def _():
            o_ref[...] += part

    outs, carried = _carry_call(
        body, carry, name=name, grid=(K // tk, N // tn, nt),
        in_specs=[pl.BlockSpec((tt, tk), lambda ki, nj, t: (t, ki)), pl.BlockSpec((tt, tn), lambda ki, nj, t: (t, nj))],
        out_specs=[out_spec], out_shape=[out_shape], args=[a, b])
    return outs[0] if carry is None else (outs[0], carried)


def _ret_consts(S, LB):
    log_gamma = jnp.log1p(-jnp.exp2(-5.0 - jnp.arange(RET_HEADS, dtype=F32)))
    idx = jnp.arange(LB, dtype=F32)
    n, m = idx[:, None], idx[None, :]
    cn, cm = jnp.floor(n / CHUNK), jnp.floor(m / CHUNK)
    dist = jnp.where(cm == cn, jnp.abs(n - m), n - m)
    dmat = jnp.where((cm <= cn)[None], jnp.exp(log_gamma[:, None, None] * dist[None]), 0.0)
    qd = jnp.exp(log_gamma[:, None] * (idx + 1.0)[None, :])[..., None]
    kd = jnp.exp(log_gamma[:, None] * (LB - 1 - idx)[None, :])[..., None]
    bd = jnp.exp(log_gamma * LB).reshape(RET_HEADS, 1, 1) * jnp.ones((RET_HEADS, 1, 128), F32)
    half = RET_DK // 2
    inv = jnp.exp(-jnp.log(ROPE_BASE) * jnp.arange(half, dtype=F32) / half)
    ang = jnp.arange(S, dtype=F32)[:, None] * inv[None, :]
    return dmat.astype(F32), qd.astype(F32), kd.astype(F32), bd, jnp.cos(ang), jnp.sin(ang)


def _rope(t, c, s):
    t1, t2 = t[:, :128], t[:, 128:]
    return jnp.concatenate([t1 * c - t2 * s, t1 * s + t2 * c], axis=-1)


def _rope_inv(d, c, s):
    d1, d2 = d[:, :128], d[:, 128:]
    return jnp.concatenate([d1 * c + d2 * s, d2 * c - d1 * s], axis=-1)


def _ret_block_fwd(p_ref, h, c, s, d_ref, qd_ref, kd_ref, stb):
    q = _rope(p_ref[:, h * RET_DK:(h + 1) * RET_DK].astype(F32), c, s)
    k = _rope(p_ref[:, 1024 + h * RET_DK:1024 + (h + 1) * RET_DK].astype(F32), c, s) * (RET_DK ** -0.5)
    v = p_ref[:, 2048 + h * RET_DV:2048 + (h + 1) * RET_DV]
    qb, kb = q.astype(BF16), k.astype(BF16)
    scb = (_dot_nt(qb, kb) * d_ref[h]).astype(BF16)
    o = _dot(scb, v) + qd_ref[h] * _dot(qb, stb)
    return q, k, qb, kb, v, scb, o


def _ret_fwd(proj, gn_g, consts, *, B, S):
    LB = RET_BLOCK
    nb = S // LB
    T = B * S
    dmat, qd, kd, bd, cos, sin = consts

    def body(p_ref, cos_ref, sin_ref, d_ref, qd_ref, kd_ref, bd_ref, gng_ref, y_ref, st_ref, state_s):
        i = pl.program_id(1)

        @pl.when(i == 0)
        def _():
            state_s[...] = jnp.zeros_like(state_s)
        c, s = cos_ref[...], sin_ref[...]
        for h in range(RET_HEADS):
            st = state_s[h]
            stb = st.astype(BF16)
            st_ref[h] = stb
            q, k, qb, kb, v, scb, o = _ret_block_fwd(p_ref, h, c, s, d_ref, qd_ref, kd_ref, stb)
            kdk = (k * kd_ref[h]).astype(BF16)
            state_s[h] = st * bd_ref[h][:, :1] + _dot_tn(kdk, v)
            gate = p_ref[:, 4096 + h * RET_DV:4096 + (h + 1) * RET_DV].astype(F32)
            mu = jnp.mean(o, axis=-1, keepdims=True)
            oc = o - mu
            xh = oc * lax.rsqrt(jnp.mean(oc * oc, axis=-1, keepdims=True) + EPS)
            y = (gate * _sigmoid(gate)) * (xh * gng_ref[:, h * RET_DV:(h + 1) * RET_DV])
            y_ref[:, h * RET_DV:(h + 1) * RET_DV] = y.astype(BF16)

    const = lambda b, i: (0, 0, 0)
    return pl.pallas_call(
        body, name="ret_fwd", grid=(B, nb),
        in_specs=[pl.BlockSpec((LB, 6144), lambda b, i: (b * nb + i, 0)),
                  pl.BlockSpec((LB, 128), lambda b, i: (i, 0)), pl.BlockSpec((LB, 128), lambda b, i: (i, 0)),
                  pl.BlockSpec((RET_HEADS, LB, LB), const), pl.BlockSpec((RET_HEADS, LB, 1), const),
                  pl.BlockSpec((RET_HEADS, LB, 1), const), pl.BlockSpec((RET_HEADS, 1, 128), const),
                  pl.BlockSpec((1, 2048), lambda b, i: (0, 0))],
        out_specs=[pl.BlockSpec((LB, 2048), lambda b, i: (b * nb + i, 0)),
                   pl.BlockSpec((None, None, RET_HEADS, RET_DK, RET_DV), lambda b, i: (b, i, 0, 0, 0))],
        out_shape=[jax.ShapeDtypeStruct((T, 2048), BF16), jax.ShapeDtypeStruct((B, nb, RET_HEADS, RET_DK, RET_DV), BF16)],
        scratch_shapes=[pltpu.VMEM((RET_HEADS, RET_DK, RET_DV), F32)], compiler_params=_params(2),
    )(proj, cos, sin, dmat, qd, kd, bd, gn_g.reshape(1, 2048))


def _ret_bwd(proj, dy, states, gn_g, consts, *, B, S):
    LB = RET_BLOCK
    nb = S // LB
    T = B * S
    dmat, qd, kd, bd, cos, sin = consts

    def body(p_ref, dy_ref, st_ref, cos_ref, sin_ref, d_ref, qd_ref, kd_ref, bd_ref, gng_ref, dp_ref, dgn_ref, dstate_s):
        b, i = pl.program_id(0), pl.program_id(1)

        @pl.when(i == 0)
        def _():
            dstate_s[...] = jnp.zeros_like(dstate_s)

        @pl.when((b == 0) & (i == 0))
        def _():
            dgn_ref[...] = jnp.zeros_like(dgn_ref)
        c, s = cos_ref[...], sin_ref[...]
        for h in range(RET_HEADS):
            vs = slice(h * RET_DV, (h + 1) * RET_DV)
            stb = st_ref[h]
            q, k, qb, kb, v, scb, o = _ret_block_fwd(p_ref, h, c, s, d_ref, qd_ref, kd_ref, stb)
            gate = p_ref[:, 4096 + h * RET_DV:4096 + (h + 1) * RET_DV].astype(F32)
            mu = jnp.mean(o, axis=-1, keepdims=True)
            oc = o - mu
            rstd = lax.rsqrt(jnp.mean(oc * oc, axis=-1, keepdims=True) + EPS)
            xh = oc * rstd
            gng = gng_ref[:, vs]
            dyh = dy_ref[:, vs].astype(F32)
            sg = _sigmoid(gate)
            silu = gate * sg
            dgn_ref[:, vs] += jnp.sum(dyh * silu * xh, axis=0, keepdims=True)
            dxh = dyh * silu * gng
            do = rstd * (dxh - jnp.mean(dxh, axis=-1, keepdims=True) - xh * jnp.mean(dxh * xh, axis=-1, keepdims=True))
            dgate = dyh * xh * gng * (sg * (1.0 + gate * (1.0 - sg)))
            dob = do.astype(BF16)
            dsb = (_dot_nt(dob, v) * d_ref[h]).astype(BF16)
            dst = dstate_s[h]
            dstb = dst.astype(BF16)
            kdk = (k * kd_ref[h]).astype(BF16)
            dqr = _dot(dsb, kb) + qd_ref[h] * _dot_nt(dob, stb)
            dkr = _dot_tn(dsb, qb) + kd_ref[h] * _dot_nt(v, dstb)
            dv = _dot_tn(scb, dob) + _dot(kdk, dstb)
            dstate_s[h] = dst * bd_ref[h][:, :1] + _dot_tn((q * qd_ref[h]).astype(BF16), dob)
            dp_ref[:, h * RET_DK:(h + 1) * RET_DK] = _rope_inv(dqr, c, s).astype(BF16)
            dp_ref[:, 1024 + h * RET_DK:1024 + (h + 1) * RET_DK] = (_rope_inv(dkr, c, s) * (RET_DK ** -0.5)).astype(BF16)
            dp_ref[:, 2048 + h * RET_DV:2048 + (h + 1) * RET_DV] = dv.astype(BF16)
            dp_ref[:, 4096 + h * RET_DV:4096 + (h + 1) * RET_DV] = dgate.astype(BF16)

    const = lambda b, i: (0, 0, 0)
    rev = lambda b, i: (b * nb + nb - 1 - i, 0)
    return pl.pallas_call(
        body, name="ret_bwd", grid=(B, nb),
        in_specs=[pl.BlockSpec((LB, 6144), rev), pl.BlockSpec((LB, 2048), rev),
                  pl.BlockSpec((None, None, RET_HEADS, RET_DK, RET_DV), lambda b, i: (b, nb - 1 - i, 0, 0, 0)),
                  pl.BlockSpec((LB, 128), lambda b, i: (nb - 1 - i, 0)), pl.BlockSpec((LB, 128), lambda b, i: (nb - 1 - i, 0)),
                  pl.BlockSpec((RET_HEADS, LB, LB), const), pl.BlockSpec((RET_HEADS, LB, 1), const),
                  pl.BlockSpec((RET_HEADS, LB, 1), const), pl.BlockSpec((RET_HEADS, 1, 128), const),
                  pl.BlockSpec((1, 2048), lambda b, i: (0, 0))],
        out_specs=[pl.BlockSpec((LB, 6144), rev), pl.BlockSpec((1, 2048), lambda b, i: (0, 0))],
        out_shape=[jax.ShapeDtypeStruct((T, 6144), BF16), jax.ShapeDtypeStruct((1, 2048), F32)],
        scratch_shapes=[pltpu.VMEM((RET_HEADS, RET_DK, RET_DV), F32)], compiler_params=_params(2),
    )(proj, dy, states, cos, sin, dmat, qd, kd, bd, gn_g.reshape(1, 2048))


BIAS_LANES = 4 * ATT_BLOCK


def _diag_onehot():
    r = lax.broadcasted_iota(jnp.int32, (REL_TABLE, BIAS_LANES), 0)
    j = lax.broadcasted_iota(jnp.int32, (REL_TABLE, BIAS_LANES), 1)
    idx = jnp.maximum(j - ATT_BLOCK - PAST, -MAX_REL) + MAX_REL
    return jnp.where(idx == r, 1.0, 0.0).astype(F32)


def _row_is(j):
    return lax.broadcasted_iota(jnp.int32, (8, BIAS_LANES), 0) == j


def _att_bias(table):
    QB, KW = ATT_BLOCK, 3 * ATT_BLOCK

    def body(t_ref, b_ref, bt_ref):
        row = jnp.broadcast_to(t_ref[...], (8, REL_TABLE))
        diag = jnp.dot(row, _diag_onehot(), preferred_element_type=F32, precision=lax.Precision.HIGHEST)
        rows = jnp.zeros((8, BIAS_LANES), F32)
        for j in range(8):
            rows = jnp.where(_row_is(j), diag if j == 0 else pltpu.roll(diag, j, axis=1), rows)
        n = 8
        while n < QB:
            rows = jnp.concatenate([rows, pltpu.roll(rows, n, axis=1)], axis=0)
            n *= 2
        bias = rows[:, QB:]
        qi = lax.broadcasted_iota(jnp.int32, (QB, KW), 0)
        kj = lax.broadcasted_iota(jnp.int32, (QB, KW), 1)
        lo = (qi // CHUNK) * CHUNK
        bias = jnp.where((kj >= lo) & (kj < lo + PAST + CHUNK), bias, NEG)
        b_ref[...] = bias
        bt_ref[...] = bias.T

    return pl.pallas_call(
        body, name="att_bias", grid=(ATT_HEADS,),
        in_specs=[pl.BlockSpec((None, 1, REL_TABLE), lambda h: (h, 0, 0))],
        out_specs=[pl.BlockSpec((None, QB, KW), lambda h: (h, 0, 0)), pl.BlockSpec((None, KW, QB), lambda h: (h // 2, 0, h % 2))],
        out_shape=[jax.ShapeDtypeStruct((ATT_HEADS, QB, KW), F32), jax.ShapeDtypeStruct((ATT_HEADS // 2, KW, 2 * QB), F32)],
        compiler_params=_params(1),
    )(table.reshape(ATT_HEADS, 1, REL_TABLE))


def _att_bias_grad(dbias_t):
    QB, KW = ATT_BLOCK, 3 * ATT_BLOCK

    def body(d_ref, o_ref):
        rows = jnp.concatenate([jnp.zeros((QB, QB), F32), d_ref[...].T], axis=1)
        n = QB // 2
        while n >= 8:
            rows = rows[:n] + pltpu.roll(rows[n:], BIAS_LANES - n, axis=1)
            n //= 2
        acc = jnp.zeros((8, BIAS_LANES), F32)
        for j in range(8):
            acc = acc + jnp.where(_row_is(j), rows if j == 0 else pltpu.roll(rows, BIAS_LANES - j, axis=1), 0.0)
        diag = jnp.broadcast_to(jnp.sum(acc, axis=0, keepdims=True), (8, BIAS_LANES))
        grad = lax.dot_general(diag, _diag_onehot(), (((1,), (1,)), ((), ())), preferred_element_type=F32,
                               precision=lax.Precision.HIGHEST)
        o_ref[...] = grad[:1]

    return pl.pallas_call(
        body, name="att_bias_grad", grid=(ATT_HEADS,),
        in_specs=[pl.BlockSpec((None, KW, QB), lambda h: (h // 2, 0, h % 2))],
        out_specs=pl.BlockSpec((None, 1, REL_TABLE), lambda h: (h, 0, 0)),
        out_shape=jax.ShapeDtypeStruct((ATT_HEADS, 1, REL_TABLE), F32), compiler_params=_params(1),
    )(dbias_t).reshape(ATT_HEADS, REL_TABLE)


def _att_fwd(qkv, bias, *, B, S):
    QB = ATT_BLOCK
    nb = S // QB
    KW = 3 * QB
    T = B * S
    scale = ATT_DH ** -0.5

    def body(q_ref, k0, k1, k2, v0, v1, v2, b_ref, o_ref, lse_ref):
        i = pl.program_id(2)
        k3 = jnp.concatenate([k0[...], k1[...], k2[...]], axis=0)
        v3 = jnp.concatenate([v0[...], v1[...], v2[...]], axis=0)
        TQ = QB
        col = lax.broadcasted_iota(jnp.int32, (TQ, KW), 1)
        in_seq = col >= (2 - i) * QB
        lane = lax.broadcasted_iota(jnp.int32, (TQ, 128), 1)
        lane_kv = lax.broadcasted_iota(jnp.int32, (KW, 128), 1)
        vh = [jnp.where((lane_kv < ATT_DH) == (hh == 0), v3, jnp.zeros_like(v3)) for hh in range(2)]
        outs, lses = [], []
        for r in range(QB // TQ):
            rows = slice(r * TQ, (r + 1) * TQ)
            q = (q_ref[rows, :].astype(F32) * scale).astype(BF16)
            out = jnp.zeros((TQ, 128), F32)
            lse = jnp.zeros((TQ, 128), F32)
            for hh in range(2):
                qh = jnp.where((lane < ATT_DH) == (hh == 0), q, jnp.zeros_like(q))
                s = jnp.where(in_seq, _dot_nt(qh, k3) + b_ref[hh, rows, :], NEG)
                m = jnp.max(s, axis=-1, keepdims=True)
                e = jnp.exp(s - m)
                l = jnp.sum(e, axis=-1, keepdims=True)
                out = out + _dot(e.astype(BF16), vh[hh]) / l
                lse = jnp.where(lane == hh, m + jnp.log(l), lse)
            outs.append(out)
            lses.append(lse)
        o_ref[...] = jnp.concatenate(outs, axis=0).astype(BF16)
        lse_ref[...] = jnp.concatenate(lses, axis=0).T[:8]

    def kv(d, col0):
        return pl.BlockSpec((QB, 128), lambda hp, b, i: (b * nb + jnp.maximum(i - d, 0), col0 + hp))

    return pl.pallas_call(
        body, name="att_fwd", grid=(8, B, nb),
        in_specs=[pl.BlockSpec((QB, 128), lambda hp, b, i: (b * nb + i, hp)),
                  kv(2, 8), kv(1, 8), kv(0, 8), kv(2, 16), kv(1, 16), kv(0, 16),
                  pl.BlockSpec((2, QB, KW), lambda hp, b, i: (hp, 0, 0))],
        out_specs=[pl.BlockSpec((QB, 128), lambda hp, b, i: (b * nb + i, hp)),
                   pl.BlockSpec((None, 8, QB), lambda hp, b, i: (hp, 0, b * nb + i))],
        out_shape=[jax.ShapeDtypeStruct((T, 1024), BF16), jax.ShapeDtypeStruct((8, 8, T), F32)], compiler_params=_params(3),
    )(qkv, qkv, qkv, qkv, qkv, qkv, qkv, bias)


def _att_bwd(qkv, do, o, lse, bias_t, *, B, S):
    QB = ATT_BLOCK
    nb = S // QB
    KW = 3 * QB
    T = B * S
    scale = ATT_DH ** -0.5
    TK = 256

    def body(q_ref, k0, k1, k2, v0, v1, v2, do_ref, o_ref, lse_ref, b_ref, dq_ref, dk_ref, dv_ref, db_ref, dk_acc, dv_acc):
        b, i = pl.program_id(1), pl.program_id(2)

        @pl.when(i == 0)
        def _():
            dk_acc[...] = jnp.zeros_like(dk_acc)
            dv_acc[...] = jnp.zeros_like(dv_acc)

        @pl.when((b == 0) & (i == 0))
        def _():
            db_ref[...] = jnp.zeros_like(db_ref)

        @pl.when(i < nb)
        def _():
            lane = lax.broadcasted_iota(jnp.int32, (QB, 128), 1)
            first = lane < ATT_DH

            def by_head(x):
                zero = jnp.zeros_like(x)
                return jnp.concatenate([jnp.where(first, x, zero), jnp.where(first, zero, x)], axis=0)

            dout = do_ref[...]
            q2 = by_head((q_ref[...].astype(F32) * scale).astype(BF16))
            do2 = by_head(dout)
            delta_t = (o_ref[...].astype(F32) * dout.astype(F32)).T
            delta2 = jnp.concatenate([jnp.sum(delta_t[:ATT_DH], axis=0, keepdims=True),
                                      jnp.sum(delta_t[ATT_DH:], axis=0, keepdims=True)], axis=1)
            lse2 = jnp.concatenate([lse_ref[0:1, :], lse_ref[1:2, :]], axis=1)
            dq_t = jnp.zeros((128, 2 * QB), F32)
            for d, (k_ref, v_ref) in enumerate(((k0, v0), (k1, v1), (k2, v2))):
                kblk, vblk = k_ref[...], v_ref[...]
                kt = kblk.astype(F32).T.astype(BF16)
                lse_d = jnp.where(i + d >= 2, lse2, -NEG)
                slot = (i + 1 + d) % 3
                for t in range(QB // TK):
                    rows = slice(t * TK, (t + 1) * TK)
                    wrows = slice(d * QB + t * TK, d * QB + (t + 1) * TK)
                    p = jnp.exp(_dot_nt(kblk[rows], q2) + b_ref[wrows, :] - lse_d)
                    ds = p * (_dot_nt(vblk[rows], do2) - delta2)
                    db_ref[wrows, :] += ds
                    dsb = ds.astype(BF16)
                    dk_acc[slot, rows, :] += _dot(dsb, q2)
                    dv_acc[slot, rows, :] += _dot(p.astype(BF16), do2)
                    dq_t += _dot(kt[:, rows], dsb)
            row = lax.broadcasted_iota(jnp.int32, (128, QB), 0)
            dq_ref[...] = (jnp.where(row < ATT_DH, dq_t[:, :QB], dq_t[:, QB:]) * scale).T.astype(BF16)

        @pl.when(i >= 2)
        def _():
            slot = (i + 1) % 3
            dk_ref[...] = dk_acc[slot].astype(BF16)
            dv_ref[...] = dv_acc[slot].astype(BF16)
            dk_acc[slot] = jnp.zeros((QB, 128), F32)
            dv_acc[slot] = jnp.zeros((QB, 128), F32)

    def qrow(b, i):
        return b * nb + jnp.minimum(i, nb - 1)

    def kv(d, col0):
        return pl.BlockSpec((QB, 128), lambda hp, b, i: (b * nb + jnp.maximum(jnp.minimum(i, nb - 1) - d, 0), col0 + hp))

    late = pl.BlockSpec((QB, 128), lambda hp, b, i: (b * nb + jnp.maximum(i - 2, 0), hp))
    return pl.pallas_call(
        body, name="att_bwd", grid=(8, B, nb + 2),
        in_specs=[pl.BlockSpec((QB, 128), lambda hp, b, i: (qrow(b, i), hp)),
                  kv(2, 8), kv(1, 8), kv(0, 8), kv(2, 16), kv(1, 16), kv(0, 16),
                  pl.BlockSpec((QB, 128), lambda hp, b, i: (qrow(b, i), hp)),
                  pl.BlockSpec((QB, 128), lambda hp, b, i: (qrow(b, i), hp)),
                  pl.BlockSpec((None, 8, QB), lambda hp, b, i: (hp, 0, qrow(b, i))),
                  pl.BlockSpec((None, KW, 2 * QB), lambda hp, b, i: (hp, 0, 0))],
        out_specs=[pl.BlockSpec((QB, 128), lambda hp, b, i: (qrow(b, i), hp)), late, late,
                   pl.BlockSpec((None, KW, 2 * QB), lambda hp, b, i: (hp, 0, 0))],
        out_shape=[jax.ShapeDtypeStruct((T, 1024), BF16)] * 3 + [jax.ShapeDtypeStruct((ATT_HEADS // 2, KW, 2 * QB), F32)],
        scratch_shapes=[pltpu.VMEM((3, QB, 128), F32), pltpu.VMEM((3, QB, 128), F32)], compiler_params=_params(3),
    )(qkv, qkv, qkv, qkv, qkv, qkv, qkv, do, o, lse, bias_t)


def _loss_head(h, tgt, g, *, tm):
    T, D = h.shape
    n = T // tm

    def body(h_ref, t_ref, g_ref, dh_ref, dg_ref, loss_ref, acc_ref):
        i = pl.program_id(0)
        x = h_ref[...]
        r = lax.rsqrt(jnp.mean(x * x, axis=-1, keepdims=True) + EPS)
        xh = x * r
        gg = g_ref[...]
        diff = xh * gg - t_ref[...]
        sq = jnp.sum(diff * diff, axis=0, keepdims=True)
        dy = diff * (1.0 / D)
        dg_part = jnp.sum(dy * xh, axis=0, keepdims=True)

        @pl.when(i == 0)
        def _():
            acc_ref[...] = sq
            dg_ref[...] = dg_part

        @pl.when(i > 0)
        def _():
            acc_ref[...] += sq
            dg_ref[...] += dg_part
        t = dy * gg
        dh_ref[...] = r * (t - xh * jnp.mean(t * xh, axis=-1, keepdims=True))

        @pl.when(i == n - 1)
        def _():
            loss_ref[...] = (0.5 / D) * jnp.sum(acc_ref[...], axis=1, keepdims=True)

    return pl.pallas_call(
        body, name="loss_head", grid=(n,),
        in_specs=[pl.BlockSpec((tm, D), lambda i: (i, 0)), pl.BlockSpec((tm, D), lambda i: (i, 0)),
                  pl.BlockSpec((1, D), lambda i: (0, 0))],
        out_specs=[pl.BlockSpec((tm, D), lambda i: (i, 0)), pl.BlockSpec((1, D), lambda i: (0, 0)),
                   pl.BlockSpec((1, 1), lambda i: (0, 0))],
        out_shape=[jax.ShapeDtypeStruct((T, D), F32), jax.ShapeDtypeStruct((1, D), F32), jax.ShapeDtypeStruct((1, 1), F32)],
        scratch_shapes=[pltpu.VMEM((1, D), F32)], compiler_params=_params(1),
    )(h, tgt, g.reshape(1, D))


def _tok_tile(T, want):
    t = min(T, want)
    assert T % t == 0
    return t


def _step(x, tgt, slab, slab_rel, place, mix_g, gn_g, mlp_g, fin_g):
    B, S, D = x.shape
    T = B * S
    h0 = x.reshape(T, D)
    tgt = tgt.reshape(T, D)
    tm = _tok_tile(T, 1024)
    tb = _tok_tile(T, 512)
    tq = _tok_tile(T, 256)
    tt = _tok_tile(T, 8192)
    consts = _ret_consts(S, RET_BLOCK)
    w = {}

    (w["ret_w_in"],) = _all_gather_slabs([slab["ret_w_in"]])
    (proj, hn0), (w["ret_w_out"], w["mlp_w1_0"], w["mlp_w2_0"]) = _mm_nn(
        h0, w["ret_w_in"], "col", tm=tb, out_dtype=BF16, name="ret_in", norm_g=mix_g[0],
        carry=_gather_carry([slab["ret_w_out"], slab["mlp_w1_0"], slab["mlp_w2_0"]]))
    y_ret, states = _ret_fwd(proj, gn_g, consts, B=B, S=S)
    h1, (w["att_w_in"], w["att_w_out"], rel_slabs) = _mm_nn(
        y_ret, w["ret_w_out"], "row", tm=tm, out_dtype=F32, name="ret_out", res=h0,
        carry=_gather_carry([slab["att_w_in"], slab["att_w_out"], slab_rel]))
    (a0, hm0), (w["mlp_w1_1"],) = _mm_nn(h1, w["mlp_w1_0"], "col", tm=tb, out_dtype=BF16, name="mlp0_up", norm_g=mlp_g[0],
                                         act="relu2", carry=_gather_carry([slab["mlp_w1_1"]]))
    h2, (w["mlp_w2_1"],) = _mm_nn(a0, w["mlp_w2_0"], "row", tm=tb, out_dtype=F32, name="mlp0_down", res=h1,
                                  carry=_gather_carry([slab["mlp_w2_1"]]))
    rel_bias = jnp.transpose(rel_slabs, (1, 0, 2)).reshape(ATT_HEADS, REL_TABLE)
    bias, bias_t = _att_bias(rel_bias)
    qkv, hn1 = _mm_nn(h2, w["att_w_in"], "col", tm=tb, out_dtype=BF16, name="att_in", norm_g=mix_g[1])
    o_att, lse = _att_fwd(qkv, bias, B=B, S=S)
    h3 = _mm_nn(o_att, w["att_w_out"], "row", tm=tm, out_dtype=F32, name="att_out", res=h2)
    a1, hm1 = _mm_nn(h3, w["mlp_w1_1"], "col", tm=tb, out_dtype=BF16, name="mlp1_up", norm_g=mlp_g[1], act="relu2")
    h4 = _mm_nn(a1, w["mlp_w2_1"], "row", tm=tb, out_dtype=F32, name="mlp1_down", res=h3)
    dh4, d_fin_g, loss = _loss_head(h4, tgt, fin_g, tm=tb)

    gw = {}
    gw["mlp_w2_1"] = _mm_tn(a1, dh4, "row", tt=tt, tk=1024, tn=256, name="d_mlp1_w2")
    dz1 = _mm_nt(dh4, w["mlp_w2_1"], "row", tm=tb, name="d_mlp1_act", epi="sqrt2", a2=a1)
    gw["mlp_w1_1"] = _mm_tn(hm1, dz1, "col", tt=tt, tk=D, tn=256, name="d_mlp1_w1")
    dh3, d_mlp_g1 = _mm_nt(dz1, w["mlp_w1_1"], "col", tm=tb, name="d_mlp1_in", epi="normbwd",
                           h=h3, g=mlp_g[1], dres=dh4)
    gw["att_w_out"] = _mm_tn(o_att, dh3, "row", tt=tt, tk=256, tn=256, name="d_att_wout")
    do_att = _mm_nt(dh3, w["att_w_out"], "row", tm=tb, name="d_att_o", epi="bf16")
    dq, dk, dv, dbias_t = _att_bwd(qkv, do_att, o_att, lse, bias_t, B=B, S=S)
    d_rel = _att_bias_grad(dbias_t)
    dqkv = jnp.concatenate([dq, dk, dv], axis=1)
    gw["att_w_in"] = _mm_tn(hn1, dqkv, "col", tt=tt, tk=D, tn=256, name="d_att_win")
    sums, landed = {}, {}

    def swap_halves(names):
        return _sibling_carry([gw[n] for n in names])

    def add_halves(names, carried):
        for t, n in enumerate(names):
            sums[n] = _add_sibling(place, carried[t], carried[len(names) + t], name="chip_sum_" + n)

    def carried_exchange(names):
        return _chips_carry([sums[n][0] for n in names])

    def keep(names, carried):
        landed.update(zip(names, carried[len(names):]))

    layer1 = ["mlp_w1_1", "mlp_w2_1", "att_w_in", "att_w_out"]
    (dh2, d_mix_g1), carried = _mm_nt(dqkv, w["att_w_in"], "col", tm=tb, name="d_att_in", epi="normbwd",
                                      h=h2, g=mix_g[1], dres=dh3, carry=swap_halves(layer1))
    add_halves(layer1, carried)
    gw["mlp_w2_0"] = _mm_tn(a0, dh2, "row", tt=tt, tk=1024, tn=256, name="d_mlp0_w2")
    dz0, carried = _mm_nt(dh2, w["mlp_w2_0"], "row", tm=tb, name="d_mlp0_act", epi="sqrt2", a2=a0,
                          carry=carried_exchange(["mlp_w1_1", "mlp_w2_1"]))
    keep(["mlp_w1_1", "mlp_w2_1"], carried)
    gw["mlp_w1_0"] = _mm_tn(hm0, dz0, "col", tt=tt, tk=D, tn=256, name="d_mlp0_w1")
    (dh1, d_mlp_g0), carried = _mm_nt(dz0, w["mlp_w1_0"], "col", tm=tb, name="d_mlp0_in", epi="normbwd",
                                      h=h1, g=mlp_g[0], dres=dh2,
                                      carry=_merge(carried_exchange(["att_w_in", "att_w_out"]), swap_halves(["mlp_w1_0", "mlp_w2_0"])))
    keep(["att_w_in", "att_w_out"], carried[:4])
    add_halves(["mlp_w1_0", "mlp_w2_0"], carried[4:])
    gw["ret_w_out"] = _mm_tn(y_ret, dh1, "row", tt=tt, tk=512, tn=256, name="d_ret_wout")
    dy_ret, carried = _mm_nt(dh1, w["ret_w_out"], "row", tm=tb, name="d_ret_y", epi="bf16", carry=swap_halves(["ret_w_out"]))
    add_halves(["ret_w_out"], carried)
    dproj, d_gn = _ret_bwd(proj, dy_ret, states, gn_g, consts, B=B, S=S)
    gw["ret_w_in"], carried = _mm_tn(hn0, dproj, "col", tt=tt, tk=D, tn=256, name="d_ret_win",
                                     carry=carried_exchange(["mlp_w1_0", "mlp_w2_0", "ret_w_out"]))
    keep(["mlp_w1_0", "mlp_w2_0", "ret_w_out"], carried)
    add_halves(["ret_w_in"], [gw["ret_w_in"]] + list(_exchange_with_sibling([gw["ret_w_in"]], "ret_in")))
    (dx, d_mix_g0), carried = _mm_nt(dproj, w["ret_w_in"], "col", tm=tq, name="d_ret_in", epi="normbwd",
                                     h=h0, g=mix_g[0], dres=dh1, carry=carried_exchange(["ret_w_in"]))
    keep(["ret_w_in"], carried)
    totals = {n: _add_chips(place, sums[n][1], landed[n], name="total_" + n) for n in BIG}
    small = [d_mix_g0, d_mix_g1, d_mlp_g0, d_mlp_g1, d_fin_g, d_gn.reshape(2, D), d_rel.reshape(5, D)]
    return loss.reshape(()), dx.reshape(B, S, D), totals, small


def _row_tile(r, want=256):
    t = min(r, want)
    assert r % t == 0
    return t


def _into_slab(place, a, layer, dtype, name):
    _, r, c = a.shape
    tr = _row_tile(r)

    def body(place_ref, a_ref, o_ref):
        o_ref[...] = a_ref[...].astype(dtype)

    grid_spec = pltpu.PrefetchScalarGridSpec(
        num_scalar_prefetch=1, grid=(r // tr,), in_specs=[pl.BlockSpec((None, tr, c), lambda i, pr: (layer, i, 0))],
        out_specs=pl.BlockSpec((None, tr, c), lambda i, pr: (pr[0], i, 0)),
    )
    return pl.pallas_call(
        body, name=name, grid_spec=grid_spec, out_shape=jax.ShapeDtypeStruct((N_CHIPS, r, c), dtype), compiler_params=_params(1),
    )(place, a)


def _add_sibling(place, g, recv, name):
    _, r, c = g.shape
    hr = r // 2
    tr = _row_tile(hr)
    nrt = hr // tr

    def body(place_ref, g_ref, r_ref, sb_ref, own_ref):
        v = g_ref[...] + r_ref[...]
        sb_ref[...] = v.astype(BF16)

        @pl.when(pl.program_id(1) == place_ref[0])
        def _():
            own_ref[...] = v

    grid_spec = pltpu.PrefetchScalarGridSpec(
        num_scalar_prefetch=1, grid=(nrt, N_CHIPS),
        in_specs=[pl.BlockSpec((None, tr, c), lambda i, s, pr: (s, pr[1] * nrt + i, 0)),
                  pl.BlockSpec((None, tr, c), lambda i, s, pr: (s, i, 0))],
        out_specs=[pl.BlockSpec((None, tr, c), lambda i, s, pr: (s, i, 0)), pl.BlockSpec((tr, c), lambda i, s, pr: (i, 0))],
    )
    return pl.pallas_call(
        body, name=name, grid_spec=grid_spec,
        out_shape=[jax.ShapeDtypeStruct((N_CHIPS, hr, c), BF16), jax.ShapeDtypeStruct((hr, c), F32)],
        compiler_params=_params(2),
    )(place, g, recv)


def _add_chips(place, own, recv, name):
    hr, c = own.shape
    tr = _row_tile(hr)
    nrt = hr // tr

    def body(place_ref, o_ref, r_ref, t_ref):
        t_ref[...] = ((o_ref[...] + r_ref[0].astype(F32)) + r_ref[1].astype(F32)) + r_ref[2].astype(F32)

    grid_spec = pltpu.PrefetchScalarGridSpec(
        num_scalar_prefetch=1, grid=(nrt,),
        in_specs=[pl.BlockSpec((tr, c), lambda i, pr: (i, 0)), pl.BlockSpec((3, tr, c), lambda i, pr: (0, i, 0))],
        out_specs=pl.BlockSpec((tr, c), lambda i, pr: (pr[1] * nrt + i, 0)),
    )
    return pl.pallas_call(
        body, name=name, grid_spec=grid_spec, out_shape=jax.ShapeDtypeStruct((2 * hr, c), F32), compiler_params=_params(1),
    )(place, own, recv)


def _adamw(w, gs, m, v, name):
    L, r, c = w.shape
    tr = _row_tile(r)
    assert len(gs) == L

    def body(*refs):
        w_ref, m_ref, v_ref = refs[:3]
        g_refs = refs[3:3 + L]
        go_ref, d_ref, nm_ref, nv_ref = refs[3 + L:]
        gg = g_refs[0][...]
        for k in range(1, L):
            gg = jnp.where(pl.program_id(0) == k, g_refs[k][...], gg)
        go_ref[...] = gg
        nm = ADAM_B1 * m_ref[...] + (1.0 - ADAM_B1) * gg
        nv = ADAM_B2 * v_ref[...] + (1.0 - ADAM_B2) * (gg * gg)
        m_hat = nm / (1.0 - ADAM_B1 ** ADAM_STEP)
        v_hat = nv / (1.0 - ADAM_B2 ** ADAM_STEP)
        d_ref[...] = -ADAM_LR * (m_hat / (jnp.sqrt(v_hat) + ADAM_EPS) + ADAM_WD * w_ref[...])
        nm_ref[...] = nm
        nv_ref[...] = nv

    spec = pl.BlockSpec((None, tr, c), lambda l, i: (l, i, 0))
    return pl.pallas_call(
        body, name=name, grid=(L, r // tr), in_specs=[spec] * 3 + [pl.BlockSpec((tr, c), lambda l, i: (i, 0))] * L,
        out_specs=[spec] * 4, out_shape=[jax.ShapeDtypeStruct((L, r, c), F32)] * 4, compiler_params=_params(2),
    )(w, m, v, *gs)


def _place():
    return lax.axis_index("x"), lax.axis_index("y"), lax.axis_index("c")


def _other_chips(x, y):
    return [(1 - x, y), (x, 1 - y), (1 - x, 1 - y)]


def _remote(src, dst, ssem, rsem, dev):
    return pltpu.make_async_remote_copy(src_ref=src, dst_ref=dst, send_sem=ssem, recv_sem=rsem, device_id=dev,
                                        device_id_type=MESH)


def _gather_phases(n):
    def geometry(refs, t):
        x, y, c = _place()
        hr = refs[t].shape[1] // 2
        chips = _other_chips(x, y)
        return x, y, c, 2 * x + y, chips, [2 * qx + qy for qx, qy in chips], pl.ds(c * hr, hr), pl.ds((1 - c) * hr, hr)

    def send(refs, sems):
        s1, r1, _, _ = sems
        for t in range(n):
            x, y, c, p, chips, cidx, mine, theirs = geometry(refs, t)
            for j, (qx, qy) in enumerate(chips):
                _remote(refs[t].at[p, mine], refs[t].at[p, mine], s1.at[t, j], r1.at[t, j], (qx, qy, c)).start()

    def pass_on(refs, sems):
        s1, r1, s2, r2 = sems
        for t in range(n):
            x, y, c, p, chips, cidx, mine, theirs = geometry(refs, t)
            for j, (qx, qy) in enumerate(chips):
                got = refs[t].at[cidx[j], mine]
                _remote(got, got, s1.at[t, j], r1.at[t, j], (qx, qy, c)).wait_recv()
                _remote(got, got, s2.at[t, j], r2.at[t, j], (x, y, 1 - c)).start()

    def finish(refs, sems):
        s1, r1, s2, r2 = sems
        for t in range(n):
            x, y, c, p, chips, cidx, mine, theirs = geometry(refs, t)
            for j, (qx, qy) in enumerate(chips):
                got = refs[t].at[cidx[j], theirs]
                _remote(got, got, s2.at[t, j], r2.at[t, j], (x, y, 1 - c)).wait_recv()
        for t in range(n):
            x, y, c, p, chips, cidx, mine, theirs = geometry(refs, t)
            for j, (qx, qy) in enumerate(chips):
                _remote(refs[t].at[p, mine], refs[t].at[p, mine], s1.at[t, j], r1.at[t, j], (qx, qy, c)).wait_send()
                sent = refs[t].at[cidx[j], mine]
                _remote(sent, sent, s2.at[t, j], r2.at[t, j], (x, y, 1 - c)).wait_send()

    sem = pltpu.SemaphoreType.DMA
    return send, pass_on, finish, [sem((n, 3)), sem((n, 3)), sem((n, 3)), sem((n, 3))]


def _gather_carry(slabs):
    send, pass_on, finish, sems = _gather_phases(len(slabs))
    return _Carry(slabs, sems, [(0, send), (-2, pass_on), (-1, finish)])


def _all_gather_slabs(slabs):
    n = len(slabs)
    send, pass_on, finish, sems = _gather_phases(n)

    def body(*refs):
        outs, scratch = refs[n:2 * n], refs[2 * n:]
        send(outs, scratch)
        pass_on(outs, scratch)
        finish(outs, scratch)

    return pl.pallas_call(
        body, name="gather_weights", in_specs=[HBM_SPEC] * n, out_specs=[HBM_SPEC] * n,
        out_shape=[jax.ShapeDtypeStruct(s.shape, s.dtype) for s in slabs], input_output_aliases={t: t for t in range(n)},
        scratch_shapes=sems, compiler_params=pltpu.CompilerParams(has_side_effects=True),
    )(*slabs)


def _sibling_phases(n):
    def copies(refs, sems):
        ssem, rsem = sems
        x, y, c = _place()
        out = []
        for t in range(n):
            hr = refs[t].shape[1] // 2
            out.append(_remote(refs[t].at[:, pl.ds((1 - c) * hr, hr), :], refs[n + t], ssem.at[t], rsem.at[t], (x, y, 1 - c)))
        return out

    def send(refs, sems):
        for cp in copies(refs, sems):
            cp.start()

    def finish(refs, sems):
        for cp in copies(refs, sems):
            cp.wait()

    sem = pltpu.SemaphoreType.DMA
    return send, finish, [sem((n,)), sem((n,))]


def _halves_landing(grads):
    return [lax.empty((N_CHIPS, g.shape[1] // 2, g.shape[2]), g.dtype) for g in grads]


def _sibling_carry(grads):
    send, finish, sems = _sibling_phases(len(grads))
    return _Carry(list(grads) + _halves_landing(grads), sems, [(0, send), (-1, finish)])


def _exchange_with_sibling(grads, tag):
    n = len(grads)
    send, finish, sems = _sibling_phases(n)

    def body(*refs):
        both, scratch = refs[:2 * n], refs[2 * n:]
        send(both, scratch)
        finish(both, scratch)

    return pl.pallas_call(
        body, name="grads_to_sibling_" + tag, in_specs=[HBM_SPEC] * n, out_specs=[HBM_SPEC] * n,
        out_shape=[jax.ShapeDtypeStruct((N_CHIPS, g.shape[1] // 2, g.shape[2]), g.dtype) for g in grads],
        scratch_shapes=sems, compiler_params=pltpu.CompilerParams(has_side_effects=True),
    )(*grads)


def _chips_phases(n):
    def copies(refs, sems):
        ssem, rsem = sems
        x, y, c = _place()
        return [_remote(refs[t].at[2 * qx + qy], refs[n + t].at[j], ssem.at[t, j], rsem.at[t, j], (qx, qy, c))
                for t in range(n) for j, (qx, qy) in enumerate(_other_chips(x, y))]

    def send(refs, sems):
        for cp in copies(refs, sems):
            cp.start()

    def finish(refs, sems):
        for cp in copies(refs, sems):
            cp.wait()

    sem = pltpu.SemaphoreType.DMA
    return send, finish, [sem((n, 3)), sem((n, 3))]


def _landing(sums):
    return [lax.empty((3,) + s.shape[1:], s.dtype) for s in sums]


def _chips_carry(sums):
    send, finish, sems = _chips_phases(len(sums))
    return _Carry(list(sums) + _landing(sums), sems, [(0, send), (-1, finish)])


def _share_with_sibling(shards):
    n = len(shards)

    def body(*refs):
        outs = refs[n:2 * n]
        ssem, rsem = refs[2 * n:]
        x, y, c = _place()
        copies = []
        for t in range(n):
            hr = outs[t].shape[0] // 2
            mine = outs[t].at[pl.ds(c * hr, hr)]
            cp = _remote(mine, mine, ssem.at[t], rsem.at[t], (x, y, 1 - c))
            cp.start()
            copies.append(cp)
        for t in range(n):
            hr = outs[t].shape[0] // 2
            theirs = outs[t].at[pl.ds((1 - c) * hr, hr)]
            _remote(theirs, theirs, ssem.at[t], rsem.at[t], (x, y, 1 - c)).wait_recv()
        for cp in copies:
            cp.wait_send()

    sem = pltpu.SemaphoreType.DMA
    return pl.pallas_call(
        body, name="grads_share", in_specs=[HBM_SPEC] * n, out_specs=[HBM_SPEC] * n,
        out_shape=[jax.ShapeDtypeStruct(s.shape, s.dtype) for s in shards], input_output_aliases={t: t for t in range(n)},
        scratch_shapes=[sem((n,)), sem((n,))], compiler_params=pltpu.CompilerParams(has_side_effects=True),
    )(*shards)


def _all_reduce_small(buf):
    R, C = buf.shape

    def body(in_ref, out_ref, gather, ssem, rsem):
        x, y, c = _place()
        me = 4 * x + 2 * y + c
        gather[me] = in_ref[...]
        flips = [(fx, fy, fc) for fx in (0, 1) for fy in (0, 1) for fc in (0, 1) if fx or fy or fc]
        peers = [(x + fx - 2 * x * fx, y + fy - 2 * y * fy, c + fc - 2 * c * fc) for fx, fy, fc in flips]
        copies = [_remote(in_ref, gather.at[me], ssem.at[k], rsem.at[k], peer) for k, peer in enumerate(peers)]
        for cp in copies:
            cp.start()
        for k, (px, py, pc) in enumerate(peers):
            _remote(in_ref, gather.at[4 * px + 2 * py + pc], ssem.at[k], rsem.at[k], (px, py, pc)).wait_recv()
        for cp in copies:
            cp.wait_send()
        acc = gather[0]
        for d in range(1, 8):
            acc = acc + gather[d]
        out_ref[...] = acc

    sem = pltpu.SemaphoreType.DMA
    vmem = pl.BlockSpec(memory_space=pltpu.VMEM)
    return pl.pallas_call(
        body, name="small_grads_sum", in_specs=[vmem], out_specs=vmem, out_shape=jax.ShapeDtypeStruct((R, C), F32),
        scratch_shapes=[pltpu.VMEM((8, R, C), F32), sem((7,)), sem((7,))],
        compiler_params=pltpu.CompilerParams(has_side_effects=True),
    )(buf)


BIG = ["ret_w_in", "ret_w_out", "att_w_in", "att_w_out", "mlp_w1_0", "mlp_w1_1", "mlp_w2_0", "mlp_w2_1"]
LAYER_OF = {"ret_w_in": ("ret_w_in", 0), "ret_w_out": ("ret_w_out", 0), "att_w_in": ("att_w_in", 0), "att_w_out": ("att_w_out", 0),
            "mlp_w1_0": ("mlp_w1", 0), "mlp_w1_1": ("mlp_w1", 1), "mlp_w2_0": ("mlp_w2", 0), "mlp_w2_1": ("mlp_w2", 1)}
ORDER = ["mix_norm_g", "ret_w_in", "ret_gn_g", "ret_w_out", "att_w_in", "att_rel_bias", "att_w_out", "mlp_norm_g", "mlp_w1", "mlp_w2",
         "final_norm_g"]


def kernel(x, mix_norm_g, ret_w_in, ret_gn_g, ret_w_out, att_w_in, att_rel_bias, att_w_out, mlp_norm_g, mlp_w1, mlp_w2, final_norm_g, loss_target, m_mix_norm_g, m_ret_w_in, m_ret_gn_g, m_ret_w_out, m_att_w_in, m_att_rel_bias, m_att_w_out, m_mlp_norm_g, m_mlp_w1, m_mlp_w2, m_final_norm_g, v_mix_norm_g, v_ret_w_in, v_ret_gn_g, v_ret_w_out, v_att_w_in, v_att_rel_bias, v_att_w_out, v_mlp_norm_g, v_mlp_w1, v_mlp_w2, v_final_norm_g):
    xi, yi, ci = _place()
    chip = 2 * xi + yi
    weights = dict(zip(ORDER, (mix_norm_g, ret_w_in, ret_gn_g, ret_w_out, att_w_in, att_rel_bias, att_w_out, mlp_norm_g, mlp_w1,
                               mlp_w2, final_norm_g)))
    first = dict(zip(ORDER, (m_mix_norm_g, m_ret_w_in, m_ret_gn_g, m_ret_w_out, m_att_w_in, m_att_rel_bias, m_att_w_out,
                             m_mlp_norm_g, m_mlp_w1, m_mlp_w2, m_final_norm_g)))
    second = dict(zip(ORDER, (v_mix_norm_g, v_ret_w_in, v_ret_gn_g, v_ret_w_out, v_att_w_in, v_att_rel_bias, v_att_w_out,
                              v_mlp_norm_g, v_mlp_w1, v_mlp_w2, v_final_norm_g)))

    place = jnp.stack([chip, ci]).astype(jnp.int32)
    slab = {n: _into_slab(place, weights[LAYER_OF[n][0]], LAYER_OF[n][1], BF16, name="cast_" + n) for n in BIG}
    slab_rel = _into_slab(place, att_rel_bias, 0, F32, name="slab_rel_bias")

    loss_local, grad_x, totals, small = _step(x, loss_target, slab, slab_rel, place, mix_norm_g, ret_gn_g[0], mlp_norm_g,
                                              final_norm_g)
    loss = lax.psum(loss_local, ("x", "y", "c"))

    g_big = dict(zip(BIG, _share_with_sibling([totals[n] for n in BIG])))

    rows, at = jnp.zeros((16, D_MODEL), F32), 0
    for part in small:
        rows = rows + jnp.pad(part, ((at, 16 - at - part.shape[0]), (0, 0)))
        at += part.shape[0]
    rows = _all_reduce_small(rows)
    grads = {"mix_norm_g": [rows[0:2]], "mlp_norm_g": [rows[2:4]], "final_norm_g": [rows[4:5]], "ret_gn_g": [rows[5:7].reshape(1, 2048)],
             "att_rel_bias": [lax.dynamic_slice_in_dim(rows[7:12].reshape(ATT_HEADS, REL_TABLE), chip * (REL_TABLE // N_CHIPS),
                                                       REL_TABLE // N_CHIPS, axis=1)]}
    for n in BIG:
        grads.setdefault(LAYER_OF[n][0], []).append(g_big[n])

    def as3(a):
        return a.reshape((1,) * (3 - a.ndim) + a.shape)

    results = {}
    for n in ORDER:
        outs = _adamw(as3(weights[n]), grads[n], as3(first[n]), as3(second[n]), name="adamw_" + n)
        results[n] = [o.reshape(weights[n].shape) for o in outs]
    return (loss, grad_x) + tuple(results[n][k] for k in range(4) for n in ORDER)
```

```python
import math

import jax
import jax.numpy as jnp
from jax import lax
from jax.experimental import pallas as pl
from jax.experimental.pallas import tpu as pltpu

F32 = jnp.float32
BF16 = jnp.bfloat16
MESH = pl.DeviceIdType.MESH

D_MODEL = 1024
CHUNK = 64
RET_HEADS = 4
RET_DK = 256
RET_DV = 512
ROPE_BASE = 10000.0
ATT_HEADS = 16
ATT_DH = 64
PAST = 512
MAX_REL = 256
REL_TABLE = MAX_REL + CHUNK
EPS = 1e-6
NEG = -1e30
N_CHIPS = 4

ADAM_LR = 0.001
ADAM_B1 = 0.9
ADAM_B2 = 0.999
ADAM_EPS = 1e-08
ADAM_WD = 0.01
ADAM_STEP = 10

RET_BLOCK = 256
ATT_BLOCK = 256
VMEM_LIMIT = 56 * 1024 * 1024


def _params(n_axes, **kw):
    return pltpu.CompilerParams(dimension_semantics=("arbitrary",) * n_axes, vmem_limit_bytes=VMEM_LIMIT, **kw)


def _dot(a, b):
    return jnp.dot(a, b, preferred_element_type=F32)


def _dot_nt(a, b):
    return lax.dot_general(a, b, (((1,), (1,)), ((), ())), preferred_element_type=F32)


def _dot_tn(a, b):
    return lax.dot_general(a, b, (((0,), (0,)), ((), ())), preferred_element_type=F32)


def _sigmoid(x):
    return 1.0 / (1.0 + jnp.exp(-x))


HBM_SPEC = pl.BlockSpec(memory_space=pltpu.HBM)


class _Carry:
    def __init__(self, arrays, sems, stages):
        self.arrays, self.sems, self.stages = list(arrays), list(sems), list(stages)


def _merge(a, b):
    na, sa = len(a.arrays), len(a.sems)

    def of_a(fn):
        return lambda refs, sems: fn(refs[:na], sems[:sa])

    def of_b(fn):
        return lambda refs, sems: fn(refs[na:], sems[sa:])

    return _Carry(a.arrays + b.arrays, a.sems + b.sems,
                  [(at, of_a(fn)) for at, fn in a.stages] + [(at, of_b(fn)) for at, fn in b.stages])


def _carry_call(body, carry, *, name, grid, in_specs, out_specs, out_shape, args):
    if carry is None:
        outs = pl.pallas_call(body, name=name, grid=grid, in_specs=in_specs, out_specs=out_specs, out_shape=out_shape,
                              compiler_params=_params(len(grid)))(*args)
        return list(outs), []
    n_in, n_out, n_c = len(in_specs), len(out_specs), len(carry.arrays)
    steps = 1
    for g in grid:
        steps *= g
    assert all(-steps <= at < steps for at, _ in carry.stages)

    def carrying(*refs):
        ins, outs = refs[:n_in], refs[n_in + n_c:n_in + n_c + n_out]
        carried = refs[n_in + n_c + n_out:n_in + 2 * n_c + n_out]
        sems = refs[n_in + 2 * n_c + n_out:]
        step = pl.program_id(0)
        for axis in range(1, len(grid)):
            step = step * grid[axis] + pl.program_id(axis)
        for at, fn in carry.stages:
            if at == 0:
                pl.when(step == 0)(lambda fn=fn: fn(carried, sems))
        body(*ins, *outs)
        for at, fn in carry.stages:
            if at != 0:
                pl.when(step == at % steps)(lambda fn=fn: fn(carried, sems))

    outs = pl.pallas_call(
        carrying, name=name, grid=grid, in_specs=list(in_specs) + [HBM_SPEC] * n_c, out_specs=list(out_specs) + [HBM_SPEC] * n_c,
        out_shape=list(out_shape) + [jax.ShapeDtypeStruct(a.shape, a.dtype) for a in carry.arrays],
        input_output_aliases={n_in + t: n_out + t for t in range(n_c)}, scratch_shapes=carry.sems,
        compiler_params=_params(len(grid), has_side_effects=True),
    )(*args, *carry.arrays)
    return list(outs[:n_out]), list(outs[n_out:])


def _mm_nn(a, w, wkind, *, tm, out_dtype, name, norm_g=None, act=None, res=None, carry=None):
    M, K = a.shape
    cols = w.shape[2]
    N = N_CHIPS * cols if wkind == "col" else cols
    has_norm = norm_g is not None
    assert M % tm == 0 and K == (w.shape[1] if wkind == "col" else N_CHIPS * w.shape[1])

    def body(*refs):
        it = iter(refs)
        a_ref, w_ref = next(it), next(it)
        g_ref = next(it) if has_norm else None
        r_ref = next(it) if res is not None else None
        o_ref = next(it)
        hn_ref = next(it) if has_norm else None
        if has_norm:
            x = a_ref[...].astype(F32)
            r = lax.rsqrt(jnp.mean(x * x, axis=-1, keepdims=True) + EPS)
            lhs = (x * r * g_ref[...]).astype(BF16)
            hn_ref[...] = lhs
        else:
            lhs = a_ref[...].astype(BF16)

        def finish(acc, sl):
            if act == "relu2":
                u = jnp.maximum(acc, 0.0)
                acc = u * u
            if r_ref is not None:
                acc = acc + r_ref[:, sl]
            o_ref[:, sl] = acc.astype(out_dtype)

        if wkind == "col":
            for s in range(N_CHIPS):
                finish(_dot(lhs, w_ref[s]), slice(s * cols, (s + 1) * cols))
        else:
            finish(_dot(lhs, w_ref[...].reshape(K, N)), slice(None))

    in_specs = [pl.BlockSpec((tm, K), lambda i: (i, 0)), pl.BlockSpec(w.shape, lambda i: (0, 0, 0))]
    args = [a, w]
    if has_norm:
        in_specs.append(pl.BlockSpec((1, K), lambda i: (0, 0)))
        args.append(norm_g.reshape(1, K))
    if res is not None:
        in_specs.append(pl.BlockSpec((tm, N), lambda i: (i, 0)))
        args.append(res)
    out_shape = [jax.ShapeDtypeStruct((M, N), out_dtype)]
    out_specs = [pl.BlockSpec((tm, N), lambda i: (i, 0))]
    if has_norm:
        out_shape.append(jax.ShapeDtypeStruct((M, K), BF16))
        out_specs.append(pl.BlockSpec((tm, K), lambda i: (i, 0)))
    outs, carried = _carry_call(body, carry, name=name, grid=(M // tm,), in_specs=in_specs, out_specs=out_specs,
                                out_shape=out_shape, args=args)
    result = outs if has_norm else outs[0]
    return result if carry is None else (result, carried)


def _mm_nt(a, w, wkind, *, tm, name, epi, a2=None, h=None, g=None, dres=None, carry=None):
    parts = tuple(a) if isinstance(a, (tuple, list)) else (a,)
    M, part_w = parts[0].shape
    Nw = part_w * len(parts)
    rows, cols = w.shape[1], w.shape[2]
    Kw = rows if wkind == "col" else N_CHIPS * rows
    assert M % tm == 0 and Nw == (N_CHIPS * cols if wkind == "col" else cols)
    assert epi != "normbwd" or wkind == "col"
    assert len(parts) == 1 or wkind == "col"
    chunk = math.gcd(part_w, cols)

    def body(*refs):
        it = iter(refs)
        a_refs = [next(it) for _ in parts]
        a_ref, w_ref = a_refs[0], next(it)
        a2_ref = next(it) if epi == "sqrt2" else None
        if epi == "normbwd":
            h_ref, g_ref, dres_ref = next(it), next(it), next(it)
        o_ref = next(it)
        dg_ref = next(it) if epi == "normbwd" else None
        i = pl.program_id(0)

        def finish(acc, sl):
            if epi == "bf16":
                o_ref[:, sl] = acc.astype(BF16)
            elif epi == "sqrt2":
                o_ref[:, sl] = (acc * (2.0 * jnp.sqrt(a2_ref[:, sl].astype(F32)))).astype(BF16)
            else:
                x = h_ref[...]
                r = lax.rsqrt(jnp.mean(x * x, axis=-1, keepdims=True) + EPS)
                xh = x * r
                dg_part = jnp.sum(acc * xh, axis=0, keepdims=True)

                @pl.when(i == 0)
                def _():
                    dg_ref[...] = dg_part

                @pl.when(i > 0)
                def _():
                    dg_ref[...] += dg_part
                t = acc * g_ref[...]
                dx = r * (t - xh * jnp.mean(t * xh, axis=-1, keepdims=True))
                o_ref[...] = dres_ref[...] + dx

        if wkind == "col":
            acc = None
            for lo in range(0, Nw, chunk):
                (src, a_lo), (s, w_lo) = divmod(lo, part_w), divmod(lo, cols)
                part = _dot_nt(a_refs[src][:, a_lo:a_lo + chunk].astype(BF16), w_ref[s, :, w_lo:w_lo + chunk])
                acc = part if acc is None else acc + part
            finish(acc, slice(None))
        else:
            lhs = a_ref[...].astype(BF16)
            for s in range(N_CHIPS):
                finish(_dot_nt(lhs, w_ref[s]), slice(s * rows, (s + 1) * rows))

    in_specs = [pl.BlockSpec((tm, part_w), lambda i: (i, 0)) for _ in parts] + [pl.BlockSpec(w.shape, lambda i: (0, 0, 0))]
    args = [*parts, w]
    out_dtype = BF16
    if epi == "sqrt2":
        in_specs.append(pl.BlockSpec((tm, Kw), lambda i: (i, 0)))
        args.append(a2)
    if epi == "normbwd":
        in_specs += [pl.BlockSpec((tm, Kw), lambda i: (i, 0)), pl.BlockSpec((1, Kw), lambda i: (0, 0)),
                     pl.BlockSpec((tm, Kw), lambda i: (i, 0))]
        args += [h, g.reshape(1, Kw), dres]
        out_dtype = F32
    out_shape = [jax.ShapeDtypeStruct((M, Kw), out_dtype)]
    out_specs = [pl.BlockSpec((tm, Kw), lambda i: (i, 0))]
    if epi == "normbwd":
        out_shape.append(jax.ShapeDtypeStruct((1, Kw), F32))
        out_specs.append(pl.BlockSpec((1, Kw), lambda i: (0, 0)))
    outs, carried = _carry_call(body, carry, name=name, grid=(M // tm,), in_specs=in_specs, out_specs=out_specs,
                                out_shape=out_shape, args=args)
    result = outs if epi == "normbwd" else outs[0]
    return result if carry is None else (result, carried)


def _mm_tn(a, b, okind, *, tt, tk, tn, name, carry=None):
    parts = tuple(b) if isinstance(b, (tuple, list)) else (b,)
    T, K = a.shape
    part_w = parts[0].shape[1]
    N = part_w * len(parts)
    assert T % tt == 0 and K % tk == 0 and part_w % tn == 0
    nt = T // tt
    per_part = part_w // tn
    if okind == "col":
        per = (N // N_CHIPS) // tn
        assert (N // N_CHIPS) % tn == 0
        out_shape = jax.ShapeDtypeStruct((N_CHIPS, K, N // N_CHIPS), F32)
        out_spec = pl.BlockSpec((None, tk, tn), lambda ki, nj, t: (nj // per, ki, nj % per))
    else:
        per = (K // N_CHIPS) // tk
        assert (K // N_CHIPS) % tk == 0
        out_shape = jax.ShapeDtypeStruct((N_CHIPS, K // N_CHIPS, N), F32)
        out_spec = pl.BlockSpec((None, tk, tn), lambda ki, nj, t: (ki // per, ki % per, nj))

    def body(a_ref, *refs):
        b_refs, o_ref = refs[:-1], refs[-1]
        nj, t = pl.program_id(1), pl.program_id(2)

        def accumulate(b_ref):
            part = _dot_tn(a_ref[...].astype(BF16), b_ref[...].astype(BF16))
            if nt == 1:
                o_ref[...] = part
            else:
                @pl.when(t == 0)
                def _():
                    o_ref[...] = part

                @pl.when(t > 0)
                def _():
                    o_ref[...] += part

        if len(parts) == 1:
            accumulate(b_refs[0])
        else:
            for s, b_ref in enumerate(b_refs):
                pl.when(nj // per_part == s)(lambda b_ref=b_ref: accumulate(b_ref))

    def b_spec(s):
        return pl.BlockSpec((tt, tn), lambda ki, nj, t: (t, jnp.clip(nj - s * per_part, 0, per_part - 1)))

    outs, carried = _carry_call(
        body, carry, name=name, grid=(K // tk, N // tn, nt),
        in_specs=[pl.BlockSpec((tt, tk), lambda ki, nj, t: (t, ki))] + [b_spec(s) for s in range(len(parts))],
        out_specs=[out_spec], out_shape=[out_shape], args=[a, *parts])
    return outs[0] if carry is None else (outs[0], carried)


def _ret_consts(S, LB):
    log_gamma = jnp.log1p(-jnp.exp2(-5.0 - jnp.arange(RET_HEADS, dtype=F32)))
    idx = jnp.arange(LB, dtype=F32)
    n, m = idx[:, None], idx[None, :]
    cn, cm = jnp.floor(n / CHUNK), jnp.floor(m / CHUNK)
    dist = jnp.where(cm == cn, jnp.abs(n - m), n - m)
    dmat = jnp.where((cm <= cn)[None], jnp.exp(log_gamma[:, None, None] * dist[None]), 0.0)
    qd = jnp.exp(log_gamma[:, None] * (idx + 1.0)[None, :])[..., None]
    kd = jnp.exp(log_gamma[:, None] * (LB - 1 - idx)[None, :])[..., None]
    bd = jnp.exp(log_gamma * LB).reshape(RET_HEADS, 1, 1) * jnp.ones((RET_HEADS, 1, 128), F32)
    half = RET_DK // 2
    inv = jnp.exp(-jnp.log(ROPE_BASE) * jnp.arange(half, dtype=F32) / half)
    ang = jnp.arange(S, dtype=F32)[:, None] * inv[None, :]
    return dmat.astype(F32), qd.astype(F32), kd.astype(F32), bd, jnp.cos(ang), jnp.sin(ang)


def _rope(t, c, s):
    t1, t2 = t[:, :128], t[:, 128:]
    return jnp.concatenate([t1 * c - t2 * s, t1 * s + t2 * c], axis=-1)


def _rope_inv(d, c, s):
    d1, d2 = d[:, :128], d[:, 128:]
    return jnp.concatenate([d1 * c + d2 * s, d2 * c - d1 * s], axis=-1)


def _ret_block_fwd(p_ref, h, c, s, d_ref, qd_ref, kd_ref, stb):
    q = _rope(p_ref[:, h * RET_DK:(h + 1) * RET_DK].astype(F32), c, s)
    k = _rope(p_ref[:, 1024 + h * RET_DK:1024 + (h + 1) * RET_DK].astype(F32), c, s) * (RET_DK ** -0.5)
    v = p_ref[:, 2048 + h * RET_DV:2048 + (h + 1) * RET_DV]
    qb, kb = q.astype(BF16), k.astype(BF16)
    scb = (_dot_nt(qb, kb) * d_ref[h]).astype(BF16)
    o = _dot(scb, v) + qd_ref[h] * _dot(qb, stb)
    return q, k, qb, kb, v, scb, o


def _ret_fwd(proj, gn_g, consts, *, B, S):
    LB = RET_BLOCK
    nb = S // LB
    T = B * S
    dmat, qd, kd, bd, cos, sin = consts

    def body(p_ref, cos_ref, sin_ref, d_ref, qd_ref, kd_ref, bd_ref, gng_ref, y_ref, st_ref, state_s):
        i = pl.program_id(1)

        @pl.when(i == 0)
        def _():
            state_s[...] = jnp.zeros_like(state_s)
        c, s = cos_ref[...], sin_ref[...]
        for h in range(RET_HEADS):
            st = state_s[h]
            stb = st.astype(BF16)
            st_ref[h] = stb
            q, k, qb, kb, v, scb, o = _ret_block_fwd(p_ref, h, c, s, d_ref, qd_ref, kd_ref, stb)
            kdk = (k * kd_ref[h]).astype(BF16)
            state_s[h] = st * bd_ref[h][:, :1] + _dot_tn(kdk, v)
            gate = p_ref[:, 4096 + h * RET_DV:4096 + (h + 1) * RET_DV].astype(F32)
            mu = jnp.mean(o, axis=-1, keepdims=True)
            oc = o - mu
            xh = oc * lax.rsqrt(jnp.mean(oc * oc, axis=-1, keepdims=True) + EPS)
            y = (gate * _sigmoid(gate)) * (xh * gng_ref[:, h * RET_DV:(h + 1) * RET_DV])
            y_ref[:, h * RET_DV:(h + 1) * RET_DV] = y.astype(BF16)

    const = lambda b, i: (0, 0, 0)
    return pl.pallas_call(
        body, name="ret_fwd", grid=(B, nb),
        in_specs=[pl.BlockSpec((LB, 6144), lambda b, i: (b * nb + i, 0)),
                  pl.BlockSpec((LB, 128), lambda b, i: (i, 0)), pl.BlockSpec((LB, 128), lambda b, i: (i, 0)),
                  pl.BlockSpec((RET_HEADS, LB, LB), const), pl.BlockSpec((RET_HEADS, LB, 1), const),
                  pl.BlockSpec((RET_HEADS, LB, 1), const), pl.BlockSpec((RET_HEADS, 1, 128), const),
                  pl.BlockSpec((1, 2048), lambda b, i: (0, 0))],
        out_specs=[pl.BlockSpec((LB, 2048), lambda b, i: (b * nb + i, 0)),
                   pl.BlockSpec((None, None, RET_HEADS, RET_DK, RET_DV), lambda b, i: (b, i, 0, 0, 0))],
        out_shape=[jax.ShapeDtypeStruct((T, 2048), BF16), jax.ShapeDtypeStruct((B, nb, RET_HEADS, RET_DK, RET_DV), BF16)],
        scratch_shapes=[pltpu.VMEM((RET_HEADS, RET_DK, RET_DV), F32)], compiler_params=_params(2),
    )(proj, cos, sin, dmat, qd, kd, bd, gn_g.reshape(1, 2048))


def _ret_bwd(proj, dy, states, gn_g, consts, *, B, S):
    LB = RET_BLOCK
    nb = S // LB
    T = B * S
    dmat, qd, kd, bd, cos, sin = consts

    def body(p_ref, dy_ref, st_ref, cos_ref, sin_ref, d_ref, qd_ref, kd_ref, bd_ref, gng_ref, dp_ref, dgn_ref, dstate_s):
        b, i = pl.program_id(0), pl.program_id(1)

        @pl.when(i == 0)
        def _():
            dstate_s[...] = jnp.zeros_like(dstate_s)

        @pl.when((b == 0) & (i == 0))
        def _():
            dgn_ref[...] = jnp.zeros_like(dgn_ref)
        c, s = cos_ref[...], sin_ref[...]
        for h in range(RET_HEADS):
            vs = slice(h * RET_DV, (h + 1) * RET_DV)
            stb = st_ref[h]
            q, k, qb, kb, v, scb, o = _ret_block_fwd(p_ref, h, c, s, d_ref, qd_ref, kd_ref, stb)
            gate = p_ref[:, 4096 + h * RET_DV:4096 + (h + 1) * RET_DV].astype(F32)
            mu = jnp.mean(o, axis=-1, keepdims=True)
            oc = o - mu
            rstd = lax.rsqrt(jnp.mean(oc * oc, axis=-1, keepdims=True) + EPS)
            xh = oc * rstd
            gng = gng_ref[:, vs]
            dyh = dy_ref[:, vs].astype(F32)
            sg = _sigmoid(gate)
            silu = gate * sg
            dgn_ref[:, vs] += jnp.sum(dyh * silu * xh, axis=0, keepdims=True)
            dxh = dyh * silu * gng
            do = rstd * (dxh - jnp.mean(dxh, axis=-1, keepdims=True) - xh * jnp.mean(dxh * xh, axis=-1, keepdims=True))
            dgate = dyh * xh * gng * (sg * (1.0 + gate * (1.0 - sg)))
            dob = do.astype(BF16)
            dsb = (_dot_nt(dob, v) * d_ref[h]).astype(BF16)
            dst = dstate_s[h]
            dstb = dst.astype(BF16)
            kdk = (k * kd_ref[h]).astype(BF16)
            dqr = _dot(dsb, kb) + qd_ref[h] * _dot_nt(dob, stb)
            dkr = _dot_tn(dsb, qb) + kd_ref[h] * _dot_nt(v, dstb)
            dv = _dot_tn(scb, dob) + _dot(kdk, dstb)
            dstate_s[h] = dst * bd_ref[h][:, :1] + _dot_tn((q * qd_ref[h]).astype(BF16), dob)
            dp_ref[:, h * RET_DK:(h + 1) * RET_DK] = _rope_inv(dqr, c, s).astype(BF16)
            dp_ref[:, 1024 + h * RET_DK:1024 + (h + 1) * RET_DK] = (_rope_inv(dkr, c, s) * (RET_DK ** -0.5)).astype(BF16)
            dp_ref[:, 2048 + h * RET_DV:2048 + (h + 1) * RET_DV] = dv.astype(BF16)
            dp_ref[:, 4096 + h * RET_DV:4096 + (h + 1) * RET_DV] = dgate.astype(BF16)

    const = lambda b, i: (0, 0, 0)
    rev = lambda b, i: (b * nb + nb - 1 - i, 0)
    return pl.pallas_call(
        body, name="ret_bwd", grid=(B, nb),
        in_specs=[pl.BlockSpec((LB, 6144), rev), pl.BlockSpec((LB, 2048), rev),
                  pl.BlockSpec((None, None, RET_HEADS, RET_DK, RET_DV), lambda b, i: (b, nb - 1 - i, 0, 0, 0)),
                  pl.BlockSpec((LB, 128), lambda b, i: (nb - 1 - i, 0)), pl.BlockSpec((LB, 128), lambda b, i: (nb - 1 - i, 0)),
                  pl.BlockSpec((RET_HEADS, LB, LB), const), pl.BlockSpec((RET_HEADS, LB, 1), const),
                  pl.BlockSpec((RET_HEADS, LB, 1), const), pl.BlockSpec((RET_HEADS, 1, 128), const),
                  pl.BlockSpec((1, 2048), lambda b, i: (0, 0))],
        out_specs=[pl.BlockSpec((LB, 6144), rev), pl.BlockSpec((1, 2048), lambda b, i: (0, 0))],
        out_shape=[jax.ShapeDtypeStruct((T, 6144), BF16), jax.ShapeDtypeStruct((1, 2048), F32)],
        scratch_shapes=[pltpu.VMEM((RET_HEADS, RET_DK, RET_DV), F32)], compiler_params=_params(2),
    )(proj, dy, states, cos, sin, dmat, qd, kd, bd, gn_g.reshape(1, 2048))


BIAS_LANES = 4 * ATT_BLOCK


def _diag_onehot():
    r = lax.broadcasted_iota(jnp.int32, (REL_TABLE, BIAS_LANES), 0)
    j = lax.broadcasted_iota(jnp.int32, (REL_TABLE, BIAS_LANES), 1)
    idx = jnp.maximum(j - ATT_BLOCK - PAST, -MAX_REL) + MAX_REL
    return jnp.where(idx == r, 1.0, 0.0).astype(F32)


def _row_is(j):
    return lax.broadcasted_iota(jnp.int32, (8, BIAS_LANES), 0) == j


def _att_bias(table):
    QB, KW = ATT_BLOCK, 3 * ATT_BLOCK

    def body(t_ref, b_ref, bt_ref):
        row = jnp.broadcast_to(t_ref[...], (8, REL_TABLE))
        diag = jnp.dot(row, _diag_onehot(), preferred_element_type=F32, precision=lax.Precision.HIGHEST)
        rows = jnp.zeros((8, BIAS_LANES), F32)
        for j in range(8):
            rows = jnp.where(_row_is(j), diag if j == 0 else pltpu.roll(diag, j, axis=1), rows)
        n = 8
        while n < QB:
            rows = jnp.concatenate([rows, pltpu.roll(rows, n, axis=1)], axis=0)
            n *= 2
        bias = rows[:, QB:]
        qi = lax.broadcasted_iota(jnp.int32, (QB, KW), 0)
        kj = lax.broadcasted_iota(jnp.int32, (QB, KW), 1)
        lo = (qi // CHUNK) * CHUNK
        bias = jnp.where((kj >= lo) & (kj < lo + PAST + CHUNK), bias, NEG)
        b_ref[...] = bias
        bt_ref[...] = bias.T

    return pl.pallas_call(
        body, name="att_bias", grid=(ATT_HEADS,),
        in_specs=[pl.BlockSpec((None, 1, REL_TABLE), lambda h: (h, 0, 0))],
        out_specs=[pl.BlockSpec((None, QB, KW), lambda h: (h, 0, 0)), pl.BlockSpec((None, KW, QB), lambda h: (h // 2, 0, h % 2))],
        out_shape=[jax.ShapeDtypeStruct((ATT_HEADS, QB, KW), F32), jax.ShapeDtypeStruct((ATT_HEADS // 2, KW, 2 * QB), F32)],
        compiler_params=_params(1),
    )(table.reshape(ATT_HEADS, 1, REL_TABLE))


def _att_bias_grad(dbias_t):
    QB, KW = ATT_BLOCK, 3 * ATT_BLOCK

    def body(d_ref, o_ref):
        rows = jnp.concatenate([jnp.zeros((QB, QB), F32), d_ref[...].T], axis=1)
        n = QB // 2
        while n >= 8:
            rows = rows[:n] + pltpu.roll(rows[n:], BIAS_LANES - n, axis=1)
            n //= 2
        acc = jnp.zeros((8, BIAS_LANES), F32)
        for j in range(8):
            acc = acc + jnp.where(_row_is(j), rows if j == 0 else pltpu.roll(rows, BIAS_LANES - j, axis=1), 0.0)
        diag = jnp.broadcast_to(jnp.sum(acc, axis=0, keepdims=True), (8, BIAS_LANES))
        grad = lax.dot_general(diag, _diag_onehot(), (((1,), (1,)), ((), ())), preferred_element_type=F32,
                               precision=lax.Precision.HIGHEST)
        o_ref[...] = grad[:1]

    return pl.pallas_call(
        body, name="att_bias_grad", grid=(ATT_HEADS,),
        in_specs=[pl.BlockSpec((None, KW, QB), lambda h: (h // 2, 0, h % 2))],
        out_specs=pl.BlockSpec((None, 1, REL_TABLE), lambda h: (h, 0, 0)),
        out_shape=jax.ShapeDtypeStruct((ATT_HEADS, 1, REL_TABLE), F32), compiler_params=_params(1),
    )(dbias_t).reshape(ATT_HEADS, REL_TABLE)


def _att_fwd(qkv, bias, *, B, S):
    QB = ATT_BLOCK
    nb = S // QB
    KW = 3 * QB
    T = B * S
    scale = ATT_DH ** -0.5

    def body(q_ref, k0, k1, k2, v0, v1, v2, b_ref, o_ref, lse_ref):
        i = pl.program_id(2)
        k3 = jnp.concatenate([k0[...], k1[...], k2[...]], axis=0)
        v3 = jnp.concatenate([v0[...], v1[...], v2[...]], axis=0)
        TQ = QB
        col = lax.broadcasted_iota(jnp.int32, (TQ, KW), 1)
        in_seq = col >= (2 - i) * QB
        lane = lax.broadcasted_iota(jnp.int32, (TQ, 128), 1)
        lane_kv = lax.broadcasted_iota(jnp.int32, (KW, 128), 1)
        vh = [jnp.where((lane_kv < ATT_DH) == (hh == 0), v3, jnp.zeros_like(v3)) for hh in range(2)]
        outs, lses = [], []
        for r in range(QB // TQ):
            rows = slice(r * TQ, (r + 1) * TQ)
            q = (q_ref[rows, :].astype(F32) * scale).astype(BF16)
            out = jnp.zeros((TQ, 128), F32)
            lse = jnp.zeros((TQ, 128), F32)
            for hh in range(2):
                qh = jnp.where((lane < ATT_DH) == (hh == 0), q, jnp.zeros_like(q))
                s = jnp.where(in_seq, _dot_nt(qh, k3) + b_ref[hh, rows, :], NEG)
                m = jnp.max(s, axis=-1, keepdims=True)
                e = jnp.exp(s - m)
                l = jnp.sum(e, axis=-1, keepdims=True)
                out = out + _dot(e.astype(BF16), vh[hh]) / l
                lse = jnp.where(lane == hh, m + jnp.log(l), lse)
            outs.append(out)
            lses.append(lse)
        o_ref[...] = jnp.concatenate(outs, axis=0).astype(BF16)
        lse_ref[...] = jnp.concatenate(lses, axis=0).T[:8]

    def kv(d, col0):
        return pl.BlockSpec((QB, 128), lambda hp, b, i: (b * nb + jnp.maximum(i - d, 0), col0 + hp))

    return pl.pallas_call(
        body, name="att_fwd", grid=(8, B, nb),
        in_specs=[pl.BlockSpec((QB, 128), lambda hp, b, i: (b * nb + i, hp)),
                  kv(2, 8), kv(1, 8), kv(0, 8), kv(2, 16), kv(1, 16), kv(0, 16),
                  pl.BlockSpec((2, QB, KW), lambda hp, b, i: (hp, 0, 0))],
        out_specs=[pl.BlockSpec((QB, 128), lambda hp, b, i: (b * nb + i, hp)),
                   pl.BlockSpec((None, 8, QB), lambda hp, b, i: (hp, 0, b * nb + i))],
        out_shape=[jax.ShapeDtypeStruct((T, 1024), BF16), jax.ShapeDtypeStruct((8, 8, T), F32)], compiler_params=_params(3),
    )(qkv, qkv, qkv, qkv, qkv, qkv, qkv, bias)


def _att_bwd(qkv, do, o, lse, bias_t, *, B, S):
    QB = ATT_BLOCK
    nb = S // QB
    KW = 3 * QB
    T = B * S
    scale = ATT_DH ** -0.5
    TK = 256

    def body(q_ref, k0, k1, k2, v0, v1, v2, do_ref, o_ref, lse_ref, b_ref, dq_ref, dk_ref, dv_ref, db_ref, dk_acc, dv_acc):
        b, i = pl.program_id(1), pl.program_id(2)

        @pl.when(i == 0)
        def _():
            dk_acc[...] = jnp.zeros_like(dk_acc)
            dv_acc[...] = jnp.zeros_like(dv_acc)

        @pl.when((b == 0) & (i == 0))
        def _():
            db_ref[...] = jnp.zeros_like(db_ref)

        @pl.when(i < nb)
        def _():
            lane = lax.broadcasted_iota(jnp.int32, (QB, 128), 1)
            first = lane < ATT_DH

            def by_head(x):
                zero = jnp.zeros_like(x)
                return jnp.concatenate([jnp.where(first, x, zero), jnp.where(first, zero, x)], axis=0)

            dout = do_ref[...]
            q2 = by_head((q_ref[...].astype(F32) * scale).astype(BF16))
            do2 = by_head(dout)
            delta_t = (o_ref[...].astype(F32) * dout.astype(F32)).T
            delta2 = jnp.concatenate([jnp.sum(delta_t[:ATT_DH], axis=0, keepdims=True),
                                      jnp.sum(delta_t[ATT_DH:], axis=0, keepdims=True)], axis=1)
            lse2 = jnp.concatenate([lse_ref[0:1, :], lse_ref[1:2, :]], axis=1)
            dq_t = jnp.zeros((128, 2 * QB), F32)
            for d, (k_ref, v_ref) in enumerate(((k0, v0), (k1, v1), (k2, v2))):
                kblk, vblk = k_ref[...], v_ref[...]
                kt = kblk.astype(F32).T.astype(BF16)
                lse_d = jnp.where(i + d >= 2, lse2, -NEG)
                slot = (i + 1 + d) % 3
                for t in range(QB // TK):
                    rows = slice(t * TK, (t + 1) * TK)
                    wrows = slice(d * QB + t * TK, d * QB + (t + 1) * TK)
                    p = jnp.exp(_dot_nt(kblk[rows], q2) + b_ref[wrows, :] - lse_d)
                    ds = p * (_dot_nt(vblk[rows], do2) - delta2)
                    db_ref[wrows, :] += ds
                    dsb = ds.astype(BF16)
                    dk_acc[slot, rows, :] += _dot(dsb, q2)
                    dv_acc[slot, rows, :] += _dot(p.astype(BF16), do2)
                    dq_t += _dot(kt[:, rows], dsb)
            row = lax.broadcasted_iota(jnp.int32, (128, QB), 0)
            dq_ref[...] = (jnp.where(row < ATT_DH, dq_t[:, :QB], dq_t[:, QB:]) * scale).T.astype(BF16)

        @pl.when(i >= 2)
        def _():
            slot = (i + 1) % 3
            dk_ref[...] = dk_acc[slot].astype(BF16)
            dv_ref[...] = dv_acc[slot].astype(BF16)
            dk_acc[slot] = jnp.zeros((QB, 128), F32)
            dv_acc[slot] = jnp.zeros((QB, 128), F32)

    def qrow(b, i):
        return b * nb + jnp.minimum(i, nb - 1)

    def kv(d, col0):
        return pl.BlockSpec((QB, 128), lambda hp, b, i: (b * nb + jnp.maximum(jnp.minimum(i, nb - 1) - d, 0), col0 + hp))

    late = pl.BlockSpec((QB, 128), lambda hp, b, i: (b * nb + jnp.maximum(i - 2, 0), hp))
    return pl.pallas_call(
        body, name="att_bwd", grid=(8, B, nb + 2),
        in_specs=[pl.BlockSpec((QB, 128), lambda hp, b, i: (qrow(b, i), hp)),
                  kv(2, 8), kv(1, 8), kv(0, 8), kv(2, 16), kv(1, 16), kv(0, 16),
                  pl.BlockSpec((QB, 128), lambda hp, b, i: (qrow(b, i), hp)),
                  pl.BlockSpec((QB, 128), lambda hp, b, i: (qrow(b, i), hp)),
                  pl.BlockSpec((None, 8, QB), lambda hp, b, i: (hp, 0, qrow(b, i))),
                  pl.BlockSpec((None, KW, 2 * QB), lambda hp, b, i: (hp, 0, 0))],
        out_specs=[pl.BlockSpec((QB, 128), lambda hp, b, i: (qrow(b, i), hp)), late, late,
                   pl.BlockSpec((None, KW, 2 * QB), lambda hp, b, i: (hp, 0, 0))],
        out_shape=[jax.ShapeDtypeStruct((T, 1024), BF16)] * 3 + [jax.ShapeDtypeStruct((ATT_HEADS // 2, KW, 2 * QB), F32)],
        scratch_shapes=[pltpu.VMEM((3, QB, 128), F32), pltpu.VMEM((3, QB, 128), F32)], compiler_params=_params(3),
    )(qkv, qkv, qkv, qkv, qkv, qkv, qkv, do, o, lse, bias_t)


def _loss_head(h, tgt, g, *, tm):
    T, D = h.shape
    n = T // tm

    def body(h_ref, t_ref, g_ref, dh_ref, dg_ref, loss_ref, acc_ref):
        i = pl.program_id(0)
        x = h_ref[...]
        r = lax.rsqrt(jnp.mean(x * x, axis=-1, keepdims=True) + EPS)
        xh = x * r
        gg = g_ref[...]
        diff = xh * gg - t_ref[...]
        sq = jnp.sum(diff * diff, axis=0, keepdims=True)
        dy = diff * (1.0 / D)
        dg_part = jnp.sum(dy * xh, axis=0, keepdims=True)

        @pl.when(i == 0)
        def _():
            acc_ref[...] = sq
            dg_ref[...] = dg_part

        @pl.when(i > 0)
        def _():
            acc_ref[...] += sq
            dg_ref[...] += dg_part
        t = dy * gg
        dh_ref[...] = r * (t - xh * jnp.mean(t * xh, axis=-1, keepdims=True))

        @pl.when(i == n - 1)
        def _():
            loss_ref[...] = (0.5 / D) * jnp.sum(acc_ref[...], axis=1, keepdims=True)

    return pl.pallas_call(
        body, name="loss_head", grid=(n,),
        in_specs=[pl.BlockSpec((tm, D), lambda i: (i, 0)), pl.BlockSpec((tm, D), lambda i: (i, 0)),
                  pl.BlockSpec((1, D), lambda i: (0, 0))],
        out_specs=[pl.BlockSpec((tm, D), lambda i: (i, 0)), pl.BlockSpec((1, D), lambda i: (0, 0)),
                   pl.BlockSpec((1, 1), lambda i: (0, 0))],
        out_shape=[jax.ShapeDtypeStruct((T, D), F32), jax.ShapeDtypeStruct((1, D), F32), jax.ShapeDtypeStruct((1, 1), F32)],
        scratch_shapes=[pltpu.VMEM((1, D), F32)], compiler_params=_params(1),
    )(h, tgt, g.reshape(1, D))


def _tok_tile(T, want):
    t = min(T, want)
    assert T % t == 0
    return t


def _step(x, tgt, slab, slab_rel, place, mix_g, gn_g, mlp_g, fin_g):
    B, S, D = x.shape
    T = B * S
    h0 = x.reshape(T, D)
    tgt = tgt.reshape(T, D)
    tm = _tok_tile(T, 1024)
    tb = _tok_tile(T, 512)
    tq = _tok_tile(T, 256)
    tt = _tok_tile(T, 8192)
    tf = _tok_tile(T, 2048)
    consts = _ret_consts(S, RET_BLOCK)
    w = {}

    (w["ret_w_in"],) = _all_gather_slabs([slab["ret_w_in"]])
    (proj, hn0), (w["ret_w_out"], w["mlp_w1_0"], w["mlp_w2_0"]) = _mm_nn(
        h0, w["ret_w_in"], "col", tm=tb, out_dtype=BF16, name="ret_in", norm_g=mix_g[0],
        carry=_gather_carry([slab["ret_w_out"], slab["mlp_w1_0"], slab["mlp_w2_0"]]))
    y_ret, states = _ret_fwd(proj, gn_g, consts, B=B, S=S)
    h1, (w["att_w_in"], w["att_w_out"], rel_slabs) = _mm_nn(
        y_ret, w["ret_w_out"], "row", tm=tm, out_dtype=F32, name="ret_out", res=h0,
        carry=_gather_carry([slab["att_w_in"], slab["att_w_out"], slab_rel]))
    (a0, hm0), (w["mlp_w1_1"],) = _mm_nn(h1, w["mlp_w1_0"], "col", tm=tb, out_dtype=BF16, name="mlp0_up", norm_g=mlp_g[0],
                                         act="relu2", carry=_gather_carry([slab["mlp_w1_1"]]))
    h2, (w["mlp_w2_1"],) = _mm_nn(a0, w["mlp_w2_0"], "row", tm=tb, out_dtype=F32, name="mlp0_down", res=h1,
                                  carry=_gather_carry([slab["mlp_w2_1"]]))
    rel_bias = jnp.transpose(rel_slabs, (1, 0, 2)).reshape(ATT_HEADS, REL_TABLE)
    bias, bias_t = _att_bias(rel_bias)
    qkv, hn1 = _mm_nn(h2, w["att_w_in"], "col", tm=tb, out_dtype=BF16, name="att_in", norm_g=mix_g[1])
    o_att, lse = _att_fwd(qkv, bias, B=B, S=S)
    h3 = _mm_nn(o_att, w["att_w_out"], "row", tm=tm, out_dtype=F32, name="att_out", res=h2)
    a1, hm1 = _mm_nn(h3, w["mlp_w1_1"], "col", tm=tb, out_dtype=BF16, name="mlp1_up", norm_g=mlp_g[1], act="relu2")
    h4 = _mm_nn(a1, w["mlp_w2_1"], "row", tm=tb, out_dtype=F32, name="mlp1_down", res=h3)
    dh4, d_fin_g, loss = _loss_head(h4, tgt, fin_g, tm=tb)

    gw = {}
    gw["mlp_w2_1"] = _mm_tn(a1, dh4, "row", tt=tf, tk=1024, tn=D, name="d_mlp1_w2")
    dz1 = _mm_nt(dh4, w["mlp_w2_1"], "row", tm=tb, name="d_mlp1_act", epi="sqrt2", a2=a1)
    gw["mlp_w1_1"] = _mm_tn(hm1, dz1, "col", tt=tt, tk=D, tn=256, name="d_mlp1_w1")
    dh3, d_mlp_g1 = _mm_nt(dz1, w["mlp_w1_1"], "col", tm=tb, name="d_mlp1_in", epi="normbwd",
                           h=h3, g=mlp_g[1], dres=dh4)
    gw["att_w_out"] = _mm_tn(o_att, dh3, "row", tt=tf, tk=256, tn=D, name="d_att_wout")
    do_att = _mm_nt(dh3, w["att_w_out"], "row", tm=tb, name="d_att_o", epi="bf16")
    dq, dk, dv, dbias_t = _att_bwd(qkv, do_att, o_att, lse, bias_t, B=B, S=S)
    d_rel = _att_bias_grad(dbias_t)
    dqkv = (dq, dk, dv)
    gw["att_w_in"] = _mm_tn(hn1, dqkv, "col", tt=tt, tk=D, tn=256, name="d_att_win")
    sums, landed = {}, {}

    def swap_halves(names):
        return _sibling_carry([gw[n] for n in names])

    def add_halves(names, carried):
        for t, n in enumerate(names):
            sums[n] = _add_sibling(place, carried[t], carried[len(names) + t], name="chip_sum_" + n)

    def carried_exchange(names):
        return _chips_carry([sums[n][0] for n in names])

    def keep(names, carried):
        landed.update(zip(names, carried[len(names):]))

    layer1 = ["mlp_w1_1", "mlp_w2_1", "att_w_in", "att_w_out"]
    (dh2, d_mix_g1), carried = _mm_nt(dqkv, w["att_w_in"], "col", tm=tb, name="d_att_in", epi="normbwd",
                                      h=h2, g=mix_g[1], dres=dh3, carry=swap_halves(layer1))
    add_halves(layer1, carried)
    gw["mlp_w2_0"] = _mm_tn(a0, dh2, "row", tt=tf, tk=1024, tn=D, name="d_mlp0_w2")
    dz0, carried = _mm_nt(dh2, w["mlp_w2_0"], "row", tm=tb, name="d_mlp0_act", epi="sqrt2", a2=a0,
                          carry=carried_exchange(["mlp_w1_1", "mlp_w2_1"]))
    keep(["mlp_w1_1", "mlp_w2_1"], carried)
    gw["mlp_w1_0"] = _mm_tn(hm0, dz0, "col", tt=tt, tk=D, tn=256, name="d_mlp0_w1")
    (dh1, d_mlp_g0), carried = _mm_nt(dz0, w["mlp_w1_0"], "col", tm=tb, name="d_mlp0_in", epi="normbwd",
                                      h=h1, g=mlp_g[0], dres=dh2,
                                      carry=_merge(carried_exchange(["att_w_in", "att_w_out"]), swap_halves(["mlp_w1_0", "mlp_w2_0"])))
    keep(["att_w_in", "att_w_out"], carried[:4])
    add_halves(["mlp_w1_0", "mlp_w2_0"], carried[4:])
    gw["ret_w_out"] = _mm_tn(y_ret, dh1, "row", tt=tf, tk=512, tn=D, name="d_ret_wout")
    dy_ret, carried = _mm_nt(dh1, w["ret_w_out"], "row", tm=tb, name="d_ret_y", epi="bf16", carry=swap_halves(["ret_w_out"]))
    add_halves(["ret_w_out"], carried)
    dproj, d_gn = _ret_bwd(proj, dy_ret, states, gn_g, consts, B=B, S=S)
    gw["ret_w_in"], carried = _mm_tn(hn0, dproj, "col", tt=tt, tk=D, tn=256, name="d_ret_win",
                                     carry=carried_exchange(["mlp_w1_0", "mlp_w2_0", "ret_w_out"]))
    keep(["mlp_w1_0", "mlp_w2_0", "ret_w_out"], carried)
    add_halves(["ret_w_in"], [gw["ret_w_in"]] + list(_exchange_with_sibling([gw["ret_w_in"]], "ret_in")))
    (dx, d_mix_g0), carried = _mm_nt(dproj, w["ret_w_in"], "col", tm=tq, name="d_ret_in", epi="normbwd",
                                     h=h0, g=mix_g[0], dres=dh1, carry=carried_exchange(["ret_w_in"]))
    keep(["ret_w_in"], carried)
    totals = {n: _add_chips(place, sums[n][1], landed[n], name="total_" + n) for n in BIG}
    small = [d_mix_g0, d_mix_g1, d_mlp_g0, d_mlp_g1, d_fin_g, d_gn.reshape(2, D), d_rel.reshape(5, D)]
    return loss.reshape(()), dx.reshape(B, S, D), totals, small


def _row_tile(r, want=256):
    t = min(r, want)
    assert r % t == 0
    return t


def _into_slab(place, a, layer, dtype, name):
    _, r, c = a.shape
    tr = _row_tile(r)

    def body(place_ref, a_ref, o_ref):
        o_ref[...] = a_ref[...].astype(dtype)

    grid_spec = pltpu.PrefetchScalarGridSpec(
        num_scalar_prefetch=1, grid=(r // tr,), in_specs=[pl.BlockSpec((None, tr, c), lambda i, pr: (layer, i, 0))],
        out_specs=pl.BlockSpec((None, tr, c), lambda i, pr: (pr[0], i, 0)),
    )
    return pl.pallas_call(
        body, name=name, grid_spec=grid_spec, out_shape=jax.ShapeDtypeStruct((N_CHIPS, r, c), dtype), compiler_params=_params(1),
    )(place, a)


def _add_sibling(place, g, recv, name):
    _, r, c = g.shape
    hr = r // 2
    tr = _row_tile(hr)
    nrt = hr // tr

    def body(place_ref, g_ref, r_ref, sb_ref, own_ref):
        v = g_ref[...] + r_ref[...]
        sb_ref[...] = v.astype(BF16)

        @pl.when(pl.program_id(1) == place_ref[0])
        def _():
            own_ref[...] = v

    grid_spec = pltpu.PrefetchScalarGridSpec(
        num_scalar_prefetch=1, grid=(nrt, N_CHIPS),
        in_specs=[pl.BlockSpec((None, tr, c), lambda i, s, pr: (s, pr[1] * nrt + i, 0)),
                  pl.BlockSpec((None, tr, c), lambda i, s, pr: (s, i, 0))],
        out_specs=[pl.BlockSpec((None, tr, c), lambda i, s, pr: (s, i, 0)), pl.BlockSpec((tr, c), lambda i, s, pr: (i, 0))],
    )
    return pl.pallas_call(
        body, name=name, grid_spec=grid_spec,
        out_shape=[jax.ShapeDtypeStruct((N_CHIPS, hr, c), BF16), jax.ShapeDtypeStruct((hr, c), F32)],
        compiler_params=_params(2),
    )(place, g, recv)


def _add_chips(place, own, recv, name):
    hr, c = own.shape
    tr = _row_tile(hr)
    nrt = hr // tr

    def body(place_ref, o_ref, r_ref, t_ref):
        t_ref[...] = ((o_ref[...] + r_ref[0].astype(F32)) + r_ref[1].astype(F32)) + r_ref[2].astype(F32)

    grid_spec = pltpu.PrefetchScalarGridSpec(
        num_scalar_prefetch=1, grid=(nrt,),
        in_specs=[pl.BlockSpec((tr, c), lambda i, pr: (i, 0)), pl.BlockSpec((3, tr, c), lambda i, pr: (0, i, 0))],
        out_specs=pl.BlockSpec((tr, c), lambda i, pr: (pr[1] * nrt + i, 0)),
    )
    return pl.pallas_call(
        body, name=name, grid_spec=grid_spec, out_shape=jax.ShapeDtypeStruct((2 * hr, c), F32), compiler_params=_params(1),
    )(place, own, recv)


def _adamw(w, gs, m, v, name):
    L, r, c = w.shape
    tr = _row_tile(r)
    assert len(gs) == L

    def body(*refs):
        w_ref, m_ref, v_ref = refs[:3]
        g_refs = refs[3:3 + L]
        go_ref, d_ref, nm_ref, nv_ref = refs[3 + L:]
        gg = g_refs[0][...]
        for k in range(1, L):
            gg = jnp.where(pl.program_id(0) == k, g_refs[k][...], gg)
        go_ref[...] = gg
        nm = ADAM_B1 * m_ref[...] + (1.0 - ADAM_B1) * gg
        nv = ADAM_B2 * v_ref[...] + (1.0 - ADAM_B2) * (gg * gg)
        m_hat = nm / (1.0 - ADAM_B1 ** ADAM_STEP)
        v_hat = nv / (1.0 - ADAM_B2 ** ADAM_STEP)
        d_ref[...] = -ADAM_LR * (m_hat / (jnp.sqrt(v_hat) + ADAM_EPS) + ADAM_WD * w_ref[...])
        nm_ref[...] = nm
        nv_ref[...] = nv

    spec = pl.BlockSpec((None, tr, c), lambda l, i: (l, i, 0))
    return pl.pallas_call(
        body, name=name, grid=(L, r // tr), in_specs=[spec] * 3 + [pl.BlockSpec((tr, c), lambda l, i: (i, 0))] * L,
        out_specs=[spec] * 4, out_shape=[jax.ShapeDtypeStruct((L, r, c), F32)] * 4, compiler_params=_params(2),
    )(w, m, v, *gs)


def _place():
    return lax.axis_index("x"), lax.axis_index("y"), lax.axis_index("c")


def _other_chips(x, y):
    return [(1 - x, y), (x, 1 - y), (1 - x, 1 - y)]


def _remote(src, dst, ssem, rsem, dev):
    return pltpu.make_async_remote_copy(src_ref=src, dst_ref=dst, send_sem=ssem, recv_sem=rsem, device_id=dev,
                                        device_id_type=MESH)


def _gather_phases(n):
    def geometry(refs, t):
        x, y, c = _place()
        hr = refs[t].shape[1] // 2
        chips = _other_chips(x, y)
        return x, y, c, 2 * x + y, chips, [2 * qx + qy for qx, qy in chips], pl.ds(c * hr, hr), pl.ds((1 - c) * hr, hr)

    def send(refs, sems):
        s1, r1, _, _ = sems
        for t in range(n):
            x, y, c, p, chips, cidx, mine, theirs = geometry(refs, t)
            for j, (qx, qy) in enumerate(chips):
                _remote(refs[t].at[p, mine], refs[t].at[p, mine], s1.at[t, j], r1.at[t, j], (qx, qy, c)).start()

    def pass_on(refs, sems):
        s1, r1, s2, r2 = sems
        for t in range(n):
            x, y, c, p, chips, cidx, mine, theirs = geometry(refs, t)
            for j, (qx, qy) in enumerate(chips):
                got = refs[t].at[cidx[j], mine]
                _remote(got, got, s1.at[t, j], r1.at[t, j], (qx, qy, c)).wait_recv()
                _remote(got, got, s2.at[t, j], r2.at[t, j], (x, y, 1 - c)).start()

    def finish(refs, sems):
        s1, r1, s2, r2 = sems
        for t in range(n):
            x, y, c, p, chips, cidx, mine, theirs = geometry(refs, t)
            for j, (qx, qy) in enumerate(chips):
                got = refs[t].at[cidx[j], theirs]
                _remote(got, got, s2.at[t, j], r2.at[t, j], (x, y, 1 - c)).wait_recv()
        for t in range(n):
            x, y, c, p, chips, cidx, mine, theirs = geometry(refs, t)
            for j, (qx, qy) in enumerate(chips):
                _remote(refs[t].at[p, mine], refs[t].at[p, mine], s1.at[t, j], r1.at[t, j], (qx, qy, c)).wait_send()
                sent = refs[t].at[cidx[j], mine]
                _remote(sent, sent, s2.at[t, j], r2.at[t, j], (x, y, 1 - c)).wait_send()

    sem = pltpu.SemaphoreType.DMA
    return send, pass_on, finish, [sem((n, 3)), sem((n, 3)), sem((n, 3)), sem((n, 3))]


def _gather_carry(slabs):
    send, pass_on, finish, sems = _gather_phases(len(slabs))
    return _Carry(slabs, sems, [(0, send), (-2, pass_on), (-1, finish)])


def _all_gather_slabs(slabs):
    n = len(slabs)
    send, pass_on, finish, sems = _gather_phases(n)

    def body(*refs):
        outs, scratch = refs[n:2 * n], refs[2 * n:]
        send(outs, scratch)
        pass_on(outs, scratch)
        finish(outs, scratch)

    return pl.pallas_call(
        body, name="gather_weights", in_specs=[HBM_SPEC] * n, out_specs=[HBM_SPEC] * n,
        out_shape=[jax.ShapeDtypeStruct(s.shape, s.dtype) for s in slabs], input_output_aliases={t: t for t in range(n)},
        scratch_shapes=sems, compiler_params=pltpu.CompilerParams(has_side_effects=True),
    )(*slabs)


def _sibling_phases(n):
    def copies(refs, sems):
        ssem, rsem = sems
        x, y, c = _place()
        out = []
        for t in range(n):
            hr = refs[t].shape[1] // 2
            out.append(_remote(refs[t].at[:, pl.ds((1 - c) * hr, hr), :], refs[n + t], ssem.at[t], rsem.at[t], (x, y, 1 - c)))
        return out

    def send(refs, sems):
        for cp in copies(refs, sems):
            cp.start()

    def finish(refs, sems):
        for cp in copies(refs, sems):
            cp.wait()

    sem = pltpu.SemaphoreType.DMA
    return send, finish, [sem((n,)), sem((n,))]


def _halves_landing(grads):
    return [lax.empty((N_CHIPS, g.shape[1] // 2, g.shape[2]), g.dtype) for g in grads]


def _sibling_carry(grads):
    send, finish, sems = _sibling_phases(len(grads))
    return _Carry(list(grads) + _halves_landing(grads), sems, [(0, send), (-1, finish)])


def _exchange_with_sibling(grads, tag):
    n = len(grads)
    send, finish, sems = _sibling_phases(n)

    def body(*refs):
        both, scratch = refs[:2 * n], refs[2 * n:]
        send(both, scratch)
        finish(both, scratch)

    return pl.pallas_call(
        body, name="grads_to_sibling_" + tag, in_specs=[HBM_SPEC] * n, out_specs=[HBM_SPEC] * n,
        out_shape=[jax.ShapeDtypeStruct((N_CHIPS, g.shape[1] // 2, g.shape[2]), g.dtype) for g in grads],
        scratch_shapes=sems, compiler_params=pltpu.CompilerParams(has_side_effects=True),
    )(*grads)


def _chips_phases(n):
    def copies(refs, sems):
        ssem, rsem = sems
        x, y, c = _place()
        return [_remote(refs[t].at[2 * qx + qy], refs[n + t].at[j], ssem.at[t, j], rsem.at[t, j], (qx, qy, c))
                for t in range(n) for j, (qx, qy) in enumerate(_other_chips(x, y))]

    def send(refs, sems):
        for cp in copies(refs, sems):
            cp.start()

    def finish(refs, sems):
        for cp in copies(refs, sems):
            cp.wait()

    sem = pltpu.SemaphoreType.DMA
    return send, finish, [sem((n, 3)), sem((n, 3))]


def _landing(sums):
    return [lax.empty((3,) + s.shape[1:], s.dtype) for s in sums]


def _chips_carry(sums):
    send, finish, sems = _chips_phases(len(sums))
    return _Carry(list(sums) + _landing(sums), sems, [(0, send), (-1, finish)])


def _share_with_sibling(shards):
    n = len(shards)

    def body(*refs):
        outs = refs[n:2 * n]
        ssem, rsem = refs[2 * n:]
        x, y, c = _place()
        copies = []
        for t in range(n):
            hr = outs[t].shape[0] // 2
            mine = outs[t].at[pl.ds(c * hr, hr)]
            cp = _remote(mine, mine, ssem.at[t], rsem.at[t], (x, y, 1 - c))
            cp.start()
            copies.append(cp)
        for t in range(n):
            hr = outs[t].shape[0] // 2
            theirs = outs[t].at[pl.ds((1 - c) * hr, hr)]
            _remote(theirs, theirs, ssem.at[t], rsem.at[t], (x, y, 1 - c)).wait_recv()
        for cp in copies:
            cp.wait_send()

    sem = pltpu.SemaphoreType.DMA
    return pl.pallas_call(
        body, name="grads_share", in_specs=[HBM_SPEC] * n, out_specs=[HBM_SPEC] * n,
        out_shape=[jax.ShapeDtypeStruct(s.shape, s.dtype) for s in shards], input_output_aliases={t: t for t in range(n)},
        scratch_shapes=[sem((n,)), sem((n,))], compiler_params=pltpu.CompilerParams(has_side_effects=True),
    )(*shards)


def _all_reduce_small(buf):
    R, C = buf.shape

    def body(in_ref, out_ref, gather, ssem, rsem):
        x, y, c = _place()
        me = 4 * x + 2 * y + c
        gather[me] = in_ref[...]
        flips = [(fx, fy, fc) for fx in (0, 1) for fy in (0, 1) for fc in (0, 1) if fx or fy or fc]
        peers = [(x + fx - 2 * x * fx, y + fy - 2 * y * fy, c + fc - 2 * c * fc) for fx, fy, fc in flips]
        copies = [_remote(in_ref, gather.at[me], ssem.at[k], rsem.at[k], peer) for k, peer in enumerate(peers)]
        for cp in copies:
            cp.start()
        for k, (px, py, pc) in enumerate(peers):
            _remote(in_ref, gather.at[4 * px + 2 * py + pc], ssem.at[k], rsem.at[k], (px, py, pc)).wait_recv()
        for cp in copies:
            cp.wait_send()
        acc = gather[0]
        for d in range(1, 8):
            acc = acc + gather[d]
        out_ref[...] = acc

    sem = pltpu.SemaphoreType.DMA
    vmem = pl.BlockSpec(memory_space=pltpu.VMEM)
    return pl.pallas_call(
        body, name="small_grads_sum", in_specs=[vmem], out_specs=vmem, out_shape=jax.ShapeDtypeStruct((R, C), F32),
        scratch_shapes=[pltpu.VMEM((8, R, C), F32), sem((7,)), sem((7,))],
        compiler_params=pltpu.CompilerParams(has_side_effects=True),
    )(buf)


BIG = ["ret_w_in", "ret_w_out", "att_w_in", "att_w_out", "mlp_w1_0", "mlp_w1_1", "mlp_w2_0", "mlp_w2_1"]
LAYER_OF = {"ret_w_in": ("ret_w_in", 0), "ret_w_out": ("ret_w_out", 0), "att_w_in": ("att_w_in", 0), "att_w_out": ("att_w_out", 0),
            "mlp_w1_0": ("mlp_w1", 0), "mlp_w1_1": ("mlp_w1", 1), "mlp_w2_0": ("mlp_w2", 0), "mlp_w2_1": ("mlp_w2", 1)}
ORDER = ["mix_norm_g", "ret_w_in", "ret_gn_g", "ret_w_out", "att_w_in", "att_rel_bias", "att_w_out", "mlp_norm_g", "mlp_w1", "mlp_w2",
         "final_norm_g"]


def kernel(x, mix_norm_g, ret_w_in, ret_gn_g, ret_w_out, att_w_in, att_rel_bias, att_w_out, mlp_norm_g, mlp_w1, mlp_w2, final_norm_g, loss_target, m_mix_norm_g, m_ret_w_in, m_ret_gn_g, m_ret_w_out, m_att_w_in, m_att_rel_bias, m_att_w_out, m_mlp_norm_g, m_mlp_w1, m_mlp_w2, m_final_norm_g, v_mix_norm_g, v_ret_w_in, v_ret_gn_g, v_ret_w_out, v_att_w_in, v_att_rel_bias, v_att_w_out, v_mlp_norm_g, v_mlp_w1, v_mlp_w2, v_final_norm_g):
    xi, yi, ci = _place()
    chip = 2 * xi + yi
    weights = dict(zip(ORDER, (mix_norm_g, ret_w_in, ret_gn_g, ret_w_out, att_w_in, att_rel_bias, att_w_out, mlp_norm_g, mlp_w1,
                               mlp_w2, final_norm_g)))
    first = dict(zip(ORDER, (m_mix_norm_g, m_ret_w_in, m_ret_gn_g, m_ret_w_out, m_att_w_in, m_att_rel_bias, m_att_w_out,
                             m_mlp_norm_g, m_mlp_w1, m_mlp_w2, m_final_norm_g)))
    second = dict(zip(ORDER, (v_mix_norm_g, v_ret_w_in, v_ret_gn_g, v_ret_w_out, v_att_w_in, v_att_rel_bias, v_att_w_out,
                              v_mlp_norm_g, v_mlp_w1, v_mlp_w2, v_final_norm_g)))

    place = jnp.stack([chip, ci]).astype(jnp.int32)
    slab = {n: _into_slab(place, weights[LAYER_OF[n][0]], LAYER_OF[n][1], BF16, name="cast_" + n) for n in BIG}
    slab_rel = _into_slab(place, att_rel_bias, 0, F32, name="slab_rel_bias")

    loss_local, grad_x, totals, small = _step(x, loss_target, slab, slab_rel, place, mix_norm_g, ret_gn_g[0], mlp_norm_g,
                                              final_norm_g)
    loss = lax.psum(loss_local, ("x", "y", "c"))

    g_big = dict(zip(BIG, _share_with_sibling([totals[n] for n in BIG])))

    rows, at = jnp.zeros((16, D_MODEL), F32), 0
    for part in small:
        rows = rows + jnp.pad(part, ((at, 16 - at - part.shape[0]), (0, 0)))
        at += part.shape[0]
    rows = _all_reduce_small(rows)
    grads = {"mix_norm_g": [rows[0:2]], "mlp_norm_g": [rows[2:4]], "final_norm_g": [rows[4:5]], "ret_gn_g": [rows[5:7].reshape(1, 2048)],
             "att_rel_bias": [lax.dynamic_slice_in_dim(rows[7:12].reshape(ATT_HEADS, REL_TABLE), chip * (REL_TABLE // N_CHIPS),
                                                       REL_TABLE // N_CHIPS, axis=1)]}
    for n in BIG:
        grads.setdefault(LAYER_OF[n][0], []).append(g_big[n])

    def as3(a):
        return a.reshape((1,) * (3 - a.ndim) + a.shape)

    results = {}
    for n in ORDER:
        outs = _adamw(as3(weights[n]), grads[n], as3(first[n]), as3(second[n]), name="adamw_" + n)
        results[n] = [o.reshape(weights[n].shape) for o in outs]
    return (loss, grad_x) + tuple(results[n][k] for k in range(4) for n in ORDER)
```

```python
import math

import jax
import jax.numpy as jnp
from jax import lax
from jax.experimental import pallas as pl
from jax.experimental.pallas import tpu as pltpu

F32 = jnp.float32
BF16 = jnp.bfloat16
MESH = pl.DeviceIdType.MESH

D_MODEL = 1024
CHUNK = 64
RET_HEADS = 4
RET_DK = 256
RET_DV = 512
ROPE_BASE = 10000.0
ATT_HEADS = 16
ATT_DH = 64
PAST = 512
MAX_REL = 256
REL_TABLE = MAX_REL + CHUNK
EPS = 1e-6
NEG = -1e30
N_CHIPS = 4

ADAM_LR = 0.001
ADAM_B1 = 0.9
ADAM_B2 = 0.999
ADAM_EPS = 1e-08
ADAM_WD = 0.01
ADAM_STEP = 10

RET_BLOCK = 256
ATT_BLOCK = 256
VMEM_LIMIT = 56 * 1024 * 1024


def _params(n_axes, **kw):
    return pltpu.CompilerParams(dimension_semantics=("arbitrary",) * n_axes, vmem_limit_bytes=VMEM_LIMIT, **kw)


def _dot(a, b):
    return jnp.dot(a, b, preferred_element_type=F32)


def _dot_nt(a, b):
    return lax.dot_general(a, b, (((1,), (1,)), ((), ())), preferred_element_type=F32)


def _dot_tn(a, b):
    return lax.dot_general(a, b, (((0,), (0,)), ((), ())), preferred_element_type=F32)


def _sigmoid(x):
    return 1.0 / (1.0 + jnp.exp(-x))


HBM_SPEC = pl.BlockSpec(memory_space=pltpu.HBM)


class _Carry:
    def __init__(self, arrays, sems, stages):
        self.arrays, self.sems, self.stages = list(arrays), list(sems), list(stages)


def _merge(a, b):
    na, sa = len(a.arrays), len(a.sems)

    def of_a(fn):
        return lambda refs, sems: fn(refs[:na], sems[:sa])

    def of_b(fn):
        return lambda refs, sems: fn(refs[na:], sems[sa:])

    return _Carry(a.arrays + b.arrays, a.sems + b.sems,
                  [(at, of_a(fn)) for at, fn in a.stages] + [(at, of_b(fn)) for at, fn in b.stages])


def _carry_call(body, carry, *, name, grid, in_specs, out_specs, out_shape, args):
    if carry is None:
        outs = pl.pallas_call(body, name=name, grid=grid, in_specs=in_specs, out_specs=out_specs, out_shape=out_shape,
                              compiler_params=_params(len(grid)))(*args)
        return list(outs), []
    n_in, n_out, n_c = len(in_specs), len(out_specs), len(carry.arrays)
    steps = 1
    for g in grid:
        steps *= g
    assert all(-steps <= at < steps for at, _ in carry.stages)

    def carrying(*refs):
        ins, outs = refs[:n_in], refs[n_in + n_c:n_in + n_c + n_out]
        carried = refs[n_in + n_c + n_out:n_in + 2 * n_c + n_out]
        sems = refs[n_in + 2 * n_c + n_out:]
        step = pl.program_id(0)
        for axis in range(1, len(grid)):
            step = step * grid[axis] + pl.program_id(axis)
        for at, fn in carry.stages:
            if at == 0:
                pl.when(step == 0)(lambda fn=fn: fn(carried, sems))
        body(*ins, *outs)
        for at, fn in carry.stages:
            if at != 0:
                pl.when(step == at % steps)(lambda fn=fn: fn(carried, sems))

    outs = pl.pallas_call(
        carrying, name=name, grid=grid, in_specs=list(in_specs) + [HBM_SPEC] * n_c, out_specs=list(out_specs) + [HBM_SPEC] * n_c,
        out_shape=list(out_shape) + [jax.ShapeDtypeStruct(a.shape, a.dtype) for a in carry.arrays],
        input_output_aliases={n_in + t: n_out + t for t in range(n_c)}, scratch_shapes=carry.sems,
        compiler_params=_params(len(grid), has_side_effects=True),
    )(*args, *carry.arrays)
    return list(outs[:n_out]), list(outs[n_out:])


def _mm_nn(a, w, wkind, *, tm, out_dtype, name, norm_g=None, act=None, square=False, res=None, loss=None, carry=None):
    M, K = a.shape
    cols = w.shape[2]
    N = N_CHIPS * cols if wkind == "col" else cols
    has_norm = norm_g is not None
    steps = M // tm
    assert M % tm == 0 and K == (w.shape[1] if wkind == "col" else N_CHIPS * w.shape[1])
    assert loss is None or (wkind == "row" and act is None)

    def body(*refs):
        it = iter(refs)
        a_ref, w_ref = next(it), next(it)
        g_ref = next(it) if has_norm else None
        r_ref = next(it) if res is not None else None
        if loss is not None:
            t_ref, fg_ref = next(it), next(it)
        o_ref = next(it)
        hn_ref = next(it) if has_norm else None
        if loss is not None:
            dg_ref, sq_ref, loss_ref = next(it), next(it), next(it)
        if has_norm:
            x = a_ref[...].astype(F32)
            r = lax.rsqrt(jnp.mean(x * x, axis=-1, keepdims=True) + EPS)
            lhs = (x * r * g_ref[...]).astype(BF16)
            hn_ref[...] = lhs
        elif square:
            lhs = a_ref[...].astype(BF16)
            lhs = lhs * lhs
        else:
            lhs = a_ref[...].astype(BF16)

        def loss_head(x):
            i = pl.program_id(0)
            r = lax.rsqrt(jnp.mean(x * x, axis=-1, keepdims=True) + EPS)
            xh = x * r
            gg = fg_ref[...]
            diff = xh * gg - t_ref[...]
            sq = jnp.sum(diff * diff, axis=0, keepdims=True)
            dy = diff * (1.0 / N)
            dg_part = jnp.sum(dy * xh, axis=0, keepdims=True)

            @pl.when(i == 0)
            def _():
                sq_ref[...] = sq
                dg_ref[...] = dg_part

            @pl.when(i > 0)
            def _():
                sq_ref[...] += sq
                dg_ref[...] += dg_part
            t = dy * gg
            o_ref[...] = r * (t - xh * jnp.mean(t * xh, axis=-1, keepdims=True))

            @pl.when(i == steps - 1)
            def _():
                loss_ref[...] = (0.5 / N) * jnp.sum(sq_ref[...], axis=1, keepdims=True)

        def finish(acc, sl):
            if act == "relu":
                acc = jnp.maximum(acc, 0.0)
            if r_ref is not None:
                acc = acc + r_ref[:, sl]
            if loss is not None:
                loss_head(acc)
            else:
                o_ref[:, sl] = acc.astype(out_dtype)

        if wkind == "col":
            for s in range(N_CHIPS):
                finish(_dot(lhs, w_ref[s]), slice(s * cols, (s + 1) * cols))
        else:
            finish(_dot(lhs, w_ref[...].reshape(K, N)), slice(None))

    in_specs = [pl.BlockSpec((tm, K), lambda i: (i, 0)), pl.BlockSpec(w.shape, lambda i: (0, 0, 0))]
    args = [a, w]
    if has_norm:
        in_specs.append(pl.BlockSpec((1, K), lambda i: (0, 0)))
        args.append(norm_g.reshape(1, K))
    if res is not None:
        in_specs.append(pl.BlockSpec((tm, N), lambda i: (i, 0)))
        args.append(res)
    if loss is not None:
        in_specs += [pl.BlockSpec((tm, N), lambda i: (i, 0)), pl.BlockSpec((1, N), lambda i: (0, 0))]
        args += [loss[0], loss[1].reshape(1, N)]
    out_shape = [jax.ShapeDtypeStruct((M, N), out_dtype)]
    out_specs = [pl.BlockSpec((tm, N), lambda i: (i, 0))]
    if has_norm:
        out_shape.append(jax.ShapeDtypeStruct((M, K), BF16))
        out_specs.append(pl.BlockSpec((tm, K), lambda i: (i, 0)))
    if loss is not None:
        out_shape += [jax.ShapeDtypeStruct((1, N), F32), jax.ShapeDtypeStruct((1, N), F32), jax.ShapeDtypeStruct((1, 1), F32)]
        out_specs += [pl.BlockSpec((1, N), lambda i: (0, 0)), pl.BlockSpec((1, N), lambda i: (0, 0)),
                      pl.BlockSpec((1, 1), lambda i: (0, 0))]
    outs, carried = _carry_call(body, carry, name=name, grid=(steps,), in_specs=in_specs, out_specs=out_specs,
                                out_shape=out_shape, args=args)
    result = outs if has_norm or loss is not None else outs[0]
    return result if carry is None else (result, carried)


def _mm_nt(a, w, wkind, *, tm, name, epi, a2=None, h=None, g=None, dres=None, carry=None):
    parts = tuple(a) if isinstance(a, (tuple, list)) else (a,)
    M, part_w = parts[0].shape
    Nw = part_w * len(parts)
    rows, cols = w.shape[1], w.shape[2]
    Kw = rows if wkind == "col" else N_CHIPS * rows
    assert M % tm == 0 and Nw == (N_CHIPS * cols if wkind == "col" else cols)
    assert epi != "normbwd" or wkind == "col"
    assert len(parts) == 1 or wkind == "col"
    chunk = math.gcd(part_w, cols)

    def body(*refs):
        it = iter(refs)
        a_refs = [next(it) for _ in parts]
        a_ref, w_ref = a_refs[0], next(it)
        a2_ref = next(it) if epi == "relu2bwd" else None
        if epi == "normbwd":
            h_ref, g_ref, dres_ref = next(it), next(it), next(it)
        o_ref = next(it)
        dg_ref = next(it) if epi == "normbwd" else None
        i = pl.program_id(0)

        def finish(acc, sl):
            if epi == "bf16":
                o_ref[:, sl] = acc.astype(BF16)
            elif epi == "relu2bwd":
                o_ref[:, sl] = (acc * (2.0 * a2_ref[:, sl].astype(F32))).astype(BF16)
            else:
                x = h_ref[...]
                r = lax.rsqrt(jnp.mean(x * x, axis=-1, keepdims=True) + EPS)
                xh = x * r
                dg_part = jnp.sum(acc * xh, axis=0, keepdims=True)

                @pl.when(i == 0)
                def _():
                    dg_ref[...] = dg_part

                @pl.when(i > 0)
                def _():
                    dg_ref[...] += dg_part
                t = acc * g_ref[...]
                dx = r * (t - xh * jnp.mean(t * xh, axis=-1, keepdims=True))
                o_ref[...] = dres_ref[...] + dx

        if wkind == "col":
            acc = None
            for lo in range(0, Nw, chunk):
                (src, a_lo), (s, w_lo) = divmod(lo, part_w), divmod(lo, cols)
                part = _dot_nt(a_refs[src][:, a_lo:a_lo + chunk].astype(BF16), w_ref[s, :, w_lo:w_lo + chunk])
                acc = part if acc is None else acc + part
            finish(acc, slice(None))
        else:
            lhs = a_ref[...].astype(BF16)
            for s in range(N_CHIPS):
                finish(_dot_nt(lhs, w_ref[s]), slice(s * rows, (s + 1) * rows))

    in_specs = [pl.BlockSpec((tm, part_w), lambda i: (i, 0)) for _ in parts] + [pl.BlockSpec(w.shape, lambda i: (0, 0, 0))]
    args = [*parts, w]
    out_dtype = BF16
    if epi == "relu2bwd":
        in_specs.append(pl.BlockSpec((tm, Kw), lambda i: (i, 0)))
        args.append(a2)
    if epi == "normbwd":
        in_specs += [pl.BlockSpec((tm, Kw), lambda i: (i, 0)), pl.BlockSpec((1, Kw), lambda i: (0, 0)),
                     pl.BlockSpec((tm, Kw), lambda i: (i, 0))]
        args += [h, g.reshape(1, Kw), dres]
        out_dtype = F32
    out_shape = [jax.ShapeDtypeStruct((M, Kw), out_dtype)]
    out_specs = [pl.BlockSpec((tm, Kw), lambda i: (i, 0))]
    if epi == "normbwd":
        out_shape.append(jax.ShapeDtypeStruct((1, Kw), F32))
        out_specs.append(pl.BlockSpec((1, Kw), lambda i: (0, 0)))
    outs, carried = _carry_call(body, carry, name=name, grid=(M // tm,), in_specs=in_specs, out_specs=out_specs,
                                out_shape=out_shape, args=args)
    result = outs if epi == "normbwd" else outs[0]
    return result if carry is None else (result, carried)


def _mm_tn(a, b, okind, *, tt, tk, tn, name, square=False, carry=None):
    parts = tuple(b) if isinstance(b, (tuple, list)) else (b,)
    T, K = a.shape
    part_w = parts[0].shape[1]
    N = part_w * len(parts)
    assert T % tt == 0 and K % tk == 0 and part_w % tn == 0
    nt = T // tt
    per_part = part_w // tn
    if okind == "col":
        per = (N // N_CHIPS) // tn
        assert (N // N_CHIPS) % tn == 0
        out_shape = jax.ShapeDtypeStruct((N_CHIPS, K, N // N_CHIPS), F32)
        out_spec = pl.BlockSpec((None, tk, tn), lambda ki, nj, t: (nj // per, ki, nj % per))
    else:
        per = (K // N_CHIPS) // tk
        assert (K // N_CHIPS) % tk == 0
        out_shape = jax.ShapeDtypeStruct((N_CHIPS, K // N_CHIPS, N), F32)
        out_spec = pl.BlockSpec((None, tk, tn), lambda ki, nj, t: (ki // per, ki % per, nj))

    def body(a_ref, *refs):
        b_refs, o_ref = refs[:-1], refs[-1]
        nj, t = pl.program_id(1), pl.program_id(2)

        def accumulate(b_ref):
            lhs = a_ref[...].astype(BF16)
            part = _dot_tn(lhs * lhs if square else lhs, b_ref[...].astype(BF16))
            if nt == 1:
                o_ref[...] = part
            else:
                @pl.when(t == 0)
                def _():
                    o_ref[...] = part

                @pl.when(t > 0)
                def _():
                    o_ref[...] += part

        if len(parts) == 1:
            accumulate(b_refs[0])
        else:
            for s, b_ref in enumerate(b_refs):
                pl.when(nj // per_part == s)(lambda b_ref=b_ref: accumulate(b_ref))

    def b_spec(s):
        return pl.BlockSpec((tt, tn), lambda ki, nj, t: (t, jnp.clip(nj - s * per_part, 0, per_part - 1)))

    outs, carried = _carry_call(
        body, carry, name=name, grid=(K // tk, N // tn, nt),
        in_specs=[pl.BlockSpec((tt, tk), lambda ki, nj, t: (t, ki))] + [b_spec(s) for s in range(len(parts))],
        out_specs=[out_spec], out_shape=[out_shape], args=[a, *parts])
    return outs[0] if carry is None else (outs[0], carried)


def _ret_consts(S, LB):
    log_gamma = jnp.log1p(-jnp.exp2(-5.0 - jnp.arange(RET_HEADS, dtype=F32)))
    idx = jnp.arange(LB, dtype=F32)
    n, m = idx[:, None], idx[None, :]
    cn, cm = jnp.floor(n / CHUNK), jnp.floor(m / CHUNK)
    dist = jnp.where(cm == cn, jnp.abs(n - m), n - m)
    dmat = jnp.where((cm <= cn)[None], jnp.exp(log_gamma[:, None, None] * dist[None]), 0.0)
    qd = jnp.exp(log_gamma[:, None] * (idx + 1.0)[None, :])[..., None]
    kd = jnp.exp(log_gamma[:, None] * (LB - 1 - idx)[None, :])[..., None]
    bd = jnp.exp(log_gamma * LB).reshape(RET_HEADS, 1, 1) * jnp.ones((RET_HEADS, 1, 128), F32)
    half = RET_DK // 2
    inv = jnp.exp(-jnp.log(ROPE_BASE) * jnp.arange(half, dtype=F32) / half)
    ang = jnp.arange(S, dtype=F32)[:, None] * inv[None, :]
    return dmat.astype(F32), qd.astype(F32), kd.astype(F32), bd, jnp.cos(ang), jnp.sin(ang)


def _rope(t, c, s):
    t1, t2 = t[:, :128], t[:, 128:]
    return jnp.concatenate([t1 * c - t2 * s, t1 * s + t2 * c], axis=-1)


def _rope_inv(d, c, s):
    d1, d2 = d[:, :128], d[:, 128:]
    return jnp.concatenate([d1 * c + d2 * s, d2 * c - d1 * s], axis=-1)


def _ret_block_fwd(p_ref, h, c, s, d_ref, qd_ref, kd_ref, stb):
    q = _rope(p_ref[:, h * RET_DK:(h + 1) * RET_DK].astype(F32), c, s)
    k = _rope(p_ref[:, 1024 + h * RET_DK:1024 + (h + 1) * RET_DK].astype(F32), c, s) * (RET_DK ** -0.5)
    v = p_ref[:, 2048 + h * RET_DV:2048 + (h + 1) * RET_DV]
    qb, kb = q.astype(BF16), k.astype(BF16)
    scb = (_dot_nt(qb, kb) * d_ref[h]).astype(BF16)
    o = _dot(scb, v) + qd_ref[h] * _dot(qb, stb)
    return q, k, qb, kb, v, scb, o


def _ret_fwd(proj, gn_g, consts, *, B, S):
    LB = RET_BLOCK
    nb = S // LB
    T = B * S
    dmat, qd, kd, bd, cos, sin = consts

    def body(p_ref, cos_ref, sin_ref, d_ref, qd_ref, kd_ref, bd_ref, gng_ref, y_ref, st_ref, state_s):
        i = pl.program_id(1)

        @pl.when(i == 0)
        def _():
            state_s[...] = jnp.zeros_like(state_s)
        c, s = cos_ref[...], sin_ref[...]
        for h in range(RET_HEADS):
            st = state_s[h]
            stb = st.astype(BF16)
            st_ref[h] = stb
            q, k, qb, kb, v, scb, o = _ret_block_fwd(p_ref, h, c, s, d_ref, qd_ref, kd_ref, stb)
            kdk = (k * kd_ref[h]).astype(BF16)
            state_s[h] = st * bd_ref[h][:, :1] + _dot_tn(kdk, v)
            gate = p_ref[:, 4096 + h * RET_DV:4096 + (h + 1) * RET_DV].astype(F32)
            mu = jnp.mean(o, axis=-1, keepdims=True)
            oc = o - mu
            xh = oc * lax.rsqrt(jnp.mean(oc * oc, axis=-1, keepdims=True) + EPS)
            y = (gate * _sigmoid(gate)) * (xh * gng_ref[:, h * RET_DV:(h + 1) * RET_DV])
            y_ref[:, h * RET_DV:(h + 1) * RET_DV] = y.astype(BF16)

    const = lambda b, i: (0, 0, 0)
    return pl.pallas_call(
        body, name="ret_fwd", grid=(B, nb),
        in_specs=[pl.BlockSpec((LB, 6144), lambda b, i: (b * nb + i, 0)),
                  pl.BlockSpec((LB, 128), lambda b, i: (i, 0)), pl.BlockSpec((LB, 128), lambda b, i: (i, 0)),
                  pl.BlockSpec((RET_HEADS, LB, LB), const), pl.BlockSpec((RET_HEADS, LB, 1), const),
                  pl.BlockSpec((RET_HEADS, LB, 1), const), pl.BlockSpec((RET_HEADS, 1, 128), const),
                  pl.BlockSpec((1, 2048), lambda b, i: (0, 0))],
        out_specs=[pl.BlockSpec((LB, 2048), lambda b, i: (b * nb + i, 0)),
                   pl.BlockSpec((None, None, RET_HEADS, RET_DK, RET_DV), lambda b, i: (b, i, 0, 0, 0))],
        out_shape=[jax.ShapeDtypeStruct((T, 2048), BF16), jax.ShapeDtypeStruct((B, nb, RET_HEADS, RET_DK, RET_DV), BF16)],
        scratch_shapes=[pltpu.VMEM((RET_HEADS, RET_DK, RET_DV), F32)], compiler_params=_params(2),
    )(proj, cos, sin, dmat, qd, kd, bd, gn_g.reshape(1, 2048))


def _ret_bwd(proj, dy, states, gn_g, consts, *, B, S):
    LB = RET_BLOCK
    nb = S // LB
    T = B * S
    dmat, qd, kd, bd, cos, sin = consts

    def body(p_ref, dy_ref, st_ref, cos_ref, sin_ref, d_ref, qd_ref, kd_ref, bd_ref, gng_ref, dp_ref, dgn_ref, dstate_s):
        b, i = pl.program_id(0), pl.program_id(1)

        @pl.when(i == 0)
        def _():
            dstate_s[...] = jnp.zeros_like(dstate_s)

        @pl.when((b == 0) & (i == 0))
        def _():
            dgn_ref[...] = jnp.zeros_like(dgn_ref)
        c, s = cos_ref[...], sin_ref[...]
        for h in range(RET_HEADS):
            vs = slice(h * RET_DV, (h + 1) * RET_DV)
            stb = st_ref[h]
            q, k, qb, kb, v, scb, o = _ret_block_fwd(p_ref, h, c, s, d_ref, qd_ref, kd_ref, stb)
            gate = p_ref[:, 4096 + h * RET_DV:4096 + (h + 1) * RET_DV].astype(F32)
            mu = jnp.mean(o, axis=-1, keepdims=True)
            oc = o - mu
            rstd = lax.rsqrt(jnp.mean(oc * oc, axis=-1, keepdims=True) + EPS)
            xh = oc * rstd
            gng = gng_ref[:, vs]
            dyh = dy_ref[:, vs].astype(F32)
            sg = _sigmoid(gate)
            silu = gate * sg
            dgn_ref[:, vs] += jnp.sum(dyh * silu * xh, axis=0, keepdims=True)
            dxh = dyh * silu * gng
            do = rstd * (dxh - jnp.mean(dxh, axis=-1, keepdims=True) - xh * jnp.mean(dxh * xh, axis=-1, keepdims=True))
            dgate = dyh * xh * gng * (sg * (1.0 + gate * (1.0 - sg)))
            dob = do.astype(BF16)
            dsb = (_dot_nt(dob, v) * d_ref[h]).astype(BF16)
            dst = dstate_s[h]
            dstb = dst.astype(BF16)
            kdk = (k * kd_ref[h]).astype(BF16)
            dqr = _dot(dsb, kb) + qd_ref[h] * _dot_nt(dob, stb)
            dkr = _dot_tn(dsb, qb) + kd_ref[h] * _dot_nt(v, dstb)
            dv = _dot_tn(scb, dob) + _dot(kdk, dstb)
            dstate_s[h] = dst * bd_ref[h][:, :1] + _dot_tn((q * qd_ref[h]).astype(BF16), dob)
            dp_ref[:, h * RET_DK:(h + 1) * RET_DK] = _rope_inv(dqr, c, s).astype(BF16)
            dp_ref[:, 1024 + h * RET_DK:1024 + (h + 1) * RET_DK] = (_rope_inv(dkr, c, s) * (RET_DK ** -0.5)).astype(BF16)
            dp_ref[:, 2048 + h * RET_DV:2048 + (h + 1) * RET_DV] = dv.astype(BF16)
            dp_ref[:, 4096 + h * RET_DV:4096 + (h + 1) * RET_DV] = dgate.astype(BF16)

    const = lambda b, i: (0, 0, 0)
    rev = lambda b, i: (b * nb + nb - 1 - i, 0)
    return pl.pallas_call(
        body, name="ret_bwd", grid=(B, nb),
        in_specs=[pl.BlockSpec((LB, 6144), rev), pl.BlockSpec((LB, 2048), rev),
                  pl.BlockSpec((None, None, RET_HEADS, RET_DK, RET_DV), lambda b, i: (b, nb - 1 - i, 0, 0, 0)),
                  pl.BlockSpec((LB, 128), lambda b, i: (nb - 1 - i, 0)), pl.BlockSpec((LB, 128), lambda b, i: (nb - 1 - i, 0)),
                  pl.BlockSpec((RET_HEADS, LB, LB), const), pl.BlockSpec((RET_HEADS, LB, 1), const),
                  pl.BlockSpec((RET_HEADS, LB, 1), const), pl.BlockSpec((RET_HEADS, 1, 128), const),
                  pl.BlockSpec((1, 2048), lambda b, i: (0, 0))],
        out_specs=[pl.BlockSpec((LB, 6144), rev), pl.BlockSpec((1, 2048), lambda b, i: (0, 0))],
        out_shape=[jax.ShapeDtypeStruct((T, 6144), BF16), jax.ShapeDtypeStruct((1, 2048), F32)],
        scratch_shapes=[pltpu.VMEM((RET_HEADS, RET_DK, RET_DV), F32)], compiler_params=_params(2),
    )(proj, dy, states, cos, sin, dmat, qd, kd, bd, gn_g.reshape(1, 2048))


BIAS_LANES = 4 * ATT_BLOCK


def _diag_onehot():
    r = lax.broadcasted_iota(jnp.int32, (REL_TABLE, BIAS_LANES), 0)
    j = lax.broadcasted_iota(jnp.int32, (REL_TABLE, BIAS_LANES), 1)
    idx = jnp.maximum(j - ATT_BLOCK - PAST, -MAX_REL) + MAX_REL
    return jnp.where(idx == r, 1.0, 0.0).astype(F32)


def _row_is(j):
    return lax.broadcasted_iota(jnp.int32, (8, BIAS_LANES), 0) == j


def _att_bias(table):
    QB, KW = ATT_BLOCK, 3 * ATT_BLOCK

    def body(t_ref, b_ref, bt_ref):
        row = jnp.broadcast_to(t_ref[...], (8, REL_TABLE))
        diag = jnp.dot(row, _diag_onehot(), preferred_element_type=F32, precision=lax.Precision.HIGHEST)
        rows = jnp.zeros((8, BIAS_LANES), F32)
        for j in range(8):
            rows = jnp.where(_row_is(j), diag if j == 0 else pltpu.roll(diag, j, axis=1), rows)
        n = 8
        while n < QB:
            rows = jnp.concatenate([rows, pltpu.roll(rows, n, axis=1)], axis=0)
            n *= 2
        bias = rows[:, QB:]
        qi = lax.broadcasted_iota(jnp.int32, (QB, KW), 0)
        kj = lax.broadcasted_iota(jnp.int32, (QB, KW), 1)
        lo = (qi // CHUNK) * CHUNK
        bias = jnp.where((kj >= lo) & (kj < lo + PAST + CHUNK), bias, NEG)
        b_ref[...] = bias
        bt_ref[...] = bias.T

    return pl.pallas_call(
        body, name="att_bias", grid=(ATT_HEADS,),
        in_specs=[pl.BlockSpec((None, 1, REL_TABLE), lambda h: (h, 0, 0))],
        out_specs=[pl.BlockSpec((None, QB, KW), lambda h: (h, 0, 0)), pl.BlockSpec((None, KW, QB), lambda h: (h // 2, 0, h % 2))],
        out_shape=[jax.ShapeDtypeStruct((ATT_HEADS, QB, KW), F32), jax.ShapeDtypeStruct((ATT_HEADS // 2, KW, 2 * QB), F32)],
        compiler_params=_params(1),
    )(table.reshape(ATT_HEADS, 1, REL_TABLE))


def _att_bias_grad(dbias_t):
    QB, KW = ATT_BLOCK, 3 * ATT_BLOCK

    def body(d_ref, o_ref):
        rows = jnp.concatenate([jnp.zeros((QB, QB), F32), d_ref[...].T], axis=1)
        n = QB // 2
        while n >= 8:
            rows = rows[:n] + pltpu.roll(rows[n:], BIAS_LANES - n, axis=1)
            n //= 2
        acc = jnp.zeros((8, BIAS_LANES), F32)
        for j in range(8):
            acc = acc + jnp.where(_row_is(j), rows if j == 0 else pltpu.roll(rows, BIAS_LANES - j, axis=1), 0.0)
        diag = jnp.broadcast_to(jnp.sum(acc, axis=0, keepdims=True), (8, BIAS_LANES))
        grad = lax.dot_general(diag, _diag_onehot(), (((1,), (1,)), ((), ())), preferred_element_type=F32,
                               precision=lax.Precision.HIGHEST)
        o_ref[...] = grad[:1]

    return pl.pallas_call(
        body, name="att_bias_grad", grid=(ATT_HEADS,),
        in_specs=[pl.BlockSpec((None, KW, QB), lambda h: (h // 2, 0, h % 2))],
        out_specs=pl.BlockSpec((None, 1, REL_TABLE), lambda h: (h, 0, 0)),
        out_shape=jax.ShapeDtypeStruct((ATT_HEADS, 1, REL_TABLE), F32), compiler_params=_params(1),
    )(dbias_t).reshape(ATT_HEADS, REL_TABLE)


def _att_fwd(qkv, bias, *, B, S):
    QB = ATT_BLOCK
    nb = S // QB
    KW = 3 * QB
    T = B * S
    scale = ATT_DH ** -0.5

    def body(q_ref, k0, k1, k2, v0, v1, v2, b_ref, o_ref, lse_ref):
        i = pl.program_id(2)
        k3 = jnp.concatenate([k0[...], k1[...], k2[...]], axis=0)
        v3 = jnp.concatenate([v0[...], v1[...], v2[...]], axis=0)
        TQ = QB
        col = lax.broadcasted_iota(jnp.int32, (TQ, KW), 1)
        in_seq = col >= (2 - i) * QB
        lane = lax.broadcasted_iota(jnp.int32, (TQ, 128), 1)
        lane_kv = lax.broadcasted_iota(jnp.int32, (KW, 128), 1)
        vh = [jnp.where((lane_kv < ATT_DH) == (hh == 0), v3, jnp.zeros_like(v3)) for hh in range(2)]
        outs, lses = [], []
        for r in range(QB // TQ):
            rows = slice(r * TQ, (r + 1) * TQ)
            q = (q_ref[rows, :].astype(F32) * scale).astype(BF16)
            out = jnp.zeros((TQ, 128), F32)
            lse = jnp.zeros((TQ, 128), F32)
            for hh in range(2):
                qh = jnp.where((lane < ATT_DH) == (hh == 0), q, jnp.zeros_like(q))
                s = jnp.where(in_seq, _dot_nt(qh, k3) + b_ref[hh, rows, :], NEG)
                m = jnp.max(s, axis=-1, keepdims=True)
                e = jnp.exp(s - m)
                l = jnp.sum(e, axis=-1, keepdims=True)
                out = out + _dot(e.astype(BF16), vh[hh]) / l
                lse = jnp.where(lane == hh, m + jnp.log(l), lse)
            outs.append(out)
            lses.append(lse)
        o_ref[...] = jnp.concatenate(outs, axis=0).astype(BF16)
        lse_ref[...] = jnp.concatenate(lses, axis=0).T[:8]

    def kv(d, col0):
        return pl.BlockSpec((QB, 128), lambda hp, b, i: (b * nb + jnp.maximum(i - d, 0), col0 + hp))

    return pl.pallas_call(
        body, name="att_fwd", grid=(8, B, nb),
        in_specs=[pl.BlockSpec((QB, 128), lambda hp, b, i: (b * nb + i, hp)),
                  kv(2, 8), kv(1, 8), kv(0, 8), kv(2, 16), kv(1, 16), kv(0, 16),
                  pl.BlockSpec((2, QB, KW), lambda hp, b, i: (hp, 0, 0))],
        out_specs=[pl.BlockSpec((QB, 128), lambda hp, b, i: (b * nb + i, hp)),
                   pl.BlockSpec((None, 8, QB), lambda hp, b, i: (hp, 0, b * nb + i))],
        out_shape=[jax.ShapeDtypeStruct((T, 1024), BF16), jax.ShapeDtypeStruct((8, 8, T), F32)], compiler_params=_params(3),
    )(qkv, qkv, qkv, qkv, qkv, qkv, qkv, bias)


def _att_bwd(qkv, do, o, lse, bias_t, *, B, S):
    QB = ATT_BLOCK
    nb = S // QB
    KW = 3 * QB
    T = B * S
    scale = ATT_DH ** -0.5
    TK = 256

    def body(q_ref, k0, k1, k2, v0, v1, v2, do_ref, o_ref, lse_ref, b_ref, dq_ref, dk_ref, dv_ref, db_ref, dk_acc, dv_acc):
        b, i = pl.program_id(1), pl.program_id(2)

        @pl.when(i == 0)
        def _():
            dk_acc[...] = jnp.zeros_like(dk_acc)
            dv_acc[...] = jnp.zeros_like(dv_acc)

        @pl.when((b == 0) & (i == 0))
        def _():
            db_ref[...] = jnp.zeros_like(db_ref)

        @pl.when(i < nb)
        def _():
            lane = lax.broadcasted_iota(jnp.int32, (QB, 128), 1)
            first = lane < ATT_DH

            def by_head(x):
                zero = jnp.zeros_like(x)
                return jnp.concatenate([jnp.where(first, x, zero), jnp.where(first, zero, x)], axis=0)

            dout = do_ref[...]
            q2 = by_head((q_ref[...].astype(F32) * scale).astype(BF16))
            do2 = by_head(dout)
            delta_t = (o_ref[...].astype(F32) * dout.astype(F32)).T
            delta2 = jnp.concatenate([jnp.sum(delta_t[:ATT_DH], axis=0, keepdims=True),
                                      jnp.sum(delta_t[ATT_DH:], axis=0, keepdims=True)], axis=1)
            lse2 = jnp.concatenate([lse_ref[0:1, :], lse_ref[1:2, :]], axis=1)
            dq_t = jnp.zeros((128, 2 * QB), F32)
            for d, (k_ref, v_ref) in enumerate(((k0, v0), (k1, v1), (k2, v2))):
                kblk, vblk = k_ref[...], v_ref[...]
                kt = kblk.astype(F32).T.astype(BF16)
                lse_d = jnp.where(i + d >= 2, lse2, -NEG)
                slot = (i + 1 + d) % 3
                for t in range(QB // TK):
                    rows = slice(t * TK, (t + 1) * TK)
                    wrows = slice(d * QB + t * TK, d * QB + (t + 1) * TK)
                    p = jnp.exp(_dot_nt(kblk[rows], q2) + b_ref[wrows, :] - lse_d)
                    ds = p * (_dot_nt(vblk[rows], do2) - delta2)
                    db_ref[wrows, :] += ds
                    dsb = ds.astype(BF16)
                    dk_acc[slot, rows, :] += _dot(dsb, q2)
                    dv_acc[slot, rows, :] += _dot(p.astype(BF16), do2)
                    dq_t += _dot(kt[:, rows], dsb)
            row = lax.broadcasted_iota(jnp.int32, (128, QB), 0)
            dq_ref[...] = (jnp.where(row < ATT_DH, dq_t[:, :QB], dq_t[:, QB:]) * scale).T.astype(BF16)

        @pl.when(i >= 2)
        def _():
            slot = (i + 1) % 3
            dk_ref[...] = dk_acc[slot].astype(BF16)
            dv_ref[...] = dv_acc[slot].astype(BF16)
            dk_acc[slot] = jnp.zeros((QB, 128), F32)
            dv_acc[slot] = jnp.zeros((QB, 128), F32)

    def qrow(b, i):
        return b * nb + jnp.minimum(i, nb - 1)

    def kv(d, col0):
        return pl.BlockSpec((QB, 128), lambda hp, b, i: (b * nb + jnp.maximum(jnp.minimum(i, nb - 1) - d, 0), col0 + hp))

    late = pl.BlockSpec((QB, 128), lambda hp, b, i: (b * nb + jnp.maximum(i - 2, 0), hp))
    return pl.pallas_call(
        body, name="att_bwd", grid=(8, B, nb + 2),
        in_specs=[pl.BlockSpec((QB, 128), lambda hp, b, i: (qrow(b, i), hp)),
                  kv(2, 8), kv(1, 8), kv(0, 8), kv(2, 16), kv(1, 16), kv(0, 16),
                  pl.BlockSpec((QB, 128), lambda hp, b, i: (qrow(b, i), hp)),
                  pl.BlockSpec((QB, 128), lambda hp, b, i: (qrow(b, i), hp)),
                  pl.BlockSpec((None, 8, QB), lambda hp, b, i: (hp, 0, qrow(b, i))),
                  pl.BlockSpec((None, KW, 2 * QB), lambda hp, b, i: (hp, 0, 0))],
        out_specs=[pl.BlockSpec((QB, 128), lambda hp, b, i: (qrow(b, i), hp)), late, late,
                   pl.BlockSpec((None, KW, 2 * QB), lambda hp, b, i: (hp, 0, 0))],
        out_shape=[jax.ShapeDtypeStruct((T, 1024), BF16)] * 3 + [jax.ShapeDtypeStruct((ATT_HEADS // 2, KW, 2 * QB), F32)],
        scratch_shapes=[pltpu.VMEM((3, QB, 128), F32), pltpu.VMEM((3, QB, 128), F32)], compiler_params=_params(3),
    )(qkv, qkv, qkv, qkv, qkv, qkv, qkv, do, o, lse, bias_t)


def _tok_tile(T, want):
    t = min(T, want)
    assert T % t == 0
    return t


def _step(x, tgt, slab, slab_rel, place, mix_g, gn_g, mlp_g, fin_g):
    B, S, D = x.shape
    T = B * S
    h0 = x.reshape(T, D)
    tgt = tgt.reshape(T, D)
    tm = _tok_tile(T, 1024)
    tb = _tok_tile(T, 512)
    tq = _tok_tile(T, 256)
    tt = _tok_tile(T, 8192)
    tf = _tok_tile(T, 2048)
    consts = _ret_consts(S, RET_BLOCK)
    w = {}

    (w["ret_w_in"],) = _all_gather_slabs([slab["ret_w_in"]])
    (proj, hn0), (w["ret_w_out"], w["mlp_w1_0"], w["mlp_w2_0"]) = _mm_nn(
        h0, w["ret_w_in"], "col", tm=tb, out_dtype=BF16, name="ret_in", norm_g=mix_g[0],
        carry=_gather_carry([slab["ret_w_out"], slab["mlp_w1_0"], slab["mlp_w2_0"]]))
    y_ret, states = _ret_fwd(proj, gn_g, consts, B=B, S=S)
    h1, (w["att_w_in"], w["att_w_out"], rel_slabs) = _mm_nn(
        y_ret, w["ret_w_out"], "row", tm=tm, out_dtype=F32, name="ret_out", res=h0,
        carry=_gather_carry([slab["att_w_in"], slab["att_w_out"], slab_rel]))
    (a0, hm0), (w["mlp_w1_1"],) = _mm_nn(h1, w["mlp_w1_0"], "col", tm=tb, out_dtype=BF16, name="mlp0_up", norm_g=mlp_g[0],
                                         act="relu", carry=_gather_carry([slab["mlp_w1_1"]]))
    h2, (w["mlp_w2_1"],) = _mm_nn(a0, w["mlp_w2_0"], "row", tm=tb, out_dtype=F32, name="mlp0_down", square=True, res=h1,
                                  carry=_gather_carry([slab["mlp_w2_1"]]))
    rel_bias = jnp.transpose(rel_slabs, (1, 0, 2)).reshape(ATT_HEADS, REL_TABLE)
    bias, bias_t = _att_bias(rel_bias)
    qkv, hn1 = _mm_nn(h2, w["att_w_in"], "col", tm=tb, out_dtype=BF16, name="att_in", norm_g=mix_g[1])
    o_att, lse = _att_fwd(qkv, bias, B=B, S=S)
    h3 = _mm_nn(o_att, w["att_w_out"], "row", tm=tm, out_dtype=F32, name="att_out", res=h2)
    a1, hm1 = _mm_nn(h3, w["mlp_w1_1"], "col", tm=tb, out_dtype=BF16, name="mlp1_up", norm_g=mlp_g[1], act="relu")
    dh4, d_fin_g, _, loss = _mm_nn(a1, w["mlp_w2_1"], "row", tm=tb, out_dtype=F32, name="mlp1_down", square=True, res=h3,
                                   loss=(tgt, fin_g))

    gw = {}
    gw["mlp_w2_1"] = _mm_tn(a1, dh4, "row", tt=tf, tk=1024, tn=D, name="d_mlp1_w2", square=True)
    dz1 = _mm_nt(dh4, w["mlp_w2_1"], "row", tm=tb, name="d_mlp1_act", epi="relu2bwd", a2=a1)
    gw["mlp_w1_1"] = _mm_tn(hm1, dz1, "col", tt=tt, tk=D, tn=256, name="d_mlp1_w1")
    dh3, d_mlp_g1 = _mm_nt(dz1, w["mlp_w1_1"], "col", tm=tb, name="d_mlp1_in", epi="normbwd",
                           h=h3, g=mlp_g[1], dres=dh4)
    gw["att_w_out"] = _mm_tn(o_att, dh3, "row", tt=tf, tk=256, tn=D, name="d_att_wout")
    do_att = _mm_nt(dh3, w["att_w_out"], "row", tm=tb, name="d_att_o", epi="bf16")
    dq, dk, dv, dbias_t = _att_bwd(qkv, do_att, o_att, lse, bias_t, B=B, S=S)
    d_rel = _att_bias_grad(dbias_t)
    dqkv = (dq, dk, dv)
    gw["att_w_in"] = _mm_tn(hn1, dqkv, "col", tt=tt, tk=D, tn=256, name="d_att_win")
    sums, landed = {}, {}

    def swap_halves(names):
        return _sibling_carry([gw[n] for n in names])

    def add_halves(names, carried):
        for t, n in enumerate(names):
            sums[n] = _add_sibling(place, carried[t], carried[len(names) + t], name="chip_sum_" + n)

    def carried_exchange(names):
        return _chips_carry([sums[n][0] for n in names])

    def keep(names, carried):
        landed.update(zip(names, carried[len(names):]))

    layer1 = ["mlp_w1_1", "mlp_w2_1", "att_w_in", "att_w_out"]
    (dh2, d_mix_g1), carried = _mm_nt(dqkv, w["att_w_in"], "col", tm=tb, name="d_att_in", epi="normbwd",
                                      h=h2, g=mix_g[1], dres=dh3, carry=swap_halves(layer1))
    add_halves(layer1, carried)
    gw["mlp_w2_0"] = _mm_tn(a0, dh2, "row", tt=tf, tk=1024, tn=D, name="d_mlp0_w2", square=True)
    dz0, carried = _mm_nt(dh2, w["mlp_w2_0"], "row", tm=tb, name="d_mlp0_act", epi="relu2bwd", a2=a0,
                          carry=carried_exchange(["mlp_w1_1", "mlp_w2_1"]))
    keep(["mlp_w1_1", "mlp_w2_1"], carried)
    gw["mlp_w1_0"] = _mm_tn(hm0, dz0, "col", tt=tt, tk=D, tn=256, name="d_mlp0_w1")
    (dh1, d_mlp_g0), carried = _mm_nt(dz0, w["mlp_w1_0"], "col", tm=tb, name="d_mlp0_in", epi="normbwd",
                                      h=h1, g=mlp_g[0], dres=dh2,
                                      carry=_merge(carried_exchange(["att_w_in", "att_w_out"]), swap_halves(["mlp_w1_0", "mlp_w2_0"])))
    keep(["att_w_in", "att_w_out"], carried[:4])
    add_halves(["mlp_w1_0", "mlp_w2_0"], carried[4:])
    gw["ret_w_out"] = _mm_tn(y_ret, dh1, "row", tt=tf, tk=512, tn=D, name="d_ret_wout")
    dy_ret, carried = _mm_nt(dh1, w["ret_w_out"], "row", tm=tb, name="d_ret_y", epi="bf16", carry=swap_halves(["ret_w_out"]))
    add_halves(["ret_w_out"], carried)
    dproj, d_gn = _ret_bwd(proj, dy_ret, states, gn_g, consts, B=B, S=S)
    gw["ret_w_in"], carried = _mm_tn(hn0, dproj, "col", tt=tt, tk=D, tn=256, name="d_ret_win",
                                     carry=carried_exchange(["mlp_w1_0", "mlp_w2_0", "ret_w_out"]))
    keep(["mlp_w1_0", "mlp_w2_0", "ret_w_out"], carried)
    add_halves(["ret_w_in"], [gw["ret_w_in"]] + list(_exchange_with_sibling([gw["ret_w_in"]], "ret_in")))
    (dx, d_mix_g0), carried = _mm_nt(dproj, w["ret_w_in"], "col", tm=tq, name="d_ret_in", epi="normbwd",
                                     h=h0, g=mix_g[0], dres=dh1, carry=carried_exchange(["ret_w_in"]))
    keep(["ret_w_in"], carried)
    totals = {n: _add_chips(place, sums[n][1], landed[n], name="total_" + n) for n in BIG}
    small = [d_mix_g0, d_mix_g1, d_mlp_g0, d_mlp_g1, d_fin_g, d_gn.reshape(2, D), d_rel.reshape(5, D)]
    return loss.reshape(()), dx.reshape(B, S, D), totals, small


def _row_tile(r, want=256):
    t = min(r, want)
    assert r % t == 0
    return t


def _into_slab(place, a, layer, dtype, name):
    _, r, c = a.shape
    tr = _row_tile(r)

    def body(place_ref, a_ref, o_ref):
        o_ref[...] = a_ref[...].astype(dtype)

    grid_spec = pltpu.PrefetchScalarGridSpec(
        num_scalar_prefetch=1, grid=(r // tr,), in_specs=[pl.BlockSpec((None, tr, c), lambda i, pr: (layer, i, 0))],
        out_specs=pl.BlockSpec((None, tr, c), lambda i, pr: (pr[0], i, 0)),
    )
    return pl.pallas_call(
        body, name=name, grid_spec=grid_spec, out_shape=jax.ShapeDtypeStruct((N_CHIPS, r, c), dtype), compiler_params=_params(1),
    )(place, a)


def _add_sibling(place, g, recv, name):
    _, r, c = g.shape
    hr = r // 2
    tr = _row_tile(hr)
    nrt = hr // tr

    def body(place_ref, g_ref, r_ref, sb_ref, own_ref):
        v = g_ref[...] + r_ref[...]
        sb_ref[...] = v.astype(BF16)

        @pl.when(pl.program_id(1) == place_ref[0])
        def _():
            own_ref[...] = v

    grid_spec = pltpu.PrefetchScalarGridSpec(
        num_scalar_prefetch=1, grid=(nrt, N_CHIPS),
        in_specs=[pl.BlockSpec((None, tr, c), lambda i, s, pr: (s, pr[1] * nrt + i, 0)),
                  pl.BlockSpec((None, tr, c), lambda i, s, pr: (s, i, 0))],
        out_specs=[pl.BlockSpec((None, tr, c), lambda i, s, pr: (s, i, 0)), pl.BlockSpec((tr, c), lambda i, s, pr: (i, 0))],
    )
    return pl.pallas_call(
        body, name=name, grid_spec=grid_spec,
        out_shape=[jax.ShapeDtypeStruct((N_CHIPS, hr, c), BF16), jax.ShapeDtypeStruct((hr, c), F32)],
        compiler_params=_params(2),
    )(place, g, recv)


def _add_chips(place, own, recv, name):
    hr, c = own.shape
    tr = _row_tile(hr)
    nrt = hr // tr

    def body(place_ref, o_ref, r_ref, t_ref):
        t_ref[...] = ((o_ref[...] + r_ref[0].astype(F32)) + r_ref[1].astype(F32)) + r_ref[2].astype(F32)

    grid_spec = pltpu.PrefetchScalarGridSpec(
        num_scalar_prefetch=1, grid=(nrt,),
        in_specs=[pl.BlockSpec((tr, c), lambda i, pr: (i, 0)), pl.BlockSpec((3, tr, c), lambda i, pr: (0, i, 0))],
        out_specs=pl.BlockSpec((tr, c), lambda i, pr: (pr[1] * nrt + i, 0)),
    )
    return pl.pallas_call(
        body, name=name, grid_spec=grid_spec, out_shape=jax.ShapeDtypeStruct((2 * hr, c), F32), compiler_params=_params(1),
    )(place, own, recv)


def _adamw(w, gs, m, v, name):
    L, r, c = w.shape
    tr = _row_tile(r)
    assert len(gs) == L

    def body(*refs):
        w_ref, m_ref, v_ref = refs[:3]
        g_refs = refs[3:3 + L]
        go_ref, d_ref, nm_ref, nv_ref = refs[3 + L:]
        gg = g_refs[0][...]
        for k in range(1, L):
            gg = jnp.where(pl.program_id(0) == k, g_refs[k][...], gg)
        go_ref[...] = gg
        nm = ADAM_B1 * m_ref[...] + (1.0 - ADAM_B1) * gg
        nv = ADAM_B2 * v_ref[...] + (1.0 - ADAM_B2) * (gg * gg)
        m_hat = nm / (1.0 - ADAM_B1 ** ADAM_STEP)
        v_hat = nv / (1.0 - ADAM_B2 ** ADAM_STEP)
        d_ref[...] = -ADAM_LR * (m_hat / (jnp.sqrt(v_hat) + ADAM_EPS) + ADAM_WD * w_ref[...])
        nm_ref[...] = nm
        nv_ref[...] = nv

    spec = pl.BlockSpec((None, tr, c), lambda l, i: (l, i, 0))
    return pl.pallas_call(
        body, name=name, grid=(L, r // tr), in_specs=[spec] * 3 + [pl.BlockSpec((tr, c), lambda l, i: (i, 0))] * L,
        out_specs=[spec] * 4, out_shape=[jax.ShapeDtypeStruct((L, r, c), F32)] * 4, compiler_params=_params(2),
    )(w, m, v, *gs)


def _place():
    return lax.axis_index("x"), lax.axis_index("y"), lax.axis_index("c")


def _other_chips(x, y):
    return [(1 - x, y), (x, 1 - y), (1 - x, 1 - y)]


def _remote(src, dst, ssem, rsem, dev):
    return pltpu.make_async_remote_copy(src_ref=src, dst_ref=dst, send_sem=ssem, recv_sem=rsem, device_id=dev,
                                        device_id_type=MESH)


def _gather_phases(n):
    def geometry(refs, t):
        x, y, c = _place()
        hr = refs[t].shape[1] // 2
        chips = _other_chips(x, y)
        return x, y, c, 2 * x + y, chips, [2 * qx + qy for qx, qy in chips], pl.ds(c * hr, hr), pl.ds((1 - c) * hr, hr)

    def send(refs, sems):
        s1, r1, _, _ = sems
        for t in range(n):
            x, y, c, p, chips, cidx, mine, theirs = geometry(refs, t)
            for j, (qx, qy) in enumerate(chips):
                _remote(refs[t].at[p, mine], refs[t].at[p, mine], s1.at[t, j], r1.at[t, j], (qx, qy, c)).start()

    def pass_on(refs, sems):
        s1, r1, s2, r2 = sems
        for t in range(n):
            x, y, c, p, chips, cidx, mine, theirs = geometry(refs, t)
            for j, (qx, qy) in enumerate(chips):
                got = refs[t].at[cidx[j], mine]
                _remote(got, got, s1.at[t, j], r1.at[t, j], (qx, qy, c)).wait_recv()
                _remote(got, got, s2.at[t, j], r2.at[t, j], (x, y, 1 - c)).start()

    def finish(refs, sems):
        s1, r1, s2, r2 = sems
        for t in range(n):
            x, y, c, p, chips, cidx, mine, theirs = geometry(refs, t)
            for j, (qx, qy) in enumerate(chips):
                got = refs[t].at[cidx[j], theirs]
                _remote(got, got, s2.at[t, j], r2.at[t, j], (x, y, 1 - c)).wait_recv()
        for t in range(n):
            x, y, c, p, chips, cidx, mine, theirs = geometry(refs, t)
            for j, (qx, qy) in enumerate(chips):
                _remote(refs[t].at[p, mine], refs[t].at[p, mine], s1.at[t, j], r1.at[t, j], (qx, qy, c)).wait_send()
                sent = refs[t].at[cidx[j], mine]
                _remote(sent, sent, s2.at[t, j], r2.at[t, j], (x, y, 1 - c)).wait_send()

    sem = pltpu.SemaphoreType.DMA
    return send, pass_on, finish, [sem((n, 3)), sem((n, 3)), sem((n, 3)), sem((n, 3))]


def _gather_carry(slabs):
    send, pass_on, finish, sems = _gather_phases(len(slabs))
    return _Carry(slabs, sems, [(0, send), (-2, pass_on), (-1, finish)])


def _all_gather_slabs(slabs):
    n = len(slabs)
    send, pass_on, finish, sems = _gather_phases(n)

    def body(*refs):
        outs, scratch = refs[n:2 * n], refs[2 * n:]
        send(outs, scratch)
        pass_on(outs, scratch)
        finish(outs, scratch)

    return pl.pallas_call(
        body, name="gather_weights", in_specs=[HBM_SPEC] * n, out_specs=[HBM_SPEC] * n,
        out_shape=[jax.ShapeDtypeStruct(s.shape, s.dtype) for s in slabs], input_output_aliases={t: t for t in range(n)},
        scratch_shapes=sems, compiler_params=pltpu.CompilerParams(has_side_effects=True),
    )(*slabs)


def _sibling_phases(n):
    def copies(refs, sems):
        ssem, rsem = sems
        x, y, c = _place()
        out = []
        for t in range(n):
            hr = refs[t].shape[1] // 2
            out.append(_remote(refs[t].at[:, pl.ds((1 - c) * hr, hr), :], refs[n + t], ssem.at[t], rsem.at[t], (x, y, 1 - c)))
        return out

    def send(refs, sems):
        for cp in copies(refs, sems):
            cp.start()

    def finish(refs, sems):
        for cp in copies(refs, sems):
            cp.wait()

    sem = pltpu.SemaphoreType.DMA
    return send, finish, [sem((n,)), sem((n,))]


def _halves_landing(grads):
    return [lax.empty((N_CHIPS, g.shape[1] // 2, g.shape[2]), g.dtype) for g in grads]


def _sibling_carry(grads):
    send, finish, sems = _sibling_phases(len(grads))
    return _Carry(list(grads) + _halves_landing(grads), sems, [(0, send), (-1, finish)])


def _exchange_with_sibling(grads, tag):
    n = len(grads)
    send, finish, sems = _sibling_phases(n)

    def body(*refs):
        both, scratch = refs[:2 * n], refs[2 * n:]
        send(both, scratch)
        finish(both, scratch)

    return pl.pallas_call(
        body, name="grads_to_sibling_" + tag, in_specs=[HBM_SPEC] * n, out_specs=[HBM_SPEC] * n,
        out_shape=[jax.ShapeDtypeStruct((N_CHIPS, g.shape[1] // 2, g.shape[2]), g.dtype) for g in grads],
        scratch_shapes=sems, compiler_params=pltpu.CompilerParams(has_side_effects=True),
    )(*grads)


def _chips_phases(n):
    def copies(refs, sems):
        ssem, rsem = sems
        x, y, c = _place()
        return [_remote(refs[t].at[2 * qx + qy], refs[n + t].at[j], ssem.at[t, j], rsem.at[t, j], (qx, qy, c))
                for t in range(n) for j, (qx, qy) in enumerate(_other_chips(x, y))]

    def send(refs, sems):
        for cp in copies(refs, sems):
            cp.start()

    def finish(refs, sems):
        for cp in copies(refs, sems):
            cp.wait()

    sem = pltpu.SemaphoreType.DMA
    return send, finish, [sem((n, 3)), sem((n, 3))]


def _landing(sums):
    return [lax.empty((3,) + s.shape[1:], s.dtype) for s in sums]


def _chips_carry(sums):
    send, finish, sems = _chips_phases(len(sums))
    return _Carry(list(sums) + _landing(sums), sems, [(0, send), (-1, finish)])


def _share_with_sibling(shards):
    n = len(shards)

    def body(*refs):
        outs = refs[n:2 * n]
        ssem, rsem = refs[2 * n:]
        x, y, c = _place()
        copies = []
        for t in range(n):
            hr = outs[t].shape[0] // 2
            mine = outs[t].at[pl.ds(c * hr, hr)]
            cp = _remote(mine, mine, ssem.at[t], rsem.at[t], (x, y, 1 - c))
            cp.start()
            copies.append(cp)
        for t in range(n):
            hr = outs[t].shape[0] // 2
            theirs = outs[t].at[pl.ds((1 - c) * hr, hr)]
            _remote(theirs, theirs, ssem.at[t], rsem.at[t], (x, y, 1 - c)).wait_recv()
        for cp in copies:
            cp.wait_send()

    sem = pltpu.SemaphoreType.DMA
    return pl.pallas_call(
        body, name="grads_share", in_specs=[HBM_SPEC] * n, out_specs=[HBM_SPEC] * n,
        out_shape=[jax.ShapeDtypeStruct(s.shape, s.dtype) for s in shards], input_output_aliases={t: t for t in range(n)},
        scratch_shapes=[sem((n,)), sem((n,))], compiler_params=pltpu.CompilerParams(has_side_effects=True),
    )(*shards)


def _all_reduce_small(buf):
    R, C = buf.shape

    def body(in_ref, out_ref, gather, ssem, rsem):
        x, y, c = _place()
        me = 4 * x + 2 * y + c
        gather[me] = in_ref[...]
        flips = [(fx, fy, fc) for fx in (0, 1) for fy in (0, 1) for fc in (0, 1) if fx or fy or fc]
        peers = [(x + fx - 2 * x * fx, y + fy - 2 * y * fy, c + fc - 2 * c * fc) for fx, fy, fc in flips]
        copies = [_remote(in_ref, gather.at[me], ssem.at[k], rsem.at[k], peer) for k, peer in enumerate(peers)]
        for cp in copies:
            cp.start()
        for k, (px, py, pc) in enumerate(peers):
            _remote(in_ref, gather.at[4 * px + 2 * py + pc], ssem.at[k], rsem.at[k], (px, py, pc)).wait_recv()
        for cp in copies:
            cp.wait_send()
        acc = gather[0]
        for d in range(1, 8):
            acc = acc + gather[d]
        out_ref[...] = acc

    sem = pltpu.SemaphoreType.DMA
    vmem = pl.BlockSpec(memory_space=pltpu.VMEM)
    return pl.pallas_call(
        body, name="small_grads_sum", in_specs=[vmem], out_specs=vmem, out_shape=jax.ShapeDtypeStruct((R, C), F32),
        scratch_shapes=[pltpu.VMEM((8, R, C), F32), sem((7,)), sem((7,))],
        compiler_params=pltpu.CompilerParams(has_side_effects=True),
    )(buf)


BIG = ["ret_w_in", "ret_w_out", "att_w_in", "att_w_out", "mlp_w1_0", "mlp_w1_1", "mlp_w2_0", "mlp_w2_1"]
LAYER_OF = {"ret_w_in": ("ret_w_in", 0), "ret_w_out": ("ret_w_out", 0), "att_w_in": ("att_w_in", 0), "att_w_out": ("att_w_out", 0),
            "mlp_w1_0": ("mlp_w1", 0), "mlp_w1_1": ("mlp_w1", 1), "mlp_w2_0": ("mlp_w2", 0), "mlp_w2_1": ("mlp_w2", 1)}
ORDER = ["mix_norm_g", "ret_w_in", "ret_gn_g", "ret_w_out", "att_w_in", "att_rel_bias", "att_w_out", "mlp_norm_g", "mlp_w1", "mlp_w2",
         "final_norm_g"]


def kernel(x, mix_norm_g, ret_w_in, ret_gn_g, ret_w_out, att_w_in, att_rel_bias, att_w_out, mlp_norm_g, mlp_w1, mlp_w2, final_norm_g, loss_target, m_mix_norm_g, m_ret_w_in, m_ret_gn_g, m_ret_w_out, m_att_w_in, m_att_rel_bias, m_att_w_out, m_mlp_norm_g, m_mlp_w1, m_mlp_w2, m_final_norm_g, v_mix_norm_g, v_ret_w_in, v_ret_gn_g, v_ret_w_out, v_att_w_in, v_att_rel_bias, v_att_w_out, v_mlp_norm_g, v_mlp_w1, v_mlp_w2, v_final_norm_g):
    xi, yi, ci = _place()
    chip = 2 * xi + yi
    weights = dict(zip(ORDER, (mix_norm_g, ret_w_in, ret_gn_g, ret_w_out, att_w_in, att_rel_bias, att_w_out, mlp_norm_g, mlp_w1,
                               mlp_w2, final_norm_g)))
    first = dict(zip(ORDER, (m_mix_norm_g, m_ret_w_in, m_ret_gn_g, m_ret_w_out, m_att_w_in, m_att_rel_bias, m_att_w_out,
                             m_mlp_norm_g, m_mlp_w1, m_mlp_w2, m_final_norm_g)))
    second = dict(zip(ORDER, (v_mix_norm_g, v_ret_w_in, v_ret_gn_g, v_ret_w_out, v_att_w_in, v_att_rel_bias, v_att_w_out,
                              v_mlp_norm_g, v_mlp_w1, v_mlp_w2, v_final_norm_g)))

    place = jnp.stack([chip, ci]).astype(jnp.int32)
    slab = {n: _into_slab(place, weights[LAYER_OF[n][0]], LAYER_OF[n][1], BF16, name="cast_" + n) for n in BIG}
    slab_rel = _into_slab(place, att_rel_bias, 0, F32, name="slab_rel_bias")

    loss_local, grad_x, totals, small = _step(x, loss_target, slab, slab_rel, place, mix_norm_g, ret_gn_g[0], mlp_norm_g,
                                              final_norm_g)
    loss = lax.psum(loss_local, ("x", "y", "c"))

    g_big = dict(zip(BIG, _share_with_sibling([totals[n] for n in BIG])))

    rows, at = jnp.zeros((16, D_MODEL), F32), 0
    for part in small:
        rows = rows + jnp.pad(part, ((at, 16 - at - part.shape[0]), (0, 0)))
        at += part.shape[0]
    rows = _all_reduce_small(rows)
    grads = {"mix_norm_g": [rows[0:2]], "mlp_norm_g": [rows[2:4]], "final_norm_g": [rows[4:5]], "ret_gn_g": [rows[5:7].reshape(1, 2048)],
             "att_rel_bias": [lax.dynamic_slice_in_dim(rows[7:12].reshape(ATT_HEADS, REL_TABLE), chip * (REL_TABLE // N_CHIPS),
                                                       REL_TABLE // N_CHIPS, axis=1)]}
    for n in BIG:
        grads.setdefault(LAYER_OF[n][0], []).append(g_big[n])

    def as3(a):
        return a.reshape((1,) * (3 - a.ndim) + a.shape)

    results = {}
    for n in ORDER:
        outs = _adamw(as3(weights[n]), grads[n], as3(first[n]), as3(second[n]), name="adamw_" + n)
        results[n] = [o.reshape(weights[n].shape) for o in outs]
    return (loss, grad_x) + tuple(results[n][k] for k in range(4) for n in ORDER)
```

```python
import math

import jax
import jax.numpy as jnp
from jax import lax
from jax.experimental import pallas as pl
from jax.experimental.pallas import tpu as pltpu

F32 = jnp.float32
BF16 = jnp.bfloat16
MESH = pl.DeviceIdType.MESH

D_MODEL = 1024
CHUNK = 64
RET_HEADS = 4
RET_DK = 256
RET_DV = 512
ROPE_BASE = 10000.0
ATT_HEADS = 16
ATT_DH = 64
PAST = 512
MAX_REL = 256
REL_TABLE = MAX_REL + CHUNK
EPS = 1e-6
NEG = -1e30
N_CHIPS = 4

ADAM_LR = 0.001
ADAM_B1 = 0.9
ADAM_B2 = 0.999
ADAM_EPS = 1e-08
ADAM_WD = 0.01
ADAM_STEP = 10

RET_BLOCK = 256
ATT_BLOCK = 256
VMEM_LIMIT = 56 * 1024 * 1024


def _params(n_axes, **kw):
    return pltpu.CompilerParams(dimension_semantics=("arbitrary",) * n_axes, vmem_limit_bytes=VMEM_LIMIT, **kw)


def _dot(a, b):
    return jnp.dot(a, b, preferred_element_type=F32)


def _dot_nt(a, b):
    return lax.dot_general(a, b, (((1,), (1,)), ((), ())), preferred_element_type=F32)


def _dot_tn(a, b):
    return lax.dot_general(a, b, (((0,), (0,)), ((), ())), preferred_element_type=F32)


def _sigmoid(x):
    return 0.5 * jnp.tanh(0.5 * x) + 0.5


HBM_SPEC = pl.BlockSpec(memory_space=pltpu.HBM)


class _Carry:
    def __init__(self, arrays, sems, stages):
        self.arrays, self.sems, self.stages = list(arrays), list(sems), list(stages)


def _merge(a, b):
    na, sa = len(a.arrays), len(a.sems)

    def of_a(fn):
        return lambda refs, sems: fn(refs[:na], sems[:sa])

    def of_b(fn):
        return lambda refs, sems: fn(refs[na:], sems[sa:])

    return _Carry(a.arrays + b.arrays, a.sems + b.sems,
                  [(at, of_a(fn)) for at, fn in a.stages] + [(at, of_b(fn)) for at, fn in b.stages])


def _carry_call(body, carry, *, name, grid, in_specs, out_specs, out_shape, args):
    if carry is None:
        outs = pl.pallas_call(body, name=name, grid=grid, in_specs=in_specs, out_specs=out_specs, out_shape=out_shape,
                              compiler_params=_params(len(grid)))(*args)
        return list(outs), []
    n_in, n_out, n_c = len(in_specs), len(out_specs), len(carry.arrays)
    steps = 1
    for g in grid:
        steps *= g
    assert all(-steps <= at < steps for at, _ in carry.stages)

    def carrying(*refs):
        ins, outs = refs[:n_in], refs[n_in + n_c:n_in + n_c + n_out]
        carried = refs[n_in + n_c + n_out:n_in + 2 * n_c + n_out]
        sems = refs[n_in + 2 * n_c + n_out:]
        step = pl.program_id(0)
        for axis in range(1, len(grid)):
            step = step * grid[axis] + pl.program_id(axis)
        for at, fn in carry.stages:
            if at == 0:
                pl.when(step == 0)(lambda fn=fn: fn(carried, sems))
        body(*ins, *outs)
        for at, fn in carry.stages:
            if at != 0:
                pl.when(step == at % steps)(lambda fn=fn: fn(carried, sems))

    outs = pl.pallas_call(
        carrying, name=name, grid=grid, in_specs=list(in_specs) + [HBM_SPEC] * n_c, out_specs=list(out_specs) + [HBM_SPEC] * n_c,
        out_shape=list(out_shape) + [jax.ShapeDtypeStruct(a.shape, a.dtype) for a in carry.arrays],
        input_output_aliases={n_in + t: n_out + t for t in range(n_c)}, scratch_shapes=carry.sems,
        compiler_params=_params(len(grid), has_side_effects=True),
    )(*args, *carry.arrays)
    return list(outs[:n_out]), list(outs[n_out:])


def _mm_nn(a, w, wkind, *, tm, out_dtype, name, norm_g=None, act=None, square=False, res=None, loss=None, carry=None):
    M, K = a.shape
    cols = w.shape[2]
    N = N_CHIPS * cols if wkind == "col" else cols
    has_norm = norm_g is not None
    steps = M // tm
    assert M % tm == 0 and K == (w.shape[1] if wkind == "col" else N_CHIPS * w.shape[1])
    assert loss is None or (wkind == "row" and act is None)

    def body(*refs):
        it = iter(refs)
        a_ref, w_ref = next(it), next(it)
        g_ref = next(it) if has_norm else None
        r_ref = next(it) if res is not None else None
        if loss is not None:
            t_ref, fg_ref = next(it), next(it)
        o_ref = next(it)
        hn_ref = next(it) if has_norm else None
        if loss is not None:
            dg_ref, sq_ref, loss_ref = next(it), next(it), next(it)
        if has_norm:
            x = a_ref[...].astype(F32)
            r = lax.rsqrt(jnp.mean(x * x, axis=-1, keepdims=True) + EPS)
            lhs = (x * r * g_ref[...]).astype(BF16)
            hn_ref[...] = lhs
        elif square:
            lhs = a_ref[...].astype(BF16)
            lhs = lhs * lhs
        else:
            lhs = a_ref[...].astype(BF16)

        def loss_head(x):
            i = pl.program_id(0)
            r = lax.rsqrt(jnp.mean(x * x, axis=-1, keepdims=True) + EPS)
            xh = x * r
            gg = fg_ref[...]
            diff = xh * gg - t_ref[...]
            sq = jnp.sum(diff * diff, axis=0, keepdims=True)
            dy = diff * (1.0 / N)
            dg_part = jnp.sum(dy * xh, axis=0, keepdims=True)

            @pl.when(i == 0)
            def _():
                sq_ref[...] = sq
                dg_ref[...] = dg_part

            @pl.when(i > 0)
            def _():
                sq_ref[...] += sq
                dg_ref[...] += dg_part
            t = dy * gg
            o_ref[...] = r * (t - xh * jnp.mean(t * xh, axis=-1, keepdims=True))

            @pl.when(i == steps - 1)
            def _():
                loss_ref[...] = (0.5 / N) * jnp.sum(sq_ref[...], axis=1, keepdims=True)

        def finish(acc, sl):
            if act == "relu":
                acc = jnp.maximum(acc, 0.0)
            if r_ref is not None:
                acc = acc + r_ref[:, sl]
            if loss is not None:
                loss_head(acc)
            else:
                o_ref[:, sl] = acc.astype(out_dtype)

        if wkind == "col":
            for s in range(N_CHIPS):
                finish(_dot(lhs, w_ref[s]), slice(s * cols, (s + 1) * cols))
        else:
            finish(_dot(lhs, w_ref[...].reshape(K, N)), slice(None))

    in_specs = [pl.BlockSpec((tm, K), lambda i: (i, 0)), pl.BlockSpec(w.shape, lambda i: (0, 0, 0))]
    args = [a, w]
    if has_norm:
        in_specs.append(pl.BlockSpec((1, K), lambda i: (0, 0)))
        args.append(norm_g.reshape(1, K))
    if res is not None:
        in_specs.append(pl.BlockSpec((tm, N), lambda i: (i, 0)))
        args.append(res)
    if loss is not None:
        in_specs += [pl.BlockSpec((tm, N), lambda i: (i, 0)), pl.BlockSpec((1, N), lambda i: (0, 0))]
        args += [loss[0], loss[1].reshape(1, N)]
    out_shape = [jax.ShapeDtypeStruct((M, N), out_dtype)]
    out_specs = [pl.BlockSpec((tm, N), lambda i: (i, 0))]
    if has_norm:
        out_shape.append(jax.ShapeDtypeStruct((M, K), BF16))
        out_specs.append(pl.BlockSpec((tm, K), lambda i: (i, 0)))
    if loss is not None:
        out_shape += [jax.ShapeDtypeStruct((1, N), F32), jax.ShapeDtypeStruct((1, N), F32), jax.ShapeDtypeStruct((1, 1), F32)]
        out_specs += [pl.BlockSpec((1, N), lambda i: (0, 0)), pl.BlockSpec((1, N), lambda i: (0, 0)),
                      pl.BlockSpec((1, 1), lambda i: (0, 0))]
    outs, carried = _carry_call(body, carry, name=name, grid=(steps,), in_specs=in_specs, out_specs=out_specs,
                                out_shape=out_shape, args=args)
    result = outs if has_norm or loss is not None else outs[0]
    return result if carry is None else (result, carried)


def _mm_nt(a, w, wkind, *, tm, name, epi, a2=None, h=None, g=None, dres=None, carry=None):
    parts = tuple(a) if isinstance(a, (tuple, list)) else (a,)
    M, part_w = parts[0].shape
    Nw = part_w * len(parts)
    rows, cols = w.shape[1], w.shape[2]
    Kw = rows if wkind == "col" else N_CHIPS * rows
    assert M % tm == 0 and Nw == (N_CHIPS * cols if wkind == "col" else cols)
    assert epi != "normbwd" or wkind == "col"
    assert len(parts) == 1 or wkind == "col"
    chunk = math.gcd(part_w, cols)

    def body(*refs):
        it = iter(refs)
        a_refs = [next(it) for _ in parts]
        a_ref, w_ref = a_refs[0], next(it)
        a2_ref = next(it) if epi == "relu2bwd" else None
        if epi == "normbwd":
            h_ref, g_ref, dres_ref = next(it), next(it), next(it)
        o_ref = next(it)
        dg_ref = next(it) if epi == "normbwd" else None
        i = pl.program_id(0)

        def finish(acc, sl):
            if epi == "bf16":
                o_ref[:, sl] = acc.astype(BF16)
            elif epi == "relu2bwd":
                o_ref[:, sl] = (acc * (2.0 * a2_ref[:, sl].astype(F32))).astype(BF16)
            else:
                x = h_ref[...]
                r = lax.rsqrt(jnp.mean(x * x, axis=-1, keepdims=True) + EPS)
                xh = x * r
                dg_part = jnp.sum(acc * xh, axis=0, keepdims=True)

                @pl.when(i == 0)
                def _():
                    dg_ref[...] = dg_part

                @pl.when(i > 0)
                def _():
                    dg_ref[...] += dg_part
                t = acc * g_ref[...]
                dx = r * (t - xh * jnp.mean(t * xh, axis=-1, keepdims=True))
                o_ref[...] = dres_ref[...] + dx

        if wkind == "col":
            acc = None
            for lo in range(0, Nw, chunk):
                (src, a_lo), (s, w_lo) = divmod(lo, part_w), divmod(lo, cols)
                part = _dot_nt(a_refs[src][:, a_lo:a_lo + chunk].astype(BF16), w_ref[s, :, w_lo:w_lo + chunk])
                acc = part if acc is None else acc + part
            finish(acc, slice(None))
        else:
            lhs = a_ref[...].astype(BF16)
            for s in range(N_CHIPS):
                finish(_dot_nt(lhs, w_ref[s]), slice(s * rows, (s + 1) * rows))

    in_specs = [pl.BlockSpec((tm, part_w), lambda i: (i, 0)) for _ in parts] + [pl.BlockSpec(w.shape, lambda i: (0, 0, 0))]
    args = [*parts, w]
    out_dtype = BF16
    if epi == "relu2bwd":
        in_specs.append(pl.BlockSpec((tm, Kw), lambda i: (i, 0)))
        args.append(a2)
    if epi == "normbwd":
        in_specs += [pl.BlockSpec((tm, Kw), lambda i: (i, 0)), pl.BlockSpec((1, Kw), lambda i: (0, 0)),
                     pl.BlockSpec((tm, Kw), lambda i: (i, 0))]
        args += [h, g.reshape(1, Kw), dres]
        out_dtype = F32
    out_shape = [jax.ShapeDtypeStruct((M, Kw), out_dtype)]
    out_specs = [pl.BlockSpec((tm, Kw), lambda i: (i, 0))]
    if epi == "normbwd":
        out_shape.append(jax.ShapeDtypeStruct((1, Kw), F32))
        out_specs.append(pl.BlockSpec((1, Kw), lambda i: (0, 0)))
    outs, carried = _carry_call(body, carry, name=name, grid=(M // tm,), in_specs=in_specs, out_specs=out_specs,
                                out_shape=out_shape, args=args)
    result = outs if epi == "normbwd" else outs[0]
    return result if carry is None else (result, carried)


def _mm_tn(a, b, okind, *, tt, tk, tn, name, square=False, carry=None):
    parts = tuple(b) if isinstance(b, (tuple, list)) else (b,)
    T, K = a.shape
    part_w = parts[0].shape[1]
    N = part_w * len(parts)
    assert T % tt == 0 and K % tk == 0 and part_w % tn == 0
    nt = T // tt
    per_part = part_w // tn
    if okind == "col":
        per = (N // N_CHIPS) // tn
        assert (N // N_CHIPS) % tn == 0
        out_shape = jax.ShapeDtypeStruct((N_CHIPS, K, N // N_CHIPS), F32)
        out_spec = pl.BlockSpec((None, tk, tn), lambda ki, nj, t: (nj // per, ki, nj % per))
    else:
        per = (K // N_CHIPS) // tk
        assert (K // N_CHIPS) % tk == 0
        out_shape = jax.ShapeDtypeStruct((N_CHIPS, K // N_CHIPS, N), F32)
        out_spec = pl.BlockSpec((None, tk, tn), lambda ki, nj, t: (ki // per, ki % per, nj))

    def body(a_ref, *refs):
        b_refs, o_ref = refs[:-1], refs[-1]
        nj, t = pl.program_id(1), pl.program_id(2)

        def accumulate(b_ref):
            lhs = a_ref[...].astype(BF16)
            part = _dot_tn(lhs * lhs if square else lhs, b_ref[...].astype(BF16))
            if nt == 1:
                o_ref[...] = part
            else:
                @pl.when(t == 0)
                def _():
                    o_ref[...] = part

                @pl.when(t > 0)
                def _():
                    o_ref[...] += part

        if len(parts) == 1:
            accumulate(b_refs[0])
        else:
            for s, b_ref in enumerate(b_refs):
                pl.when(nj // per_part == s)(lambda b_ref=b_ref: accumulate(b_ref))

    def b_spec(s):
        return pl.BlockSpec((tt, tn), lambda ki, nj, t: (t, jnp.clip(nj - s * per_part, 0, per_part - 1)))

    outs, carried = _carry_call(
        body, carry, name=name, grid=(K // tk, N // tn, nt),
        in_specs=[pl.BlockSpec((tt, tk), lambda ki, nj, t: (t, ki))] + [b_spec(s) for s in range(len(parts))],
        out_specs=[out_spec], out_shape=[out_shape], args=[a, *parts])
    return outs[0] if carry is None else (outs[0], carried)


def _ret_consts(S, LB):
    log_gamma = jnp.log1p(-jnp.exp2(-5.0 - jnp.arange(RET_HEADS, dtype=F32)))
    idx = jnp.arange(LB, dtype=F32)
    n, m = idx[:, None], idx[None, :]
    cn, cm = jnp.floor(n / CHUNK), jnp.floor(m / CHUNK)
    dist = jnp.where(cm == cn, jnp.abs(n - m), n - m)
    dmat = jnp.where((cm <= cn)[None], jnp.exp(log_gamma[:, None, None] * dist[None]), 0.0)
    qd = jnp.exp(log_gamma[:, None] * (idx + 1.0)[None, :])[..., None]
    kd = jnp.exp(log_gamma[:, None] * (LB - 1 - idx)[None, :])[..., None]
    bd = jnp.exp(log_gamma * LB).reshape(RET_HEADS, 1, 1) * jnp.ones((RET_HEADS, 1, 128), F32)
    half = RET_DK // 2
    inv = jnp.exp(-jnp.log(ROPE_BASE) * jnp.arange(half, dtype=F32) / half)
    ang = jnp.arange(S, dtype=F32)[:, None] * inv[None, :]
    return dmat.astype(F32), qd.astype(F32), kd.astype(F32), bd, jnp.cos(ang), jnp.sin(ang)


def _rope(t, c, s):
    t1, t2 = t[:, :128], t[:, 128:]
    return jnp.concatenate([t1 * c - t2 * s, t1 * s + t2 * c], axis=-1)


def _rope_inv(d, c, s):
    d1, d2 = d[:, :128], d[:, 128:]
    return jnp.concatenate([d1 * c + d2 * s, d2 * c - d1 * s], axis=-1)


def _ret_block_fwd(p_ref, h, c, s, d_ref, qd_ref, kd_ref, stb):
    q = _rope(p_ref[:, h * RET_DK:(h + 1) * RET_DK].astype(F32), c, s)
    k = _rope(p_ref[:, 1024 + h * RET_DK:1024 + (h + 1) * RET_DK].astype(F32), c, s) * (RET_DK ** -0.5)
    v = p_ref[:, 2048 + h * RET_DV:2048 + (h + 1) * RET_DV]
    qb, kb = q.astype(BF16), k.astype(BF16)
    scb = (_dot_nt(qb, kb) * d_ref[h]).astype(BF16)
    o = _dot(scb, v) + qd_ref[h] * _dot(qb, stb)
    return q, k, qb, kb, v, scb, o


def _ret_fwd(proj, gn_g, consts, *, B, S):
    LB = RET_BLOCK
    nb = S // LB
    T = B * S
    dmat, qd, kd, bd, cos, sin = consts

    def body(p_ref, cos_ref, sin_ref, d_ref, qd_ref, kd_ref, bd_ref, gng_ref, y_ref, st_ref, state_s):
        i = pl.program_id(1)

        @pl.when(i == 0)
        def _():
            state_s[...] = jnp.zeros_like(state_s)
        c, s = cos_ref[...], sin_ref[...]
        for h in range(RET_HEADS):
            st = state_s[h]
            stb = st.astype(BF16)
            st_ref[h] = stb
            q, k, qb, kb, v, scb, o = _ret_block_fwd(p_ref, h, c, s, d_ref, qd_ref, kd_ref, stb)
            kdk = (k * kd_ref[h]).astype(BF16)
            state_s[h] = st * bd_ref[h][:, :1] + _dot_tn(kdk, v)
            gate = p_ref[:, 4096 + h * RET_DV:4096 + (h + 1) * RET_DV].astype(F32)
            mu = jnp.mean(o, axis=-1, keepdims=True)
            oc = o - mu
            xh = oc * lax.rsqrt(jnp.mean(oc * oc, axis=-1, keepdims=True) + EPS)
            y = (gate * _sigmoid(gate)) * (xh * gng_ref[:, h * RET_DV:(h + 1) * RET_DV])
            y_ref[:, h * RET_DV:(h + 1) * RET_DV] = y.astype(BF16)

    const = lambda b, i: (0, 0, 0)
    return pl.pallas_call(
        body, name="ret_fwd", grid=(B, nb),
        in_specs=[pl.BlockSpec((LB, 6144), lambda b, i: (b * nb + i, 0)),
                  pl.BlockSpec((LB, 128), lambda b, i: (i, 0)), pl.BlockSpec((LB, 128), lambda b, i: (i, 0)),
                  pl.BlockSpec((RET_HEADS, LB, LB), const), pl.BlockSpec((RET_HEADS, LB, 1), const),
                  pl.BlockSpec((RET_HEADS, LB, 1), const), pl.BlockSpec((RET_HEADS, 1, 128), const),
                  pl.BlockSpec((1, 2048), lambda b, i: (0, 0))],
        out_specs=[pl.BlockSpec((LB, 2048), lambda b, i: (b * nb + i, 0)),
                   pl.BlockSpec((None, None, RET_HEADS, RET_DK, RET_DV), lambda b, i: (b, i, 0, 0, 0))],
        out_shape=[jax.ShapeDtypeStruct((T, 2048), BF16), jax.ShapeDtypeStruct((B, nb, RET_HEADS, RET_DK, RET_DV), BF16)],
        scratch_shapes=[pltpu.VMEM((RET_HEADS, RET_DK, RET_DV), F32)], compiler_params=_params(2),
    )(proj, cos, sin, dmat, qd, kd, bd, gn_g.reshape(1, 2048))


def _ret_bwd(proj, dy, states, gn_g, consts, *, B, S):
    LB = RET_BLOCK
    nb = S // LB
    T = B * S
    dmat, qd, kd, bd, cos, sin = consts

    def body(p_ref, dy_ref, st_ref, cos_ref, sin_ref, d_ref, qd_ref, kd_ref, bd_ref, gng_ref, dp_ref, dgn_ref, dstate_s):
        b, i = pl.program_id(0), pl.program_id(1)

        @pl.when(i == 0)
        def _():
            dstate_s[...] = jnp.zeros_like(dstate_s)

        @pl.when((b == 0) & (i == 0))
        def _():
            dgn_ref[...] = jnp.zeros_like(dgn_ref)
        c, s = cos_ref[...], sin_ref[...]
        for h in range(RET_HEADS):
            vs = slice(h * RET_DV, (h + 1) * RET_DV)
            stb = st_ref[h]
            q, k, qb, kb, v, scb, o = _ret_block_fwd(p_ref, h, c, s, d_ref, qd_ref, kd_ref, stb)
            gate = p_ref[:, 4096 + h * RET_DV:4096 + (h + 1) * RET_DV].astype(F32)
            mu = jnp.mean(o, axis=-1, keepdims=True)
            oc = o - mu
            rstd = lax.rsqrt(jnp.mean(oc * oc, axis=-1, keepdims=True) + EPS)
            xh = oc * rstd
            gng = gng_ref[:, vs]
            dyh = dy_ref[:, vs].astype(F32)
            sg = _sigmoid(gate)
            silu = gate * sg
            dgn_ref[:, vs] += jnp.sum(dyh * silu * xh, axis=0, keepdims=True)
            dxh = dyh * silu * gng
            do = rstd * (dxh - jnp.mean(dxh, axis=-1, keepdims=True) - xh * jnp.mean(dxh * xh, axis=-1, keepdims=True))
            dgate = dyh * xh * gng * (sg * (1.0 + gate * (1.0 - sg)))
            dob = do.astype(BF16)
            dsb = (_dot_nt(dob, v) * d_ref[h]).astype(BF16)
            dst = dstate_s[h]
            dstb = dst.astype(BF16)
            kdk = (k * kd_ref[h]).astype(BF16)
            dqr = _dot(dsb, kb) + qd_ref[h] * _dot_nt(dob, stb)
            dkr = _dot_tn(dsb, qb) + kd_ref[h] * _dot_nt(v, dstb)
            dv = _dot_tn(scb, dob) + _dot(kdk, dstb)
            dstate_s[h] = dst * bd_ref[h][:, :1] + _dot_tn((q * qd_ref[h]).astype(BF16), dob)
            dp_ref[:, h * RET_DK:(h + 1) * RET_DK] = _rope_inv(dqr, c, s).astype(BF16)
            dp_ref[:, 1024 + h * RET_DK:1024 + (h + 1) * RET_DK] = (_rope_inv(dkr, c, s) * (RET_DK ** -0.5)).astype(BF16)
            dp_ref[:, 2048 + h * RET_DV:2048 + (h + 1) * RET_DV] = dv.astype(BF16)
            dp_ref[:, 4096 + h * RET_DV:4096 + (h + 1) * RET_DV] = dgate.astype(BF16)

    const = lambda b, i: (0, 0, 0)
    rev = lambda b, i: (b * nb + nb - 1 - i, 0)
    return pl.pallas_call(
        body, name="ret_bwd", grid=(B, nb),
        in_specs=[pl.BlockSpec((LB, 6144), rev), pl.BlockSpec((LB, 2048), rev),
                  pl.BlockSpec((None, None, RET_HEADS, RET_DK, RET_DV), lambda b, i: (b, nb - 1 - i, 0, 0, 0)),
                  pl.BlockSpec((LB, 128), lambda b, i: (nb - 1 - i, 0)), pl.BlockSpec((LB, 128), lambda b, i: (nb - 1 - i, 0)),
                  pl.BlockSpec((RET_HEADS, LB, LB), const), pl.BlockSpec((RET_HEADS, LB, 1), const),
                  pl.BlockSpec((RET_HEADS, LB, 1), const), pl.BlockSpec((RET_HEADS, 1, 128), const),
                  pl.BlockSpec((1, 2048), lambda b, i: (0, 0))],
        out_specs=[pl.BlockSpec((LB, 6144), rev), pl.BlockSpec((1, 2048), lambda b, i: (0, 0))],
        out_shape=[jax.ShapeDtypeStruct((T, 6144), BF16), jax.ShapeDtypeStruct((1, 2048), F32)],
        scratch_shapes=[pltpu.VMEM((RET_HEADS, RET_DK, RET_DV), F32)], compiler_params=_params(2),
    )(proj, dy, states, cos, sin, dmat, qd, kd, bd, gn_g.reshape(1, 2048))


BIAS_LANES = 4 * ATT_BLOCK


def _diag_onehot():
    r = lax.broadcasted_iota(jnp.int32, (REL_TABLE, BIAS_LANES), 0)
    j = lax.broadcasted_iota(jnp.int32, (REL_TABLE, BIAS_LANES), 1)
    idx = jnp.maximum(j - ATT_BLOCK - PAST, -MAX_REL) + MAX_REL
    return jnp.where(idx == r, 1.0, 0.0).astype(F32)


def _row_is(j):
    return lax.broadcasted_iota(jnp.int32, (8, BIAS_LANES), 0) == j


def _att_bias(table):
    QB, KW = ATT_BLOCK, 3 * ATT_BLOCK

    def body(t_ref, b_ref, bt_ref):
        row = jnp.broadcast_to(t_ref[...], (8, REL_TABLE))
        diag = jnp.dot(row, _diag_onehot(), preferred_element_type=F32, precision=lax.Precision.HIGHEST)
        rows = jnp.zeros((8, BIAS_LANES), F32)
        for j in range(8):
            rows = jnp.where(_row_is(j), diag if j == 0 else pltpu.roll(diag, j, axis=1), rows)
        n = 8
        while n < QB:
            rows = jnp.concatenate([rows, pltpu.roll(rows, n, axis=1)], axis=0)
            n *= 2
        bias = rows[:, QB:]
        qi = lax.broadcasted_iota(jnp.int32, (QB, KW), 0)
        kj = lax.broadcasted_iota(jnp.int32, (QB, KW), 1)
        lo = (qi // CHUNK) * CHUNK
        bias = jnp.where((kj >= lo) & (kj < lo + PAST + CHUNK), bias, NEG)
        b_ref[...] = bias
        bt_ref[...] = bias.T

    return pl.pallas_call(
        body, name="att_bias", grid=(ATT_HEADS,),
        in_specs=[pl.BlockSpec((None, 1, REL_TABLE), lambda h: (h, 0, 0))],
        out_specs=[pl.BlockSpec((None, QB, KW), lambda h: (h, 0, 0)), pl.BlockSpec((None, KW, QB), lambda h: (h // 2, 0, h % 2))],
        out_shape=[jax.ShapeDtypeStruct((ATT_HEADS, QB, KW), F32), jax.ShapeDtypeStruct((ATT_HEADS // 2, KW, 2 * QB), F32)],
        compiler_params=_params(1),
    )(table.reshape(ATT_HEADS, 1, REL_TABLE))


def _att_bias_grad(dbias_t):
    QB, KW = ATT_BLOCK, 3 * ATT_BLOCK

    def body(d_ref, o_ref):
        rows = jnp.concatenate([jnp.zeros((QB, QB), F32), d_ref[...].T], axis=1)
        n = QB // 2
        while n >= 8:
            rows = rows[:n] + pltpu.roll(rows[n:], BIAS_LANES - n, axis=1)
            n //= 2
        acc = jnp.zeros((8, BIAS_LANES), F32)
        for j in range(8):
            acc = acc + jnp.where(_row_is(j), rows if j == 0 else pltpu.roll(rows, BIAS_LANES - j, axis=1), 0.0)
        diag = jnp.broadcast_to(jnp.sum(acc, axis=0, keepdims=True), (8, BIAS_LANES))
        grad = lax.dot_general(diag, _diag_onehot(), (((1,), (1,)), ((), ())), preferred_element_type=F32,
                               precision=lax.Precision.HIGHEST)
        o_ref[...] = grad[:1]

    return pl.pallas_call(
        body, name="att_bias_grad", grid=(ATT_HEADS,),
        in_specs=[pl.BlockSpec((None, KW, QB), lambda h: (h // 2, 0, h % 2))],
        out_specs=pl.BlockSpec((None, 1, REL_TABLE), lambda h: (h, 0, 0)),
        out_shape=jax.ShapeDtypeStruct((ATT_HEADS, 1, REL_TABLE), F32), compiler_params=_params(1),
    )(dbias_t).reshape(ATT_HEADS, REL_TABLE)


def _att_fwd(qkv, bias, *, B, S):
    QB = ATT_BLOCK
    nb = S // QB
    KW = 3 * QB
    T = B * S
    scale = ATT_DH ** -0.5

    def body(q_ref, k0, k1, k2, v0, v1, v2, b_ref, o_ref, lse_ref):
        i = pl.program_id(2)
        k3 = jnp.concatenate([k0[...], k1[...], k2[...]], axis=0)
        v3 = jnp.concatenate([v0[...], v1[...], v2[...]], axis=0)
        TQ = QB
        col = lax.broadcasted_iota(jnp.int32, (TQ, KW), 1)
        in_seq = col >= (2 - i) * QB
        lane = lax.broadcasted_iota(jnp.int32, (TQ, 128), 1)
        lane_kv = lax.broadcasted_iota(jnp.int32, (KW, 128), 1)
        vh = [jnp.where((lane_kv < ATT_DH) == (hh == 0), v3, jnp.zeros_like(v3)) for hh in range(2)]
        outs, lses = [], []
        for r in range(QB // TQ):
            rows = slice(r * TQ, (r + 1) * TQ)
            q = (q_ref[rows, :].astype(F32) * scale).astype(BF16)
            out = jnp.zeros((TQ, 128), F32)
            lse = jnp.zeros((TQ, 128), F32)
            for hh in range(2):
                qh = jnp.where((lane < ATT_DH) == (hh == 0), q, jnp.zeros_like(q))
                s = jnp.where(in_seq, _dot_nt(qh, k3) + b_ref[hh, rows, :], NEG)
                m = jnp.max(s, axis=-1, keepdims=True)
                e = jnp.exp(s - m)
                l = jnp.sum(e, axis=-1, keepdims=True)
                out = out + _dot(e.astype(BF16), vh[hh]) / l
                lse = jnp.where(lane == hh, m + jnp.log(l), lse)
            outs.append(out)
            lses.append(lse)
        o_ref[...] = jnp.concatenate(outs, axis=0).astype(BF16)
        lse_ref[...] = jnp.concatenate(lses, axis=0).T[:8]

    def kv(d, col0):
        return pl.BlockSpec((QB, 128), lambda hp, b, i: (b * nb + jnp.maximum(i - d, 0), col0 + hp))

    return pl.pallas_call(
        body, name="att_fwd", grid=(8, B, nb),
        in_specs=[pl.BlockSpec((QB, 128), lambda hp, b, i: (b * nb + i, hp)),
                  kv(2, 8), kv(1, 8), kv(0, 8), kv(2, 16), kv(1, 16), kv(0, 16),
                  pl.BlockSpec((2, QB, KW), lambda hp, b, i: (hp, 0, 0))],
        out_specs=[pl.BlockSpec((QB, 128), lambda hp, b, i: (b * nb + i, hp)),
                   pl.BlockSpec((None, 8, QB), lambda hp, b, i: (hp, 0, b * nb + i))],
        out_shape=[jax.ShapeDtypeStruct((T, 1024), BF16), jax.ShapeDtypeStruct((8, 8, T), F32)], compiler_params=_params(3),
    )(qkv, qkv, qkv, qkv, qkv, qkv, qkv, bias)


def _att_bwd(qkv, do, o, lse, bias_t, *, B, S):
    QB = ATT_BLOCK
    nb = S // QB
    KW = 3 * QB
    T = B * S
    scale = ATT_DH ** -0.5
    TK = 256

    def body(q_ref, k0, k1, k2, v0, v1, v2, do_ref, o_ref, lse_ref, b_ref, dq_ref, dk_ref, dv_ref, db_ref, dk_acc, dv_acc):
        b, i = pl.program_id(1), pl.program_id(2)

        @pl.when(i == 0)
        def _():
            dk_acc[...] = jnp.zeros_like(dk_acc)
            dv_acc[...] = jnp.zeros_like(dv_acc)

        @pl.when((b == 0) & (i == 0))
        def _():
            db_ref[...] = jnp.zeros_like(db_ref)

        @pl.when(i < nb)
        def _():
            lane = lax.broadcasted_iota(jnp.int32, (QB, 128), 1)
            first = lane < ATT_DH

            def by_head(x):
                zero = jnp.zeros_like(x)
                return jnp.concatenate([jnp.where(first, x, zero), jnp.where(first, zero, x)], axis=0)

            dout = do_ref[...]
            q2 = by_head((q_ref[...].astype(F32) * scale).astype(BF16))
            do2 = by_head(dout)
            delta_t = (o_ref[...].astype(F32) * dout.astype(F32)).T
            delta2 = jnp.concatenate([jnp.sum(delta_t[:ATT_DH], axis=0, keepdims=True),
                                      jnp.sum(delta_t[ATT_DH:], axis=0, keepdims=True)], axis=1)
            lse2 = jnp.concatenate([lse_ref[0:1, :], lse_ref[1:2, :]], axis=1)
            dq_t = jnp.zeros((128, 2 * QB), F32)
            for d, (k_ref, v_ref) in enumerate(((k0, v0), (k1, v1), (k2, v2))):
                kblk, vblk = k_ref[...], v_ref[...]
                kt = kblk.astype(F32).T.astype(BF16)
                lse_d = jnp.where(i + d >= 2, lse2, -NEG)
                slot = (i + 1 + d) % 3
                for t in range(QB // TK):
                    rows = slice(t * TK, (t + 1) * TK)
                    wrows = slice(d * QB + t * TK, d * QB + (t + 1) * TK)
                    p = jnp.exp(_dot_nt(kblk[rows], q2) + b_ref[wrows, :] - lse_d)
                    ds = p * (_dot_nt(vblk[rows], do2) - delta2)
                    db_ref[wrows, :] += ds
                    dsb = ds.astype(BF16)
                    dk_acc[slot, rows, :] += _dot(dsb, q2)
                    dv_acc[slot, rows, :] += _dot(p.astype(BF16), do2)
                    dq_t += _dot(kt[:, rows], dsb)
            row = lax.broadcasted_iota(jnp.int32, (128, QB), 0)
            dq_ref[...] = (jnp.where(row < ATT_DH, dq_t[:, :QB], dq_t[:, QB:]) * scale).T.astype(BF16)

        @pl.when(i >= 2)
        def _():
            slot = (i + 1) % 3
            dk_ref[...] = dk_acc[slot].astype(BF16)
            dv_ref[...] = dv_acc[slot].astype(BF16)
            dk_acc[slot] = jnp.zeros((QB, 128), F32)
            dv_acc[slot] = jnp.zeros((QB, 128), F32)

    def qrow(b, i):
        return b * nb + jnp.minimum(i, nb - 1)

    def kv(d, col0):
        return pl.BlockSpec((QB, 128), lambda hp, b, i: (b * nb + jnp.maximum(jnp.minimum(i, nb - 1) - d, 0), col0 + hp))

    late = pl.BlockSpec((QB, 128), lambda hp, b, i: (b * nb + jnp.maximum(i - 2, 0), hp))
    return pl.pallas_call(
        body, name="att_bwd", grid=(8, B, nb + 2),
        in_specs=[pl.BlockSpec((QB, 128), lambda hp, b, i: (qrow(b, i), hp)),
                  kv(2, 8), kv(1, 8), kv(0, 8), kv(2, 16), kv(1, 16), kv(0, 16),
                  pl.BlockSpec((QB, 128), lambda hp, b, i: (qrow(b, i), hp)),
                  pl.BlockSpec((QB, 128), lambda hp, b, i: (qrow(b, i), hp)),
                  pl.BlockSpec((None, 8, QB), lambda hp, b, i: (hp, 0, qrow(b, i))),
                  pl.BlockSpec((None, KW, 2 * QB), lambda hp, b, i: (hp, 0, 0))],
        out_specs=[pl.BlockSpec((QB, 128), lambda hp, b, i: (qrow(b, i), hp)), late, late,
                   pl.BlockSpec((None, KW, 2 * QB), lambda hp, b, i: (hp, 0, 0))],
        out_shape=[jax.ShapeDtypeStruct((T, 1024), BF16)] * 3 + [jax.ShapeDtypeStruct((ATT_HEADS // 2, KW, 2 * QB), F32)],
        scratch_shapes=[pltpu.VMEM((3, QB, 128), F32), pltpu.VMEM((3, QB, 128), F32)], compiler_params=_params(3),
    )(qkv, qkv, qkv, qkv, qkv, qkv, qkv, do, o, lse, bias_t)


def _tok_tile(T, want):
    t = min(T, want)
    assert T % t == 0
    return t


def _step(x, tgt, slab, slab_rel, place, mix_g, gn_g, mlp_g, fin_g):
    B, S, D = x.shape
    T = B * S
    h0 = x.reshape(T, D)
    tgt = tgt.reshape(T, D)
    tm = _tok_tile(T, 1024)
    tb = _tok_tile(T, 512)
    tq = _tok_tile(T, 256)
    tt = _tok_tile(T, 8192)
    tf = _tok_tile(T, 2048)
    consts = _ret_consts(S, RET_BLOCK)
    w = {}

    (w["ret_w_in"],) = _all_gather_slabs([slab["ret_w_in"]])
    (proj, hn0), (w["ret_w_out"], w["mlp_w1_0"], w["mlp_w2_0"]) = _mm_nn(
        h0, w["ret_w_in"], "col", tm=tb, out_dtype=BF16, name="ret_in", norm_g=mix_g[0],
        carry=_gather_carry([slab["ret_w_out"], slab["mlp_w1_0"], slab["mlp_w2_0"]]))
    y_ret, states = _ret_fwd(proj, gn_g, consts, B=B, S=S)
    h1, (w["att_w_in"], w["att_w_out"], rel_slabs) = _mm_nn(
        y_ret, w["ret_w_out"], "row", tm=tm, out_dtype=F32, name="ret_out", res=h0,
        carry=_gather_carry([slab["att_w_in"], slab["att_w_out"], slab_rel]))
    (a0, hm0), (w["mlp_w1_1"],) = _mm_nn(h1, w["mlp_w1_0"], "col", tm=tb, out_dtype=BF16, name="mlp0_up", norm_g=mlp_g[0],
                                         act="relu", carry=_gather_carry([slab["mlp_w1_1"]]))
    h2, (w["mlp_w2_1"],) = _mm_nn(a0, w["mlp_w2_0"], "row", tm=tb, out_dtype=F32, name="mlp0_down", square=True, res=h1,
                                  carry=_gather_carry([slab["mlp_w2_1"]]))
    rel_bias = jnp.transpose(rel_slabs, (1, 0, 2)).reshape(ATT_HEADS, REL_TABLE)
    bias, bias_t = _att_bias(rel_bias)
    qkv, hn1 = _mm_nn(h2, w["att_w_in"], "col", tm=tb, out_dtype=BF16, name="att_in", norm_g=mix_g[1])
    o_att, lse = _att_fwd(qkv, bias, B=B, S=S)
    h3 = _mm_nn(o_att, w["att_w_out"], "row", tm=tm, out_dtype=F32, name="att_out", res=h2)
    a1, hm1 = _mm_nn(h3, w["mlp_w1_1"], "col", tm=tb, out_dtype=BF16, name="mlp1_up", norm_g=mlp_g[1], act="relu")
    dh4, d_fin_g, _, loss = _mm_nn(a1, w["mlp_w2_1"], "row", tm=tb, out_dtype=F32, name="mlp1_down", square=True, res=h3,
                                   loss=(tgt, fin_g))

    gw = {}
    gw["mlp_w2_1"] = _mm_tn(a1, dh4, "row", tt=tf, tk=1024, tn=D, name="d_mlp1_w2", square=True)
    dz1 = _mm_nt(dh4, w["mlp_w2_1"], "row", tm=tb, name="d_mlp1_act", epi="relu2bwd", a2=a1)
    gw["mlp_w1_1"] = _mm_tn(hm1, dz1, "col", tt=tt, tk=D, tn=256, name="d_mlp1_w1")
    dh3, d_mlp_g1 = _mm_nt(dz1, w["mlp_w1_1"], "col", tm=tb, name="d_mlp1_in", epi="normbwd",
                           h=h3, g=mlp_g[1], dres=dh4)
    gw["att_w_out"] = _mm_tn(o_att, dh3, "row", tt=tf, tk=256, tn=D, name="d_att_wout")
    do_att = _mm_nt(dh3, w["att_w_out"], "row", tm=tb, name="d_att_o", epi="bf16")
    dq, dk, dv, dbias_t = _att_bwd(qkv, do_att, o_att, lse, bias_t, B=B, S=S)
    d_rel = _att_bias_grad(dbias_t)
    dqkv = (dq, dk, dv)
    gw["att_w_in"] = _mm_tn(hn1, dqkv, "col", tt=tt, tk=D, tn=256, name="d_att_win")
    sums, landed = {}, {}

    def swap_halves(names):
        return _sibling_carry([gw[n] for n in names])

    def add_halves(names, carried):
        for t, n in enumerate(names):
            sums[n] = _add_sibling(place, carried[t], carried[len(names) + t], name="chip_sum_" + n)

    def carried_exchange(names):
        return _chips_carry([sums[n][0] for n in names])

    def keep(names, carried):
        landed.update(zip(names, carried[len(names):]))

    layer1 = ["mlp_w1_1", "mlp_w2_1", "att_w_in", "att_w_out"]
    (dh2, d_mix_g1), carried = _mm_nt(dqkv, w["att_w_in"], "col", tm=tb, name="d_att_in", epi="normbwd",
                                      h=h2, g=mix_g[1], dres=dh3, carry=swap_halves(layer1))
    add_halves(layer1, carried)
    gw["mlp_w2_0"] = _mm_tn(a0, dh2, "row", tt=tf, tk=1024, tn=D, name="d_mlp0_w2", square=True)
    dz0, carried = _mm_nt(dh2, w["mlp_w2_0"], "row", tm=tb, name="d_mlp0_act", epi="relu2bwd", a2=a0,
                          carry=carried_exchange(["mlp_w1_1", "mlp_w2_1"]))
    keep(["mlp_w1_1", "mlp_w2_1"], carried)
    gw["mlp_w1_0"] = _mm_tn(hm0, dz0, "col", tt=tt, tk=D, tn=256, name="d_mlp0_w1")
    (dh1, d_mlp_g0), carried = _mm_nt(dz0, w["mlp_w1_0"], "col", tm=tb, name="d_mlp0_in", epi="normbwd",
                                      h=h1, g=mlp_g[0], dres=dh2,
                                      carry=_merge(carried_exchange(["att_w_in", "att_w_out"]), swap_halves(["mlp_w1_0", "mlp_w2_0"])))
    keep(["att_w_in", "att_w_out"], carried[:4])
    add_halves(["mlp_w1_0", "mlp_w2_0"], carried[4:])
    gw["ret_w_out"] = _mm_tn(y_ret, dh1, "row", tt=tf, tk=512, tn=D, name="d_ret_wout")
    dy_ret, carried = _mm_nt(dh1, w["ret_w_out"], "row", tm=tb, name="d_ret_y", epi="bf16", carry=swap_halves(["ret_w_out"]))
    add_halves(["ret_w_out"], carried)
    dproj, d_gn = _ret_bwd(proj, dy_ret, states, gn_g, consts, B=B, S=S)
    gw["ret_w_in"], carried = _mm_tn(hn0, dproj, "col", tt=tt, tk=D, tn=256, name="d_ret_win",
                                     carry=carried_exchange(["mlp_w1_0", "mlp_w2_0", "ret_w_out"]))
    keep(["mlp_w1_0", "mlp_w2_0", "ret_w_out"], carried)
    add_halves(["ret_w_in"], [gw["ret_w_in"]] + list(_exchange_with_sibling([gw["ret_w_in"]], "ret_in")))
    done = [n for n in BIG if n != "ret_w_in"]
    shards = {n: _add_chips(place, sums[n][1], landed[n], name="total_" + n) for n in done}
    (dx, d_mix_g0), carried = _mm_nt(dproj, w["ret_w_in"], "col", tm=tq, name="d_ret_in", epi="normbwd", h=h0, g=mix_g[0], dres=dh1,
                                     carry=_merge(carried_exchange(["ret_w_in"]), _share_carry([shards[n] for n in done])))
    keep(["ret_w_in"], carried[:2])
    shards.update(zip(done, carried[2:]))
    (shards["ret_w_in"],) = _share_with_sibling([_add_chips(place, sums["ret_w_in"][1], landed["ret_w_in"], name="total_ret_w_in")])
    small = [d_mix_g0, d_mix_g1, d_mlp_g0, d_mlp_g1, d_fin_g, d_gn.reshape(2, D), d_rel.reshape(5, D), loss]
    return dx.reshape(B, S, D), shards, small


def _row_tile(r, want=256):
    t = min(r, want)
    assert r % t == 0
    return t


def _into_slab(place, a, layer, dtype, name):
    _, r, c = a.shape
    tr = _row_tile(r)

    def body(place_ref, a_ref, o_ref):
        o_ref[...] = a_ref[...].astype(dtype)

    grid_spec = pltpu.PrefetchScalarGridSpec(
        num_scalar_prefetch=1, grid=(r // tr,), in_specs=[pl.BlockSpec((None, tr, c), lambda i, pr: (layer, i, 0))],
        out_specs=pl.BlockSpec((None, tr, c), lambda i, pr: (pr[0], i, 0)),
    )
    return pl.pallas_call(
        body, name=name, grid_spec=grid_spec, out_shape=jax.ShapeDtypeStruct((N_CHIPS, r, c), dtype), compiler_params=_params(1),
    )(place, a)


def _add_sibling(place, g, recv, name):
    _, r, c = g.shape
    hr = r // 2
    tr = _row_tile(hr)
    nrt = hr // tr

    def body(place_ref, g_ref, r_ref, sb_ref, own_ref):
        v = g_ref[...] + r_ref[...]
        sb_ref[...] = v.astype(BF16)

        @pl.when(pl.program_id(1) == place_ref[0])
        def _():
            own_ref[...] = v

    grid_spec = pltpu.PrefetchScalarGridSpec(
        num_scalar_prefetch=1, grid=(nrt, N_CHIPS),
        in_specs=[pl.BlockSpec((None, tr, c), lambda i, s, pr: (s, pr[1] * nrt + i, 0)),
                  pl.BlockSpec((None, tr, c), lambda i, s, pr: (s, i, 0))],
        out_specs=[pl.BlockSpec((None, tr, c), lambda i, s, pr: (s, i, 0)), pl.BlockSpec((tr, c), lambda i, s, pr: (i, 0))],
    )
    return pl.pallas_call(
        body, name=name, grid_spec=grid_spec,
        out_shape=[jax.ShapeDtypeStruct((N_CHIPS, hr, c), BF16), jax.ShapeDtypeStruct((hr, c), F32)],
        compiler_params=_params(2),
    )(place, g, recv)


def _add_chips(place, own, recv, name):
    hr, c = own.shape
    tr = _row_tile(hr)
    nrt = hr // tr

    def body(place_ref, o_ref, r_ref, t_ref):
        t_ref[...] = ((o_ref[...] + r_ref[0].astype(F32)) + r_ref[1].astype(F32)) + r_ref[2].astype(F32)

    grid_spec = pltpu.PrefetchScalarGridSpec(
        num_scalar_prefetch=1, grid=(nrt,),
        in_specs=[pl.BlockSpec((tr, c), lambda i, pr: (i, 0)), pl.BlockSpec((3, tr, c), lambda i, pr: (0, i, 0))],
        out_specs=pl.BlockSpec((tr, c), lambda i, pr: (pr[1] * nrt + i, 0)),
    )
    return pl.pallas_call(
        body, name=name, grid_spec=grid_spec, out_shape=jax.ShapeDtypeStruct((2 * hr, c), F32), compiler_params=_params(1),
    )(place, own, recv)


def _adamw(w, gs, m, v, name):
    L, r, c = w.shape
    tr = _row_tile(r)
    assert len(gs) == L

    def body(*refs):
        w_ref, m_ref, v_ref = refs[:3]
        g_refs = refs[3:3 + L]
        go_ref, d_ref, nm_ref, nv_ref = refs[3 + L:]
        gg = g_refs[0][...]
        for k in range(1, L):
            gg = jnp.where(pl.program_id(0) == k, g_refs[k][...], gg)
        go_ref[...] = gg
        nm = ADAM_B1 * m_ref[...] + (1.0 - ADAM_B1) * gg
        nv = ADAM_B2 * v_ref[...] + (1.0 - ADAM_B2) * (gg * gg)
        m_hat = nm / (1.0 - ADAM_B1 ** ADAM_STEP)
        v_hat = nv / (1.0 - ADAM_B2 ** ADAM_STEP)
        d_ref[...] = -ADAM_LR * (m_hat / (jnp.sqrt(v_hat) + ADAM_EPS) + ADAM_WD * w_ref[...])
        nm_ref[...] = nm
        nv_ref[...] = nv

    spec = pl.BlockSpec((None, tr, c), lambda l, i: (l, i, 0))
    return pl.pallas_call(
        body, name=name, grid=(L, r // tr), in_specs=[spec] * 3 + [pl.BlockSpec((tr, c), lambda l, i: (i, 0))] * L,
        out_specs=[spec] * 4, out_shape=[jax.ShapeDtypeStruct((L, r, c), F32)] * 4, compiler_params=_params(2),
    )(w, m, v, *gs)


def _place():
    return lax.axis_index("x"), lax.axis_index("y"), lax.axis_index("c")


def _other_chips(x, y):
    return [(1 - x, y), (x, 1 - y), (1 - x, 1 - y)]


def _remote(src, dst, ssem, rsem, dev):
    return pltpu.make_async_remote_copy(src_ref=src, dst_ref=dst, send_sem=ssem, recv_sem=rsem, device_id=dev,
                                        device_id_type=MESH)


def _gather_phases(n):
    def geometry(refs, t):
        x, y, c = _place()
        hr = refs[t].shape[1] // 2
        chips = _other_chips(x, y)
        return x, y, c, 2 * x + y, chips, [2 * qx + qy for qx, qy in chips], pl.ds(c * hr, hr), pl.ds((1 - c) * hr, hr)

    def send(refs, sems):
        s1, r1, _, _ = sems
        for t in range(n):
            x, y, c, p, chips, cidx, mine, theirs = geometry(refs, t)
            for j, (qx, qy) in enumerate(chips):
                _remote(refs[t].at[p, mine], refs[t].at[p, mine], s1.at[t, j], r1.at[t, j], (qx, qy, c)).start()

    def pass_on(refs, sems):
        s1, r1, s2, r2 = sems
        for t in range(n):
            x, y, c, p, chips, cidx, mine, theirs = geometry(refs, t)
            for j, (qx, qy) in enumerate(chips):
                got = refs[t].at[cidx[j], mine]
                _remote(got, got, s1.at[t, j], r1.at[t, j], (qx, qy, c)).wait_recv()
                _remote(got, got, s2.at[t, j], r2.at[t, j], (x, y, 1 - c)).start()

    def finish(refs, sems):
        s1, r1, s2, r2 = sems
        for t in range(n):
            x, y, c, p, chips, cidx, mine, theirs = geometry(refs, t)
            for j, (qx, qy) in enumerate(chips):
                got = refs[t].at[cidx[j], theirs]
                _remote(got, got, s2.at[t, j], r2.at[t, j], (x, y, 1 - c)).wait_recv()
        for t in range(n):
            x, y, c, p, chips, cidx, mine, theirs = geometry(refs, t)
            for j, (qx, qy) in enumerate(chips):
                _remote(refs[t].at[p, mine], refs[t].at[p, mine], s1.at[t, j], r1.at[t, j], (qx, qy, c)).wait_send()
                sent = refs[t].at[cidx[j], mine]
                _remote(sent, sent, s2.at[t, j], r2.at[t, j], (x, y, 1 - c)).wait_send()

    sem = pltpu.SemaphoreType.DMA
    return send, pass_on, finish, [sem((n, 3)), sem((n, 3)), sem((n, 3)), sem((n, 3))]


def _gather_carry(slabs):
    send, pass_on, finish, sems = _gather_phases(len(slabs))
    return _Carry(slabs, sems, [(0, send), (-2, pass_on), (-1, finish)])


def _all_gather_slabs(slabs):
    n = len(slabs)
    send, pass_on, finish, sems = _gather_phases(n)

    def body(*refs):
        outs, scratch = refs[n:2 * n], refs[2 * n:]
        send(outs, scratch)
        pass_on(outs, scratch)
        finish(outs, scratch)

    return pl.pallas_call(
        body, name="gather_weights", in_specs=[HBM_SPEC] * n, out_specs=[HBM_SPEC] * n,
        out_shape=[jax.ShapeDtypeStruct(s.shape, s.dtype) for s in slabs], input_output_aliases={t: t for t in range(n)},
        scratch_shapes=sems, compiler_params=pltpu.CompilerParams(has_side_effects=True),
    )(*slabs)


def _sibling_phases(n):
    def copies(refs, sems):
        ssem, rsem = sems
        x, y, c = _place()
        out = []
        for t in range(n):
            hr = refs[t].shape[1] // 2
            out.append(_remote(refs[t].at[:, pl.ds((1 - c) * hr, hr), :], refs[n + t], ssem.at[t], rsem.at[t], (x, y, 1 - c)))
        return out

    def send(refs, sems):
        for cp in copies(refs, sems):
            cp.start()

    def finish(refs, sems):
        for cp in copies(refs, sems):
            cp.wait()

    sem = pltpu.SemaphoreType.DMA
    return send, finish, [sem((n,)), sem((n,))]


def _halves_landing(grads):
    return [lax.empty((N_CHIPS, g.shape[1] // 2, g.shape[2]), g.dtype) for g in grads]


def _sibling_carry(grads):
    send, finish, sems = _sibling_phases(len(grads))
    return _Carry(list(grads) + _halves_landing(grads), sems, [(0, send), (-1, finish)])


def _exchange_with_sibling(grads, tag):
    n = len(grads)
    send, finish, sems = _sibling_phases(n)

    def body(*refs):
        both, scratch = refs[:2 * n], refs[2 * n:]
        send(both, scratch)
        finish(both, scratch)

    return pl.pallas_call(
        body, name="grads_to_sibling_" + tag, in_specs=[HBM_SPEC] * n, out_specs=[HBM_SPEC] * n,
        out_shape=[jax.ShapeDtypeStruct((N_CHIPS, g.shape[1] // 2, g.shape[2]), g.dtype) for g in grads],
        scratch_shapes=sems, compiler_params=pltpu.CompilerParams(has_side_effects=True),
    )(*grads)


def _chips_phases(n):
    def copies(refs, sems):
        ssem, rsem = sems
        x, y, c = _place()
        return [_remote(refs[t].at[2 * qx + qy], refs[n + t].at[j], ssem.at[t, j], rsem.at[t, j], (qx, qy, c))
                for t in range(n) for j, (qx, qy) in enumerate(_other_chips(x, y))]

    def send(refs, sems):
        for cp in copies(refs, sems):
            cp.start()

    def finish(refs, sems):
        for cp in copies(refs, sems):
            cp.wait()

    sem = pltpu.SemaphoreType.DMA
    return send, finish, [sem((n, 3)), sem((n, 3))]


def _landing(sums):
    return [lax.empty((3,) + s.shape[1:], s.dtype) for s in sums]


def _chips_carry(sums):
    send, finish, sems = _chips_phases(len(sums))
    return _Carry(list(sums) + _landing(sums), sems, [(0, send), (-1, finish)])


def _share_phases(n):
    def halves(refs, t):
        x, y, c = _place()
        hr = refs[t].shape[0] // 2
        return refs[t].at[pl.ds(c * hr, hr)], refs[t].at[pl.ds((1 - c) * hr, hr)], (x, y, 1 - c)

    def send(refs, sems):
        ssem, rsem = sems
        for t in range(n):
            mine, _, sibling = halves(refs, t)
            _remote(mine, mine, ssem.at[t], rsem.at[t], sibling).start()

    def finish(refs, sems):
        ssem, rsem = sems
        for t in range(n):
            mine, theirs, sibling = halves(refs, t)
            _remote(theirs, theirs, ssem.at[t], rsem.at[t], sibling).wait_recv()
            _remote(mine, mine, ssem.at[t], rsem.at[t], sibling).wait_send()

    sem = pltpu.SemaphoreType.DMA
    return send, finish, [sem((n,)), sem((n,))]


def _share_carry(shards):
    send, finish, sems = _share_phases(len(shards))
    return _Carry(shards, sems, [(0, send), (-1, finish)])


def _share_with_sibling(shards):
    n = len(shards)
    send, finish, sems = _share_phases(n)

    def body(*refs):
        outs, scratch = refs[n:2 * n], refs[2 * n:]
        send(outs, scratch)
        finish(outs, scratch)

    return pl.pallas_call(
        body, name="grads_share", in_specs=[HBM_SPEC] * n, out_specs=[HBM_SPEC] * n,
        out_shape=[jax.ShapeDtypeStruct(s.shape, s.dtype) for s in shards], input_output_aliases={t: t for t in range(n)},
        scratch_shapes=sems, compiler_params=pltpu.CompilerParams(has_side_effects=True),
    )(*shards)


def _all_reduce_small(buf):
    R, C = buf.shape

    def body(in_ref, out_ref, gather, ssem, rsem):
        x, y, c = _place()
        me = 4 * x + 2 * y + c
        gather[me] = in_ref[...]
        flips = [(fx, fy, fc) for fx in (0, 1) for fy in (0, 1) for fc in (0, 1) if fx or fy or fc]
        peers = [(x + fx - 2 * x * fx, y + fy - 2 * y * fy, c + fc - 2 * c * fc) for fx, fy, fc in flips]
        copies = [_remote(in_ref, gather.at[me], ssem.at[k], rsem.at[k], peer) for k, peer in enumerate(peers)]
        for cp in copies:
            cp.start()
        for k, (px, py, pc) in enumerate(peers):
            _remote(in_ref, gather.at[4 * px + 2 * py + pc], ssem.at[k], rsem.at[k], (px, py, pc)).wait_recv()
        for cp in copies:
            cp.wait_send()
        acc = gather[0]
        for d in range(1, 8):
            acc = acc + gather[d]
        out_ref[...] = acc

    sem = pltpu.SemaphoreType.DMA
    vmem = pl.BlockSpec(memory_space=pltpu.VMEM)
    return pl.pallas_call(
        body, name="small_grads_sum", in_specs=[vmem], out_specs=vmem, out_shape=jax.ShapeDtypeStruct((R, C), F32),
        scratch_shapes=[pltpu.VMEM((8, R, C), F32), sem((7,)), sem((7,))],
        compiler_params=pltpu.CompilerParams(has_side_effects=True),
    )(buf)


BIG = ["ret_w_in", "ret_w_out", "att_w_in", "att_w_out", "mlp_w1_0", "mlp_w1_1", "mlp_w2_0", "mlp_w2_1"]
LAYER_OF = {"ret_w_in": ("ret_w_in", 0), "ret_w_out": ("ret_w_out", 0), "att_w_in": ("att_w_in", 0), "att_w_out": ("att_w_out", 0),
            "mlp_w1_0": ("mlp_w1", 0), "mlp_w1_1": ("mlp_w1", 1), "mlp_w2_0": ("mlp_w2", 0), "mlp_w2_1": ("mlp_w2", 1)}
ORDER = ["mix_norm_g", "ret_w_in", "ret_gn_g", "ret_w_out", "att_w_in", "att_rel_bias", "att_w_out", "mlp_norm_g", "mlp_w1", "mlp_w2",
         "final_norm_g"]


def kernel(x, mix_norm_g, ret_w_in, ret_gn_g, ret_w_out, att_w_in, att_rel_bias, att_w_out, mlp_norm_g, mlp_w1, mlp_w2, final_norm_g, loss_target, m_mix_norm_g, m_ret_w_in, m_ret_gn_g, m_ret_w_out, m_att_w_in, m_att_rel_bias, m_att_w_out, m_mlp_norm_g, m_mlp_w1, m_mlp_w2, m_final_norm_g, v_mix_norm_g, v_ret_w_in, v_ret_gn_g, v_ret_w_out, v_att_w_in, v_att_rel_bias, v_att_w_out, v_mlp_norm_g, v_mlp_w1, v_mlp_w2, v_final_norm_g):
    xi, yi, ci = _place()
    chip = 2 * xi + yi
    weights = dict(zip(ORDER, (mix_norm_g, ret_w_in, ret_gn_g, ret_w_out, att_w_in, att_rel_bias, att_w_out, mlp_norm_g, mlp_w1,
                               mlp_w2, final_norm_g)))
    first = dict(zip(ORDER, (m_mix_norm_g, m_ret_w_in, m_ret_gn_g, m_ret_w_out, m_att_w_in, m_att_rel_bias, m_att_w_out,
                             m_mlp_norm_g, m_mlp_w1, m_mlp_w2, m_final_norm_g)))
    second = dict(zip(ORDER, (v_mix_norm_g, v_ret_w_in, v_ret_gn_g, v_ret_w_out, v_att_w_in, v_att_rel_bias, v_att_w_out,
                              v_mlp_norm_g, v_mlp_w1, v_mlp_w2, v_final_norm_g)))

    place = jnp.stack([chip, ci]).astype(jnp.int32)
    slab = {n: _into_slab(place, weights[LAYER_OF[n][0]], LAYER_OF[n][1], BF16, name="cast_" + n) for n in BIG}
    slab_rel = _into_slab(place, att_rel_bias, 0, F32, name="slab_rel_bias")

    grad_x, g_big, small = _step(x, loss_target, slab, slab_rel, place, mix_norm_g, ret_gn_g[0], mlp_norm_g, final_norm_g)

    rows, at = jnp.zeros((16, D_MODEL), F32), 0
    for part in small:
        rows = rows + jnp.pad(part, ((at, 16 - at - part.shape[0]), (0, D_MODEL - part.shape[1])))
        at += part.shape[0]
    rows = _all_reduce_small(rows)
    loss = rows[12, 0]
    grads = {"mix_norm_g": [rows[0:2]], "mlp_norm_g": [rows[2:4]], "final_norm_g": [rows[4:5]], "ret_gn_g": [rows[5:7].reshape(1, 2048)],
             "att_rel_bias": [lax.dynamic_slice_in_dim(rows[7:12].reshape(ATT_HEADS, REL_TABLE), chip * (REL_TABLE // N_CHIPS),
                                                       REL_TABLE // N_CHIPS, axis=1)]}
    for n in BIG:
        grads.setdefault(LAYER_OF[n][0], []).append(g_big[n])

    def as3(a):
        return a.reshape((1,) * (3 - a.ndim) + a.shape)

    results = {}
    for n in ORDER:
        outs = _adamw(as3(weights[n]), grads[n], as3(first[n]), as3(second[n]), name="adamw_" + n)
        results[n] = [o.reshape(weights[n].shape) for o in outs]
    return (loss, grad_x) + tuple(results[n][k] for k in range(4) for n in ORDER)
```

```python
import math

import jax
import jax.numpy as jnp
from jax import lax
from jax.experimental import pallas as pl
from jax.experimental.pallas import tpu as pltpu

F32 = jnp.float32
BF16 = jnp.bfloat16
MESH = pl.DeviceIdType.MESH

D_MODEL = 1024
CHUNK = 64
RET_HEADS = 4
RET_DK = 256
RET_DV = 512
ROPE_BASE = 10000.0
ATT_HEADS = 16
ATT_DH = 64
PAST = 512
MAX_REL = 256
REL_TABLE = MAX_REL + CHUNK
EPS = 1e-6
NEG = -1e30
N_CHIPS = 4

ADAM_LR = 0.001
ADAM_B1 = 0.9
ADAM_B2 = 0.999
ADAM_EPS = 1e-08
ADAM_WD = 0.01
ADAM_STEP = 10

RET_BLOCK = 256
ATT_BLOCK = 256
VMEM_LIMIT = 56 * 1024 * 1024


def _params(n_axes, **kw):
    return pltpu.CompilerParams(dimension_semantics=("arbitrary",) * n_axes, vmem_limit_bytes=VMEM_LIMIT, **kw)


def _dot(a, b):
    return jnp.dot(a, b, preferred_element_type=F32)


def _dot_nt(a, b):
    return lax.dot_general(a, b, (((1,), (1,)), ((), ())), preferred_element_type=F32)


def _dot_tn(a, b):
    return lax.dot_general(a, b, (((0,), (0,)), ((), ())), preferred_element_type=F32)


def _sigmoid(x):
    return 0.5 * jnp.tanh(0.5 * x) + 0.5


HBM_SPEC = pl.BlockSpec(memory_space=pltpu.HBM)


class _Carry:
    def __init__(self, arrays, sems, stages):
        self.arrays, self.sems, self.stages = list(arrays), list(sems), list(stages)


def _merge(a, b):
    na, sa = len(a.arrays), len(a.sems)

    def of_a(fn):
        return lambda refs, sems: fn(refs[:na], sems[:sa])

    def of_b(fn):
        return lambda refs, sems: fn(refs[na:], sems[sa:])

    return _Carry(a.arrays + b.arrays, a.sems + b.sems,
                  [(at, of_a(fn)) for at, fn in a.stages] + [(at, of_b(fn)) for at, fn in b.stages])


def _carry_call(body, carry, *, name, grid, in_specs, out_specs, out_shape, args):
    if carry is None:
        outs = pl.pallas_call(body, name=name, grid=grid, in_specs=in_specs, out_specs=out_specs, out_shape=out_shape,
                              compiler_params=_params(len(grid)))(*args)
        return list(outs), []
    n_in, n_out, n_c = len(in_specs), len(out_specs), len(carry.arrays)
    steps = 1
    for g in grid:
        steps *= g
    assert all(-steps <= at < steps for at, _ in carry.stages)

    def carrying(*refs):
        ins, outs = refs[:n_in], refs[n_in + n_c:n_in + n_c + n_out]
        carried = refs[n_in + n_c + n_out:n_in + 2 * n_c + n_out]
        sems = refs[n_in + 2 * n_c + n_out:]
        step = pl.program_id(0)
        for axis in range(1, len(grid)):
            step = step * grid[axis] + pl.program_id(axis)
        for at, fn in carry.stages:
            if at == 0:
                pl.when(step == 0)(lambda fn=fn: fn(carried, sems))
        body(*ins, *outs)
        for at, fn in carry.stages:
            if at != 0:
                pl.when(step == at % steps)(lambda fn=fn: fn(carried, sems))

    outs = pl.pallas_call(
        carrying, name=name, grid=grid, in_specs=list(in_specs) + [HBM_SPEC] * n_c, out_specs=list(out_specs) + [HBM_SPEC] * n_c,
        out_shape=list(out_shape) + [jax.ShapeDtypeStruct(a.shape, a.dtype) for a in carry.arrays],
        input_output_aliases={n_in + t: n_out + t for t in range(n_c)}, scratch_shapes=carry.sems,
        compiler_params=_params(len(grid), has_side_effects=True),
    )(*args, *carry.arrays)
    return list(outs[:n_out]), list(outs[n_out:])


def _mm_nn(a, w, wkind, *, tm, out_dtype, name, norm_g=None, act=None, square=False, res=None, loss=None, carry=None):
    M, K = a.shape
    cols = w.shape[2]
    N = N_CHIPS * cols if wkind == "col" else cols
    has_norm = norm_g is not None
    steps = M // tm
    assert M % tm == 0 and K == (w.shape[1] if wkind == "col" else N_CHIPS * w.shape[1])
    assert loss is None or (wkind == "row" and act is None)

    def body(*refs):
        it = iter(refs)
        a_ref, w_ref = next(it), next(it)
        g_ref = next(it) if has_norm else None
        r_ref = next(it) if res is not None else None
        if loss is not None:
            t_ref, fg_ref = next(it), next(it)
        o_ref = next(it)
        hn_ref = next(it) if has_norm else None
        if loss is not None:
            dg_ref, sq_ref, loss_ref = next(it), next(it), next(it)
        if has_norm:
            x = a_ref[...].astype(F32)
            r = lax.rsqrt(jnp.mean(x * x, axis=-1, keepdims=True) + EPS)
            lhs = (x * r * g_ref[...]).astype(BF16)
            hn_ref[...] = lhs
        elif square:
            lhs = a_ref[...].astype(BF16)
            lhs = lhs * lhs
        else:
            lhs = a_ref[...].astype(BF16)

        def loss_head(x):
            i = pl.program_id(0)
            r = lax.rsqrt(jnp.mean(x * x, axis=-1, keepdims=True) + EPS)
            xh = x * r
            gg = fg_ref[...]
            diff = xh * gg - t_ref[...]
            sq = jnp.sum(diff * diff, axis=0, keepdims=True)
            dy = diff * (1.0 / N)
            dg_part = jnp.sum(dy * xh, axis=0, keepdims=True)

            @pl.when(i == 0)
            def _():
                sq_ref[...] = sq
                dg_ref[...] = dg_part

            @pl.when(i > 0)
            def _():
                sq_ref[...] += sq
                dg_ref[...] += dg_part
            t = dy * gg
            o_ref[...] = r * (t - xh * jnp.mean(t * xh, axis=-1, keepdims=True))

            @pl.when(i == steps - 1)
            def _():
                loss_ref[...] = (0.5 / N) * jnp.sum(sq_ref[...], axis=1, keepdims=True)

        def finish(acc, sl):
            if act == "relu":
                acc = jnp.maximum(acc, 0.0)
            if r_ref is not None:
                acc = acc + r_ref[:, sl]
            if loss is not None:
                loss_head(acc)
            else:
                o_ref[:, sl] = acc.astype(out_dtype)

        if wkind == "col":
            for s in range(N_CHIPS):
                finish(_dot(lhs, w_ref[s]), slice(s * cols, (s + 1) * cols))
        else:
            finish(_dot(lhs, w_ref[...].reshape(K, N)), slice(None))

    in_specs = [pl.BlockSpec((tm, K), lambda i: (i, 0)), pl.BlockSpec(w.shape, lambda i: (0, 0, 0))]
    args = [a, w]
    if has_norm:
        in_specs.append(pl.BlockSpec((1, K), lambda i: (0, 0)))
        args.append(norm_g.reshape(1, K))
    if res is not None:
        in_specs.append(pl.BlockSpec((tm, N), lambda i: (i, 0)))
        args.append(res)
    if loss is not None:
        in_specs += [pl.BlockSpec((tm, N), lambda i: (i, 0)), pl.BlockSpec((1, N), lambda i: (0, 0))]
        args += [loss[0], loss[1].reshape(1, N)]
    out_shape = [jax.ShapeDtypeStruct((M, N), out_dtype)]
    out_specs = [pl.BlockSpec((tm, N), lambda i: (i, 0))]
    if has_norm:
        out_shape.append(jax.ShapeDtypeStruct((M, K), BF16))
        out_specs.append(pl.BlockSpec((tm, K), lambda i: (i, 0)))
    if loss is not None:
        out_shape += [jax.ShapeDtypeStruct((1, N), F32), jax.ShapeDtypeStruct((1, N), F32), jax.ShapeDtypeStruct((1, 1), F32)]
        out_specs += [pl.BlockSpec((1, N), lambda i: (0, 0)), pl.BlockSpec((1, N), lambda i: (0, 0)),
                      pl.BlockSpec((1, 1), lambda i: (0, 0))]
    outs, carried = _carry_call(body, carry, name=name, grid=(steps,), in_specs=in_specs, out_specs=out_specs,
                                out_shape=out_shape, args=args)
    result = outs if has_norm or loss is not None else outs[0]
    return result if carry is None else (result, carried)


def _mm_nt(a, w, wkind, *, tm, name, epi, a2=None, h=None, g=None, dres=None, carry=None):
    parts = tuple(a) if isinstance(a, (tuple, list)) else (a,)
    M, part_w = parts[0].shape
    Nw = part_w * len(parts)
    rows, cols = w.shape[1], w.shape[2]
    Kw = rows if wkind == "col" else N_CHIPS * rows
    assert M % tm == 0 and Nw == (N_CHIPS * cols if wkind == "col" else cols)
    assert epi != "normbwd" or wkind == "col"
    assert len(parts) == 1 or wkind == "col"
    chunk = math.gcd(part_w, cols)

    def body(*refs):
        it = iter(refs)
        a_refs = [next(it) for _ in parts]
        a_ref, w_ref = a_refs[0], next(it)
        a2_ref = next(it) if epi == "relu2bwd" else None
        if epi == "normbwd":
            h_ref, g_ref, dres_ref = next(it), next(it), next(it)
        o_ref = next(it)
        dg_ref = next(it) if epi == "normbwd" else None
        i = pl.program_id(0)

        def finish(acc, sl):
            if epi == "bf16":
                o_ref[:, sl] = acc.astype(BF16)
            elif epi == "relu2bwd":
                o_ref[:, sl] = (acc * (2.0 * a2_ref[:, sl].astype(F32))).astype(BF16)
            else:
                x = h_ref[...]
                r = lax.rsqrt(jnp.mean(x * x, axis=-1, keepdims=True) + EPS)
                xh = x * r
                dg_part = jnp.sum(acc * xh, axis=0, keepdims=True)

                @pl.when(i == 0)
                def _():
                    dg_ref[...] = dg_part

                @pl.when(i > 0)
                def _():
                    dg_ref[...] += dg_part
                t = acc * g_ref[...]
                dx = r * (t - xh * jnp.mean(t * xh, axis=-1, keepdims=True))
                o_ref[...] = dres_ref[...] + dx

        if wkind == "col":
            acc = None
            for lo in range(0, Nw, chunk):
                (src, a_lo), (s, w_lo) = divmod(lo, part_w), divmod(lo, cols)
                part = _dot_nt(a_refs[src][:, a_lo:a_lo + chunk].astype(BF16), w_ref[s, :, w_lo:w_lo + chunk])
                acc = part if acc is None else acc + part
            finish(acc, slice(None))
        else:
            lhs = a_ref[...].astype(BF16)
            for s in range(N_CHIPS):
                finish(_dot_nt(lhs, w_ref[s]), slice(s * rows, (s + 1) * rows))

    in_specs = [pl.BlockSpec((tm, part_w), lambda i: (i, 0)) for _ in parts] + [pl.BlockSpec(w.shape, lambda i: (0, 0, 0))]
    args = [*parts, w]
    out_dtype = BF16
    if epi == "relu2bwd":
        in_specs.append(pl.BlockSpec((tm, Kw), lambda i: (i, 0)))
        args.append(a2)
    if epi == "normbwd":
        in_specs += [pl.BlockSpec((tm, Kw), lambda i: (i, 0)), pl.BlockSpec((1, Kw), lambda i: (0, 0)),
                     pl.BlockSpec((tm, Kw), lambda i: (i, 0))]
        args += [h, g.reshape(1, Kw), dres]
        out_dtype = F32
    out_shape = [jax.ShapeDtypeStruct((M, Kw), out_dtype)]
    out_specs = [pl.BlockSpec((tm, Kw), lambda i: (i, 0))]
    if epi == "normbwd":
        out_shape.append(jax.ShapeDtypeStruct((1, Kw), F32))
        out_specs.append(pl.BlockSpec((1, Kw), lambda i: (0, 0)))
    outs, carried = _carry_call(body, carry, name=name, grid=(M // tm,), in_specs=in_specs, out_specs=out_specs,
                                out_shape=out_shape, args=args)
    result = outs if epi == "normbwd" else outs[0]
    return result if carry is None else (result, carried)


def _mm_tn(a, b, okind, *, tt, tk, tn, name, square=False, carry=None):
    parts = tuple(b) if isinstance(b, (tuple, list)) else (b,)
    T, K = a.shape
    part_w = parts[0].shape[1]
    N = part_w * len(parts)
    assert T % tt == 0 and K % tk == 0 and part_w % tn == 0
    nt = T // tt
    per_part = part_w // tn
    if okind == "col":
        per = (N // N_CHIPS) // tn
        assert (N // N_CHIPS) % tn == 0
        out_shape = jax.ShapeDtypeStruct((N_CHIPS, K, N // N_CHIPS), F32)
        out_spec = pl.BlockSpec((None, tk, tn), lambda ki, nj, t: (nj // per, ki, nj % per))
    else:
        per = (K // N_CHIPS) // tk
        assert (K // N_CHIPS) % tk == 0
        out_shape = jax.ShapeDtypeStruct((N_CHIPS, K // N_CHIPS, N), F32)
        out_spec = pl.BlockSpec((None, tk, tn), lambda ki, nj, t: (ki // per, ki % per, nj))

    def body(a_ref, *refs):
        b_refs, o_ref = refs[:-1], refs[-1]
        nj, t = pl.program_id(1), pl.program_id(2)

        def accumulate(b_ref):
            lhs = a_ref[...].astype(BF16)
            part = _dot_tn(lhs * lhs if square else lhs, b_ref[...].astype(BF16))
            if nt == 1:
                o_ref[...] = part
            else:
                @pl.when(t == 0)
                def _():
                    o_ref[...] = part

                @pl.when(t > 0)
                def _():
                    o_ref[...] += part

        if len(parts) == 1:
            accumulate(b_refs[0])
        else:
            for s, b_ref in enumerate(b_refs):
                pl.when(nj // per_part == s)(lambda b_ref=b_ref: accumulate(b_ref))

    def b_spec(s):
        return pl.BlockSpec((tt, tn), lambda ki, nj, t: (t, jnp.clip(nj - s * per_part, 0, per_part - 1)))

    outs, carried = _carry_call(
        body, carry, name=name, grid=(K // tk, N // tn, nt),
        in_specs=[pl.BlockSpec((tt, tk), lambda ki, nj, t: (t, ki))] + [b_spec(s) for s in range(len(parts))],
        out_specs=[out_spec], out_shape=[out_shape], args=[a, *parts])
    return outs[0] if carry is None else (outs[0], carried)


def _ret_consts(S, LB):
    log_gamma = jnp.log1p(-jnp.exp2(-5.0 - jnp.arange(RET_HEADS, dtype=F32)))
    idx = jnp.arange(LB, dtype=F32)
    n, m = idx[:, None], idx[None, :]
    cn, cm = jnp.floor(n / CHUNK), jnp.floor(m / CHUNK)
    dist = jnp.where(cm == cn, jnp.abs(n - m), n - m)
    dmat = jnp.where((cm <= cn)[None], jnp.exp(log_gamma[:, None, None] * dist[None]), 0.0)
    qd = jnp.exp(log_gamma[:, None] * (idx + 1.0)[None, :])[..., None]
    kd = jnp.exp(log_gamma[:, None] * (LB - 1 - idx)[None, :])[..., None]
    bd = jnp.exp(log_gamma * LB).reshape(RET_HEADS, 1, 1) * jnp.ones((RET_HEADS, 1, 128), F32)
    half = RET_DK // 2
    inv = jnp.exp(-jnp.log(ROPE_BASE) * jnp.arange(half, dtype=F32) / half)
    ang = jnp.arange(S, dtype=F32)[:, None] * inv[None, :]
    return dmat.astype(F32), qd.astype(F32), kd.astype(F32), bd, jnp.cos(ang), jnp.sin(ang)


def _rope(t, c, s):
    t1, t2 = t[:, :128], t[:, 128:]
    return jnp.concatenate([t1 * c - t2 * s, t1 * s + t2 * c], axis=-1)


def _rope_inv(d, c, s):
    d1, d2 = d[:, :128], d[:, 128:]
    return jnp.concatenate([d1 * c + d2 * s, d2 * c - d1 * s], axis=-1)


def _ret_block_fwd(p_ref, h, c, s, d_ref, qd_ref, kd_ref, stb):
    q = _rope(p_ref[:, h * RET_DK:(h + 1) * RET_DK].astype(F32), c, s)
    k = _rope(p_ref[:, 1024 + h * RET_DK:1024 + (h + 1) * RET_DK].astype(F32), c, s) * (RET_DK ** -0.5)
    v = p_ref[:, 2048 + h * RET_DV:2048 + (h + 1) * RET_DV]
    qb, kb = q.astype(BF16), k.astype(BF16)
    scb = (_dot_nt(qb, kb) * d_ref[h]).astype(BF16)
    o = _dot(scb, v) + qd_ref[h] * _dot(qb, stb)
    return q, k, qb, kb, v, scb, o


def _ret_fwd(proj, gn_g, consts, *, B, S):
    LB = RET_BLOCK
    nb = S // LB
    T = B * S
    dmat, qd, kd, bd, cos, sin = consts

    def body(p_ref, cos_ref, sin_ref, d_ref, qd_ref, kd_ref, bd_ref, gng_ref, y_ref, st_ref, state_s):
        i = pl.program_id(1)

        @pl.when(i == 0)
        def _():
            state_s[...] = jnp.zeros_like(state_s)
        c, s = cos_ref[...], sin_ref[...]
        for h in range(RET_HEADS):
            st = state_s[h]
            stb = st.astype(BF16)
            st_ref[h] = stb
            q, k, qb, kb, v, scb, o = _ret_block_fwd(p_ref, h, c, s, d_ref, qd_ref, kd_ref, stb)
            kdk = (k * kd_ref[h]).astype(BF16)
            state_s[h] = st * bd_ref[h][:, :1] + _dot_tn(kdk, v)
            gate = p_ref[:, 4096 + h * RET_DV:4096 + (h + 1) * RET_DV].astype(F32)
            mu = jnp.mean(o, axis=-1, keepdims=True)
            oc = o - mu
            xh = oc * lax.rsqrt(jnp.mean(oc * oc, axis=-1, keepdims=True) + EPS)
            y = (gate * _sigmoid(gate)) * (xh * gng_ref[:, h * RET_DV:(h + 1) * RET_DV])
            y_ref[:, h * RET_DV:(h + 1) * RET_DV] = y.astype(BF16)

    const = lambda b, i: (0, 0, 0)
    return pl.pallas_call(
        body, name="ret_fwd", grid=(B, nb),
        in_specs=[pl.BlockSpec((LB, 6144), lambda b, i: (b * nb + i, 0)),
                  pl.BlockSpec((LB, 128), lambda b, i: (i, 0)), pl.BlockSpec((LB, 128), lambda b, i: (i, 0)),
                  pl.BlockSpec((RET_HEADS, LB, LB), const), pl.BlockSpec((RET_HEADS, LB, 1), const),
                  pl.BlockSpec((RET_HEADS, LB, 1), const), pl.BlockSpec((RET_HEADS, 1, 128), const),
                  pl.BlockSpec((1, 2048), lambda b, i: (0, 0))],
        out_specs=[pl.BlockSpec((LB, 2048), lambda b, i: (b * nb + i, 0)),
                   pl.BlockSpec((None, None, RET_HEADS, RET_DK, RET_DV), lambda b, i: (b, i, 0, 0, 0))],
        out_shape=[jax.ShapeDtypeStruct((T, 2048), BF16), jax.ShapeDtypeStruct((B, nb, RET_HEADS, RET_DK, RET_DV), BF16)],
        scratch_shapes=[pltpu.VMEM((RET_HEADS, RET_DK, RET_DV), F32)], compiler_params=_params(2),
    )(proj, cos, sin, dmat, qd, kd, bd, gn_g.reshape(1, 2048))


def _ret_bwd(proj, dy, states, gn_g, consts, *, B, S):
    LB = RET_BLOCK
    nb = S // LB
    T = B * S
    dmat, qd, kd, bd, cos, sin = consts

    def body(p_ref, dy_ref, st_ref, cos_ref, sin_ref, d_ref, qd_ref, kd_ref, bd_ref, gng_ref, dp_ref, dgn_ref, dstate_s):
        b, i = pl.program_id(0), pl.program_id(1)

        @pl.when(i == 0)
        def _():
            dstate_s[...] = jnp.zeros_like(dstate_s)

        @pl.when((b == 0) & (i == 0))
        def _():
            dgn_ref[...] = jnp.zeros_like(dgn_ref)
        c, s = cos_ref[...], sin_ref[...]
        for h in range(RET_HEADS):
            vs = slice(h * RET_DV, (h + 1) * RET_DV)
            stb = st_ref[h]
            q, k, qb, kb, v, scb, o = _ret_block_fwd(p_ref, h, c, s, d_ref, qd_ref, kd_ref, stb)
            gate = p_ref[:, 4096 + h * RET_DV:4096 + (h + 1) * RET_DV].astype(F32)
            mu = jnp.mean(o, axis=-1, keepdims=True)
            oc = o - mu
            rstd = lax.rsqrt(jnp.mean(oc * oc, axis=-1, keepdims=True) + EPS)
            xh = oc * rstd
            gng = gng_ref[:, vs]
            dyh = dy_ref[:, vs].astype(F32)
            sg = _sigmoid(gate)
            silu = gate * sg
            dgn_ref[:, vs] += jnp.sum(dyh * silu * xh, axis=0, keepdims=True)
            dxh = dyh * silu * gng
            do = rstd * (dxh - jnp.mean(dxh, axis=-1, keepdims=True) - xh * jnp.mean(dxh * xh, axis=-1, keepdims=True))
            dgate = dyh * xh * gng * (sg * (1.0 + gate * (1.0 - sg)))
            dob = do.astype(BF16)
            dsb = (_dot_nt(dob, v) * d_ref[h]).astype(BF16)
            dst = dstate_s[h]
            dstb = dst.astype(BF16)
            kdk = (k * kd_ref[h]).astype(BF16)
            dqr = _dot(dsb, kb) + qd_ref[h] * _dot_nt(dob, stb)
            dkr = _dot_tn(dsb, qb) + kd_ref[h] * _dot_nt(v, dstb)
            dv = _dot_tn(scb, dob) + _dot(kdk, dstb)
            dstate_s[h] = dst * bd_ref[h][:, :1] + _dot_tn((q * qd_ref[h]).astype(BF16), dob)
            dp_ref[:, h * RET_DK:(h + 1) * RET_DK] = _rope_inv(dqr, c, s).astype(BF16)
            dp_ref[:, 1024 + h * RET_DK:1024 + (h + 1) * RET_DK] = (_rope_inv(dkr, c, s) * (RET_DK ** -0.5)).astype(BF16)
            dp_ref[:, 2048 + h * RET_DV:2048 + (h + 1) * RET_DV] = dv.astype(BF16)
            dp_ref[:, 4096 + h * RET_DV:4096 + (h + 1) * RET_DV] = dgate.astype(BF16)

    const = lambda b, i: (0, 0, 0)
    rev = lambda b, i: (b * nb + nb - 1 - i, 0)
    return pl.pallas_call(
        body, name="ret_bwd", grid=(B, nb),
        in_specs=[pl.BlockSpec((LB, 6144), rev), pl.BlockSpec((LB, 2048), rev),
                  pl.BlockSpec((None, None, RET_HEADS, RET_DK, RET_DV), lambda b, i: (b, nb - 1 - i, 0, 0, 0)),
                  pl.BlockSpec((LB, 128), lambda b, i: (nb - 1 - i, 0)), pl.BlockSpec((LB, 128), lambda b, i: (nb - 1 - i, 0)),
                  pl.BlockSpec((RET_HEADS, LB, LB), const), pl.BlockSpec((RET_HEADS, LB, 1), const),
                  pl.BlockSpec((RET_HEADS, LB, 1), const), pl.BlockSpec((RET_HEADS, 1, 128), const),
                  pl.BlockSpec((1, 2048), lambda b, i: (0, 0))],
        out_specs=[pl.BlockSpec((LB, 6144), rev), pl.BlockSpec((1, 2048), lambda b, i: (0, 0))],
        out_shape=[jax.ShapeDtypeStruct((T, 6144), BF16), jax.ShapeDtypeStruct((1, 2048), F32)],
        scratch_shapes=[pltpu.VMEM((RET_HEADS, RET_DK, RET_DV), F32)], compiler_params=_params(2),
    )(proj, dy, states, cos, sin, dmat, qd, kd, bd, gn_g.reshape(1, 2048))


BIAS_LANES = 4 * ATT_BLOCK


def _diag_onehot():
    r = lax.broadcasted_iota(jnp.int32, (REL_TABLE, BIAS_LANES), 0)
    j = lax.broadcasted_iota(jnp.int32, (REL_TABLE, BIAS_LANES), 1)
    idx = jnp.maximum(j - ATT_BLOCK - PAST, -MAX_REL) + MAX_REL
    return jnp.where(idx == r, 1.0, 0.0).astype(F32)


def _row_is(j):
    return lax.broadcasted_iota(jnp.int32, (8, BIAS_LANES), 0) == j


def _att_bias(table):
    QB, KW = ATT_BLOCK, 3 * ATT_BLOCK

    def body(t_ref, bt_ref):
        row = jnp.broadcast_to(t_ref[...], (8, REL_TABLE))
        diag = jnp.dot(row, _diag_onehot(), preferred_element_type=F32, precision=lax.Precision.HIGHEST)
        rows = jnp.zeros((8, BIAS_LANES), F32)
        for j in range(8):
            rows = jnp.where(_row_is(j), diag if j == 0 else pltpu.roll(diag, j, axis=1), rows)
        n = 8
        while n < QB:
            rows = jnp.concatenate([rows, pltpu.roll(rows, n, axis=1)], axis=0)
            n *= 2
        bias = rows[:, QB:]
        qi = lax.broadcasted_iota(jnp.int32, (QB, KW), 0)
        kj = lax.broadcasted_iota(jnp.int32, (QB, KW), 1)
        lo = (qi // CHUNK) * CHUNK
        bt_ref[...] = jnp.where((kj >= lo) & (kj < lo + PAST + CHUNK), bias, NEG).T

    return pl.pallas_call(
        body, name="att_bias", grid=(ATT_HEADS,),
        in_specs=[pl.BlockSpec((None, 1, REL_TABLE), lambda h: (h, 0, 0))],
        out_specs=pl.BlockSpec((None, KW, QB), lambda h: (h // 2, 0, h % 2)),
        out_shape=jax.ShapeDtypeStruct((ATT_HEADS // 2, KW, 2 * QB), F32),
        compiler_params=_params(1),
    )(table.reshape(ATT_HEADS, 1, REL_TABLE))


def _att_bias_grad(dbias_t):
    QB, KW = ATT_BLOCK, 3 * ATT_BLOCK

    def body(d_ref, o_ref):
        rows = jnp.concatenate([jnp.zeros((QB, QB), F32), d_ref[...].T], axis=1)
        n = QB // 2
        while n >= 8:
            rows = rows[:n] + pltpu.roll(rows[n:], BIAS_LANES - n, axis=1)
            n //= 2
        acc = jnp.zeros((8, BIAS_LANES), F32)
        for j in range(8):
            acc = acc + jnp.where(_row_is(j), rows if j == 0 else pltpu.roll(rows, BIAS_LANES - j, axis=1), 0.0)
        diag = jnp.broadcast_to(jnp.sum(acc, axis=0, keepdims=True), (8, BIAS_LANES))
        grad = lax.dot_general(diag, _diag_onehot(), (((1,), (1,)), ((), ())), preferred_element_type=F32,
                               precision=lax.Precision.HIGHEST)
        o_ref[...] = grad[:1]

    return pl.pallas_call(
        body, name="att_bias_grad", grid=(ATT_HEADS,),
        in_specs=[pl.BlockSpec((None, KW, QB), lambda h: (h // 2, 0, h % 2))],
        out_specs=pl.BlockSpec((None, 1, REL_TABLE), lambda h: (h, 0, 0)),
        out_shape=jax.ShapeDtypeStruct((ATT_HEADS, 1, REL_TABLE), F32), compiler_params=_params(1),
    )(dbias_t).reshape(ATT_HEADS, REL_TABLE)


def _by_head(x):
    first = lax.broadcasted_iota(jnp.int32, x.shape, 1) < ATT_DH
    zero = jnp.zeros_like(x)
    return jnp.concatenate([jnp.where(first, x, zero), jnp.where(first, zero, x)], axis=0)


def _att_fwd(qkv, bias_t, *, B, S):
    QB = ATT_BLOCK
    nb = S // QB
    KW = 3 * QB
    T = B * S
    scale = ATT_DH ** -0.5

    def body(q_ref, k0, k1, k2, v0, v1, v2, b_ref, o_ref, lse_ref, s_blk):
        i = pl.program_id(2)
        q2 = _by_head((q_ref[...].astype(F32) * scale).astype(BF16))
        m = jnp.full((1, 2 * QB), NEG, F32)
        for d, k_ref in enumerate((k0, k1, k2)):
            st = _dot_nt(k_ref[...], q2) + b_ref[d * QB:(d + 1) * QB, :]
            st = jnp.where(i + d >= 2, st, NEG)
            s_blk[d] = st
            m = jnp.maximum(m, jnp.max(st, axis=0, keepdims=True))
        l = jnp.zeros((1, 2 * QB), F32)
        o_t = jnp.zeros((128, 2 * QB), F32)
        for d, v_ref in enumerate((v0, v1, v2)):
            e = jnp.exp(s_blk[d] - m)
            l = l + jnp.sum(e, axis=0, keepdims=True)
            o_t = o_t + _dot(v_ref[...].astype(F32).T.astype(BF16), e.astype(BF16))
        o_t = o_t / l
        row = lax.broadcasted_iota(jnp.int32, (128, QB), 0)
        o_ref[...] = jnp.where(row < ATT_DH, o_t[:, :QB], o_t[:, QB:]).T.astype(BF16)
        lse = m + jnp.log(l)
        row8 = lax.broadcasted_iota(jnp.int32, (8, QB), 0)
        lse_ref[...] = jnp.where(row8 == 0, lse[:, :QB], jnp.where(row8 == 1, lse[:, QB:], 0.0))

    def kv(d, col0):
        return pl.BlockSpec((QB, 128), lambda hp, b, i: (b * nb + jnp.maximum(i - d, 0), col0 + hp))

    return pl.pallas_call(
        body, name="att_fwd", grid=(8, B, nb),
        in_specs=[pl.BlockSpec((QB, 128), lambda hp, b, i: (b * nb + i, hp)),
                  kv(2, 8), kv(1, 8), kv(0, 8), kv(2, 16), kv(1, 16), kv(0, 16),
                  pl.BlockSpec((None, KW, 2 * QB), lambda hp, b, i: (hp, 0, 0))],
        out_specs=[pl.BlockSpec((QB, 128), lambda hp, b, i: (b * nb + i, hp)),
                   pl.BlockSpec((None, 8, QB), lambda hp, b, i: (hp, 0, b * nb + i))],
        out_shape=[jax.ShapeDtypeStruct((T, 1024), BF16), jax.ShapeDtypeStruct((8, 8, T), F32)],
        scratch_shapes=[pltpu.VMEM((3, QB, 2 * QB), F32)], compiler_params=_params(3),
    )(qkv, qkv, qkv, qkv, qkv, qkv, qkv, bias_t)


def _att_bwd(qkv, do, o, lse, bias_t, *, B, S):
    QB = ATT_BLOCK
    nb = S // QB
    KW = 3 * QB
    T = B * S
    scale = ATT_DH ** -0.5
    TK = 256

    def body(q_ref, k0, k1, k2, v0, v1, v2, do_ref, o_ref, lse_ref, b_ref, dq_ref, dk_ref, dv_ref, db_ref, dk_acc, dv_acc):
        b, i = pl.program_id(1), pl.program_id(2)

        @pl.when(i == 0)
        def _():
            dk_acc[...] = jnp.zeros_like(dk_acc)
            dv_acc[...] = jnp.zeros_like(dv_acc)

        @pl.when((b == 0) & (i == 0))
        def _():
            db_ref[...] = jnp.zeros_like(db_ref)

        @pl.when(i < nb)
        def _():
            dout = do_ref[...]
            q2 = _by_head((q_ref[...].astype(F32) * scale).astype(BF16))
            do2 = _by_head(dout)
            delta_t = (o_ref[...].astype(F32) * dout.astype(F32)).T
            delta2 = jnp.concatenate([jnp.sum(delta_t[:ATT_DH], axis=0, keepdims=True),
                                      jnp.sum(delta_t[ATT_DH:], axis=0, keepdims=True)], axis=1)
            lse2 = jnp.concatenate([lse_ref[0:1, :], lse_ref[1:2, :]], axis=1)
            dq_t = jnp.zeros((128, 2 * QB), F32)
            for d, (k_ref, v_ref) in enumerate(((k0, v0), (k1, v1), (k2, v2))):
                kblk, vblk = k_ref[...], v_ref[...]
                kt = kblk.astype(F32).T.astype(BF16)
                lse_d = jnp.where(i + d >= 2, lse2, -NEG)
                slot = (i + 1 + d) % 3
                for t in range(QB // TK):
                    rows = slice(t * TK, (t + 1) * TK)
                    wrows = slice(d * QB + t * TK, d * QB + (t + 1) * TK)
                    p = jnp.exp(_dot_nt(kblk[rows], q2) + b_ref[wrows, :] - lse_d)
                    ds = p * (_dot_nt(vblk[rows], do2) - delta2)
                    db_ref[wrows, :] += ds
                    dsb = ds.astype(BF16)
                    dk_acc[slot, rows, :] += _dot(dsb, q2)
                    dv_acc[slot, rows, :] += _dot(p.astype(BF16), do2)
                    dq_t += _dot(kt[:, rows], dsb)
            row = lax.broadcasted_iota(jnp.int32, (128, QB), 0)
            dq_ref[...] = (jnp.where(row < ATT_DH, dq_t[:, :QB], dq_t[:, QB:]) * scale).T.astype(BF16)

        @pl.when(i >= 2)
        def _():
            slot = (i + 1) % 3
            dk_ref[...] = dk_acc[slot].astype(BF16)
            dv_ref[...] = dv_acc[slot].astype(BF16)
            dk_acc[slot] = jnp.zeros((QB, 128), F32)
            dv_acc[slot] = jnp.zeros((QB, 128), F32)

    def qrow(b, i):
        return b * nb + jnp.minimum(i, nb - 1)

    def kv(d, col0):
        return pl.BlockSpec((QB, 128), lambda hp, b, i: (b * nb + jnp.maximum(jnp.minimum(i, nb - 1) - d, 0), col0 + hp))

    late = pl.BlockSpec((QB, 128), lambda hp, b, i: (b * nb + jnp.maximum(i - 2, 0), hp))
    return pl.pallas_call(
        body, name="att_bwd", grid=(8, B, nb + 2),
        in_specs=[pl.BlockSpec((QB, 128), lambda hp, b, i: (qrow(b, i), hp)),
                  kv(2, 8), kv(1, 8), kv(0, 8), kv(2, 16), kv(1, 16), kv(0, 16),
                  pl.BlockSpec((QB, 128), lambda hp, b, i: (qrow(b, i), hp)),
                  pl.BlockSpec((QB, 128), lambda hp, b, i: (qrow(b, i), hp)),
                  pl.BlockSpec((None, 8, QB), lambda hp, b, i: (hp, 0, qrow(b, i))),
                  pl.BlockSpec((None, KW, 2 * QB), lambda hp, b, i: (hp, 0, 0))],
        out_specs=[pl.BlockSpec((QB, 128), lambda hp, b, i: (qrow(b, i), hp)), late, late,
                   pl.BlockSpec((None, KW, 2 * QB), lambda hp, b, i: (hp, 0, 0))],
        out_shape=[jax.ShapeDtypeStruct((T, 1024), BF16)] * 3 + [jax.ShapeDtypeStruct((ATT_HEADS // 2, KW, 2 * QB), F32)],
        scratch_shapes=[pltpu.VMEM((3, QB, 128), F32), pltpu.VMEM((3, QB, 128), F32)], compiler_params=_params(3),
    )(qkv, qkv, qkv, qkv, qkv, qkv, qkv, do, o, lse, bias_t)


def _tok_tile(T, want):
    t = min(T, want)
    assert T % t == 0
    return t


def _step(x, tgt, slab, slab_rel, place, mix_g, gn_g, mlp_g, fin_g):
    B, S, D = x.shape
    T = B * S
    h0 = x.reshape(T, D)
    tgt = tgt.reshape(T, D)
    tm = _tok_tile(T, 1024)
    tb = _tok_tile(T, 512)
    tq = _tok_tile(T, 256)
    tt = _tok_tile(T, 8192)
    tf = _tok_tile(T, 2048)
    consts = _ret_consts(S, RET_BLOCK)
    w = {}

    (w["ret_w_in"],) = _all_gather_slabs([slab["ret_w_in"]])
    (proj, hn0), (w["ret_w_out"], w["mlp_w1_0"], w["mlp_w2_0"]) = _mm_nn(
        h0, w["ret_w_in"], "col", tm=tb, out_dtype=BF16, name="ret_in", norm_g=mix_g[0],
        carry=_gather_carry([slab["ret_w_out"], slab["mlp_w1_0"], slab["mlp_w2_0"]]))
    y_ret, states = _ret_fwd(proj, gn_g, consts, B=B, S=S)
    h1, (w["att_w_in"], w["att_w_out"], rel_slabs) = _mm_nn(
        y_ret, w["ret_w_out"], "row", tm=tm, out_dtype=F32, name="ret_out", res=h0,
        carry=_gather_carry([slab["att_w_in"], slab["att_w_out"], slab_rel]))
    (a0, hm0), (w["mlp_w1_1"],) = _mm_nn(h1, w["mlp_w1_0"], "col", tm=tb, out_dtype=BF16, name="mlp0_up", norm_g=mlp_g[0],
                                         act="relu", carry=_gather_carry([slab["mlp_w1_1"]]))
    h2, (w["mlp_w2_1"],) = _mm_nn(a0, w["mlp_w2_0"], "row", tm=tb, out_dtype=F32, name="mlp0_down", square=True, res=h1,
                                  carry=_gather_carry([slab["mlp_w2_1"]]))
    rel_bias = jnp.transpose(rel_slabs, (1, 0, 2)).reshape(ATT_HEADS, REL_TABLE)
    bias_t = _att_bias(rel_bias)
    qkv, hn1 = _mm_nn(h2, w["att_w_in"], "col", tm=tb, out_dtype=BF16, name="att_in", norm_g=mix_g[1])
    o_att, lse = _att_fwd(qkv, bias_t, B=B, S=S)
    h3 = _mm_nn(o_att, w["att_w_out"], "row", tm=tm, out_dtype=F32, name="att_out", res=h2)
    a1, hm1 = _mm_nn(h3, w["mlp_w1_1"], "col", tm=tb, out_dtype=BF16, name="mlp1_up", norm_g=mlp_g[1], act="relu")
    dh4, d_fin_g, _, loss = _mm_nn(a1, w["mlp_w2_1"], "row", tm=tb, out_dtype=F32, name="mlp1_down", square=True, res=h3,
                                   loss=(tgt, fin_g))

    gw = {}
    gw["mlp_w2_1"] = _mm_tn(a1, dh4, "row", tt=tf, tk=1024, tn=D, name="d_mlp1_w2", square=True)
    dz1 = _mm_nt(dh4, w["mlp_w2_1"], "row", tm=tb, name="d_mlp1_act", epi="relu2bwd", a2=a1)
    gw["mlp_w1_1"] = _mm_tn(hm1, dz1, "col", tt=tt, tk=D, tn=256, name="d_mlp1_w1")
    dh3, d_mlp_g1 = _mm_nt(dz1, w["mlp_w1_1"], "col", tm=tb, name="d_mlp1_in", epi="normbwd",
                           h=h3, g=mlp_g[1], dres=dh4)
    gw["att_w_out"] = _mm_tn(o_att, dh3, "row", tt=tf, tk=256, tn=D, name="d_att_wout")
    do_att = _mm_nt(dh3, w["att_w_out"], "row", tm=tb, name="d_att_o", epi="bf16")
    dq, dk, dv, dbias_t = _att_bwd(qkv, do_att, o_att, lse, bias_t, B=B, S=S)
    d_rel = _att_bias_grad(dbias_t)
    dqkv = (dq, dk, dv)
    gw["att_w_in"] = _mm_tn(hn1, dqkv, "col", tt=tt, tk=D, tn=256, name="d_att_win")
    sums, landed = {}, {}

    def swap_halves(names):
        return _sibling_carry([gw[n] for n in names])

    def add_halves(names, carried):
        for t, n in enumerate(names):
            sums[n] = _add_sibling(place, carried[t], carried[len(names) + t], name="chip_sum_" + n)

    def carried_exchange(names):
        return _chips_carry([sums[n][0] for n in names])

    def keep(names, carried):
        landed.update(zip(names, carried[len(names):]))

    layer1 = ["mlp_w1_1", "mlp_w2_1", "att_w_in", "att_w_out"]
    (dh2, d_mix_g1), carried = _mm_nt(dqkv, w["att_w_in"], "col", tm=tb, name="d_att_in", epi="normbwd",
                                      h=h2, g=mix_g[1], dres=dh3, carry=swap_halves(layer1))
    add_halves(layer1, carried)
    gw["mlp_w2_0"] = _mm_tn(a0, dh2, "row", tt=tf, tk=1024, tn=D, name="d_mlp0_w2", square=True)
    dz0, carried = _mm_nt(dh2, w["mlp_w2_0"], "row", tm=tb, name="d_mlp0_act", epi="relu2bwd", a2=a0,
                          carry=carried_exchange(["mlp_w1_1", "mlp_w2_1"]))
    keep(["mlp_w1_1", "mlp_w2_1"], carried)
    gw["mlp_w1_0"] = _mm_tn(hm0, dz0, "col", tt=tt, tk=D, tn=256, name="d_mlp0_w1")
    (dh1, d_mlp_g0), carried = _mm_nt(dz0, w["mlp_w1_0"], "col", tm=tb, name="d_mlp0_in", epi="normbwd",
                                      h=h1, g=mlp_g[0], dres=dh2,
                                      carry=_merge(carried_exchange(["att_w_in", "att_w_out"]), swap_halves(["mlp_w1_0", "mlp_w2_0"])))
    keep(["att_w_in", "att_w_out"], carried[:4])
    add_halves(["mlp_w1_0", "mlp_w2_0"], carried[4:])
    gw["ret_w_out"] = _mm_tn(y_ret, dh1, "row", tt=tf, tk=512, tn=D, name="d_ret_wout")
    dy_ret, carried = _mm_nt(dh1, w["ret_w_out"], "row", tm=tb, name="d_ret_y", epi="bf16", carry=swap_halves(["ret_w_out"]))
    add_halves(["ret_w_out"], carried)
    dproj, d_gn = _ret_bwd(proj, dy_ret, states, gn_g, consts, B=B, S=S)
    gw["ret_w_in"], carried = _mm_tn(hn0, dproj, "col", tt=tt, tk=D, tn=256, name="d_ret_win",
                                     carry=carried_exchange(["mlp_w1_0", "mlp_w2_0", "ret_w_out"]))
    keep(["mlp_w1_0", "mlp_w2_0", "ret_w_out"], carried)
    add_halves(["ret_w_in"], [gw["ret_w_in"]] + list(_exchange_with_sibling([gw["ret_w_in"]], "ret_in")))
    done = [n for n in BIG if n != "ret_w_in"]
    shards = {n: _add_chips(place, sums[n][1], landed[n], name="total_" + n) for n in done}
    (dx, d_mix_g0), carried = _mm_nt(dproj, w["ret_w_in"], "col", tm=tb, name="d_ret_in", epi="normbwd", h=h0, g=mix_g[0], dres=dh1,
                                     carry=_merge(carried_exchange(["ret_w_in"]), _share_carry([shards[n] for n in done])))
    keep(["ret_w_in"], carried[:2])
    shards.update(zip(done, carried[2:]))
    (shards["ret_w_in"],) = _share_with_sibling([_add_chips(place, sums["ret_w_in"][1], landed["ret_w_in"], name="total_ret_w_in")])
    small = [d_mix_g0, d_mix_g1, d_mlp_g0, d_mlp_g1, d_fin_g, d_gn.reshape(2, D), d_rel.reshape(5, D), loss]
    return dx.reshape(B, S, D), shards, small


def _row_tile(r, want=256):
    t = min(r, want)
    assert r % t == 0
    return t


def _into_slab(place, a, layer, dtype, name):
    _, r, c = a.shape
    tr = _row_tile(r)

    def body(place_ref, a_ref, o_ref):
        o_ref[...] = a_ref[...].astype(dtype)

    grid_spec = pltpu.PrefetchScalarGridSpec(
        num_scalar_prefetch=1, grid=(r // tr,), in_specs=[pl.BlockSpec((None, tr, c), lambda i, pr: (layer, i, 0))],
        out_specs=pl.BlockSpec((None, tr, c), lambda i, pr: (pr[0], i, 0)),
    )
    return pl.pallas_call(
        body, name=name, grid_spec=grid_spec, out_shape=jax.ShapeDtypeStruct((N_CHIPS, r, c), dtype), compiler_params=_params(1),
    )(place, a)


def _add_sibling(place, g, recv, name):
    _, r, c = g.shape
    hr = r // 2
    tr = _row_tile(hr)
    nrt = hr // tr

    def body(place_ref, g_ref, r_ref, sb_ref, own_ref):
        v = g_ref[...] + r_ref[...]
        sb_ref[...] = v.astype(BF16)

        @pl.when(pl.program_id(1) == place_ref[0])
        def _():
            own_ref[...] = v

    grid_spec = pltpu.PrefetchScalarGridSpec(
        num_scalar_prefetch=1, grid=(nrt, N_CHIPS),
        in_specs=[pl.BlockSpec((None, tr, c), lambda i, s, pr: (s, pr[1] * nrt + i, 0)),
                  pl.BlockSpec((None, tr, c), lambda i, s, pr: (s, i, 0))],
        out_specs=[pl.BlockSpec((None, tr, c), lambda i, s, pr: (s, i, 0)), pl.BlockSpec((tr, c), lambda i, s, pr: (i, 0))],
    )
    return pl.pallas_call(
        body, name=name, grid_spec=grid_spec,
        out_shape=[jax.ShapeDtypeStruct((N_CHIPS, hr, c), BF16), jax.ShapeDtypeStruct((hr, c), F32)],
        compiler_params=_params(2),
    )(place, g, recv)


def _add_chips(place, own, recv, name):
    hr, c = own.shape
    tr = _row_tile(hr)
    nrt = hr // tr

    def body(place_ref, o_ref, r_ref, t_ref):
        t_ref[...] = ((o_ref[...] + r_ref[0].astype(F32)) + r_ref[1].astype(F32)) + r_ref[2].astype(F32)

    grid_spec = pltpu.PrefetchScalarGridSpec(
        num_scalar_prefetch=1, grid=(nrt,),
        in_specs=[pl.BlockSpec((tr, c), lambda i, pr: (i, 0)), pl.BlockSpec((3, tr, c), lambda i, pr: (0, i, 0))],
        out_specs=pl.BlockSpec((tr, c), lambda i, pr: (pr[1] * nrt + i, 0)),
    )
    return pl.pallas_call(
        body, name=name, grid_spec=grid_spec, out_shape=jax.ShapeDtypeStruct((2 * hr, c), F32), compiler_params=_params(1),
    )(place, own, recv)


def _adamw(w, gs, m, v, name):
    L, r, c = w.shape
    tr = _row_tile(r)
    assert len(gs) == L

    def body(*refs):
        w_ref, m_ref, v_ref = refs[:3]
        g_refs = refs[3:3 + L]
        go_ref, d_ref, nm_ref, nv_ref = refs[3 + L:]
        gg = g_refs[0][...]
        for k in range(1, L):
            gg = jnp.where(pl.program_id(0) == k, g_refs[k][...], gg)
        go_ref[...] = gg
        nm = ADAM_B1 * m_ref[...] + (1.0 - ADAM_B1) * gg
        nv = ADAM_B2 * v_ref[...] + (1.0 - ADAM_B2) * (gg * gg)
        m_hat = nm / (1.0 - ADAM_B1 ** ADAM_STEP)
        v_hat = nv / (1.0 - ADAM_B2 ** ADAM_STEP)
        d_ref[...] = -ADAM_LR * (m_hat / (jnp.sqrt(v_hat) + ADAM_EPS) + ADAM_WD * w_ref[...])
        nm_ref[...] = nm
        nv_ref[...] = nv

    spec = pl.BlockSpec((None, tr, c), lambda l, i: (l, i, 0))
    return pl.pallas_call(
        body, name=name, grid=(L, r // tr), in_specs=[spec] * 3 + [pl.BlockSpec((tr, c), lambda l, i: (i, 0))] * L,
        out_specs=[spec] * 4, out_shape=[jax.ShapeDtypeStruct((L, r, c), F32)] * 4, compiler_params=_params(2),
    )(w, m, v, *gs)


def _place():
    return lax.axis_index("x"), lax.axis_index("y"), lax.axis_index("c")


def _other_chips(x, y):
    return [(1 - x, y), (x, 1 - y), (1 - x, 1 - y)]


def _remote(src, dst, ssem, rsem, dev):
    return pltpu.make_async_remote_copy(src_ref=src, dst_ref=dst, send_sem=ssem, recv_sem=rsem, device_id=dev,
                                        device_id_type=MESH)


def _gather_phases(n):
    def geometry(refs, t):
        x, y, c = _place()
        hr = refs[t].shape[1] // 2
        chips = _other_chips(x, y)
        return x, y, c, 2 * x + y, chips, [2 * qx + qy for qx, qy in chips], pl.ds(c * hr, hr), pl.ds((1 - c) * hr, hr)

    def send(refs, sems):
        s1, r1, _, _ = sems
        for t in range(n):
            x, y, c, p, chips, cidx, mine, theirs = geometry(refs, t)
            for j, (qx, qy) in enumerate(chips):
                _remote(refs[t].at[p, mine], refs[t].at[p, mine], s1.at[t, j], r1.at[t, j], (qx, qy, c)).start()

    def pass_on(refs, sems):
        s1, r1, s2, r2 = sems
        for t in range(n):
            x, y, c, p, chips, cidx, mine, theirs = geometry(refs, t)
            for j, (qx, qy) in enumerate(chips):
                got = refs[t].at[cidx[j], mine]
                _remote(got, got, s1.at[t, j], r1.at[t, j], (qx, qy, c)).wait_recv()
                _remote(got, got, s2.at[t, j], r2.at[t, j], (x, y, 1 - c)).start()

    def finish(refs, sems):
        s1, r1, s2, r2 = sems
        for t in range(n):
            x, y, c, p, chips, cidx, mine, theirs = geometry(refs, t)
            for j, (qx, qy) in enumerate(chips):
                got = refs[t].at[cidx[j], theirs]
                _remote(got, got, s2.at[t, j], r2.at[t, j], (x, y, 1 - c)).wait_recv()
        for t in range(n):
            x, y, c, p, chips, cidx, mine, theirs = geometry(refs, t)
            for j, (qx, qy) in enumerate(chips):
                _remote(refs[t].at[p, mine], refs[t].at[p, mine], s1.at[t, j], r1.at[t, j], (qx, qy, c)).wait_send()
                sent = refs[t].at[cidx[j], mine]
                _remote(sent, sent, s2.at[t, j], r2.at[t, j], (x, y, 1 - c)).wait_send()

    sem = pltpu.SemaphoreType.DMA
    return send, pass_on, finish, [sem((n, 3)), sem((n, 3)), sem((n, 3)), sem((n, 3))]


def _gather_carry(slabs):
    send, pass_on, finish, sems = _gather_phases(len(slabs))
    return _Carry(slabs, sems, [(0, send), (-2, pass_on), (-1, finish)])


def _all_gather_slabs(slabs):
    n = len(slabs)
    send, pass_on, finish, sems = _gather_phases(n)

    def body(*refs):
        outs, scratch = refs[n:2 * n], refs[2 * n:]
        send(outs, scratch)
        pass_on(outs, scratch)
        finish(outs, scratch)

    return pl.pallas_call(
        body, name="gather_weights", in_specs=[HBM_SPEC] * n, out_specs=[HBM_SPEC] * n,
        out_shape=[jax.ShapeDtypeStruct(s.shape, s.dtype) for s in slabs], input_output_aliases={t: t for t in range(n)},
        scratch_shapes=sems, compiler_params=pltpu.CompilerParams(has_side_effects=True),
    )(*slabs)


def _sibling_phases(n):
    def copies(refs, sems):
        ssem, rsem = sems
        x, y, c = _place()
        out = []
        for t in range(n):
            hr = refs[t].shape[1] // 2
            out.append(_remote(refs[t].at[:, pl.ds((1 - c) * hr, hr), :], refs[n + t], ssem.at[t], rsem.at[t], (x, y, 1 - c)))
        return out

    def send(refs, sems):
        for cp in copies(refs, sems):
            cp.start()

    def finish(refs, sems):
        for cp in copies(refs, sems):
            cp.wait()

    sem = pltpu.SemaphoreType.DMA
    return send, finish, [sem((n,)), sem((n,))]


def _halves_landing(grads):
    return [lax.empty((N_CHIPS, g.shape[1] // 2, g.shape[2]), g.dtype) for g in grads]


def _sibling_carry(grads):
    send, finish, sems = _sibling_phases(len(grads))
    return _Carry(list(grads) + _halves_landing(grads), sems, [(0, send), (-1, finish)])


def _exchange_with_sibling(grads, tag):
    n = len(grads)
    send, finish, sems = _sibling_phases(n)

    def body(*refs):
        both, scratch = refs[:2 * n], refs[2 * n:]
        send(both, scratch)
        finish(both, scratch)

    return pl.pallas_call(
        body, name="grads_to_sibling_" + tag, in_specs=[HBM_SPEC] * n, out_specs=[HBM_SPEC] * n,
        out_shape=[jax.ShapeDtypeStruct((N_CHIPS, g.shape[1] // 2, g.shape[2]), g.dtype) for g in grads],
        scratch_shapes=sems, compiler_params=pltpu.CompilerParams(has_side_effects=True),
    )(*grads)


def _chips_phases(n):
    def copies(refs, sems):
        ssem, rsem = sems
        x, y, c = _place()
        return [_remote(refs[t].at[2 * qx + qy], refs[n + t].at[j], ssem.at[t, j], rsem.at[t, j], (qx, qy, c))
                for t in range(n) for j, (qx, qy) in enumerate(_other_chips(x, y))]

    def send(refs, sems):
        for cp in copies(refs, sems):
            cp.start()

    def finish(refs, sems):
        for cp in copies(refs, sems):
            cp.wait()

    sem = pltpu.SemaphoreType.DMA
    return send, finish, [sem((n, 3)), sem((n, 3))]


def _landing(sums):
    return [lax.empty((3,) + s.shape[1:], s.dtype) for s in sums]


def _chips_carry(sums):
    send, finish, sems = _chips_phases(len(sums))
    return _Carry(list(sums) + _landing(sums), sems, [(0, send), (-1, finish)])


def _share_phases(n):
    def halves(refs, t):
        x, y, c = _place()
        hr = refs[t].shape[0] // 2
        return refs[t].at[pl.ds(c * hr, hr)], refs[t].at[pl.ds((1 - c) * hr, hr)], (x, y, 1 - c)

    def send(refs, sems):
        ssem, rsem = sems
        for t in range(n):
            mine, _, sibling = halves(refs, t)
            _remote(mine, mine, ssem.at[t], rsem.at[t], sibling).start()

    def finish(refs, sems):
        ssem, rsem = sems
        for t in range(n):
            mine, theirs, sibling = halves(refs, t)
            _remote(theirs, theirs, ssem.at[t], rsem.at[t], sibling).wait_recv()
            _remote(mine, mine, ssem.at[t], rsem.at[t], sibling).wait_send()

    sem = pltpu.SemaphoreType.DMA
    return send, finish, [sem((n,)), sem((n,))]


def _share_carry(shards):
    send, finish, sems = _share_phases(len(shards))
    return _Carry(shards, sems, [(0, send), (-1, finish)])


def _share_with_sibling(shards):
    n = len(shards)
    send, finish, sems = _share_phases(n)

    def body(*refs):
        outs, scratch = refs[n:2 * n], refs[2 * n:]
        send(outs, scratch)
        finish(outs, scratch)

    return pl.pallas_call(
        body, name="grads_share", in_specs=[HBM_SPEC] * n, out_specs=[HBM_SPEC] * n,
        out_shape=[jax.ShapeDtypeStruct(s.shape, s.dtype) for s in shards], input_output_aliases={t: t for t in range(n)},
        scratch_shapes=sems, compiler_params=pltpu.CompilerParams(has_side_effects=True),
    )(*shards)


def _all_reduce_small(buf):
    R, C = buf.shape

    def body(in_ref, out_ref, gather, ssem, rsem):
        x, y, c = _place()
        me = 4 * x + 2 * y + c
        gather[me] = in_ref[...]
        flips = [(fx, fy, fc) for fx in (0, 1) for fy in (0, 1) for fc in (0, 1) if fx or fy or fc]
        peers = [(x + fx - 2 * x * fx, y + fy - 2 * y * fy, c + fc - 2 * c * fc) for fx, fy, fc in flips]
        copies = [_remote(in_ref, gather.at[me], ssem.at[k], rsem.at[k], peer) for k, peer in enumerate(peers)]
        for cp in copies:
            cp.start()
        for k, (px, py, pc) in enumerate(peers):
            _remote(in_ref, gather.at[4 * px + 2 * py + pc], ssem.at[k], rsem.at[k], (px, py, pc)).wait_recv()
        for cp in copies:
            cp.wait_send()
        acc = gather[0]
        for d in range(1, 8):
            acc = acc + gather[d]
        out_ref[...] = acc

    sem = pltpu.SemaphoreType.DMA
    vmem = pl.BlockSpec(memory_space=pltpu.VMEM)
    return pl.pallas_call(
        body, name="small_grads_sum", in_specs=[vmem], out_specs=vmem, out_shape=jax.ShapeDtypeStruct((R, C), F32),
        scratch_shapes=[pltpu.VMEM((8, R, C), F32), sem((7,)), sem((7,))],
        compiler_params=pltpu.CompilerParams(has_side_effects=True),
    )(buf)


BIG = ["ret_w_in", "ret_w_out", "att_w_in", "att_w_out", "mlp_w1_0", "mlp_w1_1", "mlp_w2_0", "mlp_w2_1"]
LAYER_OF = {"ret_w_in": ("ret_w_in", 0), "ret_w_out": ("ret_w_out", 0), "att_w_in": ("att_w_in", 0), "att_w_out": ("att_w_out", 0),
            "mlp_w1_0": ("mlp_w1", 0), "mlp_w1_1": ("mlp_w1", 1), "mlp_w2_0": ("mlp_w2", 0), "mlp_w2_1": ("mlp_w2", 1)}
ORDER = ["mix_norm_g", "ret_w_in", "ret_gn_g", "ret_w_out", "att_w_in", "att_rel_bias", "att_w_out", "mlp_norm_g", "mlp_w1", "mlp_w2",
         "final_norm_g"]


def kernel(x, mix_norm_g, ret_w_in, ret_gn_g, ret_w_out, att_w_in, att_rel_bias, att_w_out, mlp_norm_g, mlp_w1, mlp_w2, final_norm_g, loss_target, m_mix_norm_g, m_ret_w_in, m_ret_gn_g, m_ret_w_out, m_att_w_in, m_att_rel_bias, m_att_w_out, m_mlp_norm_g, m_mlp_w1, m_mlp_w2, m_final_norm_g, v_mix_norm_g, v_ret_w_in, v_ret_gn_g, v_ret_w_out, v_att_w_in, v_att_rel_bias, v_att_w_out, v_mlp_norm_g, v_mlp_w1, v_mlp_w2, v_final_norm_g):
    xi, yi, ci = _place()
    chip = 2 * xi + yi
    weights = dict(zip(ORDER, (mix_norm_g, ret_w_in, ret_gn_g, ret_w_out, att_w_in, att_rel_bias, att_w_out, mlp_norm_g, mlp_w1,
                               mlp_w2, final_norm_g)))
    first = dict(zip(ORDER, (m_mix_norm_g, m_ret_w_in, m_ret_gn_g, m_ret_w_out, m_att_w_in, m_att_rel_bias, m_att_w_out,
                             m_mlp_norm_g, m_mlp_w1, m_mlp_w2, m_final_norm_g)))
    second = dict(zip(ORDER, (v_mix_norm_g, v_ret_w_in, v_ret_gn_g, v_ret_w_out, v_att_w_in, v_att_rel_bias, v_att_w_out,
                              v_mlp_norm_g, v_mlp_w1, v_mlp_w2, v_final_norm_g)))

    place = jnp.stack([chip, ci]).astype(jnp.int32)
    slab = {n: _into_slab(place, weights[LAYER_OF[n][0]], LAYER_OF[n][1], BF16, name="cast_" + n) for n in BIG}
    slab_rel = _into_slab(place, att_rel_bias, 0, F32, name="slab_rel_bias")

    grad_x, g_big, small = _step(x, loss_target, slab, slab_rel, place, mix_norm_g, ret_gn_g[0], mlp_norm_g, final_norm_g)

    rows, at = jnp.zeros((16, D_MODEL), F32), 0
    for part in small:
        rows = rows + jnp.pad(part, ((at, 16 - at - part.shape[0]), (0, D_MODEL - part.shape[1])))
        at += part.shape[0]
    rows = _all_reduce_small(rows)
    loss = rows[12, 0]
    grads = {"mix_norm_g": [rows[0:2]], "mlp_norm_g": [rows[2:4]], "final_norm_g": [rows[4:5]], "ret_gn_g": [rows[5:7].reshape(1, 2048)],
             "att_rel_bias": [lax.dynamic_slice_in_dim(rows[7:12].reshape(ATT_HEADS, REL_TABLE), chip * (REL_TABLE // N_CHIPS),
                                                       REL_TABLE // N_CHIPS, axis=1)]}
    for n in BIG:
        grads.setdefault(LAYER_OF[n][0], []).append(g_big[n])

    def as3(a):
        return a.reshape((1,) * (3 - a.ndim) + a.shape)

    results = {}
    for n in ORDER:
        outs = _adamw(as3(weights[n]), grads[n], as3(first[n]), as3(second[n]), name="adamw_" + n)
        results[n] = [o.reshape(weights[n].shape) for o in outs]
    return (loss, grad_x) + tuple(results[n][k] for k in range(4) for n in ORDER)
```

```python
import math

import jax
import jax.numpy as jnp
from jax import lax
from jax.experimental import pallas as pl
from jax.experimental.pallas import tpu as pltpu

F32 = jnp.float32
BF16 = jnp.bfloat16
MESH = pl.DeviceIdType.MESH

D_MODEL = 1024
CHUNK = 64
RET_HEADS = 4
RET_DK = 256
RET_DV = 512
ROPE_BASE = 10000.0
ATT_HEADS = 16
ATT_DH = 64
PAST = 512
MAX_REL = 256
REL_TABLE = MAX_REL + CHUNK
EPS = 1e-6
NEG = -1e30
N_CHIPS = 4

ADAM_LR = 0.001
ADAM_B1 = 0.9
ADAM_B2 = 0.999
ADAM_EPS = 1e-08
ADAM_WD = 0.01
ADAM_STEP = 10

RET_BLOCK = 256
ATT_BLOCK = 256
VMEM_LIMIT = 56 * 1024 * 1024


def _params(n_axes, **kw):
    return pltpu.CompilerParams(dimension_semantics=("arbitrary",) * n_axes, vmem_limit_bytes=VMEM_LIMIT, **kw)


def _dot(a, b):
    return jnp.dot(a, b, preferred_element_type=F32)


def _dot_nt(a, b):
    return lax.dot_general(a, b, (((1,), (1,)), ((), ())), preferred_element_type=F32)


def _dot_tn(a, b):
    return lax.dot_general(a, b, (((0,), (0,)), ((), ())), preferred_element_type=F32)


def _sigmoid(x):
    return 0.5 * jnp.tanh(0.5 * x) + 0.5


HBM_SPEC = pl.BlockSpec(memory_space=pltpu.HBM)


class _Carry:
    def __init__(self, arrays, sems, stages):
        self.arrays, self.sems, self.stages = list(arrays), list(sems), list(stages)


def _merge(a, b):
    na, sa = len(a.arrays), len(a.sems)

    def of_a(fn):
        return lambda refs, sems: fn(refs[:na], sems[:sa])

    def of_b(fn):
        return lambda refs, sems: fn(refs[na:], sems[sa:])

    return _Carry(a.arrays + b.arrays, a.sems + b.sems,
                  [(at, of_a(fn)) for at, fn in a.stages] + [(at, of_b(fn)) for at, fn in b.stages])


def _carry_call(body, carry, *, name, grid, in_specs, out_specs, out_shape, args):
    if carry is None:
        outs = pl.pallas_call(body, name=name, grid=grid, in_specs=in_specs, out_specs=out_specs, out_shape=out_shape,
                              compiler_params=_params(len(grid)))(*args)
        return list(outs), []
    n_in, n_out, n_c = len(in_specs), len(out_specs), len(carry.arrays)
    steps = 1
    for g in grid:
        steps *= g
    assert all(-steps <= at < steps for at, _ in carry.stages)

    def carrying(*refs):
        ins, outs = refs[:n_in], refs[n_in + n_c:n_in + n_c + n_out]
        carried = refs[n_in + n_c + n_out:n_in + 2 * n_c + n_out]
        sems = refs[n_in + 2 * n_c + n_out:]
        step = pl.program_id(0)
        for axis in range(1, len(grid)):
            step = step * grid[axis] + pl.program_id(axis)
        for at, fn in carry.stages:
            if at == 0:
                pl.when(step == 0)(lambda fn=fn: fn(carried, sems))
        body(*ins, *outs)
        for at, fn in carry.stages:
            if at != 0:
                pl.when(step == at % steps)(lambda fn=fn: fn(carried, sems))

    outs = pl.pallas_call(
        carrying, name=name, grid=grid, in_specs=list(in_specs) + [HBM_SPEC] * n_c, out_specs=list(out_specs) + [HBM_SPEC] * n_c,
        out_shape=list(out_shape) + [jax.ShapeDtypeStruct(a.shape, a.dtype) for a in carry.arrays],
        input_output_aliases={n_in + t: n_out + t for t in range(n_c)}, scratch_shapes=carry.sems,
        compiler_params=_params(len(grid), has_side_effects=True),
    )(*args, *carry.arrays)
    return list(outs[:n_out]), list(outs[n_out:])


def _mm_nn(a, w, wkind, *, tm, out_dtype, name, norm_g=None, act=None, square=False, res=None, loss=None, carry=None):
    M, K = a.shape
    cols = w.shape[2]
    N = N_CHIPS * cols if wkind == "col" else cols
    has_norm = norm_g is not None
    steps = M // tm
    assert M % tm == 0 and K == (w.shape[1] if wkind == "col" else N_CHIPS * w.shape[1])
    assert loss is None or (wkind == "row" and act is None)

    def body(*refs):
        it = iter(refs)
        a_ref, w_ref = next(it), next(it)
        g_ref = next(it) if has_norm else None
        r_ref = next(it) if res is not None else None
        if loss is not None:
            t_ref, fg_ref = next(it), next(it)
        o_ref = next(it)
        hn_ref = next(it) if has_norm else None
        if loss is not None:
            dg_ref, sq_ref, loss_ref = next(it), next(it), next(it)
        if has_norm:
            x = a_ref[...].astype(F32)
            r = lax.rsqrt(jnp.mean(x * x, axis=-1, keepdims=True) + EPS)
            lhs = (x * r * g_ref[...]).astype(BF16)
            hn_ref[...] = lhs
        elif square:
            lhs = a_ref[...].astype(BF16)
            lhs = lhs * lhs
        else:
            lhs = a_ref[...].astype(BF16)

        def loss_head(x):
            i = pl.program_id(0)
            r = lax.rsqrt(jnp.mean(x * x, axis=-1, keepdims=True) + EPS)
            xh = x * r
            gg = fg_ref[...]
            diff = xh * gg - t_ref[...]
            sq = jnp.sum(diff * diff, axis=0, keepdims=True)
            dy = diff * (1.0 / N)
            dg_part = jnp.sum(dy * xh, axis=0, keepdims=True)

            @pl.when(i == 0)
            def _():
                sq_ref[...] = sq
                dg_ref[...] = dg_part

            @pl.when(i > 0)
            def _():
                sq_ref[...] += sq
                dg_ref[...] += dg_part
            t = dy * gg
            o_ref[...] = r * (t - xh * jnp.mean(t * xh, axis=-1, keepdims=True))

            @pl.when(i == steps - 1)
            def _():
                loss_ref[...] = (0.5 / N) * jnp.sum(sq_ref[...], axis=1, keepdims=True)

        def finish(acc, sl):
            if act == "relu":
                acc = jnp.maximum(acc, 0.0)
            if r_ref is not None:
                acc = acc + r_ref[:, sl]
            if loss is not None:
                loss_head(acc)
            else:
                o_ref[:, sl] = acc.astype(out_dtype)

        if wkind == "col":
            for s in range(N_CHIPS):
                finish(_dot(lhs, w_ref[s]), slice(s * cols, (s + 1) * cols))
        else:
            finish(_dot(lhs, w_ref[...].reshape(K, N)), slice(None))

    in_specs = [pl.BlockSpec((tm, K), lambda i: (i, 0)), pl.BlockSpec(w.shape, lambda i: (0, 0, 0))]
    args = [a, w]
    if has_norm:
        in_specs.append(pl.BlockSpec((1, K), lambda i: (0, 0)))
        args.append(norm_g.reshape(1, K))
    if res is not None:
        in_specs.append(pl.BlockSpec((tm, N), lambda i: (i, 0)))
        args.append(res)
    if loss is not None:
        in_specs += [pl.BlockSpec((tm, N), lambda i: (i, 0)), pl.BlockSpec((1, N), lambda i: (0, 0))]
        args += [loss[0], loss[1].reshape(1, N)]
    out_shape = [jax.ShapeDtypeStruct((M, N), out_dtype)]
    out_specs = [pl.BlockSpec((tm, N), lambda i: (i, 0))]
    if has_norm:
        out_shape.append(jax.ShapeDtypeStruct((M, K), BF16))
        out_specs.append(pl.BlockSpec((tm, K), lambda i: (i, 0)))
    if loss is not None:
        out_shape += [jax.ShapeDtypeStruct((1, N), F32), jax.ShapeDtypeStruct((1, N), F32), jax.ShapeDtypeStruct((1, 1), F32)]
        out_specs += [pl.BlockSpec((1, N), lambda i: (0, 0)), pl.BlockSpec((1, N), lambda i: (0, 0)),
                      pl.BlockSpec((1, 1), lambda i: (0, 0))]
    outs, carried = _carry_call(body, carry, name=name, grid=(steps,), in_specs=in_specs, out_specs=out_specs,
                                out_shape=out_shape, args=args)
    result = outs if has_norm or loss is not None else outs[0]
    return result if carry is None else (result, carried)


def _mm_nt(a, w, wkind, *, tm, name, epi, a2=None, h=None, g=None, dres=None, carry=None):
    parts = tuple(a) if isinstance(a, (tuple, list)) else (a,)
    M, part_w = parts[0].shape
    Nw = part_w * len(parts)
    rows, cols = w.shape[1], w.shape[2]
    Kw = rows if wkind == "col" else N_CHIPS * rows
    assert M % tm == 0 and Nw == (N_CHIPS * cols if wkind == "col" else cols)
    assert epi != "normbwd" or wkind == "col"
    assert len(parts) == 1 or wkind == "col"
    chunk = math.gcd(part_w, cols)

    def body(*refs):
        it = iter(refs)
        a_refs = [next(it) for _ in parts]
        a_ref, w_ref = a_refs[0], next(it)
        a2_ref = next(it) if epi == "relu2bwd" else None
        if epi == "normbwd":
            h_ref, g_ref, dres_ref = next(it), next(it), next(it)
        o_ref = next(it)
        dg_ref = next(it) if epi == "normbwd" else None
        i = pl.program_id(0)

        def finish(acc, sl):
            if epi == "bf16":
                o_ref[:, sl] = acc.astype(BF16)
            elif epi == "relu2bwd":
                o_ref[:, sl] = (acc * (2.0 * a2_ref[:, sl].astype(F32))).astype(BF16)
            else:
                x = h_ref[...]
                r = lax.rsqrt(jnp.mean(x * x, axis=-1, keepdims=True) + EPS)
                xh = x * r
                dg_part = jnp.sum(acc * xh, axis=0, keepdims=True)

                @pl.when(i == 0)
                def _():
                    dg_ref[...] = dg_part

                @pl.when(i > 0)
                def _():
                    dg_ref[...] += dg_part
                t = acc * g_ref[...]
                dx = r * (t - xh * jnp.mean(t * xh, axis=-1, keepdims=True))
                o_ref[...] = dres_ref[...] + dx

        if wkind == "col":
            acc = None
            for lo in range(0, Nw, chunk):
                (src, a_lo), (s, w_lo) = divmod(lo, part_w), divmod(lo, cols)
                part = _dot_nt(a_refs[src][:, a_lo:a_lo + chunk].astype(BF16), w_ref[s, :, w_lo:w_lo + chunk])
                acc = part if acc is None else acc + part
            finish(acc, slice(None))
        else:
            lhs = a_ref[...].astype(BF16)
            for s in range(N_CHIPS):
                finish(_dot_nt(lhs, w_ref[s]), slice(s * rows, (s + 1) * rows))

    in_specs = [pl.BlockSpec((tm, part_w), lambda i: (i, 0)) for _ in parts] + [pl.BlockSpec(w.shape, lambda i: (0, 0, 0))]
    args = [*parts, w]
    out_dtype = BF16
    if epi == "relu2bwd":
        in_specs.append(pl.BlockSpec((tm, Kw), lambda i: (i, 0)))
        args.append(a2)
    if epi == "normbwd":
        in_specs += [pl.BlockSpec((tm, Kw), lambda i: (i, 0)), pl.BlockSpec((1, Kw), lambda i: (0, 0)),
                     pl.BlockSpec((tm, Kw), lambda i: (i, 0))]
        args += [h, g.reshape(1, Kw), dres]
        out_dtype = F32
    out_shape = [jax.ShapeDtypeStruct((M, Kw), out_dtype)]
    out_specs = [pl.BlockSpec((tm, Kw), lambda i: (i, 0))]
    if epi == "normbwd":
        out_shape.append(jax.ShapeDtypeStruct((1, Kw), F32))
        out_specs.append(pl.BlockSpec((1, Kw), lambda i: (0, 0)))
    outs, carried = _carry_call(body, carry, name=name, grid=(M // tm,), in_specs=in_specs, out_specs=out_specs,
                                out_shape=out_shape, args=args)
    result = outs if epi == "normbwd" else outs[0]
    return result if carry is None else (result, carried)


def _mm_tn(a, b, okind, *, tt, tk, tn, name, square=False, carry=None):
    parts = tuple(b) if isinstance(b, (tuple, list)) else (b,)
    T, K = a.shape
    part_w = parts[0].shape[1]
    N = part_w * len(parts)
    assert T % tt == 0 and K % tk == 0 and part_w % tn == 0
    nt = T // tt
    per_part = part_w // tn
    if okind == "col":
        per = (N // N_CHIPS) // tn
        assert (N // N_CHIPS) % tn == 0
        out_shape = jax.ShapeDtypeStruct((N_CHIPS, K, N // N_CHIPS), F32)
        out_spec = pl.BlockSpec((None, tk, tn), lambda ki, nj, t: (nj // per, ki, nj % per))
    else:
        per = (K // N_CHIPS) // tk
        assert (K // N_CHIPS) % tk == 0
        out_shape = jax.ShapeDtypeStruct((N_CHIPS, K // N_CHIPS, N), F32)
        out_spec = pl.BlockSpec((None, tk, tn), lambda ki, nj, t: (ki // per, ki % per, nj))

    def body(a_ref, *refs):
        b_refs, o_ref = refs[:-1], refs[-1]
        nj, t = pl.program_id(1), pl.program_id(2)

        def accumulate(b_ref):
            lhs = a_ref[...].astype(BF16)
            part = _dot_tn(lhs * lhs if square else lhs, b_ref[...].astype(BF16))
            if nt == 1:
                o_ref[...] = part
            else:
                @pl.when(t == 0)
                def _():
                    o_ref[...] = part

                @pl.when(t > 0)
                def _():
                    o_ref[...] += part

        if len(parts) == 1:
            accumulate(b_refs[0])
        else:
            for s, b_ref in enumerate(b_refs):
                pl.when(nj // per_part == s)(lambda b_ref=b_ref: accumulate(b_ref))

    def b_spec(s):
        return pl.BlockSpec((tt, tn), lambda ki, nj, t: (t, jnp.clip(nj - s * per_part, 0, per_part - 1)))

    outs, carried = _carry_call(
        body, carry, name=name, grid=(K // tk, N // tn, nt),
        in_specs=[pl.BlockSpec((tt, tk), lambda ki, nj, t: (t, ki))] + [b_spec(s) for s in range(len(parts))],
        out_specs=[out_spec], out_shape=[out_shape], args=[a, *parts])
    return outs[0] if carry is None else (outs[0], carried)


def _ret_consts(S, LB):
    log_gamma = jnp.log1p(-jnp.exp2(-5.0 - jnp.arange(RET_HEADS, dtype=F32)))
    idx = jnp.arange(LB, dtype=F32)
    n, m = idx[:, None], idx[None, :]
    cn, cm = jnp.floor(n / CHUNK), jnp.floor(m / CHUNK)
    dist = jnp.where(cm == cn, jnp.abs(n - m), n - m)
    dmat = jnp.where((cm <= cn)[None], jnp.exp(log_gamma[:, None, None] * dist[None]), 0.0)
    qd = jnp.exp(log_gamma[:, None] * (idx + 1.0)[None, :])[..., None]
    kd = jnp.exp(log_gamma[:, None] * (LB - 1 - idx)[None, :])[..., None]
    bd = jnp.exp(log_gamma * LB).reshape(RET_HEADS, 1, 1) * jnp.ones((RET_HEADS, 1, 128), F32)
    half = RET_DK // 2
    inv = jnp.exp(-jnp.log(ROPE_BASE) * jnp.arange(half, dtype=F32) / half)
    ang = jnp.arange(S, dtype=F32)[:, None] * inv[None, :]
    return dmat.astype(F32), qd.astype(F32), kd.astype(F32), bd, jnp.cos(ang), jnp.sin(ang)


def _rope(t, c, s):
    t1, t2 = t[:, :128], t[:, 128:]
    return jnp.concatenate([t1 * c - t2 * s, t1 * s + t2 * c], axis=-1)


def _rope_inv(d, c, s):
    d1, d2 = d[:, :128], d[:, 128:]
    return jnp.concatenate([d1 * c + d2 * s, d2 * c - d1 * s], axis=-1)


def _ret_block_fwd(p_ref, h, c, s, d_ref, qd_ref, kd_ref, stb):
    q = _rope(p_ref[:, h * RET_DK:(h + 1) * RET_DK].astype(F32), c, s)
    k = _rope(p_ref[:, 1024 + h * RET_DK:1024 + (h + 1) * RET_DK].astype(F32), c, s) * (RET_DK ** -0.5)
    v = p_ref[:, 2048 + h * RET_DV:2048 + (h + 1) * RET_DV]
    qb, kb = q.astype(BF16), k.astype(BF16)
    scb = (_dot_nt(qb, kb) * d_ref[h]).astype(BF16)
    o = _dot(scb, v) + qd_ref[h] * _dot(qb, stb)
    return q, k, qb, kb, v, scb, o


def _ret_fwd(proj, gn_g, consts, *, B, S):
    LB = RET_BLOCK
    nb = S // LB
    T = B * S
    dmat, qd, kd, bd, cos, sin = consts

    def body(p_ref, cos_ref, sin_ref, d_ref, qd_ref, kd_ref, bd_ref, gng_ref, y_ref, st_ref, state_s):
        i = pl.program_id(1)

        @pl.when(i == 0)
        def _():
            state_s[...] = jnp.zeros_like(state_s)
        c, s = cos_ref[...], sin_ref[...]
        for h in range(RET_HEADS):
            st = state_s[h]
            stb = st.astype(BF16)
            st_ref[h] = stb
            q, k, qb, kb, v, scb, o = _ret_block_fwd(p_ref, h, c, s, d_ref, qd_ref, kd_ref, stb)
            kdk = (k * kd_ref[h]).astype(BF16)
            state_s[h] = st * bd_ref[h][:, :1] + _dot_tn(kdk, v)
            gate = p_ref[:, 4096 + h * RET_DV:4096 + (h + 1) * RET_DV].astype(F32)
            mu = jnp.mean(o, axis=-1, keepdims=True)
            oc = o - mu
            xh = oc * lax.rsqrt(jnp.mean(oc * oc, axis=-1, keepdims=True) + EPS)
            y = (gate * _sigmoid(gate)) * (xh * gng_ref[:, h * RET_DV:(h + 1) * RET_DV])
            y_ref[:, h * RET_DV:(h + 1) * RET_DV] = y.astype(BF16)

    const = lambda b, i: (0, 0, 0)
    return pl.pallas_call(
        body, name="ret_fwd", grid=(B, nb),
        in_specs=[pl.BlockSpec((LB, 6144), lambda b, i: (b * nb + i, 0)),
                  pl.BlockSpec((LB, 128), lambda b, i: (i, 0)), pl.BlockSpec((LB, 128), lambda b, i: (i, 0)),
                  pl.BlockSpec((RET_HEADS, LB, LB), const), pl.BlockSpec((RET_HEADS, LB, 1), const),
                  pl.BlockSpec((RET_HEADS, LB, 1), const), pl.BlockSpec((RET_HEADS, 1, 128), const),
                  pl.BlockSpec((1, 2048), lambda b, i: (0, 0))],
        out_specs=[pl.BlockSpec((LB, 2048), lambda b, i: (b * nb + i, 0)),
                   pl.BlockSpec((None, None, RET_HEADS, RET_DK, RET_DV), lambda b, i: (b, i, 0, 0, 0))],
        out_shape=[jax.ShapeDtypeStruct((T, 2048), BF16), jax.ShapeDtypeStruct((B, nb, RET_HEADS, RET_DK, RET_DV), BF16)],
        scratch_shapes=[pltpu.VMEM((RET_HEADS, RET_DK, RET_DV), F32)], compiler_params=_params(2),
    )(proj, cos, sin, dmat, qd, kd, bd, gn_g.reshape(1, 2048))


def _ret_bwd(proj, dy, states, gn_g, consts, *, B, S):
    LB = RET_BLOCK
    nb = S // LB
    T = B * S
    dmat, qd, kd, bd, cos, sin = consts

    def body(p_ref, dy_ref, st_ref, cos_ref, sin_ref, d_ref, qd_ref, kd_ref, bd_ref, gng_ref, dp_ref, dgn_ref, dstate_s):
        b, i = pl.program_id(0), pl.program_id(1)

        @pl.when(i == 0)
        def _():
            dstate_s[...] = jnp.zeros_like(dstate_s)

        @pl.when((b == 0) & (i == 0))
        def _():
            dgn_ref[...] = jnp.zeros_like(dgn_ref)
        c, s = cos_ref[...], sin_ref[...]
        for h in range(RET_HEADS):
            vs = slice(h * RET_DV, (h + 1) * RET_DV)
            stb = st_ref[h]
            q, k, qb, kb, v, scb, o = _ret_block_fwd(p_ref, h, c, s, d_ref, qd_ref, kd_ref, stb)
            gate = p_ref[:, 4096 + h * RET_DV:4096 + (h + 1) * RET_DV].astype(F32)
            mu = jnp.mean(o, axis=-1, keepdims=True)
            oc = o - mu
            rstd = lax.rsqrt(jnp.mean(oc * oc, axis=-1, keepdims=True) + EPS)
            xh = oc * rstd
            gng = gng_ref[:, vs]
            dyh = dy_ref[:, vs].astype(F32)
            sg = _sigmoid(gate)
            silu = gate * sg
            dgn_ref[:, vs] += jnp.sum(dyh * silu * xh, axis=0, keepdims=True)
            dxh = dyh * silu * gng
            do = rstd * (dxh - jnp.mean(dxh, axis=-1, keepdims=True) - xh * jnp.mean(dxh * xh, axis=-1, keepdims=True))
            dgate = dyh * xh * gng * (sg * (1.0 + gate * (1.0 - sg)))
            dob = do.astype(BF16)
            dsb = (_dot_nt(dob, v) * d_ref[h]).astype(BF16)
            dst = dstate_s[h]
            dstb = dst.astype(BF16)
            kdk = (k * kd_ref[h]).astype(BF16)
            dqr = _dot(dsb, kb) + qd_ref[h] * _dot_nt(dob, stb)
            dkr = _dot_tn(dsb, qb) + kd_ref[h] * _dot_nt(v, dstb)
            dv = _dot_tn(scb, dob) + _dot(kdk, dstb)
            dstate_s[h] = dst * bd_ref[h][:, :1] + _dot_tn((q * qd_ref[h]).astype(BF16), dob)
            dp_ref[:, h * RET_DK:(h + 1) * RET_DK] = _rope_inv(dqr, c, s).astype(BF16)
            dp_ref[:, 1024 + h * RET_DK:1024 + (h + 1) * RET_DK] = (_rope_inv(dkr, c, s) * (RET_DK ** -0.5)).astype(BF16)
            dp_ref[:, 2048 + h * RET_DV:2048 + (h + 1) * RET_DV] = dv.astype(BF16)
            dp_ref[:, 4096 + h * RET_DV:4096 + (h + 1) * RET_DV] = dgate.astype(BF16)

    const = lambda b, i: (0, 0, 0)
    rev = lambda b, i: (b * nb + nb - 1 - i, 0)
    return pl.pallas_call(
        body, name="ret_bwd", grid=(B, nb),
        in_specs=[pl.BlockSpec((LB, 6144), rev), pl.BlockSpec((LB, 2048), rev),
                  pl.BlockSpec((None, None, RET_HEADS, RET_DK, RET_DV), lambda b, i: (b, nb - 1 - i, 0, 0, 0)),
                  pl.BlockSpec((LB, 128), lambda b, i: (nb - 1 - i, 0)), pl.BlockSpec((LB, 128), lambda b, i: (nb - 1 - i, 0)),
                  pl.BlockSpec((RET_HEADS, LB, LB), const), pl.BlockSpec((RET_HEADS, LB, 1), const),
                  pl.BlockSpec((RET_HEADS, LB, 1), const), pl.BlockSpec((RET_HEADS, 1, 128), const),
                  pl.BlockSpec((1, 2048), lambda b, i: (0, 0))],
        out_specs=[pl.BlockSpec((LB, 6144), rev), pl.BlockSpec((1, 2048), lambda b, i: (0, 0))],
        out_shape=[jax.ShapeDtypeStruct((T, 6144), BF16), jax.ShapeDtypeStruct((1, 2048), F32)],
        scratch_shapes=[pltpu.VMEM((RET_HEADS, RET_DK, RET_DV), F32)], compiler_params=_params(2),
    )(proj, dy, states, cos, sin, dmat, qd, kd, bd, gn_g.reshape(1, 2048))


BIAS_LANES = 4 * ATT_BLOCK


def _diag_onehot():
    r = lax.broadcasted_iota(jnp.int32, (REL_TABLE, BIAS_LANES), 0)
    j = lax.broadcasted_iota(jnp.int32, (REL_TABLE, BIAS_LANES), 1)
    idx = jnp.maximum(j - ATT_BLOCK - PAST, -MAX_REL) + MAX_REL
    return jnp.where(idx == r, 1.0, 0.0).astype(F32)


def _row_is(j):
    return lax.broadcasted_iota(jnp.int32, (8, BIAS_LANES), 0) == j


def _att_bias(table):
    QB, KW = ATT_BLOCK, 3 * ATT_BLOCK

    def body(t_ref, bt_ref):
        row = jnp.broadcast_to(t_ref[...], (8, REL_TABLE))
        diag = jnp.dot(row, _diag_onehot(), preferred_element_type=F32, precision=lax.Precision.HIGHEST)
        rows = jnp.zeros((8, BIAS_LANES), F32)
        for j in range(8):
            rows = jnp.where(_row_is(j), diag if j == 0 else pltpu.roll(diag, j, axis=1), rows)
        n = 8
        while n < QB:
            rows = jnp.concatenate([rows, pltpu.roll(rows, n, axis=1)], axis=0)
            n *= 2
        bias = rows[:, QB:]
        qi = lax.broadcasted_iota(jnp.int32, (QB, KW), 0)
        kj = lax.broadcasted_iota(jnp.int32, (QB, KW), 1)
        lo = (qi // CHUNK) * CHUNK
        bt_ref[...] = jnp.where((kj >= lo) & (kj < lo + PAST + CHUNK), bias, NEG).T

    return pl.pallas_call(
        body, name="att_bias", grid=(ATT_HEADS,),
        in_specs=[pl.BlockSpec((None, 1, REL_TABLE), lambda h: (h, 0, 0))],
        out_specs=pl.BlockSpec((None, KW, QB), lambda h: (h // 2, 0, h % 2)),
        out_shape=jax.ShapeDtypeStruct((ATT_HEADS // 2, KW, 2 * QB), F32),
        compiler_params=_params(1),
    )(table.reshape(ATT_HEADS, 1, REL_TABLE))


def _att_bias_grad(dbias_t):
    QB, KW = ATT_BLOCK, 3 * ATT_BLOCK

    def body(d_ref, o_ref):
        rows = jnp.concatenate([jnp.zeros((QB, QB), F32), d_ref[...].T], axis=1)
        n = QB // 2
        while n >= 8:
            rows = rows[:n] + pltpu.roll(rows[n:], BIAS_LANES - n, axis=1)
            n //= 2
        acc = jnp.zeros((8, BIAS_LANES), F32)
        for j in range(8):
            acc = acc + jnp.where(_row_is(j), rows if j == 0 else pltpu.roll(rows, BIAS_LANES - j, axis=1), 0.0)
        diag = jnp.broadcast_to(jnp.sum(acc, axis=0, keepdims=True), (8, BIAS_LANES))
        grad = lax.dot_general(diag, _diag_onehot(), (((1,), (1,)), ((), ())), preferred_element_type=F32,
                               precision=lax.Precision.HIGHEST)
        o_ref[...] = grad[:1]

    return pl.pallas_call(
        body, name="att_bias_grad", grid=(ATT_HEADS,),
        in_specs=[pl.BlockSpec((None, KW, QB), lambda h: (h // 2, 0, h % 2))],
        out_specs=pl.BlockSpec((None, 1, REL_TABLE), lambda h: (h, 0, 0)),
        out_shape=jax.ShapeDtypeStruct((ATT_HEADS, 1, REL_TABLE), F32), compiler_params=_params(1),
    )(dbias_t).reshape(ATT_HEADS, REL_TABLE)


def _by_head(x):
    first = lax.broadcasted_iota(jnp.int32, x.shape, 1) < ATT_DH
    zero = jnp.zeros_like(x)
    return jnp.concatenate([jnp.where(first, x, zero), jnp.where(first, zero, x)], axis=0)


def _att_fwd(qkv, bias_t, *, B, S):
    QB = ATT_BLOCK
    nb = S // QB
    KW = 3 * QB
    T = B * S
    scale = ATT_DH ** -0.5

    def body(q_ref, k0, k1, k2, v0, v1, v2, b_ref, o_ref, lse_ref, s_blk):
        i = pl.program_id(2)
        q2 = _by_head((q_ref[...].astype(F32) * scale).astype(BF16))
        m = jnp.full((1, 2 * QB), NEG, F32)
        for d, k_ref in enumerate((k0, k1, k2)):
            st = _dot_nt(k_ref[...], q2) + b_ref[d * QB:(d + 1) * QB, :]
            st = jnp.where(i + d >= 2, st, NEG)
            s_blk[d] = st
            m = jnp.maximum(m, jnp.max(st, axis=0, keepdims=True))
        l = jnp.zeros((1, 2 * QB), F32)
        o_t = jnp.zeros((128, 2 * QB), F32)
        for d, v_ref in enumerate((v0, v1, v2)):
            e = jnp.exp(s_blk[d] - m)
            l = l + jnp.sum(e, axis=0, keepdims=True)
            o_t = o_t + _dot(v_ref[...].astype(F32).T.astype(BF16), e.astype(BF16))
        o_t = o_t / l
        row = lax.broadcasted_iota(jnp.int32, (128, QB), 0)
        o_ref[...] = jnp.where(row < ATT_DH, o_t[:, :QB], o_t[:, QB:]).T.astype(BF16)
        lse = m + jnp.log(l)
        row8 = lax.broadcasted_iota(jnp.int32, (8, QB), 0)
        lse_ref[...] = jnp.where(row8 == 0, lse[:, :QB], jnp.where(row8 == 1, lse[:, QB:], 0.0))

    def kv(d, col0):
        return pl.BlockSpec((QB, 128), lambda hp, b, i: (b * nb + jnp.maximum(i - d, 0), col0 + hp))

    return pl.pallas_call(
        body, name="att_fwd", grid=(8, B, nb),
        in_specs=[pl.BlockSpec((QB, 128), lambda hp, b, i: (b * nb + i, hp)),
                  kv(2, 8), kv(1, 8), kv(0, 8), kv(2, 16), kv(1, 16), kv(0, 16),
                  pl.BlockSpec((None, KW, 2 * QB), lambda hp, b, i: (hp, 0, 0))],
        out_specs=[pl.BlockSpec((QB, 128), lambda hp, b, i: (b * nb + i, hp)),
                   pl.BlockSpec((None, 8, QB), lambda hp, b, i: (hp, 0, b * nb + i))],
        out_shape=[jax.ShapeDtypeStruct((T, 1024), BF16), jax.ShapeDtypeStruct((8, 8, T), F32)],
        scratch_shapes=[pltpu.VMEM((3, QB, 2 * QB), F32)], compiler_params=_params(3),
    )(qkv, qkv, qkv, qkv, qkv, qkv, qkv, bias_t)


def _att_bwd(qkv, do, o, lse, bias_t, *, B, S):
    QB = ATT_BLOCK
    nb = S // QB
    KW = 3 * QB
    T = B * S
    scale = ATT_DH ** -0.5
    TK = 256

    def body(q_ref, k0, k1, k2, v0, v1, v2, do_ref, o_ref, lse_ref, b_ref, dq_ref, dk_ref, dv_ref, db_ref, dk_acc, dv_acc):
        b, i = pl.program_id(1), pl.program_id(2)

        @pl.when(i == 0)
        def _():
            dk_acc[...] = jnp.zeros_like(dk_acc)
            dv_acc[...] = jnp.zeros_like(dv_acc)

        @pl.when((b == 0) & (i == 0))
        def _():
            db_ref[...] = jnp.zeros_like(db_ref)

        def write_out():
            slot = (i + 1) % 3
            dk_ref[...] = dk_acc[slot].astype(BF16)
            dv_ref[...] = dv_acc[slot].astype(BF16)
            dk_acc[slot] = jnp.zeros((QB, 128), F32)
            dv_acc[slot] = jnp.zeros((QB, 128), F32)

        @pl.when(i < nb)
        def _():
            dout = do_ref[...]
            q2 = _by_head((q_ref[...].astype(F32) * scale).astype(BF16))
            do2 = _by_head(dout)
            delta_t = (o_ref[...].astype(F32) * dout.astype(F32)).T
            delta2 = jnp.concatenate([jnp.sum(delta_t[:ATT_DH], axis=0, keepdims=True),
                                      jnp.sum(delta_t[ATT_DH:], axis=0, keepdims=True)], axis=1)
            lse2 = jnp.concatenate([lse_ref[0:1, :], lse_ref[1:2, :]], axis=1)
            dq_t = jnp.zeros((128, 2 * QB), F32)
            for d, (k_ref, v_ref) in enumerate(((k0, v0), (k1, v1), (k2, v2))):
                kblk, vblk = k_ref[...], v_ref[...]
                kt = kblk.astype(F32).T.astype(BF16)
                lse_d = jnp.where(i + d >= 2, lse2, -NEG)
                slot = (i + 1 + d) % 3
                for t in range(QB // TK):
                    rows = slice(t * TK, (t + 1) * TK)
                    wrows = slice(d * QB + t * TK, d * QB + (t + 1) * TK)
                    p = jnp.exp(_dot_nt(kblk[rows], q2) + b_ref[wrows, :] - lse_d)
                    ds = p * (_dot_nt(vblk[rows], do2) - delta2)
                    db_ref[wrows, :] += ds
                    dsb = ds.astype(BF16)
                    dk_acc[slot, rows, :] += _dot(dsb, q2)
                    dv_acc[slot, rows, :] += _dot(p.astype(BF16), do2)
                    dq_t += _dot(kt[:, rows], dsb)
            row = lax.broadcasted_iota(jnp.int32, (128, QB), 0)
            dq_ref[...] = (jnp.where(row < ATT_DH, dq_t[:, :QB], dq_t[:, QB:]) * scale).T.astype(BF16)
            write_out()

        pl.when(i >= nb)(write_out)

    def qrow(b, i):
        return b * nb + jnp.minimum(i, nb - 1)

    def kv(d, col0):
        return pl.BlockSpec((QB, 128), lambda hp, b, i: (b * nb + jnp.maximum(jnp.minimum(i, nb - 1) - d, 0), col0 + hp))

    late = pl.BlockSpec((QB, 128), lambda hp, b, i: (b * nb + jnp.maximum(i - 2, 0), hp))
    return pl.pallas_call(
        body, name="att_bwd", grid=(8, B, nb + 2),
        in_specs=[pl.BlockSpec((QB, 128), lambda hp, b, i: (qrow(b, i), hp)),
                  kv(2, 8), kv(1, 8), kv(0, 8), kv(2, 16), kv(1, 16), kv(0, 16),
                  pl.BlockSpec((QB, 128), lambda hp, b, i: (qrow(b, i), hp)),
                  pl.BlockSpec((QB, 128), lambda hp, b, i: (qrow(b, i), hp)),
                  pl.BlockSpec((None, 8, QB), lambda hp, b, i: (hp, 0, qrow(b, i))),
                  pl.BlockSpec((None, KW, 2 * QB), lambda hp, b, i: (hp, 0, 0))],
        out_specs=[pl.BlockSpec((QB, 128), lambda hp, b, i: (qrow(b, i), hp)), late, late,
                   pl.BlockSpec((None, KW, 2 * QB), lambda hp, b, i: (hp, 0, 0))],
        out_shape=[jax.ShapeDtypeStruct((T, 1024), BF16)] * 3 + [jax.ShapeDtypeStruct((ATT_HEADS // 2, KW, 2 * QB), F32)],
        scratch_shapes=[pltpu.VMEM((3, QB, 128), F32), pltpu.VMEM((3, QB, 128), F32)], compiler_params=_params(3),
    )(qkv, qkv, qkv, qkv, qkv, qkv, qkv, do, o, lse, bias_t)


def _tok_tile(T, want):
    t = min(T, want)
    assert T % t == 0
    return t


def _step(x, tgt, slab, slab_rel, place, mix_g, gn_g, mlp_g, fin_g):
    B, S, D = x.shape
    T = B * S
    h0 = x.reshape(T, D)
    tgt = tgt.reshape(T, D)
    tm = _tok_tile(T, 1024)
    tb = _tok_tile(T, 512)
    tq = _tok_tile(T, 256)
    tt = _tok_tile(T, 8192)
    tf = _tok_tile(T, 2048)
    consts = _ret_consts(S, RET_BLOCK)
    w = {}

    (w["ret_w_in"],) = _all_gather_slabs([slab["ret_w_in"]])
    (proj, hn0), (w["ret_w_out"], w["mlp_w1_0"], w["mlp_w2_0"]) = _mm_nn(
        h0, w["ret_w_in"], "col", tm=tb, out_dtype=BF16, name="ret_in", norm_g=mix_g[0],
        carry=_gather_carry([slab["ret_w_out"], slab["mlp_w1_0"], slab["mlp_w2_0"]]))
    y_ret, states = _ret_fwd(proj, gn_g, consts, B=B, S=S)
    h1, (w["att_w_in"], w["att_w_out"], rel_slabs) = _mm_nn(
        y_ret, w["ret_w_out"], "row", tm=tm, out_dtype=F32, name="ret_out", res=h0,
        carry=_gather_carry([slab["att_w_in"], slab["att_w_out"], slab_rel]))
    (a0, hm0), (w["mlp_w1_1"],) = _mm_nn(h1, w["mlp_w1_0"], "col", tm=tb, out_dtype=BF16, name="mlp0_up", norm_g=mlp_g[0],
                                         act="relu", carry=_gather_carry([slab["mlp_w1_1"]]))
    h2, (w["mlp_w2_1"],) = _mm_nn(a0, w["mlp_w2_0"], "row", tm=tb, out_dtype=F32, name="mlp0_down", square=True, res=h1,
                                  carry=_gather_carry([slab["mlp_w2_1"]]))
    rel_bias = jnp.transpose(rel_slabs, (1, 0, 2)).reshape(ATT_HEADS, REL_TABLE)
    bias_t = _att_bias(rel_bias)
    qkv, hn1 = _mm_nn(h2, w["att_w_in"], "col", tm=tb, out_dtype=BF16, name="att_in", norm_g=mix_g[1])
    o_att, lse = _att_fwd(qkv, bias_t, B=B, S=S)
    h3 = _mm_nn(o_att, w["att_w_out"], "row", tm=tm, out_dtype=F32, name="att_out", res=h2)
    a1, hm1 = _mm_nn(h3, w["mlp_w1_1"], "col", tm=tb, out_dtype=BF16, name="mlp1_up", norm_g=mlp_g[1], act="relu")
    dh4, d_fin_g, _, loss = _mm_nn(a1, w["mlp_w2_1"], "row", tm=tb, out_dtype=F32, name="mlp1_down", square=True, res=h3,
                                   loss=(tgt, fin_g))

    gw = {}
    gw["mlp_w2_1"] = _mm_tn(a1, dh4, "row", tt=tf, tk=1024, tn=D, name="d_mlp1_w2", square=True)
    dz1 = _mm_nt(dh4, w["mlp_w2_1"], "row", tm=tb, name="d_mlp1_act", epi="relu2bwd", a2=a1)
    gw["mlp_w1_1"] = _mm_tn(hm1, dz1, "col", tt=tt, tk=D, tn=256, name="d_mlp1_w1")
    dh3, d_mlp_g1 = _mm_nt(dz1, w["mlp_w1_1"], "col", tm=tb, name="d_mlp1_in", epi="normbwd",
                           h=h3, g=mlp_g[1], dres=dh4)
    gw["att_w_out"] = _mm_tn(o_att, dh3, "row", tt=tf, tk=256, tn=D, name="d_att_wout")
    do_att = _mm_nt(dh3, w["att_w_out"], "row", tm=tb, name="d_att_o", epi="bf16")
    dq, dk, dv, dbias_t = _att_bwd(qkv, do_att, o_att, lse, bias_t, B=B, S=S)
    d_rel = _att_bias_grad(dbias_t)
    dqkv = (dq, dk, dv)
    gw["att_w_in"] = _mm_tn(hn1, dqkv, "col", tt=tt, tk=D, tn=256, name="d_att_win")
    sums, landed = {}, {}

    def swap_halves(names):
        return _sibling_carry([gw[n] for n in names])

    def add_halves(names, carried):
        for t, n in enumerate(names):
            sums[n] = _add_sibling(place, carried[t], carried[len(names) + t], name="chip_sum_" + n)

    def carried_exchange(names):
        return _chips_carry([sums[n][0] for n in names])

    def keep(names, carried):
        landed.update(zip(names, carried[len(names):]))

    layer1 = ["mlp_w1_1", "mlp_w2_1", "att_w_in", "att_w_out"]
    (dh2, d_mix_g1), carried = _mm_nt(dqkv, w["att_w_in"], "col", tm=tb, name="d_att_in", epi="normbwd",
                                      h=h2, g=mix_g[1], dres=dh3, carry=swap_halves(layer1))
    add_halves(layer1, carried)
    gw["mlp_w2_0"] = _mm_tn(a0, dh2, "row", tt=tf, tk=1024, tn=D, name="d_mlp0_w2", square=True)
    dz0, carried = _mm_nt(dh2, w["mlp_w2_0"], "row", tm=tb, name="d_mlp0_act", epi="relu2bwd", a2=a0,
                          carry=carried_exchange(["mlp_w1_1", "mlp_w2_1"]))
    keep(["mlp_w1_1", "mlp_w2_1"], carried)
    gw["mlp_w1_0"] = _mm_tn(hm0, dz0, "col", tt=tt, tk=D, tn=256, name="d_mlp0_w1")
    (dh1, d_mlp_g0), carried = _mm_nt(dz0, w["mlp_w1_0"], "col", tm=tb, name="d_mlp0_in", epi="normbwd",
                                      h=h1, g=mlp_g[0], dres=dh2,
                                      carry=_merge(carried_exchange(["att_w_in", "att_w_out"]), swap_halves(["mlp_w1_0", "mlp_w2_0"])))
    keep(["att_w_in", "att_w_out"], carried[:4])
    add_halves(["mlp_w1_0", "mlp_w2_0"], carried[4:])
    gw["ret_w_out"] = _mm_tn(y_ret, dh1, "row", tt=tf, tk=512, tn=D, name="d_ret_wout")
    dy_ret, carried = _mm_nt(dh1, w["ret_w_out"], "row", tm=tb, name="d_ret_y", epi="bf16", carry=swap_halves(["ret_w_out"]))
    add_halves(["ret_w_out"], carried)
    dproj, d_gn = _ret_bwd(proj, dy_ret, states, gn_g, consts, B=B, S=S)
    gw["ret_w_in"], carried = _mm_tn(hn0, dproj, "col", tt=tt, tk=D, tn=256, name="d_ret_win",
                                     carry=carried_exchange(["mlp_w1_0", "mlp_w2_0", "ret_w_out"]))
    keep(["mlp_w1_0", "mlp_w2_0", "ret_w_out"], carried)
    add_halves(["ret_w_in"], [gw["ret_w_in"]] + list(_exchange_with_sibling([gw["ret_w_in"]], "ret_in")))
    done = [n for n in BIG if n != "ret_w_in"]
    shards = {n: _add_chips(place, sums[n][1], landed[n], name="total_" + n) for n in done}
    (dx, d_mix_g0), carried = _mm_nt(dproj, w["ret_w_in"], "col", tm=tb, name="d_ret_in", epi="normbwd", h=h0, g=mix_g[0], dres=dh1,
                                     carry=_merge(carried_exchange(["ret_w_in"]), _share_carry([shards[n] for n in done])))
    keep(["ret_w_in"], carried[:2])
    shards.update(zip(done, carried[2:]))
    (shards["ret_w_in"],) = _share_with_sibling([_add_chips(place, sums["ret_w_in"][1], landed["ret_w_in"], name="total_ret_w_in")])
    small = [d_mix_g0, d_mix_g1, d_mlp_g0, d_mlp_g1, d_fin_g, d_gn.reshape(2, D), d_rel.reshape(5, D), loss]
    return dx.reshape(B, S, D), shards, small


def _row_tile(r, want=256):
    t = min(r, want)
    assert r % t == 0
    return t


def _into_slab(place, a, layer, dtype, name):
    _, r, c = a.shape
    tr = _row_tile(r)

    def body(place_ref, a_ref, o_ref):
        o_ref[...] = a_ref[...].astype(dtype)

    grid_spec = pltpu.PrefetchScalarGridSpec(
        num_scalar_prefetch=1, grid=(r // tr,), in_specs=[pl.BlockSpec((None, tr, c), lambda i, pr: (layer, i, 0))],
        out_specs=pl.BlockSpec((None, tr, c), lambda i, pr: (pr[0], i, 0)),
    )
    return pl.pallas_call(
        body, name=name, grid_spec=grid_spec, out_shape=jax.ShapeDtypeStruct((N_CHIPS, r, c), dtype), compiler_params=_params(1),
    )(place, a)


def _add_sibling(place, g, recv, name):
    _, r, c = g.shape
    hr = r // 2
    tr = _row_tile(hr)
    nrt = hr // tr

    def body(place_ref, g_ref, r_ref, sb_ref, own_ref):
        v = g_ref[...] + r_ref[...]
        sb_ref[...] = v.astype(BF16)

        @pl.when(pl.program_id(1) == place_ref[0])
        def _():
            own_ref[...] = v

    grid_spec = pltpu.PrefetchScalarGridSpec(
        num_scalar_prefetch=1, grid=(nrt, N_CHIPS),
        in_specs=[pl.BlockSpec((None, tr, c), lambda i, s, pr: (s, pr[1] * nrt + i, 0)),
                  pl.BlockSpec((None, tr, c), lambda i, s, pr: (s, i, 0))],
        out_specs=[pl.BlockSpec((None, tr, c), lambda i, s, pr: (s, i, 0)), pl.BlockSpec((tr, c), lambda i, s, pr: (i, 0))],
    )
    return pl.pallas_call(
        body, name=name, grid_spec=grid_spec,
        out_shape=[jax.ShapeDtypeStruct((N_CHIPS, hr, c), BF16), jax.ShapeDtypeStruct((hr, c), F32)],
        compiler_params=_params(2),
    )(place, g, recv)


def _add_chips(place, own, recv, name):
    hr, c = own.shape
    tr = _row_tile(hr)
    nrt = hr // tr

    def body(place_ref, o_ref, r_ref, t_ref):
        t_ref[...] = ((o_ref[...] + r_ref[0].astype(F32)) + r_ref[1].astype(F32)) + r_ref[2].astype(F32)

    grid_spec = pltpu.PrefetchScalarGridSpec(
        num_scalar_prefetch=1, grid=(nrt,),
        in_specs=[pl.BlockSpec((tr, c), lambda i, pr: (i, 0)), pl.BlockSpec((3, tr, c), lambda i, pr: (0, i, 0))],
        out_specs=pl.BlockSpec((tr, c), lambda i, pr: (pr[1] * nrt + i, 0)),
    )
    return pl.pallas_call(
        body, name=name, grid_spec=grid_spec, out_shape=jax.ShapeDtypeStruct((2 * hr, c), F32), compiler_params=_params(1),
    )(place, own, recv)


def _adamw(w, gs, m, v, name):
    L, r, c = w.shape
    tr = _row_tile(r)
    assert len(gs) == L

    def body(*refs):
        w_ref, m_ref, v_ref = refs[:3]
        g_refs = refs[3:3 + L]
        go_ref, d_ref, nm_ref, nv_ref = refs[3 + L:]
        gg = g_refs[0][...]
        for k in range(1, L):
            gg = jnp.where(pl.program_id(0) == k, g_refs[k][...], gg)
        go_ref[...] = gg
        nm = ADAM_B1 * m_ref[...] + (1.0 - ADAM_B1) * gg
        nv = ADAM_B2 * v_ref[...] + (1.0 - ADAM_B2) * (gg * gg)
        m_hat = nm / (1.0 - ADAM_B1 ** ADAM_STEP)
        v_hat = nv / (1.0 - ADAM_B2 ** ADAM_STEP)
        d_ref[...] = -ADAM_LR * (m_hat / (jnp.sqrt(v_hat) + ADAM_EPS) + ADAM_WD * w_ref[...])
        nm_ref[...] = nm
        nv_ref[...] = nv

    spec = pl.BlockSpec((None, tr, c), lambda l, i: (l, i, 0))
    return pl.pallas_call(
        body, name=name, grid=(L, r // tr), in_specs=[spec] * 3 + [pl.BlockSpec((tr, c), lambda l, i: (i, 0))] * L,
        out_specs=[spec] * 4, out_shape=[jax.ShapeDtypeStruct((L, r, c), F32)] * 4, compiler_params=_params(2),
    )(w, m, v, *gs)


def _place():
    return lax.axis_index("x"), lax.axis_index("y"), lax.axis_index("c")


def _other_chips(x, y):
    return [(1 - x, y), (x, 1 - y), (1 - x, 1 - y)]


def _remote(src, dst, ssem, rsem, dev):
    return pltpu.make_async_remote_copy(src_ref=src, dst_ref=dst, send_sem=ssem, recv_sem=rsem, device_id=dev,
                                        device_id_type=MESH)


def _gather_phases(n):
    def geometry(refs, t):
        x, y, c = _place()
        hr = refs[t].shape[1] // 2
        chips = _other_chips(x, y)
        return x, y, c, 2 * x + y, chips, [2 * qx + qy for qx, qy in chips], pl.ds(c * hr, hr), pl.ds((1 - c) * hr, hr)

    def send(refs, sems):
        s1, r1, _, _ = sems
        for t in range(n):
            x, y, c, p, chips, cidx, mine, theirs = geometry(refs, t)
            for j, (qx, qy) in enumerate(chips):
                _remote(refs[t].at[p, mine], refs[t].at[p, mine], s1.at[t, j], r1.at[t, j], (qx, qy, c)).start()

    def pass_on(refs, sems):
        s1, r1, s2, r2 = sems
        for t in range(n):
            x, y, c, p, chips, cidx, mine, theirs = geometry(refs, t)
            for j, (qx, qy) in enumerate(chips):
                got = refs[t].at[cidx[j], mine]
                _remote(got, got, s1.at[t, j], r1.at[t, j], (qx, qy, c)).wait_recv()
                _remote(got, got, s2.at[t, j], r2.at[t, j], (x, y, 1 - c)).start()

    def finish(refs, sems):
        s1, r1, s2, r2 = sems
        for t in range(n):
            x, y, c, p, chips, cidx, mine, theirs = geometry(refs, t)
            for j, (qx, qy) in enumerate(chips):
                got = refs[t].at[cidx[j], theirs]
                _remote(got, got, s2.at[t, j], r2.at[t, j], (x, y, 1 - c)).wait_recv()
        for t in range(n):
            x, y, c, p, chips, cidx, mine, theirs = geometry(refs, t)
            for j, (qx, qy) in enumerate(chips):
                _remote(refs[t].at[p, mine], refs[t].at[p, mine], s1.at[t, j], r1.at[t, j], (qx, qy, c)).wait_send()
                sent = refs[t].at[cidx[j], mine]
                _remote(sent, sent, s2.at[t, j], r2.at[t, j], (x, y, 1 - c)).wait_send()

    sem = pltpu.SemaphoreType.DMA
    return send, pass_on, finish, [sem((n, 3)), sem((n, 3)), sem((n, 3)), sem((n, 3))]


def _gather_carry(slabs):
    send, pass_on, finish, sems = _gather_phases(len(slabs))
    return _Carry(slabs, sems, [(0, send), (-2, pass_on), (-1, finish)])


def _all_gather_slabs(slabs):
    n = len(slabs)
    send, pass_on, finish, sems = _gather_phases(n)

    def body(*refs):
        outs, scratch = refs[n:2 * n], refs[2 * n:]
        send(outs, scratch)
        pass_on(outs, scratch)
        finish(outs, scratch)

    return pl.pallas_call(
        body, name="gather_weights", in_specs=[HBM_SPEC] * n, out_specs=[HBM_SPEC] * n,
        out_shape=[jax.ShapeDtypeStruct(s.shape, s.dtype) for s in slabs], input_output_aliases={t: t for t in range(n)},
        scratch_shapes=sems, compiler_params=pltpu.CompilerParams(has_side_effects=True),
    )(*slabs)


def _sibling_phases(n):
    def copies(refs, sems):
        ssem, rsem = sems
        x, y, c = _place()
        out = []
        for t in range(n):
            hr = refs[t].shape[1] // 2
            out.append(_remote(refs[t].at[:, pl.ds((1 - c) * hr, hr), :], refs[n + t], ssem.at[t], rsem.at[t], (x, y, 1 - c)))
        return out

    def send(refs, sems):
        for cp in copies(refs, sems):
            cp.start()

    def finish(refs, sems):
        for cp in copies(refs, sems):
            cp.wait()

    sem = pltpu.SemaphoreType.DMA
    return send, finish, [sem((n,)), sem((n,))]


def _halves_landing(grads):
    return [lax.empty((N_CHIPS, g.shape[1] // 2, g.shape[2]), g.dtype) for g in grads]


def _sibling_carry(grads):
    send, finish, sems = _sibling_phases(len(grads))
    return _Carry(list(grads) + _halves_landing(grads), sems, [(0, send), (-1, finish)])


def _exchange_with_sibling(grads, tag):
    n = len(grads)
    send, finish, sems = _sibling_phases(n)

    def body(*refs):
        both, scratch = refs[:2 * n], refs[2 * n:]
        send(both, scratch)
        finish(both, scratch)

    return pl.pallas_call(
        body, name="grads_to_sibling_" + tag, in_specs=[HBM_SPEC] * n, out_specs=[HBM_SPEC] * n,
        out_shape=[jax.ShapeDtypeStruct((N_CHIPS, g.shape[1] // 2, g.shape[2]), g.dtype) for g in grads],
        scratch_shapes=sems, compiler_params=pltpu.CompilerParams(has_side_effects=True),
    )(*grads)


def _chips_phases(n):
    def copies(refs, sems):
        ssem, rsem = sems
        x, y, c = _place()
        return [_remote(refs[t].at[2 * qx + qy], refs[n + t].at[j], ssem.at[t, j], rsem.at[t, j], (qx, qy, c))
                for t in range(n) for j, (qx, qy) in enumerate(_other_chips(x, y))]

    def send(refs, sems):
        for cp in copies(refs, sems):
            cp.start()

    def finish(refs, sems):
        for cp in copies(refs, sems):
            cp.wait()

    sem = pltpu.SemaphoreType.DMA
    return send, finish, [sem((n, 3)), sem((n, 3))]


def _landing(sums):
    return [lax.empty((3,) + s.shape[1:], s.dtype) for s in sums]


def _chips_carry(sums):
    send, finish, sems = _chips_phases(len(sums))
    return _Carry(list(sums) + _landing(sums), sems, [(0, send), (-1, finish)])


def _share_phases(n):
    def halves(refs, t):
        x, y, c = _place()
        hr = refs[t].shape[0] // 2
        return refs[t].at[pl.ds(c * hr, hr)], refs[t].at[pl.ds((1 - c) * hr, hr)], (x, y, 1 - c)

    def send(refs, sems):
        ssem, rsem = sems
        for t in range(n):
            mine, _, sibling = halves(refs, t)
            _remote(mine, mine, ssem.at[t], rsem.at[t], sibling).start()

    def finish(refs, sems):
        ssem, rsem = sems
        for t in range(n):
            mine, theirs, sibling = halves(refs, t)
            _remote(theirs, theirs, ssem.at[t], rsem.at[t], sibling).wait_recv()
            _remote(mine, mine, ssem.at[t], rsem.at[t], sibling).wait_send()

    sem = pltpu.SemaphoreType.DMA
    return send, finish, [sem((n,)), sem((n,))]


def _share_carry(shards):
    send, finish, sems = _share_phases(len(shards))
    return _Carry(shards, sems, [(0, send), (-1, finish)])


def _share_with_sibling(shards):
    n = len(shards)
    send, finish, sems = _share_phases(n)

    def body(*refs):
        outs, scratch = refs[n:2 * n], refs[2 * n:]
        send(outs, scratch)
        finish(outs, scratch)

    return pl.pallas_call(
        body, name="grads_share", in_specs=[HBM_SPEC] * n, out_specs=[HBM_SPEC] * n,
        out_shape=[jax.ShapeDtypeStruct(s.shape, s.dtype) for s in shards], input_output_aliases={t: t for t in range(n)},
        scratch_shapes=sems, compiler_params=pltpu.CompilerParams(has_side_effects=True),
    )(*shards)


def _all_reduce_small(buf):
    R, C = buf.shape

    def body(in_ref, out_ref, gather, ssem, rsem):
        x, y, c = _place()
        me = 4 * x + 2 * y + c
        gather[me] = in_ref[...]
        flips = [(fx, fy, fc) for fx in (0, 1) for fy in (0, 1) for fc in (0, 1) if fx or fy or fc]
        peers = [(x + fx - 2 * x * fx, y + fy - 2 * y * fy, c + fc - 2 * c * fc) for fx, fy, fc in flips]
        copies = [_remote(in_ref, gather.at[me], ssem.at[k], rsem.at[k], peer) for k, peer in enumerate(peers)]
        for cp in copies:
            cp.start()
        for k, (px, py, pc) in enumerate(peers):
            _remote(in_ref, gather.at[4 * px + 2 * py + pc], ssem.at[k], rsem.at[k], (px, py, pc)).wait_recv()
        for cp in copies:
            cp.wait_send()
        acc = gather[0]
        for d in range(1, 8):
            acc = acc + gather[d]
        out_ref[...] = acc

    sem = pltpu.SemaphoreType.DMA
    vmem = pl.BlockSpec(memory_space=pltpu.VMEM)
    return pl.pallas_call(
        body, name="small_grads_sum", in_specs=[vmem], out_specs=vmem, out_shape=jax.ShapeDtypeStruct((R, C), F32),
        scratch_shapes=[pltpu.VMEM((8, R, C), F32), sem((7,)), sem((7,))],
        compiler_params=pltpu.CompilerParams(has_side_effects=True),
    )(buf)


BIG = ["ret_w_in", "ret_w_out", "att_w_in", "att_w_out", "mlp_w1_0", "mlp_w1_1", "mlp_w2_0", "mlp_w2_1"]
LAYER_OF = {"ret_w_in": ("ret_w_in", 0), "ret_w_out": ("ret_w_out", 0), "att_w_in": ("att_w_in", 0), "att_w_out": ("att_w_out", 0),
            "mlp_w1_0": ("mlp_w1", 0), "mlp_w1_1": ("mlp_w1", 1), "mlp_w2_0": ("mlp_w2", 0), "mlp_w2_1": ("mlp_w2", 1)}
ORDER = ["mix_norm_g", "ret_w_in", "ret_gn_g", "ret_w_out", "att_w_in", "att_rel_bias", "att_w_out", "mlp_norm_g", "mlp_w1", "mlp_w2",
         "final_norm_g"]


def kernel(x, mix_norm_g, ret_w_in, ret_gn_g, ret_w_out, att_w_in, att_rel_bias, att_w_out, mlp_norm_g, mlp_w1, mlp_w2, final_norm_g, loss_target, m_mix_norm_g, m_ret_w_in, m_ret_gn_g, m_ret_w_out, m_att_w_in, m_att_rel_bias, m_att_w_out, m_mlp_norm_g, m_mlp_w1, m_mlp_w2, m_final_norm_g, v_mix_norm_g, v_ret_w_in, v_ret_gn_g, v_ret_w_out, v_att_w_in, v_att_rel_bias, v_att_w_out, v_mlp_norm_g, v_mlp_w1, v_mlp_w2, v_final_norm_g):
    xi, yi, ci = _place()
    chip = 2 * xi + yi
    weights = dict(zip(ORDER, (mix_norm_g, ret_w_in, ret_gn_g, ret_w_out, att_w_in, att_rel_bias, att_w_out, mlp_norm_g, mlp_w1,
                               mlp_w2, final_norm_g)))
    first = dict(zip(ORDER, (m_mix_norm_g, m_ret_w_in, m_ret_gn_g, m_ret_w_out, m_att_w_in, m_att_rel_bias, m_att_w_out,
                             m_mlp_norm_g, m_mlp_w1, m_mlp_w2, m_final_norm_g)))
    second = dict(zip(ORDER, (v_mix_norm_g, v_ret_w_in, v_ret_gn_g, v_ret_w_out, v_att_w_in, v_att_rel_bias, v_att_w_out,
                              v_mlp_norm_g, v_mlp_w1, v_mlp_w2, v_final_norm_g)))

    place = jnp.stack([chip, ci]).astype(jnp.int32)
    slab = {n: _into_slab(place, weights[LAYER_OF[n][0]], LAYER_OF[n][1], BF16, name="cast_" + n) for n in BIG}
    slab_rel = _into_slab(place, att_rel_bias, 0, F32, name="slab_rel_bias")

    grad_x, g_big, small = _step(x, loss_target, slab, slab_rel, place, mix_norm_g, ret_gn_g[0], mlp_norm_g, final_norm_g)

    rows, at = jnp.zeros((16, D_MODEL), F32), 0
    for part in small:
        rows = rows + jnp.pad(part, ((at, 16 - at - part.shape[0]), (0, D_MODEL - part.shape[1])))
        at += part.shape[0]
    rows = _all_reduce_small(rows)
    loss = rows[12, 0]
    grads = {"mix_norm_g": [rows[0:2]], "mlp_norm_g": [rows[2:4]], "final_norm_g": [rows[4:5]], "ret_gn_g": [rows[5:7].reshape(1, 2048)],
             "att_rel_bias": [lax.dynamic_slice_in_dim(rows[7:12].reshape(ATT_HEADS, REL_TABLE), chip * (REL_TABLE // N_CHIPS),
                                                       REL_TABLE // N_CHIPS, axis=1)]}
    for n in BIG:
        grads.setdefault(LAYER_OF[n][0], []).append(g_big[n])

    def as3(a):
        return a.reshape((1,) * (3 - a.ndim) + a.shape)

    results = {}
    for n in ORDER:
        outs = _adamw(as3(weights[n]), grads[n], as3(first[n]), as3(second[n]), name="adamw_" + n)
        results[n] = [o.reshape(weights[n].shape) for o in outs]
    return (loss, grad_x) + tuple(results[n][k] for k in range(4) for n in ORDER)
```

```python
import math

import jax
import jax.numpy as jnp
from jax import lax
from jax.experimental import pallas as pl
from jax.experimental.pallas import tpu as pltpu

F32 = jnp.float32
BF16 = jnp.bfloat16
MESH = pl.DeviceIdType.MESH

D_MODEL = 1024
CHUNK = 64
RET_HEADS = 4
RET_DK = 256
RET_DV = 512
ROPE_BASE = 10000.0
ATT_HEADS = 16
ATT_DH = 64
PAST = 512
MAX_REL = 256
REL_TABLE = MAX_REL + CHUNK
EPS = 1e-6
NEG = -1e30
N_CHIPS = 4

ADAM_LR = 0.001
ADAM_B1 = 0.9
ADAM_B2 = 0.999
ADAM_EPS = 1e-08
ADAM_WD = 0.01
ADAM_STEP = 10

RET_BLOCK = 256
ATT_BLOCK = 256
VMEM_LIMIT = 56 * 1024 * 1024


def _params(n_axes, **kw):
    return pltpu.CompilerParams(dimension_semantics=("arbitrary",) * n_axes, vmem_limit_bytes=VMEM_LIMIT, **kw)


def _dot(a, b):
    return jnp.dot(a, b, preferred_element_type=F32)


def _dot_nt(a, b):
    return lax.dot_general(a, b, (((1,), (1,)), ((), ())), preferred_element_type=F32)


def _dot_tn(a, b):
    return lax.dot_general(a, b, (((0,), (0,)), ((), ())), preferred_element_type=F32)


def _sigmoid(x):
    return 0.5 * jnp.tanh(0.5 * x) + 0.5


HBM_SPEC = pl.BlockSpec(memory_space=pltpu.HBM)


class _Carry:
    def __init__(self, arrays, sems, stages):
        self.arrays, self.sems, self.stages = list(arrays), list(sems), list(stages)


def _merge(a, b):
    na, sa = len(a.arrays), len(a.sems)

    def of_a(fn):
        return lambda refs, sems: fn(refs[:na], sems[:sa])

    def of_b(fn):
        return lambda refs, sems: fn(refs[na:], sems[sa:])

    return _Carry(a.arrays + b.arrays, a.sems + b.sems,
                  [(at, of_a(fn)) for at, fn in a.stages] + [(at, of_b(fn)) for at, fn in b.stages])


def _carry_call(body, carry, *, name, grid, in_specs, out_specs, out_shape, args):
    if carry is None:
        outs = pl.pallas_call(body, name=name, grid=grid, in_specs=in_specs, out_specs=out_specs, out_shape=out_shape,
                              compiler_params=_params(len(grid)))(*args)
        return list(outs), []
    n_in, n_out, n_c = len(in_specs), len(out_specs), len(carry.arrays)
    steps = 1
    for g in grid:
        steps *= g
    assert all(-steps <= at < steps for at, _ in carry.stages)

    def carrying(*refs):
        ins, outs = refs[:n_in], refs[n_in + n_c:n_in + n_c + n_out]
        carried = refs[n_in + n_c + n_out:n_in + 2 * n_c + n_out]
        sems = refs[n_in + 2 * n_c + n_out:]
        step = pl.program_id(0)
        for axis in range(1, len(grid)):
            step = step * grid[axis] + pl.program_id(axis)
        for at, fn in carry.stages:
            if at == 0:
                pl.when(step == 0)(lambda fn=fn: fn(carried, sems))
        body(*ins, *outs)
        for at, fn in carry.stages:
            if at != 0:
                pl.when(step == at % steps)(lambda fn=fn: fn(carried, sems))

    outs = pl.pallas_call(
        carrying, name=name, grid=grid, in_specs=list(in_specs) + [HBM_SPEC] * n_c, out_specs=list(out_specs) + [HBM_SPEC] * n_c,
        out_shape=list(out_shape) + [jax.ShapeDtypeStruct(a.shape, a.dtype) for a in carry.arrays],
        input_output_aliases={n_in + t: n_out + t for t in range(n_c)}, scratch_shapes=carry.sems,
        compiler_params=_params(len(grid), has_side_effects=True),
    )(*args, *carry.arrays)
    return list(outs[:n_out]), list(outs[n_out:])


def _mm_nn(a, w, wkind, *, tm, out_dtype, name, norm_g=None, act=None, square=False, res=None, loss=None, carry=None):
    M, K = a.shape
    cols = w.shape[2]
    N = N_CHIPS * cols if wkind == "col" else cols
    has_norm = norm_g is not None
    steps = M // tm
    assert M % tm == 0 and K == (w.shape[1] if wkind == "col" else N_CHIPS * w.shape[1])
    assert loss is None or (wkind == "row" and act is None)

    def body(*refs):
        it = iter(refs)
        a_ref, w_ref = next(it), next(it)
        g_ref = next(it) if has_norm else None
        r_ref = next(it) if res is not None else None
        if loss is not None:
            t_ref, fg_ref = next(it), next(it)
        o_ref = next(it)
        hn_ref = next(it) if has_norm else None
        if loss is not None:
            dg_ref, sq_ref, loss_ref = next(it), next(it), next(it)
        if has_norm:
            x = a_ref[...].astype(F32)
            r = lax.rsqrt(jnp.mean(x * x, axis=-1, keepdims=True) + EPS)
            lhs = (x * r * g_ref[...]).astype(BF16)
            hn_ref[...] = lhs
        elif square:
            lhs = a_ref[...].astype(BF16)
            lhs = lhs * lhs
        else:
            lhs = a_ref[...].astype(BF16)

        def loss_head(x):
            i = pl.program_id(0)
            r = lax.rsqrt(jnp.mean(x * x, axis=-1, keepdims=True) + EPS)
            xh = x * r
            gg = fg_ref[...]
            diff = xh * gg - t_ref[...]
            sq = jnp.sum(diff * diff, axis=0, keepdims=True)
            dy = diff * (1.0 / N)
            dg_part = jnp.sum(dy * xh, axis=0, keepdims=True)

            @pl.when(i == 0)
            def _():
                sq_ref[...] = sq
                dg_ref[...] = dg_part

            @pl.when(i > 0)
            def _():
                sq_ref[...] += sq
                dg_ref[...] += dg_part
            t = dy * gg
            o_ref[...] = r * (t - xh * jnp.mean(t * xh, axis=-1, keepdims=True))

            @pl.when(i == steps - 1)
            def _():
                loss_ref[...] = (0.5 / N) * jnp.sum(sq_ref[...], axis=1, keepdims=True)

        def finish(acc, sl):
            if act == "relu":
                acc = jnp.maximum(acc, 0.0)
            if r_ref is not None:
                acc = acc + r_ref[:, sl]
            if loss is not None:
                loss_head(acc)
            else:
                o_ref[:, sl] = acc.astype(out_dtype)

        if wkind == "col":
            for s in range(N_CHIPS):
                finish(_dot(lhs, w_ref[s]), slice(s * cols, (s + 1) * cols))
        else:
            finish(_dot(lhs, w_ref[...].reshape(K, N)), slice(None))

    in_specs = [pl.BlockSpec((tm, K), lambda i: (i, 0)), pl.BlockSpec(w.shape, lambda i: (0, 0, 0))]
    args = [a, w]
    if has_norm:
        in_specs.append(pl.BlockSpec((1, K), lambda i: (0, 0)))
        args.append(norm_g.reshape(1, K))
    if res is not None:
        in_specs.append(pl.BlockSpec((tm, N), lambda i: (i, 0)))
        args.append(res)
    if loss is not None:
        in_specs += [pl.BlockSpec((tm, N), lambda i: (i, 0)), pl.BlockSpec((1, N), lambda i: (0, 0))]
        args += [loss[0], loss[1].reshape(1, N)]
    out_shape = [jax.ShapeDtypeStruct((M, N), out_dtype)]
    out_specs = [pl.BlockSpec((tm, N), lambda i: (i, 0))]
    if has_norm:
        out_shape.append(jax.ShapeDtypeStruct((M, K), BF16))
        out_specs.append(pl.BlockSpec((tm, K), lambda i: (i, 0)))
    if loss is not None:
        out_shape += [jax.ShapeDtypeStruct((1, N), F32), jax.ShapeDtypeStruct((1, N), F32), jax.ShapeDtypeStruct((1, 1), F32)]
        out_specs += [pl.BlockSpec((1, N), lambda i: (0, 0)), pl.BlockSpec((1, N), lambda i: (0, 0)),
                      pl.BlockSpec((1, 1), lambda i: (0, 0))]
    outs, carried = _carry_call(body, carry, name=name, grid=(steps,), in_specs=in_specs, out_specs=out_specs,
                                out_shape=out_shape, args=args)
    result = outs if has_norm or loss is not None else outs[0]
    return result if carry is None else (result, carried)


def _mm_nt(a, w, wkind, *, tm, name, epi, a2=None, h=None, g=None, dres=None, carry=None):
    parts = tuple(a) if isinstance(a, (tuple, list)) else (a,)
    M, part_w = parts[0].shape
    Nw = part_w * len(parts)
    rows, cols = w.shape[1], w.shape[2]
    Kw = rows if wkind == "col" else N_CHIPS * rows
    assert M % tm == 0 and Nw == (N_CHIPS * cols if wkind == "col" else cols)
    assert epi != "normbwd" or wkind == "col"
    assert len(parts) == 1 or wkind == "col"
    chunk = math.gcd(part_w, cols)

    def body(*refs):
        it = iter(refs)
        a_refs = [next(it) for _ in parts]
        a_ref, w_ref = a_refs[0], next(it)
        a2_ref = next(it) if epi == "relu2bwd" else None
        if epi == "normbwd":
            h_ref, g_ref, dres_ref = next(it), next(it), next(it)
        o_ref = next(it)
        dg_ref = next(it) if epi == "normbwd" else None
        i = pl.program_id(0)

        def finish(acc, sl):
            if epi == "bf16":
                o_ref[:, sl] = acc.astype(BF16)
            elif epi == "relu2bwd":
                o_ref[:, sl] = (acc * (2.0 * a2_ref[:, sl].astype(F32))).astype(BF16)
            else:
                x = h_ref[...]
                r = lax.rsqrt(jnp.mean(x * x, axis=-1, keepdims=True) + EPS)
                xh = x * r
                dg_part = jnp.sum(acc * xh, axis=0, keepdims=True)

                @pl.when(i == 0)
                def _():
                    dg_ref[...] = dg_part

                @pl.when(i > 0)
                def _():
                    dg_ref[...] += dg_part
                t = acc * g_ref[...]
                dx = r * (t - xh * jnp.mean(t * xh, axis=-1, keepdims=True))
                o_ref[...] = dres_ref[...] + dx

        if wkind == "col":
            acc = None
            for lo in range(0, Nw, chunk):
                (src, a_lo), (s, w_lo) = divmod(lo, part_w), divmod(lo, cols)
                part = _dot_nt(a_refs[src][:, a_lo:a_lo + chunk].astype(BF16), w_ref[s, :, w_lo:w_lo + chunk])
                acc = part if acc is None else acc + part
            finish(acc, slice(None))
        else:
            lhs = a_ref[...].astype(BF16)
            for s in range(N_CHIPS):
                finish(_dot_nt(lhs, w_ref[s]), slice(s * rows, (s + 1) * rows))

    in_specs = [pl.BlockSpec((tm, part_w), lambda i: (i, 0)) for _ in parts] + [pl.BlockSpec(w.shape, lambda i: (0, 0, 0))]
    args = [*parts, w]
    out_dtype = BF16
    if epi == "relu2bwd":
        in_specs.append(pl.BlockSpec((tm, Kw), lambda i: (i, 0)))
        args.append(a2)
    if epi == "normbwd":
        in_specs += [pl.BlockSpec((tm, Kw), lambda i: (i, 0)), pl.BlockSpec((1, Kw), lambda i: (0, 0)),
                     pl.BlockSpec((tm, Kw), lambda i: (i, 0))]
        args += [h, g.reshape(1, Kw), dres]
        out_dtype = F32
    out_shape = [jax.ShapeDtypeStruct((M, Kw), out_dtype)]
    out_specs = [pl.BlockSpec((tm, Kw), lambda i: (i, 0))]
    if epi == "normbwd":
        out_shape.append(jax.ShapeDtypeStruct((1, Kw), F32))
        out_specs.append(pl.BlockSpec((1, Kw), lambda i: (0, 0)))
    outs, carried = _carry_call(body, carry, name=name, grid=(M // tm,), in_specs=in_specs, out_specs=out_specs,
                                out_shape=out_shape, args=args)
    result = outs if epi == "normbwd" else outs[0]
    return result if carry is None else (result, carried)


def _mm_tn(a, b, okind, *, tt, tk, tn, name, square=False, carry=None):
    parts = tuple(b) if isinstance(b, (tuple, list)) else (b,)
    T, K = a.shape
    part_w = parts[0].shape[1]
    N = part_w * len(parts)
    assert T % tt == 0 and K % tk == 0 and part_w % tn == 0
    nt = T // tt
    per_part = part_w // tn
    if okind == "col":
        per = (N // N_CHIPS) // tn
        assert (N // N_CHIPS) % tn == 0
        out_shape = jax.ShapeDtypeStruct((N_CHIPS, K, N // N_CHIPS), F32)
        out_spec = pl.BlockSpec((None, tk, tn), lambda ki, nj, t: (nj // per, ki, nj % per))
    else:
        per = (K // N_CHIPS) // tk
        assert (K // N_CHIPS) % tk == 0
        out_shape = jax.ShapeDtypeStruct((N_CHIPS, K // N_CHIPS, N), F32)
        out_spec = pl.BlockSpec((None, tk, tn), lambda ki, nj, t: (ki // per, ki % per, nj))

    def body(a_ref, *refs):
        b_refs, o_ref = refs[:-1], refs[-1]
        nj, t = pl.program_id(1), pl.program_id(2)

        def accumulate(b_ref):
            lhs = a_ref[...].astype(BF16)
            part = _dot_tn(lhs * lhs if square else lhs, b_ref[...].astype(BF16))
            if nt == 1:
                o_ref[...] = part
            else:
                @pl.when(t == 0)
                def _():
                    o_ref[...] = part

                @pl.when(t > 0)
                def _():
                    o_ref[...] += part

        if len(parts) == 1:
            accumulate(b_refs[0])
        else:
            for s, b_ref in enumerate(b_refs):
                pl.when(nj // per_part == s)(lambda b_ref=b_ref: accumulate(b_ref))

    def b_spec(s):
        return pl.BlockSpec((tt, tn), lambda ki, nj, t: (t, jnp.clip(nj - s * per_part, 0, per_part - 1)))

    outs, carried = _carry_call(
        body, carry, name=name, grid=(K // tk, N // tn, nt),
        in_specs=[pl.BlockSpec((tt, tk), lambda ki, nj, t: (t, ki))] + [b_spec(s) for s in range(len(parts))],
        out_specs=[out_spec], out_shape=[out_shape], args=[a, *parts])
    return outs[0] if carry is None else (outs[0], carried)


def _ret_consts(S, LB):
    log_gamma = jnp.log1p(-jnp.exp2(-5.0 - jnp.arange(RET_HEADS, dtype=F32)))
    idx = jnp.arange(LB, dtype=F32)
    n, m = idx[:, None], idx[None, :]
    cn, cm = jnp.floor(n / CHUNK), jnp.floor(m / CHUNK)
    dist = jnp.where(cm == cn, jnp.abs(n - m), n - m)
    dmat = jnp.where((cm <= cn)[None], jnp.exp(log_gamma[:, None, None] * dist[None]), 0.0)
    qd = jnp.exp(log_gamma[:, None] * (idx + 1.0)[None, :])[..., None]
    kd = jnp.exp(log_gamma[:, None] * (LB - 1 - idx)[None, :])[..., None]
    bd = jnp.exp(log_gamma * LB).reshape(RET_HEADS, 1, 1) * jnp.ones((RET_HEADS, 1, 128), F32)
    half = RET_DK // 2
    inv = jnp.exp(-jnp.log(ROPE_BASE) * jnp.arange(half, dtype=F32) / half)
    ang = jnp.arange(S, dtype=F32)[:, None] * inv[None, :]
    return dmat.astype(F32), qd.astype(F32), kd.astype(F32), bd, jnp.cos(ang), jnp.sin(ang)


def _rope(t, c, s):
    t1, t2 = t[:, :128], t[:, 128:]
    return jnp.concatenate([t1 * c - t2 * s, t1 * s + t2 * c], axis=-1)


def _rope_inv(d, c, s):
    d1, d2 = d[:, :128], d[:, 128:]
    return jnp.concatenate([d1 * c + d2 * s, d2 * c - d1 * s], axis=-1)


def _ret_block_fwd(p_ref, h, c, s, d_ref, qd_ref, kd_ref, stb):
    q = _rope(p_ref[:, h * RET_DK:(h + 1) * RET_DK].astype(F32), c, s)
    k = _rope(p_ref[:, 1024 + h * RET_DK:1024 + (h + 1) * RET_DK].astype(F32), c, s) * (RET_DK ** -0.5)
    v = p_ref[:, 2048 + h * RET_DV:2048 + (h + 1) * RET_DV]
    qb, kb = q.astype(BF16), k.astype(BF16)
    scb = (_dot_nt(qb, kb) * d_ref[h]).astype(BF16)
    o = _dot(scb, v) + qd_ref[h] * _dot(qb, stb)
    return q, k, qb, kb, v, scb, o


def _ret_fwd(proj, gn_g, consts, *, B, S):
    LB = RET_BLOCK
    nb = S // LB
    T = B * S
    dmat, qd, kd, bd, cos, sin = consts

    def body(p_ref, cos_ref, sin_ref, d_ref, qd_ref, kd_ref, bd_ref, gng_ref, y_ref, st_ref, state_s):
        i = pl.program_id(1)

        @pl.when(i == 0)
        def _():
            state_s[...] = jnp.zeros_like(state_s)
        c, s = cos_ref[...], sin_ref[...]
        for h in range(RET_HEADS):
            st = state_s[h]
            stb = st.astype(BF16)
            st_ref[h] = stb
            q, k, qb, kb, v, scb, o = _ret_block_fwd(p_ref, h, c, s, d_ref, qd_ref, kd_ref, stb)
            kdk = (k * kd_ref[h]).astype(BF16)
            state_s[h] = st * bd_ref[h][:, :1] + _dot_tn(kdk, v)
            gate = p_ref[:, 4096 + h * RET_DV:4096 + (h + 1) * RET_DV].astype(F32)
            mu = jnp.mean(o, axis=-1, keepdims=True)
            oc = o - mu
            xh = oc * lax.rsqrt(jnp.mean(oc * oc, axis=-1, keepdims=True) + EPS)
            y = (gate * _sigmoid(gate)) * (xh * gng_ref[:, h * RET_DV:(h + 1) * RET_DV])
            y_ref[:, h * RET_DV:(h + 1) * RET_DV] = y.astype(BF16)

    const = lambda b, i: (0, 0, 0)
    return pl.pallas_call(
        body, name="ret_fwd", grid=(B, nb),
        in_specs=[pl.BlockSpec((LB, 6144), lambda b, i: (b * nb + i, 0)),
                  pl.BlockSpec((LB, 128), lambda b, i: (i, 0)), pl.BlockSpec((LB, 128), lambda b, i: (i, 0)),
                  pl.BlockSpec((RET_HEADS, LB, LB), const), pl.BlockSpec((RET_HEADS, LB, 1), const),
                  pl.BlockSpec((RET_HEADS, LB, 1), const), pl.BlockSpec((RET_HEADS, 1, 128), const),
                  pl.BlockSpec((1, 2048), lambda b, i: (0, 0))],
        out_specs=[pl.BlockSpec((LB, 2048), lambda b, i: (b * nb + i, 0)),
                   pl.BlockSpec((None, None, RET_HEADS, RET_DK, RET_DV), lambda b, i: (b, i, 0, 0, 0))],
        out_shape=[jax.ShapeDtypeStruct((T, 2048), BF16), jax.ShapeDtypeStruct((B, nb, RET_HEADS, RET_DK, RET_DV), BF16)],
        scratch_shapes=[pltpu.VMEM((RET_HEADS, RET_DK, RET_DV), F32)], compiler_params=_params(2),
    )(proj, cos, sin, dmat, qd, kd, bd, gn_g.reshape(1, 2048))


def _ret_bwd(proj, dy, states, gn_g, consts, *, B, S):
    LB = RET_BLOCK
    nb = S // LB
    T = B * S
    dmat, qd, kd, bd, cos, sin = consts

    def body(p_ref, dy_ref, st_ref, cos_ref, sin_ref, d_ref, qd_ref, kd_ref, bd_ref, gng_ref, dp_ref, dgn_ref, dstate_s):
        b, i = pl.program_id(0), pl.program_id(1)

        @pl.when(i == 0)
        def _():
            dstate_s[...] = jnp.zeros_like(dstate_s)

        @pl.when((b == 0) & (i == 0))
        def _():
            dgn_ref[...] = jnp.zeros_like(dgn_ref)
        c, s = cos_ref[...], sin_ref[...]
        for h in range(RET_HEADS):
            vs = slice(h * RET_DV, (h + 1) * RET_DV)
            stb = st_ref[h]
            q, k, qb, kb, v, scb, o = _ret_block_fwd(p_ref, h, c, s, d_ref, qd_ref, kd_ref, stb)
            gate = p_ref[:, 4096 + h * RET_DV:4096 + (h + 1) * RET_DV].astype(F32)
            mu = jnp.mean(o, axis=-1, keepdims=True)
            oc = o - mu
            rstd = lax.rsqrt(jnp.mean(oc * oc, axis=-1, keepdims=True) + EPS)
            xh = oc * rstd
            gng = gng_ref[:, vs]
            dyh = dy_ref[:, vs].astype(F32)
            sg = _sigmoid(gate)
            silu = gate * sg
            dgn_ref[:, vs] += jnp.sum(dyh * silu * xh, axis=0, keepdims=True)
            dxh = dyh * silu * gng
            do = rstd * (dxh - jnp.mean(dxh, axis=-1, keepdims=True) - xh * jnp.mean(dxh * xh, axis=-1, keepdims=True))
            dgate = dyh * xh * gng * (sg * (1.0 + gate * (1.0 - sg)))
            dob = do.astype(BF16)
            dsb = (_dot_nt(dob, v) * d_ref[h]).astype(BF16)
            dst = dstate_s[h]
            dstb = dst.astype(BF16)
            kdk = (k * kd_ref[h]).astype(BF16)
            dqr = _dot(dsb, kb) + qd_ref[h] * _dot_nt(dob, stb)
            dkr = _dot_tn(dsb, qb) + kd_ref[h] * _dot_nt(v, dstb)
            dv = _dot_tn(scb, dob) + _dot(kdk, dstb)
            dstate_s[h] = dst * bd_ref[h][:, :1] + _dot_tn((q * qd_ref[h]).astype(BF16), dob)
            dp_ref[:, h * RET_DK:(h + 1) * RET_DK] = _rope_inv(dqr, c, s).astype(BF16)
            dp_ref[:, 1024 + h * RET_DK:1024 + (h + 1) * RET_DK] = (_rope_inv(dkr, c, s) * (RET_DK ** -0.5)).astype(BF16)
            dp_ref[:, 2048 + h * RET_DV:2048 + (h + 1) * RET_DV] = dv.astype(BF16)
            dp_ref[:, 4096 + h * RET_DV:4096 + (h + 1) * RET_DV] = dgate.astype(BF16)

    const = lambda b, i: (0, 0, 0)
    rev = lambda b, i: (b * nb + nb - 1 - i, 0)
    return pl.pallas_call(
        body, name="ret_bwd", grid=(B, nb),
        in_specs=[pl.BlockSpec((LB, 6144), rev), pl.BlockSpec((LB, 2048), rev),
                  pl.BlockSpec((None, None, RET_HEADS, RET_DK, RET_DV), lambda b, i: (b, nb - 1 - i, 0, 0, 0)),
                  pl.BlockSpec((LB, 128), lambda b, i: (nb - 1 - i, 0)), pl.BlockSpec((LB, 128), lambda b, i: (nb - 1 - i, 0)),
                  pl.BlockSpec((RET_HEADS, LB, LB), const), pl.BlockSpec((RET_HEADS, LB, 1), const),
                  pl.BlockSpec((RET_HEADS, LB, 1), const), pl.BlockSpec((RET_HEADS, 1, 128), const),
                  pl.BlockSpec((1, 2048), lambda b, i: (0, 0))],
        out_specs=[pl.BlockSpec((LB, 6144), rev), pl.BlockSpec((1, 2048), lambda b, i: (0, 0))],
        out_shape=[jax.ShapeDtypeStruct((T, 6144), BF16), jax.ShapeDtypeStruct((1, 2048), F32)],
        scratch_shapes=[pltpu.VMEM((RET_HEADS, RET_DK, RET_DV), F32)], compiler_params=_params(2),
    )(proj, dy, states, cos, sin, dmat, qd, kd, bd, gn_g.reshape(1, 2048))


BIAS_LANES = 4 * ATT_BLOCK


def _diag_onehot():
    r = lax.broadcasted_iota(jnp.int32, (REL_TABLE, BIAS_LANES), 0)
    j = lax.broadcasted_iota(jnp.int32, (REL_TABLE, BIAS_LANES), 1)
    idx = jnp.maximum(j - ATT_BLOCK - PAST, -MAX_REL) + MAX_REL
    return jnp.where(idx == r, 1.0, 0.0).astype(F32)


def _row_is(j):
    return lax.broadcasted_iota(jnp.int32, (8, BIAS_LANES), 0) == j


def _att_bias(table):
    QB, KW = ATT_BLOCK, 3 * ATT_BLOCK

    def body(t_ref, bt_ref):
        row = jnp.broadcast_to(t_ref[...], (8, REL_TABLE))
        diag = jnp.dot(row, _diag_onehot(), preferred_element_type=F32, precision=lax.Precision.HIGHEST)
        rows = jnp.zeros((8, BIAS_LANES), F32)
        for j in range(8):
            rows = jnp.where(_row_is(j), diag if j == 0 else pltpu.roll(diag, j, axis=1), rows)
        n = 8
        while n < QB:
            rows = jnp.concatenate([rows, pltpu.roll(rows, n, axis=1)], axis=0)
            n *= 2
        bias = rows[:, QB:]
        qi = lax.broadcasted_iota(jnp.int32, (QB, KW), 0)
        kj = lax.broadcasted_iota(jnp.int32, (QB, KW), 1)
        lo = (qi // CHUNK) * CHUNK
        bt_ref[...] = jnp.where((kj >= lo) & (kj < lo + PAST + CHUNK), bias, NEG).T

    return pl.pallas_call(
        body, name="att_bias", grid=(ATT_HEADS,),
        in_specs=[pl.BlockSpec((None, 1, REL_TABLE), lambda h: (h, 0, 0))],
        out_specs=pl.BlockSpec((None, KW, QB), lambda h: (h // 2, 0, h % 2)),
        out_shape=jax.ShapeDtypeStruct((ATT_HEADS // 2, KW, 2 * QB), F32),
        compiler_params=_params(1),
    )(table.reshape(ATT_HEADS, 1, REL_TABLE))


def _att_bias_grad(dbias_t):
    QB, KW = ATT_BLOCK, 3 * ATT_BLOCK

    def body(d_ref, o_ref):
        rows = jnp.concatenate([jnp.zeros((QB, QB), F32), d_ref[...].T], axis=1)
        n = QB // 2
        while n >= 8:
            rows = rows[:n] + pltpu.roll(rows[n:], BIAS_LANES - n, axis=1)
            n //= 2
        acc = jnp.zeros((8, BIAS_LANES), F32)
        for j in range(8):
            acc = acc + jnp.where(_row_is(j), rows if j == 0 else pltpu.roll(rows, BIAS_LANES - j, axis=1), 0.0)
        diag = jnp.broadcast_to(jnp.sum(acc, axis=0, keepdims=True), (8, BIAS_LANES))
        grad = lax.dot_general(diag, _diag_onehot(), (((1,), (1,)), ((), ())), preferred_element_type=F32,
                               precision=lax.Precision.HIGHEST)
        o_ref[...] = grad[:1]

    return pl.pallas_call(
        body, name="att_bias_grad", grid=(ATT_HEADS,),
        in_specs=[pl.BlockSpec((None, KW, QB), lambda h: (h // 2, 0, h % 2))],
        out_specs=pl.BlockSpec((None, 1, REL_TABLE), lambda h: (h, 0, 0)),
        out_shape=jax.ShapeDtypeStruct((ATT_HEADS, 1, REL_TABLE), F32), compiler_params=_params(1),
    )(dbias_t).reshape(ATT_HEADS, REL_TABLE)


def _by_head(x):
    first = lax.broadcasted_iota(jnp.int32, x.shape, 1) < ATT_DH
    zero = jnp.zeros_like(x)
    return jnp.concatenate([jnp.where(first, x, zero), jnp.where(first, zero, x)], axis=0)


def _att_fwd(qkv, bias_t, *, B, S):
    QB = ATT_BLOCK
    nb = S // QB
    KW = 3 * QB
    T = B * S
    scale = ATT_DH ** -0.5

    def body(q_ref, k0, k1, k2, v0, v1, v2, b_ref, o_ref, lse_ref, s_blk):
        i = pl.program_id(2)
        q2 = _by_head((q_ref[...].astype(F32) * scale).astype(BF16))
        m = jnp.full((1, 2 * QB), NEG, F32)
        for d, k_ref in enumerate((k0, k1, k2)):
            st = _dot_nt(k_ref[...], q2) + b_ref[d * QB:(d + 1) * QB, :]
            st = jnp.where(i + d >= 2, st, NEG)
            s_blk[d] = st
            m = jnp.maximum(m, jnp.max(st, axis=0, keepdims=True))
        l = jnp.zeros((1, 2 * QB), F32)
        o_t = jnp.zeros((128, 2 * QB), F32)
        for d, v_ref in enumerate((v0, v1, v2)):
            e = jnp.exp(s_blk[d] - m)
            l = l + jnp.sum(e, axis=0, keepdims=True)
            o_t = o_t + _dot(v_ref[...].astype(F32).T.astype(BF16), e.astype(BF16))
        o_t = o_t / l
        row = lax.broadcasted_iota(jnp.int32, (128, QB), 0)
        o_ref[...] = jnp.where(row < ATT_DH, o_t[:, :QB], o_t[:, QB:]).T.astype(BF16)
        lse = m + jnp.log(l)
        row8 = lax.broadcasted_iota(jnp.int32, (8, QB), 0)
        lse_ref[...] = jnp.where(row8 == 0, lse[:, :QB], jnp.where(row8 == 1, lse[:, QB:], 0.0))

    def kv(d, col0):
        return pl.BlockSpec((QB, 128), lambda hp, b, i: (b * nb + jnp.maximum(i - d, 0), col0 + hp))

    return pl.pallas_call(
        body, name="att_fwd", grid=(8, B, nb),
        in_specs=[pl.BlockSpec((QB, 128), lambda hp, b, i: (b * nb + i, hp)),
                  kv(2, 8), kv(1, 8), kv(0, 8), kv(2, 16), kv(1, 16), kv(0, 16),
                  pl.BlockSpec((None, KW, 2 * QB), lambda hp, b, i: (hp, 0, 0))],
        out_specs=[pl.BlockSpec((QB, 128), lambda hp, b, i: (b * nb + i, hp)),
                   pl.BlockSpec((None, 8, QB), lambda hp, b, i: (hp, 0, b * nb + i))],
        out_shape=[jax.ShapeDtypeStruct((T, 1024), BF16), jax.ShapeDtypeStruct((8, 8, T), F32)],
        scratch_shapes=[pltpu.VMEM((3, QB, 2 * QB), F32)], compiler_params=_params(3),
    )(qkv, qkv, qkv, qkv, qkv, qkv, qkv, bias_t)


def _att_bwd(qkv, do, o, lse, bias_t, *, B, S):
    QB = ATT_BLOCK
    nb = S // QB
    KW = 3 * QB
    T = B * S
    scale = ATT_DH ** -0.5
    TK = 256

    def body(q_ref, k0, k1, k2, v0, v1, v2, do_ref, o_ref, lse_ref, b_ref, dq_ref, dk_ref, dv_ref, db_ref, dk_acc, dv_acc):
        b, i = pl.program_id(1), pl.program_id(2)

        @pl.when(i == 0)
        def _():
            dk_acc[...] = jnp.zeros_like(dk_acc)
            dv_acc[...] = jnp.zeros_like(dv_acc)

        @pl.when((b == 0) & (i == 0))
        def _():
            db_ref[...] = jnp.zeros_like(db_ref)

        def write_out():
            slot = (i + 1) % 3
            dk_ref[...] = dk_acc[slot].astype(BF16)
            dv_ref[...] = dv_acc[slot].astype(BF16)
            dk_acc[slot] = jnp.zeros((QB, 128), F32)
            dv_acc[slot] = jnp.zeros((QB, 128), F32)

        @pl.when(i < nb)
        def _():
            dout = do_ref[...]
            q2 = _by_head((q_ref[...].astype(F32) * scale).astype(BF16))
            do2 = _by_head(dout)
            delta_t = (o_ref[...].astype(F32) * dout.astype(F32)).T
            delta2 = jnp.concatenate([jnp.sum(delta_t[:ATT_DH], axis=0, keepdims=True),
                                      jnp.sum(delta_t[ATT_DH:], axis=0, keepdims=True)], axis=1)
            lse2 = jnp.concatenate([lse_ref[0:1, :], lse_ref[1:2, :]], axis=1)
            dq_t = jnp.zeros((128, 2 * QB), F32)
            for d, (k_ref, v_ref) in enumerate(((k0, v0), (k1, v1), (k2, v2))):
                kblk, vblk = k_ref[...], v_ref[...]
                kt = kblk.astype(F32).T.astype(BF16)
                lse_d = jnp.where(i + d >= 2, lse2, -NEG)
                slot = (i + 1 + d) % 3
                for t in range(QB // TK):
                    rows = slice(t * TK, (t + 1) * TK)
                    wrows = slice(d * QB + t * TK, d * QB + (t + 1) * TK)
                    p = jnp.exp(_dot_nt(kblk[rows], q2) + b_ref[wrows, :] - lse_d)
                    ds = p * (_dot_nt(vblk[rows], do2) - delta2)
                    db_ref[wrows, :] += ds
                    dsb = ds.astype(BF16)
                    dk_acc[slot, rows, :] += _dot(dsb, q2)
                    dv_acc[slot, rows, :] += _dot(p.astype(BF16), do2)
                    dq_t += _dot(kt[:, rows], dsb)
            row = lax.broadcasted_iota(jnp.int32, (128, QB), 0)
            dq_ref[...] = (jnp.where(row < ATT_DH, dq_t[:, :QB], dq_t[:, QB:]) * scale).T.astype(BF16)
            write_out()

        pl.when(i >= nb)(write_out)

    def qrow(b, i):
        return b * nb + jnp.minimum(i, nb - 1)

    def kv(d, col0):
        return pl.BlockSpec((QB, 128), lambda hp, b, i: (b * nb + jnp.maximum(jnp.minimum(i, nb - 1) - d, 0), col0 + hp))

    late = pl.BlockSpec((QB, 128), lambda hp, b, i: (b * nb + jnp.maximum(i - 2, 0), hp))
    return pl.pallas_call(
        body, name="att_bwd", grid=(8, B, nb + 2),
        in_specs=[pl.BlockSpec((QB, 128), lambda hp, b, i: (qrow(b, i), hp)),
                  kv(2, 8), kv(1, 8), kv(0, 8), kv(2, 16), kv(1, 16), kv(0, 16),
                  pl.BlockSpec((QB, 128), lambda hp, b, i: (qrow(b, i), hp)),
                  pl.BlockSpec((QB, 128), lambda hp, b, i: (qrow(b, i), hp)),
                  pl.BlockSpec((None, 8, QB), lambda hp, b, i: (hp, 0, qrow(b, i))),
                  pl.BlockSpec((None, KW, 2 * QB), lambda hp, b, i: (hp, 0, 0))],
        out_specs=[pl.BlockSpec((QB, 128), lambda hp, b, i: (qrow(b, i), hp)), late, late,
                   pl.BlockSpec((None, KW, 2 * QB), lambda hp, b, i: (hp, 0, 0))],
        out_shape=[jax.ShapeDtypeStruct((T, 1024), BF16)] * 3 + [jax.ShapeDtypeStruct((ATT_HEADS // 2, KW, 2 * QB), F32)],
        scratch_shapes=[pltpu.VMEM((3, QB, 128), F32), pltpu.VMEM((3, QB, 128), F32)], compiler_params=_params(3),
    )(qkv, qkv, qkv, qkv, qkv, qkv, qkv, do, o, lse, bias_t)


def _tok_tile(T, want):
    t = min(T, want)
    assert T % t == 0
    return t


def _step(x, tgt, slab, slab_rel, place, mix_g, gn_g, mlp_g, fin_g):
    B, S, D = x.shape
    T = B * S
    h0 = x.reshape(T, D)
    tgt = tgt.reshape(T, D)
    tm = _tok_tile(T, 1024)
    tb = _tok_tile(T, 512)
    tq = _tok_tile(T, 256)
    tt = _tok_tile(T, 8192)
    tf = _tok_tile(T, 2048)
    consts = _ret_consts(S, RET_BLOCK)
    w = {}

    (w["ret_w_in"],) = _all_gather_slabs([slab["ret_w_in"]])
    (proj, hn0), (w["ret_w_out"], w["mlp_w1_0"]) = _mm_nn(
        h0, w["ret_w_in"], "col", tm=tb, out_dtype=BF16, name="ret_in", norm_g=mix_g[0],
        carry=_gather_carry([slab["ret_w_out"], slab["mlp_w1_0"]]))
    y_ret, states = _ret_fwd(proj, gn_g, consts, B=B, S=S)
    h1, (w["mlp_w2_0"],) = _mm_nn(y_ret, w["ret_w_out"], "row", tm=tm, out_dtype=F32, name="ret_out", res=h0,
                                  carry=_gather_carry([slab["mlp_w2_0"]]))
    (a0, hm0), (w["att_w_in"], w["att_w_out"], rel_slabs) = _mm_nn(
        h1, w["mlp_w1_0"], "col", tm=tb, out_dtype=BF16, name="mlp0_up", norm_g=mlp_g[0], act="relu",
        carry=_gather_carry([slab["att_w_in"], slab["att_w_out"], slab_rel]))
    h2, (w["mlp_w1_1"],) = _mm_nn(a0, w["mlp_w2_0"], "row", tm=tb, out_dtype=F32, name="mlp0_down", square=True, res=h1,
                                  carry=_gather_carry([slab["mlp_w1_1"]]))
    rel_bias = jnp.transpose(rel_slabs, (1, 0, 2)).reshape(ATT_HEADS, REL_TABLE)
    bias_t = _att_bias(rel_bias)
    (qkv, hn1), (w["mlp_w2_1"],) = _mm_nn(h2, w["att_w_in"], "col", tm=tb, out_dtype=BF16, name="att_in", norm_g=mix_g[1],
                                          carry=_gather_carry([slab["mlp_w2_1"]]))
    o_att, lse = _att_fwd(qkv, bias_t, B=B, S=S)
    h3 = _mm_nn(o_att, w["att_w_out"], "row", tm=tm, out_dtype=F32, name="att_out", res=h2)
    a1, hm1 = _mm_nn(h3, w["mlp_w1_1"], "col", tm=tb, out_dtype=BF16, name="mlp1_up", norm_g=mlp_g[1], act="relu")
    dh4, d_fin_g, _, loss = _mm_nn(a1, w["mlp_w2_1"], "row", tm=tb, out_dtype=F32, name="mlp1_down", square=True, res=h3,
                                   loss=(tgt, fin_g))

    gw = {}
    gw["mlp_w2_1"] = _mm_tn(a1, dh4, "row", tt=tf, tk=1024, tn=D, name="d_mlp1_w2", square=True)
    dz1 = _mm_nt(dh4, w["mlp_w2_1"], "row", tm=tb, name="d_mlp1_act", epi="relu2bwd", a2=a1)
    gw["mlp_w1_1"] = _mm_tn(hm1, dz1, "col", tt=tt, tk=D, tn=256, name="d_mlp1_w1")
    dh3, d_mlp_g1 = _mm_nt(dz1, w["mlp_w1_1"], "col", tm=tb, name="d_mlp1_in", epi="normbwd",
                           h=h3, g=mlp_g[1], dres=dh4)
    gw["att_w_out"] = _mm_tn(o_att, dh3, "row", tt=tf, tk=256, tn=D, name="d_att_wout")
    do_att = _mm_nt(dh3, w["att_w_out"], "row", tm=tb, name="d_att_o", epi="bf16")
    dq, dk, dv, dbias_t = _att_bwd(qkv, do_att, o_att, lse, bias_t, B=B, S=S)
    d_rel = _att_bias_grad(dbias_t)
    dqkv = (dq, dk, dv)
    gw["att_w_in"] = _mm_tn(hn1, dqkv, "col", tt=tt, tk=D, tn=256, name="d_att_win")
    sums, landed = {}, {}

    def swap_halves(names):
        return _sibling_carry([gw[n] for n in names])

    def add_halves(names, carried):
        for t, n in enumerate(names):
            sums[n] = _add_sibling(place, carried[t], carried[len(names) + t], name="chip_sum_" + n)

    def carried_exchange(names):
        return _chips_carry([sums[n][0] for n in names])

    def keep(names, carried):
        landed.update(zip(names, carried[len(names):]))

    layer1 = ["mlp_w1_1", "mlp_w2_1", "att_w_in", "att_w_out"]
    (dh2, d_mix_g1), carried = _mm_nt(dqkv, w["att_w_in"], "col", tm=tb, name="d_att_in", epi="normbwd",
                                      h=h2, g=mix_g[1], dres=dh3, carry=swap_halves(layer1))
    add_halves(layer1, carried)
    gw["mlp_w2_0"] = _mm_tn(a0, dh2, "row", tt=tf, tk=1024, tn=D, name="d_mlp0_w2", square=True)
    dz0, carried = _mm_nt(dh2, w["mlp_w2_0"], "row", tm=tb, name="d_mlp0_act", epi="relu2bwd", a2=a0,
                          carry=carried_exchange(["mlp_w1_1"]))
    keep(["mlp_w1_1"], carried)
    gw["mlp_w1_0"], carried = _mm_tn(hm0, dz0, "col", tt=tt, tk=D, tn=256, name="d_mlp0_w1", carry=carried_exchange(["mlp_w2_1"]))
    keep(["mlp_w2_1"], carried)
    (dh1, d_mlp_g0), carried = _mm_nt(dz0, w["mlp_w1_0"], "col", tm=tb, name="d_mlp0_in", epi="normbwd",
                                      h=h1, g=mlp_g[0], dres=dh2,
                                      carry=_merge(carried_exchange(["att_w_in", "att_w_out"]), swap_halves(["mlp_w1_0", "mlp_w2_0"])))
    keep(["att_w_in", "att_w_out"], carried[:4])
    add_halves(["mlp_w1_0", "mlp_w2_0"], carried[4:])
    gw["ret_w_out"] = _mm_tn(y_ret, dh1, "row", tt=tf, tk=512, tn=D, name="d_ret_wout")
    dy_ret, carried = _mm_nt(dh1, w["ret_w_out"], "row", tm=tb, name="d_ret_y", epi="bf16", carry=swap_halves(["ret_w_out"]))
    add_halves(["ret_w_out"], carried)
    dproj, d_gn = _ret_bwd(proj, dy_ret, states, gn_g, consts, B=B, S=S)
    gw["ret_w_in"], carried = _mm_tn(hn0, dproj, "col", tt=tt, tk=D, tn=256, name="d_ret_win",
                                     carry=carried_exchange(["mlp_w1_0", "mlp_w2_0", "ret_w_out"]))
    keep(["mlp_w1_0", "mlp_w2_0", "ret_w_out"], carried)
    add_halves(["ret_w_in"], [gw["ret_w_in"]] + list(_exchange_with_sibling([gw["ret_w_in"]], "ret_in")))
    done = [n for n in BIG if n != "ret_w_in"]
    shards = {n: _add_chips(place, sums[n][1], landed[n], name="total_" + n) for n in done}
    (dx, d_mix_g0), carried = _mm_nt(dproj, w["ret_w_in"], "col", tm=tb, name="d_ret_in", epi="normbwd", h=h0, g=mix_g[0], dres=dh1,
                                     carry=_merge(carried_exchange(["ret_w_in"]), _share_carry([shards[n] for n in done])))
    keep(["ret_w_in"], carried[:2])
    shards.update(zip(done, carried[2:]))
    (shards["ret_w_in"],) = _share_with_sibling([_add_chips(place, sums["ret_w_in"][1], landed["ret_w_in"], name="total_ret_w_in")])
    small = [d_mix_g0, d_mix_g1, d_mlp_g0, d_mlp_g1, d_fin_g, d_gn.reshape(2, D), d_rel.reshape(5, D), loss]
    return dx.reshape(B, S, D), shards, small


def _row_tile(r, want=256):
    t = min(r, want)
    assert r % t == 0
    return t


def _into_slab(place, a, layer, dtype, name):
    _, r, c = a.shape
    tr = _row_tile(r)

    def body(place_ref, a_ref, o_ref):
        o_ref[...] = a_ref[...].astype(dtype)

    grid_spec = pltpu.PrefetchScalarGridSpec(
        num_scalar_prefetch=1, grid=(r // tr,), in_specs=[pl.BlockSpec((None, tr, c), lambda i, pr: (layer, i, 0))],
        out_specs=pl.BlockSpec((None, tr, c), lambda i, pr: (pr[0], i, 0)),
    )
    return pl.pallas_call(
        body, name=name, grid_spec=grid_spec, out_shape=jax.ShapeDtypeStruct((N_CHIPS, r, c), dtype), compiler_params=_params(1),
    )(place, a)


def _add_sibling(place, g, recv, name):
    _, r, c = g.shape
    hr = r // 2
    tr = _row_tile(hr)
    nrt = hr // tr

    def body(place_ref, g_ref, r_ref, sb_ref, own_ref):
        v = g_ref[...] + r_ref[...]
        sb_ref[...] = v.astype(BF16)

        @pl.when(pl.program_id(1) == place_ref[0])
        def _():
            own_ref[...] = v

    grid_spec = pltpu.PrefetchScalarGridSpec(
        num_scalar_prefetch=1, grid=(nrt, N_CHIPS),
        in_specs=[pl.BlockSpec((None, tr, c), lambda i, s, pr: (s, pr[1] * nrt + i, 0)),
                  pl.BlockSpec((None, tr, c), lambda i, s, pr: (s, i, 0))],
        out_specs=[pl.BlockSpec((None, tr, c), lambda i, s, pr: (s, i, 0)), pl.BlockSpec((tr, c), lambda i, s, pr: (i, 0))],
    )
    return pl.pallas_call(
        body, name=name, grid_spec=grid_spec,
        out_shape=[jax.ShapeDtypeStruct((N_CHIPS, hr, c), BF16), jax.ShapeDtypeStruct((hr, c), F32)],
        compiler_params=_params(2),
    )(place, g, recv)


def _add_chips(place, own, recv, name):
    hr, c = own.shape
    tr = _row_tile(hr)
    nrt = hr // tr

    def body(place_ref, o_ref, r_ref, t_ref):
        t_ref[...] = ((o_ref[...] + r_ref[0].astype(F32)) + r_ref[1].astype(F32)) + r_ref[2].astype(F32)

    grid_spec = pltpu.PrefetchScalarGridSpec(
        num_scalar_prefetch=1, grid=(nrt,),
        in_specs=[pl.BlockSpec((tr, c), lambda i, pr: (i, 0)), pl.BlockSpec((3, tr, c), lambda i, pr: (0, i, 0))],
        out_specs=pl.BlockSpec((tr, c), lambda i, pr: (pr[1] * nrt + i, 0)),
    )
    return pl.pallas_call(
        body, name=name, grid_spec=grid_spec, out_shape=jax.ShapeDtypeStruct((2 * hr, c), F32), compiler_params=_params(1),
    )(place, own, recv)


def _adamw(w, gs, m, v, name):
    L, r, c = w.shape
    tr = _row_tile(r)
    assert len(gs) == L

    def body(*refs):
        w_ref, m_ref, v_ref = refs[:3]
        g_refs = refs[3:3 + L]
        go_ref, d_ref, nm_ref, nv_ref = refs[3 + L:]
        gg = g_refs[0][...]
        for k in range(1, L):
            gg = jnp.where(pl.program_id(0) == k, g_refs[k][...], gg)
        go_ref[...] = gg
        nm = ADAM_B1 * m_ref[...] + (1.0 - ADAM_B1) * gg
        nv = ADAM_B2 * v_ref[...] + (1.0 - ADAM_B2) * (gg * gg)
        m_hat = nm / (1.0 - ADAM_B1 ** ADAM_STEP)
        v_hat = nv / (1.0 - ADAM_B2 ** ADAM_STEP)
        d_ref[...] = -ADAM_LR * (m_hat / (jnp.sqrt(v_hat) + ADAM_EPS) + ADAM_WD * w_ref[...])
        nm_ref[...] = nm
        nv_ref[...] = nv

    spec = pl.BlockSpec((None, tr, c), lambda l, i: (l, i, 0))
    return pl.pallas_call(
        body, name=name, grid=(L, r // tr), in_specs=[spec] * 3 + [pl.BlockSpec((tr, c), lambda l, i: (i, 0))] * L,
        out_specs=[spec] * 4, out_shape=[jax.ShapeDtypeStruct((L, r, c), F32)] * 4, compiler_params=_params(2),
    )(w, m, v, *gs)


def _place():
    return lax.axis_index("x"), lax.axis_index("y"), lax.axis_index("c")


def _other_chips(x, y):
    return [(1 - x, y), (x, 1 - y), (1 - x, 1 - y)]


def _remote(src, dst, ssem, rsem, dev):
    return pltpu.make_async_remote_copy(src_ref=src, dst_ref=dst, send_sem=ssem, recv_sem=rsem, device_id=dev,
                                        device_id_type=MESH)


def _gather_phases(n):
    def geometry(refs, t):
        x, y, c = _place()
        hr = refs[t].shape[1] // 2
        chips = _other_chips(x, y)
        return x, y, c, 2 * x + y, chips, [2 * qx + qy for qx, qy in chips], pl.ds(c * hr, hr), pl.ds((1 - c) * hr, hr)

    def send(refs, sems):
        s1, r1, _, _ = sems
        for t in range(n):
            x, y, c, p, chips, cidx, mine, theirs = geometry(refs, t)
            for j, (qx, qy) in enumerate(chips):
                _remote(refs[t].at[p, mine], refs[t].at[p, mine], s1.at[t, j], r1.at[t, j], (qx, qy, c)).start()

    def pass_on(refs, sems):
        s1, r1, s2, r2 = sems
        for t in range(n):
            x, y, c, p, chips, cidx, mine, theirs = geometry(refs, t)
            for j, (qx, qy) in enumerate(chips):
                got = refs[t].at[cidx[j], mine]
                _remote(got, got, s1.at[t, j], r1.at[t, j], (qx, qy, c)).wait_recv()
                _remote(got, got, s2.at[t, j], r2.at[t, j], (x, y, 1 - c)).start()

    def finish(refs, sems):
        s1, r1, s2, r2 = sems
        for t in range(n):
            x, y, c, p, chips, cidx, mine, theirs = geometry(refs, t)
            for j, (qx, qy) in enumerate(chips):
                got = refs[t].at[cidx[j], theirs]
                _remote(got, got, s2.at[t, j], r2.at[t, j], (x, y, 1 - c)).wait_recv()
        for t in range(n):
            x, y, c, p, chips, cidx, mine, theirs = geometry(refs, t)
            for j, (qx, qy) in enumerate(chips):
                _remote(refs[t].at[p, mine], refs[t].at[p, mine], s1.at[t, j], r1.at[t, j], (qx, qy, c)).wait_send()
                sent = refs[t].at[cidx[j], mine]
                _remote(sent, sent, s2.at[t, j], r2.at[t, j], (x, y, 1 - c)).wait_send()

    sem = pltpu.SemaphoreType.DMA
    return send, pass_on, finish, [sem((n, 3)), sem((n, 3)), sem((n, 3)), sem((n, 3))]


def _gather_carry(slabs):
    send, pass_on, finish, sems = _gather_phases(len(slabs))
    return _Carry(slabs, sems, [(0, send), (-2, pass_on), (-1, finish)])


def _all_gather_slabs(slabs):
    n = len(slabs)
    send, pass_on, finish, sems = _gather_phases(n)

    def body(*refs):
        outs, scratch = refs[n:2 * n], refs[2 * n:]
        send(outs, scratch)
        pass_on(outs, scratch)
        finish(outs, scratch)

    return pl.pallas_call(
        body, name="gather_weights", in_specs=[HBM_SPEC] * n, out_specs=[HBM_SPEC] * n,
        out_shape=[jax.ShapeDtypeStruct(s.shape, s.dtype) for s in slabs], input_output_aliases={t: t for t in range(n)},
        scratch_shapes=sems, compiler_params=pltpu.CompilerParams(has_side_effects=True),
    )(*slabs)


def _sibling_phases(n):
    def copies(refs, sems):
        ssem, rsem = sems
        x, y, c = _place()
        out = []
        for t in range(n):
            hr = refs[t].shape[1] // 2
            out.append(_remote(refs[t].at[:, pl.ds((1 - c) * hr, hr), :], refs[n + t], ssem.at[t], rsem.at[t], (x, y, 1 - c)))
        return out

    def send(refs, sems):
        for cp in copies(refs, sems):
            cp.start()

    def finish(refs, sems):
        for cp in copies(refs, sems):
            cp.wait()

    sem = pltpu.SemaphoreType.DMA
    return send, finish, [sem((n,)), sem((n,))]


def _halves_landing(grads):
    return [lax.empty((N_CHIPS, g.shape[1] // 2, g.shape[2]), g.dtype) for g in grads]


def _sibling_carry(grads):
    send, finish, sems = _sibling_phases(len(grads))
    return _Carry(list(grads) + _halves_landing(grads), sems, [(0, send), (-1, finish)])


def _exchange_with_sibling(grads, tag):
    n = len(grads)
    send, finish, sems = _sibling_phases(n)

    def body(*refs):
        both, scratch = refs[:2 * n], refs[2 * n:]
        send(both, scratch)
        finish(both, scratch)

    return pl.pallas_call(
        body, name="grads_to_sibling_" + tag, in_specs=[HBM_SPEC] * n, out_specs=[HBM_SPEC] * n,
        out_shape=[jax.ShapeDtypeStruct((N_CHIPS, g.shape[1] // 2, g.shape[2]), g.dtype) for g in grads],
        scratch_shapes=sems, compiler_params=pltpu.CompilerParams(has_side_effects=True),
    )(*grads)


def _chips_phases(n):
    def copies(refs, sems):
        ssem, rsem = sems
        x, y, c = _place()
        return [_remote(refs[t].at[2 * qx + qy], refs[n + t].at[j], ssem.at[t, j], rsem.at[t, j], (qx, qy, c))
                for t in range(n) for j, (qx, qy) in enumerate(_other_chips(x, y))]

    def send(refs, sems):
        for cp in copies(refs, sems):
            cp.start()

    def finish(refs, sems):
        for cp in copies(refs, sems):
            cp.wait()

    sem = pltpu.SemaphoreType.DMA
    return send, finish, [sem((n, 3)), sem((n, 3))]


def _landing(sums):
    return [lax.empty((3,) + s.shape[1:], s.dtype) for s in sums]


def _chips_carry(sums):
    send, finish, sems = _chips_phases(len(sums))
    return _Carry(list(sums) + _landing(sums), sems, [(0, send), (-1, finish)])


def _share_phases(n):
    def halves(refs, t):
        x, y, c = _place()
        hr = refs[t].shape[0] // 2
        return refs[t].at[pl.ds(c * hr, hr)], refs[t].at[pl.ds((1 - c) * hr, hr)], (x, y, 1 - c)

    def send(refs, sems):
        ssem, rsem = sems
        for t in range(n):
            mine, _, sibling = halves(refs, t)
            _remote(mine, mine, ssem.at[t], rsem.at[t], sibling).start()

    def finish(refs, sems):
        ssem, rsem = sems
        for t in range(n):
            mine, theirs, sibling = halves(refs, t)
            _remote(theirs, theirs, ssem.at[t], rsem.at[t], sibling).wait_recv()
            _remote(mine, mine, ssem.at[t], rsem.at[t], sibling).wait_send()

    sem = pltpu.SemaphoreType.DMA
    return send, finish, [sem((n,)), sem((n,))]


def _share_carry(shards):
    send, finish, sems = _share_phases(len(shards))
    return _Carry(shards, sems, [(0, send), (-1, finish)])


def _share_with_sibling(shards):
    n = len(shards)
    send, finish, sems = _share_phases(n)

    def body(*refs):
        outs, scratch = refs[n:2 * n], refs[2 * n:]
        send(outs, scratch)
        finish(outs, scratch)

    return pl.pallas_call(
        body, name="grads_share", in_specs=[HBM_SPEC] * n, out_specs=[HBM_SPEC] * n,
        out_shape=[jax.ShapeDtypeStruct(s.shape, s.dtype) for s in shards], input_output_aliases={t: t for t in range(n)},
        scratch_shapes=sems, compiler_params=pltpu.CompilerParams(has_side_effects=True),
    )(*shards)


def _all_reduce_small(buf):
    R, C = buf.shape

    def body(in_ref, out_ref, gather, ssem, rsem):
        x, y, c = _place()
        me = 4 * x + 2 * y + c
        gather[me] = in_ref[...]
        flips = [(fx, fy, fc) for fx in (0, 1) for fy in (0, 1) for fc in (0, 1) if fx or fy or fc]
        peers = [(x + fx - 2 * x * fx, y + fy - 2 * y * fy, c + fc - 2 * c * fc) for fx, fy, fc in flips]
        copies = [_remote(in_ref, gather.at[me], ssem.at[k], rsem.at[k], peer) for k, peer in enumerate(peers)]
        for cp in copies:
            cp.start()
        for k, (px, py, pc) in enumerate(peers):
            _remote(in_ref, gather.at[4 * px + 2 * py + pc], ssem.at[k], rsem.at[k], (px, py, pc)).wait_recv()
        for cp in copies:
            cp.wait_send()
        acc = gather[0]
        for d in range(1, 8):
            acc = acc + gather[d]
        out_ref[...] = acc

    sem = pltpu.SemaphoreType.DMA
    vmem = pl.BlockSpec(memory_space=pltpu.VMEM)
    return pl.pallas_call(
        body, name="small_grads_sum", in_specs=[vmem], out_specs=vmem, out_shape=jax.ShapeDtypeStruct((R, C), F32),
        scratch_shapes=[pltpu.VMEM((8, R, C), F32), sem((7,)), sem((7,))],
        compiler_params=pltpu.CompilerParams(has_side_effects=True),
    )(buf)


BIG = ["ret_w_in", "ret_w_out", "att_w_in", "att_w_out", "mlp_w1_0", "mlp_w1_1", "mlp_w2_0", "mlp_w2_1"]
LAYER_OF = {"ret_w_in": ("ret_w_in", 0), "ret_w_out": ("ret_w_out", 0), "att_w_in": ("att_w_in", 0), "att_w_out": ("att_w_out", 0),
            "mlp_w1_0": ("mlp_w1", 0), "mlp_w1_1": ("mlp_w1", 1), "mlp_w2_0": ("mlp_w2", 0), "mlp_w2_1": ("mlp_w2", 1)}
ORDER = ["mix_norm_g", "ret_w_in", "ret_gn_g", "ret_w_out", "att_w_in", "att_rel_bias", "att_w_out", "mlp_norm_g", "mlp_w1", "mlp_w2",
         "final_norm_g"]


def kernel(x, mix_norm_g, ret_w_in, ret_gn_g, ret_w_out, att_w_in, att_rel_bias, att_w_out, mlp_norm_g, mlp_w1, mlp_w2, final_norm_g, loss_target, m_mix_norm_g, m_ret_w_in, m_ret_gn_g, m_ret_w_out, m_att_w_in, m_att_rel_bias, m_att_w_out, m_mlp_norm_g, m_mlp_w1, m_mlp_w2, m_final_norm_g, v_mix_norm_g, v_ret_w_in, v_ret_gn_g, v_ret_w_out, v_att_w_in, v_att_rel_bias, v_att_w_out, v_mlp_norm_g, v_mlp_w1, v_mlp_w2, v_final_norm_g):
    xi, yi, ci = _place()
    chip = 2 * xi + yi
    weights = dict(zip(ORDER, (mix_norm_g, ret_w_in, ret_gn_g, ret_w_out, att_w_in, att_rel_bias, att_w_out, mlp_norm_g, mlp_w1,
                               mlp_w2, final_norm_g)))
    first = dict(zip(ORDER, (m_mix_norm_g, m_ret_w_in, m_ret_gn_g, m_ret_w_out, m_att_w_in, m_att_rel_bias, m_att_w_out,
                             m_mlp_norm_g, m_mlp_w1, m_mlp_w2, m_final_norm_g)))
    second = dict(zip(ORDER, (v_mix_norm_g, v_ret_w_in, v_ret_gn_g, v_ret_w_out, v_att_w_in, v_att_rel_bias, v_att_w_out,
                              v_mlp_norm_g, v_mlp_w1, v_mlp_w2, v_final_norm_g)))

    place = jnp.stack([chip, ci]).astype(jnp.int32)
    slab = {n: _into_slab(place, weights[LAYER_OF[n][0]], LAYER_OF[n][1], BF16, name="cast_" + n) for n in BIG}
    slab_rel = _into_slab(place, att_rel_bias, 0, F32, name="slab_rel_bias")

    grad_x, g_big, small = _step(x, loss_target, slab, slab_rel, place, mix_norm_g, ret_gn_g[0], mlp_norm_g, final_norm_g)

    rows, at = jnp.zeros((16, D_MODEL), F32), 0
    for part in small:
        rows = rows + jnp.pad(part, ((at, 16 - at - part.shape[0]), (0, D_MODEL - part.shape[1])))
        at += part.shape[0]
    rows = _all_reduce_small(rows)
    loss = rows[12, 0]
    grads = {"mix_norm_g": [rows[0:2]], "mlp_norm_g": [rows[2:4]], "final_norm_g": [rows[4:5]], "ret_gn_g": [rows[5:7].reshape(1, 2048)],
             "att_rel_bias": [lax.dynamic_slice_in_dim(rows[7:12].reshape(ATT_HEADS, REL_TABLE), chip * (REL_TABLE // N_CHIPS),
                                                       REL_TABLE // N_CHIPS, axis=1)]}
    for n in BIG:
        grads.setdefault(LAYER_OF[n][0], []).append(g_big[n])

    def as3(a):
        return a.reshape((1,) * (3 - a.ndim) + a.shape)

    results = {}
    for n in ORDER:
        outs = _adamw(as3(weights[n]), grads[n], as3(first[n]), as3(second[n]), name="adamw_" + n)
        results[n] = [o.reshape(weights[n].shape) for o in outs]
    return (loss, grad_x) + tuple(results[n][k] for k in range(4) for n in ORDER)
```

```python
import math

import jax
import jax.numpy as jnp
from jax import lax
from jax.experimental import pallas as pl
from jax.experimental.pallas import tpu as pltpu

F32 = jnp.float32
BF16 = jnp.bfloat16
MESH = pl.DeviceIdType.MESH

D_MODEL = 1024
CHUNK = 64
RET_HEADS = 4
RET_DK = 256
RET_DV = 512
ROPE_BASE = 10000.0
ATT_HEADS = 16
ATT_DH = 64
PAST = 512
MAX_REL = 256
REL_TABLE = MAX_REL + CHUNK
EPS = 1e-6
NEG = -1e30
N_CHIPS = 4

ADAM_LR = 0.001
ADAM_B1 = 0.9
ADAM_B2 = 0.999
ADAM_EPS = 1e-08
ADAM_WD = 0.01
ADAM_STEP = 10

RET_BLOCK = 256
ATT_BLOCK = 256
VMEM_LIMIT = 56 * 1024 * 1024


def _params(n_axes, **kw):
    return pltpu.CompilerParams(dimension_semantics=("arbitrary",) * n_axes, vmem_limit_bytes=VMEM_LIMIT, **kw)


def _dot(a, b):
    return jnp.dot(a, b, preferred_element_type=F32)


def _dot_nt(a, b):
    return lax.dot_general(a, b, (((1,), (1,)), ((), ())), preferred_element_type=F32)


def _dot_tn(a, b):
    return lax.dot_general(a, b, (((0,), (0,)), ((), ())), preferred_element_type=F32)


def _sigmoid(x):
    return 0.5 * jnp.tanh(0.5 * x) + 0.5


HBM_SPEC = pl.BlockSpec(memory_space=pltpu.HBM)


class _Carry:
    def __init__(self, arrays, sems, stages):
        self.arrays, self.sems, self.stages = list(arrays), list(sems), list(stages)


def _merge(a, b):
    na, sa = len(a.arrays), len(a.sems)

    def of_a(fn):
        return lambda refs, sems: fn(refs[:na], sems[:sa])

    def of_b(fn):
        return lambda refs, sems: fn(refs[na:], sems[sa:])

    return _Carry(a.arrays + b.arrays, a.sems + b.sems,
                  [(at, of_a(fn)) for at, fn in a.stages] + [(at, of_b(fn)) for at, fn in b.stages])


def _carry_call(body, carry, *, name, grid, in_specs, out_specs, out_shape, args):
    if carry is None:
        outs = pl.pallas_call(body, name=name, grid=grid, in_specs=in_specs, out_specs=out_specs, out_shape=out_shape,
                              compiler_params=_params(len(grid)))(*args)
        return list(outs), []
    n_in, n_out, n_c = len(in_specs), len(out_specs), len(carry.arrays)
    steps = 1
    for g in grid:
        steps *= g
    assert all(-steps <= at < steps for at, _ in carry.stages)

    def carrying(*refs):
        ins, outs = refs[:n_in], refs[n_in + n_c:n_in + n_c + n_out]
        carried = refs[n_in + n_c + n_out:n_in + 2 * n_c + n_out]
        sems = refs[n_in + 2 * n_c + n_out:]
        step = pl.program_id(0)
        for axis in range(1, len(grid)):
            step = step * grid[axis] + pl.program_id(axis)
        for at, fn in carry.stages:
            if at == 0:
                pl.when(step == 0)(lambda fn=fn: fn(carried, sems))
        body(*ins, *outs)
        for at, fn in carry.stages:
            if at != 0:
                pl.when(step == at % steps)(lambda fn=fn: fn(carried, sems))

    outs = pl.pallas_call(
        carrying, name=name, grid=grid, in_specs=list(in_specs) + [HBM_SPEC] * n_c, out_specs=list(out_specs) + [HBM_SPEC] * n_c,
        out_shape=list(out_shape) + [jax.ShapeDtypeStruct(a.shape, a.dtype) for a in carry.arrays],
        input_output_aliases={n_in + t: n_out + t for t in range(n_c)}, scratch_shapes=carry.sems,
        compiler_params=_params(len(grid), has_side_effects=True),
    )(*args, *carry.arrays)
    return list(outs[:n_out]), list(outs[n_out:])


def _mm_nn(a, w, wkind, *, tm, out_dtype, name, norm_g=None, act=None, square=False, res=None, loss=None, carry=None):
    M, K = a.shape
    cols = w.shape[2]
    N = N_CHIPS * cols if wkind == "col" else cols
    has_norm = norm_g is not None
    steps = M // tm
    assert M % tm == 0 and K == (w.shape[1] if wkind == "col" else N_CHIPS * w.shape[1])
    assert loss is None or (wkind == "row" and act is None)

    def body(*refs):
        it = iter(refs)
        a_ref, w_ref = next(it), next(it)
        g_ref = next(it) if has_norm else None
        r_ref = next(it) if res is not None else None
        if loss is not None:
            t_ref, fg_ref = next(it), next(it)
        o_ref = next(it)
        hn_ref = next(it) if has_norm else None
        if loss is not None:
            dg_ref, sq_ref, loss_ref, o16_ref = next(it), next(it), next(it), next(it)
        if has_norm:
            x = a_ref[...].astype(F32)
            r = lax.rsqrt(jnp.mean(x * x, axis=-1, keepdims=True) + EPS)
            lhs = (x * r * g_ref[...]).astype(BF16)
            hn_ref[...] = lhs
        elif square:
            lhs = a_ref[...].astype(BF16)
            lhs = lhs * lhs
        else:
            lhs = a_ref[...].astype(BF16)

        def loss_head(x):
            i = pl.program_id(0)
            r = lax.rsqrt(jnp.mean(x * x, axis=-1, keepdims=True) + EPS)
            xh = x * r
            gg = fg_ref[...]
            diff = xh * gg - t_ref[...]
            sq = jnp.sum(diff * diff, axis=0, keepdims=True)
            dy = diff * (1.0 / N)
            dg_part = jnp.sum(dy * xh, axis=0, keepdims=True)

            @pl.when(i == 0)
            def _():
                sq_ref[...] = sq
                dg_ref[...] = dg_part

            @pl.when(i > 0)
            def _():
                sq_ref[...] += sq
                dg_ref[...] += dg_part
            t = dy * gg
            d = r * (t - xh * jnp.mean(t * xh, axis=-1, keepdims=True))
            o_ref[...] = d
            o16_ref[...] = d.astype(BF16)

            @pl.when(i == steps - 1)
            def _():
                loss_ref[...] = (0.5 / N) * jnp.sum(sq_ref[...], axis=1, keepdims=True)

        def finish(acc, sl):
            if act == "relu":
                acc = jnp.maximum(acc, 0.0)
            if r_ref is not None:
                acc = acc + r_ref[:, sl]
            if loss is not None:
                loss_head(acc)
            else:
                o_ref[:, sl] = acc.astype(out_dtype)

        if wkind == "col":
            for s in range(N_CHIPS):
                finish(_dot(lhs, w_ref[s]), slice(s * cols, (s + 1) * cols))
        else:
            finish(_dot(lhs, w_ref[...].reshape(K, N)), slice(None))

    in_specs = [pl.BlockSpec((tm, K), lambda i: (i, 0)), pl.BlockSpec(w.shape, lambda i: (0, 0, 0))]
    args = [a, w]
    if has_norm:
        in_specs.append(pl.BlockSpec((1, K), lambda i: (0, 0)))
        args.append(norm_g.reshape(1, K))
    if res is not None:
        in_specs.append(pl.BlockSpec((tm, N), lambda i: (i, 0)))
        args.append(res)
    if loss is not None:
        in_specs += [pl.BlockSpec((tm, N), lambda i: (i, 0)), pl.BlockSpec((1, N), lambda i: (0, 0))]
        args += [loss[0], loss[1].reshape(1, N)]
    out_shape = [jax.ShapeDtypeStruct((M, N), out_dtype)]
    out_specs = [pl.BlockSpec((tm, N), lambda i: (i, 0))]
    if has_norm:
        out_shape.append(jax.ShapeDtypeStruct((M, K), BF16))
        out_specs.append(pl.BlockSpec((tm, K), lambda i: (i, 0)))
    if loss is not None:
        out_shape += [jax.ShapeDtypeStruct((1, N), F32), jax.ShapeDtypeStruct((1, N), F32), jax.ShapeDtypeStruct((1, 1), F32),
                      jax.ShapeDtypeStruct((M, N), BF16)]
        out_specs += [pl.BlockSpec((1, N), lambda i: (0, 0)), pl.BlockSpec((1, N), lambda i: (0, 0)),
                      pl.BlockSpec((1, 1), lambda i: (0, 0)), pl.BlockSpec((tm, N), lambda i: (i, 0))]
    outs, carried = _carry_call(body, carry, name=name, grid=(steps,), in_specs=in_specs, out_specs=out_specs,
                                out_shape=out_shape, args=args)
    result = outs if has_norm or loss is not None else outs[0]
    return result if carry is None else (result, carried)


def _mm_nt(a, w, wkind, *, tm, name, epi, a2=None, h=None, g=None, dres=None, bf16_copy=False, carry=None):
    parts = tuple(a) if isinstance(a, (tuple, list)) else (a,)
    M, part_w = parts[0].shape
    Nw = part_w * len(parts)
    rows, cols = w.shape[1], w.shape[2]
    Kw = rows if wkind == "col" else N_CHIPS * rows
    assert M % tm == 0 and Nw == (N_CHIPS * cols if wkind == "col" else cols)
    assert epi != "normbwd" or wkind == "col"
    assert len(parts) == 1 or wkind == "col"
    chunk = math.gcd(part_w, cols)

    def body(*refs):
        it = iter(refs)
        a_refs = [next(it) for _ in parts]
        a_ref, w_ref = a_refs[0], next(it)
        a2_ref = next(it) if epi == "relu2bwd" else None
        if epi == "normbwd":
            h_ref, g_ref, dres_ref = next(it), next(it), next(it)
        o_ref = next(it)
        dg_ref = next(it) if epi == "normbwd" else None
        o16_ref = next(it) if bf16_copy else None
        i = pl.program_id(0)

        def finish(acc, sl):
            if epi == "bf16":
                o_ref[:, sl] = acc.astype(BF16)
            elif epi == "relu2bwd":
                o_ref[:, sl] = (acc * (2.0 * a2_ref[:, sl].astype(F32))).astype(BF16)
            else:
                x = h_ref[...]
                r = lax.rsqrt(jnp.mean(x * x, axis=-1, keepdims=True) + EPS)
                xh = x * r
                dg_part = jnp.sum(acc * xh, axis=0, keepdims=True)

                @pl.when(i == 0)
                def _():
                    dg_ref[...] = dg_part

                @pl.when(i > 0)
                def _():
                    dg_ref[...] += dg_part
                t = acc * g_ref[...]
                d = dres_ref[...] + r * (t - xh * jnp.mean(t * xh, axis=-1, keepdims=True))
                o_ref[...] = d
                if bf16_copy:
                    o16_ref[...] = d.astype(BF16)

        if wkind == "col":
            acc = None
            for lo in range(0, Nw, chunk):
                (src, a_lo), (s, w_lo) = divmod(lo, part_w), divmod(lo, cols)
                part = _dot_nt(a_refs[src][:, a_lo:a_lo + chunk].astype(BF16), w_ref[s, :, w_lo:w_lo + chunk])
                acc = part if acc is None else acc + part
            finish(acc, slice(None))
        else:
            lhs = a_ref[...].astype(BF16)
            for s in range(N_CHIPS):
                finish(_dot_nt(lhs, w_ref[s]), slice(s * rows, (s + 1) * rows))

    in_specs = [pl.BlockSpec((tm, part_w), lambda i: (i, 0)) for _ in parts] + [pl.BlockSpec(w.shape, lambda i: (0, 0, 0))]
    args = [*parts, w]
    out_dtype = BF16
    if epi == "relu2bwd":
        in_specs.append(pl.BlockSpec((tm, Kw), lambda i: (i, 0)))
        args.append(a2)
    if epi == "normbwd":
        in_specs += [pl.BlockSpec((tm, Kw), lambda i: (i, 0)), pl.BlockSpec((1, Kw), lambda i: (0, 0)),
                     pl.BlockSpec((tm, Kw), lambda i: (i, 0))]
        args += [h, g.reshape(1, Kw), dres]
        out_dtype = F32
    out_shape = [jax.ShapeDtypeStruct((M, Kw), out_dtype)]
    out_specs = [pl.BlockSpec((tm, Kw), lambda i: (i, 0))]
    if epi == "normbwd":
        out_shape.append(jax.ShapeDtypeStruct((1, Kw), F32))
        out_specs.append(pl.BlockSpec((1, Kw), lambda i: (0, 0)))
    assert not bf16_copy or epi == "normbwd"
    if bf16_copy:
        out_shape.append(jax.ShapeDtypeStruct((M, Kw), BF16))
        out_specs.append(pl.BlockSpec((tm, Kw), lambda i: (i, 0)))
    outs, carried = _carry_call(body, carry, name=name, grid=(M // tm,), in_specs=in_specs, out_specs=out_specs,
                                out_shape=out_shape, args=args)
    result = outs if epi == "normbwd" else outs[0]
    return result if carry is None else (result, carried)


def _mm_tn(a, b, okind, *, tt, tk, tn, name, square=False, carry=None):
    parts = tuple(b) if isinstance(b, (tuple, list)) else (b,)
    T, K = a.shape
    part_w = parts[0].shape[1]
    N = part_w * len(parts)
    assert T % tt == 0 and K % tk == 0 and part_w % tn == 0
    nt = T // tt
    per_part = part_w // tn
    if okind == "col":
        per = (N // N_CHIPS) // tn
        assert (N // N_CHIPS) % tn == 0
        out_shape = jax.ShapeDtypeStruct((N_CHIPS, K, N // N_CHIPS), F32)
        out_spec = pl.BlockSpec((None, tk, tn), lambda ki, nj, t: (nj // per, ki, nj % per))
    else:
        per = (K // N_CHIPS) // tk
        assert (K // N_CHIPS) % tk == 0
        out_shape = jax.ShapeDtypeStruct((N_CHIPS, K // N_CHIPS, N), F32)
        out_spec = pl.BlockSpec((None, tk, tn), lambda ki, nj, t: (ki // per, ki % per, nj))

    def body(a_ref, *refs):
        b_refs, o_ref = refs[:-1], refs[-1]
        nj, t = pl.program_id(1), pl.program_id(2)

        def accumulate(b_ref):
            lhs = a_ref[...].astype(BF16)
            part = _dot_tn(lhs * lhs if square else lhs, b_ref[...].astype(BF16))
            if nt == 1:
                o_ref[...] = part
            else:
                @pl.when(t == 0)
                def _():
                    o_ref[...] = part

                @pl.when(t > 0)
                def _():
                    o_ref[...] += part

        if len(parts) == 1:
            accumulate(b_refs[0])
        else:
            for s, b_ref in enumerate(b_refs):
                pl.when(nj // per_part == s)(lambda b_ref=b_ref: accumulate(b_ref))

    def b_spec(s):
        return pl.BlockSpec((tt, tn), lambda ki, nj, t: (t, jnp.clip(nj - s * per_part, 0, per_part - 1)))

    outs, carried = _carry_call(
        body, carry, name=name, grid=(K // tk, N // tn, nt),
        in_specs=[pl.BlockSpec((tt, tk), lambda ki, nj, t: (t, ki))] + [b_spec(s) for s in range(len(parts))],
        out_specs=[out_spec], out_shape=[out_shape], args=[a, *parts])
    return outs[0] if carry is None else (outs[0], carried)


def _ret_consts(S, LB):
    log_gamma = jnp.log1p(-jnp.exp2(-5.0 - jnp.arange(RET_HEADS, dtype=F32)))
    idx = jnp.arange(LB, dtype=F32)
    n, m = idx[:, None], idx[None, :]
    cn, cm = jnp.floor(n / CHUNK), jnp.floor(m / CHUNK)
    dist = jnp.where(cm == cn, jnp.abs(n - m), n - m)
    dmat = jnp.where((cm <= cn)[None], jnp.exp(log_gamma[:, None, None] * dist[None]), 0.0)
    qd = jnp.exp(log_gamma[:, None] * (idx + 1.0)[None, :])[..., None]
    kd = jnp.exp(log_gamma[:, None] * (LB - 1 - idx)[None, :])[..., None]
    bd = jnp.exp(log_gamma * LB).reshape(RET_HEADS, 1, 1) * jnp.ones((RET_HEADS, 1, 128), F32)
    half = RET_DK // 2
    inv = jnp.exp(-jnp.log(ROPE_BASE) * jnp.arange(half, dtype=F32) / half)
    ang = jnp.arange(S, dtype=F32)[:, None] * inv[None, :]
    return dmat.astype(F32), qd.astype(F32), kd.astype(F32), bd, jnp.cos(ang), jnp.sin(ang)


def _rope(t, c, s):
    t1, t2 = t[:, :128], t[:, 128:]
    return jnp.concatenate([t1 * c - t2 * s, t1 * s + t2 * c], axis=-1)


def _rope_inv(d, c, s):
    d1, d2 = d[:, :128], d[:, 128:]
    return jnp.concatenate([d1 * c + d2 * s, d2 * c - d1 * s], axis=-1)


def _ret_block_fwd(p_ref, h, c, s, d_ref, qd_ref, kd_ref, stb):
    q = _rope(p_ref[:, h * RET_DK:(h + 1) * RET_DK].astype(F32), c, s)
    k = _rope(p_ref[:, 1024 + h * RET_DK:1024 + (h + 1) * RET_DK].astype(F32), c, s) * (RET_DK ** -0.5)
    v = p_ref[:, 2048 + h * RET_DV:2048 + (h + 1) * RET_DV]
    qb, kb = q.astype(BF16), k.astype(BF16)
    scb = (_dot_nt(qb, kb) * d_ref[h]).astype(BF16)
    o = _dot(scb, v) + qd_ref[h] * _dot(qb, stb)
    return q, k, qb, kb, v, scb, o


def _ret_fwd(proj, gn_g, consts, *, B, S):
    LB = RET_BLOCK
    nb = S // LB
    T = B * S
    dmat, qd, kd, bd, cos, sin = consts

    def body(p_ref, cos_ref, sin_ref, d_ref, qd_ref, kd_ref, bd_ref, gng_ref, y_ref, st_ref, state_s):
        i = pl.program_id(1)

        @pl.when(i == 0)
        def _():
            state_s[...] = jnp.zeros_like(state_s)
        c, s = cos_ref[...], sin_ref[...]
        for h in range(RET_HEADS):
            st = state_s[h]
            stb = st.astype(BF16)
            st_ref[h] = stb
            q, k, qb, kb, v, scb, o = _ret_block_fwd(p_ref, h, c, s, d_ref, qd_ref, kd_ref, stb)
            kdk = (k * kd_ref[h]).astype(BF16)
            state_s[h] = st * bd_ref[h][:, :1] + _dot_tn(kdk, v)
            gate = p_ref[:, 4096 + h * RET_DV:4096 + (h + 1) * RET_DV].astype(F32)
            mu = jnp.mean(o, axis=-1, keepdims=True)
            oc = o - mu
            xh = oc * lax.rsqrt(jnp.mean(oc * oc, axis=-1, keepdims=True) + EPS)
            y = (gate * _sigmoid(gate)) * (xh * gng_ref[:, h * RET_DV:(h + 1) * RET_DV])
            y_ref[:, h * RET_DV:(h + 1) * RET_DV] = y.astype(BF16)

    const = lambda b, i: (0, 0, 0)
    return pl.pallas_call(
        body, name="ret_fwd", grid=(B, nb),
        in_specs=[pl.BlockSpec((LB, 6144), lambda b, i: (b * nb + i, 0)),
                  pl.BlockSpec((LB, 128), lambda b, i: (i, 0)), pl.BlockSpec((LB, 128), lambda b, i: (i, 0)),
                  pl.BlockSpec((RET_HEADS, LB, LB), const), pl.BlockSpec((RET_HEADS, LB, 1), const),
                  pl.BlockSpec((RET_HEADS, LB, 1), const), pl.BlockSpec((RET_HEADS, 1, 128), const),
                  pl.BlockSpec((1, 2048), lambda b, i: (0, 0))],
        out_specs=[pl.BlockSpec((LB, 2048), lambda b, i: (b * nb + i, 0)),
                   pl.BlockSpec((None, None, RET_HEADS, RET_DK, RET_DV), lambda b, i: (b, i, 0, 0, 0))],
        out_shape=[jax.ShapeDtypeStruct((T, 2048), BF16), jax.ShapeDtypeStruct((B, nb, RET_HEADS, RET_DK, RET_DV), BF16)],
        scratch_shapes=[pltpu.VMEM((RET_HEADS, RET_DK, RET_DV), F32)], compiler_params=_params(2),
    )(proj, cos, sin, dmat, qd, kd, bd, gn_g.reshape(1, 2048))


def _ret_bwd(proj, dy, states, gn_g, consts, *, B, S):
    LB = RET_BLOCK
    nb = S // LB
    T = B * S
    dmat, qd, kd, bd, cos, sin = consts

    def body(p_ref, dy_ref, st_ref, cos_ref, sin_ref, d_ref, qd_ref, kd_ref, bd_ref, gng_ref, dp_ref, dgn_ref, dstate_s):
        b, i = pl.program_id(0), pl.program_id(1)

        @pl.when(i == 0)
        def _():
            dstate_s[...] = jnp.zeros_like(dstate_s)

        @pl.when((b == 0) & (i == 0))
        def _():
            dgn_ref[...] = jnp.zeros_like(dgn_ref)
        c, s = cos_ref[...], sin_ref[...]
        for h in range(RET_HEADS):
            vs = slice(h * RET_DV, (h + 1) * RET_DV)
            stb = st_ref[h]
            q, k, qb, kb, v, scb, o = _ret_block_fwd(p_ref, h, c, s, d_ref, qd_ref, kd_ref, stb)
            gate = p_ref[:, 4096 + h * RET_DV:4096 + (h + 1) * RET_DV].astype(F32)
            mu = jnp.mean(o, axis=-1, keepdims=True)
            oc = o - mu
            rstd = lax.rsqrt(jnp.mean(oc * oc, axis=-1, keepdims=True) + EPS)
            xh = oc * rstd
            gng = gng_ref[:, vs]
            dyh = dy_ref[:, vs].astype(F32)
            sg = _sigmoid(gate)
            silu = gate * sg
            dgn_ref[:, vs] += jnp.sum(dyh * silu * xh, axis=0, keepdims=True)
            dxh = dyh * silu * gng
            do = rstd * (dxh - jnp.mean(dxh, axis=-1, keepdims=True) - xh * jnp.mean(dxh * xh, axis=-1, keepdims=True))
            dgate = dyh * xh * gng * (sg * (1.0 + gate * (1.0 - sg)))
            dob = do.astype(BF16)
            dsb = (_dot_nt(dob, v) * d_ref[h]).astype(BF16)
            dst = dstate_s[h]
            dstb = dst.astype(BF16)
            kdk = (k * kd_ref[h]).astype(BF16)
            dqr = _dot(dsb, kb) + qd_ref[h] * _dot_nt(dob, stb)
            dkr = _dot_tn(dsb, qb) + kd_ref[h] * _dot_nt(v, dstb)
            dv = _dot_tn(scb, dob) + _dot(kdk, dstb)
            dstate_s[h] = dst * bd_ref[h][:, :1] + _dot_tn((q * qd_ref[h]).astype(BF16), dob)
            dp_ref[:, h * RET_DK:(h + 1) * RET_DK] = _rope_inv(dqr, c, s).astype(BF16)
            dp_ref[:, 1024 + h * RET_DK:1024 + (h + 1) * RET_DK] = (_rope_inv(dkr, c, s) * (RET_DK ** -0.5)).astype(BF16)
            dp_ref[:, 2048 + h * RET_DV:2048 + (h + 1) * RET_DV] = dv.astype(BF16)
            dp_ref[:, 4096 + h * RET_DV:4096 + (h + 1) * RET_DV] = dgate.astype(BF16)

    const = lambda b, i: (0, 0, 0)
    rev = lambda b, i: (b * nb + nb - 1 - i, 0)
    return pl.pallas_call(
        body, name="ret_bwd", grid=(B, nb),
        in_specs=[pl.BlockSpec((LB, 6144), rev), pl.BlockSpec((LB, 2048), rev),
                  pl.BlockSpec((None, None, RET_HEADS, RET_DK, RET_DV), lambda b, i: (b, nb - 1 - i, 0, 0, 0)),
                  pl.BlockSpec((LB, 128), lambda b, i: (nb - 1 - i, 0)), pl.BlockSpec((LB, 128), lambda b, i: (nb - 1 - i, 0)),
                  pl.BlockSpec((RET_HEADS, LB, LB), const), pl.BlockSpec((RET_HEADS, LB, 1), const),
                  pl.BlockSpec((RET_HEADS, LB, 1), const), pl.BlockSpec((RET_HEADS, 1, 128), const),
                  pl.BlockSpec((1, 2048), lambda b, i: (0, 0))],
        out_specs=[pl.BlockSpec((LB, 6144), rev), pl.BlockSpec((1, 2048), lambda b, i: (0, 0))],
        out_shape=[jax.ShapeDtypeStruct((T, 6144), BF16), jax.ShapeDtypeStruct((1, 2048), F32)],
        scratch_shapes=[pltpu.VMEM((RET_HEADS, RET_DK, RET_DV), F32)], compiler_params=_params(2),
    )(proj, dy, states, cos, sin, dmat, qd, kd, bd, gn_g.reshape(1, 2048))


BIAS_LANES = 4 * ATT_BLOCK


def _diag_onehot():
    r = lax.broadcasted_iota(jnp.int32, (REL_TABLE, BIAS_LANES), 0)
    j = lax.broadcasted_iota(jnp.int32, (REL_TABLE, BIAS_LANES), 1)
    idx = jnp.maximum(j - ATT_BLOCK - PAST, -MAX_REL) + MAX_REL
    return jnp.where(idx == r, 1.0, 0.0).astype(F32)


def _row_is(j):
    return lax.broadcasted_iota(jnp.int32, (8, BIAS_LANES), 0) == j


def _att_bias(table):
    QB, KW = ATT_BLOCK, 3 * ATT_BLOCK

    def body(t_ref, bt_ref):
        row = jnp.broadcast_to(t_ref[...], (8, REL_TABLE))
        diag = jnp.dot(row, _diag_onehot(), preferred_element_type=F32, precision=lax.Precision.HIGHEST)
        rows = jnp.zeros((8, BIAS_LANES), F32)
        for j in range(8):
            rows = jnp.where(_row_is(j), diag if j == 0 else pltpu.roll(diag, j, axis=1), rows)
        n = 8
        while n < QB:
            rows = jnp.concatenate([rows, pltpu.roll(rows, n, axis=1)], axis=0)
            n *= 2
        bias = rows[:, QB:]
        qi = lax.broadcasted_iota(jnp.int32, (QB, KW), 0)
        kj = lax.broadcasted_iota(jnp.int32, (QB, KW), 1)
        lo = (qi // CHUNK) * CHUNK
        bt_ref[...] = jnp.where((kj >= lo) & (kj < lo + PAST + CHUNK), bias, NEG).T

    return pl.pallas_call(
        body, name="att_bias", grid=(ATT_HEADS,),
        in_specs=[pl.BlockSpec((None, 1, REL_TABLE), lambda h: (h, 0, 0))],
        out_specs=pl.BlockSpec((None, KW, QB), lambda h: (h // 2, 0, h % 2)),
        out_shape=jax.ShapeDtypeStruct((ATT_HEADS // 2, KW, 2 * QB), F32),
        compiler_params=_params(1),
    )(table.reshape(ATT_HEADS, 1, REL_TABLE))


def _att_bias_grad(dbias_t):
    QB, KW = ATT_BLOCK, 3 * ATT_BLOCK

    def body(d_ref, o_ref):
        rows = jnp.concatenate([jnp.zeros((QB, QB), F32), d_ref[...].T], axis=1)
        n = QB // 2
        while n >= 8:
            rows = rows[:n] + pltpu.roll(rows[n:], BIAS_LANES - n, axis=1)
            n //= 2
        acc = jnp.zeros((8, BIAS_LANES), F32)
        for j in range(8):
            acc = acc + jnp.where(_row_is(j), rows if j == 0 else pltpu.roll(rows, BIAS_LANES - j, axis=1), 0.0)
        diag = jnp.broadcast_to(jnp.sum(acc, axis=0, keepdims=True), (8, BIAS_LANES))
        grad = lax.dot_general(diag, _diag_onehot(), (((1,), (1,)), ((), ())), preferred_element_type=F32,
                               precision=lax.Precision.HIGHEST)
        o_ref[...] = grad[:1]

    return pl.pallas_call(
        body, name="att_bias_grad", grid=(ATT_HEADS,),
        in_specs=[pl.BlockSpec((None, KW, QB), lambda h: (h // 2, 0, h % 2))],
        out_specs=pl.BlockSpec((None, 1, REL_TABLE), lambda h: (h, 0, 0)),
        out_shape=jax.ShapeDtypeStruct((ATT_HEADS, 1, REL_TABLE), F32), compiler_params=_params(1),
    )(dbias_t).reshape(ATT_HEADS, REL_TABLE)


def _by_head(x):
    first = lax.broadcasted_iota(jnp.int32, x.shape, 1) < ATT_DH
    zero = jnp.zeros_like(x)
    return jnp.concatenate([jnp.where(first, x, zero), jnp.where(first, zero, x)], axis=0)


def _att_fwd(qkv, bias_t, *, B, S):
    QB = ATT_BLOCK
    nb = S // QB
    KW = 3 * QB
    T = B * S
    scale = ATT_DH ** -0.5

    def body(q_ref, k0, k1, k2, v0, v1, v2, b_ref, o_ref, lse_ref, s_blk):
        i = pl.program_id(2)
        q2 = _by_head((q_ref[...].astype(F32) * scale).astype(BF16))
        m = jnp.full((1, 2 * QB), NEG, F32)
        for d, k_ref in enumerate((k0, k1, k2)):
            st = _dot_nt(k_ref[...], q2) + b_ref[d * QB:(d + 1) * QB, :]
            st = jnp.where(i + d >= 2, st, NEG)
            s_blk[d] = st
            m = jnp.maximum(m, jnp.max(st, axis=0, keepdims=True))
        l = jnp.zeros((1, 2 * QB), F32)
        o_t = jnp.zeros((128, 2 * QB), F32)
        for d, v_ref in enumerate((v0, v1, v2)):
            e = jnp.exp(s_blk[d] - m)
            l = l + jnp.sum(e, axis=0, keepdims=True)
            o_t = o_t + _dot(v_ref[...].astype(F32).T.astype(BF16), e.astype(BF16))
        o_t = o_t / l
        row = lax.broadcasted_iota(jnp.int32, (128, QB), 0)
        o_ref[...] = jnp.where(row < ATT_DH, o_t[:, :QB], o_t[:, QB:]).T.astype(BF16)
        lse = m + jnp.log(l)
        row8 = lax.broadcasted_iota(jnp.int32, (8, QB), 0)
        lse_ref[...] = jnp.where(row8 == 0, lse[:, :QB], jnp.where(row8 == 1, lse[:, QB:], 0.0))

    def kv(d, col0):
        return pl.BlockSpec((QB, 128), lambda hp, b, i: (b * nb + jnp.maximum(i - d, 0), col0 + hp))

    return pl.pallas_call(
        body, name="att_fwd", grid=(8, B, nb),
        in_specs=[pl.BlockSpec((QB, 128), lambda hp, b, i: (b * nb + i, hp)),
                  kv(2, 8), kv(1, 8), kv(0, 8), kv(2, 16), kv(1, 16), kv(0, 16),
                  pl.BlockSpec((None, KW, 2 * QB), lambda hp, b, i: (hp, 0, 0))],
        out_specs=[pl.BlockSpec((QB, 128), lambda hp, b, i: (b * nb + i, hp)),
                   pl.BlockSpec((None, 8, QB), lambda hp, b, i: (hp, 0, b * nb + i))],
        out_shape=[jax.ShapeDtypeStruct((T, 1024), BF16), jax.ShapeDtypeStruct((8, 8, T), F32)],
        scratch_shapes=[pltpu.VMEM((3, QB, 2 * QB), F32)], compiler_params=_params(3),
    )(qkv, qkv, qkv, qkv, qkv, qkv, qkv, bias_t)


def _att_bwd(qkv, do, o, lse, bias_t, *, B, S):
    QB = ATT_BLOCK
    nb = S // QB
    KW = 3 * QB
    T = B * S
    scale = ATT_DH ** -0.5
    TK = 256

    def body(q_ref, k0, k1, k2, v0, v1, v2, do_ref, o_ref, lse_ref, b_ref, dq_ref, dk_ref, dv_ref, db_ref, dk_acc, dv_acc):
        b, i = pl.program_id(1), pl.program_id(2)

        @pl.when(i == 0)
        def _():
            dk_acc[...] = jnp.zeros_like(dk_acc)
            dv_acc[...] = jnp.zeros_like(dv_acc)

        @pl.when((b == 0) & (i == 0))
        def _():
            db_ref[...] = jnp.zeros_like(db_ref)

        def write_out():
            slot = (i + 1) % 3
            dk_ref[...] = dk_acc[slot].astype(BF16)
            dv_ref[...] = dv_acc[slot].astype(BF16)
            dk_acc[slot] = jnp.zeros((QB, 128), F32)
            dv_acc[slot] = jnp.zeros((QB, 128), F32)

        @pl.when(i < nb)
        def _():
            dout = do_ref[...]
            q2 = _by_head((q_ref[...].astype(F32) * scale).astype(BF16))
            do2 = _by_head(dout)
            delta_t = (o_ref[...].astype(F32) * dout.astype(F32)).T
            delta2 = jnp.concatenate([jnp.sum(delta_t[:ATT_DH], axis=0, keepdims=True),
                                      jnp.sum(delta_t[ATT_DH:], axis=0, keepdims=True)], axis=1)
            lse2 = jnp.concatenate([lse_ref[0:1, :], lse_ref[1:2, :]], axis=1)
            dq_t = jnp.zeros((128, 2 * QB), F32)
            for d, (k_ref, v_ref) in enumerate(((k0, v0), (k1, v1), (k2, v2))):
                kblk, vblk = k_ref[...], v_ref[...]
                kt = kblk.astype(F32).T.astype(BF16)
                lse_d = jnp.where(i + d >= 2, lse2, -NEG)
                slot = (i + 1 + d) % 3
                for t in range(QB // TK):
                    rows = slice(t * TK, (t + 1) * TK)
                    wrows = slice(d * QB + t * TK, d * QB + (t + 1) * TK)
                    p = jnp.exp(_dot_nt(kblk[rows], q2) + b_ref[wrows, :] - lse_d)
                    ds = p * (_dot_nt(vblk[rows], do2) - delta2)
                    db_ref[wrows, :] += ds
                    dsb = ds.astype(BF16)
                    dk_acc[slot, rows, :] += _dot(dsb, q2)
                    dv_acc[slot, rows, :] += _dot(p.astype(BF16), do2)
                    dq_t += _dot(kt[:, rows], dsb)
            row = lax.broadcasted_iota(jnp.int32, (128, QB), 0)
            dq_ref[...] = (jnp.where(row < ATT_DH, dq_t[:, :QB], dq_t[:, QB:]) * scale).T.astype(BF16)
            write_out()

        pl.when(i >= nb)(write_out)

    def qrow(b, i):
        return b * nb + jnp.minimum(i, nb - 1)

    def kv(d, col0):
        return pl.BlockSpec((QB, 128), lambda hp, b, i: (b * nb + jnp.maximum(jnp.minimum(i, nb - 1) - d, 0), col0 + hp))

    late = pl.BlockSpec((QB, 128), lambda hp, b, i: (b * nb + jnp.maximum(i - 2, 0), hp))
    return pl.pallas_call(
        body, name="att_bwd", grid=(8, B, nb + 2),
        in_specs=[pl.BlockSpec((QB, 128), lambda hp, b, i: (qrow(b, i), hp)),
                  kv(2, 8), kv(1, 8), kv(0, 8), kv(2, 16), kv(1, 16), kv(0, 16),
                  pl.BlockSpec((QB, 128), lambda hp, b, i: (qrow(b, i), hp)),
                  pl.BlockSpec((QB, 128), lambda hp, b, i: (qrow(b, i), hp)),
                  pl.BlockSpec((None, 8, QB), lambda hp, b, i: (hp, 0, qrow(b, i))),
                  pl.BlockSpec((None, KW, 2 * QB), lambda hp, b, i: (hp, 0, 0))],
        out_specs=[pl.BlockSpec((QB, 128), lambda hp, b, i: (qrow(b, i), hp)), late, late,
                   pl.BlockSpec((None, KW, 2 * QB), lambda hp, b, i: (hp, 0, 0))],
        out_shape=[jax.ShapeDtypeStruct((T, 1024), BF16)] * 3 + [jax.ShapeDtypeStruct((ATT_HEADS // 2, KW, 2 * QB), F32)],
        scratch_shapes=[pltpu.VMEM((3, QB, 128), F32), pltpu.VMEM((3, QB, 128), F32)], compiler_params=_params(3),
    )(qkv, qkv, qkv, qkv, qkv, qkv, qkv, do, o, lse, bias_t)


def _tok_tile(T, want):
    t = min(T, want)
    assert T % t == 0
    return t


def _step(x, tgt, slab, slab_rel, place, mix_g, gn_g, mlp_g, fin_g):
    B, S, D = x.shape
    T = B * S
    h0 = x.reshape(T, D)
    tgt = tgt.reshape(T, D)
    tm = _tok_tile(T, 1024)
    tb = _tok_tile(T, 512)
    tq = _tok_tile(T, 256)
    tt = _tok_tile(T, 8192)
    consts = _ret_consts(S, RET_BLOCK)
    w = {}

    (w["ret_w_in"],) = _all_gather_slabs([slab["ret_w_in"]])
    (proj, hn0), (w["ret_w_out"], w["mlp_w1_0"]) = _mm_nn(
        h0, w["ret_w_in"], "col", tm=tb, out_dtype=BF16, name="ret_in", norm_g=mix_g[0],
        carry=_gather_carry([slab["ret_w_out"], slab["mlp_w1_0"]]))
    y_ret, states = _ret_fwd(proj, gn_g, consts, B=B, S=S)
    h1, (w["mlp_w2_0"],) = _mm_nn(y_ret, w["ret_w_out"], "row", tm=tm, out_dtype=F32, name="ret_out", res=h0,
                                  carry=_gather_carry([slab["mlp_w2_0"]]))
    (a0, hm0), (w["att_w_in"], w["att_w_out"], rel_slabs) = _mm_nn(
        h1, w["mlp_w1_0"], "col", tm=tb, out_dtype=BF16, name="mlp0_up", norm_g=mlp_g[0], act="relu",
        carry=_gather_carry([slab["att_w_in"], slab["att_w_out"], slab_rel]))
    h2, (w["mlp_w1_1"],) = _mm_nn(a0, w["mlp_w2_0"], "row", tm=tb, out_dtype=F32, name="mlp0_down", square=True, res=h1,
                                  carry=_gather_carry([slab["mlp_w1_1"]]))
    rel_bias = jnp.transpose(rel_slabs, (1, 0, 2)).reshape(ATT_HEADS, REL_TABLE)
    bias_t = _att_bias(rel_bias)
    (qkv, hn1), (w["mlp_w2_1"],) = _mm_nn(h2, w["att_w_in"], "col", tm=tb, out_dtype=BF16, name="att_in", norm_g=mix_g[1],
                                          carry=_gather_carry([slab["mlp_w2_1"]]))
    o_att, lse = _att_fwd(qkv, bias_t, B=B, S=S)
    h3 = _mm_nn(o_att, w["att_w_out"], "row", tm=tm, out_dtype=F32, name="att_out", res=h2)
    a1, hm1 = _mm_nn(h3, w["mlp_w1_1"], "col", tm=tb, out_dtype=BF16, name="mlp1_up", norm_g=mlp_g[1], act="relu")
    dh4, d_fin_g, _, loss, dh4_b = _mm_nn(a1, w["mlp_w2_1"], "row", tm=tb, out_dtype=F32, name="mlp1_down", square=True, res=h3,
                                          loss=(tgt, fin_g))

    gw = {}
    gw["mlp_w2_1"] = _mm_tn(a1, dh4_b, "row", tt=tt, tk=1024, tn=256, name="d_mlp1_w2", square=True)
    dz1 = _mm_nt(dh4_b, w["mlp_w2_1"], "row", tm=tb, name="d_mlp1_act", epi="relu2bwd", a2=a1)
    gw["mlp_w1_1"] = _mm_tn(hm1, dz1, "col", tt=tt, tk=D, tn=256, name="d_mlp1_w1")
    dh3, d_mlp_g1, dh3_b = _mm_nt(dz1, w["mlp_w1_1"], "col", tm=tb, name="d_mlp1_in", epi="normbwd",
                                  h=h3, g=mlp_g[1], dres=dh4, bf16_copy=True)
    gw["att_w_out"] = _mm_tn(o_att, dh3_b, "row", tt=tt, tk=256, tn=256, name="d_att_wout")
    do_att = _mm_nt(dh3_b, w["att_w_out"], "row", tm=tb, name="d_att_o", epi="bf16")
    dq, dk, dv, dbias_t = _att_bwd(qkv, do_att, o_att, lse, bias_t, B=B, S=S)
    d_rel = _att_bias_grad(dbias_t)
    dqkv = (dq, dk, dv)
    gw["att_w_in"] = _mm_tn(hn1, dqkv, "col", tt=tt, tk=D, tn=256, name="d_att_win")
    sums, landed = {}, {}

    def swap_halves(names):
        return _sibling_carry([gw[n] for n in names])

    def add_halves(names, carried):
        for t, n in enumerate(names):
            sums[n] = _add_sibling(place, carried[t], carried[len(names) + t], name="chip_sum_" + n)

    def carried_exchange(names):
        return _chips_carry([sums[n][0] for n in names])

    def keep(names, carried):
        landed.update(zip(names, carried[len(names):]))

    layer1 = ["mlp_w1_1", "mlp_w2_1", "att_w_in", "att_w_out"]
    (dh2, d_mix_g1, dh2_b), carried = _mm_nt(dqkv, w["att_w_in"], "col", tm=tb, name="d_att_in", epi="normbwd",
                                             h=h2, g=mix_g[1], dres=dh3, bf16_copy=True, carry=swap_halves(layer1))
    add_halves(layer1, carried)
    gw["mlp_w2_0"] = _mm_tn(a0, dh2_b, "row", tt=tt, tk=1024, tn=256, name="d_mlp0_w2", square=True)
    dz0, carried = _mm_nt(dh2_b, w["mlp_w2_0"], "row", tm=tb, name="d_mlp0_act", epi="relu2bwd", a2=a0,
                          carry=carried_exchange(["mlp_w1_1"]))
    keep(["mlp_w1_1"], carried)
    gw["mlp_w1_0"], carried = _mm_tn(hm0, dz0, "col", tt=tt, tk=D, tn=256, name="d_mlp0_w1", carry=carried_exchange(["mlp_w2_1"]))
    keep(["mlp_w2_1"], carried)
    (dh1, d_mlp_g0, dh1_b), carried = _mm_nt(dz0, w["mlp_w1_0"], "col", tm=tb, name="d_mlp0_in", epi="normbwd",
                                             h=h1, g=mlp_g[0], dres=dh2, bf16_copy=True,
                                             carry=_merge(carried_exchange(["att_w_in", "att_w_out"]),
                                                          swap_halves(["mlp_w1_0", "mlp_w2_0"])))
    keep(["att_w_in", "att_w_out"], carried[:4])
    add_halves(["mlp_w1_0", "mlp_w2_0"], carried[4:])
    gw["ret_w_out"] = _mm_tn(y_ret, dh1_b, "row", tt=tt, tk=512, tn=256, name="d_ret_wout")
    dy_ret, carried = _mm_nt(dh1_b, w["ret_w_out"], "row", tm=tb, name="d_ret_y", epi="bf16", carry=swap_halves(["ret_w_out"]))
    add_halves(["ret_w_out"], carried)
    dproj, d_gn = _ret_bwd(proj, dy_ret, states, gn_g, consts, B=B, S=S)
    gw["ret_w_in"], carried = _mm_tn(hn0, dproj, "col", tt=tt, tk=D, tn=256, name="d_ret_win",
                                     carry=carried_exchange(["mlp_w1_0", "mlp_w2_0", "ret_w_out"]))
    keep(["mlp_w1_0", "mlp_w2_0", "ret_w_out"], carried)
    add_halves(["ret_w_in"], [gw["ret_w_in"]] + list(_exchange_with_sibling([gw["ret_w_in"]], "ret_in")))
    done = [n for n in BIG if n != "ret_w_in"]
    shards = {n: _add_chips(place, sums[n][1], landed[n], name="total_" + n) for n in done}
    (dx, d_mix_g0), carried = _mm_nt(dproj, w["ret_w_in"], "col", tm=tb, name="d_ret_in", epi="normbwd", h=h0, g=mix_g[0], dres=dh1,
                                     carry=_merge(carried_exchange(["ret_w_in"]), _share_carry([shards[n] for n in done])))
    keep(["ret_w_in"], carried[:2])
    shards.update(zip(done, carried[2:]))
    (shards["ret_w_in"],) = _share_with_sibling([_add_chips(place, sums["ret_w_in"][1], landed["ret_w_in"], name="total_ret_w_in")])
    small = [d_mix_g0, d_mix_g1, d_mlp_g0, d_mlp_g1, d_fin_g, d_gn.reshape(2, D), d_rel.reshape(5, D), loss]
    return dx.reshape(B, S, D), shards, small


def _row_tile(r, want=256):
    t = min(r, want)
    assert r % t == 0
    return t


def _into_slab(place, a, layer, dtype, name):
    _, r, c = a.shape
    tr = _row_tile(r)

    def body(place_ref, a_ref, o_ref):
        o_ref[...] = a_ref[...].astype(dtype)

    grid_spec = pltpu.PrefetchScalarGridSpec(
        num_scalar_prefetch=1, grid=(r // tr,), in_specs=[pl.BlockSpec((None, tr, c), lambda i, pr: (layer, i, 0))],
        out_specs=pl.BlockSpec((None, tr, c), lambda i, pr: (pr[0], i, 0)),
    )
    return pl.pallas_call(
        body, name=name, grid_spec=grid_spec, out_shape=jax.ShapeDtypeStruct((N_CHIPS, r, c), dtype), compiler_params=_params(1),
    )(place, a)


def _add_sibling(place, g, recv, name):
    _, r, c = g.shape
    hr = r // 2
    tr = _row_tile(hr)
    nrt = hr // tr

    def body(place_ref, g_ref, r_ref, sb_ref, own_ref):
        v = g_ref[...] + r_ref[...]
        sb_ref[...] = v.astype(BF16)

        @pl.when(pl.program_id(1) == place_ref[0])
        def _():
            own_ref[...] = v

    grid_spec = pltpu.PrefetchScalarGridSpec(
        num_scalar_prefetch=1, grid=(nrt, N_CHIPS),
        in_specs=[pl.BlockSpec((None, tr, c), lambda i, s, pr: (s, pr[1] * nrt + i, 0)),
                  pl.BlockSpec((None, tr, c), lambda i, s, pr: (s, i, 0))],
        out_specs=[pl.BlockSpec((None, tr, c), lambda i, s, pr: (s, i, 0)), pl.BlockSpec((tr, c), lambda i, s, pr: (i, 0))],
    )
    return pl.pallas_call(
        body, name=name, grid_spec=grid_spec,
        out_shape=[jax.ShapeDtypeStruct((N_CHIPS, hr, c), BF16), jax.ShapeDtypeStruct((hr, c), F32)],
        compiler_params=_params(2),
    )(place, g, recv)


def _add_chips(place, own, recv, name):
    hr, c = own.shape
    tr = _row_tile(hr)
    nrt = hr // tr

    def body(place_ref, o_ref, r_ref, t_ref):
        t_ref[...] = ((o_ref[...] + r_ref[0].astype(F32)) + r_ref[1].astype(F32)) + r_ref[2].astype(F32)

    grid_spec = pltpu.PrefetchScalarGridSpec(
        num_scalar_prefetch=1, grid=(nrt,),
        in_specs=[pl.BlockSpec((tr, c), lambda i, pr: (i, 0)), pl.BlockSpec((3, tr, c), lambda i, pr: (0, i, 0))],
        out_specs=pl.BlockSpec((tr, c), lambda i, pr: (pr[1] * nrt + i, 0)),
    )
    return pl.pallas_call(
        body, name=name, grid_spec=grid_spec, out_shape=jax.ShapeDtypeStruct((2 * hr, c), F32), compiler_params=_params(1),
    )(place, own, recv)


def _adamw(w, gs, m, v, name):
    L, r, c = w.shape
    tr = _row_tile(r)
    assert len(gs) == L

    def body(*refs):
        w_ref, m_ref, v_ref = refs[:3]
        g_refs = refs[3:3 + L]
        go_ref, d_ref, nm_ref, nv_ref = refs[3 + L:]
        gg = g_refs[0][...]
        for k in range(1, L):
            gg = jnp.where(pl.program_id(0) == k, g_refs[k][...], gg)
        go_ref[...] = gg
        nm = ADAM_B1 * m_ref[...] + (1.0 - ADAM_B1) * gg
        nv = ADAM_B2 * v_ref[...] + (1.0 - ADAM_B2) * (gg * gg)
        m_hat = nm / (1.0 - ADAM_B1 ** ADAM_STEP)
        v_hat = nv / (1.0 - ADAM_B2 ** ADAM_STEP)
        d_ref[...] = -ADAM_LR * (m_hat / (jnp.sqrt(v_hat) + ADAM_EPS) + ADAM_WD * w_ref[...])
        nm_ref[...] = nm
        nv_ref[...] = nv

    spec = pl.BlockSpec((None, tr, c), lambda l, i: (l, i, 0))
    return pl.pallas_call(
        body, name=name, grid=(L, r // tr), in_specs=[spec] * 3 + [pl.BlockSpec((tr, c), lambda l, i: (i, 0))] * L,
        out_specs=[spec] * 4, out_shape=[jax.ShapeDtypeStruct((L, r, c), F32)] * 4, compiler_params=_params(2),
    )(w, m, v, *gs)


def _place():
    return lax.axis_index("x"), lax.axis_index("y"), lax.axis_index("c")


def _other_chips(x, y):
    return [(1 - x, y), (x, 1 - y), (1 - x, 1 - y)]


def _remote(src, dst, ssem, rsem, dev):
    return pltpu.make_async_remote_copy(src_ref=src, dst_ref=dst, send_sem=ssem, recv_sem=rsem, device_id=dev,
                                        device_id_type=MESH)


def _gather_phases(n):
    def geometry(refs, t):
        x, y, c = _place()
        hr = refs[t].shape[1] // 2
        chips = _other_chips(x, y)
        return x, y, c, 2 * x + y, chips, [2 * qx + qy for qx, qy in chips], pl.ds(c * hr, hr), pl.ds((1 - c) * hr, hr)

    def send(refs, sems):
        s1, r1, _, _ = sems
        for t in range(n):
            x, y, c, p, chips, cidx, mine, theirs = geometry(refs, t)
            for j, (qx, qy) in enumerate(chips):
                _remote(refs[t].at[p, mine], refs[t].at[p, mine], s1.at[t, j], r1.at[t, j], (qx, qy, c)).start()

    def pass_on(refs, sems):
        s1, r1, s2, r2 = sems
        for t in range(n):
            x, y, c, p, chips, cidx, mine, theirs = geometry(refs, t)
            for j, (qx, qy) in enumerate(chips):
                got = refs[t].at[cidx[j], mine]
                _remote(got, got, s1.at[t, j], r1.at[t, j], (qx, qy, c)).wait_recv()
                _remote(got, got, s2.at[t, j], r2.at[t, j], (x, y, 1 - c)).start()

    def finish(refs, sems):
        s1, r1, s2, r2 = sems
        for t in range(n):
            x, y, c, p, chips, cidx, mine, theirs = geometry(refs, t)
            for j, (qx, qy) in enumerate(chips):
                got = refs[t].at[cidx[j], theirs]
                _remote(got, got, s2.at[t, j], r2.at[t, j], (x, y, 1 - c)).wait_recv()
        for t in range(n):
            x, y, c, p, chips, cidx, mine, theirs = geometry(refs, t)
            for j, (qx, qy) in enumerate(chips):
                _remote(refs[t].at[p, mine], refs[t].at[p, mine], s1.at[t, j], r1.at[t, j], (qx, qy, c)).wait_send()
                sent = refs[t].at[cidx[j], mine]
                _remote(sent, sent, s2.at[t, j], r2.at[t, j], (x, y, 1 - c)).wait_send()

    sem = pltpu.SemaphoreType.DMA
    return send, pass_on, finish, [sem((n, 3)), sem((n, 3)), sem((n, 3)), sem((n, 3))]


def _gather_carry(slabs):
    send, pass_on, finish, sems = _gather_phases(len(slabs))
    return _Carry(slabs, sems, [(0, send), (-2, pass_on), (-1, finish)])


def _all_gather_slabs(slabs):
    n = len(slabs)
    send, pass_on, finish, sems = _gather_phases(n)

    def body(*refs):
        outs, scratch = refs[n:2 * n], refs[2 * n:]
        send(outs, scratch)
        pass_on(outs, scratch)
        finish(outs, scratch)

    return pl.pallas_call(
        body, name="gather_weights", in_specs=[HBM_SPEC] * n, out_specs=[HBM_SPEC] * n,
        out_shape=[jax.ShapeDtypeStruct(s.shape, s.dtype) for s in slabs], input_output_aliases={t: t for t in range(n)},
        scratch_shapes=sems, compiler_params=pltpu.CompilerParams(has_side_effects=True),
    )(*slabs)


def _sibling_phases(n):
    def copies(refs, sems):
        ssem, rsem = sems
        x, y, c = _place()
        out = []
        for t in range(n):
            hr = refs[t].shape[1] // 2
            out.append(_remote(refs[t].at[:, pl.ds((1 - c) * hr, hr), :], refs[n + t], ssem.at[t], rsem.at[t], (x, y, 1 - c)))
        return out

    def send(refs, sems):
        for cp in copies(refs, sems):
            cp.start()

    def finish(refs, sems):
        for cp in copies(refs, sems):
            cp.wait()

    sem = pltpu.SemaphoreType.DMA
    return send, finish, [sem((n,)), sem((n,))]


def _halves_landing(grads):
    return [lax.empty((N_CHIPS, g.shape[1] // 2, g.shape[2]), g.dtype) for g in grads]


def _sibling_carry(grads):
    send, finish, sems = _sibling_phases(len(grads))
    return _Carry(list(grads) + _halves_landing(grads), sems, [(0, send), (-1, finish)])


def _exchange_with_sibling(grads, tag):
    n = len(grads)
    send, finish, sems = _sibling_phases(n)

    def body(*refs):
        both, scratch = refs[:2 * n], refs[2 * n:]
        send(both, scratch)
        finish(both, scratch)

    return pl.pallas_call(
        body, name="grads_to_sibling_" + tag, in_specs=[HBM_SPEC] * n, out_specs=[HBM_SPEC] * n,
        out_shape=[jax.ShapeDtypeStruct((N_CHIPS, g.shape[1] // 2, g.shape[2]), g.dtype) for g in grads],
        scratch_shapes=sems, compiler_params=pltpu.CompilerParams(has_side_effects=True),
    )(*grads)


def _chips_phases(n):
    def copies(refs, sems):
        ssem, rsem = sems
        x, y, c = _place()
        return [_remote(refs[t].at[2 * qx + qy], refs[n + t].at[j], ssem.at[t, j], rsem.at[t, j], (qx, qy, c))
                for t in range(n) for j, (qx, qy) in enumerate(_other_chips(x, y))]

    def send(refs, sems):
        for cp in copies(refs, sems):
            cp.start()

    def finish(refs, sems):
        for cp in copies(refs, sems):
            cp.wait()

    sem = pltpu.SemaphoreType.DMA
    return send, finish, [sem((n, 3)), sem((n, 3))]


def _landing(sums):
    return [lax.empty((3,) + s.shape[1:], s.dtype) for s in sums]


def _chips_carry(sums):
    send, finish, sems = _chips_phases(len(sums))
    return _Carry(list(sums) + _landing(sums), sems, [(0, send), (-1, finish)])


def _share_phases(n):
    def halves(refs, t):
        x, y, c = _place()
        hr = refs[t].shape[0] // 2
        return refs[t].at[pl.ds(c * hr, hr)], refs[t].at[pl.ds((1 - c) * hr, hr)], (x, y, 1 - c)

    def send(refs, sems):
        ssem, rsem = sems
        for t in range(n):
            mine, _, sibling = halves(refs, t)
            _remote(mine, mine, ssem.at[t], rsem.at[t], sibling).start()

    def finish(refs, sems):
        ssem, rsem = sems
        for t in range(n):
            mine, theirs, sibling = halves(refs, t)
            _remote(theirs, theirs, ssem.at[t], rsem.at[t], sibling).wait_recv()
            _remote(mine, mine, ssem.at[t], rsem.at[t], sibling).wait_send()

    sem = pltpu.SemaphoreType.DMA
    return send, finish, [sem((n,)), sem((n,))]


def _share_carry(shards):
    send, finish, sems = _share_phases(len(shards))
    return _Carry(shards, sems, [(0, send), (-1, finish)])


def _share_with_sibling(shards):
    n = len(shards)
    send, finish, sems = _share_phases(n)

    def body(*refs):
        outs, scratch = refs[n:2 * n], refs[2 * n:]
        send(outs, scratch)
        finish(outs, scratch)

    return pl.pallas_call(
        body, name="grads_share", in_specs=[HBM_SPEC] * n, out_specs=[HBM_SPEC] * n,
        out_shape=[jax.ShapeDtypeStruct(s.shape, s.dtype) for s in shards], input_output_aliases={t: t for t in range(n)},
        scratch_shapes=sems, compiler_params=pltpu.CompilerParams(has_side_effects=True),
    )(*shards)


def _all_reduce_small(buf):
    R, C = buf.shape

    def body(in_ref, out_ref, gather, ssem, rsem):
        x, y, c = _place()
        me = 4 * x + 2 * y + c
        gather[me] = in_ref[...]
        flips = [(fx, fy, fc) for fx in (0, 1) for fy in (0, 1) for fc in (0, 1) if fx or fy or fc]
        peers = [(x + fx - 2 * x * fx, y + fy - 2 * y * fy, c + fc - 2 * c * fc) for fx, fy, fc in flips]
        copies = [_remote(in_ref, gather.at[me], ssem.at[k], rsem.at[k], peer) for k, peer in enumerate(peers)]
        for cp in copies:
            cp.start()
        for k, (px, py, pc) in enumerate(peers):
            _remote(in_ref, gather.at[4 * px + 2 * py + pc], ssem.at[k], rsem.at[k], (px, py, pc)).wait_recv()
        for cp in copies:
            cp.wait_send()
        acc = gather[0]
        for d in range(1, 8):
            acc = acc + gather[d]
        out_ref[...] = acc

    sem = pltpu.SemaphoreType.DMA
    vmem = pl.BlockSpec(memory_space=pltpu.VMEM)
    return pl.pallas_call(
        body, name="small_grads_sum", in_specs=[vmem], out_specs=vmem, out_shape=jax.ShapeDtypeStruct((R, C), F32),
        scratch_shapes=[pltpu.VMEM((8, R, C), F32), sem((7,)), sem((7,))],
        compiler_params=pltpu.CompilerParams(has_side_effects=True),
    )(buf)


BIG = ["ret_w_in", "ret_w_out", "att_w_in", "att_w_out", "mlp_w1_0", "mlp_w1_1", "mlp_w2_0", "mlp_w2_1"]
LAYER_OF = {"ret_w_in": ("ret_w_in", 0), "ret_w_out": ("ret_w_out", 0), "att_w_in": ("att_w_in", 0), "att_w_out": ("att_w_out", 0),
            "mlp_w1_0": ("mlp_w1", 0), "mlp_w1_1": ("mlp_w1", 1), "mlp_w2_0": ("mlp_w2", 0), "mlp_w2_1": ("mlp_w2", 1)}
ORDER = ["mix_norm_g", "ret_w_in", "ret_gn_g", "ret_w_out", "att_w_in", "att_rel_bias", "att_w_out", "mlp_norm_g", "mlp_w1", "mlp_w2",
         "final_norm_g"]


def kernel(x, mix_norm_g, ret_w_in, ret_gn_g, ret_w_out, att_w_in, att_rel_bias, att_w_out, mlp_norm_g, mlp_w1, mlp_w2, final_norm_g, loss_target, m_mix_norm_g, m_ret_w_in, m_ret_gn_g, m_ret_w_out, m_att_w_in, m_att_rel_bias, m_att_w_out, m_mlp_norm_g, m_mlp_w1, m_mlp_w2, m_final_norm_g, v_mix_norm_g, v_ret_w_in, v_ret_gn_g, v_ret_w_out, v_att_w_in, v_att_rel_bias, v_att_w_out, v_mlp_norm_g, v_mlp_w1, v_mlp_w2, v_final_norm_g):
    xi, yi, ci = _place()
    chip = 2 * xi + yi
    weights = dict(zip(ORDER, (mix_norm_g, ret_w_in, ret_gn_g, ret_w_out, att_w_in, att_rel_bias, att_w_out, mlp_norm_g, mlp_w1,
                               mlp_w2, final_norm_g)))
    first = dict(zip(ORDER, (m_mix_norm_g, m_ret_w_in, m_ret_gn_g, m_ret_w_out, m_att_w_in, m_att_rel_bias, m_att_w_out,
                             m_mlp_norm_g, m_mlp_w1, m_mlp_w2, m_final_norm_g)))
    second = dict(zip(ORDER, (v_mix_norm_g, v_ret_w_in, v_ret_gn_g, v_ret_w_out, v_att_w_in, v_att_rel_bias, v_att_w_out,
                              v_mlp_norm_g, v_mlp_w1, v_mlp_w2, v_final_norm_g)))

    place = jnp.stack([chip, ci]).astype(jnp.int32)
    slab = {n: _into_slab(place, weights[LAYER_OF[n][0]], LAYER_OF[n][1], BF16, name="cast_" + n) for n in BIG}
    slab_rel = _into_slab(place, att_rel_bias, 0, F32, name="slab_rel_bias")

    grad_x, g_big, small = _step(x, loss_target, slab, slab_rel, place, mix_norm_g, ret_gn_g[0], mlp_norm_g, final_norm_g)

    rows, at = jnp.zeros((16, D_MODEL), F32), 0
    for part in small:
        rows = rows + jnp.pad(part, ((at, 16 - at - part.shape[0]), (0, D_MODEL - part.shape[1])))
        at += part.shape[0]
    rows = _all_reduce_small(rows)
    loss = rows[12, 0]
    grads = {"mix_norm_g": [rows[0:2]], "mlp_norm_g": [rows[2:4]], "final_norm_g": [rows[4:5]], "ret_gn_g": [rows[5:7].reshape(1, 2048)],
             "att_rel_bias": [lax.dynamic_slice_in_dim(rows[7:12].reshape(ATT_HEADS, REL_TABLE), chip * (REL_TABLE // N_CHIPS),
                                                       REL_TABLE // N_CHIPS, axis=1)]}
    for n in BIG:
        grads.setdefault(LAYER_OF[n][0], []).append(g_big[n])

    def as3(a):
        return a.reshape((1,) * (3 - a.ndim) + a.shape)

    results = {}
    for n in ORDER:
        outs = _adamw(as3(weights[n]), grads[n], as3(first[n]), as3(second[n]), name="adamw_" + n)
        results[n] = [o.reshape(weights[n].shape) for o in outs]
    return (loss, grad_x) + tuple(results[n][k] for k in range(4) for n in ORDER)
```

```python
import math

import jax
import jax.numpy as jnp
from jax import lax
from jax.experimental import pallas as pl
from jax.experimental.pallas import tpu as pltpu

F32 = jnp.float32
BF16 = jnp.bfloat16
MESH = pl.DeviceIdType.MESH

D_MODEL = 1024
CHUNK = 64
RET_HEADS = 4
RET_DK = 256
RET_DV = 512
ROPE_BASE = 10000.0
ATT_HEADS = 16
ATT_DH = 64
PAST = 512
MAX_REL = 256
REL_TABLE = MAX_REL + CHUNK
EPS = 1e-6
NEG = -1e30
N_CHIPS = 4

ADAM_LR = 0.001
ADAM_B1 = 0.9
ADAM_B2 = 0.999
ADAM_EPS = 1e-08
ADAM_WD = 0.01
ADAM_STEP = 10

RET_BLOCK = 256
ATT_BLOCK = 256
VMEM_LIMIT = 56 * 1024 * 1024


def _params(n_axes, **kw):
    return pltpu.CompilerParams(dimension_semantics=("arbitrary",) * n_axes, vmem_limit_bytes=VMEM_LIMIT, **kw)


def _dot(a, b):
    return jnp.dot(a, b, preferred_element_type=F32)


def _dot_nt(a, b):
    return lax.dot_general(a, b, (((1,), (1,)), ((), ())), preferred_element_type=F32)


def _dot_tn(a, b):
    return lax.dot_general(a, b, (((0,), (0,)), ((), ())), preferred_element_type=F32)


def _sigmoid(x):
    return 0.5 * jnp.tanh(0.5 * x) + 0.5


HBM_SPEC = pl.BlockSpec(memory_space=pltpu.HBM)


class _Carry:
    def __init__(self, arrays, sems, stages):
        self.arrays, self.sems, self.stages = list(arrays), list(sems), list(stages)


def _merge(a, b):
    na, sa = len(a.arrays), len(a.sems)

    def of_a(fn):
        return lambda refs, sems: fn(refs[:na], sems[:sa])

    def of_b(fn):
        return lambda refs, sems: fn(refs[na:], sems[sa:])

    return _Carry(a.arrays + b.arrays, a.sems + b.sems,
                  [(at, of_a(fn)) for at, fn in a.stages] + [(at, of_b(fn)) for at, fn in b.stages])


def _carry_call(body, carry, *, name, grid, in_specs, out_specs, out_shape, args):
    if carry is None:
        outs = pl.pallas_call(body, name=name, grid=grid, in_specs=in_specs, out_specs=out_specs, out_shape=out_shape,
                              compiler_params=_params(len(grid)))(*args)
        return list(outs), []
    n_in, n_out, n_c = len(in_specs), len(out_specs), len(carry.arrays)
    steps = 1
    for g in grid:
        steps *= g
    assert all(-steps <= at < steps for at, _ in carry.stages)

    def carrying(*refs):
        ins, outs = refs[:n_in], refs[n_in + n_c:n_in + n_c + n_out]
        carried = refs[n_in + n_c + n_out:n_in + 2 * n_c + n_out]
        sems = refs[n_in + 2 * n_c + n_out:]
        step = pl.program_id(0)
        for axis in range(1, len(grid)):
            step = step * grid[axis] + pl.program_id(axis)
        for at, fn in carry.stages:
            if at == 0:
                pl.when(step == 0)(lambda fn=fn: fn(carried, sems))
        body(*ins, *outs)
        for at, fn in carry.stages:
            if at != 0:
                pl.when(step == at % steps)(lambda fn=fn: fn(carried, sems))

    outs = pl.pallas_call(
        carrying, name=name, grid=grid, in_specs=list(in_specs) + [HBM_SPEC] * n_c, out_specs=list(out_specs) + [HBM_SPEC] * n_c,
        out_shape=list(out_shape) + [jax.ShapeDtypeStruct(a.shape, a.dtype) for a in carry.arrays],
        input_output_aliases={n_in + t: n_out + t for t in range(n_c)}, scratch_shapes=carry.sems,
        compiler_params=_params(len(grid), has_side_effects=True),
    )(*args, *carry.arrays)
    return list(outs[:n_out]), list(outs[n_out:])


def _mm_nn(a, w, wkind, *, tm, out_dtype, name, norm_g=None, act=None, square=False, res=None, loss=None, carry=None):
    M, K = a.shape
    cols = w.shape[2]
    N = N_CHIPS * cols if wkind == "col" else cols
    has_norm = norm_g is not None
    steps = M // tm
    assert M % tm == 0 and K == (w.shape[1] if wkind == "col" else N_CHIPS * w.shape[1])
    assert loss is None or (wkind == "row" and act is None)

    def body(*refs):
        it = iter(refs)
        a_ref, w_ref = next(it), next(it)
        g_ref = next(it) if has_norm else None
        r_ref = next(it) if res is not None else None
        if loss is not None:
            t_ref, fg_ref = next(it), next(it)
        o_ref = next(it)
        hn_ref = next(it) if has_norm else None
        if loss is not None:
            dg_ref, sq_ref, loss_ref, o16_ref = next(it), next(it), next(it), next(it)
        if has_norm:
            x = a_ref[...].astype(F32)
            r = lax.rsqrt(jnp.mean(x * x, axis=-1, keepdims=True) + EPS)
            lhs = (x * r * g_ref[...]).astype(BF16)
            hn_ref[...] = lhs
        elif square:
            lhs = a_ref[...].astype(BF16)
            lhs = lhs * lhs
        else:
            lhs = a_ref[...].astype(BF16)

        def loss_head(x):
            i = pl.program_id(0)
            r = lax.rsqrt(jnp.mean(x * x, axis=-1, keepdims=True) + EPS)
            xh = x * r
            gg = fg_ref[...]
            diff = xh * gg - t_ref[...]
            sq = jnp.sum(diff * diff, axis=0, keepdims=True)
            dy = diff * (1.0 / N)
            dg_part = jnp.sum(dy * xh, axis=0, keepdims=True)

            @pl.when(i == 0)
            def _():
                sq_ref[...] = sq
                dg_ref[...] = dg_part

            @pl.when(i > 0)
            def _():
                sq_ref[...] += sq
                dg_ref[...] += dg_part
            t = dy * gg
            d = r * (t - xh * jnp.mean(t * xh, axis=-1, keepdims=True))
            o_ref[...] = d
            o16_ref[...] = d.astype(BF16)

            @pl.when(i == steps - 1)
            def _():
                loss_ref[...] = (0.5 / N) * jnp.sum(sq_ref[...], axis=1, keepdims=True)

        def finish(acc, sl):
            if act == "relu":
                acc = jnp.maximum(acc, 0.0)
            if r_ref is not None:
                acc = acc + r_ref[:, sl]
            if loss is not None:
                loss_head(acc)
            else:
                o_ref[:, sl] = acc.astype(out_dtype)

        if wkind == "col":
            for s in range(N_CHIPS):
                finish(_dot(lhs, w_ref[s]), slice(s * cols, (s + 1) * cols))
        else:
            finish(_dot(lhs, w_ref[...].reshape(K, N)), slice(None))

    in_specs = [pl.BlockSpec((tm, K), lambda i: (i, 0)), pl.BlockSpec(w.shape, lambda i: (0, 0, 0))]
    args = [a, w]
    if has_norm:
        in_specs.append(pl.BlockSpec((1, K), lambda i: (0, 0)))
        args.append(norm_g.reshape(1, K))
    if res is not None:
        in_specs.append(pl.BlockSpec((tm, N), lambda i: (i, 0)))
        args.append(res)
    if loss is not None:
        in_specs += [pl.BlockSpec((tm, N), lambda i: (i, 0)), pl.BlockSpec((1, N), lambda i: (0, 0))]
        args += [loss[0], loss[1].reshape(1, N)]
    out_shape = [jax.ShapeDtypeStruct((M, N), out_dtype)]
    out_specs = [pl.BlockSpec((tm, N), lambda i: (i, 0))]
    if has_norm:
        out_shape.append(jax.ShapeDtypeStruct((M, K), BF16))
        out_specs.append(pl.BlockSpec((tm, K), lambda i: (i, 0)))
    if loss is not None:
        out_shape += [jax.ShapeDtypeStruct((1, N), F32), jax.ShapeDtypeStruct((1, N), F32), jax.ShapeDtypeStruct((1, 1), F32),
                      jax.ShapeDtypeStruct((M, N), BF16)]
        out_specs += [pl.BlockSpec((1, N), lambda i: (0, 0)), pl.BlockSpec((1, N), lambda i: (0, 0)),
                      pl.BlockSpec((1, 1), lambda i: (0, 0)), pl.BlockSpec((tm, N), lambda i: (i, 0))]
    outs, carried = _carry_call(body, carry, name=name, grid=(steps,), in_specs=in_specs, out_specs=out_specs,
                                out_shape=out_shape, args=args)
    result = outs if has_norm or loss is not None else outs[0]
    return result if carry is None else (result, carried)


def _mm_nt(a, w, wkind, *, tm, name, epi, a2=None, h=None, g=None, dres=None, bf16_copy=False, carry=None):
    parts = tuple(a) if isinstance(a, (tuple, list)) else (a,)
    M, part_w = parts[0].shape
    Nw = part_w * len(parts)
    rows, cols = w.shape[1], w.shape[2]
    Kw = rows if wkind == "col" else N_CHIPS * rows
    assert M % tm == 0 and Nw == (N_CHIPS * cols if wkind == "col" else cols)
    assert epi != "normbwd" or wkind == "col"
    assert len(parts) == 1 or wkind == "col"
    chunk = math.gcd(part_w, cols)

    def body(*refs):
        it = iter(refs)
        a_refs = [next(it) for _ in parts]
        a_ref, w_ref = a_refs[0], next(it)
        a2_ref = next(it) if epi == "relu2bwd" else None
        if epi == "normbwd":
            h_ref, g_ref, dres_ref = next(it), next(it), next(it)
        o_ref = next(it)
        dg_ref = next(it) if epi == "normbwd" else None
        o16_ref = next(it) if bf16_copy else None
        i = pl.program_id(0)

        def finish(acc, sl):
            if epi == "bf16":
                o_ref[:, sl] = acc.astype(BF16)
            elif epi == "relu2bwd":
                o_ref[:, sl] = (acc * (2.0 * a2_ref[:, sl].astype(F32))).astype(BF16)
            else:
                x = h_ref[...]
                r = lax.rsqrt(jnp.mean(x * x, axis=-1, keepdims=True) + EPS)
                xh = x * r
                dg_part = jnp.sum(acc * xh, axis=0, keepdims=True)

                @pl.when(i == 0)
                def _():
                    dg_ref[...] = dg_part

                @pl.when(i > 0)
                def _():
                    dg_ref[...] += dg_part
                t = acc * g_ref[...]
                d = dres_ref[...] + r * (t - xh * jnp.mean(t * xh, axis=-1, keepdims=True))
                o_ref[...] = d
                if bf16_copy:
                    o16_ref[...] = d.astype(BF16)

        if wkind == "col":
            acc = None
            for lo in range(0, Nw, chunk):
                (src, a_lo), (s, w_lo) = divmod(lo, part_w), divmod(lo, cols)
                part = _dot_nt(a_refs[src][:, a_lo:a_lo + chunk].astype(BF16), w_ref[s, :, w_lo:w_lo + chunk])
                acc = part if acc is None else acc + part
            finish(acc, slice(None))
        else:
            lhs = a_ref[...].astype(BF16)
            for s in range(N_CHIPS):
                finish(_dot_nt(lhs, w_ref[s]), slice(s * rows, (s + 1) * rows))

    in_specs = [pl.BlockSpec((tm, part_w), lambda i: (i, 0)) for _ in parts] + [pl.BlockSpec(w.shape, lambda i: (0, 0, 0))]
    args = [*parts, w]
    out_dtype = BF16
    if epi == "relu2bwd":
        in_specs.append(pl.BlockSpec((tm, Kw), lambda i: (i, 0)))
        args.append(a2)
    if epi == "normbwd":
        in_specs += [pl.BlockSpec((tm, Kw), lambda i: (i, 0)), pl.BlockSpec((1, Kw), lambda i: (0, 0)),
                     pl.BlockSpec((tm, Kw), lambda i: (i, 0))]
        args += [h, g.reshape(1, Kw), dres]
        out_dtype = F32
    out_shape = [jax.ShapeDtypeStruct((M, Kw), out_dtype)]
    out_specs = [pl.BlockSpec((tm, Kw), lambda i: (i, 0))]
    if epi == "normbwd":
        out_shape.append(jax.ShapeDtypeStruct((1, Kw), F32))
        out_specs.append(pl.BlockSpec((1, Kw), lambda i: (0, 0)))
    assert not bf16_copy or epi == "normbwd"
    if bf16_copy:
        out_shape.append(jax.ShapeDtypeStruct((M, Kw), BF16))
        out_specs.append(pl.BlockSpec((tm, Kw), lambda i: (i, 0)))
    outs, carried = _carry_call(body, carry, name=name, grid=(M // tm,), in_specs=in_specs, out_specs=out_specs,
                                out_shape=out_shape, args=args)
    result = outs if epi == "normbwd" else outs[0]
    return result if carry is None else (result, carried)


def _mm_tn(a, b, okind, *, tt, tk, tn, name, square=False, carry=None):
    parts = tuple(b) if isinstance(b, (tuple, list)) else (b,)
    T, K = a.shape
    part_w = parts[0].shape[1]
    N = part_w * len(parts)
    assert T % tt == 0 and K % tk == 0 and part_w % tn == 0
    nt = T // tt
    per_part = part_w // tn
    if okind == "col":
        per = (N // N_CHIPS) // tn
        assert (N // N_CHIPS) % tn == 0
        out_shape = jax.ShapeDtypeStruct((N_CHIPS, K, N // N_CHIPS), F32)
        out_spec = pl.BlockSpec((None, tk, tn), lambda ki, nj, t: (nj // per, ki, nj % per))
    else:
        per = (K // N_CHIPS) // tk
        assert (K // N_CHIPS) % tk == 0
        out_shape = jax.ShapeDtypeStruct((N_CHIPS, K // N_CHIPS, N), F32)
        out_spec = pl.BlockSpec((None, tk, tn), lambda ki, nj, t: (ki // per, ki % per, nj))

    def body(a_ref, *refs):
        b_refs, o_ref = refs[:-1], refs[-1]
        nj, t = pl.program_id(1), pl.program_id(2)

        def accumulate(b_ref):
            lhs = a_ref[...].astype(BF16)
            part = _dot_tn(lhs * lhs if square else lhs, b_ref[...].astype(BF16))
            if nt == 1:
                o_ref[...] = part
            else:
                @pl.when(t == 0)
                def _():
                    o_ref[...] = part

                @pl.when(t > 0)
                def _():
                    o_ref[...] += part

        if len(parts) == 1:
            accumulate(b_refs[0])
        else:
            for s, b_ref in enumerate(b_refs):
                pl.when(nj // per_part == s)(lambda b_ref=b_ref: accumulate(b_ref))

    def b_spec(s):
        return pl.BlockSpec((tt, tn), lambda ki, nj, t: (t, jnp.clip(nj - s * per_part, 0, per_part - 1)))

    outs, carried = _carry_call(
        body, carry, name=name, grid=(K // tk, N // tn, nt),
        in_specs=[pl.BlockSpec((tt, tk), lambda ki, nj, t: (t, ki))] + [b_spec(s) for s in range(len(parts))],
        out_specs=[out_spec], out_shape=[out_shape], args=[a, *parts])
    return outs[0] if carry is None else (outs[0], carried)


def _ret_consts(S, LB):
    log_gamma = jnp.log1p(-jnp.exp2(-5.0 - jnp.arange(RET_HEADS, dtype=F32)))
    idx = jnp.arange(LB, dtype=F32)
    n, m = idx[:, None], idx[None, :]
    cn, cm = jnp.floor(n / CHUNK), jnp.floor(m / CHUNK)
    dist = jnp.where(cm == cn, jnp.abs(n - m), n - m)
    dmat = jnp.where((cm <= cn)[None], jnp.exp(log_gamma[:, None, None] * dist[None]), 0.0)
    qd = jnp.exp(log_gamma[:, None] * (idx + 1.0)[None, :])[..., None]
    kd = jnp.exp(log_gamma[:, None] * (LB - 1 - idx)[None, :])[..., None]
    bd = jnp.exp(log_gamma * LB).reshape(RET_HEADS, 1, 1) * jnp.ones((RET_HEADS, 1, 128), F32)
    half = RET_DK // 2
    inv = jnp.exp(-jnp.log(ROPE_BASE) * jnp.arange(half, dtype=F32) / half)
    ang = jnp.arange(S, dtype=F32)[:, None] * inv[None, :]
    return dmat.astype(F32), qd.astype(F32), kd.astype(F32), bd, jnp.cos(ang), jnp.sin(ang)


def _rope(t, c, s):
    t1, t2 = t[:, :128], t[:, 128:]
    return jnp.concatenate([t1 * c - t2 * s, t1 * s + t2 * c], axis=-1)


def _rope_inv(d, c, s):
    d1, d2 = d[:, :128], d[:, 128:]
    return jnp.concatenate([d1 * c + d2 * s, d2 * c - d1 * s], axis=-1)


def _ret_block_fwd(p_ref, h, c, s, d_ref, qd_ref, kd_ref, stb):
    q = _rope(p_ref[:, h * RET_DK:(h + 1) * RET_DK].astype(F32), c, s)
    k = _rope(p_ref[:, 1024 + h * RET_DK:1024 + (h + 1) * RET_DK].astype(F32), c, s) * (RET_DK ** -0.5)
    v = p_ref[:, 2048 + h * RET_DV:2048 + (h + 1) * RET_DV]
    qb, kb = q.astype(BF16), k.astype(BF16)
    scb = (_dot_nt(qb, kb) * d_ref[h]).astype(BF16)
    o = _dot(scb, v) + qd_ref[h] * _dot(qb, stb)
    return q, k, qb, kb, v, scb, o


def _ret_fwd(proj, gn_g, consts, *, B, S):
    LB = RET_BLOCK
    nb = S // LB
    T = B * S
    dmat, qd, kd, bd, cos, sin = consts

    def body(p_ref, cos_ref, sin_ref, d_ref, qd_ref, kd_ref, bd_ref, gng_ref, y_ref, st_ref, state_s):
        i = pl.program_id(1)

        @pl.when(i == 0)
        def _():
            state_s[...] = jnp.zeros_like(state_s)
        c, s = cos_ref[...], sin_ref[...]
        for h in range(RET_HEADS):
            st = state_s[h]
            stb = st.astype(BF16)
            st_ref[h] = stb
            q, k, qb, kb, v, scb, o = _ret_block_fwd(p_ref, h, c, s, d_ref, qd_ref, kd_ref, stb)
            kdk = (k * kd_ref[h]).astype(BF16)
            state_s[h] = st * bd_ref[h][:, :1] + _dot_tn(kdk, v)
            gate = p_ref[:, 4096 + h * RET_DV:4096 + (h + 1) * RET_DV].astype(F32)
            mu = jnp.mean(o, axis=-1, keepdims=True)
            oc = o - mu
            xh = oc * lax.rsqrt(jnp.mean(oc * oc, axis=-1, keepdims=True) + EPS)
            y = (gate * _sigmoid(gate)) * (xh * gng_ref[:, h * RET_DV:(h + 1) * RET_DV])
            y_ref[:, h * RET_DV:(h + 1) * RET_DV] = y.astype(BF16)

    const = lambda b, i: (0, 0, 0)
    return pl.pallas_call(
        body, name="ret_fwd", grid=(B, nb),
        in_specs=[pl.BlockSpec((LB, 6144), lambda b, i: (b * nb + i, 0)),
                  pl.BlockSpec((LB, 128), lambda b, i: (i, 0)), pl.BlockSpec((LB, 128), lambda b, i: (i, 0)),
                  pl.BlockSpec((RET_HEADS, LB, LB), const), pl.BlockSpec((RET_HEADS, LB, 1), const),
                  pl.BlockSpec((RET_HEADS, LB, 1), const), pl.BlockSpec((RET_HEADS, 1, 128), const),
                  pl.BlockSpec((1, 2048), lambda b, i: (0, 0))],
        out_specs=[pl.BlockSpec((LB, 2048), lambda b, i: (b * nb + i, 0)),
                   pl.BlockSpec((None, None, RET_HEADS, RET_DK, RET_DV), lambda b, i: (b, i, 0, 0, 0))],
        out_shape=[jax.ShapeDtypeStruct((T, 2048), BF16), jax.ShapeDtypeStruct((B, nb, RET_HEADS, RET_DK, RET_DV), BF16)],
        scratch_shapes=[pltpu.VMEM((RET_HEADS, RET_DK, RET_DV), F32)], compiler_params=_params(2),
    )(proj, cos, sin, dmat, qd, kd, bd, gn_g.reshape(1, 2048))


def _ret_bwd(proj, dy, states, gn_g, consts, *, B, S):
    LB = RET_BLOCK
    nb = S // LB
    T = B * S
    dmat, qd, kd, bd, cos, sin = consts

    def body(p_ref, dy_ref, st_ref, cos_ref, sin_ref, d_ref, qd_ref, kd_ref, bd_ref, gng_ref, dp_ref, dgn_ref, dstate_s):
        b, i = pl.program_id(0), pl.program_id(1)

        @pl.when(i == 0)
        def _():
            dstate_s[...] = jnp.zeros_like(dstate_s)

        @pl.when((b == 0) & (i == 0))
        def _():
            dgn_ref[...] = jnp.zeros_like(dgn_ref)
        c, s = cos_ref[...], sin_ref[...]
        for h in range(RET_HEADS):
            vs = slice(h * RET_DV, (h + 1) * RET_DV)
            stb = st_ref[h]
            q, k, qb, kb, v, scb, o = _ret_block_fwd(p_ref, h, c, s, d_ref, qd_ref, kd_ref, stb)
            gate = p_ref[:, 4096 + h * RET_DV:4096 + (h + 1) * RET_DV].astype(F32)
            mu = jnp.mean(o, axis=-1, keepdims=True)
            oc = o - mu
            rstd = lax.rsqrt(jnp.mean(oc * oc, axis=-1, keepdims=True) + EPS)
            xh = oc * rstd
            gng = gng_ref[:, vs]
            dyh = dy_ref[:, vs].astype(F32)
            sg = _sigmoid(gate)
            silu = gate * sg
            dgn_ref[:, vs] += jnp.sum(dyh * silu * xh, axis=0, keepdims=True)
            dxh = dyh * silu * gng
            do = rstd * (dxh - jnp.mean(dxh, axis=-1, keepdims=True) - xh * jnp.mean(dxh * xh, axis=-1, keepdims=True))
            dgate = dyh * xh * gng * (sg * (1.0 + gate * (1.0 - sg)))
            dob = do.astype(BF16)
            dsb = (_dot_nt(dob, v) * d_ref[h]).astype(BF16)
            dst = dstate_s[h]
            dstb = dst.astype(BF16)
            kdk = (k * kd_ref[h]).astype(BF16)
            dqr = _dot(dsb, kb) + qd_ref[h] * _dot_nt(dob, stb)
            dkr = _dot_tn(dsb, qb) + kd_ref[h] * _dot_nt(v, dstb)
            dv = _dot_tn(scb, dob) + _dot(kdk, dstb)
            dstate_s[h] = dst * bd_ref[h][:, :1] + _dot_tn((q * qd_ref[h]).astype(BF16), dob)
            dp_ref[:, h * RET_DK:(h + 1) * RET_DK] = _rope_inv(dqr, c, s).astype(BF16)
            dp_ref[:, 1024 + h * RET_DK:1024 + (h + 1) * RET_DK] = (_rope_inv(dkr, c, s) * (RET_DK ** -0.5)).astype(BF16)
            dp_ref[:, 2048 + h * RET_DV:2048 + (h + 1) * RET_DV] = dv.astype(BF16)
            dp_ref[:, 4096 + h * RET_DV:4096 + (h + 1) * RET_DV] = dgate.astype(BF16)

    const = lambda b, i: (0, 0, 0)
    rev = lambda b, i: (b * nb + nb - 1 - i, 0)
    return pl.pallas_call(
        body, name="ret_bwd", grid=(B, nb),
        in_specs=[pl.BlockSpec((LB, 6144), rev), pl.BlockSpec((LB, 2048), rev),
                  pl.BlockSpec((None, None, RET_HEADS, RET_DK, RET_DV), lambda b, i: (b, nb - 1 - i, 0, 0, 0)),
                  pl.BlockSpec((LB, 128), lambda b, i: (nb - 1 - i, 0)), pl.BlockSpec((LB, 128), lambda b, i: (nb - 1 - i, 0)),
                  pl.BlockSpec((RET_HEADS, LB, LB), const), pl.BlockSpec((RET_HEADS, LB, 1), const),
                  pl.BlockSpec((RET_HEADS, LB, 1), const), pl.BlockSpec((RET_HEADS, 1, 128), const),
                  pl.BlockSpec((1, 2048), lambda b, i: (0, 0))],
        out_specs=[pl.BlockSpec((LB, 6144), rev), pl.BlockSpec((1, 2048), lambda b, i: (0, 0))],
        out_shape=[jax.ShapeDtypeStruct((T, 6144), BF16), jax.ShapeDtypeStruct((1, 2048), F32)],
        scratch_shapes=[pltpu.VMEM((RET_HEADS, RET_DK, RET_DV), F32)], compiler_params=_params(2),
    )(proj, dy, states, cos, sin, dmat, qd, kd, bd, gn_g.reshape(1, 2048))


BIAS_LANES = 4 * ATT_BLOCK


def _diag_onehot():
    r = lax.broadcasted_iota(jnp.int32, (REL_TABLE, BIAS_LANES), 0)
    j = lax.broadcasted_iota(jnp.int32, (REL_TABLE, BIAS_LANES), 1)
    idx = jnp.maximum(j - ATT_BLOCK - PAST, -MAX_REL) + MAX_REL
    return jnp.where(idx == r, 1.0, 0.0).astype(F32)


def _row_is(j):
    return lax.broadcasted_iota(jnp.int32, (8, BIAS_LANES), 0) == j


def _att_bias(table):
    QB, KW = ATT_BLOCK, 3 * ATT_BLOCK

    def body(t_ref, bt_ref):
        row = jnp.broadcast_to(t_ref[...], (8, REL_TABLE))
        diag = jnp.dot(row, _diag_onehot(), preferred_element_type=F32, precision=lax.Precision.HIGHEST)
        rows = jnp.zeros((8, BIAS_LANES), F32)
        for j in range(8):
            rows = jnp.where(_row_is(j), diag if j == 0 else pltpu.roll(diag, j, axis=1), rows)
        n = 8
        while n < QB:
            rows = jnp.concatenate([rows, pltpu.roll(rows, n, axis=1)], axis=0)
            n *= 2
        bias = rows[:, QB:]
        qi = lax.broadcasted_iota(jnp.int32, (QB, KW), 0)
        kj = lax.broadcasted_iota(jnp.int32, (QB, KW), 1)
        lo = (qi // CHUNK) * CHUNK
        bt_ref[...] = jnp.where((kj >= lo) & (kj < lo + PAST + CHUNK), bias, NEG).T

    return pl.pallas_call(
        body, name="att_bias", grid=(ATT_HEADS,),
        in_specs=[pl.BlockSpec((None, 1, REL_TABLE), lambda h: (h, 0, 0))],
        out_specs=pl.BlockSpec((None, KW, QB), lambda h: (h // 2, 0, h % 2)),
        out_shape=jax.ShapeDtypeStruct((ATT_HEADS // 2, KW, 2 * QB), F32),
        compiler_params=_params(1),
    )(table.reshape(ATT_HEADS, 1, REL_TABLE))


def _att_bias_grad(dbias_t):
    QB, KW = ATT_BLOCK, 3 * ATT_BLOCK

    def body(d_ref, o_ref):
        rows = jnp.concatenate([jnp.zeros((QB, QB), F32), d_ref[...].T], axis=1)
        n = QB // 2
        while n >= 8:
            rows = rows[:n] + pltpu.roll(rows[n:], BIAS_LANES - n, axis=1)
            n //= 2
        acc = jnp.zeros((8, BIAS_LANES), F32)
        for j in range(8):
            acc = acc + jnp.where(_row_is(j), rows if j == 0 else pltpu.roll(rows, BIAS_LANES - j, axis=1), 0.0)
        diag = jnp.broadcast_to(jnp.sum(acc, axis=0, keepdims=True), (8, BIAS_LANES))
        grad = lax.dot_general(diag, _diag_onehot(), (((1,), (1,)), ((), ())), preferred_element_type=F32,
                               precision=lax.Precision.HIGHEST)
        o_ref[...] = grad[:1]

    return pl.pallas_call(
        body, name="att_bias_grad", grid=(ATT_HEADS,),
        in_specs=[pl.BlockSpec((None, KW, QB), lambda h: (h // 2, 0, h % 2))],
        out_specs=pl.BlockSpec((None, 1, REL_TABLE), lambda h: (h, 0, 0)),
        out_shape=jax.ShapeDtypeStruct((ATT_HEADS, 1, REL_TABLE), F32), compiler_params=_params(1),
    )(dbias_t).reshape(ATT_HEADS, REL_TABLE)


def _by_head(x):
    first = lax.broadcasted_iota(jnp.int32, x.shape, 1) < ATT_DH
    zero = jnp.zeros_like(x)
    return jnp.concatenate([jnp.where(first, x, zero), jnp.where(first, zero, x)], axis=0)


def _att_fwd(qkv, bias_t, *, B, S):
    QB = ATT_BLOCK
    nb = S // QB
    KW = 3 * QB
    T = B * S
    scale = ATT_DH ** -0.5

    def body(q_ref, k0, k1, k2, v0, v1, v2, b_ref, o_ref, lse_ref, s_blk):
        i = pl.program_id(2)
        q2 = _by_head((q_ref[...].astype(F32) * scale).astype(BF16))
        m = jnp.full((1, 2 * QB), NEG, F32)
        for d, k_ref in enumerate((k0, k1, k2)):
            st = _dot_nt(k_ref[...], q2) + b_ref[d * QB:(d + 1) * QB, :]
            st = jnp.where(i + d >= 2, st, NEG)
            s_blk[d] = st
            m = jnp.maximum(m, jnp.max(st, axis=0, keepdims=True))
        l = jnp.zeros((1, 2 * QB), F32)
        o_t = jnp.zeros((128, 2 * QB), F32)
        for d, v_ref in enumerate((v0, v1, v2)):
            e = jnp.exp(s_blk[d] - m)
            l = l + jnp.sum(e, axis=0, keepdims=True)
            o_t = o_t + _dot(v_ref[...].astype(F32).T.astype(BF16), e.astype(BF16))
        o_t = o_t / l
        row = lax.broadcasted_iota(jnp.int32, (128, QB), 0)
        o_ref[...] = jnp.where(row < ATT_DH, o_t[:, :QB], o_t[:, QB:]).T.astype(BF16)
        lse = m + jnp.log(l)
        row8 = lax.broadcasted_iota(jnp.int32, (8, QB), 0)
        lse_ref[...] = jnp.where(row8 == 0, lse[:, :QB], jnp.where(row8 == 1, lse[:, QB:], 0.0))

    def kv(d, col0):
        return pl.BlockSpec((QB, 128), lambda hp, b, i: (b * nb + jnp.maximum(i - d, 0), col0 + hp))

    return pl.pallas_call(
        body, name="att_fwd", grid=(8, B, nb),
        in_specs=[pl.BlockSpec((QB, 128), lambda hp, b, i: (b * nb + i, hp)),
                  kv(2, 8), kv(1, 8), kv(0, 8), kv(2, 16), kv(1, 16), kv(0, 16),
                  pl.BlockSpec((None, KW, 2 * QB), lambda hp, b, i: (hp, 0, 0))],
        out_specs=[pl.BlockSpec((QB, 128), lambda hp, b, i: (b * nb + i, hp)),
                   pl.BlockSpec((None, 8, QB), lambda hp, b, i: (hp, 0, b * nb + i))],
        out_shape=[jax.ShapeDtypeStruct((T, 1024), BF16), jax.ShapeDtypeStruct((8, 8, T), F32)],
        scratch_shapes=[pltpu.VMEM((3, QB, 2 * QB), F32)], compiler_params=_params(3),
    )(qkv, qkv, qkv, qkv, qkv, qkv, qkv, bias_t)


def _att_bwd(qkv, do, o, lse, bias_t, *, B, S):
    QB = ATT_BLOCK
    nb = S // QB
    KW = 3 * QB
    T = B * S
    scale = ATT_DH ** -0.5
    TK = 256

    def body(q_ref, k0, k1, k2, v0, v1, v2, do_ref, o_ref, lse_ref, b_ref, dq_ref, dk_ref, dv_ref, db_ref, dk_acc, dv_acc):
        b, i = pl.program_id(1), pl.program_id(2)

        @pl.when(i == 0)
        def _():
            dk_acc[...] = jnp.zeros_like(dk_acc)
            dv_acc[...] = jnp.zeros_like(dv_acc)

        @pl.when((b == 0) & (i == 0))
        def _():
            db_ref[...] = jnp.zeros_like(db_ref)

        def write_out():
            slot = (i + 1) % 3
            dk_ref[...] = dk_acc[slot].astype(BF16)
            dv_ref[...] = dv_acc[slot].astype(BF16)
            dk_acc[slot] = jnp.zeros((QB, 128), F32)
            dv_acc[slot] = jnp.zeros((QB, 128), F32)

        @pl.when(i < nb)
        def _():
            dout = do_ref[...]
            q2 = _by_head((q_ref[...].astype(F32) * scale).astype(BF16))
            do2 = _by_head(dout)
            delta_t = (o_ref[...].astype(F32) * dout.astype(F32)).T
            delta2 = jnp.concatenate([jnp.sum(delta_t[:ATT_DH], axis=0, keepdims=True),
                                      jnp.sum(delta_t[ATT_DH:], axis=0, keepdims=True)], axis=1)
            lse2 = jnp.concatenate([lse_ref[0:1, :], lse_ref[1:2, :]], axis=1)
            dq_t = jnp.zeros((128, 2 * QB), F32)
            for d, (k_ref, v_ref) in enumerate(((k0, v0), (k1, v1), (k2, v2))):
                kblk, vblk = k_ref[...], v_ref[...]
                kt = kblk.astype(F32).T.astype(BF16)
                lse_d = jnp.where(i + d >= 2, lse2, -NEG)
                slot = (i + 1 + d) % 3
                for t in range(QB // TK):
                    rows = slice(t * TK, (t + 1) * TK)
                    wrows = slice(d * QB + t * TK, d * QB + (t + 1) * TK)
                    p = jnp.exp(_dot_nt(kblk[rows], q2) + b_ref[wrows, :] - lse_d)
                    ds = p * (_dot_nt(vblk[rows], do2) - delta2)
                    db_ref[wrows, :] += ds
                    dsb = ds.astype(BF16)
                    dk_acc[slot, rows, :] += _dot(dsb, q2)
                    dv_acc[slot, rows, :] += _dot(p.astype(BF16), do2)
                    dq_t += _dot(kt[:, rows], dsb)
            row = lax.broadcasted_iota(jnp.int32, (128, QB), 0)
            dq_ref[...] = (jnp.where(row < ATT_DH, dq_t[:, :QB], dq_t[:, QB:]) * scale).T.astype(BF16)
            write_out()

        pl.when(i >= nb)(write_out)

    def qrow(b, i):
        return b * nb + jnp.minimum(i, nb - 1)

    def kv(d, col0):
        return pl.BlockSpec((QB, 128), lambda hp, b, i: (b * nb + jnp.maximum(jnp.minimum(i, nb - 1) - d, 0), col0 + hp))

    late = pl.BlockSpec((QB, 128), lambda hp, b, i: (b * nb + jnp.maximum(i - 2, 0), hp))
    return pl.pallas_call(
        body, name="att_bwd", grid=(8, B, nb + 2),
        in_specs=[pl.BlockSpec((QB, 128), lambda hp, b, i: (qrow(b, i), hp)),
                  kv(2, 8), kv(1, 8), kv(0, 8), kv(2, 16), kv(1, 16), kv(0, 16),
                  pl.BlockSpec((QB, 128), lambda hp, b, i: (qrow(b, i), hp)),
                  pl.BlockSpec((QB, 128), lambda hp, b, i: (qrow(b, i), hp)),
                  pl.BlockSpec((None, 8, QB), lambda hp, b, i: (hp, 0, qrow(b, i))),
                  pl.BlockSpec((None, KW, 2 * QB), lambda hp, b, i: (hp, 0, 0))],
        out_specs=[pl.BlockSpec((QB, 128), lambda hp, b, i: (qrow(b, i), hp)), late, late,
                   pl.BlockSpec((None, KW, 2 * QB), lambda hp, b, i: (hp, 0, 0))],
        out_shape=[jax.ShapeDtypeStruct((T, 1024), BF16)] * 3 + [jax.ShapeDtypeStruct((ATT_HEADS // 2, KW, 2 * QB), F32)],
        scratch_shapes=[pltpu.VMEM((3, QB, 128), F32), pltpu.VMEM((3, QB, 128), F32)], compiler_params=_params(3),
    )(qkv, qkv, qkv, qkv, qkv, qkv, qkv, do, o, lse, bias_t)


def _tok_tile(T, want):
    t = min(T, want)
    assert T % t == 0
    return t


def _step(x, tgt, slab, slab_rel, place, mix_g, gn_g, mlp_g, fin_g):
    B, S, D = x.shape
    T = B * S
    h0 = x.reshape(T, D)
    tgt = tgt.reshape(T, D)
    tm = _tok_tile(T, 1024)
    tb = _tok_tile(T, 512)
    tq = _tok_tile(T, 256)
    tt = _tok_tile(T, 8192)
    tf = _tok_tile(T, 2048)
    consts = _ret_consts(S, RET_BLOCK)
    w = {}

    (w["ret_w_in"],) = _all_gather_slabs([slab["ret_w_in"]])
    (proj, hn0), (w["ret_w_out"], w["mlp_w1_0"]) = _mm_nn(
        h0, w["ret_w_in"], "col", tm=tb, out_dtype=BF16, name="ret_in", norm_g=mix_g[0],
        carry=_gather_carry([slab["ret_w_out"], slab["mlp_w1_0"]]))
    y_ret, states = _ret_fwd(proj, gn_g, consts, B=B, S=S)
    h1, (w["mlp_w2_0"],) = _mm_nn(y_ret, w["ret_w_out"], "row", tm=tm, out_dtype=F32, name="ret_out", res=h0,
                                  carry=_gather_carry([slab["mlp_w2_0"]]))
    (a0, hm0), (w["att_w_in"], w["att_w_out"], rel_slabs) = _mm_nn(
        h1, w["mlp_w1_0"], "col", tm=tb, out_dtype=BF16, name="mlp0_up", norm_g=mlp_g[0], act="relu",
        carry=_gather_carry([slab["att_w_in"], slab["att_w_out"], slab_rel]))
    h2, (w["mlp_w1_1"],) = _mm_nn(a0, w["mlp_w2_0"], "row", tm=tb, out_dtype=F32, name="mlp0_down", square=True, res=h1,
                                  carry=_gather_carry([slab["mlp_w1_1"]]))
    rel_bias = jnp.transpose(rel_slabs, (1, 0, 2)).reshape(ATT_HEADS, REL_TABLE)
    bias_t = _att_bias(rel_bias)
    (qkv, hn1), (w["mlp_w2_1"],) = _mm_nn(h2, w["att_w_in"], "col", tm=tb, out_dtype=BF16, name="att_in", norm_g=mix_g[1],
                                          carry=_gather_carry([slab["mlp_w2_1"]]))
    o_att, lse = _att_fwd(qkv, bias_t, B=B, S=S)
    h3 = _mm_nn(o_att, w["att_w_out"], "row", tm=tm, out_dtype=F32, name="att_out", res=h2)
    a1, hm1 = _mm_nn(h3, w["mlp_w1_1"], "col", tm=tb, out_dtype=BF16, name="mlp1_up", norm_g=mlp_g[1], act="relu")
    dh4, d_fin_g, _, loss, dh4_b = _mm_nn(a1, w["mlp_w2_1"], "row", tm=tb, out_dtype=F32, name="mlp1_down", square=True, res=h3,
                                          loss=(tgt, fin_g))

    gw = {}
    gw["mlp_w2_1"] = _mm_tn(a1, dh4_b, "row", tt=tf, tk=1024, tn=D, name="d_mlp1_w2", square=True)
    dz1 = _mm_nt(dh4_b, w["mlp_w2_1"], "row", tm=tb, name="d_mlp1_act", epi="relu2bwd", a2=a1)
    gw["mlp_w1_1"] = _mm_tn(hm1, dz1, "col", tt=tt, tk=D, tn=256, name="d_mlp1_w1")
    dh3, d_mlp_g1, dh3_b = _mm_nt(dz1, w["mlp_w1_1"], "col", tm=tb, name="d_mlp1_in", epi="normbwd",
                                  h=h3, g=mlp_g[1], dres=dh4, bf16_copy=True)
    gw["att_w_out"] = _mm_tn(o_att, dh3_b, "row", tt=tt, tk=256, tn=256, name="d_att_wout")
    do_att = _mm_nt(dh3_b, w["att_w_out"], "row", tm=tb, name="d_att_o", epi="bf16")
    dq, dk, dv, dbias_t = _att_bwd(qkv, do_att, o_att, lse, bias_t, B=B, S=S)
    d_rel = _att_bias_grad(dbias_t)
    dqkv = (dq, dk, dv)
    gw["att_w_in"] = _mm_tn(hn1, dqkv, "col", tt=tt, tk=D, tn=256, name="d_att_win")
    sums, landed = {}, {}

    def swap_halves(names):
        return _sibling_carry([gw[n] for n in names])

    def add_halves(names, carried):
        for t, n in enumerate(names):
            sums[n] = _add_sibling(place, carried[t], carried[len(names) + t], name="chip_sum_" + n)

    def carried_exchange(names):
        return _chips_carry([sums[n][0] for n in names])

    def keep(names, carried):
        landed.update(zip(names, carried[len(names):]))

    layer1 = ["mlp_w1_1", "mlp_w2_1", "att_w_in", "att_w_out"]
    (dh2, d_mix_g1, dh2_b), carried = _mm_nt(dqkv, w["att_w_in"], "col", tm=tb, name="d_att_in", epi="normbwd",
                                             h=h2, g=mix_g[1], dres=dh3, bf16_copy=True, carry=swap_halves(layer1))
    add_halves(layer1, carried)
    gw["mlp_w2_0"] = _mm_tn(a0, dh2_b, "row", tt=tf, tk=1024, tn=D, name="d_mlp0_w2", square=True)
    dz0, carried = _mm_nt(dh2_b, w["mlp_w2_0"], "row", tm=tb, name="d_mlp0_act", epi="relu2bwd", a2=a0,
                          carry=carried_exchange(["mlp_w1_1"]))
    keep(["mlp_w1_1"], carried)
    gw["mlp_w1_0"], carried = _mm_tn(hm0, dz0, "col", tt=tt, tk=D, tn=256, name="d_mlp0_w1", carry=carried_exchange(["mlp_w2_1"]))
    keep(["mlp_w2_1"], carried)
    (dh1, d_mlp_g0, dh1_b), carried = _mm_nt(dz0, w["mlp_w1_0"], "col", tm=tb, name="d_mlp0_in", epi="normbwd",
                                             h=h1, g=mlp_g[0], dres=dh2, bf16_copy=True,
                                             carry=_merge(carried_exchange(["att_w_in", "att_w_out"]),
                                                          swap_halves(["mlp_w1_0", "mlp_w2_0"])))
    keep(["att_w_in", "att_w_out"], carried[:4])
    add_halves(["mlp_w1_0", "mlp_w2_0"], carried[4:])
    gw["ret_w_out"] = _mm_tn(y_ret, dh1_b, "row", tt=tt, tk=512, tn=256, name="d_ret_wout")
    dy_ret, carried = _mm_nt(dh1_b, w["ret_w_out"], "row", tm=tb, name="d_ret_y", epi="bf16", carry=swap_halves(["ret_w_out"]))
    add_halves(["ret_w_out"], carried)
    dproj, d_gn = _ret_bwd(proj, dy_ret, states, gn_g, consts, B=B, S=S)
    gw["ret_w_in"], carried = _mm_tn(hn0, dproj, "col", tt=tt, tk=D, tn=256, name="d_ret_win",
                                     carry=carried_exchange(["mlp_w1_0", "mlp_w2_0", "ret_w_out"]))
    keep(["mlp_w1_0", "mlp_w2_0", "ret_w_out"], carried)
    add_halves(["ret_w_in"], [gw["ret_w_in"]] + list(_exchange_with_sibling([gw["ret_w_in"]], "ret_in")))
    done = [n for n in BIG if n != "ret_w_in"]
    shards = {n: _add_chips(place, sums[n][1], landed[n], name="total_" + n) for n in done}
    (dx, d_mix_g0), carried = _mm_nt(dproj, w["ret_w_in"], "col", tm=tb, name="d_ret_in", epi="normbwd", h=h0, g=mix_g[0], dres=dh1,
                                     carry=_merge(carried_exchange(["ret_w_in"]), _share_carry([shards[n] for n in done])))
    keep(["ret_w_in"], carried[:2])
    shards.update(zip(done, carried[2:]))
    (shards["ret_w_in"],) = _share_with_sibling([_add_chips(place, sums["ret_w_in"][1], landed["ret_w_in"], name="total_ret_w_in")])
    small = [d_mix_g0, d_mix_g1, d_mlp_g0, d_mlp_g1, d_fin_g, d_gn.reshape(2, D), d_rel.reshape(5, D), loss]
    return dx.reshape(B, S, D), shards, small


def _row_tile(r, want=256):
    t = min(r, want)
    assert r % t == 0
    return t


def _into_slab(place, a, layer, dtype, name):
    _, r, c = a.shape
    tr = _row_tile(r)

    def body(place_ref, a_ref, o_ref):
        o_ref[...] = a_ref[...].astype(dtype)

    grid_spec = pltpu.PrefetchScalarGridSpec(
        num_scalar_prefetch=1, grid=(r // tr,), in_specs=[pl.BlockSpec((None, tr, c), lambda i, pr: (layer, i, 0))],
        out_specs=pl.BlockSpec((None, tr, c), lambda i, pr: (pr[0], i, 0)),
    )
    return pl.pallas_call(
        body, name=name, grid_spec=grid_spec, out_shape=jax.ShapeDtypeStruct((N_CHIPS, r, c), dtype), compiler_params=_params(1),
    )(place, a)


def _add_sibling(place, g, recv, name):
    _, r, c = g.shape
    hr = r // 2
    tr = _row_tile(hr)
    nrt = hr // tr

    def body(place_ref, g_ref, r_ref, sb_ref, own_ref):
        v = g_ref[...] + r_ref[...]
        sb_ref[...] = v.astype(BF16)

        @pl.when(pl.program_id(1) == place_ref[0])
        def _():
            own_ref[...] = v

    grid_spec = pltpu.PrefetchScalarGridSpec(
        num_scalar_prefetch=1, grid=(nrt, N_CHIPS),
        in_specs=[pl.BlockSpec((None, tr, c), lambda i, s, pr: (s, pr[1] * nrt + i, 0)),
                  pl.BlockSpec((None, tr, c), lambda i, s, pr: (s, i, 0))],
        out_specs=[pl.BlockSpec((None, tr, c), lambda i, s, pr: (s, i, 0)), pl.BlockSpec((tr, c), lambda i, s, pr: (i, 0))],
    )
    return pl.pallas_call(
        body, name=name, grid_spec=grid_spec,
        out_shape=[jax.ShapeDtypeStruct((N_CHIPS, hr, c), BF16), jax.ShapeDtypeStruct((hr, c), F32)],
        compiler_params=_params(2),
    )(place, g, recv)


def _add_chips(place, own, recv, name):
    hr, c = own.shape
    tr = _row_tile(hr)
    nrt = hr // tr

    def body(place_ref, o_ref, r_ref, t_ref):
        t_ref[...] = ((o_ref[...] + r_ref[0].astype(F32)) + r_ref[1].astype(F32)) + r_ref[2].astype(F32)

    grid_spec = pltpu.PrefetchScalarGridSpec(
        num_scalar_prefetch=1, grid=(nrt,),
        in_specs=[pl.BlockSpec((tr, c), lambda i, pr: (i, 0)), pl.BlockSpec((3, tr, c), lambda i, pr: (0, i, 0))],
        out_specs=pl.BlockSpec((tr, c), lambda i, pr: (pr[1] * nrt + i, 0)),
    )
    return pl.pallas_call(
        body, name=name, grid_spec=grid_spec, out_shape=jax.ShapeDtypeStruct((2 * hr, c), F32), compiler_params=_params(1),
    )(place, own, recv)


def _adamw(w, gs, m, v, name):
    L, r, c = w.shape
    tr = _row_tile(r)
    assert len(gs) == L

    def body(*refs):
        w_ref, m_ref, v_ref = refs[:3]
        g_refs = refs[3:3 + L]
        go_ref, d_ref, nm_ref, nv_ref = refs[3 + L:]
        gg = g_refs[0][...]
        for k in range(1, L):
            gg = jnp.where(pl.program_id(0) == k, g_refs[k][...], gg)
        go_ref[...] = gg
        nm = ADAM_B1 * m_ref[...] + (1.0 - ADAM_B1) * gg
        nv = ADAM_B2 * v_ref[...] + (1.0 - ADAM_B2) * (gg * gg)
        m_hat = nm / (1.0 - ADAM_B1 ** ADAM_STEP)
        v_hat = nv / (1.0 - ADAM_B2 ** ADAM_STEP)
        d_ref[...] = -ADAM_LR * (m_hat / (jnp.sqrt(v_hat) + ADAM_EPS) + ADAM_WD * w_ref[...])
        nm_ref[...] = nm
        nv_ref[...] = nv

    spec = pl.BlockSpec((None, tr, c), lambda l, i: (l, i, 0))
    return pl.pallas_call(
        body, name=name, grid=(L, r // tr), in_specs=[spec] * 3 + [pl.BlockSpec((tr, c), lambda l, i: (i, 0))] * L,
        out_specs=[spec] * 4, out_shape=[jax.ShapeDtypeStruct((L, r, c), F32)] * 4, compiler_params=_params(2),
    )(w, m, v, *gs)


def _place():
    return lax.axis_index("x"), lax.axis_index("y"), lax.axis_index("c")


def _other_chips(x, y):
    return [(1 - x, y), (x, 1 - y), (1 - x, 1 - y)]


def _remote(src, dst, ssem, rsem, dev):
    return pltpu.make_async_remote_copy(src_ref=src, dst_ref=dst, send_sem=ssem, recv_sem=rsem, device_id=dev,
                                        device_id_type=MESH)


def _gather_phases(n):
    def geometry(refs, t):
        x, y, c = _place()
        hr = refs[t].shape[1] // 2
        chips = _other_chips(x, y)
        return x, y, c, 2 * x + y, chips, [2 * qx + qy for qx, qy in chips], pl.ds(c * hr, hr), pl.ds((1 - c) * hr, hr)

    def send(refs, sems):
        s1, r1, _, _ = sems
        for t in range(n):
            x, y, c, p, chips, cidx, mine, theirs = geometry(refs, t)
            for j, (qx, qy) in enumerate(chips):
                _remote(refs[t].at[p, mine], refs[t].at[p, mine], s1.at[t, j], r1.at[t, j], (qx, qy, c)).start()

    def pass_on(refs, sems):
        s1, r1, s2, r2 = sems
        for t in range(n):
            x, y, c, p, chips, cidx, mine, theirs = geometry(refs, t)
            for j, (qx, qy) in enumerate(chips):
                got = refs[t].at[cidx[j], mine]
                _remote(got, got, s1.at[t, j], r1.at[t, j], (qx, qy, c)).wait_recv()
                _remote(got, got, s2.at[t, j], r2.at[t, j], (x, y, 1 - c)).start()

    def finish(refs, sems):
        s1, r1, s2, r2 = sems
        for t in range(n):
            x, y, c, p, chips, cidx, mine, theirs = geometry(refs, t)
            for j, (qx, qy) in enumerate(chips):
                got = refs[t].at[cidx[j], theirs]
                _remote(got, got, s2.at[t, j], r2.at[t, j], (x, y, 1 - c)).wait_recv()
        for t in range(n):
            x, y, c, p, chips, cidx, mine, theirs = geometry(refs, t)
            for j, (qx, qy) in enumerate(chips):
                _remote(refs[t].at[p, mine], refs[t].at[p, mine], s1.at[t, j], r1.at[t, j], (qx, qy, c)).wait_send()
                sent = refs[t].at[cidx[j], mine]
                _remote(sent, sent, s2.at[t, j], r2.at[t, j], (x, y, 1 - c)).wait_send()

    sem = pltpu.SemaphoreType.DMA
    return send, pass_on, finish, [sem((n, 3)), sem((n, 3)), sem((n, 3)), sem((n, 3))]


def _gather_carry(slabs):
    send, pass_on, finish, sems = _gather_phases(len(slabs))
    return _Carry(slabs, sems, [(0, send), (-2, pass_on), (-1, finish)])


def _all_gather_slabs(slabs):
    n = len(slabs)
    send, pass_on, finish, sems = _gather_phases(n)

    def body(*refs):
        outs, scratch = refs[n:2 * n], refs[2 * n:]
        send(outs, scratch)
        pass_on(outs, scratch)
        finish(outs, scratch)

    return pl.pallas_call(
        body, name="gather_weights", in_specs=[HBM_SPEC] * n, out_specs=[HBM_SPEC] * n,
        out_shape=[jax.ShapeDtypeStruct(s.shape, s.dtype) for s in slabs], input_output_aliases={t: t for t in range(n)},
        scratch_shapes=sems, compiler_params=pltpu.CompilerParams(has_side_effects=True),
    )(*slabs)


def _sibling_phases(n):
    def copies(refs, sems):
        ssem, rsem = sems
        x, y, c = _place()
        out = []
        for t in range(n):
            hr = refs[t].shape[1] // 2
            out.append(_remote(refs[t].at[:, pl.ds((1 - c) * hr, hr), :], refs[n + t], ssem.at[t], rsem.at[t], (x, y, 1 - c)))
        return out

    def send(refs, sems):
        for cp in copies(refs, sems):
            cp.start()

    def finish(refs, sems):
        for cp in copies(refs, sems):
            cp.wait()

    sem = pltpu.SemaphoreType.DMA
    return send, finish, [sem((n,)), sem((n,))]


def _halves_landing(grads):
    return [lax.empty((N_CHIPS, g.shape[1] // 2, g.shape[2]), g.dtype) for g in grads]


def _sibling_carry(grads):
    send, finish, sems = _sibling_phases(len(grads))
    return _Carry(list(grads) + _halves_landing(grads), sems, [(0, send), (-1, finish)])


def _exchange_with_sibling(grads, tag):
    n = len(grads)
    send, finish, sems = _sibling_phases(n)

    def body(*refs):
        both, scratch = refs[:2 * n], refs[2 * n:]
        send(both, scratch)
        finish(both, scratch)

    return pl.pallas_call(
        body, name="grads_to_sibling_" + tag, in_specs=[HBM_SPEC] * n, out_specs=[HBM_SPEC] * n,
        out_shape=[jax.ShapeDtypeStruct((N_CHIPS, g.shape[1] // 2, g.shape[2]), g.dtype) for g in grads],
        scratch_shapes=sems, compiler_params=pltpu.CompilerParams(has_side_effects=True),
    )(*grads)


def _chips_phases(n):
    def copies(refs, sems):
        ssem, rsem = sems
        x, y, c = _place()
        return [_remote(refs[t].at[2 * qx + qy], refs[n + t].at[j], ssem.at[t, j], rsem.at[t, j], (qx, qy, c))
                for t in range(n) for j, (qx, qy) in enumerate(_other_chips(x, y))]

    def send(refs, sems):
        for cp in copies(refs, sems):
            cp.start()

    def finish(refs, sems):
        for cp in copies(refs, sems):
            cp.wait()

    sem = pltpu.SemaphoreType.DMA
    return send, finish, [sem((n, 3)), sem((n, 3))]


def _landing(sums):
    return [lax.empty((3,) + s.shape[1:], s.dtype) for s in sums]


def _chips_carry(sums):
    send, finish, sems = _chips_phases(len(sums))
    return _Carry(list(sums) + _landing(sums), sems, [(0, send), (-1, finish)])


def _share_phases(n):
    def halves(refs, t):
        x, y, c = _place()
        hr = refs[t].shape[0] // 2
        return refs[t].at[pl.ds(c * hr, hr)], refs[t].at[pl.ds((1 - c) * hr, hr)], (x, y, 1 - c)

    def send(refs, sems):
        ssem, rsem = sems
        for t in range(n):
            mine, _, sibling = halves(refs, t)
            _remote(mine, mine, ssem.at[t], rsem.at[t], sibling).start()

    def finish(refs, sems):
        ssem, rsem = sems
        for t in range(n):
            mine, theirs, sibling = halves(refs, t)
            _remote(theirs, theirs, ssem.at[t], rsem.at[t], sibling).wait_recv()
            _remote(mine, mine, ssem.at[t], rsem.at[t], sibling).wait_send()

    sem = pltpu.SemaphoreType.DMA
    return send, finish, [sem((n,)), sem((n,))]


def _share_carry(shards):
    send, finish, sems = _share_phases(len(shards))
    return _Carry(shards, sems, [(0, send), (-1, finish)])


def _share_with_sibling(shards):
    n = len(shards)
    send, finish, sems = _share_phases(n)

    def body(*refs):
        outs, scratch = refs[n:2 * n], refs[2 * n:]
        send(outs, scratch)
        finish(outs, scratch)

    return pl.pallas_call(
        body, name="grads_share", in_specs=[HBM_SPEC] * n, out_specs=[HBM_SPEC] * n,
        out_shape=[jax.ShapeDtypeStruct(s.shape, s.dtype) for s in shards], input_output_aliases={t: t for t in range(n)},
        scratch_shapes=sems, compiler_params=pltpu.CompilerParams(has_side_effects=True),
    )(*shards)


def _all_reduce_small(buf):
    R, C = buf.shape

    def body(in_ref, out_ref, gather, ssem, rsem):
        x, y, c = _place()
        me = 4 * x + 2 * y + c
        gather[me] = in_ref[...]
        flips = [(fx, fy, fc) for fx in (0, 1) for fy in (0, 1) for fc in (0, 1) if fx or fy or fc]
        peers = [(x + fx - 2 * x * fx, y + fy - 2 * y * fy, c + fc - 2 * c * fc) for fx, fy, fc in flips]
        copies = [_remote(in_ref, gather.at[me], ssem.at[k], rsem.at[k], peer) for k, peer in enumerate(peers)]
        for cp in copies:
            cp.start()
        for k, (px, py, pc) in enumerate(peers):
            _remote(in_ref, gather.at[4 * px + 2 * py + pc], ssem.at[k], rsem.at[k], (px, py, pc)).wait_recv()
        for cp in copies:
            cp.wait_send()
        acc = gather[0]
        for d in range(1, 8):
            acc = acc + gather[d]
        out_ref[...] = acc

    sem = pltpu.SemaphoreType.DMA
    vmem = pl.BlockSpec(memory_space=pltpu.VMEM)
    return pl.pallas_call(
        body, name="small_grads_sum", in_specs=[vmem], out_specs=vmem, out_shape=jax.ShapeDtypeStruct((R, C), F32),
        scratch_shapes=[pltpu.VMEM((8, R, C), F32), sem((7,)), sem((7,))],
        compiler_params=pltpu.CompilerParams(has_side_effects=True),
    )(buf)


BIG = ["ret_w_in", "ret_w_out", "att_w_in", "att_w_out", "mlp_w1_0", "mlp_w1_1", "mlp_w2_0", "mlp_w2_1"]
LAYER_OF = {"ret_w_in": ("ret_w_in", 0), "ret_w_out": ("ret_w_out", 0), "att_w_in": ("att_w_in", 0), "att_w_out": ("att_w_out", 0),
            "mlp_w1_0": ("mlp_w1", 0), "mlp_w1_1": ("mlp_w1", 1), "mlp_w2_0": ("mlp_w2", 0), "mlp_w2_1": ("mlp_w2", 1)}
ORDER = ["mix_norm_g", "ret_w_in", "ret_gn_g", "ret_w_out", "att_w_in", "att_rel_bias", "att_w_out", "mlp_norm_g", "mlp_w1", "mlp_w2",
         "final_norm_g"]


def kernel(x, mix_norm_g, ret_w_in, ret_gn_g, ret_w_out, att_w_in, att_rel_bias, att_w_out, mlp_norm_g, mlp_w1, mlp_w2, final_norm_g, loss_target, m_mix_norm_g, m_ret_w_in, m_ret_gn_g, m_ret_w_out, m_att_w_in, m_att_rel_bias, m_att_w_out, m_mlp_norm_g, m_mlp_w1, m_mlp_w2, m_final_norm_g, v_mix_norm_g, v_ret_w_in, v_ret_gn_g, v_ret_w_out, v_att_w_in, v_att_rel_bias, v_att_w_out, v_mlp_norm_g, v_mlp_w1, v_mlp_w2, v_final_norm_g):
    xi, yi, ci = _place()
    chip = 2 * xi + yi
    weights = dict(zip(ORDER, (mix_norm_g, ret_w_in, ret_gn_g, ret_w_out, att_w_in, att_rel_bias, att_w_out, mlp_norm_g, mlp_w1,
                               mlp_w2, final_norm_g)))
    first = dict(zip(ORDER, (m_mix_norm_g, m_ret_w_in, m_ret_gn_g, m_ret_w_out, m_att_w_in, m_att_rel_bias, m_att_w_out,
                             m_mlp_norm_g, m_mlp_w1, m_mlp_w2, m_final_norm_g)))
    second = dict(zip(ORDER, (v_mix_norm_g, v_ret_w_in, v_ret_gn_g, v_ret_w_out, v_att_w_in, v_att_rel_bias, v_att_w_out,
                              v_mlp_norm_g, v_mlp_w1, v_mlp_w2, v_final_norm_g)))

    place = jnp.stack([chip, ci]).astype(jnp.int32)
    slab = {n: _into_slab(place, weights[LAYER_OF[n][0]], LAYER_OF[n][1], BF16, name="cast_" + n) for n in BIG}
    slab_rel = _into_slab(place, att_rel_bias, 0, F32, name="slab_rel_bias")

    grad_x, g_big, small = _step(x, loss_target, slab, slab_rel, place, mix_norm_g, ret_gn_g[0], mlp_norm_g, final_norm_g)

    rows, at = jnp.zeros((16, D_MODEL), F32), 0
    for part in small:
        rows = rows + jnp.pad(part, ((at, 16 - at - part.shape[0]), (0, D_MODEL - part.shape[1])))
        at += part.shape[0]
    rows = _all_reduce_small(rows)
    loss = rows[12, 0]
    grads = {"mix_norm_g": [rows[0:2]], "mlp_norm_g": [rows[2:4]], "final_norm_g": [rows[4:5]], "ret_gn_g": [rows[5:7].reshape(1, 2048)],
             "att_rel_bias": [lax.dynamic_slice_in_dim(rows[7:12].reshape(ATT_HEADS, REL_TABLE), chip * (REL_TABLE // N_CHIPS),
                                                       REL_TABLE // N_CHIPS, axis=1)]}
    for n in BIG:
        grads.setdefault(LAYER_OF[n][0], []).append(g_big[n])

    def as3(a):
        return a.reshape((1,) * (3 - a.ndim) + a.shape)

    results = {}
    for n in ORDER:
        outs = _adamw(as3(weights[n]), grads[n], as3(first[n]), as3(second[n]), name="adamw_" + n)
        results[n] = [o.reshape(weights[n].shape) for o in outs]
    return (loss, grad_x) + tuple(results[n][k] for k in range(4) for n in ORDER)
```

```python
import math

import jax
import jax.numpy as jnp
from jax import lax
from jax.experimental import pallas as pl
from jax.experimental.pallas import tpu as pltpu

F32 = jnp.float32
BF16 = jnp.bfloat16
MESH = pl.DeviceIdType.MESH

D_MODEL = 1024
CHUNK = 64
RET_HEADS = 4
RET_DK = 256
RET_DV = 512
ROPE_BASE = 10000.0
ATT_HEADS = 16
ATT_DH = 64
PAST = 512
MAX_REL = 256
REL_TABLE = MAX_REL + CHUNK
EPS = 1e-6
NEG = -1e30
N_CHIPS = 4

ADAM_LR = 0.001
ADAM_B1 = 0.9
ADAM_B2 = 0.999
ADAM_EPS = 1e-08
ADAM_WD = 0.01
ADAM_STEP = 10

RET_BLOCK = 256
ATT_BLOCK = 256
VMEM_LIMIT = 56 * 1024 * 1024


def _params(n_axes, **kw):
    return pltpu.CompilerParams(dimension_semantics=("arbitrary",) * n_axes, vmem_limit_bytes=VMEM_LIMIT, **kw)


def _dot(a, b):
    return jnp.dot(a, b, preferred_element_type=F32)


def _dot_nt(a, b):
    return lax.dot_general(a, b, (((1,), (1,)), ((), ())), preferred_element_type=F32)


def _dot_tn(a, b):
    return lax.dot_general(a, b, (((0,), (0,)), ((), ())), preferred_element_type=F32)


def _sigmoid(x):
    return 0.5 * jnp.tanh(0.5 * x) + 0.5


HBM_SPEC = pl.BlockSpec(memory_space=pltpu.HBM)


class _Carry:
    def __init__(self, arrays, sems, stages):
        self.arrays, self.sems, self.stages = list(arrays), list(sems), list(stages)


def _merge(a, b):
    na, sa = len(a.arrays), len(a.sems)

    def of_a(fn):
        return lambda refs, sems: fn(refs[:na], sems[:sa])

    def of_b(fn):
        return lambda refs, sems: fn(refs[na:], sems[sa:])

    return _Carry(a.arrays + b.arrays, a.sems + b.sems,
                  [(at, of_a(fn)) for at, fn in a.stages] + [(at, of_b(fn)) for at, fn in b.stages])


def _carry_call(body, carry, *, name, grid, in_specs, out_specs, out_shape, args):
    if carry is None:
        outs = pl.pallas_call(body, name=name, grid=grid, in_specs=in_specs, out_specs=out_specs, out_shape=out_shape,
                              compiler_params=_params(len(grid)))(*args)
        return list(outs), []
    n_in, n_out, n_c = len(in_specs), len(out_specs), len(carry.arrays)
    steps = 1
    for g in grid:
        steps *= g
    assert all(-steps <= at < steps for at, _ in carry.stages)

    def carrying(*refs):
        ins, outs = refs[:n_in], refs[n_in + n_c:n_in + n_c + n_out]
        carried = refs[n_in + n_c + n_out:n_in + 2 * n_c + n_out]
        sems = refs[n_in + 2 * n_c + n_out:]
        step = pl.program_id(0)
        for axis in range(1, len(grid)):
            step = step * grid[axis] + pl.program_id(axis)
        for at, fn in carry.stages:
            if at == 0:
                pl.when(step == 0)(lambda fn=fn: fn(carried, sems))
        body(*ins, *outs)
        for at, fn in carry.stages:
            if at != 0:
                pl.when(step == at % steps)(lambda fn=fn: fn(carried, sems))

    outs = pl.pallas_call(
        carrying, name=name, grid=grid, in_specs=list(in_specs) + [HBM_SPEC] * n_c, out_specs=list(out_specs) + [HBM_SPEC] * n_c,
        out_shape=list(out_shape) + [jax.ShapeDtypeStruct(a.shape, a.dtype) for a in carry.arrays],
        input_output_aliases={n_in + t: n_out + t for t in range(n_c)}, scratch_shapes=carry.sems,
        compiler_params=_params(len(grid), has_side_effects=True),
    )(*args, *carry.arrays)
    return list(outs[:n_out]), list(outs[n_out:])


def _mm_nn(a, w, wkind, *, tm, out_dtype, name, norm_g=None, act=None, square=False, res=None, loss=None, carry=None):
    M, K = a.shape
    cols = w.shape[2]
    N = N_CHIPS * cols if wkind == "col" else cols
    has_norm = norm_g is not None
    steps = M // tm
    assert M % tm == 0 and K == (w.shape[1] if wkind == "col" else N_CHIPS * w.shape[1])
    assert loss is None or (wkind == "row" and act is None)

    def body(*refs):
        it = iter(refs)
        a_ref, w_ref = next(it), next(it)
        g_ref = next(it) if has_norm else None
        r_ref = next(it) if res is not None else None
        if loss is not None:
            t_ref, fg_ref = next(it), next(it)
        o_ref = next(it)
        hn_ref = next(it) if has_norm else None
        if loss is not None:
            dg_ref, sq_ref, loss_ref, o16_ref = next(it), next(it), next(it), next(it)
        if has_norm:
            x = a_ref[...].astype(F32)
            r = lax.rsqrt(jnp.mean(x * x, axis=-1, keepdims=True) + EPS)
            lhs = (x * r * g_ref[...]).astype(BF16)
            hn_ref[...] = lhs
        elif square:
            lhs = a_ref[...].astype(BF16)
            lhs = lhs * lhs
        else:
            lhs = a_ref[...].astype(BF16)

        def loss_head(x):
            i = pl.program_id(0)
            r = lax.rsqrt(jnp.mean(x * x, axis=-1, keepdims=True) + EPS)
            xh = x * r
            gg = fg_ref[...]
            diff = xh * gg - t_ref[...]
            sq = jnp.sum(diff * diff, axis=0, keepdims=True)
            dy = diff * (1.0 / N)
            dg_part = jnp.sum(dy * xh, axis=0, keepdims=True)

            @pl.when(i == 0)
            def _():
                sq_ref[...] = sq
                dg_ref[...] = dg_part

            @pl.when(i > 0)
            def _():
                sq_ref[...] += sq
                dg_ref[...] += dg_part
            t = dy * gg
            d = r * (t - xh * jnp.mean(t * xh, axis=-1, keepdims=True))
            o_ref[...] = d
            o16_ref[...] = d.astype(BF16)

            @pl.when(i == steps - 1)
            def _():
                loss_ref[...] = (0.5 / N) * jnp.sum(sq_ref[...], axis=1, keepdims=True)

        def finish(acc, sl):
            if act == "relu":
                acc = jnp.maximum(acc, 0.0)
            if r_ref is not None:
                acc = acc + r_ref[:, sl]
            if loss is not None:
                loss_head(acc)
            else:
                o_ref[:, sl] = acc.astype(out_dtype)

        if wkind == "col":
            for s in range(N_CHIPS):
                finish(_dot(lhs, w_ref[s]), slice(s * cols, (s + 1) * cols))
        else:
            finish(_dot(lhs, w_ref[...].reshape(K, N)), slice(None))

    in_specs = [pl.BlockSpec((tm, K), lambda i: (i, 0)), pl.BlockSpec(w.shape, lambda i: (0, 0, 0))]
    args = [a, w]
    if has_norm:
        in_specs.append(pl.BlockSpec((1, K), lambda i: (0, 0)))
        args.append(norm_g.reshape(1, K))
    if res is not None:
        in_specs.append(pl.BlockSpec((tm, N), lambda i: (i, 0)))
        args.append(res)
    if loss is not None:
        in_specs += [pl.BlockSpec((tm, N), lambda i: (i, 0)), pl.BlockSpec((1, N), lambda i: (0, 0))]
        args += [loss[0], loss[1].reshape(1, N)]
    out_shape = [jax.ShapeDtypeStruct((M, N), out_dtype)]
    out_specs = [pl.BlockSpec((tm, N), lambda i: (i, 0))]
    if has_norm:
        out_shape.append(jax.ShapeDtypeStruct((M, K), BF16))
        out_specs.append(pl.BlockSpec((tm, K), lambda i: (i, 0)))
    if loss is not None:
        out_shape += [jax.ShapeDtypeStruct((1, N), F32), jax.ShapeDtypeStruct((1, N), F32), jax.ShapeDtypeStruct((1, 1), F32),
                      jax.ShapeDtypeStruct((M, N), BF16)]
        out_specs += [pl.BlockSpec((1, N), lambda i: (0, 0)), pl.BlockSpec((1, N), lambda i: (0, 0)),
                      pl.BlockSpec((1, 1), lambda i: (0, 0)), pl.BlockSpec((tm, N), lambda i: (i, 0))]
    outs, carried = _carry_call(body, carry, name=name, grid=(steps,), in_specs=in_specs, out_specs=out_specs,
                                out_shape=out_shape, args=args)
    result = outs if has_norm or loss is not None else outs[0]
    return result if carry is None else (result, carried)


def _mm_nt(a, w, wkind, *, tm, name, epi, a2=None, h=None, g=None, dres=None, bf16_copy=False, carry=None):
    parts = tuple(a) if isinstance(a, (tuple, list)) else (a,)
    M, part_w = parts[0].shape
    Nw = part_w * len(parts)
    rows, cols = w.shape[1], w.shape[2]
    Kw = rows if wkind == "col" else N_CHIPS * rows
    assert M % tm == 0 and Nw == (N_CHIPS * cols if wkind == "col" else cols)
    assert epi != "normbwd" or wkind == "col"
    assert len(parts) == 1 or wkind == "col"
    chunk = math.gcd(part_w, cols)

    def body(*refs):
        it = iter(refs)
        a_refs = [next(it) for _ in parts]
        a_ref, w_ref = a_refs[0], next(it)
        a2_ref = next(it) if epi == "relu2bwd" else None
        if epi == "normbwd":
            h_ref, g_ref, dres_ref = next(it), next(it), next(it)
        o_ref = next(it)
        dg_ref = next(it) if epi == "normbwd" else None
        o16_ref = next(it) if bf16_copy else None
        i = pl.program_id(0)

        def finish(acc, sl):
            if epi == "bf16":
                o_ref[:, sl] = acc.astype(BF16)
            elif epi == "relu2bwd":
                o_ref[:, sl] = (acc * (2.0 * a2_ref[:, sl].astype(F32))).astype(BF16)
            else:
                x = h_ref[...]
                r = lax.rsqrt(jnp.mean(x * x, axis=-1, keepdims=True) + EPS)
                xh = x * r
                dg_part = jnp.sum(acc * xh, axis=0, keepdims=True)

                @pl.when(i == 0)
                def _():
                    dg_ref[...] = dg_part

                @pl.when(i > 0)
                def _():
                    dg_ref[...] += dg_part
                t = acc * g_ref[...]
                d = dres_ref[...] + r * (t - xh * jnp.mean(t * xh, axis=-1, keepdims=True))
                o_ref[...] = d
                if bf16_copy:
                    o16_ref[...] = d.astype(BF16)

        if wkind == "col":
            acc = None
            for lo in range(0, Nw, chunk):
                (src, a_lo), (s, w_lo) = divmod(lo, part_w), divmod(lo, cols)
                part = _dot_nt(a_refs[src][:, a_lo:a_lo + chunk].astype(BF16), w_ref[s, :, w_lo:w_lo + chunk])
                acc = part if acc is None else acc + part
            finish(acc, slice(None))
        else:
            lhs = a_ref[...].astype(BF16)
            for s in range(N_CHIPS):
                finish(_dot_nt(lhs, w_ref[s]), slice(s * rows, (s + 1) * rows))

    in_specs = [pl.BlockSpec((tm, part_w), lambda i: (i, 0)) for _ in parts] + [pl.BlockSpec(w.shape, lambda i: (0, 0, 0))]
    args = [*parts, w]
    out_dtype = BF16
    if epi == "relu2bwd":
        in_specs.append(pl.BlockSpec((tm, Kw), lambda i: (i, 0)))
        args.append(a2)
    if epi == "normbwd":
        in_specs += [pl.BlockSpec((tm, Kw), lambda i: (i, 0)), pl.BlockSpec((1, Kw), lambda i: (0, 0)),
                     pl.BlockSpec((tm, Kw), lambda i: (i, 0))]
        args += [h, g.reshape(1, Kw), dres]
        out_dtype = F32
    out_shape = [jax.ShapeDtypeStruct((M, Kw), out_dtype)]
    out_specs = [pl.BlockSpec((tm, Kw), lambda i: (i, 0))]
    if epi == "normbwd":
        out_shape.append(jax.ShapeDtypeStruct((1, Kw), F32))
        out_specs.append(pl.BlockSpec((1, Kw), lambda i: (0, 0)))
    assert not bf16_copy or epi == "normbwd"
    if bf16_copy:
        out_shape.append(jax.ShapeDtypeStruct((M, Kw), BF16))
        out_specs.append(pl.BlockSpec((tm, Kw), lambda i: (i, 0)))
    outs, carried = _carry_call(body, carry, name=name, grid=(M // tm,), in_specs=in_specs, out_specs=out_specs,
                                out_shape=out_shape, args=args)
    result = outs if epi == "normbwd" else outs[0]
    return result if carry is None else (result, carried)


def _mm_tn(a, b, okind, *, tt, tk, tn, name, square=False, carry=None):
    parts = tuple(b) if isinstance(b, (tuple, list)) else (b,)
    T, K = a.shape
    part_w = parts[0].shape[1]
    N = part_w * len(parts)
    assert T % tt == 0 and K % tk == 0 and part_w % tn == 0
    nt = T // tt
    per_part = part_w // tn
    if okind == "col":
        per = (N // N_CHIPS) // tn
        assert (N // N_CHIPS) % tn == 0
        out_shape = jax.ShapeDtypeStruct((N_CHIPS, K, N // N_CHIPS), F32)
        out_spec = pl.BlockSpec((None, tk, tn), lambda ki, nj, t: (nj // per, ki, nj % per))
    else:
        per = (K // N_CHIPS) // tk
        assert (K // N_CHIPS) % tk == 0
        out_shape = jax.ShapeDtypeStruct((N_CHIPS, K // N_CHIPS, N), F32)
        out_spec = pl.BlockSpec((None, tk, tn), lambda ki, nj, t: (ki // per, ki % per, nj))

    def body(a_ref, *refs):
        b_refs, o_ref = refs[:-1], refs[-1]
        nj, t = pl.program_id(1), pl.program_id(2)

        def accumulate(b_ref):
            lhs = a_ref[...].astype(BF16)
            part = _dot_tn(lhs * lhs if square else lhs, b_ref[...].astype(BF16))
            if nt == 1:
                o_ref[...] = part
            else:
                @pl.when(t == 0)
                def _():
                    o_ref[...] = part

                @pl.when(t > 0)
                def _():
                    o_ref[...] += part

        if len(parts) == 1:
            accumulate(b_refs[0])
        else:
            for s, b_ref in enumerate(b_refs):
                pl.when(nj // per_part == s)(lambda b_ref=b_ref: accumulate(b_ref))

    def b_spec(s):
        return pl.BlockSpec((tt, tn), lambda ki, nj, t: (t, jnp.clip(nj - s * per_part, 0, per_part - 1)))

    outs, carried = _carry_call(
        body, carry, name=name, grid=(K // tk, N // tn, nt),
        in_specs=[pl.BlockSpec((tt, tk), lambda ki, nj, t: (t, ki))] + [b_spec(s) for s in range(len(parts))],
        out_specs=[out_spec], out_shape=[out_shape], args=[a, *parts])
    return outs[0] if carry is None else (outs[0], carried)


def _ret_consts(S, LB):
    log_gamma = jnp.log1p(-jnp.exp2(-5.0 - jnp.arange(RET_HEADS, dtype=F32)))
    idx = jnp.arange(LB, dtype=F32)
    n, m = idx[:, None], idx[None, :]
    cn, cm = jnp.floor(n / CHUNK), jnp.floor(m / CHUNK)
    dist = jnp.where(cm == cn, jnp.abs(n - m), n - m)
    dmat = jnp.where((cm <= cn)[None], jnp.exp(log_gamma[:, None, None] * dist[None]), 0.0)
    qd = jnp.exp(log_gamma[:, None] * (idx + 1.0)[None, :])[..., None]
    kd = jnp.exp(log_gamma[:, None] * (LB - 1 - idx)[None, :])[..., None]
    bd = jnp.exp(log_gamma * LB).reshape(RET_HEADS, 1, 1) * jnp.ones((RET_HEADS, 1, 128), F32)
    half = RET_DK // 2
    inv = jnp.exp(-jnp.log(ROPE_BASE) * jnp.arange(half, dtype=F32) / half)
    ang = jnp.arange(S, dtype=F32)[:, None] * inv[None, :]
    return dmat.astype(F32), qd.astype(F32), kd.astype(F32), bd, jnp.cos(ang), jnp.sin(ang)


def _rope(t, c, s):
    t1, t2 = t[:, :128], t[:, 128:]
    return jnp.concatenate([t1 * c - t2 * s, t1 * s + t2 * c], axis=-1)


def _rope_inv(d, c, s):
    d1, d2 = d[:, :128], d[:, 128:]
    return jnp.concatenate([d1 * c + d2 * s, d2 * c - d1 * s], axis=-1)


def _ret_block_fwd(p_ref, h, c, s, d_ref, qd_ref, kd_ref, stb):
    q = _rope(p_ref[:, h * RET_DK:(h + 1) * RET_DK].astype(F32), c, s)
    k = _rope(p_ref[:, 1024 + h * RET_DK:1024 + (h + 1) * RET_DK].astype(F32), c, s) * (RET_DK ** -0.5)
    v = p_ref[:, 2048 + h * RET_DV:2048 + (h + 1) * RET_DV]
    qb, kb = q.astype(BF16), k.astype(BF16)
    scb = (_dot_nt(qb, kb) * d_ref[h]).astype(BF16)
    o = _dot(scb, v) + qd_ref[h] * _dot(qb, stb)
    return q, k, qb, kb, v, scb, o


def _ret_fwd(proj, gn_g, consts, *, B, S):
    LB = RET_BLOCK
    nb = S // LB
    T = B * S
    dmat, qd, kd, bd, cos, sin = consts

    def body(p_ref, cos_ref, sin_ref, d_ref, qd_ref, kd_ref, bd_ref, gng_ref, y_ref, st_ref, state_s):
        i = pl.program_id(1)

        @pl.when(i == 0)
        def _():
            state_s[...] = jnp.zeros_like(state_s)
        c, s = cos_ref[...], sin_ref[...]
        for h in range(RET_HEADS):
            st = state_s[h]
            stb = st.astype(BF16)
            st_ref[h] = stb
            q, k, qb, kb, v, scb, o = _ret_block_fwd(p_ref, h, c, s, d_ref, qd_ref, kd_ref, stb)
            kdk = (k * kd_ref[h]).astype(BF16)
            state_s[h] = st * bd_ref[h][:, :1] + _dot_tn(kdk, v)
            gate = p_ref[:, 4096 + h * RET_DV:4096 + (h + 1) * RET_DV].astype(F32)
            mu = jnp.mean(o, axis=-1, keepdims=True)
            oc = o - mu
            xh = oc * lax.rsqrt(jnp.mean(oc * oc, axis=-1, keepdims=True) + EPS)
            y = (gate * _sigmoid(gate)) * (xh * gng_ref[:, h * RET_DV:(h + 1) * RET_DV])
            y_ref[:, h * RET_DV:(h + 1) * RET_DV] = y.astype(BF16)

    const = lambda b, i: (0, 0, 0)
    return pl.pallas_call(
        body, name="ret_fwd", grid=(B, nb),
        in_specs=[pl.BlockSpec((LB, 6144), lambda b, i: (b * nb + i, 0)),
                  pl.BlockSpec((LB, 128), lambda b, i: (i, 0)), pl.BlockSpec((LB, 128), lambda b, i: (i, 0)),
                  pl.BlockSpec((RET_HEADS, LB, LB), const), pl.BlockSpec((RET_HEADS, LB, 1), const),
                  pl.BlockSpec((RET_HEADS, LB, 1), const), pl.BlockSpec((RET_HEADS, 1, 128), const),
                  pl.BlockSpec((1, 2048), lambda b, i: (0, 0))],
        out_specs=[pl.BlockSpec((LB, 2048), lambda b, i: (b * nb + i, 0)),
                   pl.BlockSpec((None, None, RET_HEADS, RET_DK, RET_DV), lambda b, i: (b, i, 0, 0, 0))],
        out_shape=[jax.ShapeDtypeStruct((T, 2048), BF16), jax.ShapeDtypeStruct((B, nb, RET_HEADS, RET_DK, RET_DV), BF16)],
        scratch_shapes=[pltpu.VMEM((RET_HEADS, RET_DK, RET_DV), F32)], compiler_params=_params(2),
    )(proj, cos, sin, dmat, qd, kd, bd, gn_g.reshape(1, 2048))


def _ret_bwd(proj, dy, states, gn_g, consts, *, B, S):
    LB = RET_BLOCK
    nb = S // LB
    T = B * S
    dmat, qd, kd, bd, cos, sin = consts

    def body(p_ref, dy_ref, st_ref, cos_ref, sin_ref, d_ref, qd_ref, kd_ref, bd_ref, gng_ref, dp_ref, dgn_ref, dstate_s):
        b, i = pl.program_id(0), pl.program_id(1)

        @pl.when(i == 0)
        def _():
            dstate_s[...] = jnp.zeros_like(dstate_s)

        @pl.when((b == 0) & (i == 0))
        def _():
            dgn_ref[...] = jnp.zeros_like(dgn_ref)
        c, s = cos_ref[...], sin_ref[...]
        for h in range(RET_HEADS):
            vs = slice(h * RET_DV, (h + 1) * RET_DV)
            stb = st_ref[h]
            q, k, qb, kb, v, scb, o = _ret_block_fwd(p_ref, h, c, s, d_ref, qd_ref, kd_ref, stb)
            gate = p_ref[:, 4096 + h * RET_DV:4096 + (h + 1) * RET_DV].astype(F32)
            mu = jnp.mean(o, axis=-1, keepdims=True)
            oc = o - mu
            rstd = lax.rsqrt(jnp.mean(oc * oc, axis=-1, keepdims=True) + EPS)
            xh = oc * rstd
            gng = gng_ref[:, vs]
            dyh = dy_ref[:, vs].astype(F32)
            sg = _sigmoid(gate)
            silu = gate * sg
            dgn_ref[:, vs] += jnp.sum(dyh * silu * xh, axis=0, keepdims=True)
            dxh = dyh * silu * gng
            do = rstd * (dxh - jnp.mean(dxh, axis=-1, keepdims=True) - xh * jnp.mean(dxh * xh, axis=-1, keepdims=True))
            dgate = dyh * xh * gng * (sg * (1.0 + gate * (1.0 - sg)))
            dob = do.astype(BF16)
            dsb = (_dot_nt(dob, v) * d_ref[h]).astype(BF16)
            dst = dstate_s[h]
            dstb = dst.astype(BF16)
            kdk = (k * kd_ref[h]).astype(BF16)
            dqr = _dot(dsb, kb) + qd_ref[h] * _dot_nt(dob, stb)
            dkr = _dot_tn(dsb, qb) + kd_ref[h] * _dot_nt(v, dstb)
            dv = _dot_tn(scb, dob) + _dot(kdk, dstb)
            dstate_s[h] = dst * bd_ref[h][:, :1] + _dot_tn((q * qd_ref[h]).astype(BF16), dob)
            dp_ref[:, h * RET_DK:(h + 1) * RET_DK] = _rope_inv(dqr, c, s).astype(BF16)
            dp_ref[:, 1024 + h * RET_DK:1024 + (h + 1) * RET_DK] = (_rope_inv(dkr, c, s) * (RET_DK ** -0.5)).astype(BF16)
            dp_ref[:, 2048 + h * RET_DV:2048 + (h + 1) * RET_DV] = dv.astype(BF16)
            dp_ref[:, 4096 + h * RET_DV:4096 + (h + 1) * RET_DV] = dgate.astype(BF16)

    const = lambda b, i: (0, 0, 0)
    rev = lambda b, i: (b * nb + nb - 1 - i, 0)
    return pl.pallas_call(
        body, name="ret_bwd", grid=(B, nb),
        in_specs=[pl.BlockSpec((LB, 6144), rev), pl.BlockSpec((LB, 2048), rev),
                  pl.BlockSpec((None, None, RET_HEADS, RET_DK, RET_DV), lambda b, i: (b, nb - 1 - i, 0, 0, 0)),
                  pl.BlockSpec((LB, 128), lambda b, i: (nb - 1 - i, 0)), pl.BlockSpec((LB, 128), lambda b, i: (nb - 1 - i, 0)),
                  pl.BlockSpec((RET_HEADS, LB, LB), const), pl.BlockSpec((RET_HEADS, LB, 1), const),
                  pl.BlockSpec((RET_HEADS, LB, 1), const), pl.BlockSpec((RET_HEADS, 1, 128), const),
                  pl.BlockSpec((1, 2048), lambda b, i: (0, 0))],
        out_specs=[pl.BlockSpec((LB, 6144), rev), pl.BlockSpec((1, 2048), lambda b, i: (0, 0))],
        out_shape=[jax.ShapeDtypeStruct((T, 6144), BF16), jax.ShapeDtypeStruct((1, 2048), F32)],
        scratch_shapes=[pltpu.VMEM((RET_HEADS, RET_DK, RET_DV), F32)], compiler_params=_params(2),
    )(proj, dy, states, cos, sin, dmat, qd, kd, bd, gn_g.reshape(1, 2048))


BIAS_LANES = 4 * ATT_BLOCK


def _diag_onehot():
    r = lax.broadcasted_iota(jnp.int32, (REL_TABLE, BIAS_LANES), 0)
    j = lax.broadcasted_iota(jnp.int32, (REL_TABLE, BIAS_LANES), 1)
    idx = jnp.maximum(j - ATT_BLOCK - PAST, -MAX_REL) + MAX_REL
    return jnp.where(idx == r, 1.0, 0.0).astype(F32)


def _row_is(j):
    return lax.broadcasted_iota(jnp.int32, (8, BIAS_LANES), 0) == j


def _att_bias(table):
    QB, KW = ATT_BLOCK, 3 * ATT_BLOCK

    def body(t_ref, bt_ref):
        row = jnp.broadcast_to(t_ref[...], (8, REL_TABLE))
        diag = jnp.dot(row, _diag_onehot(), preferred_element_type=F32, precision=lax.Precision.HIGHEST)
        rows = jnp.zeros((8, BIAS_LANES), F32)
        for j in range(8):
            rows = jnp.where(_row_is(j), diag if j == 0 else pltpu.roll(diag, j, axis=1), rows)
        n = 8
        while n < QB:
            rows = jnp.concatenate([rows, pltpu.roll(rows, n, axis=1)], axis=0)
            n *= 2
        bias = rows[:, QB:]
        qi = lax.broadcasted_iota(jnp.int32, (QB, KW), 0)
        kj = lax.broadcasted_iota(jnp.int32, (QB, KW), 1)
        lo = (qi // CHUNK) * CHUNK
        bt_ref[...] = jnp.where((kj >= lo) & (kj < lo + PAST + CHUNK), bias, NEG).T

    return pl.pallas_call(
        body, name="att_bias", grid=(ATT_HEADS,),
        in_specs=[pl.BlockSpec((None, 1, REL_TABLE), lambda h: (h, 0, 0))],
        out_specs=pl.BlockSpec((None, KW, QB), lambda h: (h // 2, 0, h % 2)),
        out_shape=jax.ShapeDtypeStruct((ATT_HEADS // 2, KW, 2 * QB), F32),
        compiler_params=_params(1),
    )(table.reshape(ATT_HEADS, 1, REL_TABLE))


def _att_bias_grad(dbias_t):
    QB, KW = ATT_BLOCK, 3 * ATT_BLOCK

    def body(d_ref, o_ref):
        rows = jnp.concatenate([jnp.zeros((QB, QB), F32), d_ref[...].T], axis=1)
        n = QB // 2
        while n >= 8:
            rows = rows[:n] + pltpu.roll(rows[n:], BIAS_LANES - n, axis=1)
            n //= 2
        acc = jnp.zeros((8, BIAS_LANES), F32)
        for j in range(8):
            acc = acc + jnp.where(_row_is(j), rows if j == 0 else pltpu.roll(rows, BIAS_LANES - j, axis=1), 0.0)
        diag = jnp.broadcast_to(jnp.sum(acc, axis=0, keepdims=True), (8, BIAS_LANES))
        grad = lax.dot_general(diag, _diag_onehot(), (((1,), (1,)), ((), ())), preferred_element_type=F32,
                               precision=lax.Precision.HIGHEST)
        o_ref[...] = grad[:1]

    return pl.pallas_call(
        body, name="att_bias_grad", grid=(ATT_HEADS,),
        in_specs=[pl.BlockSpec((None, KW, QB), lambda h: (h // 2, 0, h % 2))],
        out_specs=pl.BlockSpec((None, 1, REL_TABLE), lambda h: (h, 0, 0)),
        out_shape=jax.ShapeDtypeStruct((ATT_HEADS, 1, REL_TABLE), F32), compiler_params=_params(1),
    )(dbias_t).reshape(ATT_HEADS, REL_TABLE)


def _by_head(x):
    first = lax.broadcasted_iota(jnp.int32, x.shape, 1) < ATT_DH
    zero = jnp.zeros_like(x)
    return jnp.concatenate([jnp.where(first, x, zero), jnp.where(first, zero, x)], axis=0)


def _att_fwd(qkv, bias_t, *, B, S):
    QB = ATT_BLOCK
    nb = S // QB
    KW = 3 * QB
    T = B * S
    scale = ATT_DH ** -0.5

    def body(q_ref, k0, k1, k2, v0, v1, v2, b_ref, o_ref, lse_ref, s_blk):
        i = pl.program_id(2)
        q2 = _by_head((q_ref[...].astype(F32) * scale).astype(BF16))
        m = jnp.full((1, 2 * QB), NEG, F32)
        for d, k_ref in enumerate((k0, k1, k2)):
            st = _dot_nt(k_ref[...], q2) + b_ref[d * QB:(d + 1) * QB, :]
            st = jnp.where(i + d >= 2, st, NEG)
            s_blk[d] = st
            m = jnp.maximum(m, jnp.max(st, axis=0, keepdims=True))
        l = jnp.zeros((1, 2 * QB), F32)
        o_t = jnp.zeros((128, 2 * QB), F32)
        for d, v_ref in enumerate((v0, v1, v2)):
            e = jnp.exp(s_blk[d] - m)
            l = l + jnp.sum(e, axis=0, keepdims=True)
            o_t = o_t + _dot(v_ref[...].astype(F32).T.astype(BF16), e.astype(BF16))
        o_t = o_t / l
        row = lax.broadcasted_iota(jnp.int32, (128, QB), 0)
        o_ref[...] = jnp.where(row < ATT_DH, o_t[:, :QB], o_t[:, QB:]).T.astype(BF16)
        lse = m + jnp.log(l)
        row8 = lax.broadcasted_iota(jnp.int32, (8, QB), 0)
        lse_ref[...] = jnp.where(row8 == 0, lse[:, :QB], jnp.where(row8 == 1, lse[:, QB:], 0.0))

    def kv(d, col0):
        return pl.BlockSpec((QB, 128), lambda hp, b, i: (b * nb + jnp.maximum(i - d, 0), col0 + hp))

    return pl.pallas_call(
        body, name="att_fwd", grid=(8, B, nb),
        in_specs=[pl.BlockSpec((QB, 128), lambda hp, b, i: (b * nb + i, hp)),
                  kv(2, 8), kv(1, 8), kv(0, 8), kv(2, 16), kv(1, 16), kv(0, 16),
                  pl.BlockSpec((None, KW, 2 * QB), lambda hp, b, i: (hp, 0, 0))],
        out_specs=[pl.BlockSpec((QB, 128), lambda hp, b, i: (b * nb + i, hp)),
                   pl.BlockSpec((None, 8, QB), lambda hp, b, i: (hp, 0, b * nb + i))],
        out_shape=[jax.ShapeDtypeStruct((T, 1024), BF16), jax.ShapeDtypeStruct((8, 8, T), F32)],
        scratch_shapes=[pltpu.VMEM((3, QB, 2 * QB), F32)], compiler_params=_params(3),
    )(qkv, qkv, qkv, qkv, qkv, qkv, qkv, bias_t)


def _att_bwd(qkv, do, o, lse, bias_t, *, B, S):
    QB = ATT_BLOCK
    nb = S // QB
    KW = 3 * QB
    T = B * S
    scale = ATT_DH ** -0.5
    TK = 256

    def body(q_ref, k0, k1, k2, v0, v1, v2, do_ref, o_ref, lse_ref, b_ref, dq_ref, dk_ref, dv_ref, db_ref, dk_acc, dv_acc):
        b, i = pl.program_id(1), pl.program_id(2)

        @pl.when(i == 0)
        def _():
            dk_acc[...] = jnp.zeros_like(dk_acc)
            dv_acc[...] = jnp.zeros_like(dv_acc)

        @pl.when((b == 0) & (i == 0))
        def _():
            db_ref[...] = jnp.zeros_like(db_ref)

        def write_out():
            slot = (i + 1) % 3
            dk_ref[...] = dk_acc[slot].astype(BF16)
            dv_ref[...] = dv_acc[slot].astype(BF16)
            dk_acc[slot] = jnp.zeros((QB, 128), F32)
            dv_acc[slot] = jnp.zeros((QB, 128), F32)

        @pl.when(i < nb)
        def _():
            dout = do_ref[...]
            q2 = _by_head((q_ref[...].astype(F32) * scale).astype(BF16))
            do2 = _by_head(dout)
            delta_t = (o_ref[...].astype(F32) * dout.astype(F32)).T
            delta2 = jnp.concatenate([jnp.sum(delta_t[:ATT_DH], axis=0, keepdims=True),
                                      jnp.sum(delta_t[ATT_DH:], axis=0, keepdims=True)], axis=1)
            lse2 = jnp.concatenate([lse_ref[0:1, :], lse_ref[1:2, :]], axis=1)
            dq_t = jnp.zeros((128, 2 * QB), F32)
            for d, (k_ref, v_ref) in enumerate(((k0, v0), (k1, v1), (k2, v2))):
                kblk, vblk = k_ref[...], v_ref[...]
                kt = kblk.astype(F32).T.astype(BF16)
                lse_d = jnp.where(i + d >= 2, lse2, -NEG)
                slot = (i + 1 + d) % 3
                for t in range(QB // TK):
                    rows = slice(t * TK, (t + 1) * TK)
                    wrows = slice(d * QB + t * TK, d * QB + (t + 1) * TK)
                    p = jnp.exp(_dot_nt(kblk[rows], q2) + b_ref[wrows, :] - lse_d)
                    ds = p * (_dot_nt(vblk[rows], do2) - delta2)
                    db_ref[wrows, :] += ds
                    dsb = ds.astype(BF16)
                    dk_acc[slot, rows, :] += _dot(dsb, q2)
                    dv_acc[slot, rows, :] += _dot(p.astype(BF16), do2)
                    dq_t += _dot(kt[:, rows], dsb)
            row = lax.broadcasted_iota(jnp.int32, (128, QB), 0)
            dq_ref[...] = (jnp.where(row < ATT_DH, dq_t[:, :QB], dq_t[:, QB:]) * scale).T.astype(BF16)
            write_out()

        pl.when(i >= nb)(write_out)

    def qrow(b, i):
        return b * nb + jnp.minimum(i, nb - 1)

    def kv(d, col0):
        return pl.BlockSpec((QB, 128), lambda hp, b, i: (b * nb + jnp.maximum(jnp.minimum(i, nb - 1) - d, 0), col0 + hp))

    late = pl.BlockSpec((QB, 128), lambda hp, b, i: (b * nb + jnp.maximum(i - 2, 0), hp))
    return pl.pallas_call(
        body, name="att_bwd", grid=(8, B, nb + 2),
        in_specs=[pl.BlockSpec((QB, 128), lambda hp, b, i: (qrow(b, i), hp)),
                  kv(2, 8), kv(1, 8), kv(0, 8), kv(2, 16), kv(1, 16), kv(0, 16),
                  pl.BlockSpec((QB, 128), lambda hp, b, i: (qrow(b, i), hp)),
                  pl.BlockSpec((QB, 128), lambda hp, b, i: (qrow(b, i), hp)),
                  pl.BlockSpec((None, 8, QB), lambda hp, b, i: (hp, 0, qrow(b, i))),
                  pl.BlockSpec((None, KW, 2 * QB), lambda hp, b, i: (hp, 0, 0))],
        out_specs=[pl.BlockSpec((QB, 128), lambda hp, b, i: (qrow(b, i), hp)), late, late,
                   pl.BlockSpec((None, KW, 2 * QB), lambda hp, b, i: (hp, 0, 0))],
        out_shape=[jax.ShapeDtypeStruct((T, 1024), BF16)] * 3 + [jax.ShapeDtypeStruct((ATT_HEADS // 2, KW, 2 * QB), F32)],
        scratch_shapes=[pltpu.VMEM((3, QB, 128), F32), pltpu.VMEM((3, QB, 128), F32)], compiler_params=_params(3),
    )(qkv, qkv, qkv, qkv, qkv, qkv, qkv, do, o, lse, bias_t)


def _tok_tile(T, want):
    t = min(T, want)
    assert T % t == 0
    return t


def _step(x, tgt, slab, slab_rel, place, mix_g, gn_g, mlp_g, fin_g):
    B, S, D = x.shape
    T = B * S
    h0 = x.reshape(T, D)
    tgt = tgt.reshape(T, D)
    tm = _tok_tile(T, 1024)
    tb = _tok_tile(T, 512)
    tt = _tok_tile(T, 8192)
    tf = _tok_tile(T, 2048)
    consts = _ret_consts(S, RET_BLOCK)
    w = {}

    (w["ret_w_in"],) = _all_gather_slabs([slab["ret_w_in"]])
    (proj, hn0), (w["ret_w_out"], w["mlp_w1_0"]) = _mm_nn(
        h0, w["ret_w_in"], "col", tm=tb, out_dtype=BF16, name="ret_in", norm_g=mix_g[0],
        carry=_gather_carry([slab["ret_w_out"], slab["mlp_w1_0"]]))
    y_ret, states = _ret_fwd(proj, gn_g, consts, B=B, S=S)
    h1, (w["mlp_w2_0"],) = _mm_nn(y_ret, w["ret_w_out"], "row", tm=tm, out_dtype=F32, name="ret_out", res=h0,
                                  carry=_gather_carry([slab["mlp_w2_0"]]))
    (a0, hm0), (w["att_w_in"], w["att_w_out"], rel_slabs) = _mm_nn(
        h1, w["mlp_w1_0"], "col", tm=tb, out_dtype=BF16, name="mlp0_up", norm_g=mlp_g[0], act="relu",
        carry=_gather_carry([slab["att_w_in"], slab["att_w_out"], slab_rel]))
    h2, (w["mlp_w1_1"],) = _mm_nn(a0, w["mlp_w2_0"], "row", tm=tb, out_dtype=F32, name="mlp0_down", square=True, res=h1,
                                  carry=_gather_carry([slab["mlp_w1_1"]]))
    rel_bias = jnp.transpose(rel_slabs, (1, 0, 2)).reshape(ATT_HEADS, REL_TABLE)
    bias_t = _att_bias(rel_bias)
    (qkv, hn1), (w["mlp_w2_1"],) = _mm_nn(h2, w["att_w_in"], "col", tm=tb, out_dtype=BF16, name="att_in", norm_g=mix_g[1],
                                          carry=_gather_carry([slab["mlp_w2_1"]]))
    o_att, lse = _att_fwd(qkv, bias_t, B=B, S=S)
    h3 = _mm_nn(o_att, w["att_w_out"], "row", tm=tm, out_dtype=F32, name="att_out", res=h2)
    a1, hm1 = _mm_nn(h3, w["mlp_w1_1"], "col", tm=tb, out_dtype=BF16, name="mlp1_up", norm_g=mlp_g[1], act="relu")
    dh4, d_fin_g, _, loss, dh4_b = _mm_nn(a1, w["mlp_w2_1"], "row", tm=tb, out_dtype=F32, name="mlp1_down", square=True, res=h3,
                                          loss=(tgt, fin_g))

    gw = {}
    gw["mlp_w2_1"] = _mm_tn(a1, dh4_b, "row", tt=tf, tk=1024, tn=D, name="d_mlp1_w2", square=True)
    dz1 = _mm_nt(dh4_b, w["mlp_w2_1"], "row", tm=tb, name="d_mlp1_act", epi="relu2bwd", a2=a1)
    gw["mlp_w1_1"] = _mm_tn(hm1, dz1, "col", tt=tt, tk=D, tn=256, name="d_mlp1_w1")
    dh3, d_mlp_g1, dh3_b = _mm_nt(dz1, w["mlp_w1_1"], "col", tm=tb, name="d_mlp1_in", epi="normbwd",
                                  h=h3, g=mlp_g[1], dres=dh4, bf16_copy=True)
    gw["att_w_out"] = _mm_tn(o_att, dh3_b, "row", tt=tt, tk=256, tn=256, name="d_att_wout")
    do_att = _mm_nt(dh3_b, w["att_w_out"], "row", tm=tb, name="d_att_o", epi="bf16")
    dq, dk, dv, dbias_t = _att_bwd(qkv, do_att, o_att, lse, bias_t, B=B, S=S)
    d_rel = _att_bias_grad(dbias_t)
    dqkv = (dq, dk, dv)
    gw["att_w_in"] = _mm_tn(hn1, dqkv, "col", tt=tt, tk=D, tn=256, name="d_att_win")
    sums, landed = {}, {}

    def swap_halves(names):
        return _sibling_carry([gw[n] for n in names])

    def add_halves(names, carried):
        for t, n in enumerate(names):
            sums[n] = _add_sibling(place, carried[t], carried[len(names) + t], name="chip_sum_" + n)

    def carried_exchange(names):
        return _chips_carry([sums[n][0] for n in names])

    def keep(names, carried):
        landed.update(zip(names, carried[len(names):]))

    layer1 = ["mlp_w1_1", "mlp_w2_1", "att_w_in", "att_w_out"]
    (dh2, d_mix_g1, dh2_b), carried = _mm_nt(dqkv, w["att_w_in"], "col", tm=tb, name="d_att_in", epi="normbwd",
                                             h=h2, g=mix_g[1], dres=dh3, bf16_copy=True, carry=swap_halves(layer1))
    add_halves(layer1, carried)
    gw["mlp_w2_0"] = _mm_tn(a0, dh2_b, "row", tt=tf, tk=1024, tn=D, name="d_mlp0_w2", square=True)
    dz0, carried = _mm_nt(dh2_b, w["mlp_w2_0"], "row", tm=tb, name="d_mlp0_act", epi="relu2bwd", a2=a0,
                          carry=carried_exchange(["mlp_w1_1"]))
    keep(["mlp_w1_1"], carried)
    gw["mlp_w1_0"], carried = _mm_tn(hm0, dz0, "col", tt=tt, tk=D, tn=256, name="d_mlp0_w1", carry=carried_exchange(["mlp_w2_1"]))
    keep(["mlp_w2_1"], carried)
    (dh1, d_mlp_g0, dh1_b), carried = _mm_nt(dz0, w["mlp_w1_0"], "col", tm=tb, name="d_mlp0_in", epi="normbwd",
                                             h=h1, g=mlp_g[0], dres=dh2, bf16_copy=True,
                                             carry=_merge(carried_exchange(["att_w_in", "att_w_out"]),
                                                          swap_halves(["mlp_w1_0", "mlp_w2_0"])))
    keep(["att_w_in", "att_w_out"], carried[:4])
    add_halves(["mlp_w1_0", "mlp_w2_0"], carried[4:])
    gw["ret_w_out"] = _mm_tn(y_ret, dh1_b, "row", tt=tt, tk=512, tn=256, name="d_ret_wout")
    dy_ret, carried = _mm_nt(dh1_b, w["ret_w_out"], "row", tm=tb, name="d_ret_y", epi="bf16", carry=swap_halves(["ret_w_out"]))
    add_halves(["ret_w_out"], carried)
    dproj, d_gn = _ret_bwd(proj, dy_ret, states, gn_g, consts, B=B, S=S)
    gw["ret_w_in"], carried = _mm_tn(hn0, dproj, "col", tt=tt, tk=D, tn=256, name="d_ret_win",
                                     carry=carried_exchange(["mlp_w1_0", "mlp_w2_0", "ret_w_out"]))
    keep(["mlp_w1_0", "mlp_w2_0", "ret_w_out"], carried)
    add_halves(["ret_w_in"], [gw["ret_w_in"]] + list(_exchange_with_sibling([gw["ret_w_in"]], "ret_in")))
    done = [n for n in BIG if n != "ret_w_in"]
    shards = {n: _add_chips(place, sums[n][1], landed[n], name="total_" + n) for n in done}
    (dx, d_mix_g0), carried = _mm_nt(dproj, w["ret_w_in"], "col", tm=tb, name="d_ret_in", epi="normbwd", h=h0, g=mix_g[0], dres=dh1,
                                     carry=_merge(carried_exchange(["ret_w_in"]), _share_carry([shards[n] for n in done])))
    keep(["ret_w_in"], carried[:2])
    shards.update(zip(done, carried[2:]))
    (shards["ret_w_in"],) = _share_with_sibling([_add_chips(place, sums["ret_w_in"][1], landed["ret_w_in"], name="total_ret_w_in")])
    small = [d_mix_g0, d_mix_g1, d_mlp_g0, d_mlp_g1, d_fin_g, d_gn.reshape(2, D), d_rel.reshape(5, D), loss]
    return dx.reshape(B, S, D), shards, small


def _row_tile(r, want=256):
    t = min(r, want)
    assert r % t == 0
    return t


def _into_slab(place, a, layer, dtype, name):
    _, r, c = a.shape
    tr = _row_tile(r)

    def body(place_ref, a_ref, o_ref):
        o_ref[...] = a_ref[...].astype(dtype)

    grid_spec = pltpu.PrefetchScalarGridSpec(
        num_scalar_prefetch=1, grid=(r // tr,), in_specs=[pl.BlockSpec((None, tr, c), lambda i, pr: (layer, i, 0))],
        out_specs=pl.BlockSpec((None, tr, c), lambda i, pr: (pr[0], i, 0)),
    )
    return pl.pallas_call(
        body, name=name, grid_spec=grid_spec, out_shape=jax.ShapeDtypeStruct((N_CHIPS, r, c), dtype), compiler_params=_params(1),
    )(place, a)


def _add_sibling(place, g, recv, name):
    _, r, c = g.shape
    hr = r // 2
    tr = _row_tile(hr)
    nrt = hr // tr

    def body(place_ref, g_ref, r_ref, sb_ref, own_ref):
        v = g_ref[...] + r_ref[...]
        sb_ref[...] = v.astype(BF16)

        @pl.when(pl.program_id(1) == place_ref[0])
        def _():
            own_ref[...] = v

    grid_spec = pltpu.PrefetchScalarGridSpec(
        num_scalar_prefetch=1, grid=(nrt, N_CHIPS),
        in_specs=[pl.BlockSpec((None, tr, c), lambda i, s, pr: (s, pr[1] * nrt + i, 0)),
                  pl.BlockSpec((None, tr, c), lambda i, s, pr: (s, i, 0))],
        out_specs=[pl.BlockSpec((None, tr, c), lambda i, s, pr: (s, i, 0)), pl.BlockSpec((tr, c), lambda i, s, pr: (i, 0))],
    )
    return pl.pallas_call(
        body, name=name, grid_spec=grid_spec,
        out_shape=[jax.ShapeDtypeStruct((N_CHIPS, hr, c), BF16), jax.ShapeDtypeStruct((hr, c), F32)],
        compiler_params=_params(2),
    )(place, g, recv)


def _add_chips(place, own, recv, name):
    hr, c = own.shape
    tr = _row_tile(hr)
    nrt = hr // tr

    def body(place_ref, o_ref, r_ref, t_ref):
        t_ref[...] = ((o_ref[...] + r_ref[0].astype(F32)) + r_ref[1].astype(F32)) + r_ref[2].astype(F32)

    grid_spec = pltpu.PrefetchScalarGridSpec(
        num_scalar_prefetch=1, grid=(nrt,),
        in_specs=[pl.BlockSpec((tr, c), lambda i, pr: (i, 0)), pl.BlockSpec((3, tr, c), lambda i, pr: (0, i, 0))],
        out_specs=pl.BlockSpec((tr, c), lambda i, pr: (pr[1] * nrt + i, 0)),
    )
    return pl.pallas_call(
        body, name=name, grid_spec=grid_spec, out_shape=jax.ShapeDtypeStruct((2 * hr, c), F32), compiler_params=_params(1),
    )(place, own, recv)


def _adamw(w, gs, m, v, name):
    L, r, c = w.shape
    tr = _row_tile(r, 128)
    assert len(gs) == L

    def body(*refs):
        w_ref, m_ref, v_ref = refs[:3]
        g_refs = refs[3:3 + L]
        go_ref, d_ref, nm_ref, nv_ref = refs[3 + L:]
        gg = g_refs[0][...]
        for k in range(1, L):
            gg = jnp.where(pl.program_id(0) == k, g_refs[k][...], gg)
        go_ref[...] = gg
        nm = ADAM_B1 * m_ref[...] + (1.0 - ADAM_B1) * gg
        nv = ADAM_B2 * v_ref[...] + (1.0 - ADAM_B2) * (gg * gg)
        m_hat = nm / (1.0 - ADAM_B1 ** ADAM_STEP)
        v_hat = nv / (1.0 - ADAM_B2 ** ADAM_STEP)
        d_ref[...] = -ADAM_LR * (m_hat / (jnp.sqrt(v_hat) + ADAM_EPS) + ADAM_WD * w_ref[...])
        nm_ref[...] = nm
        nv_ref[...] = nv

    spec = pl.BlockSpec((None, tr, c), lambda l, i: (l, i, 0))
    return pl.pallas_call(
        body, name=name, grid=(L, r // tr), in_specs=[spec] * 3 + [pl.BlockSpec((tr, c), lambda l, i: (i, 0))] * L,
        out_specs=[spec] * 4, out_shape=[jax.ShapeDtypeStruct((L, r, c), F32)] * 4, compiler_params=_params(2),
    )(w, m, v, *gs)


def _place():
    return lax.axis_index("x"), lax.axis_index("y"), lax.axis_index("c")


def _other_chips(x, y):
    return [(1 - x, y), (x, 1 - y), (1 - x, 1 - y)]


def _remote(src, dst, ssem, rsem, dev):
    return pltpu.make_async_remote_copy(src_ref=src, dst_ref=dst, send_sem=ssem, recv_sem=rsem, device_id=dev,
                                        device_id_type=MESH)


def _gather_phases(n):
    def geometry(refs, t):
        x, y, c = _place()
        hr = refs[t].shape[1] // 2
        chips = _other_chips(x, y)
        return x, y, c, 2 * x + y, chips, [2 * qx + qy for qx, qy in chips], pl.ds(c * hr, hr), pl.ds((1 - c) * hr, hr)

    def send(refs, sems):
        s1, r1, _, _ = sems
        for t in range(n):
            x, y, c, p, chips, cidx, mine, theirs = geometry(refs, t)
            for j, (qx, qy) in enumerate(chips):
                _remote(refs[t].at[p, mine], refs[t].at[p, mine], s1.at[t, j], r1.at[t, j], (qx, qy, c)).start()

    def pass_on(refs, sems):
        s1, r1, s2, r2 = sems
        for t in range(n):
            x, y, c, p, chips, cidx, mine, theirs = geometry(refs, t)
            for j, (qx, qy) in enumerate(chips):
                got = refs[t].at[cidx[j], mine]
                _remote(got, got, s1.at[t, j], r1.at[t, j], (qx, qy, c)).wait_recv()
                _remote(got, got, s2.at[t, j], r2.at[t, j], (x, y, 1 - c)).start()

    def finish(refs, sems):
        s1, r1, s2, r2 = sems
        for t in range(n):
            x, y, c, p, chips, cidx, mine, theirs = geometry(refs, t)
            for j, (qx, qy) in enumerate(chips):
                got = refs[t].at[cidx[j], theirs]
                _remote(got, got, s2.at[t, j], r2.at[t, j], (x, y, 1 - c)).wait_recv()
        for t in range(n):
            x, y, c, p, chips, cidx, mine, theirs = geometry(refs, t)
            for j, (qx, qy) in enumerate(chips):
                _remote(refs[t].at[p, mine], refs[t].at[p, mine], s1.at[t, j], r1.at[t, j], (qx, qy, c)).wait_send()
                sent = refs[t].at[cidx[j], mine]
                _remote(sent, sent, s2.at[t, j], r2.at[t, j], (x, y, 1 - c)).wait_send()

    sem = pltpu.SemaphoreType.DMA
    return send, pass_on, finish, [sem((n, 3)), sem((n, 3)), sem((n, 3)), sem((n, 3))]


def _gather_carry(slabs):
    send, pass_on, finish, sems = _gather_phases(len(slabs))
    return _Carry(slabs, sems, [(0, send), (-2, pass_on), (-1, finish)])


def _all_gather_slabs(slabs):
    n = len(slabs)
    send, pass_on, finish, sems = _gather_phases(n)

    def body(*refs):
        outs, scratch = refs[n:2 * n], refs[2 * n:]
        send(outs, scratch)
        pass_on(outs, scratch)
        finish(outs, scratch)

    return pl.pallas_call(
        body, name="gather_weights", in_specs=[HBM_SPEC] * n, out_specs=[HBM_SPEC] * n,
        out_shape=[jax.ShapeDtypeStruct(s.shape, s.dtype) for s in slabs], input_output_aliases={t: t for t in range(n)},
        scratch_shapes=sems, compiler_params=pltpu.CompilerParams(has_side_effects=True),
    )(*slabs)


def _sibling_phases(n):
    def copies(refs, sems):
        ssem, rsem = sems
        x, y, c = _place()
        out = []
        for t in range(n):
            hr = refs[t].shape[1] // 2
            out.append(_remote(refs[t].at[:, pl.ds((1 - c) * hr, hr), :], refs[n + t], ssem.at[t], rsem.at[t], (x, y, 1 - c)))
        return out

    def send(refs, sems):
        for cp in copies(refs, sems):
            cp.start()

    def finish(refs, sems):
        for cp in copies(refs, sems):
            cp.wait()

    sem = pltpu.SemaphoreType.DMA
    return send, finish, [sem((n,)), sem((n,))]


def _halves_landing(grads):
    return [lax.empty((N_CHIPS, g.shape[1] // 2, g.shape[2]), g.dtype) for g in grads]


def _sibling_carry(grads):
    send, finish, sems = _sibling_phases(len(grads))
    return _Carry(list(grads) + _halves_landing(grads), sems, [(0, send), (-1, finish)])


def _exchange_with_sibling(grads, tag):
    n = len(grads)
    send, finish, sems = _sibling_phases(n)

    def body(*refs):
        both, scratch = refs[:2 * n], refs[2 * n:]
        send(both, scratch)
        finish(both, scratch)

    return pl.pallas_call(
        body, name="grads_to_sibling_" + tag, in_specs=[HBM_SPEC] * n, out_specs=[HBM_SPEC] * n,
        out_shape=[jax.ShapeDtypeStruct((N_CHIPS, g.shape[1] // 2, g.shape[2]), g.dtype) for g in grads],
        scratch_shapes=sems, compiler_params=pltpu.CompilerParams(has_side_effects=True),
    )(*grads)


def _chips_phases(n):
    def copies(refs, sems):
        ssem, rsem = sems
        x, y, c = _place()
        return [_remote(refs[t].at[2 * qx + qy], refs[n + t].at[j], ssem.at[t, j], rsem.at[t, j], (qx, qy, c))
                for t in range(n) for j, (qx, qy) in enumerate(_other_chips(x, y))]

    def send(refs, sems):
        for cp in copies(refs, sems):
            cp.start()

    def finish(refs, sems):
        for cp in copies(refs, sems):
            cp.wait()

    sem = pltpu.SemaphoreType.DMA
    return send, finish, [sem((n, 3)), sem((n, 3))]


def _landing(sums):
    return [lax.empty((3,) + s.shape[1:], s.dtype) for s in sums]


def _chips_carry(sums):
    send, finish, sems = _chips_phases(len(sums))
    return _Carry(list(sums) + _landing(sums), sems, [(0, send), (-1, finish)])


def _share_phases(n):
    def halves(refs, t):
        x, y, c = _place()
        hr = refs[t].shape[0] // 2
        return refs[t].at[pl.ds(c * hr, hr)], refs[t].at[pl.ds((1 - c) * hr, hr)], (x, y, 1 - c)

    def send(refs, sems):
        ssem, rsem = sems
        for t in range(n):
            mine, _, sibling = halves(refs, t)
            _remote(mine, mine, ssem.at[t], rsem.at[t], sibling).start()

    def finish(refs, sems):
        ssem, rsem = sems
        for t in range(n):
            mine, theirs, sibling = halves(refs, t)
            _remote(theirs, theirs, ssem.at[t], rsem.at[t], sibling).wait_recv()
            _remote(mine, mine, ssem.at[t], rsem.at[t], sibling).wait_send()

    sem = pltpu.SemaphoreType.DMA
    return send, finish, [sem((n,)), sem((n,))]


def _share_carry(shards):
    send, finish, sems = _share_phases(len(shards))
    return _Carry(shards, sems, [(0, send), (-1, finish)])


def _share_with_sibling(shards):
    n = len(shards)
    send, finish, sems = _share_phases(n)

    def body(*refs):
        outs, scratch = refs[n:2 * n], refs[2 * n:]
        send(outs, scratch)
        finish(outs, scratch)

    return pl.pallas_call(
        body, name="grads_share", in_specs=[HBM_SPEC] * n, out_specs=[HBM_SPEC] * n,
        out_shape=[jax.ShapeDtypeStruct(s.shape, s.dtype) for s in shards], input_output_aliases={t: t for t in range(n)},
        scratch_shapes=sems, compiler_params=pltpu.CompilerParams(has_side_effects=True),
    )(*shards)


def _all_reduce_small(buf):
    R, C = buf.shape

    def body(in_ref, out_ref, gather, ssem, rsem):
        x, y, c = _place()
        me = 4 * x + 2 * y + c
        gather[me] = in_ref[...]
        flips = [(fx, fy, fc) for fx in (0, 1) for fy in (0, 1) for fc in (0, 1) if fx or fy or fc]
        peers = [(x + fx - 2 * x * fx, y + fy - 2 * y * fy, c + fc - 2 * c * fc) for fx, fy, fc in flips]
        copies = [_remote(in_ref, gather.at[me], ssem.at[k], rsem.at[k], peer) for k, peer in enumerate(peers)]
        for cp in copies:
            cp.start()
        for k, (px, py, pc) in enumerate(peers):
            _remote(in_ref, gather.at[4 * px + 2 * py + pc], ssem.at[k], rsem.at[k], (px, py, pc)).wait_recv()
        for cp in copies:
            cp.wait_send()
        acc = gather[0]
        for d in range(1, 8):
            acc = acc + gather[d]
        out_ref[...] = acc

    sem = pltpu.SemaphoreType.DMA
    vmem = pl.BlockSpec(memory_space=pltpu.VMEM)
    return pl.pallas_call(
        body, name="small_grads_sum", in_specs=[vmem], out_specs=vmem, out_shape=jax.ShapeDtypeStruct((R, C), F32),
        scratch_shapes=[pltpu.VMEM((8, R, C), F32), sem((7,)), sem((7,))],
        compiler_params=pltpu.CompilerParams(has_side_effects=True),
    )(buf)


BIG = ["ret_w_in", "ret_w_out", "att_w_in", "att_w_out", "mlp_w1_0", "mlp_w1_1", "mlp_w2_0", "mlp_w2_1"]
LAYER_OF = {"ret_w_in": ("ret_w_in", 0), "ret_w_out": ("ret_w_out", 0), "att_w_in": ("att_w_in", 0), "att_w_out": ("att_w_out", 0),
            "mlp_w1_0": ("mlp_w1", 0), "mlp_w1_1": ("mlp_w1", 1), "mlp_w2_0": ("mlp_w2", 0), "mlp_w2_1": ("mlp_w2", 1)}
ORDER = ["mix_norm_g", "ret_w_in", "ret_gn_g", "ret_w_out", "att_w_in", "att_rel_bias", "att_w_out", "mlp_norm_g", "mlp_w1", "mlp_w2",
         "final_norm_g"]


def kernel(x, mix_norm_g, ret_w_in, ret_gn_g, ret_w_out, att_w_in, att_rel_bias, att_w_out, mlp_norm_g, mlp_w1, mlp_w2, final_norm_g, loss_target, m_mix_norm_g, m_ret_w_in, m_ret_gn_g, m_ret_w_out, m_att_w_in, m_att_rel_bias, m_att_w_out, m_mlp_norm_g, m_mlp_w1, m_mlp_w2, m_final_norm_g, v_mix_norm_g, v_ret_w_in, v_ret_gn_g, v_ret_w_out, v_att_w_in, v_att_rel_bias, v_att_w_out, v_mlp_norm_g, v_mlp_w1, v_mlp_w2, v_final_norm_g):
    xi, yi, ci = _place()
    chip = 2 * xi + yi
    weights = dict(zip(ORDER, (mix_norm_g, ret_w_in, ret_gn_g, ret_w_out, att_w_in, att_rel_bias, att_w_out, mlp_norm_g, mlp_w1,
                               mlp_w2, final_norm_g)))
    first = dict(zip(ORDER, (m_mix_norm_g, m_ret_w_in, m_ret_gn_g, m_ret_w_out, m_att_w_in, m_att_rel_bias, m_att_w_out,
                             m_mlp_norm_g, m_mlp_w1, m_mlp_w2, m_final_norm_g)))
    second = dict(zip(ORDER, (v_mix_norm_g, v_ret_w_in, v_ret_gn_g, v_ret_w_out, v_att_w_in, v_att_rel_bias, v_att_w_out,
                              v_mlp_norm_g, v_mlp_w1, v_mlp_w2, v_final_norm_g)))

    place = jnp.stack([chip, ci]).astype(jnp.int32)
    slab = {n: _into_slab(place, weights[LAYER_OF[n][0]], LAYER_OF[n][1], BF16, name="cast_" + n) for n in BIG}
    slab_rel = _into_slab(place, att_rel_bias, 0, F32, name="slab_rel_bias")

    grad_x, g_big, small = _step(x, loss_target, slab, slab_rel, place, mix_norm_g, ret_gn_g[0], mlp_norm_g, final_norm_g)

    rows, at = jnp.zeros((16, D_MODEL), F32), 0
    for part in small:
        rows = rows + jnp.pad(part, ((at, 16 - at - part.shape[0]), (0, D_MODEL - part.shape[1])))
        at += part.shape[0]
    rows = _all_reduce_small(rows)
    loss = rows[12, 0]
    grads = {"mix_norm_g": [rows[0:2]], "mlp_norm_g": [rows[2:4]], "final_norm_g": [rows[4:5]], "ret_gn_g": [rows[5:7].reshape(1, 2048)],
             "att_rel_bias": [lax.dynamic_slice_in_dim(rows[7:12].reshape(ATT_HEADS, REL_TABLE), chip * (REL_TABLE // N_CHIPS),
                                                       REL_TABLE // N_CHIPS, axis=1)]}
    for n in BIG:
        grads.setdefault(LAYER_OF[n][0], []).append(g_big[n])

    def as3(a):
        return a.reshape((1,) * (3 - a.ndim) + a.shape)

    results = {}
    for n in ORDER:
        outs = _adamw(as3(weights[n]), grads[n], as3(first[n]), as3(second[n]), name="adamw_" + n)
        results[n] = [o.reshape(weights[n].shape) for o in outs]
    return (loss, grad_x) + tuple(results[n][k] for k in range(4) for n in ORDER)
```

```python
import math

import jax
import jax.numpy as jnp
from jax import lax
from jax.experimental import pallas as pl
from jax.experimental.pallas import tpu as pltpu

F32 = jnp.float32
BF16 = jnp.bfloat16
MESH = pl.DeviceIdType.MESH

D_MODEL = 1024
CHUNK = 64
RET_HEADS = 4
RET_DK = 256
RET_DV = 512
ROPE_BASE = 10000.0
ATT_HEADS = 16
ATT_DH = 64
PAST = 512
MAX_REL = 256
REL_TABLE = MAX_REL + CHUNK
EPS = 1e-6
NEG = -1e30
N_CHIPS = 4

ADAM_LR = 0.001
ADAM_B1 = 0.9
ADAM_B2 = 0.999
ADAM_EPS = 1e-08
ADAM_WD = 0.01
ADAM_STEP = 10

RET_BLOCK = 256
ATT_BLOCK = 256
VMEM_LIMIT = 56 * 1024 * 1024


def _params(n_axes, **kw):
    return pltpu.CompilerParams(dimension_semantics=("arbitrary",) * n_axes, vmem_limit_bytes=VMEM_LIMIT, **kw)


def _dot(a, b):
    return jnp.dot(a, b, preferred_element_type=F32)


def _dot_nt(a, b):
    return lax.dot_general(a, b, (((1,), (1,)), ((), ())), preferred_element_type=F32)


def _dot_tn(a, b):
    return lax.dot_general(a, b, (((0,), (0,)), ((), ())), preferred_element_type=F32)


def _sigmoid(x):
    return 0.5 * jnp.tanh(0.5 * x) + 0.5


HBM_SPEC = pl.BlockSpec(memory_space=pltpu.HBM)


class _Carry:
    def __init__(self, arrays, sems, stages):
        self.arrays, self.sems, self.stages = list(arrays), list(sems), list(stages)


def _merge(a, b):
    na, sa = len(a.arrays), len(a.sems)

    def of_a(fn):
        return lambda refs, sems: fn(refs[:na], sems[:sa])

    def of_b(fn):
        return lambda refs, sems: fn(refs[na:], sems[sa:])

    return _Carry(a.arrays + b.arrays, a.sems + b.sems,
                  [(at, of_a(fn)) for at, fn in a.stages] + [(at, of_b(fn)) for at, fn in b.stages])


def _carry_call(body, carry, *, name, grid, in_specs, out_specs, out_shape, args):
    if carry is None:
        outs = pl.pallas_call(body, name=name, grid=grid, in_specs=in_specs, out_specs=out_specs, out_shape=out_shape,
                              compiler_params=_params(len(grid)))(*args)
        return list(outs), []
    n_in, n_out, n_c = len(in_specs), len(out_specs), len(carry.arrays)
    steps = 1
    for g in grid:
        steps *= g
    assert all(-steps <= at < steps for at, _ in carry.stages)

    def carrying(*refs):
        ins, outs = refs[:n_in], refs[n_in + n_c:n_in + n_c + n_out]
        carried = refs[n_in + n_c + n_out:n_in + 2 * n_c + n_out]
        sems = refs[n_in + 2 * n_c + n_out:]
        step = pl.program_id(0)
        for axis in range(1, len(grid)):
            step = step * grid[axis] + pl.program_id(axis)
        for at, fn in carry.stages:
            if at == 0:
                pl.when(step == 0)(lambda fn=fn: fn(carried, sems))
        body(*ins, *outs)
        for at, fn in carry.stages:
            if at != 0:
                pl.when(step == at % steps)(lambda fn=fn: fn(carried, sems))

    outs = pl.pallas_call(
        carrying, name=name, grid=grid, in_specs=list(in_specs) + [HBM_SPEC] * n_c, out_specs=list(out_specs) + [HBM_SPEC] * n_c,
        out_shape=list(out_shape) + [jax.ShapeDtypeStruct(a.shape, a.dtype) for a in carry.arrays],
        input_output_aliases={n_in + t: n_out + t for t in range(n_c)}, scratch_shapes=carry.sems,
        compiler_params=_params(len(grid), has_side_effects=True),
    )(*args, *carry.arrays)
    return list(outs[:n_out]), list(outs[n_out:])


def _mm_nn(a, w, wkind, *, tm, out_dtype, name, norm_g=None, act=None, square=False, res=None, loss=None, carry=None):
    M, K = a.shape
    cols = w.shape[2]
    N = N_CHIPS * cols if wkind == "col" else cols
    has_norm = norm_g is not None
    steps = M // tm
    assert M % tm == 0 and K == (w.shape[1] if wkind == "col" else N_CHIPS * w.shape[1])
    assert loss is None or (wkind == "row" and act is None)

    def body(*refs):
        it = iter(refs)
        a_ref, w_ref = next(it), next(it)
        g_ref = next(it) if has_norm else None
        r_ref = next(it) if res is not None else None
        if loss is not None:
            t_ref, fg_ref = next(it), next(it)
        o_ref = next(it)
        hn_ref = next(it) if has_norm else None
        if loss is not None:
            dg_ref, sq_ref, loss_ref, o16_ref = next(it), next(it), next(it), next(it)
        if has_norm:
            x = a_ref[...].astype(F32)
            r = lax.rsqrt(jnp.mean(x * x, axis=-1, keepdims=True) + EPS)
            lhs = (x * r * g_ref[...]).astype(BF16)
            hn_ref[...] = lhs
        elif square:
            lhs = a_ref[...].astype(BF16)
            lhs = lhs * lhs
        else:
            lhs = a_ref[...].astype(BF16)

        def loss_head(x):
            i = pl.program_id(0)
            r = lax.rsqrt(jnp.mean(x * x, axis=-1, keepdims=True) + EPS)
            xh = x * r
            gg = fg_ref[...]
            diff = xh * gg - t_ref[...]
            sq = jnp.sum(diff * diff, axis=0, keepdims=True)
            dy = diff * (1.0 / N)
            dg_part = jnp.sum(dy * xh, axis=0, keepdims=True)

            @pl.when(i == 0)
            def _():
                sq_ref[...] = sq
                dg_ref[...] = dg_part

            @pl.when(i > 0)
            def _():
                sq_ref[...] += sq
                dg_ref[...] += dg_part
            t = dy * gg
            d = r * (t - xh * jnp.mean(t * xh, axis=-1, keepdims=True))
            o_ref[...] = d
            o16_ref[...] = d.astype(BF16)

            @pl.when(i == steps - 1)
            def _():
                loss_ref[...] = (0.5 / N) * jnp.sum(sq_ref[...], axis=1, keepdims=True)

        def finish(acc, sl):
            if act == "relu":
                acc = jnp.maximum(acc, 0.0)
            if r_ref is not None:
                acc = acc + r_ref[:, sl]
            if loss is not None:
                loss_head(acc)
            else:
                o_ref[:, sl] = acc.astype(out_dtype)

        if wkind == "col":
            for s in range(N_CHIPS):
                finish(_dot(lhs, w_ref[s]), slice(s * cols, (s + 1) * cols))
        else:
            finish(_dot(lhs, w_ref[...].reshape(K, N)), slice(None))

    in_specs = [pl.BlockSpec((tm, K), lambda i: (i, 0)), pl.BlockSpec(w.shape, lambda i: (0, 0, 0))]
    args = [a, w]
    if has_norm:
        in_specs.append(pl.BlockSpec((1, K), lambda i: (0, 0)))
        args.append(norm_g.reshape(1, K))
    if res is not None:
        in_specs.append(pl.BlockSpec((tm, N), lambda i: (i, 0)))
        args.append(res)
    if loss is not None:
        in_specs += [pl.BlockSpec((tm, N), lambda i: (i, 0)), pl.BlockSpec((1, N), lambda i: (0, 0))]
        args += [loss[0], loss[1].reshape(1, N)]
    out_shape = [jax.ShapeDtypeStruct((M, N), out_dtype)]
    out_specs = [pl.BlockSpec((tm, N), lambda i: (i, 0))]
    if has_norm:
        out_shape.append(jax.ShapeDtypeStruct((M, K), BF16))
        out_specs.append(pl.BlockSpec((tm, K), lambda i: (i, 0)))
    if loss is not None:
        out_shape += [jax.ShapeDtypeStruct((1, N), F32), jax.ShapeDtypeStruct((1, N), F32), jax.ShapeDtypeStruct((1, 1), F32),
                      jax.ShapeDtypeStruct((M, N), BF16)]
        out_specs += [pl.BlockSpec((1, N), lambda i: (0, 0)), pl.BlockSpec((1, N), lambda i: (0, 0)),
                      pl.BlockSpec((1, 1), lambda i: (0, 0)), pl.BlockSpec((tm, N), lambda i: (i, 0))]
    outs, carried = _carry_call(body, carry, name=name, grid=(steps,), in_specs=in_specs, out_specs=out_specs,
                                out_shape=out_shape, args=args)
    result = outs if has_norm or loss is not None else outs[0]
    return result if carry is None else (result, carried)


def _mm_nt(a, w, wkind, *, tm, name, epi, a2=None, h=None, g=None, dres=None, bf16_copy=False, carry=None):
    parts = tuple(a) if isinstance(a, (tuple, list)) else (a,)
    M, part_w = parts[0].shape
    Nw = part_w * len(parts)
    rows, cols = w.shape[1], w.shape[2]
    Kw = rows if wkind == "col" else N_CHIPS * rows
    assert M % tm == 0 and Nw == (N_CHIPS * cols if wkind == "col" else cols)
    assert epi != "normbwd" or wkind == "col"
    assert len(parts) == 1 or wkind == "col"
    chunk = math.gcd(part_w, cols)

    def body(*refs):
        it = iter(refs)
        a_refs = [next(it) for _ in parts]
        a_ref, w_ref = a_refs[0], next(it)
        a2_ref = next(it) if epi == "relu2bwd" else None
        if epi == "normbwd":
            h_ref, g_ref, dres_ref = next(it), next(it), next(it)
        o_ref = next(it)
        dg_ref = next(it) if epi == "normbwd" else None
        o16_ref = next(it) if bf16_copy else None
        i = pl.program_id(0)

        def finish(acc, sl):
            if epi == "bf16":
                o_ref[:, sl] = acc.astype(BF16)
            elif epi == "relu2bwd":
                o_ref[:, sl] = (acc * (2.0 * a2_ref[:, sl].astype(F32))).astype(BF16)
            else:
                x = h_ref[...]
                r = lax.rsqrt(jnp.mean(x * x, axis=-1, keepdims=True) + EPS)
                xh = x * r
                dg_part = jnp.sum(acc * xh, axis=0, keepdims=True)

                @pl.when(i == 0)
                def _():
                    dg_ref[...] = dg_part

                @pl.when(i > 0)
                def _():
                    dg_ref[...] += dg_part
                t = acc * g_ref[...]
                d = dres_ref[...] + r * (t - xh * jnp.mean(t * xh, axis=-1, keepdims=True))
                o_ref[...] = d
                if bf16_copy:
                    o16_ref[...] = d.astype(BF16)

        if wkind == "col":
            acc = None
            for lo in range(0, Nw, chunk):
                (src, a_lo), (s, w_lo) = divmod(lo, part_w), divmod(lo, cols)
                part = _dot_nt(a_refs[src][:, a_lo:a_lo + chunk].astype(BF16), w_ref[s, :, w_lo:w_lo + chunk])
                acc = part if acc is None else acc + part
            finish(acc, slice(None))
        else:
            lhs = a_ref[...].astype(BF16)
            for s in range(N_CHIPS):
                finish(_dot_nt(lhs, w_ref[s]), slice(s * rows, (s + 1) * rows))

    in_specs = [pl.BlockSpec((tm, part_w), lambda i: (i, 0)) for _ in parts] + [pl.BlockSpec(w.shape, lambda i: (0, 0, 0))]
    args = [*parts, w]
    out_dtype = BF16
    if epi == "relu2bwd":
        in_specs.append(pl.BlockSpec((tm, Kw), lambda i: (i, 0)))
        args.append(a2)
    if epi == "normbwd":
        in_specs += [pl.BlockSpec((tm, Kw), lambda i: (i, 0)), pl.BlockSpec((1, Kw), lambda i: (0, 0)),
                     pl.BlockSpec((tm, Kw), lambda i: (i, 0))]
        args += [h, g.reshape(1, Kw), dres]
        out_dtype = F32
    out_shape = [jax.ShapeDtypeStruct((M, Kw), out_dtype)]
    out_specs = [pl.BlockSpec((tm, Kw), lambda i: (i, 0))]
    if epi == "normbwd":
        out_shape.append(jax.ShapeDtypeStruct((1, Kw), F32))
        out_specs.append(pl.BlockSpec((1, Kw), lambda i: (0, 0)))
    assert not bf16_copy or epi == "normbwd"
    if bf16_copy:
        out_shape.append(jax.ShapeDtypeStruct((M, Kw), BF16))
        out_specs.append(pl.BlockSpec((tm, Kw), lambda i: (i, 0)))
    outs, carried = _carry_call(body, carry, name=name, grid=(M // tm,), in_specs=in_specs, out_specs=out_specs,
                                out_shape=out_shape, args=args)
    result = outs if epi == "normbwd" else outs[0]
    return result if carry is None else (result, carried)


def _mm_tn(a, b, okind, *, tt, tk, tn, name, square=False, carry=None):
    parts = tuple(b) if isinstance(b, (tuple, list)) else (b,)
    T, K = a.shape
    part_w = parts[0].shape[1]
    N = part_w * len(parts)
    assert T % tt == 0 and K % tk == 0 and part_w % tn == 0
    nt = T // tt
    per_part = part_w // tn
    if okind == "col":
        per = (N // N_CHIPS) // tn
        assert (N // N_CHIPS) % tn == 0
        out_shape = jax.ShapeDtypeStruct((N_CHIPS, K, N // N_CHIPS), F32)
        out_spec = pl.BlockSpec((None, tk, tn), lambda ki, nj, t: (nj // per, ki, nj % per))
    else:
        per = (K // N_CHIPS) // tk
        assert (K // N_CHIPS) % tk == 0
        out_shape = jax.ShapeDtypeStruct((N_CHIPS, K // N_CHIPS, N), F32)
        out_spec = pl.BlockSpec((None, tk, tn), lambda ki, nj, t: (ki // per, ki % per, nj))

    def body(a_ref, *refs):
        b_refs, o_ref = refs[:-1], refs[-1]
        nj, t = pl.program_id(1), pl.program_id(2)

        def accumulate(b_ref):
            lhs = a_ref[...].astype(BF16)
            part = _dot_tn(lhs * lhs if square else lhs, b_ref[...].astype(BF16))
            if nt == 1:
                o_ref[...] = part
            else:
                @pl.when(t == 0)
                def _():
                    o_ref[...] = part

                @pl.when(t > 0)
                def _():
                    o_ref[...] += part

        if len(parts) == 1:
            accumulate(b_refs[0])
        else:
            for s, b_ref in enumerate(b_refs):
                pl.when(nj // per_part == s)(lambda b_ref=b_ref: accumulate(b_ref))

    def b_spec(s):
        return pl.BlockSpec((tt, tn), lambda ki, nj, t: (t, jnp.clip(nj - s * per_part, 0, per_part - 1)))

    outs, carried = _carry_call(
        body, carry, name=name, grid=(K // tk, N // tn, nt),
        in_specs=[pl.BlockSpec((tt, tk), lambda ki, nj, t: (t, ki))] + [b_spec(s) for s in range(len(parts))],
        out_specs=[out_spec], out_shape=[out_shape], args=[a, *parts])
    return outs[0] if carry is None else (outs[0], carried)


def _ret_consts(S, LB):
    log_gamma = jnp.log1p(-jnp.exp2(-5.0 - jnp.arange(RET_HEADS, dtype=F32)))
    idx = jnp.arange(LB, dtype=F32)
    n, m = idx[:, None], idx[None, :]
    cn, cm = jnp.floor(n / CHUNK), jnp.floor(m / CHUNK)
    dist = jnp.where(cm == cn, jnp.abs(n - m), n - m)
    dmat = jnp.where((cm <= cn)[None], jnp.exp(log_gamma[:, None, None] * dist[None]), 0.0)
    qd = jnp.exp(log_gamma[:, None] * (idx + 1.0)[None, :])[..., None]
    kd = jnp.exp(log_gamma[:, None] * (LB - 1 - idx)[None, :])[..., None]
    bd = jnp.exp(log_gamma * LB).reshape(RET_HEADS, 1, 1) * jnp.ones((RET_HEADS, 1, 128), F32)
    half = RET_DK // 2
    inv = jnp.exp(-jnp.log(ROPE_BASE) * jnp.arange(half, dtype=F32) / half)
    ang = jnp.arange(S, dtype=F32)[:, None] * inv[None, :]
    return dmat.astype(F32), qd.astype(F32), kd.astype(F32), bd, jnp.cos(ang), jnp.sin(ang)


def _rope(t, c, s):
    t1, t2 = t[:, :128], t[:, 128:]
    return jnp.concatenate([t1 * c - t2 * s, t1 * s + t2 * c], axis=-1)


def _rope_inv(d, c, s):
    d1, d2 = d[:, :128], d[:, 128:]
    return jnp.concatenate([d1 * c + d2 * s, d2 * c - d1 * s], axis=-1)


def _ret_block_fwd(p_ref, h, c, s, d_ref, qd_ref, kd_ref, stb):
    q = _rope(p_ref[:, h * RET_DK:(h + 1) * RET_DK].astype(F32), c, s)
    k = _rope(p_ref[:, 1024 + h * RET_DK:1024 + (h + 1) * RET_DK].astype(F32), c, s) * (RET_DK ** -0.5)
    v = p_ref[:, 2048 + h * RET_DV:2048 + (h + 1) * RET_DV]
    qb, kb = q.astype(BF16), k.astype(BF16)
    scb = (_dot_nt(qb, kb) * d_ref[h]).astype(BF16)
    o = _dot(scb, v) + qd_ref[h] * _dot(qb, stb)
    return q, k, qb, kb, v, scb, o


def _ret_fwd(proj, gn_g, consts, *, B, S):
    LB = RET_BLOCK
    nb = S // LB
    T = B * S
    dmat, qd, kd, bd, cos, sin = consts

    def body(p_ref, cos_ref, sin_ref, d_ref, qd_ref, kd_ref, bd_ref, gng_ref, y_ref, st_ref, state_s):
        i = pl.program_id(1)

        @pl.when(i == 0)
        def _():
            state_s[...] = jnp.zeros_like(state_s)
        c, s = cos_ref[...], sin_ref[...]
        for h in range(RET_HEADS):
            st = state_s[h]
            stb = st.astype(BF16)
            st_ref[h] = stb
            q, k, qb, kb, v, scb, o = _ret_block_fwd(p_ref, h, c, s, d_ref, qd_ref, kd_ref, stb)
            kdk = (k * kd_ref[h]).astype(BF16)
            state_s[h] = st * bd_ref[h][:, :1] + _dot_tn(kdk, v)
            gate = p_ref[:, 4096 + h * RET_DV:4096 + (h + 1) * RET_DV].astype(F32)
            mu = jnp.mean(o, axis=-1, keepdims=True)
            oc = o - mu
            xh = oc * lax.rsqrt(jnp.mean(oc * oc, axis=-1, keepdims=True) + EPS)
            y = (gate * _sigmoid(gate)) * (xh * gng_ref[:, h * RET_DV:(h + 1) * RET_DV])
            y_ref[:, h * RET_DV:(h + 1) * RET_DV] = y.astype(BF16)

    const = lambda b, i: (0, 0, 0)
    return pl.pallas_call(
        body, name="ret_fwd", grid=(B, nb),
        in_specs=[pl.BlockSpec((LB, 6144), lambda b, i: (b * nb + i, 0)),
                  pl.BlockSpec((LB, 128), lambda b, i: (i, 0)), pl.BlockSpec((LB, 128), lambda b, i: (i, 0)),
                  pl.BlockSpec((RET_HEADS, LB, LB), const), pl.BlockSpec((RET_HEADS, LB, 1), const),
                  pl.BlockSpec((RET_HEADS, LB, 1), const), pl.BlockSpec((RET_HEADS, 1, 128), const),
                  pl.BlockSpec((1, 2048), lambda b, i: (0, 0))],
        out_specs=[pl.BlockSpec((LB, 2048), lambda b, i: (b * nb + i, 0)),
                   pl.BlockSpec((None, None, RET_HEADS, RET_DK, RET_DV), lambda b, i: (b, i, 0, 0, 0))],
        out_shape=[jax.ShapeDtypeStruct((T, 2048), BF16), jax.ShapeDtypeStruct((B, nb, RET_HEADS, RET_DK, RET_DV), BF16)],
        scratch_shapes=[pltpu.VMEM((RET_HEADS, RET_DK, RET_DV), F32)], compiler_params=_params(2),
    )(proj, cos, sin, dmat, qd, kd, bd, gn_g.reshape(1, 2048))


def _ret_bwd(proj, dy, states, gn_g, consts, *, B, S):
    LB = RET_BLOCK
    nb = S // LB
    T = B * S
    dmat, qd, kd, bd, cos, sin = consts

    def body(p_ref, dy_ref, st_ref, cos_ref, sin_ref, d_ref, qd_ref, kd_ref, bd_ref, gng_ref, dp_ref, dgn_ref, dstate_s):
        b, i = pl.program_id(0), pl.program_id(1)

        @pl.when(i == 0)
        def _():
            dstate_s[...] = jnp.zeros_like(dstate_s)

        @pl.when((b == 0) & (i == 0))
        def _():
            dgn_ref[...] = jnp.zeros_like(dgn_ref)
        c, s = cos_ref[...], sin_ref[...]
        for h in range(RET_HEADS):
            vs = slice(h * RET_DV, (h + 1) * RET_DV)
            stb = st_ref[h]
            q, k, qb, kb, v, scb, o = _ret_block_fwd(p_ref, h, c, s, d_ref, qd_ref, kd_ref, stb)
            gng = gng_ref[:, vs]
            do_halves = []
            for rows in (slice(0, LB // 2), slice(LB // 2, LB)):
                gate = p_ref[rows, 4096 + h * RET_DV:4096 + (h + 1) * RET_DV].astype(F32)
                o_r = o[rows]
                oc = o_r - jnp.mean(o_r, axis=-1, keepdims=True)
                rstd = lax.rsqrt(jnp.mean(oc * oc, axis=-1, keepdims=True) + EPS)
                xh = oc * rstd
                dyh = dy_ref[rows, vs].astype(F32)
                sg = _sigmoid(gate)
                silu = gate * sg
                dgn_ref[:, vs] += jnp.sum(dyh * silu * xh, axis=0, keepdims=True)
                dxh = dyh * silu * gng
                do = rstd * (dxh - jnp.mean(dxh, axis=-1, keepdims=True) - xh * jnp.mean(dxh * xh, axis=-1, keepdims=True))
                dp_ref[rows, 4096 + h * RET_DV:4096 + (h + 1) * RET_DV] = (
                    dyh * xh * gng * (sg * (1.0 + gate * (1.0 - sg)))).astype(BF16)
                do_halves.append(do.astype(BF16))
            dob = jnp.concatenate(do_halves, axis=0)
            dsb = (_dot_nt(dob, v) * d_ref[h]).astype(BF16)
            dst = dstate_s[h]
            dstb = dst.astype(BF16)
            kdk = (k * kd_ref[h]).astype(BF16)
            dqr = _dot(dsb, kb) + qd_ref[h] * _dot_nt(dob, stb)
            dkr = _dot_tn(dsb, qb) + kd_ref[h] * _dot_nt(v, dstb)
            dv = _dot_tn(scb, dob) + _dot(kdk, dstb)
            dstate_s[h] = dst * bd_ref[h][:, :1] + _dot_tn((q * qd_ref[h]).astype(BF16), dob)
            dp_ref[:, h * RET_DK:(h + 1) * RET_DK] = _rope_inv(dqr, c, s).astype(BF16)
            dp_ref[:, 1024 + h * RET_DK:1024 + (h + 1) * RET_DK] = (_rope_inv(dkr, c, s) * (RET_DK ** -0.5)).astype(BF16)
            dp_ref[:, 2048 + h * RET_DV:2048 + (h + 1) * RET_DV] = dv.astype(BF16)

    const = lambda b, i: (0, 0, 0)
    rev = lambda b, i: (b * nb + nb - 1 - i, 0)
    return pl.pallas_call(
        body, name="ret_bwd", grid=(B, nb),
        in_specs=[pl.BlockSpec((LB, 6144), rev), pl.BlockSpec((LB, 2048), rev),
                  pl.BlockSpec((None, None, RET_HEADS, RET_DK, RET_DV), lambda b, i: (b, nb - 1 - i, 0, 0, 0)),
                  pl.BlockSpec((LB, 128), lambda b, i: (nb - 1 - i, 0)), pl.BlockSpec((LB, 128), lambda b, i: (nb - 1 - i, 0)),
                  pl.BlockSpec((RET_HEADS, LB, LB), const), pl.BlockSpec((RET_HEADS, LB, 1), const),
                  pl.BlockSpec((RET_HEADS, LB, 1), const), pl.BlockSpec((RET_HEADS, 1, 128), const),
                  pl.BlockSpec((1, 2048), lambda b, i: (0, 0))],
        out_specs=[pl.BlockSpec((LB, 6144), rev), pl.BlockSpec((1, 2048), lambda b, i: (0, 0))],
        out_shape=[jax.ShapeDtypeStruct((T, 6144), BF16), jax.ShapeDtypeStruct((1, 2048), F32)],
        scratch_shapes=[pltpu.VMEM((RET_HEADS, RET_DK, RET_DV), F32)], compiler_params=_params(2),
    )(proj, dy, states, cos, sin, dmat, qd, kd, bd, gn_g.reshape(1, 2048))


BIAS_LANES = 4 * ATT_BLOCK


def _diag_onehot():
    r = lax.broadcasted_iota(jnp.int32, (REL_TABLE, BIAS_LANES), 0)
    j = lax.broadcasted_iota(jnp.int32, (REL_TABLE, BIAS_LANES), 1)
    idx = jnp.maximum(j - ATT_BLOCK - PAST, -MAX_REL) + MAX_REL
    return jnp.where(idx == r, 1.0, 0.0).astype(F32)


def _row_is(j):
    return lax.broadcasted_iota(jnp.int32, (8, BIAS_LANES), 0) == j


def _att_bias(table):
    QB, KW = ATT_BLOCK, 3 * ATT_BLOCK

    def body(t_ref, bt_ref):
        row = jnp.broadcast_to(t_ref[...], (8, REL_TABLE))
        diag = jnp.dot(row, _diag_onehot(), preferred_element_type=F32, precision=lax.Precision.HIGHEST)
        rows = jnp.zeros((8, BIAS_LANES), F32)
        for j in range(8):
            rows = jnp.where(_row_is(j), diag if j == 0 else pltpu.roll(diag, j, axis=1), rows)
        n = 8
        while n < QB:
            rows = jnp.concatenate([rows, pltpu.roll(rows, n, axis=1)], axis=0)
            n *= 2
        bias = rows[:, QB:]
        qi = lax.broadcasted_iota(jnp.int32, (QB, KW), 0)
        kj = lax.broadcasted_iota(jnp.int32, (QB, KW), 1)
        lo = (qi // CHUNK) * CHUNK
        bt_ref[...] = jnp.where((kj >= lo) & (kj < lo + PAST + CHUNK), bias, NEG).T

    return pl.pallas_call(
        body, name="att_bias", grid=(ATT_HEADS,),
        in_specs=[pl.BlockSpec((None, 1, REL_TABLE), lambda h: (h, 0, 0))],
        out_specs=pl.BlockSpec((None, KW, QB), lambda h: (h // 2, 0, h % 2)),
        out_shape=jax.ShapeDtypeStruct((ATT_HEADS // 2, KW, 2 * QB), F32),
        compiler_params=_params(1),
    )(table.reshape(ATT_HEADS, 1, REL_TABLE))


def _att_bias_grad(dbias_t):
    QB, KW = ATT_BLOCK, 3 * ATT_BLOCK

    def body(d_ref, o_ref):
        rows = jnp.concatenate([jnp.zeros((QB, QB), F32), d_ref[...].T], axis=1)
        n = QB // 2
        while n >= 8:
            rows = rows[:n] + pltpu.roll(rows[n:], BIAS_LANES - n, axis=1)
            n //= 2
        acc = jnp.zeros((8, BIAS_LANES), F32)
        for j in range(8):
            acc = acc + jnp.where(_row_is(j), rows if j == 0 else pltpu.roll(rows, BIAS_LANES - j, axis=1), 0.0)
        diag = jnp.broadcast_to(jnp.sum(acc, axis=0, keepdims=True), (8, BIAS_LANES))
        grad = lax.dot_general(diag, _diag_onehot(), (((1,), (1,)), ((), ())), preferred_element_type=F32,
                               precision=lax.Precision.HIGHEST)
        o_ref[...] = grad[:1]

    return pl.pallas_call(
        body, name="att_bias_grad", grid=(ATT_HEADS,),
        in_specs=[pl.BlockSpec((None, KW, QB), lambda h: (h // 2, 0, h % 2))],
        out_specs=pl.BlockSpec((None, 1, REL_TABLE), lambda h: (h, 0, 0)),
        out_shape=jax.ShapeDtypeStruct((ATT_HEADS, 1, REL_TABLE), F32), compiler_params=_params(1),
    )(dbias_t).reshape(ATT_HEADS, REL_TABLE)


def _by_head(x):
    first = lax.broadcasted_iota(jnp.int32, x.shape, 1) < ATT_DH
    zero = jnp.zeros_like(x)
    return jnp.concatenate([jnp.where(first, x, zero), jnp.where(first, zero, x)], axis=0)


def _att_fwd(qkv, bias_t, *, B, S):
    QB = ATT_BLOCK
    nb = S // QB
    KW = 3 * QB
    T = B * S
    scale = ATT_DH ** -0.5

    def body(q_ref, k0, k1, k2, v0, v1, v2, b_ref, o_ref, lse_ref, s_blk):
        i = pl.program_id(2)
        q2 = _by_head((q_ref[...].astype(F32) * scale).astype(BF16))
        m = jnp.full((1, 2 * QB), NEG, F32)
        for d, k_ref in enumerate((k0, k1, k2)):
            st = _dot_nt(k_ref[...], q2) + b_ref[d * QB:(d + 1) * QB, :]
            st = jnp.where(i + d >= 2, st, NEG)
            s_blk[d] = st
            m = jnp.maximum(m, jnp.max(st, axis=0, keepdims=True))
        l = jnp.zeros((1, 2 * QB), F32)
        o_t = jnp.zeros((128, 2 * QB), F32)
        for d, v_ref in enumerate((v0, v1, v2)):
            e = jnp.exp(s_blk[d] - m)
            l = l + jnp.sum(e, axis=0, keepdims=True)
            o_t = o_t + _dot(v_ref[...].astype(F32).T.astype(BF16), e.astype(BF16))
        o_t = o_t / l
        row = lax.broadcasted_iota(jnp.int32, (128, QB), 0)
        o_ref[...] = jnp.where(row < ATT_DH, o_t[:, :QB], o_t[:, QB:]).T.astype(BF16)
        lse = m + jnp.log(l)
        row8 = lax.broadcasted_iota(jnp.int32, (8, QB), 0)
        lse_ref[...] = jnp.where(row8 == 0, lse[:, :QB], jnp.where(row8 == 1, lse[:, QB:], 0.0))

    def kv(d, col0):
        return pl.BlockSpec((QB, 128), lambda hp, b, i: (b * nb + jnp.maximum(i - d, 0), col0 + hp))

    return pl.pallas_call(
        body, name="att_fwd", grid=(8, B, nb),
        in_specs=[pl.BlockSpec((QB, 128), lambda hp, b, i: (b * nb + i, hp)),
                  kv(2, 8), kv(1, 8), kv(0, 8), kv(2, 16), kv(1, 16), kv(0, 16),
                  pl.BlockSpec((None, KW, 2 * QB), lambda hp, b, i: (hp, 0, 0))],
        out_specs=[pl.BlockSpec((QB, 128), lambda hp, b, i: (b * nb + i, hp)),
                   pl.BlockSpec((None, 8, QB), lambda hp, b, i: (hp, 0, b * nb + i))],
        out_shape=[jax.ShapeDtypeStruct((T, 1024), BF16), jax.ShapeDtypeStruct((8, 8, T), F32)],
        scratch_shapes=[pltpu.VMEM((3, QB, 2 * QB), F32)], compiler_params=_params(3),
    )(qkv, qkv, qkv, qkv, qkv, qkv, qkv, bias_t)


def _att_bwd(qkv, do, o, lse, bias_t, *, B, S):
    QB = ATT_BLOCK
    nb = S // QB
    KW = 3 * QB
    T = B * S
    scale = ATT_DH ** -0.5
    TK = 256

    def body(q_ref, k0, k1, k2, v0, v1, v2, do_ref, o_ref, lse_ref, b_ref, dq_ref, dk_ref, dv_ref, db_ref, dk_acc, dv_acc):
        b, i = pl.program_id(1), pl.program_id(2)

        @pl.when(i == 0)
        def _():
            dk_acc[...] = jnp.zeros_like(dk_acc)
            dv_acc[...] = jnp.zeros_like(dv_acc)

        @pl.when((b == 0) & (i == 0))
        def _():
            db_ref[...] = jnp.zeros_like(db_ref)

        def write_out():
            slot = (i + 1) % 3
            dk_ref[...] = dk_acc[slot].astype(BF16)
            dv_ref[...] = dv_acc[slot].astype(BF16)
            dk_acc[slot] = jnp.zeros((QB, 128), F32)
            dv_acc[slot] = jnp.zeros((QB, 128), F32)

        @pl.when(i < nb)
        def _():
            dout = do_ref[...]
            q2 = _by_head((q_ref[...].astype(F32) * scale).astype(BF16))
            do2 = _by_head(dout)
            delta_t = (o_ref[...].astype(F32) * dout.astype(F32)).T
            delta2 = jnp.concatenate([jnp.sum(delta_t[:ATT_DH], axis=0, keepdims=True),
                                      jnp.sum(delta_t[ATT_DH:], axis=0, keepdims=True)], axis=1)
            lse2 = jnp.concatenate([lse_ref[0:1, :], lse_ref[1:2, :]], axis=1)
            dq_t = jnp.zeros((128, 2 * QB), F32)
            for d, (k_ref, v_ref) in enumerate(((k0, v0), (k1, v1), (k2, v2))):
                kblk, vblk = k_ref[...], v_ref[...]
                kt = kblk.astype(F32).T.astype(BF16)
                lse_d = jnp.where(i + d >= 2, lse2, -NEG)
                slot = (i + 1 + d) % 3
                for t in range(QB // TK):
                    rows = slice(t * TK, (t + 1) * TK)
                    wrows = slice(d * QB + t * TK, d * QB + (t + 1) * TK)
                    p = jnp.exp(_dot_nt(kblk[rows], q2) + b_ref[wrows, :] - lse_d)
                    ds = p * (_dot_nt(vblk[rows], do2) - delta2)
                    db_ref[wrows, :] += ds
                    dsb = ds.astype(BF16)
                    dk_acc[slot, rows, :] += _dot(dsb, q2)
                    dv_acc[slot, rows, :] += _dot(p.astype(BF16), do2)
                    dq_t += _dot(kt[:, rows], dsb)
            row = lax.broadcasted_iota(jnp.int32, (128, QB), 0)
            dq_ref[...] = (jnp.where(row < ATT_DH, dq_t[:, :QB], dq_t[:, QB:]) * scale).T.astype(BF16)
            write_out()

        pl.when(i >= nb)(write_out)

    def qrow(b, i):
        return b * nb + jnp.minimum(i, nb - 1)

    def kv(d, col0):
        return pl.BlockSpec((QB, 128), lambda hp, b, i: (b * nb + jnp.maximum(jnp.minimum(i, nb - 1) - d, 0), col0 + hp))

    late = pl.BlockSpec((QB, 128), lambda hp, b, i: (b * nb + jnp.maximum(i - 2, 0), hp))
    return pl.pallas_call(
        body, name="att_bwd", grid=(8, B, nb + 2),
        in_specs=[pl.BlockSpec((QB, 128), lambda hp, b, i: (qrow(b, i), hp)),
                  kv(2, 8), kv(1, 8), kv(0, 8), kv(2, 16), kv(1, 16), kv(0, 16),
                  pl.BlockSpec((QB, 128), lambda hp, b, i: (qrow(b, i), hp)),
                  pl.BlockSpec((QB, 128), lambda hp, b, i: (qrow(b, i), hp)),
                  pl.BlockSpec((None, 8, QB), lambda hp, b, i: (hp, 0, qrow(b, i))),
                  pl.BlockSpec((None, KW, 2 * QB), lambda hp, b, i: (hp, 0, 0))],
        out_specs=[pl.BlockSpec((QB, 128), lambda hp, b, i: (qrow(b, i), hp)), late, late,
                   pl.BlockSpec((None, KW, 2 * QB), lambda hp, b, i: (hp, 0, 0))],
        out_shape=[jax.ShapeDtypeStruct((T, 1024), BF16)] * 3 + [jax.ShapeDtypeStruct((ATT_HEADS // 2, KW, 2 * QB), F32)],
        scratch_shapes=[pltpu.VMEM((3, QB, 128), F32), pltpu.VMEM((3, QB, 128), F32)], compiler_params=_params(3),
    )(qkv, qkv, qkv, qkv, qkv, qkv, qkv, do, o, lse, bias_t)


def _tok_tile(T, want):
    t = min(T, want)
    assert T % t == 0
    return t


def _step(x, tgt, slab, slab_rel, place, mix_g, gn_g, mlp_g, fin_g):
    B, S, D = x.shape
    T = B * S
    h0 = x.reshape(T, D)
    tgt = tgt.reshape(T, D)
    tm = _tok_tile(T, 1024)
    tb = _tok_tile(T, 512)
    tq = _tok_tile(T, 256)
    tt = _tok_tile(T, 8192)
    tf = _tok_tile(T, 2048)
    consts = _ret_consts(S, RET_BLOCK)
    w = {}

    (w["ret_w_in"],) = _all_gather_slabs([slab["ret_w_in"]])
    (proj, hn0), (w["ret_w_out"], w["mlp_w1_0"]) = _mm_nn(
        h0, w["ret_w_in"], "col", tm=tb, out_dtype=BF16, name="ret_in", norm_g=mix_g[0],
        carry=_gather_carry([slab["ret_w_out"], slab["mlp_w1_0"]]))
    y_ret, states = _ret_fwd(proj, gn_g, consts, B=B, S=S)
    h1, (w["mlp_w2_0"],) = _mm_nn(y_ret, w["ret_w_out"], "row", tm=tm, out_dtype=F32, name="ret_out", res=h0,
                                  carry=_gather_carry([slab["mlp_w2_0"]]))
    (a0, hm0), (w["att_w_in"], w["att_w_out"], rel_slabs) = _mm_nn(
        h1, w["mlp_w1_0"], "col", tm=tb, out_dtype=BF16, name="mlp0_up", norm_g=mlp_g[0], act="relu",
        carry=_gather_carry([slab["att_w_in"], slab["att_w_out"], slab_rel]))
    h2, (w["mlp_w1_1"],) = _mm_nn(a0, w["mlp_w2_0"], "row", tm=tb, out_dtype=F32, name="mlp0_down", square=True, res=h1,
                                  carry=_gather_carry([slab["mlp_w1_1"]]))
    rel_bias = jnp.transpose(rel_slabs, (1, 0, 2)).reshape(ATT_HEADS, REL_TABLE)
    bias_t = _att_bias(rel_bias)
    (qkv, hn1), (w["mlp_w2_1"],) = _mm_nn(h2, w["att_w_in"], "col", tm=tb, out_dtype=BF16, name="att_in", norm_g=mix_g[1],
                                          carry=_gather_carry([slab["mlp_w2_1"]]))
    o_att, lse = _att_fwd(qkv, bias_t, B=B, S=S)
    h3 = _mm_nn(o_att, w["att_w_out"], "row", tm=tm, out_dtype=F32, name="att_out", res=h2)
    a1, hm1 = _mm_nn(h3, w["mlp_w1_1"], "col", tm=tb, out_dtype=BF16, name="mlp1_up", norm_g=mlp_g[1], act="relu")
    dh4, d_fin_g, _, loss, dh4_b = _mm_nn(a1, w["mlp_w2_1"], "row", tm=tb, out_dtype=F32, name="mlp1_down", square=True, res=h3,
                                          loss=(tgt, fin_g))

    gw = {}
    gw["mlp_w2_1"] = _mm_tn(a1, dh4_b, "row", tt=tf, tk=1024, tn=D, name="d_mlp1_w2", square=True)
    dz1 = _mm_nt(dh4_b, w["mlp_w2_1"], "row", tm=tb, name="d_mlp1_act", epi="relu2bwd", a2=a1)
    gw["mlp_w1_1"] = _mm_tn(hm1, dz1, "col", tt=tt, tk=D, tn=256, name="d_mlp1_w1")
    dh3, d_mlp_g1, dh3_b = _mm_nt(dz1, w["mlp_w1_1"], "col", tm=tb, name="d_mlp1_in", epi="normbwd",
                                  h=h3, g=mlp_g[1], dres=dh4, bf16_copy=True)
    gw["att_w_out"] = _mm_tn(o_att, dh3_b, "row", tt=tt, tk=256, tn=256, name="d_att_wout")
    do_att = _mm_nt(dh3_b, w["att_w_out"], "row", tm=tb, name="d_att_o", epi="bf16")
    dq, dk, dv, dbias_t = _att_bwd(qkv, do_att, o_att, lse, bias_t, B=B, S=S)
    d_rel = _att_bias_grad(dbias_t)
    dqkv = (dq, dk, dv)
    gw["att_w_in"] = _mm_tn(hn1, dqkv, "col", tt=tt, tk=D, tn=256, name="d_att_win")
    sums, landed = {}, {}

    def swap_halves(names):
        return _sibling_carry([gw[n] for n in names])

    def add_halves(names, carried):
        for t, n in enumerate(names):
            sums[n] = _add_sibling(place, carried[t], carried[len(names) + t], name="chip_sum_" + n)

    def carried_exchange(names):
        return _chips_carry([sums[n][0] for n in names])

    def keep(names, carried):
        landed.update(zip(names, carried[len(names):]))

    layer1 = ["mlp_w1_1", "mlp_w2_1", "att_w_in", "att_w_out"]
    (dh2, d_mix_g1, dh2_b), carried = _mm_nt(dqkv, w["att_w_in"], "col", tm=tb, name="d_att_in", epi="normbwd",
                                             h=h2, g=mix_g[1], dres=dh3, bf16_copy=True, carry=swap_halves(layer1))
    add_halves(layer1, carried)
    gw["mlp_w2_0"] = _mm_tn(a0, dh2_b, "row", tt=tf, tk=1024, tn=D, name="d_mlp0_w2", square=True)
    dz0, carried = _mm_nt(dh2_b, w["mlp_w2_0"], "row", tm=tb, name="d_mlp0_act", epi="relu2bwd", a2=a0,
                          carry=carried_exchange(["mlp_w1_1"]))
    keep(["mlp_w1_1"], carried)
    gw["mlp_w1_0"], carried = _mm_tn(hm0, dz0, "col", tt=tt, tk=D, tn=256, name="d_mlp0_w1", carry=carried_exchange(["mlp_w2_1"]))
    keep(["mlp_w2_1"], carried)
    (dh1, d_mlp_g0, dh1_b), carried = _mm_nt(dz0, w["mlp_w1_0"], "col", tm=tb, name="d_mlp0_in", epi="normbwd",
                                             h=h1, g=mlp_g[0], dres=dh2, bf16_copy=True,
                                             carry=_merge(carried_exchange(["att_w_in", "att_w_out"]),
                                                          swap_halves(["mlp_w1_0", "mlp_w2_0"])))
    keep(["att_w_in", "att_w_out"], carried[:4])
    add_halves(["mlp_w1_0", "mlp_w2_0"], carried[4:])
    gw["ret_w_out"] = _mm_tn(y_ret, dh1_b, "row", tt=tt, tk=512, tn=256, name="d_ret_wout")
    dy_ret, carried = _mm_nt(dh1_b, w["ret_w_out"], "row", tm=tb, name="d_ret_y", epi="bf16", carry=swap_halves(["ret_w_out"]))
    add_halves(["ret_w_out"], carried)
    dproj, d_gn = _ret_bwd(proj, dy_ret, states, gn_g, consts, B=B, S=S)
    gw["ret_w_in"], carried = _mm_tn(hn0, dproj, "col", tt=tt, tk=D, tn=256, name="d_ret_win",
                                     carry=carried_exchange(["mlp_w1_0", "mlp_w2_0", "ret_w_out"]))
    keep(["mlp_w1_0", "mlp_w2_0", "ret_w_out"], carried)
    add_halves(["ret_w_in"], [gw["ret_w_in"]] + list(_exchange_with_sibling([gw["ret_w_in"]], "ret_in")))
    done = [n for n in BIG if n != "ret_w_in"]
    shards = {n: _add_chips(place, sums[n][1], landed[n], name="total_" + n) for n in done}
    (dx, d_mix_g0), carried = _mm_nt(dproj, w["ret_w_in"], "col", tm=tb, name="d_ret_in", epi="normbwd", h=h0, g=mix_g[0], dres=dh1,
                                     carry=_merge(carried_exchange(["ret_w_in"]), _share_carry([shards[n] for n in done])))
    keep(["ret_w_in"], carried[:2])
    shards.update(zip(done, carried[2:]))
    (shards["ret_w_in"],) = _share_with_sibling([_add_chips(place, sums["ret_w_in"][1], landed["ret_w_in"], name="total_ret_w_in")])
    small = [d_mix_g0, d_mix_g1, d_mlp_g0, d_mlp_g1, d_fin_g, d_gn.reshape(2, D), d_rel.reshape(5, D), loss]
    return dx.reshape(B, S, D), shards, small


def _row_tile(r, want=256):
    t = min(r, want)
    assert r % t == 0
    return t


def _into_slab(place, a, layer, dtype, name):
    _, r, c = a.shape
    tr = _row_tile(r)

    def body(place_ref, a_ref, o_ref):
        o_ref[...] = a_ref[...].astype(dtype)

    grid_spec = pltpu.PrefetchScalarGridSpec(
        num_scalar_prefetch=1, grid=(r // tr,), in_specs=[pl.BlockSpec((None, tr, c), lambda i, pr: (layer, i, 0))],
        out_specs=pl.BlockSpec((None, tr, c), lambda i, pr: (pr[0], i, 0)),
    )
    return pl.pallas_call(
        body, name=name, grid_spec=grid_spec, out_shape=jax.ShapeDtypeStruct((N_CHIPS, r, c), dtype), compiler_params=_params(1),
    )(place, a)


def _add_sibling(place, g, recv, name):
    _, r, c = g.shape
    hr = r // 2
    tr = _row_tile(hr)
    nrt = hr // tr

    def body(place_ref, g_ref, r_ref, sb_ref, own_ref):
        v = g_ref[...] + r_ref[...]
        sb_ref[...] = v.astype(BF16)

        @pl.when(pl.program_id(1) == place_ref[0])
        def _():
            own_ref[...] = v

    grid_spec = pltpu.PrefetchScalarGridSpec(
        num_scalar_prefetch=1, grid=(nrt, N_CHIPS),
        in_specs=[pl.BlockSpec((None, tr, c), lambda i, s, pr: (s, pr[1] * nrt + i, 0)),
                  pl.BlockSpec((None, tr, c), lambda i, s, pr: (s, i, 0))],
        out_specs=[pl.BlockSpec((None, tr, c), lambda i, s, pr: (s, i, 0)), pl.BlockSpec((tr, c), lambda i, s, pr: (i, 0))],
    )
    return pl.pallas_call(
        body, name=name, grid_spec=grid_spec,
        out_shape=[jax.ShapeDtypeStruct((N_CHIPS, hr, c), BF16), jax.ShapeDtypeStruct((hr, c), F32)],
        compiler_params=_params(2),
    )(place, g, recv)


def _add_chips(place, own, recv, name):
    hr, c = own.shape
    tr = _row_tile(hr)
    nrt = hr // tr

    def body(place_ref, o_ref, r_ref, t_ref):
        t_ref[...] = ((o_ref[...] + r_ref[0].astype(F32)) + r_ref[1].astype(F32)) + r_ref[2].astype(F32)

    grid_spec = pltpu.PrefetchScalarGridSpec(
        num_scalar_prefetch=1, grid=(nrt,),
        in_specs=[pl.BlockSpec((tr, c), lambda i, pr: (i, 0)), pl.BlockSpec((3, tr, c), lambda i, pr: (0, i, 0))],
        out_specs=pl.BlockSpec((tr, c), lambda i, pr: (pr[1] * nrt + i, 0)),
    )
    return pl.pallas_call(
        body, name=name, grid_spec=grid_spec, out_shape=jax.ShapeDtypeStruct((2 * hr, c), F32), compiler_params=_params(1),
    )(place, own, recv)


def _adamw(w, gs, m, v, name):
    L, r, c = w.shape
    tr = _row_tile(r)
    assert len(gs) == L

    def body(*refs):
        w_ref, m_ref, v_ref = refs[:3]
        g_refs = refs[3:3 + L]
        go_ref, d_ref, nm_ref, nv_ref = refs[3 + L:]
        gg = g_refs[0][...]
        for k in range(1, L):
            gg = jnp.where(pl.program_id(0) == k, g_refs[k][...], gg)
        go_ref[...] = gg
        nm = ADAM_B1 * m_ref[...] + (1.0 - ADAM_B1) * gg
        nv = ADAM_B2 * v_ref[...] + (1.0 - ADAM_B2) * (gg * gg)
        m_hat = nm / (1.0 - ADAM_B1 ** ADAM_STEP)
        v_hat = nv / (1.0 - ADAM_B2 ** ADAM_STEP)
        d_ref[...] = -ADAM_LR * (m_hat / (jnp.sqrt(v_hat) + ADAM_EPS) + ADAM_WD * w_ref[...])
        nm_ref[...] = nm
        nv_ref[...] = nv

    spec = pl.BlockSpec((None, tr, c), lambda l, i: (l, i, 0))
    return pl.pallas_call(
        body, name=name, grid=(L, r // tr), in_specs=[spec] * 3 + [pl.BlockSpec((tr, c), lambda l, i: (i, 0))] * L,
        out_specs=[spec] * 4, out_shape=[jax.ShapeDtypeStruct((L, r, c), F32)] * 4, compiler_params=_params(2),
    )(w, m, v, *gs)


def _place():
    return lax.axis_index("x"), lax.axis_index("y"), lax.axis_index("c")


def _other_chips(x, y):
    return [(1 - x, y), (x, 1 - y), (1 - x, 1 - y)]


def _remote(src, dst, ssem, rsem, dev):
    return pltpu.make_async_remote_copy(src_ref=src, dst_ref=dst, send_sem=ssem, recv_sem=rsem, device_id=dev,
                                        device_id_type=MESH)


def _gather_phases(n):
    def geometry(refs, t):
        x, y, c = _place()
        hr = refs[t].shape[1] // 2
        chips = _other_chips(x, y)
        return x, y, c, 2 * x + y, chips, [2 * qx + qy for qx, qy in chips], pl.ds(c * hr, hr), pl.ds((1 - c) * hr, hr)

    def send(refs, sems):
        s1, r1, _, _ = sems
        for t in range(n):
            x, y, c, p, chips, cidx, mine, theirs = geometry(refs, t)
            for j, (qx, qy) in enumerate(chips):
                _remote(refs[t].at[p, mine], refs[t].at[p, mine], s1.at[t, j], r1.at[t, j], (qx, qy, c)).start()

    def pass_on(refs, sems):
        s1, r1, s2, r2 = sems
        for t in range(n):
            x, y, c, p, chips, cidx, mine, theirs = geometry(refs, t)
            for j, (qx, qy) in enumerate(chips):
                got = refs[t].at[cidx[j], mine]
                _remote(got, got, s1.at[t, j], r1.at[t, j], (qx, qy, c)).wait_recv()
                _remote(got, got, s2.at[t, j], r2.at[t, j], (x, y, 1 - c)).start()

    def finish(refs, sems):
        s1, r1, s2, r2 = sems
        for t in range(n):
            x, y, c, p, chips, cidx, mine, theirs = geometry(refs, t)
            for j, (qx, qy) in enumerate(chips):
                got = refs[t].at[cidx[j], theirs]
                _remote(got, got, s2.at[t, j], r2.at[t, j], (x, y, 1 - c)).wait_recv()
        for t in range(n):
            x, y, c, p, chips, cidx, mine, theirs = geometry(refs, t)
            for j, (qx, qy) in enumerate(chips):
                _remote(refs[t].at[p, mine], refs[t].at[p, mine], s1.at[t, j], r1.at[t, j], (qx, qy, c)).wait_send()
                sent = refs[t].at[cidx[j], mine]
                _remote(sent, sent, s2.at[t, j], r2.at[t, j], (x, y, 1 - c)).wait_send()

    sem = pltpu.SemaphoreType.DMA
    return send, pass_on, finish, [sem((n, 3)), sem((n, 3)), sem((n, 3)), sem((n, 3))]


def _gather_carry(slabs):
    send, pass_on, finish, sems = _gather_phases(len(slabs))
    return _Carry(slabs, sems, [(0, send), (-2, pass_on), (-1, finish)])


def _all_gather_slabs(slabs):
    n = len(slabs)
    send, pass_on, finish, sems = _gather_phases(n)

    def body(*refs):
        outs, scratch = refs[n:2 * n], refs[2 * n:]
        send(outs, scratch)
        pass_on(outs, scratch)
        finish(outs, scratch)

    return pl.pallas_call(
        body, name="gather_weights", in_specs=[HBM_SPEC] * n, out_specs=[HBM_SPEC] * n,
        out_shape=[jax.ShapeDtypeStruct(s.shape, s.dtype) for s in slabs], input_output_aliases={t: t for t in range(n)},
        scratch_shapes=sems, compiler_params=pltpu.CompilerParams(has_side_effects=True),
    )(*slabs)


def _sibling_phases(n):
    def copies(refs, sems):
        ssem, rsem = sems
        x, y, c = _place()
        out = []
        for t in range(n):
            hr = refs[t].shape[1] // 2
            out.append(_remote(refs[t].at[:, pl.ds((1 - c) * hr, hr), :], refs[n + t], ssem.at[t], rsem.at[t], (x, y, 1 - c)))
        return out

    def send(refs, sems):
        for cp in copies(refs, sems):
            cp.start()

    def finish(refs, sems):
        for cp in copies(refs, sems):
            cp.wait()

    sem = pltpu.SemaphoreType.DMA
    return send, finish, [sem((n,)), sem((n,))]


def _halves_landing(grads):
    return [lax.empty((N_CHIPS, g.shape[1] // 2, g.shape[2]), g.dtype) for g in grads]


def _sibling_carry(grads):
    send, finish, sems = _sibling_phases(len(grads))
    return _Carry(list(grads) + _halves_landing(grads), sems, [(0, send), (-1, finish)])


def _exchange_with_sibling(grads, tag):
    n = len(grads)
    send, finish, sems = _sibling_phases(n)

    def body(*refs):
        both, scratch = refs[:2 * n], refs[2 * n:]
        send(both, scratch)
        finish(both, scratch)

    return pl.pallas_call(
        body, name="grads_to_sibling_" + tag, in_specs=[HBM_SPEC] * n, out_specs=[HBM_SPEC] * n,
        out_shape=[jax.ShapeDtypeStruct((N_CHIPS, g.shape[1] // 2, g.shape[2]), g.dtype) for g in grads],
        scratch_shapes=sems, compiler_params=pltpu.CompilerParams(has_side_effects=True),
    )(*grads)


def _chips_phases(n):
    def copies(refs, sems):
        ssem, rsem = sems
        x, y, c = _place()
        return [_remote(refs[t].at[2 * qx + qy], refs[n + t].at[j], ssem.at[t, j], rsem.at[t, j], (qx, qy, c))
                for t in range(n) for j, (qx, qy) in enumerate(_other_chips(x, y))]

    def send(refs, sems):
        for cp in copies(refs, sems):
            cp.start()

    def finish(refs, sems):
        for cp in copies(refs, sems):
            cp.wait()

    sem = pltpu.SemaphoreType.DMA
    return send, finish, [sem((n, 3)), sem((n, 3))]


def _landing(sums):
    return [lax.empty((3,) + s.shape[1:], s.dtype) for s in sums]


def _chips_carry(sums):
    send, finish, sems = _chips_phases(len(sums))
    return _Carry(list(sums) + _landing(sums), sems, [(0, send), (-1, finish)])


def _share_phases(n):
    def halves(refs, t):
        x, y, c = _place()
        hr = refs[t].shape[0] // 2
        return refs[t].at[pl.ds(c * hr, hr)], refs[t].at[pl.ds((1 - c) * hr, hr)], (x, y, 1 - c)

    def send(refs, sems):
        ssem, rsem = sems
        for t in range(n):
            mine, _, sibling = halves(refs, t)
            _remote(mine, mine, ssem.at[t], rsem.at[t], sibling).start()

    def finish(refs, sems):
        ssem, rsem = sems
        for t in range(n):
            mine, theirs, sibling = halves(refs, t)
            _remote(theirs, theirs, ssem.at[t], rsem.at[t], sibling).wait_recv()
            _remote(mine, mine, ssem.at[t], rsem.at[t], sibling).wait_send()

    sem = pltpu.SemaphoreType.DMA
    return send, finish, [sem((n,)), sem((n,))]


def _share_carry(shards):
    send, finish, sems = _share_phases(len(shards))
    return _Carry(shards, sems, [(0, send), (-1, finish)])


def _share_with_sibling(shards):
    n = len(shards)
    send, finish, sems = _share_phases(n)

    def body(*refs):
        outs, scratch = refs[n:2 * n], refs[2 * n:]
        send(outs, scratch)
        finish(outs, scratch)

    return pl.pallas_call(
        body, name="grads_share", in_specs=[HBM_SPEC] * n, out_specs=[HBM_SPEC] * n,
        out_shape=[jax.ShapeDtypeStruct(s.shape, s.dtype) for s in shards], input_output_aliases={t: t for t in range(n)},
        scratch_shapes=sems, compiler_params=pltpu.CompilerParams(has_side_effects=True),
    )(*shards)


def _all_reduce_small(buf):
    R, C = buf.shape

    def body(in_ref, out_ref, gather, ssem, rsem):
        x, y, c = _place()
        me = 4 * x + 2 * y + c
        gather[me] = in_ref[...]
        flips = [(fx, fy, fc) for fx in (0, 1) for fy in (0, 1) for fc in (0, 1) if fx or fy or fc]
        peers = [(x + fx - 2 * x * fx, y + fy - 2 * y * fy, c + fc - 2 * c * fc) for fx, fy, fc in flips]
        copies = [_remote(in_ref, gather.at[me], ssem.at[k], rsem.at[k], peer) for k, peer in enumerate(peers)]
        for cp in copies:
            cp.start()
        for k, (px, py, pc) in enumerate(peers):
            _remote(in_ref, gather.at[4 * px + 2 * py + pc], ssem.at[k], rsem.at[k], (px, py, pc)).wait_recv()
        for cp in copies:
            cp.wait_send()
        acc = gather[0]
        for d in range(1, 8):
            acc = acc + gather[d]
        out_ref[...] = acc

    sem = pltpu.SemaphoreType.DMA
    vmem = pl.BlockSpec(memory_space=pltpu.VMEM)
    return pl.pallas_call(
        body, name="small_grads_sum", in_specs=[vmem], out_specs=vmem, out_shape=jax.ShapeDtypeStruct((R, C), F32),
        scratch_shapes=[pltpu.VMEM((8, R, C), F32), sem((7,)), sem((7,))],
        compiler_params=pltpu.CompilerParams(has_side_effects=True),
    )(buf)


BIG = ["ret_w_in", "ret_w_out", "att_w_in", "att_w_out", "mlp_w1_0", "mlp_w1_1", "mlp_w2_0", "mlp_w2_1"]
LAYER_OF = {"ret_w_in": ("ret_w_in", 0), "ret_w_out": ("ret_w_out", 0), "att_w_in": ("att_w_in", 0), "att_w_out": ("att_w_out", 0),
            "mlp_w1_0": ("mlp_w1", 0), "mlp_w1_1": ("mlp_w1", 1), "mlp_w2_0": ("mlp_w2", 0), "mlp_w2_1": ("mlp_w2", 1)}
ORDER = ["mix_norm_g", "ret_w_in", "ret_gn_g", "ret_w_out", "att_w_in", "att_rel_bias", "att_w_out", "mlp_norm_g", "mlp_w1", "mlp_w2",
         "final_norm_g"]


def kernel(x, mix_norm_g, ret_w_in, ret_gn_g, ret_w_out, att_w_in, att_rel_bias, att_w_out, mlp_norm_g, mlp_w1, mlp_w2, final_norm_g, loss_target, m_mix_norm_g, m_ret_w_in, m_ret_gn_g, m_ret_w_out, m_att_w_in, m_att_rel_bias, m_att_w_out, m_mlp_norm_g, m_mlp_w1, m_mlp_w2, m_final_norm_g, v_mix_norm_g, v_ret_w_in, v_ret_gn_g, v_ret_w_out, v_att_w_in, v_att_rel_bias, v_att_w_out, v_mlp_norm_g, v_mlp_w1, v_mlp_w2, v_final_norm_g):
    xi, yi, ci = _place()
    chip = 2 * xi + yi
    weights = dict(zip(ORDER, (mix_norm_g, ret_w_in, ret_gn_g, ret_w_out, att_w_in, att_rel_bias, att_w_out, mlp_norm_g, mlp_w1,
                               mlp_w2, final_norm_g)))
    first = dict(zip(ORDER, (m_mix_norm_g, m_ret_w_in, m_ret_gn_g, m_ret_w_out, m_att_w_in, m_att_rel_bias, m_att_w_out,
                             m_mlp_norm_g, m_mlp_w1, m_mlp_w2, m_final_norm_g)))
    second = dict(zip(ORDER, (v_mix_norm_g, v_ret_w_in, v_ret_gn_g, v_ret_w_out, v_att_w_in, v_att_rel_bias, v_att_w_out,
                              v_mlp_norm_g, v_mlp_w1, v_mlp_w2, v_final_norm_g)))

    place = jnp.stack([chip, ci]).astype(jnp.int32)
    slab = {n: _into_slab(place, weights[LAYER_OF[n][0]], LAYER_OF[n][1], BF16, name="cast_" + n) for n in BIG}
    slab_rel = _into_slab(place, att_rel_bias, 0, F32, name="slab_rel_bias")

    grad_x, g_big, small = _step(x, loss_target, slab, slab_rel, place, mix_norm_g, ret_gn_g[0], mlp_norm_g, final_norm_g)

    rows, at = jnp.zeros((16, D_MODEL), F32), 0
    for part in small:
        rows = rows + jnp.pad(part, ((at, 16 - at - part.shape[0]), (0, D_MODEL - part.shape[1])))
        at += part.shape[0]
    rows = _all_reduce_small(rows)
    loss = rows[12, 0]
    grads = {"mix_norm_g": [rows[0:2]], "mlp_norm_g": [rows[2:4]], "final_norm_g": [rows[4:5]], "ret_gn_g": [rows[5:7].reshape(1, 2048)],
             "att_rel_bias": [lax.dynamic_slice_in_dim(rows[7:12].reshape(ATT_HEADS, REL_TABLE), chip * (REL_TABLE // N_CHIPS),
                                                       REL_TABLE // N_CHIPS, axis=1)]}
    for n in BIG:
        grads.setdefault(LAYER_OF[n][0], []).append(g_big[n])

    def as3(a):
        return a.reshape((1,) * (3 - a.ndim) + a.shape)

    results = {}
    for n in ORDER:
        outs = _adamw(as3(weights[n]), grads[n], as3(first[n]), as3(second[n]), name="adamw_" + n)
        results[n] = [o.reshape(weights[n].shape) for o in outs]
    return (loss, grad_x) + tuple(results[n][k] for k in range(4) for n in ORDER)
```
